```python
import math
import jax, jax.numpy as jnp
from jax import lax
import numpy as np

D_MODEL = 1024
BATCH = 8
SEQ = 4096
DEPTH = 4

N_A_LAYERS = DEPTH // 2
N_B_LAYERS = DEPTH - N_A_LAYERS
SSM_GROUP = 16
N_GROUPS = D_MODEL // SSM_GROUP
SSM_STATE = 64
DT_MIN = 1e-3
DT_MAX = 1e-1
N_HEADS = 16
HEAD_DIM = D_MODEL // N_HEADS
Q_BLOCK = 128
ATTN_SCALE = HEAD_DIM ** -0.5
D_FF = ((8 * D_MODEL // 3 + 127) // 128) * 128
CONV_W = 3
EPS = 1e-6

kernel_name = "yoco_s5_fox_convffn_trunk"


def rmsnorm(x, g):
    xf = x.astype(jnp.float32)
    y = xf * lax.rsqrt(jnp.mean(xf * xf, axis=-1, keepdims=True) + EPS) * g.astype(jnp.float32)
    return y.astype(x.dtype)


def causal_dwconv(h, w, b):
    L = h.shape[1]
    hp = jnp.pad(h, ((0, 0), (CONV_W - 1, 0), (0, 0)))
    y = b
    for k in range(CONV_W):
        y = y + hp[:, k:k + L, :] * w[k]
    return y


def conv_ffn(h, w_in, conv_w, conv_b, w_out):
    u = causal_dwconv(h @ w_in, conv_w, conv_b)
    gate, up = jnp.split(u, 2, axis=-1)
    return (jax.nn.silu(gate) * up) @ w_out


def _ssm_combine(e_i, e_j):
    ai_re, ai_im, bi_re, bi_im = e_i
    aj_re, aj_im, bj_re, bj_im = e_j
    a_re = aj_re * ai_re - aj_im * ai_im
    a_im = aj_re * ai_im + aj_im * ai_re
    b_re = aj_re * bi_re - aj_im * bi_im + bj_re
    b_im = aj_re * bi_im + aj_im * bi_re + bj_im
    return (a_re, a_im, b_re, b_im)


def s5_mixer(h, lam_re, lam_im, log_dt, b_re, b_im, c_re, c_im, d_skip, w_glu):
    dtype = h.dtype
    bsz, L, _ = h.shape
    f32 = jnp.float32
    u = h.astype(f32).reshape(bsz, L, N_GROUPS, SSM_GROUP)
    lr = lam_re.astype(f32)
    li = lam_im.astype(f32)
    dt = jnp.exp(log_dt.astype(f32))[:, None]
    mag = jnp.exp(lr * dt)
    lb_re = mag * jnp.cos(li * dt)
    lb_im = mag * jnp.sin(li * dt)
    den = lr * lr + li * li
    nr = lb_re - 1.0
    fr = ((nr * lr + lb_im * li) / den)[..., None]
    fi = ((lb_im * lr - nr * li) / den)[..., None]
    br = b_re.astype(f32)
    bi = b_im.astype(f32)
    bb_re = fr * br - fi * bi
    bb_im = fr * bi + fi * br
    bu_re = jnp.einsum('blgh,gph->blgp', u, bb_re)
    bu_im = jnp.einsum('blgh,gph->blgp', u, bb_im)
    a_re = jnp.broadcast_to(lb_re[None, None], (1, L, N_GROUPS, SSM_STATE))
    a_im = jnp.broadcast_to(lb_im[None, None], (1, L, N_GROUPS, SSM_STATE))
    _, _, s_re, s_im = lax.associative_scan(_ssm_combine, (a_re, a_im, bu_re, bu_im), axis=1)
    y = (jnp.einsum('blgp,ghp->blgh', s_re, c_re.astype(f32))
         - jnp.einsum('blgp,ghp->blgh', s_im, c_im.astype(f32)))
    y = y.reshape(bsz, L, D_MODEL) + d_skip.astype(f32) * u.reshape(bsz, L, D_MODEL)
    y = jax.nn.gelu(y)
    z_a, z_g = jnp.split(y @ w_glu.astype(f32), 2, axis=-1)
    return (z_a * jax.nn.sigmoid(z_g)).astype(dtype)


def fox_shared_kv(h_kv, w_kvf, b_f):
    bsz, L, _ = h_kv.shape
    z = h_kv @ w_kvf
    k = z[..., :D_MODEL].reshape(bsz, L, N_HEADS, HEAD_DIM)
    v = z[..., D_MODEL:2 * D_MODEL].reshape(bsz, L, N_HEADS, HEAD_DIM)
    f_logit = z[..., 2 * D_MODEL:].astype(jnp.float32) + b_f.astype(jnp.float32)
    cum = jnp.cumsum(jax.nn.log_sigmoid(f_logit), axis=1)
    return k, v, cum


def fox_attention(h, w_q, w_o, k, v, cum):
    dtype = h.dtype
    bsz, L, _ = h.shape
    nb = L // Q_BLOCK
    f32 = jnp.float32
    q = (h @ w_q).reshape(bsz, nb, Q_BLOCK, N_HEADS, HEAD_DIM).transpose(1, 0, 2, 3, 4)
    cq = cum.reshape(bsz, nb, Q_BLOCK, N_HEADS).transpose(1, 0, 2, 3)
    kf = k.astype(f32)
    vf = v.astype(f32)
    ck = cum.transpose(0, 2, 1)[:, :, None, :]
    kpos = jnp.arange(L, dtype=jnp.int32)
    starts = jnp.arange(nb, dtype=jnp.int32) * Q_BLOCK

    def one_block(args):
        qb, cqb, start = args
        s = jnp.einsum('bqhd,bkhd->bhqk', qb.astype(f32), kf) * ATTN_SCALE
        s = s + cqb.transpose(0, 2, 1)[..., None] - ck
        qpos = start + jnp.arange(Q_BLOCK, dtype=jnp.int32)
        mask = kpos[None, :] <= qpos[:, None]
        s = jnp.where(mask, s, -jnp.inf)
        p = jax.nn.softmax(s, axis=-1)
        return jnp.einsum('bhqk,bkhd->bqhd', p, vf)

    o = lax.map(one_block, (q, cq, starts))
    o = o.transpose(1, 0, 2, 3, 4).reshape(bsz, L, D_MODEL)
    return o.astype(dtype) @ w_o


def _fwd_setup_inputs(seed: int = 0) -> dict:
    key = jax.random.key(seed)
    ks = jax.random.split(key, 24)
    nrm = jax.random.normal
    D, G, P, H, F = D_MODEL, N_GROUPS, SSM_STATE, SSM_GROUP, D_FF
    x = nrm(ks[0], (BATCH, SEQ, D), jnp.float32)
    g_mix = 1.0 + 0.02 * nrm(ks[1], (DEPTH, D), jnp.float32)
    g_ffn = 1.0 + 0.02 * nrm(ks[2], (DEPTH, D), jnp.float32)
    lam_re = -0.5 + 0.01 * nrm(ks[3], (N_A_LAYERS, G, P), jnp.float32)
    lam_im = (math.pi * jnp.arange(P, dtype=jnp.float32))[None, None, :] + 0.01 * nrm(ks[4], (N_A_LAYERS, G, P), jnp.float32)
    log_dt = jax.random.uniform(ks[5], (N_A_LAYERS, G), jnp.float32, math.log(DT_MIN), math.log(DT_MAX))
    ssm_b_re = nrm(ks[6], (N_A_LAYERS, G, P, H), jnp.float32) * (2 * H) ** -0.5
    ssm_b_im = nrm(ks[7], (N_A_LAYERS, G, P, H), jnp.float32) * (2 * H) ** -0.5
    ssm_c_re = nrm(ks[8], (N_A_LAYERS, G, H, P), jnp.float32) * P ** -0.5
    ssm_c_im = nrm(ks[9], (N_A_LAYERS, G, H, P), jnp.float32) * P ** -0.5
    ssm_d = nrm(ks[10], (N_A_LAYERS, D), jnp.float32)
    w_glu = nrm(ks[11], (N_A_LAYERS, D, 2 * D), jnp.float32) * D ** -0.5
    g_kv = 1.0 + 0.02 * nrm(ks[12], (D,), jnp.float32)
    w_kvf = nrm(ks[13], (D, 2 * D + N_HEADS), jnp.float32) * D ** -0.5
    b_f = 2.0 + 0.5 * nrm(ks[14], (N_HEADS,), jnp.float32)
    w_q = nrm(ks[15], (N_B_LAYERS, D, D), jnp.float32) * D ** -0.5
    w_o = nrm(ks[16], (N_B_LAYERS, D, D), jnp.float32) * D ** -0.5
    w_ffn_in = nrm(ks[17], (DEPTH, D, 2 * F), jnp.float32) * D ** -0.5
    ffn_conv_w = nrm(ks[18], (DEPTH, CONV_W, 2 * F), jnp.float32) * CONV_W ** -0.5
    ffn_conv_b = 0.01 * nrm(ks[19], (DEPTH, 2 * F), jnp.float32)
    w_ffn_out = nrm(ks[20], (DEPTH, F, D), jnp.float32) * F ** -0.5
    g_final = 1.0 + 0.02 * nrm(ks[21], (D,), jnp.float32)
    return {"x": x, "g_mix": g_mix, "g_ffn": g_ffn, "lam_re": lam_re, "lam_im": lam_im,
            "log_dt": log_dt, "ssm_b_re": ssm_b_re, "ssm_b_im": ssm_b_im, "ssm_c_re": ssm_c_re,
            "ssm_c_im": ssm_c_im, "ssm_d": ssm_d, "w_glu": w_glu, "g_kv": g_kv, "w_kvf": w_kvf,
            "b_f": b_f, "w_q": w_q, "w_o": w_o, "w_ffn_in": w_ffn_in, "ffn_conv_w": ffn_conv_w,
            "ffn_conv_b": ffn_conv_b, "w_ffn_out": w_ffn_out, "g_final": g_final}


def _fwd_reference(x, g_mix, g_ffn, lam_re, lam_im, log_dt, ssm_b_re, ssm_b_im, ssm_c_re, ssm_c_im,
              ssm_d, w_glu, g_kv, w_kvf, b_f, w_q, w_o, w_ffn_in, ffn_conv_w, ffn_conv_b,
              w_ffn_out, g_final):
    h = x
    k = v = cum = None
    for layer in range(DEPTH):
        if layer < N_A_LAYERS:
            h = h + s5_mixer(rmsnorm(h, g_mix[layer]), lam_re[layer], lam_im[layer], log_dt[layer],
                             ssm_b_re[layer], ssm_b_im[layer], ssm_c_re[layer], ssm_c_im[layer],
                             ssm_d[layer], w_glu[layer])
        else:
            if layer == N_A_LAYERS:
                k, v, cum = fox_shared_kv(rmsnorm(h, g_kv), w_kvf, b_f)
            j = layer - N_A_LAYERS
            h = h + fox_attention(rmsnorm(h, g_mix[layer]), w_q[j], w_o[j], k, v, cum)
        h = h + conv_ffn(rmsnorm(h, g_ffn[layer]), w_ffn_in[layer], ffn_conv_w[layer],
                         ffn_conv_b[layer], w_ffn_out[layer])
    return rmsnorm(h, g_final)


import jax as _jax
import jax.numpy as _jnp

TWIN_FORMAT = 'train_step'
FWD_PARAMS = ['x', 'g_mix', 'g_ffn', 'lam_re', 'lam_im', 'log_dt', 'ssm_b_re', 'ssm_b_im', 'ssm_c_re', 'ssm_c_im', 'ssm_d', 'w_glu', 'g_kv', 'w_kvf', 'b_f', 'w_q', 'w_o', 'w_ffn_in', 'ffn_conv_w', 'ffn_conv_b', 'w_ffn_out', 'g_final']
TWIN_WEIGHTS = ['g_mix', 'g_ffn', 'lam_re', 'lam_im', 'log_dt', 'ssm_b_re', 'ssm_b_im', 'ssm_c_re', 'ssm_c_im', 'ssm_d', 'w_glu', 'g_kv', 'w_kvf', 'b_f', 'w_q', 'w_o', 'w_ffn_in', 'ffn_conv_w', 'ffn_conv_b', 'w_ffn_out', 'g_final']
TWIN_DIFF_INPUT = 'x'
TWIN_INPUTS = ['x', 'g_mix', 'g_ffn', 'lam_re', 'lam_im', 'log_dt', 'ssm_b_re', 'ssm_b_im', 'ssm_c_re', 'ssm_c_im', 'ssm_d', 'w_glu', 'g_kv', 'w_kvf', 'b_f', 'w_q', 'w_o', 'w_ffn_in', 'ffn_conv_w', 'ffn_conv_b', 'w_ffn_out', 'g_final', 'loss_target', 'm_g_mix', 'm_g_ffn', 'm_lam_re', 'm_lam_im', 'm_log_dt', 'm_ssm_b_re', 'm_ssm_b_im', 'm_ssm_c_re', 'm_ssm_c_im', 'm_ssm_d', 'm_w_glu', 'm_g_kv', 'm_w_kvf', 'm_b_f', 'm_w_q', 'm_w_o', 'm_w_ffn_in', 'm_ffn_conv_w', 'm_ffn_conv_b', 'm_w_ffn_out', 'm_g_final', 'v_g_mix', 'v_g_ffn', 'v_lam_re', 'v_lam_im', 'v_log_dt', 'v_ssm_b_re', 'v_ssm_b_im', 'v_ssm_c_re', 'v_ssm_c_im', 'v_ssm_d', 'v_w_glu', 'v_g_kv', 'v_w_kvf', 'v_b_f', 'v_w_q', 'v_w_o', 'v_w_ffn_in', 'v_ffn_conv_w', 'v_ffn_conv_b', 'v_w_ffn_out', 'v_g_final']
TWIN_OUTPUTS = ['loss', 'grad_x', 'grad_g_mix', 'grad_g_ffn', 'grad_lam_re', 'grad_lam_im', 'grad_log_dt', 'grad_ssm_b_re', 'grad_ssm_b_im', 'grad_ssm_c_re', 'grad_ssm_c_im', 'grad_ssm_d', 'grad_w_glu', 'grad_g_kv', 'grad_w_kvf', 'grad_b_f', 'grad_w_q', 'grad_w_o', 'grad_w_ffn_in', 'grad_ffn_conv_w', 'grad_ffn_conv_b', 'grad_w_ffn_out', 'grad_g_final', 'delta_g_mix', 'delta_g_ffn', 'delta_lam_re', 'delta_lam_im', 'delta_log_dt', 'delta_ssm_b_re', 'delta_ssm_b_im', 'delta_ssm_c_re', 'delta_ssm_c_im', 'delta_ssm_d', 'delta_w_glu', 'delta_g_kv', 'delta_w_kvf', 'delta_b_f', 'delta_w_q', 'delta_w_o', 'delta_w_ffn_in', 'delta_ffn_conv_w', 'delta_ffn_conv_b', 'delta_w_ffn_out', 'delta_g_final', 'new_m_g_mix', 'new_m_g_ffn', 'new_m_lam_re', 'new_m_lam_im', 'new_m_log_dt', 'new_m_ssm_b_re', 'new_m_ssm_b_im', 'new_m_ssm_c_re', 'new_m_ssm_c_im', 'new_m_ssm_d', 'new_m_w_glu', 'new_m_g_kv', 'new_m_w_kvf', 'new_m_b_f', 'new_m_w_q', 'new_m_w_o', 'new_m_w_ffn_in', 'new_m_ffn_conv_w', 'new_m_ffn_conv_b', 'new_m_w_ffn_out', 'new_m_g_final', 'new_v_g_mix', 'new_v_g_ffn', 'new_v_lam_re', 'new_v_lam_im', 'new_v_log_dt', 'new_v_ssm_b_re', 'new_v_ssm_b_im', 'new_v_ssm_c_re', 'new_v_ssm_c_im', 'new_v_ssm_d', 'new_v_w_glu', 'new_v_g_kv', 'new_v_w_kvf', 'new_v_b_f', 'new_v_w_q', 'new_v_w_o', 'new_v_w_ffn_in', 'new_v_ffn_conv_w', 'new_v_ffn_conv_b', 'new_v_w_ffn_out', 'new_v_g_final']
TWIN_LEAF_KINDS = {'loss': 'loss', 'grad_x': 'grad_x', 'grad_g_mix': 'grad_w', 'grad_g_ffn': 'grad_w', 'grad_lam_re': 'grad_w', 'grad_lam_im': 'grad_w', 'grad_log_dt': 'grad_w', 'grad_ssm_b_re': 'grad_w', 'grad_ssm_b_im': 'grad_w', 'grad_ssm_c_re': 'grad_w', 'grad_ssm_c_im': 'grad_w', 'grad_ssm_d': 'grad_w', 'grad_w_glu': 'grad_w', 'grad_g_kv': 'grad_w', 'grad_w_kvf': 'grad_w', 'grad_b_f': 'grad_w', 'grad_w_q': 'grad_w', 'grad_w_o': 'grad_w', 'grad_w_ffn_in': 'grad_w', 'grad_ffn_conv_w': 'grad_w', 'grad_ffn_conv_b': 'grad_w', 'grad_w_ffn_out': 'grad_w', 'grad_g_final': 'grad_w', 'delta_g_mix': 'delta_w', 'delta_g_ffn': 'delta_w', 'delta_lam_re': 'delta_w', 'delta_lam_im': 'delta_w', 'delta_log_dt': 'delta_w', 'delta_ssm_b_re': 'delta_w', 'delta_ssm_b_im': 'delta_w', 'delta_ssm_c_re': 'delta_w', 'delta_ssm_c_im': 'delta_w', 'delta_ssm_d': 'delta_w', 'delta_w_glu': 'delta_w', 'delta_g_kv': 'delta_w', 'delta_w_kvf': 'delta_w', 'delta_b_f': 'delta_w', 'delta_w_q': 'delta_w', 'delta_w_o': 'delta_w', 'delta_w_ffn_in': 'delta_w', 'delta_ffn_conv_w': 'delta_w', 'delta_ffn_conv_b': 'delta_w', 'delta_w_ffn_out': 'delta_w', 'delta_g_final': 'delta_w', 'new_m_g_mix': 'new_m', 'new_m_g_ffn': 'new_m', 'new_m_lam_re': 'new_m', 'new_m_lam_im': 'new_m', 'new_m_log_dt': 'new_m', 'new_m_ssm_b_re': 'new_m', 'new_m_ssm_b_im': 'new_m', 'new_m_ssm_c_re': 'new_m', 'new_m_ssm_c_im': 'new_m', 'new_m_ssm_d': 'new_m', 'new_m_w_glu': 'new_m', 'new_m_g_kv': 'new_m', 'new_m_w_kvf': 'new_m', 'new_m_b_f': 'new_m', 'new_m_w_q': 'new_m', 'new_m_w_o': 'new_m', 'new_m_w_ffn_in': 'new_m', 'new_m_ffn_conv_w': 'new_m', 'new_m_ffn_conv_b': 'new_m', 'new_m_w_ffn_out': 'new_m', 'new_m_g_final': 'new_m', 'new_v_g_mix': 'new_v', 'new_v_g_ffn': 'new_v', 'new_v_lam_re': 'new_v', 'new_v_lam_im': 'new_v', 'new_v_log_dt': 'new_v', 'new_v_ssm_b_re': 'new_v', 'new_v_ssm_b_im': 'new_v', 'new_v_ssm_c_re': 'new_v', 'new_v_ssm_c_im': 'new_v', 'new_v_ssm_d': 'new_v', 'new_v_w_glu': 'new_v', 'new_v_g_kv': 'new_v', 'new_v_w_kvf': 'new_v', 'new_v_b_f': 'new_v', 'new_v_w_q': 'new_v', 'new_v_w_o': 'new_v', 'new_v_w_ffn_in': 'new_v', 'new_v_ffn_conv_w': 'new_v', 'new_v_ffn_conv_b': 'new_v', 'new_v_w_ffn_out': 'new_v', 'new_v_g_final': 'new_v'}


def _forward(args):
    return _fwd_reference(*[args[k] for k in FWD_PARAMS])


def _output_shape():
    out = _jax.eval_shape(lambda: _forward(_fwd_setup_inputs(0)))
    return out.shape, out.dtype

N_MICROBATCH = 1
ADAM_LR = 0.001
ADAM_B1 = 0.9
ADAM_B2 = 0.999
ADAM_EPS = 1e-08
ADAM_WD = 0.01
ADAM_STEP = 10
PER_EXAMPLE_BATCH_AXIS = {'x': 0, 'loss_target': 0}
SHARED_INPUTS = []
_WEIGHT_DTYPES = {'g_mix': _jnp.float32, 'g_ffn': _jnp.float32, 'lam_re': _jnp.float32, 'lam_im': _jnp.float32, 'log_dt': _jnp.float32, 'ssm_b_re': _jnp.float32, 'ssm_b_im': _jnp.float32, 'ssm_c_re': _jnp.float32, 'ssm_c_im': _jnp.float32, 'ssm_d': _jnp.float32, 'w_glu': _jnp.float32, 'g_kv': _jnp.float32, 'w_kvf': _jnp.float32, 'b_f': _jnp.float32, 'w_q': _jnp.float32, 'w_o': _jnp.float32, 'w_ffn_in': _jnp.float32, 'ffn_conv_w': _jnp.float32, 'ffn_conv_b': _jnp.float32, 'w_ffn_out': _jnp.float32, 'g_final': _jnp.float32}
MOMENT_SCALE = {'g_mix': 7.180730e-02, 'g_ffn': 1.272525e-01, 'lam_re': 6.536631e-03, 'lam_im': 7.373571e-03, 'log_dt': 3.935036e+00, 'ssm_b_re': 4.171640e-03, 'ssm_b_im': 4.155116e-03, 'ssm_c_re': 5.715213e-03, 'ssm_c_im': 5.841018e-03, 'ssm_d': 8.716103e-02, 'w_glu': 6.212592e-02, 'g_kv': 1.004073e-01, 'w_kvf': 7.048385e-02, 'b_f': 3.704243e-01, 'w_q': 3.874650e-02, 'w_o': 5.734433e-02, 'w_ffn_in': 5.441515e-02, 'ffn_conv_w': 5.426270e-02, 'ffn_conv_b': 5.649718e-02, 'w_ffn_out': 8.885701e-02, 'g_final': 3.202894e+01}


def _to_microbatches(a, axis):
    t = _jnp.moveaxis(a, axis, 0)
    t = t.reshape((N_MICROBATCH, t.shape[0] // N_MICROBATCH) + t.shape[1:])
    return _jnp.moveaxis(t, 1, axis + 1)


def setup_inputs(seed: int = 0) -> dict:
    inp = _fwd_setup_inputs(seed)
    key = _jax.random.fold_in(_jax.random.key(seed), 7919)
    shape, _ = _output_shape()
    out = dict(inp)
    out["loss_target"] = _jax.random.normal(_jax.random.fold_in(key, 0), shape, _jnp.float32)
    for i, name in enumerate(TWIN_WEIGHTS):
        w = inp[name].astype(_jnp.float32)
        if MOMENT_SCALE is None:
            s = _jnp.sqrt(_jnp.mean(_jnp.square(w)) + 1e-30)
        else:
            s = MOMENT_SCALE[name]
        km, kv = _jax.random.split(_jax.random.fold_in(key, i + 1))
        out[name] = w
        out["m_" + name] = s * _jax.random.normal(km, w.shape, _jnp.float32)
        out["v_" + name] = (s * s) * _jax.random.uniform(kv, w.shape, _jnp.float32, 0.5, 1.5)
    if N_MICROBATCH > 1:
        for name, axis in PER_EXAMPLE_BATCH_AXIS.items():
            out[name] = _to_microbatches(out[name], axis)
    return {'x': out['x'], 'g_mix': out['g_mix'], 'g_ffn': out['g_ffn'], 'lam_re': out['lam_re'], 'lam_im': out['lam_im'], 'log_dt': out['log_dt'], 'ssm_b_re': out['ssm_b_re'], 'ssm_b_im': out['ssm_b_im'], 'ssm_c_re': out['ssm_c_re'], 'ssm_c_im': out['ssm_c_im'], 'ssm_d': out['ssm_d'], 'w_glu': out['w_glu'], 'g_kv': out['g_kv'], 'w_kvf': out['w_kvf'], 'b_f': out['b_f'], 'w_q': out['w_q'], 'w_o': out['w_o'], 'w_ffn_in': out['w_ffn_in'], 'ffn_conv_w': out['ffn_conv_w'], 'ffn_conv_b': out['ffn_conv_b'], 'w_ffn_out': out['w_ffn_out'], 'g_final': out['g_final'], 'loss_target': out['loss_target'], 'm_g_mix': out['m_g_mix'], 'm_g_ffn': out['m_g_ffn'], 'm_lam_re': out['m_lam_re'], 'm_lam_im': out['m_lam_im'], 'm_log_dt': out['m_log_dt'], 'm_ssm_b_re': out['m_ssm_b_re'], 'm_ssm_b_im': out['m_ssm_b_im'], 'm_ssm_c_re': out['m_ssm_c_re'], 'm_ssm_c_im': out['m_ssm_c_im'], 'm_ssm_d': out['m_ssm_d'], 'm_w_glu': out['m_w_glu'], 'm_g_kv': out['m_g_kv'], 'm_w_kvf': out['m_w_kvf'], 'm_b_f': out['m_b_f'], 'm_w_q': out['m_w_q'], 'm_w_o': out['m_w_o'], 'm_w_ffn_in': out['m_w_ffn_in'], 'm_ffn_conv_w': out['m_ffn_conv_w'], 'm_ffn_conv_b': out['m_ffn_conv_b'], 'm_w_ffn_out': out['m_w_ffn_out'], 'm_g_final': out['m_g_final'], 'v_g_mix': out['v_g_mix'], 'v_g_ffn': out['v_g_ffn'], 'v_lam_re': out['v_lam_re'], 'v_lam_im': out['v_lam_im'], 'v_log_dt': out['v_log_dt'], 'v_ssm_b_re': out['v_ssm_b_re'], 'v_ssm_b_im': out['v_ssm_b_im'], 'v_ssm_c_re': out['v_ssm_c_re'], 'v_ssm_c_im': out['v_ssm_c_im'], 'v_ssm_d': out['v_ssm_d'], 'v_w_glu': out['v_w_glu'], 'v_g_kv': out['v_g_kv'], 'v_w_kvf': out['v_w_kvf'], 'v_b_f': out['v_b_f'], 'v_w_q': out['v_w_q'], 'v_w_o': out['v_w_o'], 'v_w_ffn_in': out['v_w_ffn_in'], 'v_ffn_conv_w': out['v_ffn_conv_w'], 'v_ffn_conv_b': out['v_ffn_conv_b'], 'v_w_ffn_out': out['v_w_ffn_out'], 'v_g_final': out['v_g_final']}


def _loss(weights, diff, rest, loss_target):
    with _jax.named_scope("forward"):
        args = {**rest, TWIN_DIFF_INPUT: diff, **{k: w.astype(_WEIGHT_DTYPES[k]) for k, w in weights.items()}}
        y = _forward(args)
    with _jax.named_scope("loss_head"):
        err = _jnp.square(y.astype(_jnp.float32) - loss_target)
        return 0.5 * _jnp.sum(_jnp.mean(err, axis=-1)) if err.ndim else 0.5 * err


def _adamw(w, g, m, v):
    m = ADAM_B1 * m + (1.0 - ADAM_B1) * g
    v = ADAM_B2 * v + (1.0 - ADAM_B2) * _jnp.square(g)
    m_hat = m / (1.0 - ADAM_B1 ** ADAM_STEP)
    v_hat = v / (1.0 - ADAM_B2 ** ADAM_STEP)
    delta = -ADAM_LR * (m_hat / (_jnp.sqrt(v_hat) + ADAM_EPS) + ADAM_WD * w)
    return delta, m, v


def reference(x, g_mix, g_ffn, lam_re, lam_im, log_dt, ssm_b_re, ssm_b_im, ssm_c_re, ssm_c_im, ssm_d, w_glu, g_kv, w_kvf, b_f, w_q, w_o, w_ffn_in, ffn_conv_w, ffn_conv_b, w_ffn_out, g_final, loss_target, m_g_mix, m_g_ffn, m_lam_re, m_lam_im, m_log_dt, m_ssm_b_re, m_ssm_b_im, m_ssm_c_re, m_ssm_c_im, m_ssm_d, m_w_glu, m_g_kv, m_w_kvf, m_b_f, m_w_q, m_w_o, m_w_ffn_in, m_ffn_conv_w, m_ffn_conv_b, m_w_ffn_out, m_g_final, v_g_mix, v_g_ffn, v_lam_re, v_lam_im, v_log_dt, v_ssm_b_re, v_ssm_b_im, v_ssm_c_re, v_ssm_c_im, v_ssm_d, v_w_glu, v_g_kv, v_w_kvf, v_b_f, v_w_q, v_w_o, v_w_ffn_in, v_ffn_conv_w, v_ffn_conv_b, v_w_ffn_out, v_g_final):
    given = dict(x=x, g_mix=g_mix, g_ffn=g_ffn, lam_re=lam_re, lam_im=lam_im, log_dt=log_dt, ssm_b_re=ssm_b_re, ssm_b_im=ssm_b_im, ssm_c_re=ssm_c_re, ssm_c_im=ssm_c_im, ssm_d=ssm_d, w_glu=w_glu, g_kv=g_kv, w_kvf=w_kvf, b_f=b_f, w_q=w_q, w_o=w_o, w_ffn_in=w_ffn_in, ffn_conv_w=ffn_conv_w, ffn_conv_b=ffn_conv_b, w_ffn_out=w_ffn_out, g_final=g_final, loss_target=loss_target, m_g_mix=m_g_mix, m_g_ffn=m_g_ffn, m_lam_re=m_lam_re, m_lam_im=m_lam_im, m_log_dt=m_log_dt, m_ssm_b_re=m_ssm_b_re, m_ssm_b_im=m_ssm_b_im, m_ssm_c_re=m_ssm_c_re, m_ssm_c_im=m_ssm_c_im, m_ssm_d=m_ssm_d, m_w_glu=m_w_glu, m_g_kv=m_g_kv, m_w_kvf=m_w_kvf, m_b_f=m_b_f, m_w_q=m_w_q, m_w_o=m_w_o, m_w_ffn_in=m_w_ffn_in, m_ffn_conv_w=m_ffn_conv_w, m_ffn_conv_b=m_ffn_conv_b, m_w_ffn_out=m_w_ffn_out, m_g_final=m_g_final, v_g_mix=v_g_mix, v_g_ffn=v_g_ffn, v_lam_re=v_lam_re, v_lam_im=v_lam_im, v_log_dt=v_log_dt, v_ssm_b_re=v_ssm_b_re, v_ssm_b_im=v_ssm_b_im, v_ssm_c_re=v_ssm_c_re, v_ssm_c_im=v_ssm_c_im, v_ssm_d=v_ssm_d, v_w_glu=v_w_glu, v_g_kv=v_g_kv, v_w_kvf=v_w_kvf, v_b_f=v_b_f, v_w_q=v_w_q, v_w_o=v_w_o, v_w_ffn_in=v_w_ffn_in, v_ffn_conv_w=v_ffn_conv_w, v_ffn_conv_b=v_ffn_conv_b, v_w_ffn_out=v_w_ffn_out, v_g_final=v_g_final)
    weights = {n: given[n] for n in TWIN_WEIGHTS}
    shared = {n: given[n] for n in SHARED_INPUTS}
    per_example = {n: given[n] for n in ['x']}
    grad_fn = _jax.value_and_grad(_loss, argnums=(0, 1))

    def one_microbatch(ex, loss_target):
        ex = dict(ex)
        diff = ex.pop(TWIN_DIFF_INPUT)
        return grad_fn(weights, diff, {**shared, **ex}, loss_target)

    if N_MICROBATCH == 1:
        loss, (grad_w, grad_x) = one_microbatch(per_example, given["loss_target"])
    else:
        def body(carry, xs):
            loss_sum, grad_sum = carry
            l_k, (gw_k, gx_k) = one_microbatch(xs[0], xs[1])
            with _jax.named_scope("update"):
                return (loss_sum + l_k, _jax.tree.map(_jnp.add, grad_sum, gw_k)), gx_k

        init = (_jnp.zeros((), _jnp.float32), _jax.tree.map(_jnp.zeros_like, weights))
        (loss, grad_w), grad_x = _jax.lax.scan(body, init, (per_example, given["loss_target"]))
    with _jax.named_scope("update"):
        delta_w, new_m, new_v = {}, {}, {}
        for n in TWIN_WEIGHTS:
            delta_w[n], new_m[n], new_v[n] = _adamw(weights[n], grad_w[n], given["m_" + n], given["v_" + n])
    return (loss, grad_x, *[grad_w[n] for n in TWIN_WEIGHTS], *[delta_w[n] for n in TWIN_WEIGHTS],
            *[new_m[n] for n in TWIN_WEIGHTS], *[new_v[n] for n in TWIN_WEIGHTS])
```

```python
import functools

import jax
import jax.numpy as jnp
from jax import lax
from jax.experimental import pallas as pl
from jax.experimental.pallas import tpu as pltpu

F32 = jnp.float32
BF16 = jnp.bfloat16

RMS_EPS = 1e-6
ADAM_LR = 0.001
ADAM_B1 = 0.9
ADAM_B2 = 0.999
ADAM_EPS = 1e-08
ADAM_WD = 0.01
ADAM_STEP = 10
DT_CONV_TAPS = 3

LANES = 128
SUBLANES = 8
HEAD_DIM = 64
S5_BLOCK_GROUPS = 16
VMEM_LIMIT_BYTES = 48 << 20
N_CHIPS = 4
N_CORES = 2
MESH = pl.DeviceIdType.MESH


def _cp(n_grid):
    return pltpu.CompilerParams(dimension_semantics=("arbitrary",) * n_grid, vmem_limit_bytes=VMEM_LIMIT_BYTES)


def _tile(n, pref, mult=SUBLANES):
    if n <= pref:
        return n
    t = (pref // mult) * mult
    while t >= mult:
        if n % t == 0:
            return t
        t -= mult
    return n


def _dot(a, b, ca, cb):
    return lax.dot_general(a, b, (((ca,), (cb,)), ((), ())), preferred_element_type=F32)


def _mm_cols(x, w4, layer, *, wc, out_dtype=F32, scale=None, name):
    m, k = x.shape
    slots, _, k0, k1 = w4.shape
    nb = k1 if wc == 0 else k0
    assert (k0 if wc == 0 else k1) == k
    tm = _tile(m, 512)

    def body(x_ref, w_ref, o_ref):
        acc = _dot(x_ref[...].astype(BF16), w_ref[0, 0], 1, wc)
        if scale is not None:
            acc = acc * scale
        o_ref[...] = acc.astype(out_dtype)

    return pl.pallas_call(
        body, grid=(slots, m // tm),
        in_specs=[pl.BlockSpec((tm, k), lambda s, i: (i, 0)),
                  pl.BlockSpec((1, 1, k0, k1), lambda s, i: (s, layer, 0, 0))],
        out_specs=pl.BlockSpec((tm, nb), lambda s, i: (i, s)),
        out_shape=jax.ShapeDtypeStruct((m, slots * nb), out_dtype),
        compiler_params=_cp(2), name=name)(x, w4)


def _mm_acc(x, w4, layer, *, wc, name):
    m, _ = x.shape
    slots, _, k0, k1 = w4.shape
    kb = k0 if wc == 0 else k1
    nout = k1 if wc == 0 else k0
    assert x.shape[1] == slots * kb
    tm = _tile(m, 512)

    def body(x_ref, w_ref, o_ref):
        @pl.when(pl.program_id(1) == 0)
        def _():
            o_ref[...] = jnp.zeros_like(o_ref)
        o_ref[...] += _dot(x_ref[...].astype(BF16), w_ref[0, 0], 1, wc)

    return pl.pallas_call(
        body, grid=(m // tm, slots),
        in_specs=[pl.BlockSpec((tm, kb), lambda i, s: (i, s)),
                  pl.BlockSpec((1, 1, k0, k1), lambda i, s: (s, layer, 0, 0))],
        out_specs=pl.BlockSpec((tm, nout), lambda i, s: (i, 0)),
        out_shape=jax.ShapeDtypeStruct((m, nout), F32),
        compiler_params=_cp(2), name=name)(x, w4)


def _mm_tn(x, dy, slots, *, scale=None, name):
    m, k = x.shape
    n = dy.shape[1] // slots
    ta = _tile(k, 512, LANES)
    tm = _tile(m, 512)
    n_m = m // tm

    def body(x_ref, dy_ref, o_ref):
        @pl.when(pl.program_id(2) == 0)
        def _():
            o_ref[...] = jnp.zeros_like(o_ref)
        o_ref[0] += _dot(x_ref[...].astype(BF16), dy_ref[...].astype(BF16), 0, 0)
        if scale is not None:
            @pl.when(pl.program_id(2) == n_m - 1)
            def _():
                o_ref[...] = o_ref[...] * scale

    return pl.pallas_call(
        body, grid=(slots, k // ta, n_m),
        in_specs=[pl.BlockSpec((tm, ta), lambda s, a, i: (i, a)),
                  pl.BlockSpec((tm, n), lambda s, a, i: (i, s))],
        out_specs=pl.BlockSpec((1, ta, n), lambda s, a, i: (s, a, 0)),
        out_shape=jax.ShapeDtypeStruct((slots, k, n), F32),
        compiler_params=_cp(3), name=name)(x, dy)


def _rowwise(fn, rows, consts, outs, accs=(), *, tl=256, name):
    n_rows = rows[0].shape[0]
    tl = _tile(n_rows, tl)
    n_in = len(rows) + len(consts)
    n_out = len(outs)

    def body(*refs):
        res = fn(*[r[...] for r in refs[:n_in]])
        res = res if isinstance(res, (tuple, list)) else (res,)
        o_refs = refs[n_in:n_in + n_out]
        a_refs = refs[n_in + n_out:]
        for o, val in zip(o_refs, res[:n_out]):
            o[...] = val.astype(o.dtype)
        if a_refs:
            @pl.when(pl.program_id(0) == 0)
            def _():
                for a in a_refs:
                    a[...] = jnp.zeros_like(a)
            for a, val in zip(a_refs, res[n_out:]):
                a[...] += val

    in_specs = ([pl.BlockSpec((tl, r.shape[1]), lambda i: (i, 0)) for r in rows]
                + [pl.BlockSpec(c.shape, lambda i: (0, 0)) for c in consts])
    out_specs = ([pl.BlockSpec((tl, w), lambda i: (i, 0)) for w, _ in outs]
                 + [pl.BlockSpec(s, lambda i: (0, 0)) for s in accs])
    out_shape = ([jax.ShapeDtypeStruct((n_rows, w), dt) for w, dt in outs]
                 + [jax.ShapeDtypeStruct(s, F32) for s in accs])
    return pl.pallas_call(body, grid=(n_rows // tl,), in_specs=in_specs, out_specs=out_specs,
                          out_shape=out_shape, compiler_params=_cp(1), name=name)(*rows, *consts)


def _rms(x, g):
    return x * lax.rsqrt(jnp.mean(x * x, axis=-1, keepdims=True) + RMS_EPS) * g


def _sigmoid(x):
    return 1.0 / (1.0 + jnp.exp(-x))


def _glu(zz):
    d = zz.shape[1] // 2
    return zz[:, :d] * _sigmoid(zz[:, d:])


def _gelu(y):
    return jax.nn.gelu(y)


def _row2(v):
    return v.reshape(1, -1)


def _node_bwd(d_in, h, branches, *, name):
    width = h.shape[1]
    flat = [dy for _, dys in branches for dy in dys]
    counts = [len(dys) for _, dys in branches]
    gains = [_row2(g) for g, _ in branches]

    def fn(d, hh, *rest):
        dys, gs = rest[:len(flat)], rest[len(flat):]
        tot, dgs, pos = d, [], 0
        for g, cnt in zip(gs, counts):
            dy = dys[pos].astype(F32)
            for extra in dys[pos + 1:pos + cnt]:
                dy = dy + extra.astype(F32)
            pos += cnt
            _, vjp = jax.vjp(_rms, hh, g)
            dx, dg = vjp(dy)
            tot = tot + dx
            dgs.append(dg)
        return (tot, *dgs)

    res = _rowwise(fn, [d_in, h, *flat], gains, [(width, F32)], [(1, width)] * len(branches), name=name)
    return res[0], [r[0] for r in res[1:]]


def _s5_prep_fn(lr, li, ldt, br, bi, cr, ci, *, gq, h, p):
    dt = jnp.exp(ldt)
    mag = jnp.exp(lr * dt)
    lb_re = mag * jnp.cos(li * dt)
    lb_im = mag * jnp.sin(li * dt)
    den = lr * lr + li * li
    nr = lb_re - 1.0
    fr = (nr * lr + lb_im * li) / den
    fi = (lb_im * lr - nr * li) / den
    bb_re = fr * br - fi * bi
    bb_im = fr * bi + fi * br
    shape = (gq * h, gq * p)
    r = lax.broadcasted_iota(jnp.int32, shape, 0)
    c = lax.broadcasted_iota(jnp.int32, shape, 1)
    mask = jnp.where(jnp.right_shift(r, h.bit_length() - 1) == jnp.right_shift(c, p.bit_length() - 1), 1.0, 0.0)

    def expand(t):
        return jnp.concatenate([t] * gq, axis=0) * mask

    return lb_re, lb_im, expand(bb_re), expand(bb_im), expand(cr), expand(ci)


def _s5_prep(lr, li, ldt, br, bi, cr, ci, p, *, name):
    n = lr.shape[1]
    h = br.shape[0]
    gq = S5_BLOCK_GROUPS
    nq, cq = gq * p, gq * h
    nblk = n // nq
    fn = functools.partial(_s5_prep_fn, gq=gq, h=h, p=p)

    def body(lr_r, li_r, ldt_r, br_r, bi_r, cr_r, ci_r, lbr_o, lbi_o, wbr_o, wbi_o, wcr_o, wci_o):
        lb_re, lb_im, wbr, wbi, wcr, wci = fn(lr_r[...], li_r[...], ldt_r[...], br_r[...], bi_r[...],
                                              cr_r[...], ci_r[...])
        lbr_o[...] = lb_re
        lbi_o[...] = lb_im
        wbr_o[0] = wbr.astype(BF16)
        wbi_o[0] = wbi.astype(BF16)
        wcr_o[0] = wcr.astype(BF16)
        wci_o[0] = wci.astype(BF16)

    vec = pl.BlockSpec((1, nq), lambda q: (0, q))
    tab = pl.BlockSpec((h, nq), lambda q: (0, q))
    wsp = pl.BlockSpec((1, cq, nq), lambda q: (q, 0, 0))
    wsh = jax.ShapeDtypeStruct((nblk, cq, nq), BF16)
    vsh = jax.ShapeDtypeStruct((1, n), F32)
    return pl.pallas_call(body, grid=(nblk,), in_specs=[vec, vec, vec, tab, tab, tab, tab],
                          out_specs=[vec, vec, wsp, wsp, wsp, wsp], out_shape=[vsh, vsh, wsh, wsh, wsh, wsh],
                          compiler_params=_cp(1), name=name)(lr, li, ldt, br, bi, cr, ci)


def _s5_prep_bwd(lr, li, ldt, br, bi, cr, ci, p, dlbr, dlbi, dwbr, dwbi, dwcr, dwci, *, name):
    n = lr.shape[1]
    h = br.shape[0]
    gq = S5_BLOCK_GROUPS
    nq, cq = gq * p, gq * h
    nblk = n // nq
    fn = functools.partial(_s5_prep_fn, gq=gq, h=h, p=p)

    def body(lr_r, li_r, ldt_r, br_r, bi_r, cr_r, ci_r, dlbr_r, dlbi_r, dwbr_r, dwbi_r, dwcr_r, dwci_r,
             *outs):
        _, vjp = jax.vjp(fn, lr_r[...], li_r[...], ldt_r[...], br_r[...], bi_r[...], cr_r[...], ci_r[...])
        grads = vjp((dlbr_r[0], dlbi_r[0], dwbr_r[0], dwbi_r[0], dwcr_r[0], dwci_r[0]))
        for o, g in zip(outs, grads):
            o[...] = g

    vec = pl.BlockSpec((1, nq), lambda q: (0, q))
    tab = pl.BlockSpec((h, nq), lambda q: (0, q))
    vec3 = pl.BlockSpec((1, 1, nq), lambda q: (q, 0, 0))
    wsp = pl.BlockSpec((1, cq, nq), lambda q: (q, 0, 0))
    vsh = jax.ShapeDtypeStruct((1, n), F32)
    tsh = jax.ShapeDtypeStruct((h, n), F32)
    return pl.pallas_call(body, grid=(nblk,),
                          in_specs=[vec, vec, vec, tab, tab, tab, tab, vec3, vec3, wsp, wsp, wsp, wsp],
                          out_specs=[vec, vec, vec, tab, tab, tab, tab],
                          out_shape=[vsh, vsh, vsh, tsh, tsh, tsh, tsh],
                          compiler_params=_cp(1), name=name)(lr, li, ldt, br, bi, cr, ci,
                                                             dlbr, dlbi, dwbr, dwbi, dwcr, dwci)


def _scan_rows(s_re, s_im, a_re, a_im, c_re, c_im, *, reverse):
    t_rows, n = s_re.shape
    nb = t_rows // SUBLANES
    row = lax.broadcasted_iota(jnp.int32, (SUBLANES, n), 0)

    def cmul(x, y):
        return x[0] * y[0] - x[1] * y[1], x[0] * y[1] + x[1] * y[0]

    a1 = (jnp.broadcast_to(a_re, (SUBLANES, n)), jnp.broadcast_to(a_im, (SUBLANES, n)))
    a2 = cmul(a1, a1)
    a4 = cmul(a2, a2)
    pk = (a_re, a_im)
    tab_re = jnp.zeros((SUBLANES, n), F32)
    tab_im = jnp.zeros((SUBLANES, n), F32)
    for i in range(SUBLANES):
        at = (SUBLANES - 1 - i) if reverse else i
        tab_re = jnp.where(row == at, pk[0], tab_re)
        tab_im = jnp.where(row == at, pk[1], tab_im)
        pk = cmul(pk, (a_re, a_im))

    def step(b, carry):
        cr, ci = carry
        blk = (nb - 1 - b) if reverse else b
        off = pl.multiple_of(blk * SUBLANES, SUBLANES)
        x_re = s_re[pl.ds(off, SUBLANES), :]
        x_im = s_im[pl.ds(off, SUBLANES), :]
        for d, (pr, pi) in ((1, a1), (2, a2), (4, a4)):
            if reverse:
                keep = row < SUBLANES - d
                sh = SUBLANES - d
            else:
                keep = row >= d
                sh = d
            sh_re = jnp.where(keep, pltpu.roll(x_re, sh, 0), 0.0)
            sh_im = jnp.where(keep, pltpu.roll(x_im, sh, 0), 0.0)
            x_re, x_im = x_re + pr * sh_re - pi * sh_im, x_im + pr * sh_im + pi * sh_re
        x_re, x_im = x_re + tab_re * cr - tab_im * ci, x_im + tab_re * ci + tab_im * cr
        s_re[pl.ds(off, SUBLANES), :] = x_re
        s_im[pl.ds(off, SUBLANES), :] = x_im
        edge = 0 if reverse else SUBLANES - 1
        return x_re[edge:edge + 1, :], x_im[edge:edge + 1, :]

    return lax.fori_loop(0, nb, step, (c_re, c_im))


def _s5_fwd(u, prep, dskip, *, name):
    lb_re, lb_im, wbr, wbi, wcr, wci = prep
    n_rows, _ = u.shape
    nblk, cq, nq = wbr.shape
    tt = _tile(n_rows, 512)
    nch = n_rows // tt

    def body(u_ref, wbr_r, wbi_r, wcr_r, wci_r, lbr_r, lbi_r, d_ref, y_ref, sbr_o, sbi_o, s_re, s_im, c_re, c_im):
        @pl.when(pl.program_id(1) == 0)
        def _():
            c_re[...] = jnp.zeros_like(c_re)
            c_im[...] = jnp.zeros_like(c_im)
        uf = u_ref[...]
        ub = uf.astype(BF16)
        s_re[...] = _dot(ub, wbr_r[0], 1, 0)
        s_im[...] = _dot(ub, wbi_r[0], 1, 0)
        sbr_o[0] = c_re[...]
        sbi_o[0] = c_im[...]
        cr, ci = _scan_rows(s_re, s_im, lbr_r[...], lbi_r[...], c_re[...], c_im[...], reverse=False)
        c_re[...] = cr
        c_im[...] = ci
        y = _dot(s_re[...].astype(BF16), wcr_r[0], 1, 1) - _dot(s_im[...].astype(BF16), wci_r[0], 1, 1)
        y_ref[...] = y + d_ref[...] * uf

    wsp = pl.BlockSpec((1, cq, nq), lambda q, i: (q, 0, 0))
    vec = pl.BlockSpec((1, nq), lambda q, i: (0, q))
    act = pl.BlockSpec((tt, cq), lambda q, i: (i, q))
    sb = pl.BlockSpec((1, 1, nq), lambda q, i: (i, 0, q))
    sbsh = jax.ShapeDtypeStruct((nch, 1, nblk * nq), F32)
    return pl.pallas_call(
        body, grid=(nblk, nch),
        in_specs=[act, wsp, wsp, wsp, wsp, vec, vec, pl.BlockSpec((1, cq), lambda q, i: (0, q))],
        out_specs=[act, sb, sb],
        out_shape=[jax.ShapeDtypeStruct(u.shape, F32), sbsh, sbsh],
        scratch_shapes=[pltpu.VMEM((tt, nq), F32), pltpu.VMEM((tt, nq), F32),
                        pltpu.VMEM((1, nq), F32), pltpu.VMEM((1, nq), F32)],
        compiler_params=_cp(2), name=name)(u, wbr, wbi, wcr, wci, lb_re, lb_im, dskip)


def _s5_bwd(u, dy, sb_re, sb_im, prep, dskip, *, name):
    lb_re, lb_im, wbr, wbi, wcr, wci = prep
    n_rows, _ = u.shape
    nblk, cq, nq = wbr.shape
    tt = _tile(n_rows, 512)
    nch = n_rows // tt

    def body(u_ref, dy_ref, sbr_r, sbi_r, wbr_r, wbi_r, wcr_r, wci_r, lbr_r, lbi_r, d_ref,
             du_ref, dwbr, dwbi, dwcr, dwci, dlbr, dlbi, dd_ref, s_re, s_im, g_re, g_im, lc_re, lc_im):
        @pl.when(pl.program_id(1) == 0)
        def _():
            for ref in (lc_re, lc_im, dwbr, dwbi, dwcr, dwci, dlbr, dlbi, dd_ref):
                ref[...] = jnp.zeros_like(ref)
        uf = u_ref[...]
        ub = uf.astype(BF16)
        dyf = dy_ref[...]
        dyb = dyf.astype(BF16)
        s_re[...] = _dot(ub, wbr_r[0], 1, 0)
        s_im[...] = _dot(ub, wbi_r[0], 1, 0)
        _scan_rows(s_re, s_im, lbr_r[...], lbi_r[...], sbr_r[0], sbi_r[0], reverse=False)
        sr16 = s_re[...].astype(BF16)
        si16 = s_im[...].astype(BF16)
        dwcr[0] += _dot(dyb, sr16, 0, 0)
        dwci[0] -= _dot(dyb, si16, 0, 0)
        g_re[...] = _dot(dyb, wcr_r[0], 1, 0)
        g_im[...] = -_dot(dyb, wci_r[0], 1, 0)
        lcr, lci = _scan_rows(g_re, g_im, lbr_r[...], -lbi_r[...], lc_re[...], lc_im[...], reverse=True)
        lc_re[...] = lcr
        lc_im[...] = lci
        lam_r = g_re[...]
        lam_i = g_im[...]
        first = lax.broadcasted_iota(jnp.int32, (tt, nq), 0) == 0
        prev_r = jnp.where(first, sbr_r[0], pltpu.roll(s_re[...], 1, 0))
        prev_i = jnp.where(first, sbi_r[0], pltpu.roll(s_im[...], 1, 0))
        dlbr[0] += jnp.sum(lam_r * prev_r + lam_i * prev_i, axis=0, keepdims=True)
        dlbi[0] += jnp.sum(lam_i * prev_r - lam_r * prev_i, axis=0, keepdims=True)
        lr16 = lam_r.astype(BF16)
        li16 = lam_i.astype(BF16)
        du_ref[...] = _dot(lr16, wbr_r[0], 1, 1) + _dot(li16, wbi_r[0], 1, 1) + d_ref[...] * dyf
        dwbr[0] += _dot(ub, lr16, 0, 0)
        dwbi[0] += _dot(ub, li16, 0, 0)
        dd_ref[0] += jnp.sum(dyf * uf, axis=0, keepdims=True)

    last = nch - 1
    wsp = pl.BlockSpec((1, cq, nq), lambda q, i: (q, 0, 0))
    vec = pl.BlockSpec((1, nq), lambda q, i: (0, q))
    act = pl.BlockSpec((tt, cq), lambda q, i: (last - i, q))
    sb = pl.BlockSpec((1, 1, nq), lambda q, i: (last - i, 0, q))
    vec3 = pl.BlockSpec((1, 1, nq), lambda q, i: (q, 0, 0))
    dsp = pl.BlockSpec((1, 1, cq), lambda q, i: (q, 0, 0))
    wsh = jax.ShapeDtypeStruct((nblk, cq, nq), F32)
    v3sh = jax.ShapeDtypeStruct((nblk, 1, nq), F32)
    big = pltpu.VMEM((tt, nq), F32)
    return pl.pallas_call(
        body, grid=(nblk, nch),
        in_specs=[act, act, sb, sb, wsp, wsp, wsp, wsp, vec, vec, pl.BlockSpec((1, cq), lambda q, i: (0, q))],
        out_specs=[act, wsp, wsp, wsp, wsp, vec3, vec3, dsp],
        out_shape=[jax.ShapeDtypeStruct(u.shape, F32), wsh, wsh, wsh, wsh, v3sh, v3sh,
                   jax.ShapeDtypeStruct((nblk, 1, cq), F32)],
        scratch_shapes=[big, big, big, big, pltpu.VMEM((1, nq), F32), pltpu.VMEM((1, nq), F32)],
        compiler_params=_cp(2), name=name)(u, dy, sb_re, sb_im, wbr, wbi, wcr, wci, lb_re, lb_im, dskip)


def _conv_taps(cur, prev, w, b):
    rid = lax.broadcasted_iota(jnp.int32, cur.shape, 0)
    x1 = jnp.where(rid == 0, prev[7:8, :], pltpu.roll(cur, 1, 0))
    x2 = jnp.where(rid == 0, prev[6:7, :], jnp.where(rid == 1, prev[7:8, :], pltpu.roll(cur, 2, 0)))
    return b + x2 * w[0:1, :] + x1 * w[1:2, :] + cur * w[2:3, :], x1, x2


def _conv_fwd(uu, cw, cb, *, name):
    n_rows, f2 = uu.shape
    f = f2 // 2
    tc = _tile(f, 1408, LANES)
    tl = _tile(n_rows, 256)
    nfb = f // tc

    def body(g_ref, u_ref, wg_ref, wu_ref, bg_ref, bu_ref, o_ref, pg, pu):
        @pl.when(pl.program_id(1) == 0)
        def _():
            pg[...] = jnp.zeros_like(pg)
            pu[...] = jnp.zeros_like(pu)
        gcur = g_ref[...]
        ucur = u_ref[...]
        cg, _, _ = _conv_taps(gcur, pg[...], wg_ref[...], bg_ref[...])
        cu, _, _ = _conv_taps(ucur, pu[...], wu_ref[...], bu_ref[...])
        o_ref[...] = (cg * _sigmoid(cg) * cu).astype(o_ref.dtype)
        pg[...] = gcur[tl - SUBLANES:, :]
        pu[...] = ucur[tl - SUBLANES:, :]

    return pl.pallas_call(
        body, grid=(nfb, n_rows // tl),
        in_specs=[pl.BlockSpec((tl, tc), lambda j, i: (i, j)), pl.BlockSpec((tl, tc), lambda j, i: (i, j + nfb)),
                  pl.BlockSpec((DT_CONV_TAPS, tc), lambda j, i: (0, j)),
                  pl.BlockSpec((DT_CONV_TAPS, tc), lambda j, i: (0, j + nfb)),
                  pl.BlockSpec((1, tc), lambda j, i: (0, j)), pl.BlockSpec((1, tc), lambda j, i: (0, j + nfb))],
        out_specs=pl.BlockSpec((tl, tc), lambda j, i: (i, j)),
        out_shape=jax.ShapeDtypeStruct((n_rows, f), BF16),
        scratch_shapes=[pltpu.VMEM((SUBLANES, tc), F32), pltpu.VMEM((SUBLANES, tc), F32)],
        compiler_params=_cp(2), name=name)(uu, uu, cw, cw, cb, cb)


def _conv_bwd(uu, dact, cw, cb, *, name):
    n_rows, f2 = uu.shape
    f = f2 // 2
    tc = _tile(f, 1408, LANES)
    tl = _tile(n_rows, 256)
    nfb = f // tc
    nrb = n_rows // tl
    halo_per_tile = tl // SUBLANES

    def body(own_ref, ownh_ref, par_ref, parh_ref, da_ref, wo_ref, wp_ref, bo_ref, bp_ref,
             duu_ref, dw_ref, db_ref, nxt):
        jj = pl.program_id(0)
        i = pl.program_id(1)
        rb = nrb - 1 - i

        @pl.when(i == 0)
        def _():
            nxt[...] = jnp.zeros_like(nxt)
            dw_ref[...] = jnp.zeros_like(dw_ref)
            db_ref[...] = jnp.zeros_like(db_ref)
        has_prev = jnp.where(rb > 0, 1.0, 0.0)
        own = own_ref[...]
        own_c, x1, x2 = _conv_taps(own, ownh_ref[...] * has_prev, wo_ref[...], bo_ref[...])
        par_c, _, _ = _conv_taps(par_ref[...], parh_ref[...] * has_prev, wp_ref[...], bp_ref[...])
        is_gate = jj < nfb
        cg = jnp.where(is_gate, own_c, par_c)
        cu = jnp.where(is_gate, par_c, own_c)
        sg = _sigmoid(cg)
        silu = cg * sg
        dsilu = sg * (1.0 + cg * (1.0 - sg))
        d = da_ref[...] * jnp.where(is_gate, cu * dsilu, silu)
        rid = lax.broadcasted_iota(jnp.int32, d.shape, 0)
        nx = nxt[...]
        d1 = jnp.where(rid == tl - 1, nx[0:1, :], pltpu.roll(d, tl - 1, 0))
        d2 = jnp.where(rid == tl - 2, nx[0:1, :], jnp.where(rid == tl - 1, nx[1:2, :], pltpu.roll(d, tl - 2, 0)))
        w = wo_ref[...]
        duu_ref[...] = (w[2:3, :] * d + w[1:2, :] * d1 + w[0:1, :] * d2).astype(duu_ref.dtype)
        nxt[...] = d[0:SUBLANES, :]
        dw_ref[...] += jnp.concatenate([jnp.sum(d * x2, axis=0, keepdims=True),
                                        jnp.sum(d * x1, axis=0, keepdims=True),
                                        jnp.sum(d * own, axis=0, keepdims=True)], axis=0)
        db_ref[...] += jnp.sum(d, axis=0, keepdims=True)

    def partner(jj):
        return (jj + nfb) % (2 * nfb)

    def halo(jj, i):
        return jnp.maximum((nrb - 1 - i) * halo_per_tile - 1, 0)

    return pl.pallas_call(
        body, grid=(2 * nfb, nrb),
        in_specs=[pl.BlockSpec((tl, tc), lambda jj, i: (nrb - 1 - i, jj)),
                  pl.BlockSpec((SUBLANES, tc), lambda jj, i: (halo(jj, i), jj)),
                  pl.BlockSpec((tl, tc), lambda jj, i: (nrb - 1 - i, partner(jj))),
                  pl.BlockSpec((SUBLANES, tc), lambda jj, i: (halo(jj, i), partner(jj))),
                  pl.BlockSpec((tl, tc), lambda jj, i: (nrb - 1 - i, jj % nfb)),
                  pl.BlockSpec((DT_CONV_TAPS, tc), lambda jj, i: (0, jj)),
                  pl.BlockSpec((DT_CONV_TAPS, tc), lambda jj, i: (0, partner(jj))),
                  pl.BlockSpec((1, tc), lambda jj, i: (0, jj)),
                  pl.BlockSpec((1, tc), lambda jj, i: (0, partner(jj)))],
        out_specs=[pl.BlockSpec((tl, tc), lambda jj, i: (nrb - 1 - i, jj)),
                   pl.BlockSpec((DT_CONV_TAPS, tc), lambda jj, i: (0, jj)),
                   pl.BlockSpec((1, tc), lambda jj, i: (0, jj))],
        out_shape=[jax.ShapeDtypeStruct((n_rows, f2), BF16), jax.ShapeDtypeStruct((DT_CONV_TAPS, f2), F32),
                   jax.ShapeDtypeStruct((1, f2), F32)],
        scratch_shapes=[pltpu.VMEM((SUBLANES, tc), F32)],
        compiler_params=_cp(2), name=name)(uu, uu, uu, uu, dact, cw, cw, cb, cb)


def _log_sigmoid(x):
    t = jnp.exp(-jnp.abs(x))
    log1p_t = jnp.where(t < 1e-3, t * (1.0 - t * (0.5 - t * (1.0 / 3.0))), jnp.log(1.0 + t))
    return jnp.minimum(x, 0.0) - log1p_t


def _dlog_sigmoid(x):
    t = jnp.exp(-jnp.abs(x))
    return jnp.where(x >= 0, t, 1.0) / (1.0 + t)


def _tri_dot(tri, x):
    return jnp.dot(tri, x, precision=lax.Precision.HIGHEST, preferred_element_type=F32)


def _cum_fwd(fl, bf, *, name):
    n_rows, width = fl.shape
    tc = _tile(n_rows, 256)

    def body(fl_ref, bf_ref, o_ref, carry):
        @pl.when(pl.program_id(0) == 0)
        def _():
            carry[...] = jnp.zeros_like(carry)
        x = _log_sigmoid(fl_ref[...] + bf_ref[...])
        r = lax.broadcasted_iota(jnp.int32, (tc, tc), 0)
        c = lax.broadcasted_iota(jnp.int32, (tc, tc), 1)
        y = _tri_dot(jnp.where(r >= c, 1.0, 0.0), x) + carry[...]
        o_ref[...] = y
        carry[...] = y[tc - 1:tc, :]

    return pl.pallas_call(
        body, grid=(n_rows // tc,),
        in_specs=[pl.BlockSpec((tc, width), lambda i: (i, 0)), pl.BlockSpec((1, width), lambda i: (0, 0))],
        out_specs=pl.BlockSpec((tc, width), lambda i: (i, 0)),
        out_shape=jax.ShapeDtypeStruct(fl.shape, F32),
        scratch_shapes=[pltpu.VMEM((1, width), F32)], compiler_params=_cp(1), name=name)(fl, bf)


def _cum_bwd(dcum, fl, bf, *, name):
    n_rows, width = fl.shape
    tc = _tile(n_rows, 256)
    last = n_rows // tc - 1

    def body(dc_ref, fl_ref, bf_ref, dfl_ref, dbf_ref, carry):
        @pl.when(pl.program_id(0) == 0)
        def _():
            carry[...] = jnp.zeros_like(carry)
            dbf_ref[...] = jnp.zeros_like(dbf_ref)
        r = lax.broadcasted_iota(jnp.int32, (tc, tc), 0)
        c = lax.broadcasted_iota(jnp.int32, (tc, tc), 1)
        dls = _tri_dot(jnp.where(r <= c, 1.0, 0.0), dc_ref[...]) + carry[...]
        carry[...] = dls[0:1, :]
        dfl = dls * _dlog_sigmoid(fl_ref[...] + bf_ref[...])
        dfl_ref[...] = dfl.astype(dfl_ref.dtype)
        dbf_ref[...] += jnp.sum(dfl, axis=0, keepdims=True)

    return pl.pallas_call(
        body, grid=(n_rows // tc,),
        in_specs=[pl.BlockSpec((tc, width), lambda i: (last - i, 0)),
                  pl.BlockSpec((tc, width), lambda i: (last - i, 0)),
                  pl.BlockSpec((1, width), lambda i: (0, 0))],
        out_specs=[pl.BlockSpec((tc, width), lambda i: (last - i, 0)), pl.BlockSpec((1, width), lambda i: (0, 0))],
        out_shape=[jax.ShapeDtypeStruct(fl.shape, BF16), jax.ShapeDtypeStruct((1, width), F32)],
        scratch_shapes=[pltpu.VMEM((1, width), F32)], compiler_params=_cp(1), name=name)(dcum, fl, bf)


def _head_masks():
    lane = lax.broadcasted_iota(jnp.int32, (1, LANES), 1)
    return (lane < HEAD_DIM, lane >= HEAD_DIM)


def _flash_fwd(q, kv, cq3, ck3, *, tq, name):
    n_rows, d = q.shape
    nhp = d // LANES
    tk = tq
    nq = n_rows // tq
    nk = n_rows // tk

    def body(q_ref, k_ref, v_ref, cq_ref, ck_ref, o_ref, lse_ref, m0, m1, l0, l1, acc):
        i = pl.program_id(1)
        j = pl.program_id(2)
        ms, ls = (m0, m1), (l0, l1)

        @pl.when(j == 0)
        def _():
            for h in range(2):
                ms[h][...] = jnp.full_like(ms[h], -jnp.inf)
                ls[h][...] = jnp.zeros_like(ls[h])
            acc[...] = jnp.zeros_like(acc)

        @pl.when(j * tk <= i * tq + tq - 1)
        def _():
            qv, kk, vv = q_ref[...], k_ref[...], v_ref[...]
            rowg = i * tq + lax.broadcasted_iota(jnp.int32, (tq, tk), 0)
            colg = j * tk + lax.broadcasted_iota(jnp.int32, (tq, tk), 1)
            causal = colg <= rowg
            a = acc[...]
            for h, msk in enumerate(_head_masks()):
                s = _dot(jnp.where(msk, qv, jnp.zeros_like(qv)), kk, 1, 1)
                s = s + (cq_ref[0, :, h:h + 1] - ck_ref[0, h:h + 1, :])
                s = jnp.where(causal, s, -jnp.inf)
                m_old = ms[h][...]
                m_new = jnp.maximum(m_old, jnp.max(s, axis=1, keepdims=True))
                alpha = jnp.exp(m_old - m_new)
                p = jnp.exp(s - m_new)
                ls[h][...] = alpha * ls[h][...] + jnp.sum(p, axis=1, keepdims=True)
                ms[h][...] = m_new
                pv = _dot(p.astype(BF16), jnp.where(msk, vv, jnp.zeros_like(vv)), 1, 0)
                a = a * jnp.where(msk, alpha, 1.0) + pv
            acc[...] = a

        @pl.when(j == nk - 1)
        def _():
            m_a, m_b = _head_masks()
            o_ref[...] = acc[...] * jnp.where(m_a, 1.0 / l0[...], 1.0 / l1[...])
            two = lax.broadcasted_iota(jnp.int32, (tq, 2), 1)
            lse_ref[0] = jnp.where(two == 0, m0[...] + jnp.log(l0[...]), m1[...] + jnp.log(l1[...]))

    def kv_blk(i, j):
        return jnp.minimum(j, (i * tq + tq - 1) // tk)

    col = pltpu.VMEM((tq, 1), F32)
    return pl.pallas_call(
        body, grid=(nhp, nq, nk),
        in_specs=[pl.BlockSpec((tq, LANES), lambda hp, i, j: (i, hp)),
                  pl.BlockSpec((tk, LANES), lambda hp, i, j: (kv_blk(i, j), hp)),
                  pl.BlockSpec((tk, LANES), lambda hp, i, j: (kv_blk(i, j), nhp + hp)),
                  pl.BlockSpec((1, tq, 2), lambda hp, i, j: (hp, i, 0)),
                  pl.BlockSpec((1, 2, tk), lambda hp, i, j: (hp, 0, kv_blk(i, j)))],
        out_specs=[pl.BlockSpec((tq, LANES), lambda hp, i, j: (i, hp)),
                   pl.BlockSpec((1, tq, 2), lambda hp, i, j: (hp, i, 0))],
        out_shape=[jax.ShapeDtypeStruct((n_rows, d), F32), jax.ShapeDtypeStruct((nhp, n_rows, 2), F32)],
        scratch_shapes=[col, col, col, col, pltpu.VMEM((tq, LANES), F32)],
        compiler_params=_cp(3), name=name)(q, kv, kv, cq3, ck3)


def _flash_bwd(q, kv, o, lse, do, cq3, ck3, *, tq, name):
    n_rows, d = q.shape
    nhp = d // LANES
    tk = tq
    nq = n_rows // tq
    nk = n_rows // tk

    def body(q_ref, k_ref, v_ref, o_ref, lse_ref, do_ref, cq_ref, ck_ref,
             dq_ref, dk_ref, dv_ref, dck_ref, dcq_ref):
        j = pl.program_id(1)
        i = pl.program_id(2)

        @pl.when(jnp.logical_and(j == 0, i == 0))
        def _():
            dq_ref[...] = jnp.zeros_like(dq_ref)
            dcq_ref[...] = jnp.zeros_like(dcq_ref)

        @pl.when(i == 0)
        def _():
            dk_ref[...] = jnp.zeros_like(dk_ref)
            dv_ref[...] = jnp.zeros_like(dv_ref)
            dck_ref[...] = jnp.zeros_like(dck_ref)

        @pl.when(i * tq + tq - 1 >= j * tk)
        def _():
            qv, kk, vv = q_ref[...], k_ref[...], v_ref[...]
            dov = do_ref[...]
            dob = dov.astype(BF16)
            prod = dob.astype(F32) * o_ref[...]
            rowg = i * tq + lax.broadcasted_iota(jnp.int32, (tq, tk), 0)
            colg = j * tk + lax.broadcasted_iota(jnp.int32, (tq, tk), 1)
            causal = colg <= rowg
            dq_acc = jnp.zeros((tq, LANES), F32)
            dk_acc = jnp.zeros((tk, LANES), F32)
            dv_acc = jnp.zeros((tk, LANES), F32)
            dck_rows = []
            dcq_cols = []
            for h, msk in enumerate(_head_masks()):
                qh = jnp.where(msk, qv, jnp.zeros_like(qv))
                kh = jnp.where(msk, kk, jnp.zeros_like(kk))
                doh = jnp.where(msk, dob, jnp.zeros_like(dob))
                s = _dot(qh, kk, 1, 1) + (cq_ref[0, :, h:h + 1] - ck_ref[0, h:h + 1, :])
                s = jnp.where(causal, s, -jnp.inf)
                p = jnp.exp(s - lse_ref[0, :, h:h + 1])
                dv_acc = dv_acc + _dot(p.astype(BF16), doh, 0, 0)
                dp = _dot(doh, vv, 1, 1)
                delta = jnp.sum(jnp.where(msk, prod, 0.0), axis=1, keepdims=True)
                ds = p * (dp - delta)
                dck_rows.append(-jnp.sum(ds, axis=0, keepdims=True))
                dcq_cols.append(jnp.sum(ds, axis=1, keepdims=True))
                dsb = ds.astype(BF16)
                dq_acc = dq_acc + _dot(dsb, kh, 1, 0)
                dk_acc = dk_acc + _dot(dsb, qh, 0, 0)
            off = pl.multiple_of(i * tq, tq)
            dq_ref[pl.ds(off, tq), :] += dq_acc
            dk_ref[...] += dk_acc
            dv_ref[...] += dv_acc
            dck_ref[0] += jnp.concatenate(dck_rows, axis=0)
            two = lax.broadcasted_iota(jnp.int32, (tq, 2), 1)
            dcq_ref[0, pl.ds(off, tq), :] += jnp.where(two == 0, dcq_cols[0], dcq_cols[1])

    def q_blk(j, i):
        return jnp.maximum(i, (j * tk) // tq)

    return pl.pallas_call(
        body, grid=(nhp, nk, nq),
        in_specs=[pl.BlockSpec((tq, LANES), lambda hp, j, i: (q_blk(j, i), hp)),
                  pl.BlockSpec((tk, LANES), lambda hp, j, i: (j, hp)),
                  pl.BlockSpec((tk, LANES), lambda hp, j, i: (j, nhp + hp)),
                  pl.BlockSpec((tq, LANES), lambda hp, j, i: (q_blk(j, i), hp)),
                  pl.BlockSpec((1, tq, 2), lambda hp, j, i: (hp, q_blk(j, i), 0)),
                  pl.BlockSpec((tq, LANES), lambda hp, j, i: (q_blk(j, i), hp)),
                  pl.BlockSpec((1, tq, 2), lambda hp, j, i: (hp, q_blk(j, i), 0)),
                  pl.BlockSpec((1, 2, tk), lambda hp, j, i: (hp, 0, j))],
        out_specs=[pl.BlockSpec((n_rows, LANES), lambda hp, j, i: (0, hp)),
                   pl.BlockSpec((tk, LANES), lambda hp, j, i: (j, hp)),
                   pl.BlockSpec((tk, LANES), lambda hp, j, i: (j, hp)),
                   pl.BlockSpec((1, 2, tk), lambda hp, j, i: (hp, 0, j)),
                   pl.BlockSpec((1, n_rows, 2), lambda hp, j, i: (hp, 0, 0))],
        out_shape=[jax.ShapeDtypeStruct((n_rows, d), F32), jax.ShapeDtypeStruct((n_rows, d), F32),
                   jax.ShapeDtypeStruct((n_rows, d), F32), jax.ShapeDtypeStruct((nhp, 2, n_rows), F32),
                   jax.ShapeDtypeStruct((nhp, n_rows, 2), F32)],
        compiler_params=_cp(3), name=name)(q, kv, kv, o, lse, do, cq3, ck3)


def _s5_tables(w, layer):
    g, p = w["lam_re"].shape[1:]
    h = w["ssm_b_re"].shape[3]
    n = g * p
    lr = w["lam_re"][layer].reshape(1, n)
    li = w["lam_im"][layer].reshape(1, n)
    ldt = jnp.broadcast_to(w["log_dt"][layer][:, None], (g, p)).reshape(1, n)
    br = w["ssm_b_re"][layer].transpose(2, 0, 1).reshape(h, n)
    bi = w["ssm_b_im"][layer].transpose(2, 0, 1).reshape(h, n)
    cr = w["ssm_c_re"][layer].transpose(1, 0, 2).reshape(h, n)
    ci = w["ssm_c_im"][layer].transpose(1, 0, 2).reshape(h, n)
    return (lr, li, ldt, br, bi, cr, ci), (g, p, h)


def _local_step(x, tgt, w, *, attn_tile=512):
    n_rows, d = x.shape
    n_layers = w["g_mix"].shape[0]
    n_s5 = w["lam_re"].shape[0]
    nh = w["b_f"].shape[0]
    nhp = nh // 2
    assert d == nh * HEAD_DIM
    tq = _tile(n_rows, attn_tile)
    in_slots = w["w_in"].shape[0]
    glu_slots = w["w_glu"].shape[0]
    g = {}
    saved = [dict() for _ in range(n_layers)]

    h = x
    nxt = _rowwise(lambda a, gg: _rms(a, gg), [x], [_row2(w["g_mix"][0])], [(d, F32)], name="rms_first")[0]
    kvb = fl = cum = cq3 = ck3 = hnkv = None
    bf_pad = jnp.zeros((1, LANES), F32).at[0, :nh].set(w["b_f"])
    for l in range(n_layers):
        sv = saved[l]
        sv["h"] = h
        g_ffn = _row2(w["g_ffn"][l])
        if l < n_s5:
            tabs, (_, p, _) = _s5_tables(w, l)
            prep = _s5_prep(*tabs, p, name=f"s5_prep{l}")
            dskip = w["ssm_d"][l].reshape(1, d)
            y, sb_re, sb_im = _s5_fwd(nxt, prep, dskip, name=f"s5_fwd{l}")
            z = _rowwise(_gelu, [y], [], [(d, BF16)], name=f"gelu{l}")[0]
            zz = _mm_cols(z, w["w_glu"], l, wc=0, name=f"glu_mm{l}")
            h1, hn2 = _rowwise(lambda hh, zq, gg: ((lambda t: (t, _rms(t, gg)))(hh + _glu(zq))),
                               [h, zz], [g_ffn], [(d, F32), (d, BF16)], name=f"mix_out{l}")
            sv.update(u=nxt, prep=prep, tabs=tabs, p=p, dskip=dskip, sb_re=sb_re, sb_im=sb_im, y=y, z=z, zz=zz)
        else:
            j = l - n_s5
            qs = _mm_cols(nxt, w["w_q"], j, wc=0, out_dtype=BF16, scale=HEAD_DIM ** -0.5, name=f"q_mm{j}")
            o, lse = _flash_fwd(qs, kvb, cq3, ck3, tq=tq, name=f"flash_fwd{j}")
            a = _mm_cols(o, w["w_o"], j, wc=0, name=f"o_mm{j}")
            h1, hn2 = _rowwise(lambda hh, aa, gg: ((lambda t: (t, _rms(t, gg)))(hh + aa)),
                               [h, a], [g_ffn], [(d, F32), (d, BF16)], name=f"mix_out{l}")
            sv.update(hn=nxt, qs=qs, o=o, lse=lse)
        uu = _mm_cols(hn2, w["w_in"], l, wc=0, name=f"ffn_in{l}")
        cw, cb = w["conv_w"][l], _row2(w["conv_b"][l])
        act = _conv_fwd(uu, cw, cb, name=f"conv_fwd{l}")
        f = _mm_cols(act, w["w_out"], l, wc=0, name=f"ffn_out{l}")
        sv.update(h1=h1, hn2=hn2, uu=uu, act=act, cw=cw, cb=cb)
        if l == n_layers - 1:
            def loss_fn(hh, ff, tt, gg):
                yv, vjp = jax.vjp(_rms, hh + ff, gg)
                err = yv - tt
                part = 0.5 * jnp.sum(jnp.mean(err * err, axis=-1, keepdims=True), axis=0, keepdims=True)
                dh, dg = vjp(err * (1.0 / d))
                return dh, jnp.broadcast_to(part, (1, LANES)), dg
            dcur, loss_row, dgf = _rowwise(loss_fn, [h1, f, tgt], [_row2(w["g_final"])], [(d, F32)],
                                           [(1, LANES), (1, d)], name="loss")
            loss = loss_row[0, 0]
            g["g_final"] = dgf[0]
        elif l + 1 < n_s5:
            h, nxt = _rowwise(lambda hh, ff, gg: ((lambda t: (t, _rms(t, gg)))(hh + ff)), [h1, f],
                              [_row2(w["g_mix"][l + 1])], [(d, F32), (d, F32)], name=f"ffn_res{l}")
        elif l + 1 == n_s5:
            h, nxt, hnkv = _rowwise(
                lambda hh, ff, g1, g2: ((lambda t: (t, _rms(t, g1), _rms(t, g2)))(hh + ff)), [h1, f],
                [_row2(w["g_mix"][l + 1]), _row2(w["g_kv"])], [(d, F32), (d, BF16), (d, BF16)], name=f"ffn_res{l}")
            kvb = _mm_cols(hnkv, w["w_kv"], 0, wc=0, out_dtype=BF16, name="kv_mm")
            fl = _mm_cols(hnkv, w["w_f"], 0, wc=0, name="f_mm")
            cum = _cum_fwd(fl, bf_pad, name="cum_fwd")
            cq3 = cum[:, :nh].reshape(n_rows, nhp, 2).transpose(1, 0, 2)
            ck3 = cum[:, :nh].T.reshape(nhp, 2, n_rows)
        else:
            h, nxt = _rowwise(lambda hh, ff, gg: ((lambda t: (t, _rms(t, gg)))(hh + ff)), [h1, f],
                              [_row2(w["g_mix"][l + 1])], [(d, F32), (d, BF16)], name=f"ffn_res{l}")

    per_layer = {k: [None] * n_layers for k in ("g_mix", "g_ffn", "w_in", "w_out", "conv_w", "conv_b")}
    per_s5 = {k: [None] * n_s5 for k in ("lam_re", "lam_im", "log_dt", "ssm_b_re", "ssm_b_im", "ssm_c_re",
                                         "ssm_c_im", "ssm_d", "w_glu")}
    per_fox = {k: [None] * (n_layers - n_s5) for k in ("w_q", "w_o")}
    dk_parts, dv_parts, dck_parts = [], [], []
    for l in reversed(range(n_layers)):
        sv = saved[l]
        dact = _mm_cols(dcur, w["w_out"], l, wc=1, name=f"ffn_out_dx{l}")
        per_layer["w_out"][l] = _mm_tn(sv["act"], dcur, 1, name=f"ffn_out_dw{l}")[0]
        duu, dcw, dcb = _conv_bwd(sv["uu"], dact, sv["cw"], sv["cb"], name=f"conv_bwd{l}")
        per_layer["conv_w"][l] = dcw
        per_layer["conv_b"][l] = dcb[0]
        dhn2 = _mm_acc(duu, w["w_in"], l, wc=1, name=f"ffn_in_dx{l}")
        per_layer["w_in"][l] = _mm_tn(sv["hn2"], duu, in_slots, name=f"ffn_in_dw{l}")
        d1, (dg,) = _node_bwd(dcur, sv["h1"], [(w["g_ffn"][l], [dhn2])], name=f"ffn_norm_bwd{l}")
        per_layer["g_ffn"][l] = dg
        if l < n_s5:
            def glu_bwd(zq, dd):
                _, vjp = jax.vjp(_glu, zq)
                return vjp(dd)[0]
            dzz = _rowwise(glu_bwd, [sv["zz"], d1], [], [(2 * d, BF16)], name=f"glu_bwd{l}")[0]
            dz = _mm_acc(dzz, w["w_glu"], l, wc=1, name=f"glu_dx{l}")
            per_s5["w_glu"][l] = _mm_tn(sv["z"], dzz, glu_slots, name=f"glu_dw{l}")

            def gelu_bwd(yy, dd):
                _, vjp = jax.vjp(_gelu, yy)
                return vjp(dd)[0]
            dy = _rowwise(gelu_bwd, [sv["y"], dz], [], [(d, F32)], name=f"gelu_bwd{l}")[0]
            du, dwbr, dwbi, dwcr, dwci, dlbr, dlbi, dd = _s5_bwd(sv["u"], dy, sv["sb_re"], sv["sb_im"], sv["prep"],
                                                                 sv["dskip"], name=f"s5_bwd{l}")
            dlr, dli, dldt, dbr, dbi, dcr, dci = _s5_prep_bwd(*sv["tabs"], sv["p"], dlbr, dlbi, dwbr, dwbi, dwcr,
                                                              dwci, name=f"s5_prep_bwd{l}")
            gg, p = w["lam_re"].shape[1:]
            hh = w["ssm_b_re"].shape[3]
            per_s5["lam_re"][l] = dlr.reshape(gg, p)
            per_s5["lam_im"][l] = dli.reshape(gg, p)
            per_s5["log_dt"][l] = dldt.reshape(gg, p).sum(axis=1)
            per_s5["ssm_b_re"][l] = dbr.reshape(hh, gg, p).transpose(1, 2, 0)
            per_s5["ssm_b_im"][l] = dbi.reshape(hh, gg, p).transpose(1, 2, 0)
            per_s5["ssm_c_re"][l] = dcr.reshape(hh, gg, p).transpose(1, 0, 2)
            per_s5["ssm_c_im"][l] = dci.reshape(hh, gg, p).transpose(1, 0, 2)
            per_s5["ssm_d"][l] = dd.reshape(d)
            branches = [(w["g_mix"][l], [du])]
        else:
            j = l - n_s5
            do = _mm_cols(d1, w["w_o"], j, wc=1, name=f"o_dx{j}")
            per_fox["w_o"][j] = _mm_tn(sv["o"], d1, 1, name=f"o_dw{j}")[0]
            dq, dk, dv, dck, dcq = _flash_bwd(sv["qs"], kvb, sv["o"], sv["lse"], do, cq3, ck3, tq=tq,
                                              name=f"flash_bwd{j}")
            dk_parts.append(dk)
            dv_parts.append(dv)
            dck_parts.append(dck.reshape(nh, n_rows).T + dcq.transpose(1, 0, 2).reshape(n_rows, nh))
            scale = HEAD_DIM ** -0.5
            dhn = _mm_cols(dq, w["w_q"], j, wc=1, scale=scale, name=f"q_dx{j}")
            per_fox["w_q"][j] = _mm_tn(sv["hn"], dq, 1, scale=scale, name=f"q_dw{j}")[0]
            branches = [(w["g_mix"][l], [dhn])]
            if j == 0:
                def kv_sum(*parts):
                    half = len(parts) // 2
                    return jnp.concatenate([sum(parts[:half][1:], parts[0]),
                                            sum(parts[half:][1:], parts[half])], axis=1)
                dkv = _rowwise(kv_sum, dk_parts + dv_parts, [], [(2 * d, BF16)], name="dkv_sum")[0]
                dck_tot = dck_parts[0]
                for extra in dck_parts[1:]:
                    dck_tot = dck_tot + extra
                dcum = jnp.zeros((n_rows, LANES), F32).at[:, :nh].set(dck_tot)
                dfl, dbf = _cum_bwd(dcum, fl, bf_pad, name="cum_bwd")
                g["b_f"] = dbf[0, :nh]
                dhkv_a = _mm_cols(dkv, w["w_kv"], 0, wc=1, name="kv_dx")
                dhkv_b = _mm_cols(dfl, w["w_f"], 0, wc=1, name="f_dx")
                g["w_kv"] = _mm_tn(hnkv, dkv, 1, name="kv_dw")[0]
                g["w_f"] = _mm_tn(hnkv, dfl, 1, name="f_dw")[0]
                branches.append((w["g_kv"], [dhkv_a, dhkv_b]))
        dcur, dgs = _node_bwd(d1, sv["h"], branches, name=f"mix_norm_bwd{l}")
        per_layer["g_mix"][l] = dgs[0]
        if len(dgs) > 1:
            g["g_kv"] = dgs[1]

    def by_row_shard(mats):
        return jnp.stack([m.reshape(N_CHIPS, m.shape[0] // N_CHIPS, m.shape[1]) for m in mats], axis=1)

    for k, v in per_layer.items():
        g[k] = jnp.stack(v, axis=1) if k == "w_in" else by_row_shard(v) if k == "w_out" else jnp.stack(v)
    for k, v in per_s5.items():
        g[k] = jnp.stack(v, axis=1) if k == "w_glu" else jnp.stack(v)
    for k, v in per_fox.items():
        g[k] = by_row_shard(v)
    return loss, dcur, g


def _position():
    x, y, c = lax.axis_index("x"), lax.axis_index("y"), lax.axis_index("c")
    chips = [(1 - x, y), (x, 1 - y), (1 - x, 1 - y)]
    return x, y, c, chips


def _any_specs(n):
    return [pl.BlockSpec(memory_space=pl.ANY)] * n


def _all_gather(shards, kinds, *, name):
    n = len(shards)

    def out_shape(a, kind):
        shape = (N_CHIPS,) + a.shape if kind == "C" else (a.shape[0], N_CHIPS) + a.shape[1:]
        return jax.ShapeDtypeStruct(shape, a.dtype)

    def body(*refs):
        ins, outs = refs[:n], refs[n:2 * n]
        send_sems, recv_sems, local_sems = refs[2 * n:]
        x, y, c, chips = _position()
        my_slot = 2 * x + y
        sibling = (x, y, 1 - c)

        def rows(t, half):
            hr = ins[t].shape[0] // 2
            return pl.ds(half * hr, hr)

        def piece(t, slot, half):
            return outs[t].at[slot, rows(t, half)] if kinds[t] == "C" else outs[t].at[rows(t, half), slot]

        def whole(t, slot):
            full = pl.ds(0, ins[t].shape[0])
            return outs[t].at[slot, full] if kinds[t] == "C" else outs[t].at[full, slot]

        def remote(k, t, src, dst, to):
            return pltpu.make_async_remote_copy(src_ref=src, dst_ref=dst, send_sem=send_sems.at[k, t],
                                                recv_sem=recv_sems.at[k, t], device_id=to, device_id_type=MESH)

        local = [pltpu.make_async_copy(ins[t], whole(t, my_slot), local_sems.at[t]) for t in range(n)]
        for cp in local:
            cp.start()
        started = []
        for j, chip in enumerate(chips):
            for t in range(n):
                cp = remote(j, t, ins[t].at[rows(t, c)], piece(t, my_slot, c), (*chip, c))
                cp.start()
                started.append(cp)
        for j, chip in enumerate(chips):
            slot = 2 * chip[0] + chip[1]
            for t in range(n):
                remote(j, t, piece(t, slot, c), piece(t, slot, c), (*chip, c)).wait_recv()
            for t in range(n):
                cp = remote(3 + j, t, piece(t, slot, c), piece(t, slot, c), sibling)
                cp.start()
                started.append(cp)
        for j, chip in enumerate(chips):
            slot = 2 * chip[0] + chip[1]
            for t in range(n):
                remote(3 + j, t, piece(t, slot, 1 - c), piece(t, slot, 1 - c), sibling).wait_recv()
        for cp in started:
            cp.wait_send()
        for cp in local:
            cp.wait()

    return pl.pallas_call(
        body, in_specs=_any_specs(n), out_specs=_any_specs(n),
        out_shape=[out_shape(a, k) for a, k in zip(shards, kinds)],
        scratch_shapes=[pltpu.SemaphoreType.DMA((6, n)), pltpu.SemaphoreType.DMA((6, n)),
                        pltpu.SemaphoreType.DMA((n,))],
        name=name)(*shards)


def _pair_exchange(grads, *, name):
    n = len(grads)

    def body(*refs):
        ins, outs = refs[:n], refs[n:2 * n]
        send_sems, recv_sems = refs[2 * n:]
        x, y, c, _ = _position()
        copies = [pltpu.make_async_remote_copy(src_ref=ins[t].at[:, 1 - c], dst_ref=outs[t],
                                               send_sem=send_sems.at[t], recv_sem=recv_sems.at[t],
                                               device_id=(x, y, 1 - c), device_id_type=MESH) for t in range(n)]
        for cp in copies:
            cp.start()
        for cp in copies:
            cp.wait()

    return pl.pallas_call(
        body, in_specs=_any_specs(n), out_specs=_any_specs(n),
        out_shape=[jax.ShapeDtypeStruct((a.shape[0],) + a.shape[2:], a.dtype) for a in grads],
        scratch_shapes=[pltpu.SemaphoreType.DMA((n,)), pltpu.SemaphoreType.DMA((n,))], name=name)(*grads)


def _chip_exchange(parts, *, name):
    n = len(parts)

    def body(*refs):
        ins, outs = refs[:n], refs[n:2 * n]
        send_sems, recv_sems = refs[2 * n:]
        _, _, c, chips = _position()
        copies = []
        for j, chip in enumerate(chips):
            slot = 2 * chip[0] + chip[1]
            for t in range(n):
                cp = pltpu.make_async_remote_copy(src_ref=ins[t].at[slot], dst_ref=outs[t].at[j],
                                                  send_sem=send_sems.at[j, t], recv_sem=recv_sems.at[j, t],
                                                  device_id=(*chip, c), device_id_type=MESH)
                cp.start()
                copies.append(cp)
        for cp in copies:
            cp.wait()

    return pl.pallas_call(
        body, in_specs=_any_specs(n), out_specs=_any_specs(n),
        out_shape=[jax.ShapeDtypeStruct((N_CHIPS - 1,) + a.shape[1:], a.dtype) for a in parts],
        scratch_shapes=[pltpu.SemaphoreType.DMA((N_CHIPS - 1, n)), pltpu.SemaphoreType.DMA((N_CHIPS - 1, n))],
        name=name)(*parts)


def _pair_share(halves, *, name):
    n = len(halves)

    def body(*refs):
        ins, outs = refs[:n], refs[n:2 * n]
        send_sems, recv_sems, local_sems = refs[2 * n:]
        x, y, c, _ = _position()
        local = [pltpu.make_async_copy(ins[t], outs[t].at[c], local_sems.at[t]) for t in range(n)]
        copies = [pltpu.make_async_remote_copy(src_ref=ins[t], dst_ref=outs[t].at[c], send_sem=send_sems.at[t],
                                               recv_sem=recv_sems.at[t], device_id=(x, y, 1 - c),
                                               device_id_type=MESH) for t in range(n)]
        for cp in local + copies:
            cp.start()
        for t in range(n):
            pltpu.make_async_remote_copy(src_ref=ins[t], dst_ref=outs[t].at[1 - c], send_sem=send_sems.at[t],
                                         recv_sem=recv_sems.at[t], device_id=(x, y, 1 - c),
                                         device_id_type=MESH).wait()
        for cp in local:
            cp.wait()

    return pl.pallas_call(
        body, in_specs=_any_specs(n), out_specs=_any_specs(n),
        out_shape=[jax.ShapeDtypeStruct((N_CORES,) + a.shape, a.dtype) for a in halves],
        scratch_shapes=[pltpu.SemaphoreType.DMA((n,)), pltpu.SemaphoreType.DMA((n,)),
                        pltpu.SemaphoreType.DMA((n,))], name=name)(*halves)


def _sum_pair(grad, landed, c, *, name):
    slots, _, m, n = grad.shape
    tm = _tile(m, 256)

    def body(c_ref, g_ref, l_ref, o_ref):
        o_ref[...] = g_ref[0] + l_ref[...]

    return pl.pallas_call(
        body,
        grid_spec=pltpu.PrefetchScalarGridSpec(
            num_scalar_prefetch=1, grid=(slots, m // tm),
            in_specs=[pl.BlockSpec((1, 1, tm, n), lambda s, i, c_ref: (s, c_ref[0], i, 0)),
                      pl.BlockSpec((1, tm, n), lambda s, i, c_ref: (s, i, 0))],
            out_specs=pl.BlockSpec((1, tm, n), lambda s, i, c_ref: (s, i, 0))),
        out_shape=jax.ShapeDtypeStruct((slots, m, n), F32), compiler_params=_cp(2), name=name)(c, grad, landed)


def _sum_chips(part, landed, slot, *, name):
    _, m, n = part.shape
    tm = _tile(m, 256)

    def body(s_ref, p_ref, l_ref, o_ref):
        acc = p_ref[0]
        for j in range(N_CHIPS - 1):
            acc = acc + l_ref[j]
        o_ref[...] = acc

    return pl.pallas_call(
        body,
        grid_spec=pltpu.PrefetchScalarGridSpec(
            num_scalar_prefetch=1, grid=(m // tm,),
            in_specs=[pl.BlockSpec((1, tm, n), lambda i, s_ref: (s_ref[0], i, 0)),
                      pl.BlockSpec((N_CHIPS - 1, tm, n), lambda i, s_ref: (0, i, 0))],
            out_specs=pl.BlockSpec((tm, n), lambda i, s_ref: (i, 0))),
        out_shape=jax.ShapeDtypeStruct((m, n), F32), compiler_params=_cp(1), name=name)(slot, part, landed)


def _reduce_scatter(grads):
    c = lax.axis_index("c").reshape(1).astype(jnp.int32)
    slot = (2 * lax.axis_index("x") + lax.axis_index("y")).reshape(1).astype(jnp.int32)
    views = []
    for a in grads:
        lead, last = a.shape[1], a.shape[-1]
        mid = 1
        for s in a.shape[2:-1]:
            mid *= s
        views.append(a.reshape(N_CHIPS, N_CORES, (lead // N_CORES) * mid, last))
    landed = _pair_exchange(views, name="rs_pair_exchange")
    parts = [_sum_pair(v, l, c, name=f"rs_pair_sum{t}") for t, (v, l) in enumerate(zip(views, landed))]
    landed = _chip_exchange(parts, name="rs_chip_exchange")
    halves = [_sum_chips(p, l, slot, name=f"rs_chip_sum{t}") for t, (p, l) in enumerate(zip(parts, landed))]
    full = _pair_share(halves, name="rs_pair_share")
    return [f.reshape(a.shape[1:]) for f, a in zip(full, grads)]


def _adamw(w, g, m, v, *, name):
    def fn(ww, gg, mm, vv):
        mm = ADAM_B1 * mm + (1.0 - ADAM_B1) * gg
        vv = ADAM_B2 * vv + (1.0 - ADAM_B2) * (gg * gg)
        m_hat = mm / (1.0 - ADAM_B1 ** ADAM_STEP)
        v_hat = vv / (1.0 - ADAM_B2 ** ADAM_STEP)
        delta = -ADAM_LR * (m_hat / (jnp.sqrt(v_hat) + ADAM_EPS) + ADAM_WD * ww)
        return delta, mm, vv

    shape = w.shape
    two_d = [a.reshape(-1, shape[-1]) for a in (w, g, m, v)]
    outs = _rowwise(fn, two_d, [], [(shape[-1], F32)] * 3, name=name)
    return [o.reshape(shape) for o in outs]


def _to_bf16(a, *, name):
    two_d = a.reshape(-1, a.shape[-1])
    return _rowwise(lambda t: t, [two_d], [], [(a.shape[-1], BF16)], name=name)[0].reshape(a.shape)


def _pack(arrays, rows_multiple):
    flat = jnp.concatenate([a.reshape(-1) for a in arrays])
    rows = -(-flat.shape[0] // LANES)
    rows = -(-rows // rows_multiple) * rows_multiple
    return jnp.pad(flat, (0, rows * LANES - flat.shape[0])).reshape(rows, LANES)


def _unpack(packed, like):
    flat = packed.reshape(-1)
    out, pos = [], 0
    for a in like:
        out.append(flat[pos:pos + a.size].reshape(a.shape))
        pos += a.size
    return out


_PARAMS = ("g_mix", "g_ffn", "lam_re", "lam_im", "log_dt", "ssm_b_re", "ssm_b_im", "ssm_c_re", "ssm_c_im", "ssm_d",
           "w_glu", "g_kv", "w_kvf", "b_f", "w_q", "w_o", "w_ffn_in", "ffn_conv_w", "ffn_conv_b", "w_ffn_out",
           "g_final")
_BIG = ("w_glu", "w_kvf", "w_q", "w_o", "w_ffn_in", "w_ffn_out")
_SMALL_SHARDED = ("ssm_d", "ffn_conv_w")


def kernel(x, g_mix, g_ffn, lam_re, lam_im, log_dt, ssm_b_re, ssm_b_im, ssm_c_re, ssm_c_im, ssm_d, w_glu, g_kv, w_kvf, b_f, w_q, w_o, w_ffn_in, ffn_conv_w, ffn_conv_b, w_ffn_out, g_final, loss_target, m_g_mix, m_g_ffn, m_lam_re, m_lam_im, m_log_dt, m_ssm_b_re, m_ssm_b_im, m_ssm_c_re, m_ssm_c_im, m_ssm_d, m_w_glu, m_g_kv, m_w_kvf, m_b_f, m_w_q, m_w_o, m_w_ffn_in, m_ffn_conv_w, m_ffn_conv_b, m_w_ffn_out, m_g_final, v_g_mix, v_g_ffn, v_lam_re, v_lam_im, v_log_dt, v_ssm_b_re, v_ssm_b_im, v_ssm_c_re, v_ssm_c_im, v_ssm_d, v_w_glu, v_g_kv, v_w_kvf, v_b_f, v_w_q, v_w_o, v_w_ffn_in, v_ffn_conv_w, v_ffn_conv_b, v_w_ffn_out, v_g_final):
    p = dict(g_mix=g_mix, g_ffn=g_ffn, lam_re=lam_re, lam_im=lam_im, log_dt=log_dt, ssm_b_re=ssm_b_re,
             ssm_b_im=ssm_b_im, ssm_c_re=ssm_c_re, ssm_c_im=ssm_c_im, ssm_d=ssm_d, w_glu=w_glu, g_kv=g_kv,
             w_kvf=w_kvf, b_f=b_f, w_q=w_q, w_o=w_o, w_ffn_in=w_ffn_in, ffn_conv_w=ffn_conv_w,
             ffn_conv_b=ffn_conv_b, w_ffn_out=w_ffn_out, g_final=g_final)
    mom1 = dict(zip(_PARAMS, (m_g_mix, m_g_ffn, m_lam_re, m_lam_im, m_log_dt, m_ssm_b_re, m_ssm_b_im, m_ssm_c_re,
                              m_ssm_c_im, m_ssm_d, m_w_glu, m_g_kv, m_w_kvf, m_b_f, m_w_q, m_w_o, m_w_ffn_in,
                              m_ffn_conv_w, m_ffn_conv_b, m_w_ffn_out, m_g_final)))
    mom2 = dict(zip(_PARAMS, (v_g_mix, v_g_ffn, v_lam_re, v_lam_im, v_log_dt, v_ssm_b_re, v_ssm_b_im, v_ssm_c_re,
                              v_ssm_c_im, v_ssm_d, v_w_glu, v_g_kv, v_w_kvf, v_b_f, v_w_q, v_w_o, v_w_ffn_in,
                              v_ffn_conv_w, v_ffn_conv_b, v_w_ffn_out, v_g_final)))
    d = x.shape[-1]
    nh = b_f.shape[0]
    slot = 2 * lax.axis_index("x") + lax.axis_index("y")

    shards = [_to_bf16(p[k], name=f"to_bf16_{k}") for k in _BIG] + [p[k] for k in _SMALL_SHARDED]
    kinds = ["C", "C", "R", "R", "C", "R", "C", "C"]
    gl, gkvf, gq, go, gin, gout, gd, gcw = _all_gather(shards, kinds, name="weights_all_gather")
    n_lay = gin.shape[1]
    kvf_full = gkvf.transpose(1, 0, 2).reshape(d, -1)
    w_f = jnp.zeros((d, LANES), BF16).at[:, :nh].set(kvf_full[:, 2 * d:])
    w = dict(p)
    w.update(
        w_glu=gl, w_kv=kvf_full[:, :2 * d][None, None], w_f=w_f[None, None],
        w_q=gq.reshape(1, gq.shape[0], d, d), w_o=go.reshape(1, go.shape[0], d, d), w_in=gin,
        w_out=gout.reshape(1, n_lay, -1, d),
        conv_w=gcw.transpose(1, 2, 0, 3).reshape(n_lay, DT_CONV_TAPS, -1), conv_b=ffn_conv_b,
        ssm_d=gd.transpose(1, 0, 2).reshape(gd.shape[1], d))

    loss_part, grad_x, g = _local_step(x[0], loss_target[0], w)
    loss = lax.psum(loss_part, ("x", "y", "c"))

    g_kvf = jnp.concatenate([g["w_kv"], g["w_f"][:, :nh]], axis=1)
    big = dict(w_glu=g["w_glu"], w_kvf=g_kvf.reshape(d, N_CHIPS, -1).transpose(1, 0, 2), w_q=g["w_q"], w_o=g["w_o"],
               w_ffn_in=g["w_in"], w_ffn_out=g["w_out"])
    small_names = [k for k in _PARAMS if k not in _BIG]
    small_full = dict(g_mix=g["g_mix"], g_ffn=g["g_ffn"], lam_re=g["lam_re"], lam_im=g["lam_im"], log_dt=g["log_dt"],
                      ssm_b_re=g["ssm_b_re"], ssm_b_im=g["ssm_b_im"], ssm_c_re=g["ssm_c_re"], ssm_c_im=g["ssm_c_im"],
                      ssm_d=g["ssm_d"], g_kv=g["g_kv"], b_f=g["b_f"], ffn_conv_w=g["conv_w"],
                      ffn_conv_b=g["conv_b"], g_final=g["g_final"])
    small_list = [small_full[k] for k in small_names]
    pack = _pack(small_list, N_CHIPS * N_CORES * SUBLANES)
    pack4 = pack.reshape(N_CHIPS, pack.shape[0] // N_CHIPS, LANES)
    reduced = _reduce_scatter([big[k] for k in _BIG] + [pack4])
    red_big = dict(zip(_BIG, reduced[:-1]))
    pack_all = _all_gather([reduced[-1]], ["C"], name="small_grads_all_gather")[0]
    red_small = dict(zip(small_names, _unpack(pack_all, small_list)))
    for k in _SMALL_SHARDED:
        width = p[k].shape[-1]
        red_small[k] = lax.dynamic_slice_in_dim(red_small[k], slot * width, width, axis=red_small[k].ndim - 1)

    grads, deltas, new_m, new_v = {}, {}, {}, {}
    for k in _BIG:
        grads[k] = red_big[k]
        deltas[k], new_m[k], new_v[k] = _adamw(p[k], grads[k], mom1[k], mom2[k], name=f"adamw_{k}")
    packs = [_pack([src[k] for k in small_names], SUBLANES) for src in (p, red_small, mom1, mom2)]
    like = [p[k] for k in small_names]
    outs = [_unpack(o, like) for o in _adamw(*packs, name="adamw_small")]
    for i, k in enumerate(small_names):
        grads[k] = red_small[k]
        deltas[k], new_m[k], new_v[k] = outs[0][i], outs[1][i], outs[2][i]
    return (loss, grad_x[None], *[grads[k] for k in _PARAMS], *[deltas[k] for k in _PARAMS],
            *[new_m[k] for k in _PARAMS], *[new_v[k] for k in _PARAMS])
```

```python
import functools

import jax
import jax.numpy as jnp
from jax import lax
from jax.experimental import pallas as pl
from jax.experimental.pallas import tpu as pltpu

F32 = jnp.float32
BF16 = jnp.bfloat16

RMS_EPS = 1e-6
ADAM_LR = 0.001
ADAM_B1 = 0.9
ADAM_B2 = 0.999
ADAM_EPS = 1e-08
ADAM_WD = 0.01
ADAM_STEP = 10
DT_CONV_TAPS = 3

LANES = 128
SUBLANES = 8
HEAD_DIM = 64
FLASH_ROW_TILE = 32
S5_BLOCK_GROUPS = 16
VMEM_LIMIT_BYTES = 48 << 20
N_CHIPS = 4
N_CORES = 2
MESH = pl.DeviceIdType.MESH


def _cp(n_grid):
    return pltpu.CompilerParams(dimension_semantics=("arbitrary",) * n_grid, vmem_limit_bytes=VMEM_LIMIT_BYTES)


def _tile(n, pref, mult=SUBLANES):
    if n <= pref:
        return n
    t = (pref // mult) * mult
    while t >= mult:
        if n % t == 0:
            return t
        t -= mult
    return n


def _dot(a, b, ca, cb):
    return lax.dot_general(a, b, (((ca,), (cb,)), ((), ())), preferred_element_type=F32)


def _mm_cols(x, w4, layer, *, wc, out_dtype=F32, scale=None, name):
    m, k = x.shape
    slots, _, k0, k1 = w4.shape
    nb = k1 if wc == 0 else k0
    assert (k0 if wc == 0 else k1) == k
    tm = _tile(m, 512)

    def body(x_ref, w_ref, o_ref):
        acc = _dot(x_ref[...].astype(BF16), w_ref[0, 0], 1, wc)
        if scale is not None:
            acc = acc * scale
        o_ref[...] = acc.astype(out_dtype)

    return pl.pallas_call(
        body, grid=(slots, m // tm),
        in_specs=[pl.BlockSpec((tm, k), lambda s, i: (i, 0)),
                  pl.BlockSpec((1, 1, k0, k1), lambda s, i: (s, layer, 0, 0))],
        out_specs=pl.BlockSpec((tm, nb), lambda s, i: (i, s)),
        out_shape=jax.ShapeDtypeStruct((m, slots * nb), out_dtype),
        compiler_params=_cp(2), name=name)(x, w4)


def _planes(a):
    return a if a.ndim == 3 else a[None]


def _mm_acc(x, w4, layer, *, wc, name):
    x = _planes(x)
    n_planes, m, width = x.shape
    slots, _, k0, k1 = w4.shape
    kb = k0 if wc == 0 else k1
    nout = k1 if wc == 0 else k0
    assert n_planes * width == slots * kb
    spp = slots // n_planes
    tm = _tile(m, 512)

    def body(x_ref, w_ref, o_ref):
        @pl.when(pl.program_id(1) == 0)
        def _():
            o_ref[...] = jnp.zeros_like(o_ref)
        o_ref[...] += _dot(x_ref[0].astype(BF16), w_ref[0, 0], 1, wc)

    return pl.pallas_call(
        body, grid=(m // tm, slots),
        in_specs=[pl.BlockSpec((1, tm, kb), lambda i, s: (s // spp, i, s % spp)),
                  pl.BlockSpec((1, 1, k0, k1), lambda i, s: (s, layer, 0, 0))],
        out_specs=pl.BlockSpec((tm, nout), lambda i, s: (i, 0)),
        out_shape=jax.ShapeDtypeStruct((m, nout), F32),
        compiler_params=_cp(2), name=name)(x, w4)


def _mm_tn(x, dy, slots, *, scale=None, name):
    m, k = x.shape
    dy = _planes(dy)
    n_planes, _, width = dy.shape
    n = n_planes * width // slots
    spp = slots // n_planes
    ta = _tile(k, 512, LANES)
    tm = _tile(m, 512)
    n_m = m // tm

    def body(x_ref, dy_ref, o_ref):
        @pl.when(pl.program_id(2) == 0)
        def _():
            o_ref[...] = jnp.zeros_like(o_ref)
        o_ref[0] += _dot(x_ref[...].astype(BF16), dy_ref[0].astype(BF16), 0, 0)
        if scale is not None:
            @pl.when(pl.program_id(2) == n_m - 1)
            def _():
                o_ref[...] = o_ref[...] * scale

    return pl.pallas_call(
        body, grid=(slots, k // ta, n_m),
        in_specs=[pl.BlockSpec((tm, ta), lambda s, a, i: (i, a)),
                  pl.BlockSpec((1, tm, n), lambda s, a, i: (s // spp, i, s % spp))],
        out_specs=pl.BlockSpec((1, ta, n), lambda s, a, i: (s, a, 0)),
        out_shape=jax.ShapeDtypeStruct((slots, k, n), F32),
        compiler_params=_cp(3), name=name)(x, dy)


def _rowwise(fn, rows, consts, outs, accs=(), *, tl=256, name):
    n_rows = rows[0].shape[0]
    tl = _tile(n_rows, tl)
    n_in = len(rows) + len(consts)
    n_out = len(outs)

    def body(*refs):
        res = fn(*[r[...] for r in refs[:n_in]])
        res = res if isinstance(res, (tuple, list)) else (res,)
        o_refs = refs[n_in:n_in + n_out]
        a_refs = refs[n_in + n_out:]
        for o, val in zip(o_refs, res[:n_out]):
            o[...] = val.astype(o.dtype)
        if a_refs:
            @pl.when(pl.program_id(0) == 0)
            def _():
                for a in a_refs:
                    a[...] = jnp.zeros_like(a)
            for a, val in zip(a_refs, res[n_out:]):
                a[...] += val

    in_specs = ([pl.BlockSpec((tl, r.shape[1]), lambda i: (i, 0)) for r in rows]
                + [pl.BlockSpec(c.shape, lambda i: (0, 0)) for c in consts])
    out_specs = ([pl.BlockSpec((tl, w), lambda i: (i, 0)) for w, _ in outs]
                 + [pl.BlockSpec(s, lambda i: (0, 0)) for s in accs])
    out_shape = ([jax.ShapeDtypeStruct((n_rows, w), dt) for w, dt in outs]
                 + [jax.ShapeDtypeStruct(s, F32) for s in accs])
    return pl.pallas_call(body, grid=(n_rows // tl,), in_specs=in_specs, out_specs=out_specs,
                          out_shape=out_shape, compiler_params=_cp(1), name=name)(*rows, *consts)


def _rms(x, g):
    return x * lax.rsqrt(jnp.mean(x * x, axis=-1, keepdims=True) + RMS_EPS) * g


def _sigmoid(x):
    return 1.0 / (1.0 + jnp.exp(-x))


def _glu(zz):
    d = zz.shape[1] // 2
    return zz[:, :d] * _sigmoid(zz[:, d:])


def _gelu(y):
    return jax.nn.gelu(y)


def _row2(v):
    return v.reshape(1, -1)


def _node_bwd(d_in, h, branches, *, name):
    width = h.shape[1]
    flat = [dy for _, dys in branches for dy in dys]
    counts = [len(dys) for _, dys in branches]
    gains = [_row2(g) for g, _ in branches]

    def fn(d, hh, *rest):
        dys, gs = rest[:len(flat)], rest[len(flat):]
        tot, dgs, pos = d, [], 0
        for g, cnt in zip(gs, counts):
            dy = dys[pos].astype(F32)
            for extra in dys[pos + 1:pos + cnt]:
                dy = dy + extra.astype(F32)
            pos += cnt
            _, vjp = jax.vjp(_rms, hh, g)
            dx, dg = vjp(dy)
            tot = tot + dx
            dgs.append(dg)
        return (tot, *dgs)

    res = _rowwise(fn, [d_in, h, *flat], gains, [(width, F32)], [(1, width)] * len(branches), name=name)
    return res[0], [r[0] for r in res[1:]]


def _s5_prep_fn(lr, li, ldt, br, bi, cr, ci, *, gq, h, p):
    dt = jnp.exp(ldt)
    mag = jnp.exp(lr * dt)
    lb_re = mag * jnp.cos(li * dt)
    lb_im = mag * jnp.sin(li * dt)
    den = lr * lr + li * li
    nr = lb_re - 1.0
    fr = (nr * lr + lb_im * li) / den
    fi = (lb_im * lr - nr * li) / den
    bb_re = fr * br - fi * bi
    bb_im = fr * bi + fi * br
    shape = (gq * h, gq * p)
    r = lax.broadcasted_iota(jnp.int32, shape, 0)
    c = lax.broadcasted_iota(jnp.int32, shape, 1)
    mask = jnp.where(jnp.right_shift(r, h.bit_length() - 1) == jnp.right_shift(c, p.bit_length() - 1), 1.0, 0.0)

    def expand(t):
        return jnp.concatenate([t] * gq, axis=0) * mask

    return lb_re, lb_im, expand(bb_re), expand(bb_im), expand(cr), expand(ci)


def _s5_prep(lr, li, ldt, br, bi, cr, ci, p, *, name):
    n = lr.shape[1]
    h = br.shape[0]
    gq = S5_BLOCK_GROUPS
    nq, cq = gq * p, gq * h
    nblk = n // nq
    fn = functools.partial(_s5_prep_fn, gq=gq, h=h, p=p)

    def body(lr_r, li_r, ldt_r, br_r, bi_r, cr_r, ci_r, lbr_o, lbi_o, wbr_o, wbi_o, wcr_o, wci_o):
        lb_re, lb_im, wbr, wbi, wcr, wci = fn(lr_r[...], li_r[...], ldt_r[...], br_r[...], bi_r[...],
                                              cr_r[...], ci_r[...])
        lbr_o[...] = lb_re
        lbi_o[...] = lb_im
        wbr_o[0] = wbr.astype(BF16)
        wbi_o[0] = wbi.astype(BF16)
        wcr_o[0] = wcr.astype(BF16)
        wci_o[0] = wci.astype(BF16)

    vec = pl.BlockSpec((1, nq), lambda q: (0, q))
    tab = pl.BlockSpec((h, nq), lambda q: (0, q))
    wsp = pl.BlockSpec((1, cq, nq), lambda q: (q, 0, 0))
    wsh = jax.ShapeDtypeStruct((nblk, cq, nq), BF16)
    vsh = jax.ShapeDtypeStruct((1, n), F32)
    return pl.pallas_call(body, grid=(nblk,), in_specs=[vec, vec, vec, tab, tab, tab, tab],
                          out_specs=[vec, vec, wsp, wsp, wsp, wsp], out_shape=[vsh, vsh, wsh, wsh, wsh, wsh],
                          compiler_params=_cp(1), name=name)(lr, li, ldt, br, bi, cr, ci)


def _s5_prep_bwd(lr, li, ldt, br, bi, cr, ci, p, dlbr, dlbi, dwbr, dwbi, dwcr, dwci, *, name):
    n = lr.shape[1]
    h = br.shape[0]
    gq = S5_BLOCK_GROUPS
    nq, cq = gq * p, gq * h
    nblk = n // nq
    fn = functools.partial(_s5_prep_fn, gq=gq, h=h, p=p)

    def body(lr_r, li_r, ldt_r, br_r, bi_r, cr_r, ci_r, dlbr_r, dlbi_r, dwbr_r, dwbi_r, dwcr_r, dwci_r,
             *outs):
        _, vjp = jax.vjp(fn, lr_r[...], li_r[...], ldt_r[...], br_r[...], bi_r[...], cr_r[...], ci_r[...])
        grads = vjp((dlbr_r[0], dlbi_r[0], dwbr_r[0], dwbi_r[0], dwcr_r[0], dwci_r[0]))
        for o, g in zip(outs, grads):
            o[...] = g

    vec = pl.BlockSpec((1, nq), lambda q: (0, q))
    tab = pl.BlockSpec((h, nq), lambda q: (0, q))
    vec3 = pl.BlockSpec((1, 1, nq), lambda q: (q, 0, 0))
    wsp = pl.BlockSpec((1, cq, nq), lambda q: (q, 0, 0))
    vsh = jax.ShapeDtypeStruct((1, n), F32)
    tsh = jax.ShapeDtypeStruct((h, n), F32)
    return pl.pallas_call(body, grid=(nblk,),
                          in_specs=[vec, vec, vec, tab, tab, tab, tab, vec3, vec3, wsp, wsp, wsp, wsp],
                          out_specs=[vec, vec, vec, tab, tab, tab, tab],
                          out_shape=[vsh, vsh, vsh, tsh, tsh, tsh, tsh],
                          compiler_params=_cp(1), name=name)(lr, li, ldt, br, bi, cr, ci,
                                                             dlbr, dlbi, dwbr, dwbi, dwcr, dwci)


def _scan_rows(s_re, s_im, a_re, a_im, c_re, c_im, *, reverse):
    t_rows, n = s_re.shape
    nb = t_rows // SUBLANES
    row = lax.broadcasted_iota(jnp.int32, (SUBLANES, n), 0)

    def cmul(x, y):
        return x[0] * y[0] - x[1] * y[1], x[0] * y[1] + x[1] * y[0]

    a1 = (jnp.broadcast_to(a_re, (SUBLANES, n)), jnp.broadcast_to(a_im, (SUBLANES, n)))
    a2 = cmul(a1, a1)
    a4 = cmul(a2, a2)
    pk = (a_re, a_im)
    tab_re = jnp.zeros((SUBLANES, n), F32)
    tab_im = jnp.zeros((SUBLANES, n), F32)
    for i in range(SUBLANES):
        at = (SUBLANES - 1 - i) if reverse else i
        tab_re = jnp.where(row == at, pk[0], tab_re)
        tab_im = jnp.where(row == at, pk[1], tab_im)
        pk = cmul(pk, (a_re, a_im))

    def step(b, carry):
        cr, ci = carry
        blk = (nb - 1 - b) if reverse else b
        off = pl.multiple_of(blk * SUBLANES, SUBLANES)
        x_re = s_re[pl.ds(off, SUBLANES), :]
        x_im = s_im[pl.ds(off, SUBLANES), :]
        for d, (pr, pi) in ((1, a1), (2, a2), (4, a4)):
            if reverse:
                keep = row < SUBLANES - d
                sh = SUBLANES - d
            else:
                keep = row >= d
                sh = d
            sh_re = jnp.where(keep, pltpu.roll(x_re, sh, 0), 0.0)
            sh_im = jnp.where(keep, pltpu.roll(x_im, sh, 0), 0.0)
            x_re, x_im = x_re + pr * sh_re - pi * sh_im, x_im + pr * sh_im + pi * sh_re
        x_re, x_im = x_re + tab_re * cr - tab_im * ci, x_im + tab_re * ci + tab_im * cr
        s_re[pl.ds(off, SUBLANES), :] = x_re
        s_im[pl.ds(off, SUBLANES), :] = x_im
        edge = 0 if reverse else SUBLANES - 1
        return x_re[edge:edge + 1, :], x_im[edge:edge + 1, :]

    return lax.fori_loop(0, nb, step, (c_re, c_im))


def _s5_fwd(u, prep, dskip, *, name):
    lb_re, lb_im, wbr, wbi, wcr, wci = prep
    n_rows, _ = u.shape
    nblk, cq, nq = wbr.shape
    tt = _tile(n_rows, 512)
    nch = n_rows // tt

    def body(u_ref, wbr_r, wbi_r, wcr_r, wci_r, lbr_r, lbi_r, d_ref, y_ref, sbr_o, sbi_o, s_re, s_im, c_re, c_im):
        @pl.when(pl.program_id(1) == 0)
        def _():
            c_re[...] = jnp.zeros_like(c_re)
            c_im[...] = jnp.zeros_like(c_im)
        uf = u_ref[...]
        ub = uf.astype(BF16)
        s_re[...] = _dot(ub, wbr_r[0], 1, 0)
        s_im[...] = _dot(ub, wbi_r[0], 1, 0)
        sbr_o[0] = c_re[...]
        sbi_o[0] = c_im[...]
        cr, ci = _scan_rows(s_re, s_im, lbr_r[...], lbi_r[...], c_re[...], c_im[...], reverse=False)
        c_re[...] = cr
        c_im[...] = ci
        y = _dot(s_re[...].astype(BF16), wcr_r[0], 1, 1) - _dot(s_im[...].astype(BF16), wci_r[0], 1, 1)
        y_ref[...] = y + d_ref[...] * uf

    wsp = pl.BlockSpec((1, cq, nq), lambda q, i: (q, 0, 0))
    vec = pl.BlockSpec((1, nq), lambda q, i: (0, q))
    act = pl.BlockSpec((tt, cq), lambda q, i: (i, q))
    sb = pl.BlockSpec((1, 1, nq), lambda q, i: (i, 0, q))
    sbsh = jax.ShapeDtypeStruct((nch, 1, nblk * nq), F32)
    return pl.pallas_call(
        body, grid=(nblk, nch),
        in_specs=[act, wsp, wsp, wsp, wsp, vec, vec, pl.BlockSpec((1, cq), lambda q, i: (0, q))],
        out_specs=[act, sb, sb],
        out_shape=[jax.ShapeDtypeStruct(u.shape, F32), sbsh, sbsh],
        scratch_shapes=[pltpu.VMEM((tt, nq), F32), pltpu.VMEM((tt, nq), F32),
                        pltpu.VMEM((1, nq), F32), pltpu.VMEM((1, nq), F32)],
        compiler_params=_cp(2), name=name)(u, wbr, wbi, wcr, wci, lb_re, lb_im, dskip)


def _s5_bwd(u, dy, sb_re, sb_im, prep, dskip, *, name):
    lb_re, lb_im, wbr, wbi, wcr, wci = prep
    n_rows, _ = u.shape
    nblk, cq, nq = wbr.shape
    tt = _tile(n_rows, 512)
    nch = n_rows // tt

    def body(u_ref, dy_ref, sbr_r, sbi_r, wbr_r, wbi_r, wcr_r, wci_r, lbr_r, lbi_r, d_ref,
             du_ref, dwbr, dwbi, dwcr, dwci, dlbr, dlbi, dd_ref, s_re, s_im, g_re, g_im, lc_re, lc_im):
        @pl.when(pl.program_id(1) == 0)
        def _():
            for ref in (lc_re, lc_im, dwbr, dwbi, dwcr, dwci, dlbr, dlbi, dd_ref):
                ref[...] = jnp.zeros_like(ref)
        uf = u_ref[...]
        ub = uf.astype(BF16)
        dyf = dy_ref[...]
        dyb = dyf.astype(BF16)
        s_re[...] = _dot(ub, wbr_r[0], 1, 0)
        s_im[...] = _dot(ub, wbi_r[0], 1, 0)
        _scan_rows(s_re, s_im, lbr_r[...], lbi_r[...], sbr_r[0], sbi_r[0], reverse=False)
        sr16 = s_re[...].astype(BF16)
        si16 = s_im[...].astype(BF16)
        dwcr[0] += _dot(dyb, sr16, 0, 0)
        dwci[0] -= _dot(dyb, si16, 0, 0)
        g_re[...] = _dot(dyb, wcr_r[0], 1, 0)
        g_im[...] = -_dot(dyb, wci_r[0], 1, 0)
        lcr, lci = _scan_rows(g_re, g_im, lbr_r[...], -lbi_r[...], lc_re[...], lc_im[...], reverse=True)
        lc_re[...] = lcr
        lc_im[...] = lci
        lam_r = g_re[...]
        lam_i = g_im[...]
        first = lax.broadcasted_iota(jnp.int32, (tt, nq), 0) == 0
        prev_r = jnp.where(first, sbr_r[0], pltpu.roll(s_re[...], 1, 0))
        prev_i = jnp.where(first, sbi_r[0], pltpu.roll(s_im[...], 1, 0))
        dlbr[0] += jnp.sum(lam_r * prev_r + lam_i * prev_i, axis=0, keepdims=True)
        dlbi[0] += jnp.sum(lam_i * prev_r - lam_r * prev_i, axis=0, keepdims=True)
        lr16 = lam_r.astype(BF16)
        li16 = lam_i.astype(BF16)
        du_ref[...] = _dot(lr16, wbr_r[0], 1, 1) + _dot(li16, wbi_r[0], 1, 1) + d_ref[...] * dyf
        dwbr[0] += _dot(ub, lr16, 0, 0)
        dwbi[0] += _dot(ub, li16, 0, 0)
        dd_ref[0] += jnp.sum(dyf * uf, axis=0, keepdims=True)

    last = nch - 1
    wsp = pl.BlockSpec((1, cq, nq), lambda q, i: (q, 0, 0))
    vec = pl.BlockSpec((1, nq), lambda q, i: (0, q))
    act = pl.BlockSpec((tt, cq), lambda q, i: (last - i, q))
    sb = pl.BlockSpec((1, 1, nq), lambda q, i: (last - i, 0, q))
    vec3 = pl.BlockSpec((1, 1, nq), lambda q, i: (q, 0, 0))
    dsp = pl.BlockSpec((1, 1, cq), lambda q, i: (q, 0, 0))
    wsh = jax.ShapeDtypeStruct((nblk, cq, nq), F32)
    v3sh = jax.ShapeDtypeStruct((nblk, 1, nq), F32)
    big = pltpu.VMEM((tt, nq), F32)
    return pl.pallas_call(
        body, grid=(nblk, nch),
        in_specs=[act, act, sb, sb, wsp, wsp, wsp, wsp, vec, vec, pl.BlockSpec((1, cq), lambda q, i: (0, q))],
        out_specs=[act, wsp, wsp, wsp, wsp, vec3, vec3, dsp],
        out_shape=[jax.ShapeDtypeStruct(u.shape, F32), wsh, wsh, wsh, wsh, v3sh, v3sh,
                   jax.ShapeDtypeStruct((nblk, 1, cq), F32)],
        scratch_shapes=[big, big, big, big, pltpu.VMEM((1, nq), F32), pltpu.VMEM((1, nq), F32)],
        compiler_params=_cp(2), name=name)(u, dy, sb_re, sb_im, wbr, wbi, wcr, wci, lb_re, lb_im, dskip)


def _conv_taps(cur, prev, w, b):
    rid = lax.broadcasted_iota(jnp.int32, cur.shape, 0)
    x1 = jnp.where(rid == 0, prev[7:8, :], pltpu.roll(cur, 1, 0))
    x2 = jnp.where(rid == 0, prev[6:7, :], jnp.where(rid == 1, prev[7:8, :], pltpu.roll(cur, 2, 0)))
    return b + x2 * w[0:1, :] + x1 * w[1:2, :] + cur * w[2:3, :], x1, x2


def _conv_fwd(uu, cw, cb, *, name):
    n_rows, f2 = uu.shape
    f = f2 // 2
    tc = _tile(f, 1408, LANES)
    tl = _tile(n_rows, 256)
    nfb = f // tc

    def body(g_ref, u_ref, wg_ref, wu_ref, bg_ref, bu_ref, o_ref, pg, pu):
        @pl.when(pl.program_id(1) == 0)
        def _():
            pg[...] = jnp.zeros_like(pg)
            pu[...] = jnp.zeros_like(pu)
        gcur = g_ref[...]
        ucur = u_ref[...]
        cg, _, _ = _conv_taps(gcur, pg[...], wg_ref[...], bg_ref[...])
        cu, _, _ = _conv_taps(ucur, pu[...], wu_ref[...], bu_ref[...])
        o_ref[...] = (cg * _sigmoid(cg) * cu).astype(o_ref.dtype)
        pg[...] = gcur[tl - SUBLANES:, :]
        pu[...] = ucur[tl - SUBLANES:, :]

    return pl.pallas_call(
        body, grid=(nfb, n_rows // tl),
        in_specs=[pl.BlockSpec((tl, tc), lambda j, i: (i, j)), pl.BlockSpec((tl, tc), lambda j, i: (i, j + nfb)),
                  pl.BlockSpec((DT_CONV_TAPS, tc), lambda j, i: (0, j)),
                  pl.BlockSpec((DT_CONV_TAPS, tc), lambda j, i: (0, j + nfb)),
                  pl.BlockSpec((1, tc), lambda j, i: (0, j)), pl.BlockSpec((1, tc), lambda j, i: (0, j + nfb))],
        out_specs=pl.BlockSpec((tl, tc), lambda j, i: (i, j)),
        out_shape=jax.ShapeDtypeStruct((n_rows, f), BF16),
        scratch_shapes=[pltpu.VMEM((SUBLANES, tc), F32), pltpu.VMEM((SUBLANES, tc), F32)],
        compiler_params=_cp(2), name=name)(uu, uu, cw, cw, cb, cb)


def _conv_bwd(uu, dact, cw, cb, *, name):
    n_rows, f2 = uu.shape
    f = f2 // 2
    tc = _tile(f, 1408, LANES)
    tl = _tile(n_rows, 256)
    nfb = f // tc
    nrb = n_rows // tl
    halo_per_tile = tl // SUBLANES

    def body(g_ref, gh_ref, u_ref, uh_ref, da_ref, wg_ref, wu_ref, bg_ref, bu_ref,
             duu_ref, dw_ref, db_ref, nxt_g, nxt_u):
        i = pl.program_id(1)
        rb = nrb - 1 - i

        @pl.when(i == 0)
        def _():
            for ref in (nxt_g, nxt_u, dw_ref, db_ref):
                ref[...] = jnp.zeros_like(ref)
        has_prev = jnp.where(rb > 0, 1.0, 0.0)
        gcur, ucur = g_ref[...], u_ref[...]
        wg, wu = wg_ref[...], wu_ref[...]
        cg, g1, g2 = _conv_taps(gcur, gh_ref[...] * has_prev, wg, bg_ref[...])
        cu, u1, u2 = _conv_taps(ucur, uh_ref[...] * has_prev, wu, bu_ref[...])
        sg = _sigmoid(cg)
        silu = cg * sg
        da = da_ref[...]
        rid = lax.broadcasted_iota(jnp.int32, da.shape, 0)

        def transpose_conv(plane, d, cur, x1, x2, w, nxt):
            nx = nxt[...]
            d1 = jnp.where(rid == tl - 1, nx[0:1, :], pltpu.roll(d, tl - 1, 0))
            d2 = jnp.where(rid == tl - 2, nx[0:1, :],
                           jnp.where(rid == tl - 1, nx[1:2, :], pltpu.roll(d, tl - 2, 0)))
            duu_ref[plane] = (w[2:3, :] * d + w[1:2, :] * d1 + w[0:1, :] * d2).astype(duu_ref.dtype)
            nxt[...] = d[0:SUBLANES, :]
            dw_ref[plane] += jnp.concatenate([jnp.sum(d * x2, axis=0, keepdims=True),
                                              jnp.sum(d * x1, axis=0, keepdims=True),
                                              jnp.sum(d * cur, axis=0, keepdims=True)], axis=0)
            db_ref[plane] += jnp.sum(d, axis=0, keepdims=True)

        transpose_conv(0, da * cu * (sg * (1.0 + cg * (1.0 - sg))), gcur, g1, g2, wg, nxt_g)
        transpose_conv(1, da * silu, ucur, u1, u2, wu, nxt_u)

    def halo(j, i):
        return jnp.maximum((nrb - 1 - i) * halo_per_tile - 1, 0)

    return pl.pallas_call(
        body, grid=(nfb, nrb),
        in_specs=[pl.BlockSpec((tl, tc), lambda j, i: (nrb - 1 - i, j)),
                  pl.BlockSpec((SUBLANES, tc), lambda j, i: (halo(j, i), j)),
                  pl.BlockSpec((tl, tc), lambda j, i: (nrb - 1 - i, j + nfb)),
                  pl.BlockSpec((SUBLANES, tc), lambda j, i: (halo(j, i), j + nfb)),
                  pl.BlockSpec((tl, tc), lambda j, i: (nrb - 1 - i, j)),
                  pl.BlockSpec((DT_CONV_TAPS, tc), lambda j, i: (0, j)),
                  pl.BlockSpec((DT_CONV_TAPS, tc), lambda j, i: (0, j + nfb)),
                  pl.BlockSpec((1, tc), lambda j, i: (0, j)),
                  pl.BlockSpec((1, tc), lambda j, i: (0, j + nfb))],
        out_specs=[pl.BlockSpec((2, tl, tc), lambda j, i: (0, nrb - 1 - i, j)),
                   pl.BlockSpec((2, DT_CONV_TAPS, tc), lambda j, i: (0, 0, j)),
                   pl.BlockSpec((2, 1, tc), lambda j, i: (0, 0, j))],
        out_shape=[jax.ShapeDtypeStruct((2, n_rows, f), BF16), jax.ShapeDtypeStruct((2, DT_CONV_TAPS, f), F32),
                   jax.ShapeDtypeStruct((2, 1, f), F32)],
        scratch_shapes=[pltpu.VMEM((SUBLANES, tc), F32), pltpu.VMEM((SUBLANES, tc), F32)],
        compiler_params=_cp(2), name=name)(uu, uu, uu, uu, dact, cw, cw, cb, cb)


def _log_sigmoid(x):
    t = jnp.exp(-jnp.abs(x))
    log1p_t = jnp.where(t < 1e-3, t * (1.0 - t * (0.5 - t * (1.0 / 3.0))), jnp.log(1.0 + t))
    return jnp.minimum(x, 0.0) - log1p_t


def _dlog_sigmoid(x):
    t = jnp.exp(-jnp.abs(x))
    return jnp.where(x >= 0, t, 1.0) / (1.0 + t)


def _tri_dot(tri, x):
    return jnp.dot(tri, x, precision=lax.Precision.HIGHEST, preferred_element_type=F32)


def _cum_fwd(fl, bf, *, name):
    n_rows, width = fl.shape
    tc = _tile(n_rows, 256)

    def body(fl_ref, bf_ref, o_ref, carry):
        @pl.when(pl.program_id(0) == 0)
        def _():
            carry[...] = jnp.zeros_like(carry)
        x = _log_sigmoid(fl_ref[...] + bf_ref[...])
        r = lax.broadcasted_iota(jnp.int32, (tc, tc), 0)
        c = lax.broadcasted_iota(jnp.int32, (tc, tc), 1)
        y = _tri_dot(jnp.where(r >= c, 1.0, 0.0), x) + carry[...]
        o_ref[...] = y
        carry[...] = y[tc - 1:tc, :]

    return pl.pallas_call(
        body, grid=(n_rows // tc,),
        in_specs=[pl.BlockSpec((tc, width), lambda i: (i, 0)), pl.BlockSpec((1, width), lambda i: (0, 0))],
        out_specs=pl.BlockSpec((tc, width), lambda i: (i, 0)),
        out_shape=jax.ShapeDtypeStruct(fl.shape, F32),
        scratch_shapes=[pltpu.VMEM((1, width), F32)], compiler_params=_cp(1), name=name)(fl, bf)


def _cum_bwd(dcum, fl, bf, *, name):
    n_rows, width = fl.shape
    tc = _tile(n_rows, 256)
    last = n_rows // tc - 1

    def body(dc_ref, fl_ref, bf_ref, dfl_ref, dbf_ref, carry):
        @pl.when(pl.program_id(0) == 0)
        def _():
            carry[...] = jnp.zeros_like(carry)
            dbf_ref[...] = jnp.zeros_like(dbf_ref)
        r = lax.broadcasted_iota(jnp.int32, (tc, tc), 0)
        c = lax.broadcasted_iota(jnp.int32, (tc, tc), 1)
        dls = _tri_dot(jnp.where(r <= c, 1.0, 0.0), dc_ref[...]) + carry[...]
        carry[...] = dls[0:1, :]
        dfl = dls * _dlog_sigmoid(fl_ref[...] + bf_ref[...])
        dfl_ref[...] = dfl.astype(dfl_ref.dtype)
        dbf_ref[...] += jnp.sum(dfl, axis=0, keepdims=True)

    return pl.pallas_call(
        body, grid=(n_rows // tc,),
        in_specs=[pl.BlockSpec((tc, width), lambda i: (last - i, 0)),
                  pl.BlockSpec((tc, width), lambda i: (last - i, 0)),
                  pl.BlockSpec((1, width), lambda i: (0, 0))],
        out_specs=[pl.BlockSpec((tc, width), lambda i: (last - i, 0)), pl.BlockSpec((1, width), lambda i: (0, 0))],
        out_shape=[jax.ShapeDtypeStruct(fl.shape, BF16), jax.ShapeDtypeStruct((1, width), F32)],
        scratch_shapes=[pltpu.VMEM((1, width), F32)], compiler_params=_cp(1), name=name)(dcum, fl, bf)


def _head_masks():
    lane = lax.broadcasted_iota(jnp.int32, (1, LANES), 1)
    return (lane < HEAD_DIM, lane >= HEAD_DIM)


def _flash_fwd(q, kv, cq3, ck3, *, tq, name):
    n_rows, d = q.shape
    nhp = d // LANES
    tk = tq
    nq = n_rows // tq
    nk = n_rows // tk

    rt = _tile(tq, FLASH_ROW_TILE)

    def body(q_ref, k_ref, v_ref, cq_ref, ck_ref, o_ref, lse_ref, m0, m1, l0, l1, acc, s0, s1, p0, p1):
        i = pl.program_id(1)
        j = pl.program_id(2)
        ms, ls = (m0, m1), (l0, l1)

        @pl.when(j == 0)
        def _():
            for h in range(2):
                ms[h][...] = jnp.full_like(ms[h], -jnp.inf)
                ls[h][...] = jnp.zeros_like(ls[h])
            acc[...] = jnp.zeros_like(acc)

        def block(diagonal):
            qv, kk, vv = q_ref[...], k_ref[...], v_ref[...]
            a = acc[...]
            for h, msk in enumerate(_head_masks()):
                s_sc, p_sc = ((s0, p0), (s1, p1))[h]
                s_sc[...] = _dot(jnp.where(msk, qv, jnp.zeros_like(qv)), kk, 1, 1)
                bias = cq_ref[0, 0:1, h:h + 1] - ck_ref[0, h:h + 1, :]
                m_old, l_old = ms[h][...], ls[h][...]
                m_tiles, sum_tiles = [], []
                for r in range(tq // rt):
                    rows = slice(r * rt, (r + 1) * rt)
                    s = s_sc[rows, :] + bias
                    if diagonal:
                        rr = r * rt + lax.broadcasted_iota(jnp.int32, (rt, tk), 0)
                        cc = lax.broadcasted_iota(jnp.int32, (rt, tk), 1)
                        s = jnp.where(cc <= rr, s, -jnp.inf)
                    s_sc[rows, :] = s
                    m_tiles.append(jnp.maximum(m_old[rows, :], jnp.max(s, axis=1, keepdims=True)))
                for r in range(tq // rt):
                    rows = slice(r * rt, (r + 1) * rt)
                    p = jnp.exp(s_sc[rows, :] - jnp.tile(m_tiles[r], (1, tk // LANES)))
                    sum_tiles.append(jnp.sum(p, axis=1, keepdims=True))
                    p_sc[rows, :] = p.astype(BF16)
                m_new = jnp.concatenate(m_tiles, axis=0)
                alpha = jnp.exp(m_old - m_new)
                ms[h][...] = m_new
                ls[h][...] = alpha * l_old + jnp.concatenate(sum_tiles, axis=0)
                pv = _dot(p_sc[...], jnp.where(msk, vv, jnp.zeros_like(vv)), 1, 0)
                a = a * jnp.where(msk, alpha, 1.0) + pv
            acc[...] = a

        pl.when(j < i)(functools.partial(block, False))
        pl.when(j == i)(functools.partial(block, True))

        @pl.when(j == nk - 1)
        def _():
            m_a, m_b = _head_masks()
            o_ref[...] = acc[...] * jnp.where(m_a, 1.0 / l0[...], 1.0 / l1[...])
            two = lax.broadcasted_iota(jnp.int32, (tq, 2), 1)
            lse_a = (m0[...] + jnp.log(l0[...]))[:, 0:1]
            lse_b = (m1[...] + jnp.log(l1[...]))[:, 0:1]
            lse_ref[0] = jnp.where(two == 0, lse_a, lse_b)

    def kv_blk(i, j):
        return jnp.minimum(j, i)

    col = pltpu.VMEM((tq, LANES), F32)
    return pl.pallas_call(
        body, grid=(nhp, nq, nk),
        in_specs=[pl.BlockSpec((tq, LANES), lambda hp, i, j: (i, hp)),
                  pl.BlockSpec((tk, LANES), lambda hp, i, j: (kv_blk(i, j), hp)),
                  pl.BlockSpec((tk, LANES), lambda hp, i, j: (kv_blk(i, j), nhp + hp)),
                  pl.BlockSpec((1, tq, 2), lambda hp, i, j: (hp, i, 0)),
                  pl.BlockSpec((1, 2, tk), lambda hp, i, j: (hp, 0, kv_blk(i, j)))],
        out_specs=[pl.BlockSpec((tq, LANES), lambda hp, i, j: (i, hp)),
                   pl.BlockSpec((1, tq, 2), lambda hp, i, j: (hp, i, 0))],
        out_shape=[jax.ShapeDtypeStruct((n_rows, d), F32), jax.ShapeDtypeStruct((nhp, n_rows, 2), F32)],
        scratch_shapes=[col, col, col, col, pltpu.VMEM((tq, LANES), F32), pltpu.VMEM((tq, tk), F32),
                        pltpu.VMEM((tq, tk), F32), pltpu.VMEM((tq, tk), BF16), pltpu.VMEM((tq, tk), BF16)],
        compiler_params=_cp(3), name=name)(q, kv, kv, cq3, ck3)


def _flash_bwd(q, kv, o, lse, do, cq3, ck3, *, tq, name):
    n_rows, d = q.shape
    nhp = d // LANES
    tk = tq
    nq = n_rows // tq
    nk = n_rows // tk

    rt = _tile(tq, FLASH_ROW_TILE)

    def body(q_ref, k_ref, v_ref, o_ref, lse_ref, do_ref, cq_ref, ck_ref,
             dq_ref, dk_ref, dv_ref, dck_ref, dcq_ref, s0, s1, dp0, dp1, p0, p1, ds0, ds1, dl0, dl1, lb0, lb1):
        j = pl.program_id(1)
        i = pl.program_id(2)

        @pl.when(jnp.logical_and(j == 0, i == 0))
        def _():
            dq_ref[...] = jnp.zeros_like(dq_ref)
            dcq_ref[...] = jnp.zeros_like(dcq_ref)

        @pl.when(i == 0)
        def _():
            dk_ref[...] = jnp.zeros_like(dk_ref)
            dv_ref[...] = jnp.zeros_like(dv_ref)
            dck_ref[...] = jnp.zeros_like(dck_ref)

        def block(diagonal):
            qv, kk, vv = q_ref[...], k_ref[...], v_ref[...]
            dob = do_ref[...].astype(BF16)
            prod = dob.astype(F32) * o_ref[...]
            off = pl.multiple_of(i * tq, tq)
            lse_all = lse_ref[0]
            dck_rows, row_sums = [], []
            dq_acc = jnp.zeros((tq, LANES), F32)
            dk_acc = jnp.zeros((tk, LANES), F32)
            dv_acc = jnp.zeros((tk, LANES), F32)
            for h, msk in enumerate(_head_masks()):
                s_sc, dp_sc, p_sc, ds_sc = ((s0, dp0, p0, ds0), (s1, dp1, p1, ds1))[h]
                qh = jnp.where(msk, qv, jnp.zeros_like(qv))
                kh = jnp.where(msk, kk, jnp.zeros_like(kk))
                doh = jnp.where(msk, dob, jnp.zeros_like(dob))
                s_sc[...] = _dot(qh, kk, 1, 1)
                dp_sc[...] = _dot(doh, vv, 1, 1)
                bias = cq_ref[0, 0:1, h:h + 1] - ck_ref[0, h:h + 1, :]
                delta, lse_h = ((dl0, lb0), (dl1, lb1))[h]
                delta[...] = jnp.broadcast_to(jnp.sum(jnp.where(msk, prod, 0.0), axis=1, keepdims=True),
                                              (tq, LANES))
                lse_h[...] = jnp.broadcast_to(lse_all[:, h:h + 1], (tq, LANES))
                reps = (1, tk // LANES)
                col_acc = jnp.zeros((SUBLANES, tk), F32)
                rs = []
                for r in range(tq // rt):
                    rows = slice(r * rt, (r + 1) * rt)
                    s = s_sc[rows, :] + bias
                    if diagonal:
                        rr = r * rt + lax.broadcasted_iota(jnp.int32, (rt, tk), 0)
                        cc = lax.broadcasted_iota(jnp.int32, (rt, tk), 1)
                        s = jnp.where(cc <= rr, s, -jnp.inf)
                    p = jnp.exp(s - jnp.tile(lse_h[rows, :], reps))
                    ds = p * (dp_sc[rows, :] - jnp.tile(delta[rows, :], reps))
                    rs.append(jnp.sum(ds, axis=1, keepdims=True))
                    for g in range(rt // SUBLANES):
                        col_acc = col_acc + ds[g * SUBLANES:(g + 1) * SUBLANES, :]
                    p_sc[rows, :] = p.astype(BF16)
                    ds_sc[rows, :] = ds.astype(BF16)
                dck_rows.append(-jnp.sum(col_acc, axis=0, keepdims=True))
                row_sums.append(jnp.concatenate(rs, axis=0))
                dv_acc = dv_acc + _dot(p_sc[...], doh, 0, 0)
                dsb = ds_sc[...]
                dq_acc = dq_acc + _dot(dsb, kh, 1, 0)
                dk_acc = dk_acc + _dot(dsb, qh, 0, 0)
            dq_ref[pl.ds(off, tq), :] += dq_acc
            dk_ref[...] += dk_acc
            dv_ref[...] += dv_acc
            two = lax.broadcasted_iota(jnp.int32, (tq, 2), 1)
            dcq_ref[0, pl.ds(off, tq), :] += jnp.where(two == 0, row_sums[0], row_sums[1])
            dck_ref[0] += jnp.concatenate(dck_rows, axis=0)

        pl.when(i > j)(functools.partial(block, False))
        pl.when(i == j)(functools.partial(block, True))

    def q_blk(j, i):
        return jnp.maximum(i, j)

    score = pltpu.VMEM((tq, tk), F32)
    score16 = pltpu.VMEM((tq, tk), BF16)
    rowstat = pltpu.VMEM((tq, LANES), F32)
    return pl.pallas_call(
        body, grid=(nhp, nk, nq),
        in_specs=[pl.BlockSpec((tq, LANES), lambda hp, j, i: (q_blk(j, i), hp)),
                  pl.BlockSpec((tk, LANES), lambda hp, j, i: (j, hp)),
                  pl.BlockSpec((tk, LANES), lambda hp, j, i: (j, nhp + hp)),
                  pl.BlockSpec((tq, LANES), lambda hp, j, i: (q_blk(j, i), hp)),
                  pl.BlockSpec((1, tq, 2), lambda hp, j, i: (hp, q_blk(j, i), 0)),
                  pl.BlockSpec((tq, LANES), lambda hp, j, i: (q_blk(j, i), hp)),
                  pl.BlockSpec((1, tq, 2), lambda hp, j, i: (hp, q_blk(j, i), 0)),
                  pl.BlockSpec((1, 2, tk), lambda hp, j, i: (hp, 0, j))],
        out_specs=[pl.BlockSpec((n_rows, LANES), lambda hp, j, i: (0, hp)),
                   pl.BlockSpec((tk, LANES), lambda hp, j, i: (j, hp)),
                   pl.BlockSpec((tk, LANES), lambda hp, j, i: (j, hp)),
                   pl.BlockSpec((1, 2, tk), lambda hp, j, i: (hp, 0, j)),
                   pl.BlockSpec((1, n_rows, 2), lambda hp, j, i: (hp, 0, 0))],
        out_shape=[jax.ShapeDtypeStruct((n_rows, d), F32), jax.ShapeDtypeStruct((n_rows, d), F32),
                   jax.ShapeDtypeStruct((n_rows, d), F32), jax.ShapeDtypeStruct((nhp, 2, n_rows), F32),
                   jax.ShapeDtypeStruct((nhp, n_rows, 2), F32)],
        scratch_shapes=[score, score, score, score, score16, score16, score16, score16, rowstat, rowstat,
                        rowstat, rowstat],
        compiler_params=_cp(3), name=name)(q, kv, kv, o, lse, do, cq3, ck3)


def _s5_tables(w, layer):
    g, p = w["lam_re"].shape[1:]
    h = w["ssm_b_re"].shape[3]
    n = g * p
    lr = w["lam_re"][layer].reshape(1, n)
    li = w["lam_im"][layer].reshape(1, n)
    ldt = jnp.broadcast_to(w["log_dt"][layer][:, None], (g, p)).reshape(1, n)
    br = w["ssm_b_re"][layer].transpose(2, 0, 1).reshape(h, n)
    bi = w["ssm_b_im"][layer].transpose(2, 0, 1).reshape(h, n)
    cr = w["ssm_c_re"][layer].transpose(1, 0, 2).reshape(h, n)
    ci = w["ssm_c_im"][layer].transpose(1, 0, 2).reshape(h, n)
    return (lr, li, ldt, br, bi, cr, ci), (g, p, h)


def _local_step(x, tgt, w, *, attn_tile=512):
    n_rows, d = x.shape
    n_layers = w["g_mix"].shape[0]
    n_s5 = w["lam_re"].shape[0]
    nh = w["b_f"].shape[0]
    nhp = nh // 2
    assert d == nh * HEAD_DIM
    tq = _tile(n_rows, attn_tile)
    in_slots = w["w_in"].shape[0]
    glu_slots = w["w_glu"].shape[0]
    g = {}
    saved = [dict() for _ in range(n_layers)]

    h = x
    nxt = _rowwise(lambda a, gg: _rms(a, gg), [x], [_row2(w["g_mix"][0])], [(d, F32)], name="rms_first")[0]
    kvb = fl = cum = cq3 = ck3 = hnkv = None
    bf_pad = jnp.zeros((1, LANES), F32).at[0, :nh].set(w["b_f"])
    for l in range(n_layers):
        sv = saved[l]
        sv["h"] = h
        g_ffn = _row2(w["g_ffn"][l])
        if l < n_s5:
            tabs, (_, p, _) = _s5_tables(w, l)
            prep = _s5_prep(*tabs, p, name=f"s5_prep{l}")
            dskip = w["ssm_d"][l].reshape(1, d)
            y, sb_re, sb_im = _s5_fwd(nxt, prep, dskip, name=f"s5_fwd{l}")
            z = _rowwise(_gelu, [y], [], [(d, BF16)], name=f"gelu{l}")[0]
            zz = _mm_cols(z, w["w_glu"], l, wc=0, name=f"glu_mm{l}")
            h1, hn2 = _rowwise(lambda hh, zq, gg: ((lambda t: (t, _rms(t, gg)))(hh + _glu(zq))),
                               [h, zz], [g_ffn], [(d, F32), (d, BF16)], name=f"mix_out{l}")
            sv.update(u=nxt, prep=prep, tabs=tabs, p=p, dskip=dskip, sb_re=sb_re, sb_im=sb_im, y=y, z=z, zz=zz)
        else:
            j = l - n_s5
            qs = _mm_cols(nxt, w["w_q"], j, wc=0, out_dtype=BF16, scale=HEAD_DIM ** -0.5, name=f"q_mm{j}")
            o, lse = _flash_fwd(qs, kvb, cq3, ck3, tq=tq, name=f"flash_fwd{j}")
            a = _mm_cols(o, w["w_o"], j, wc=0, name=f"o_mm{j}")
            h1, hn2 = _rowwise(lambda hh, aa, gg: ((lambda t: (t, _rms(t, gg)))(hh + aa)),
                               [h, a], [g_ffn], [(d, F32), (d, BF16)], name=f"mix_out{l}")
            sv.update(hn=nxt, qs=qs, o=o, lse=lse)
        uu = _mm_cols(hn2, w["w_in"], l, wc=0, name=f"ffn_in{l}")
        cw, cb = w["conv_w"][l], _row2(w["conv_b"][l])
        act = _conv_fwd(uu, cw, cb, name=f"conv_fwd{l}")
        f = _mm_cols(act, w["w_out"], l, wc=0, name=f"ffn_out{l}")
        sv.update(h1=h1, hn2=hn2, uu=uu, act=act, cw=cw, cb=cb)
        if l == n_layers - 1:
            def loss_fn(hh, ff, tt, gg):
                yv, vjp = jax.vjp(_rms, hh + ff, gg)
                err = yv - tt
                part = 0.5 * jnp.sum(jnp.mean(err * err, axis=-1, keepdims=True), axis=0, keepdims=True)
                dh, dg = vjp(err * (1.0 / d))
                return dh, jnp.broadcast_to(part, (1, LANES)), dg
            dcur, loss_row, dgf = _rowwise(loss_fn, [h1, f, tgt], [_row2(w["g_final"])], [(d, F32)],
                                           [(1, LANES), (1, d)], name="loss")
            loss = loss_row[0, 0]
            g["g_final"] = dgf[0]
        elif l + 1 < n_s5:
            h, nxt = _rowwise(lambda hh, ff, gg: ((lambda t: (t, _rms(t, gg)))(hh + ff)), [h1, f],
                              [_row2(w["g_mix"][l + 1])], [(d, F32), (d, F32)], name=f"ffn_res{l}")
        elif l + 1 == n_s5:
            h, nxt, hnkv = _rowwise(
                lambda hh, ff, g1, g2: ((lambda t: (t, _rms(t, g1), _rms(t, g2)))(hh + ff)), [h1, f],
                [_row2(w["g_mix"][l + 1]), _row2(w["g_kv"])], [(d, F32), (d, BF16), (d, BF16)], name=f"ffn_res{l}")
            kvb = _mm_cols(hnkv, w["w_kv"], 0, wc=0, out_dtype=BF16, name="kv_mm")
            fl = _mm_cols(hnkv, w["w_f"], 0, wc=0, name="f_mm")
            cum = _cum_fwd(fl, bf_pad, name="cum_fwd")
            cq3 = cum[:, :nh].reshape(n_rows, nhp, 2).transpose(1, 0, 2)
            ck3 = cum[:, :nh].T.reshape(nhp, 2, n_rows)
        else:
            h, nxt = _rowwise(lambda hh, ff, gg: ((lambda t: (t, _rms(t, gg)))(hh + ff)), [h1, f],
                              [_row2(w["g_mix"][l + 1])], [(d, F32), (d, BF16)], name=f"ffn_res{l}")

    per_layer = {k: [None] * n_layers for k in ("g_mix", "g_ffn", "w_in", "w_out", "conv_w", "conv_b")}
    per_s5 = {k: [None] * n_s5 for k in ("lam_re", "lam_im", "log_dt", "ssm_b_re", "ssm_b_im", "ssm_c_re",
                                         "ssm_c_im", "ssm_d", "w_glu")}
    per_fox = {k: [None] * (n_layers - n_s5) for k in ("w_q", "w_o")}
    dk_parts, dv_parts, dck_parts = [], [], []
    for l in reversed(range(n_layers)):
        sv = saved[l]
        dact = _mm_cols(dcur, w["w_out"], l, wc=1, name=f"ffn_out_dx{l}")
        per_layer["w_out"][l] = _mm_tn(sv["act"], dcur, 1, name=f"ffn_out_dw{l}")[0]
        duu, dcw, dcb = _conv_bwd(sv["uu"], dact, sv["cw"], sv["cb"], name=f"conv_bwd{l}")
        per_layer["conv_w"][l] = jnp.concatenate([dcw[0], dcw[1]], axis=-1)
        per_layer["conv_b"][l] = jnp.concatenate([dcb[0, 0], dcb[1, 0]])
        dhn2 = _mm_acc(duu, w["w_in"], l, wc=1, name=f"ffn_in_dx{l}")
        per_layer["w_in"][l] = _mm_tn(sv["hn2"], duu, in_slots, name=f"ffn_in_dw{l}")
        d1, (dg,) = _node_bwd(dcur, sv["h1"], [(w["g_ffn"][l], [dhn2])], name=f"ffn_norm_bwd{l}")
        per_layer["g_ffn"][l] = dg
        if l < n_s5:
            def glu_bwd(zq, dd):
                _, vjp = jax.vjp(_glu, zq)
                return vjp(dd)[0]
            dzz = _rowwise(glu_bwd, [sv["zz"], d1], [], [(2 * d, BF16)], name=f"glu_bwd{l}")[0]
            dz = _mm_acc(dzz, w["w_glu"], l, wc=1, name=f"glu_dx{l}")
            per_s5["w_glu"][l] = _mm_tn(sv["z"], dzz, glu_slots, name=f"glu_dw{l}")

            def gelu_bwd(yy, dd):
                _, vjp = jax.vjp(_gelu, yy)
                return vjp(dd)[0]
            dy = _rowwise(gelu_bwd, [sv["y"], dz], [], [(d, F32)], name=f"gelu_bwd{l}")[0]
            du, dwbr, dwbi, dwcr, dwci, dlbr, dlbi, dd = _s5_bwd(sv["u"], dy, sv["sb_re"], sv["sb_im"], sv["prep"],
                                                                 sv["dskip"], name=f"s5_bwd{l}")
            dlr, dli, dldt, dbr, dbi, dcr, dci = _s5_prep_bwd(*sv["tabs"], sv["p"], dlbr, dlbi, dwbr, dwbi, dwcr,
                                                              dwci, name=f"s5_prep_bwd{l}")
            gg, p = w["lam_re"].shape[1:]
            hh = w["ssm_b_re"].shape[3]
            per_s5["lam_re"][l] = dlr.reshape(gg, p)
            per_s5["lam_im"][l] = dli.reshape(gg, p)
            per_s5["log_dt"][l] = dldt.reshape(gg, p).sum(axis=1)
            per_s5["ssm_b_re"][l] = dbr.reshape(hh, gg, p).transpose(1, 2, 0)
            per_s5["ssm_b_im"][l] = dbi.reshape(hh, gg, p).transpose(1, 2, 0)
            per_s5["ssm_c_re"][l] = dcr.reshape(hh, gg, p).transpose(1, 0, 2)
            per_s5["ssm_c_im"][l] = dci.reshape(hh, gg, p).transpose(1, 0, 2)
            per_s5["ssm_d"][l] = dd.reshape(d)
            branches = [(w["g_mix"][l], [du])]
        else:
            j = l - n_s5
            do = _mm_cols(d1, w["w_o"], j, wc=1, name=f"o_dx{j}")
            per_fox["w_o"][j] = _mm_tn(sv["o"], d1, 1, name=f"o_dw{j}")[0]
            dq, dk, dv, dck, dcq = _flash_bwd(sv["qs"], kvb, sv["o"], sv["lse"], do, cq3, ck3, tq=tq,
                                              name=f"flash_bwd{j}")
            dk_parts.append(dk)
            dv_parts.append(dv)
            dck_parts.append(dck.reshape(nh, n_rows).T + dcq.transpose(1, 0, 2).reshape(n_rows, nh))
            scale = HEAD_DIM ** -0.5
            dhn = _mm_cols(dq, w["w_q"], j, wc=1, scale=scale, name=f"q_dx{j}")
            per_fox["w_q"][j] = _mm_tn(sv["hn"], dq, 1, scale=scale, name=f"q_dw{j}")[0]
            branches = [(w["g_mix"][l], [dhn])]
            if j == 0:
                def kv_sum(*parts):
                    half = len(parts) // 2
                    return jnp.concatenate([sum(parts[:half][1:], parts[0]),
                                            sum(parts[half:][1:], parts[half])], axis=1)
                dkv = _rowwise(kv_sum, dk_parts + dv_parts, [], [(2 * d, BF16)], name="dkv_sum")[0]
                dck_tot = dck_parts[0]
                for extra in dck_parts[1:]:
                    dck_tot = dck_tot + extra
                dcum = jnp.zeros((n_rows, LANES), F32).at[:, :nh].set(dck_tot)
                dfl, dbf = _cum_bwd(dcum, fl, bf_pad, name="cum_bwd")
                g["b_f"] = dbf[0, :nh]
                dhkv_a = _mm_cols(dkv, w["w_kv"], 0, wc=1, name="kv_dx")
                dhkv_b = _mm_cols(dfl, w["w_f"], 0, wc=1, name="f_dx")
                g["w_kv"] = _mm_tn(hnkv, dkv, 1, name="kv_dw")[0]
                g["w_f"] = _mm_tn(hnkv, dfl, 1, name="f_dw")[0]
                branches.append((w["g_kv"], [dhkv_a, dhkv_b]))
        dcur, dgs = _node_bwd(d1, sv["h"], branches, name=f"mix_norm_bwd{l}")
        per_layer["g_mix"][l] = dgs[0]
        if len(dgs) > 1:
            g["g_kv"] = dgs[1]

    def by_row_shard(mats):
        return jnp.stack([m.reshape(N_CHIPS, m.shape[0] // N_CHIPS, m.shape[1]) for m in mats], axis=1)

    for k, v in per_layer.items():
        g[k] = jnp.stack(v, axis=1) if k == "w_in" else by_row_shard(v) if k == "w_out" else jnp.stack(v)
    for k, v in per_s5.items():
        g[k] = jnp.stack(v, axis=1) if k == "w_glu" else jnp.stack(v)
    for k, v in per_fox.items():
        g[k] = by_row_shard(v)
    return loss, dcur, g


def _position():
    x, y, c = lax.axis_index("x"), lax.axis_index("y"), lax.axis_index("c")
    chips = [(1 - x, y), (x, 1 - y), (1 - x, 1 - y)]
    return x, y, c, chips


def _any_specs(n):
    return [pl.BlockSpec(memory_space=pl.ANY)] * n


def _all_gather(shards, kinds, *, name):
    n = len(shards)

    def out_shape(a, kind):
        shape = (N_CHIPS,) + a.shape if kind == "C" else (a.shape[0], N_CHIPS) + a.shape[1:]
        return jax.ShapeDtypeStruct(shape, a.dtype)

    def body(*refs):
        ins, outs = refs[:n], refs[n:2 * n]
        send_sems, recv_sems = refs[2 * n:]
        x, y, c, chips = _position()
        my_slot = 2 * x + y
        sibling = (x, y, 1 - c)

        def rows(t, half):
            hr = ins[t].shape[0] // 2
            return pl.ds(half * hr, hr)

        def piece(t, slot, half):
            return outs[t].at[slot, rows(t, half)] if kinds[t] == "C" else outs[t].at[rows(t, half), slot]

        def whole(t, slot):
            full = pl.ds(0, ins[t].shape[0])
            return outs[t].at[slot, full] if kinds[t] == "C" else outs[t].at[full, slot]

        def remote(k, t, src, dst, to):
            return pltpu.make_async_remote_copy(src_ref=src, dst_ref=dst, send_sem=send_sems.at[k, t],
                                                recv_sem=recv_sems.at[k, t], device_id=to, device_id_type=MESH)

        started = []
        for t in range(n):
            cp = remote(6, t, ins[t], whole(t, my_slot), sibling)
            cp.start()
            started.append(cp)
        for j, chip in enumerate(chips):
            for t in range(n):
                cp = remote(j, t, ins[t].at[rows(t, c)], piece(t, my_slot, c), (*chip, c))
                cp.start()
                started.append(cp)
        for j, chip in enumerate(chips):
            slot = 2 * chip[0] + chip[1]
            for t in range(n):
                remote(j, t, piece(t, slot, c), piece(t, slot, c), (*chip, c)).wait_recv()
            for t in range(n):
                cp = remote(3 + j, t, piece(t, slot, c), piece(t, slot, c), sibling)
                cp.start()
                started.append(cp)
        for j, chip in enumerate(chips):
            slot = 2 * chip[0] + chip[1]
            for t in range(n):
                remote(3 + j, t, piece(t, slot, 1 - c), piece(t, slot, 1 - c), sibling).wait_recv()
        for t in range(n):
            remote(6, t, ins[t], whole(t, my_slot), sibling).wait_recv()
        for cp in started:
            cp.wait_send()

    return pl.pallas_call(
        body, in_specs=_any_specs(n), out_specs=_any_specs(n),
        out_shape=[out_shape(a, k) for a, k in zip(shards, kinds)],
        scratch_shapes=[pltpu.SemaphoreType.DMA((7, n)), pltpu.SemaphoreType.DMA((7, n))],
        name=name)(*shards)


def _pair_exchange(grads, *, name):
    n = len(grads)

    def body(*refs):
        ins, outs = refs[:n], refs[n:2 * n]
        send_sems, recv_sems = refs[2 * n:]
        x, y, c, _ = _position()
        copies = [pltpu.make_async_remote_copy(src_ref=ins[t].at[:, 1 - c], dst_ref=outs[t],
                                               send_sem=send_sems.at[t], recv_sem=recv_sems.at[t],
                                               device_id=(x, y, 1 - c), device_id_type=MESH) for t in range(n)]
        for cp in copies:
            cp.start()
        for cp in copies:
            cp.wait()

    return pl.pallas_call(
        body, in_specs=_any_specs(n), out_specs=_any_specs(n),
        out_shape=[jax.ShapeDtypeStruct((a.shape[0],) + a.shape[2:], a.dtype) for a in grads],
        scratch_shapes=[pltpu.SemaphoreType.DMA((n,)), pltpu.SemaphoreType.DMA((n,))], name=name)(*grads)


def _chip_exchange(parts, *, name):
    n = len(parts)

    def body(*refs):
        ins, outs = refs[:n], refs[n:2 * n]
        send_sems, recv_sems = refs[2 * n:]
        _, _, c, chips = _position()
        copies = []
        for j, chip in enumerate(chips):
            slot = 2 * chip[0] + chip[1]
            for t in range(n):
                cp = pltpu.make_async_remote_copy(src_ref=ins[t].at[slot], dst_ref=outs[t].at[j],
                                                  send_sem=send_sems.at[j, t], recv_sem=recv_sems.at[j, t],
                                                  device_id=(*chip, c), device_id_type=MESH)
                cp.start()
                copies.append(cp)
        for cp in copies:
            cp.wait()

    return pl.pallas_call(
        body, in_specs=_any_specs(n), out_specs=_any_specs(n),
        out_shape=[jax.ShapeDtypeStruct((N_CHIPS - 1,) + a.shape[1:], a.dtype) for a in parts],
        scratch_shapes=[pltpu.SemaphoreType.DMA((N_CHIPS - 1, n)), pltpu.SemaphoreType.DMA((N_CHIPS - 1, n))],
        name=name)(*parts)


def _pair_share(both, *, name):
    n = len(both)

    def body(*refs):
        ins, outs = refs[:n], refs[n:2 * n]
        send_sems, recv_sems = refs[2 * n:]
        x, y, c, _ = _position()
        for t in range(n):
            pltpu.make_async_remote_copy(src_ref=ins[t].at[c], dst_ref=outs[t].at[c], send_sem=send_sems.at[t],
                                         recv_sem=recv_sems.at[t], device_id=(x, y, 1 - c),
                                         device_id_type=MESH).start()
        for t in range(n):
            pltpu.make_async_remote_copy(src_ref=ins[t].at[c], dst_ref=outs[t].at[1 - c], send_sem=send_sems.at[t],
                                         recv_sem=recv_sems.at[t], device_id=(x, y, 1 - c),
                                         device_id_type=MESH).wait()

    return pl.pallas_call(
        body, in_specs=_any_specs(n), out_specs=_any_specs(n),
        out_shape=[jax.ShapeDtypeStruct(a.shape, a.dtype) for a in both],
        input_output_aliases={t: t for t in range(n)},
        scratch_shapes=[pltpu.SemaphoreType.DMA((n,)), pltpu.SemaphoreType.DMA((n,))], name=name)(*both)


def _sum_pair(grad, landed, c, wire_dtype, *, name):
    slots, _, m, n = grad.shape
    tm = _tile(m, 256, 2 * SUBLANES)

    def body(c_ref, g_ref, l_ref, o_ref):
        o_ref[...] = (g_ref[0] + l_ref[...]).astype(wire_dtype)

    return pl.pallas_call(
        body,
        grid_spec=pltpu.PrefetchScalarGridSpec(
            num_scalar_prefetch=1, grid=(slots, m // tm),
            in_specs=[pl.BlockSpec((1, 1, tm, n), lambda s, i, c_ref: (s, c_ref[0], i, 0)),
                      pl.BlockSpec((1, tm, n), lambda s, i, c_ref: (s, i, 0))],
            out_specs=pl.BlockSpec((1, tm, n), lambda s, i, c_ref: (s, i, 0))),
        out_shape=jax.ShapeDtypeStruct((slots, m, n), wire_dtype), compiler_params=_cp(2), name=name)(
            c, grad, landed)


def _sum_chips(part, landed, slot_c, *, name):
    _, m, n = part.shape
    tm = _tile(m, 256, 2 * SUBLANES)

    def body(s_ref, p_ref, l_ref, o_ref):
        acc = p_ref[0].astype(F32)
        for j in range(N_CHIPS - 1):
            acc = acc + l_ref[j].astype(F32)
        o_ref[0] = acc

    return pl.pallas_call(
        body,
        grid_spec=pltpu.PrefetchScalarGridSpec(
            num_scalar_prefetch=1, grid=(m // tm,),
            in_specs=[pl.BlockSpec((1, tm, n), lambda i, s_ref: (s_ref[0], i, 0)),
                      pl.BlockSpec((N_CHIPS - 1, tm, n), lambda i, s_ref: (0, i, 0))],
            out_specs=pl.BlockSpec((1, tm, n), lambda i, s_ref: (s_ref[1], i, 0))),
        out_shape=jax.ShapeDtypeStruct((N_CORES, m, n), F32), compiler_params=_cp(1), name=name)(
            slot_c, part, landed)


def _reduce_scatter(grads, wire_dtypes):
    c = lax.axis_index("c").reshape(1).astype(jnp.int32)
    slot_c = jnp.stack([2 * lax.axis_index("x") + lax.axis_index("y"), lax.axis_index("c")]).astype(jnp.int32)
    views = []
    for a in grads:
        lead, last = a.shape[1], a.shape[-1]
        mid = 1
        for s in a.shape[2:-1]:
            mid *= s
        views.append(a.reshape(N_CHIPS, N_CORES, (lead // N_CORES) * mid, last))
    landed = _pair_exchange(views, name="rs_pair_exchange")
    parts = [_sum_pair(v, l, c, wire_dtypes[t], name=f"rs_pair_sum{t}")
             for t, (v, l) in enumerate(zip(views, landed))]
    landed = _chip_exchange(parts, name="rs_chip_exchange")
    both = [_sum_chips(p, l, slot_c, name=f"rs_chip_sum{t}") for t, (p, l) in enumerate(zip(parts, landed))]
    full = _pair_share(both, name="rs_pair_share")
    return [f.reshape(a.shape[1:]) for f, a in zip(full, grads)]


def _adamw(w, g, m, v, *, name):
    def fn(ww, gg, mm, vv):
        mm = ADAM_B1 * mm + (1.0 - ADAM_B1) * gg
        vv = ADAM_B2 * vv + (1.0 - ADAM_B2) * (gg * gg)
        m_hat = mm / (1.0 - ADAM_B1 ** ADAM_STEP)
        v_hat = vv / (1.0 - ADAM_B2 ** ADAM_STEP)
        delta = -ADAM_LR * (m_hat / (jnp.sqrt(v_hat) + ADAM_EPS) + ADAM_WD * ww)
        return delta, mm, vv

    shape = w.shape
    two_d = [a.reshape(-1, shape[-1]) for a in (w, g, m, v)]
    outs = _rowwise(fn, two_d, [], [(shape[-1], F32)] * 3, name=name)
    return [o.reshape(shape) for o in outs]


def _to_bf16(a, *, name):
    two_d = a.reshape(-1, a.shape[-1])
    return _rowwise(lambda t: t, [two_d], [], [(a.shape[-1], BF16)], name=name)[0].reshape(a.shape)


def _pack(arrays, rows_multiple):
    flat = jnp.concatenate([a.reshape(-1) for a in arrays])
    rows = -(-flat.shape[0] // LANES)
    rows = -(-rows // rows_multiple) * rows_multiple
    return jnp.pad(flat, (0, rows * LANES - flat.shape[0])).reshape(rows, LANES)


def _unpack(packed, like):
    flat = packed.reshape(-1)
    out, pos = [], 0
    for a in like:
        out.append(flat[pos:pos + a.size].reshape(a.shape))
        pos += a.size
    return out


_PARAMS = ("g_mix", "g_ffn", "lam_re", "lam_im", "log_dt", "ssm_b_re", "ssm_b_im", "ssm_c_re", "ssm_c_im", "ssm_d",
           "w_glu", "g_kv", "w_kvf", "b_f", "w_q", "w_o", "w_ffn_in", "ffn_conv_w", "ffn_conv_b", "w_ffn_out",
           "g_final")
_BIG = ("w_glu", "w_kvf", "w_q", "w_o", "w_ffn_in", "w_ffn_out")
_SMALL_SHARDED = ("ssm_d", "ffn_conv_w")


def kernel(x, g_mix, g_ffn, lam_re, lam_im, log_dt, ssm_b_re, ssm_b_im, ssm_c_re, ssm_c_im, ssm_d, w_glu, g_kv, w_kvf, b_f, w_q, w_o, w_ffn_in, ffn_conv_w, ffn_conv_b, w_ffn_out, g_final, loss_target, m_g_mix, m_g_ffn, m_lam_re, m_lam_im, m_log_dt, m_ssm_b_re, m_ssm_b_im, m_ssm_c_re, m_ssm_c_im, m_ssm_d, m_w_glu, m_g_kv, m_w_kvf, m_b_f, m_w_q, m_w_o, m_w_ffn_in, m_ffn_conv_w, m_ffn_conv_b, m_w_ffn_out, m_g_final, v_g_mix, v_g_ffn, v_lam_re, v_lam_im, v_log_dt, v_ssm_b_re, v_ssm_b_im, v_ssm_c_re, v_ssm_c_im, v_ssm_d, v_w_glu, v_g_kv, v_w_kvf, v_b_f, v_w_q, v_w_o, v_w_ffn_in, v_ffn_conv_w, v_ffn_conv_b, v_w_ffn_out, v_g_final):
    p = dict(g_mix=g_mix, g_ffn=g_ffn, lam_re=lam_re, lam_im=lam_im, log_dt=log_dt, ssm_b_re=ssm_b_re,
             ssm_b_im=ssm_b_im, ssm_c_re=ssm_c_re, ssm_c_im=ssm_c_im, ssm_d=ssm_d, w_glu=w_glu, g_kv=g_kv,
             w_kvf=w_kvf, b_f=b_f, w_q=w_q, w_o=w_o, w_ffn_in=w_ffn_in, ffn_conv_w=ffn_conv_w,
             ffn_conv_b=ffn_conv_b, w_ffn_out=w_ffn_out, g_final=g_final)
    mom1 = dict(zip(_PARAMS, (m_g_mix, m_g_ffn, m_lam_re, m_lam_im, m_log_dt, m_ssm_b_re, m_ssm_b_im, m_ssm_c_re,
                              m_ssm_c_im, m_ssm_d, m_w_glu, m_g_kv, m_w_kvf, m_b_f, m_w_q, m_w_o, m_w_ffn_in,
                              m_ffn_conv_w, m_ffn_conv_b, m_w_ffn_out, m_g_final)))
    mom2 = dict(zip(_PARAMS, (v_g_mix, v_g_ffn, v_lam_re, v_lam_im, v_log_dt, v_ssm_b_re, v_ssm_b_im, v_ssm_c_re,
                              v_ssm_c_im, v_ssm_d, v_w_glu, v_g_kv, v_w_kvf, v_b_f, v_w_q, v_w_o, v_w_ffn_in,
                              v_ffn_conv_w, v_ffn_conv_b, v_w_ffn_out, v_g_final)))
    d = x.shape[-1]
    nh = b_f.shape[0]
    slot = 2 * lax.axis_index("x") + lax.axis_index("y")

    shards = [_to_bf16(p[k], name=f"to_bf16_{k}") for k in _BIG] + [p[k] for k in _SMALL_SHARDED]
    kinds = ["C", "C", "R", "R", "C", "R", "C", "C"]
    gl, gkvf, gq, go, gin, gout, gd, gcw = _all_gather(shards, kinds, name="weights_all_gather")
    n_lay = gin.shape[1]
    kvf_full = gkvf.transpose(1, 0, 2).reshape(d, -1)
    w_f = jnp.zeros((d, LANES), BF16).at[:, :nh].set(kvf_full[:, 2 * d:])
    w = dict(p)
    w.update(
        w_glu=gl, w_kv=kvf_full[:, :2 * d][None, None], w_f=w_f[None, None],
        w_q=gq.reshape(1, gq.shape[0], d, d), w_o=go.reshape(1, go.shape[0], d, d), w_in=gin,
        w_out=gout.reshape(1, n_lay, -1, d),
        conv_w=gcw.transpose(1, 2, 0, 3).reshape(n_lay, DT_CONV_TAPS, -1), conv_b=ffn_conv_b,
        ssm_d=gd.transpose(1, 0, 2).reshape(gd.shape[1], d))

    loss_part, grad_x, g = _local_step(x[0], loss_target[0], w)
    loss = lax.psum(loss_part, ("x", "y", "c"))

    g_kvf = jnp.concatenate([g["w_kv"], g["w_f"][:, :nh]], axis=1)
    big = dict(w_glu=g["w_glu"], w_kvf=g_kvf.reshape(d, N_CHIPS, -1).transpose(1, 0, 2), w_q=g["w_q"], w_o=g["w_o"],
               w_ffn_in=g["w_in"], w_ffn_out=g["w_out"])
    small_names = [k for k in _PARAMS if k not in _BIG]
    small_full = dict(g_mix=g["g_mix"], g_ffn=g["g_ffn"], lam_re=g["lam_re"], lam_im=g["lam_im"], log_dt=g["log_dt"],
                      ssm_b_re=g["ssm_b_re"], ssm_b_im=g["ssm_b_im"], ssm_c_re=g["ssm_c_re"], ssm_c_im=g["ssm_c_im"],
                      ssm_d=g["ssm_d"], g_kv=g["g_kv"], b_f=g["b_f"], ffn_conv_w=g["conv_w"],
                      ffn_conv_b=g["conv_b"], g_final=g["g_final"])
    small_list = [small_full[k] for k in small_names]
    pack = _pack(small_list, N_CHIPS * N_CORES * 2 * SUBLANES)
    pack4 = pack.reshape(N_CHIPS, pack.shape[0] // N_CHIPS, LANES)
    reduced = _reduce_scatter([big[k] for k in _BIG] + [pack4], [BF16] * len(_BIG) + [F32])
    red_big = dict(zip(_BIG, reduced[:-1]))
    pack_all = _all_gather([reduced[-1]], ["C"], name="small_grads_all_gather")[0]
    red_small = dict(zip(small_names, _unpack(pack_all, small_list)))
    for k in _SMALL_SHARDED:
        width = p[k].shape[-1]
        red_small[k] = lax.dynamic_slice_in_dim(red_small[k], slot * width, width, axis=red_small[k].ndim - 1)

    grads, deltas, new_m, new_v = {}, {}, {}, {}
    for k in _BIG:
        grads[k] = red_big[k]
        deltas[k], new_m[k], new_v[k] = _adamw(p[k], grads[k], mom1[k], mom2[k], name=f"adamw_{k}")
    packs = [_pack([src[k] for k in small_names], SUBLANES) for src in (p, red_small, mom1, mom2)]
    like = [p[k] for k in small_names]
    outs = [_unpack(o, like) for o in _adamw(*packs, name="adamw_small")]
    for i, k in enumerate(small_names):
        grads[k] = red_small[k]
        deltas[k], new_m[k], new_v[k] = outs[0][i], outs[1][i], outs[2][i]
    return (loss, grad_x[None], *[grads[k] for k in _PARAMS], *[deltas[k] for k in _PARAMS],
            *[new_m[k] for k in _PARAMS], *[new_v[k] for k in _PARAMS])
```

```python
import functools

import jax
import jax.numpy as jnp
from jax import lax
from jax.experimental import pallas as pl
from jax.experimental.pallas import tpu as pltpu

F32 = jnp.float32
BF16 = jnp.bfloat16

RMS_EPS = 1e-6
ADAM_LR = 0.001
ADAM_B1 = 0.9
ADAM_B2 = 0.999
ADAM_EPS = 1e-08
ADAM_WD = 0.01
ADAM_STEP = 10
DT_CONV_TAPS = 3

LANES = 128
SUBLANES = 8
HEAD_DIM = 64
FLASH_ROW_TILE = 32
S5_BLOCK_GROUPS = 16
VMEM_LIMIT_BYTES = 48 << 20
MM_BLOCK_BUDGET_BYTES = 30 << 20
N_CHIPS = 4
N_CORES = 2
MESH = pl.DeviceIdType.MESH


def _cp(n_grid):
    return pltpu.CompilerParams(dimension_semantics=("arbitrary",) * n_grid, vmem_limit_bytes=VMEM_LIMIT_BYTES)


def _tile(n, pref, mult=SUBLANES):
    if n <= pref:
        return n
    t = (pref // mult) * mult
    while t >= mult:
        if n % t == 0:
            return t
        t -= mult
    return n


def _row_tile(m, bytes_per_row, fixed_bytes):
    for tm in (1024, 512):
        if m % tm == 0 and 2 * (tm * bytes_per_row + fixed_bytes) <= MM_BLOCK_BUDGET_BYTES:
            return tm
    return _tile(m, 512)


def _dot(a, b, ca, cb):
    return lax.dot_general(a, b, (((ca,), (cb,)), ((), ())), preferred_element_type=F32)


def _mm_cols(x, w4, layer, *, wc, out_dtype=F32, scale=None, name):
    m, k = x.shape
    slots, _, k0, k1 = w4.shape
    nb = k1 if wc == 0 else k0
    assert (k0 if wc == 0 else k1) == k
    tm = _row_tile(m, k * x.dtype.itemsize + nb * jnp.dtype(out_dtype).itemsize, k0 * k1 * w4.dtype.itemsize)

    def body(x_ref, w_ref, o_ref):
        acc = _dot(x_ref[...].astype(BF16), w_ref[0, 0], 1, wc)
        if scale is not None:
            acc = acc * scale
        o_ref[...] = acc.astype(out_dtype)

    return pl.pallas_call(
        body, grid=(slots, m // tm),
        in_specs=[pl.BlockSpec((tm, k), lambda s, i: (i, 0)),
                  pl.BlockSpec((1, 1, k0, k1), lambda s, i: (s, layer, 0, 0))],
        out_specs=pl.BlockSpec((tm, nb), lambda s, i: (i, s)),
        out_shape=jax.ShapeDtypeStruct((m, slots * nb), out_dtype),
        compiler_params=_cp(2), name=name)(x, w4)


def _planes(a):
    return a if a.ndim == 3 else a[None]


def _mm_acc(x, w4, layer, *, wc, name):
    x = _planes(x)
    n_planes, m, width = x.shape
    slots, _, k0, k1 = w4.shape
    kb = k0 if wc == 0 else k1
    nout = k1 if wc == 0 else k0
    assert n_planes * width == slots * kb
    spp = slots // n_planes
    tm = _row_tile(m, kb * x.dtype.itemsize + nout * 4, k0 * k1 * w4.dtype.itemsize)

    def body(x_ref, w_ref, o_ref):
        @pl.when(pl.program_id(1) == 0)
        def _():
            o_ref[...] = jnp.zeros_like(o_ref)
        o_ref[...] += _dot(x_ref[0].astype(BF16), w_ref[0, 0], 1, wc)

    return pl.pallas_call(
        body, grid=(m // tm, slots),
        in_specs=[pl.BlockSpec((1, tm, kb), lambda i, s: (s // spp, i, s % spp)),
                  pl.BlockSpec((1, 1, k0, k1), lambda i, s: (s, layer, 0, 0))],
        out_specs=pl.BlockSpec((tm, nout), lambda i, s: (i, 0)),
        out_shape=jax.ShapeDtypeStruct((m, nout), F32),
        compiler_params=_cp(2), name=name)(x, w4)


def _mm_tn(x, dy, slots, *, scale=None, name):
    m, k = x.shape
    dy = _planes(dy)
    n_planes, _, width = dy.shape
    n = n_planes * width // slots
    spp = slots // n_planes
    ta = _tile(k, 512, LANES)
    tm = m
    while tm > 512 and tm % 2 == 0 and (2 * tm * (ta * x.dtype.itemsize + n * dy.dtype.itemsize)
                                         + 2 * ta * n * 4) > MM_BLOCK_BUDGET_BYTES:
        tm //= 2
    n_m = m // tm

    def body(x_ref, dy_ref, o_ref):
        @pl.when(pl.program_id(2) == 0)
        def _():
            o_ref[...] = jnp.zeros_like(o_ref)
        o_ref[0] += _dot(x_ref[...].astype(BF16), dy_ref[0].astype(BF16), 0, 0)
        if scale is not None:
            @pl.when(pl.program_id(2) == n_m - 1)
            def _():
                o_ref[...] = o_ref[...] * scale

    return pl.pallas_call(
        body, grid=(slots, k // ta, n_m),
        in_specs=[pl.BlockSpec((tm, ta), lambda s, a, i: (i, a)),
                  pl.BlockSpec((1, tm, n), lambda s, a, i: (s // spp, i, s % spp))],
        out_specs=pl.BlockSpec((1, ta, n), lambda s, a, i: (s, a, 0)),
        out_shape=jax.ShapeDtypeStruct((slots, k, n), F32),
        compiler_params=_cp(3), name=name)(x, dy)


def _rowwise(fn, rows, consts, outs, accs=(), *, tl=256, name):
    n_rows = rows[0].shape[0]
    tl = _tile(n_rows, tl)
    n_in = len(rows) + len(consts)
    n_out = len(outs)

    def body(*refs):
        res = fn(*[r[...] for r in refs[:n_in]])
        res = res if isinstance(res, (tuple, list)) else (res,)
        o_refs = refs[n_in:n_in + n_out]
        a_refs = refs[n_in + n_out:]
        for o, val in zip(o_refs, res[:n_out]):
            o[...] = val.astype(o.dtype)
        if a_refs:
            @pl.when(pl.program_id(0) == 0)
            def _():
                for a in a_refs:
                    a[...] = jnp.zeros_like(a)
            for a, val in zip(a_refs, res[n_out:]):
                a[...] += val

    in_specs = ([pl.BlockSpec((tl, r.shape[1]), lambda i: (i, 0)) for r in rows]
                + [pl.BlockSpec(c.shape, lambda i: (0, 0)) for c in consts])
    out_specs = ([pl.BlockSpec((tl, w), lambda i: (i, 0)) for w, _ in outs]
                 + [pl.BlockSpec(s, lambda i: (0, 0)) for s in accs])
    out_shape = ([jax.ShapeDtypeStruct((n_rows, w), dt) for w, dt in outs]
                 + [jax.ShapeDtypeStruct(s, F32) for s in accs])
    return pl.pallas_call(body, grid=(n_rows // tl,), in_specs=in_specs, out_specs=out_specs,
                          out_shape=out_shape, compiler_params=_cp(1), name=name)(*rows, *consts)


def _rms(x, g):
    return x * lax.rsqrt(jnp.mean(x * x, axis=-1, keepdims=True) + RMS_EPS) * g


def _sigmoid(x):
    return 1.0 / (1.0 + jnp.exp(-x))


def _glu(zz):
    d = zz.shape[1] // 2
    return zz[:, :d] * _sigmoid(zz[:, d:])


def _gelu(y):
    return jax.nn.gelu(y)


def _row2(v):
    return v.reshape(1, -1)


def _node_bwd(d_in, h, branches, *, name):
    width = h.shape[1]
    flat = [dy for _, dys in branches for dy in dys]
    counts = [len(dys) for _, dys in branches]
    gains = [_row2(g) for g, _ in branches]

    def fn(d, hh, *rest):
        dys, gs = rest[:len(flat)], rest[len(flat):]
        tot, dgs, pos = d, [], 0
        for g, cnt in zip(gs, counts):
            dy = dys[pos].astype(F32)
            for extra in dys[pos + 1:pos + cnt]:
                dy = dy + extra.astype(F32)
            pos += cnt
            _, vjp = jax.vjp(_rms, hh, g)
            dx, dg = vjp(dy)
            tot = tot + dx
            dgs.append(dg)
        return (tot, tot, *dgs)

    res = _rowwise(fn, [d_in, h, *flat], gains, [(width, F32), (width, BF16)], [(1, width)] * len(branches),
                   name=name)
    return res[0], res[1], [r[0] for r in res[2:]]


def _s5_prep_fn(lr, li, ldt, br, bi, cr, ci, *, gq, h, p):
    dt = jnp.exp(ldt)
    mag = jnp.exp(lr * dt)
    lb_re = mag * jnp.cos(li * dt)
    lb_im = mag * jnp.sin(li * dt)
    den = lr * lr + li * li
    nr = lb_re - 1.0
    fr = (nr * lr + lb_im * li) / den
    fi = (lb_im * lr - nr * li) / den
    bb_re = fr * br - fi * bi
    bb_im = fr * bi + fi * br
    shape = (gq * h, gq * p)
    r = lax.broadcasted_iota(jnp.int32, shape, 0)
    c = lax.broadcasted_iota(jnp.int32, shape, 1)
    mask = jnp.where(jnp.right_shift(r, h.bit_length() - 1) == jnp.right_shift(c, p.bit_length() - 1), 1.0, 0.0)

    def expand(t):
        return jnp.concatenate([t] * gq, axis=0) * mask

    return lb_re, lb_im, expand(bb_re), expand(bb_im), expand(cr), expand(ci)


def _s5_prep(lr, li, ldt, br, bi, cr, ci, p, *, name):
    n = lr.shape[1]
    h = br.shape[0]
    gq = S5_BLOCK_GROUPS
    nq, cq = gq * p, gq * h
    nblk = n // nq
    fn = functools.partial(_s5_prep_fn, gq=gq, h=h, p=p)

    def body(lr_r, li_r, ldt_r, br_r, bi_r, cr_r, ci_r, lbr_o, lbi_o, wbr_o, wbi_o, wcr_o, wci_o):
        lb_re, lb_im, wbr, wbi, wcr, wci = fn(lr_r[...], li_r[...], ldt_r[...], br_r[...], bi_r[...],
                                              cr_r[...], ci_r[...])
        lbr_o[...] = lb_re
        lbi_o[...] = lb_im
        wbr_o[0] = wbr.astype(BF16)
        wbi_o[0] = wbi.astype(BF16)
        wcr_o[0] = wcr.astype(BF16)
        wci_o[0] = wci.astype(BF16)

    vec = pl.BlockSpec((1, nq), lambda q: (0, q))
    tab = pl.BlockSpec((h, nq), lambda q: (0, q))
    wsp = pl.BlockSpec((1, cq, nq), lambda q: (q, 0, 0))
    wsh = jax.ShapeDtypeStruct((nblk, cq, nq), BF16)
    vsh = jax.ShapeDtypeStruct((1, n), F32)
    return pl.pallas_call(body, grid=(nblk,), in_specs=[vec, vec, vec, tab, tab, tab, tab],
                          out_specs=[vec, vec, wsp, wsp, wsp, wsp], out_shape=[vsh, vsh, wsh, wsh, wsh, wsh],
                          compiler_params=_cp(1), name=name)(lr, li, ldt, br, bi, cr, ci)


def _s5_prep_bwd(lr, li, ldt, br, bi, cr, ci, p, dlbr, dlbi, dwbr, dwbi, dwcr, dwci, *, name):
    n = lr.shape[1]
    h = br.shape[0]
    gq = S5_BLOCK_GROUPS
    nq, cq = gq * p, gq * h
    nblk = n // nq
    fn = functools.partial(_s5_prep_fn, gq=gq, h=h, p=p)

    def body(lr_r, li_r, ldt_r, br_r, bi_r, cr_r, ci_r, dlbr_r, dlbi_r, dwbr_r, dwbi_r, dwcr_r, dwci_r,
             *outs):
        _, vjp = jax.vjp(fn, lr_r[...], li_r[...], ldt_r[...], br_r[...], bi_r[...], cr_r[...], ci_r[...])
        grads = vjp((dlbr_r[0], dlbi_r[0], dwbr_r[0], dwbi_r[0], dwcr_r[0], dwci_r[0]))
        for o, g in zip(outs, grads):
            o[...] = g

    vec = pl.BlockSpec((1, nq), lambda q: (0, q))
    tab = pl.BlockSpec((h, nq), lambda q: (0, q))
    vec3 = pl.BlockSpec((1, 1, nq), lambda q: (q, 0, 0))
    wsp = pl.BlockSpec((1, cq, nq), lambda q: (q, 0, 0))
    vsh = jax.ShapeDtypeStruct((1, n), F32)
    tsh = jax.ShapeDtypeStruct((h, n), F32)
    return pl.pallas_call(body, grid=(nblk,),
                          in_specs=[vec, vec, vec, tab, tab, tab, tab, vec3, vec3, wsp, wsp, wsp, wsp],
                          out_specs=[vec, vec, vec, tab, tab, tab, tab],
                          out_shape=[vsh, vsh, vsh, tsh, tsh, tsh, tsh],
                          compiler_params=_cp(1), name=name)(lr, li, ldt, br, bi, cr, ci,
                                                             dlbr, dlbi, dwbr, dwbi, dwcr, dwci)


def _scan_rows(s_re, s_im, a_re, a_im, c_re, c_im, *, reverse):
    t_rows, n = s_re.shape
    nb = t_rows // SUBLANES
    row = lax.broadcasted_iota(jnp.int32, (SUBLANES, n), 0)

    def cmul(x, y):
        return x[0] * y[0] - x[1] * y[1], x[0] * y[1] + x[1] * y[0]

    a1 = (jnp.broadcast_to(a_re, (SUBLANES, n)), jnp.broadcast_to(a_im, (SUBLANES, n)))
    a2 = cmul(a1, a1)
    a4 = cmul(a2, a2)
    pk = (a_re, a_im)
    tab_re = jnp.zeros((SUBLANES, n), F32)
    tab_im = jnp.zeros((SUBLANES, n), F32)
    for i in range(SUBLANES):
        at = (SUBLANES - 1 - i) if reverse else i
        tab_re = jnp.where(row == at, pk[0], tab_re)
        tab_im = jnp.where(row == at, pk[1], tab_im)
        pk = cmul(pk, (a_re, a_im))

    def step(b, carry):
        cr, ci = carry
        blk = (nb - 1 - b) if reverse else b
        off = pl.multiple_of(blk * SUBLANES, SUBLANES)
        x_re = s_re[pl.ds(off, SUBLANES), :]
        x_im = s_im[pl.ds(off, SUBLANES), :]
        for d, (pr, pi) in ((1, a1), (2, a2), (4, a4)):
            if reverse:
                keep = row < SUBLANES - d
                sh = SUBLANES - d
            else:
                keep = row >= d
                sh = d
            sh_re = jnp.where(keep, pltpu.roll(x_re, sh, 0), 0.0)
            sh_im = jnp.where(keep, pltpu.roll(x_im, sh, 0), 0.0)
            x_re, x_im = x_re + pr * sh_re - pi * sh_im, x_im + pr * sh_im + pi * sh_re
        x_re, x_im = x_re + tab_re * cr - tab_im * ci, x_im + tab_re * ci + tab_im * cr
        s_re[pl.ds(off, SUBLANES), :] = x_re
        s_im[pl.ds(off, SUBLANES), :] = x_im
        edge = 0 if reverse else SUBLANES - 1
        return x_re[edge:edge + 1, :], x_im[edge:edge + 1, :]

    return lax.fori_loop(0, nb, step, (c_re, c_im))


def _s5_fwd(u, prep, dskip, *, name):
    lb_re, lb_im, wbr, wbi, wcr, wci = prep
    n_rows, _ = u.shape
    nblk, cq, nq = wbr.shape
    tt = _tile(n_rows, 512)
    nch = n_rows // tt

    def body(u_ref, wbr_r, wbi_r, wcr_r, wci_r, lbr_r, lbi_r, d_ref, y_ref, s_re, s_im, sbr_o, sbi_o, c_re, c_im):
        @pl.when(pl.program_id(1) == 0)
        def _():
            c_re[...] = jnp.zeros_like(c_re)
            c_im[...] = jnp.zeros_like(c_im)
        uf = u_ref[...]
        ub = uf.astype(BF16)
        s_re[...] = _dot(ub, wbr_r[0], 1, 0)
        s_im[...] = _dot(ub, wbi_r[0], 1, 0)
        sbr_o[0] = c_re[...]
        sbi_o[0] = c_im[...]
        cr, ci = _scan_rows(s_re, s_im, lbr_r[...], lbi_r[...], c_re[...], c_im[...], reverse=False)
        c_re[...] = cr
        c_im[...] = ci
        y = _dot(s_re[...].astype(BF16), wcr_r[0], 1, 1) - _dot(s_im[...].astype(BF16), wci_r[0], 1, 1)
        y_ref[...] = y + d_ref[...] * uf

    wsp = pl.BlockSpec((1, cq, nq), lambda q, i: (q, 0, 0))
    vec = pl.BlockSpec((1, nq), lambda q, i: (0, q))
    act = pl.BlockSpec((tt, cq), lambda q, i: (i, q))
    sb = pl.BlockSpec((1, 1, nq), lambda q, i: (i, 0, q))
    sbsh = jax.ShapeDtypeStruct((nch, 1, nblk * nq), F32)
    states = pl.BlockSpec((tt, nq), lambda q, i: (i, q))
    stsh = jax.ShapeDtypeStruct((n_rows, nblk * nq), F32)
    return pl.pallas_call(
        body, grid=(nblk, nch),
        in_specs=[act, wsp, wsp, wsp, wsp, vec, vec, pl.BlockSpec((1, cq), lambda q, i: (0, q))],
        out_specs=[act, states, states, sb, sb],
        out_shape=[jax.ShapeDtypeStruct(u.shape, F32), stsh, stsh, sbsh, sbsh],
        scratch_shapes=[pltpu.VMEM((1, nq), F32), pltpu.VMEM((1, nq), F32)],
        compiler_params=_cp(2), name=name)(u, wbr, wbi, wcr, wci, lb_re, lb_im, dskip)


def _s5_bwd(u, dy, st_re, st_im, sb_re, sb_im, prep, dskip, *, name):
    lb_re, lb_im, wbr, wbi, wcr, wci = prep
    n_rows, _ = u.shape
    nblk, cq, nq = wbr.shape
    tt = _tile(n_rows, 512)
    nch = n_rows // tt

    def body(u_ref, dy_ref, s_re, s_im, sbr_r, sbi_r, wbr_r, wbi_r, wcr_r, wci_r, lbr_r, lbi_r, d_ref,
             du_ref, dwbr, dwbi, dwcr, dwci, dlbr, dlbi, dd_ref, g_re, g_im, lc_re, lc_im):
        @pl.when(pl.program_id(1) == 0)
        def _():
            for ref in (lc_re, lc_im, dwbr, dwbi, dwcr, dwci, dlbr, dlbi, dd_ref):
                ref[...] = jnp.zeros_like(ref)
        uf = u_ref[...]
        ub = uf.astype(BF16)
        dyf = dy_ref[...]
        dyb = dyf.astype(BF16)
        sr16 = s_re[...].astype(BF16)
        si16 = s_im[...].astype(BF16)
        dwcr[0] += _dot(dyb, sr16, 0, 0)
        dwci[0] -= _dot(dyb, si16, 0, 0)
        g_re[...] = _dot(dyb, wcr_r[0], 1, 0)
        g_im[...] = -_dot(dyb, wci_r[0], 1, 0)
        lcr, lci = _scan_rows(g_re, g_im, lbr_r[...], -lbi_r[...], lc_re[...], lc_im[...], reverse=True)
        lc_re[...] = lcr
        lc_im[...] = lci
        lam_r = g_re[...]
        lam_i = g_im[...]
        first = lax.broadcasted_iota(jnp.int32, (tt, nq), 0) == 0
        prev_r = jnp.where(first, sbr_r[0], pltpu.roll(s_re[...], 1, 0))
        prev_i = jnp.where(first, sbi_r[0], pltpu.roll(s_im[...], 1, 0))
        dlbr[0] += jnp.sum(lam_r * prev_r + lam_i * prev_i, axis=0, keepdims=True)
        dlbi[0] += jnp.sum(lam_i * prev_r - lam_r * prev_i, axis=0, keepdims=True)
        lr16 = lam_r.astype(BF16)
        li16 = lam_i.astype(BF16)
        du_ref[...] = _dot(lr16, wbr_r[0], 1, 1) + _dot(li16, wbi_r[0], 1, 1) + d_ref[...] * dyf
        dwbr[0] += _dot(ub, lr16, 0, 0)
        dwbi[0] += _dot(ub, li16, 0, 0)
        dd_ref[0] += jnp.sum(dyf * uf, axis=0, keepdims=True)

    last = nch - 1
    wsp = pl.BlockSpec((1, cq, nq), lambda q, i: (q, 0, 0))
    vec = pl.BlockSpec((1, nq), lambda q, i: (0, q))
    act = pl.BlockSpec((tt, cq), lambda q, i: (last - i, q))
    sb = pl.BlockSpec((1, 1, nq), lambda q, i: (last - i, 0, q))
    vec3 = pl.BlockSpec((1, 1, nq), lambda q, i: (q, 0, 0))
    dsp = pl.BlockSpec((1, 1, cq), lambda q, i: (q, 0, 0))
    wsh = jax.ShapeDtypeStruct((nblk, cq, nq), F32)
    v3sh = jax.ShapeDtypeStruct((nblk, 1, nq), F32)
    big = pltpu.VMEM((tt, nq), F32)
    states = pl.BlockSpec((tt, nq), lambda q, i: (last - i, q))
    return pl.pallas_call(
        body, grid=(nblk, nch),
        in_specs=[act, act, states, states, sb, sb, wsp, wsp, wsp, wsp, vec, vec,
                  pl.BlockSpec((1, cq), lambda q, i: (0, q))],
        out_specs=[act, wsp, wsp, wsp, wsp, vec3, vec3, dsp],
        out_shape=[jax.ShapeDtypeStruct(u.shape, F32), wsh, wsh, wsh, wsh, v3sh, v3sh,
                   jax.ShapeDtypeStruct((nblk, 1, cq), F32)],
        scratch_shapes=[big, big, pltpu.VMEM((1, nq), F32), pltpu.VMEM((1, nq), F32)],
        compiler_params=_cp(2), name=name)(u, dy, st_re, st_im, sb_re, sb_im, wbr, wbi, wcr, wci, lb_re, lb_im,
                                           dskip)


def _conv_taps(cur, prev, w, b):
    rid = lax.broadcasted_iota(jnp.int32, cur.shape, 0)
    x1 = jnp.where(rid == 0, prev[7:8, :], pltpu.roll(cur, 1, 0))
    x2 = jnp.where(rid == 0, prev[6:7, :], jnp.where(rid == 1, prev[7:8, :], pltpu.roll(cur, 2, 0)))
    return b + x2 * w[0:1, :] + x1 * w[1:2, :] + cur * w[2:3, :], x1, x2


def _conv_fwd(uu, cw, cb, *, name):
    n_rows, f2 = uu.shape
    f = f2 // 2
    tc = _tile(f, 1408, LANES)
    tl = _tile(n_rows, 256)
    nfb = f // tc

    def body(g_ref, u_ref, wg_ref, wu_ref, bg_ref, bu_ref, o_ref, pg, pu):
        @pl.when(pl.program_id(1) == 0)
        def _():
            pg[...] = jnp.zeros_like(pg)
            pu[...] = jnp.zeros_like(pu)
        gcur = g_ref[...]
        ucur = u_ref[...]
        cg, _, _ = _conv_taps(gcur, pg[...], wg_ref[...], bg_ref[...])
        cu, _, _ = _conv_taps(ucur, pu[...], wu_ref[...], bu_ref[...])
        o_ref[...] = (cg * _sigmoid(cg) * cu).astype(o_ref.dtype)
        pg[...] = gcur[tl - SUBLANES:, :]
        pu[...] = ucur[tl - SUBLANES:, :]

    return pl.pallas_call(
        body, grid=(nfb, n_rows // tl),
        in_specs=[pl.BlockSpec((tl, tc), lambda j, i: (i, j)), pl.BlockSpec((tl, tc), lambda j, i: (i, j + nfb)),
                  pl.BlockSpec((DT_CONV_TAPS, tc), lambda j, i: (0, j)),
                  pl.BlockSpec((DT_CONV_TAPS, tc), lambda j, i: (0, j + nfb)),
                  pl.BlockSpec((1, tc), lambda j, i: (0, j)), pl.BlockSpec((1, tc), lambda j, i: (0, j + nfb))],
        out_specs=pl.BlockSpec((tl, tc), lambda j, i: (i, j)),
        out_shape=jax.ShapeDtypeStruct((n_rows, f), BF16),
        scratch_shapes=[pltpu.VMEM((SUBLANES, tc), F32), pltpu.VMEM((SUBLANES, tc), F32)],
        compiler_params=_cp(2), name=name)(uu, uu, cw, cw, cb, cb)


def _conv_bwd(uu, dact, cw, cb, *, name):
    n_rows, f2 = uu.shape
    f = f2 // 2
    tc = _tile(f, 1408, LANES)
    tl = _tile(n_rows, 256)
    nfb = f // tc
    nrb = n_rows // tl
    halo_per_tile = tl // SUBLANES

    def body(g_ref, gh_ref, u_ref, uh_ref, da_ref, wg_ref, wu_ref, bg_ref, bu_ref,
             duu_ref, dw_ref, db_ref, nxt_g, nxt_u):
        i = pl.program_id(1)
        rb = nrb - 1 - i

        @pl.when(i == 0)
        def _():
            for ref in (nxt_g, nxt_u, dw_ref, db_ref):
                ref[...] = jnp.zeros_like(ref)
        has_prev = jnp.where(rb > 0, 1.0, 0.0)
        gcur, ucur = g_ref[...], u_ref[...]
        wg, wu = wg_ref[...], wu_ref[...]
        cg, g1, g2 = _conv_taps(gcur, gh_ref[...] * has_prev, wg, bg_ref[...])
        cu, u1, u2 = _conv_taps(ucur, uh_ref[...] * has_prev, wu, bu_ref[...])
        sg = _sigmoid(cg)
        silu = cg * sg
        da = da_ref[...]
        rid = lax.broadcasted_iota(jnp.int32, da.shape, 0)

        def transpose_conv(plane, d, cur, x1, x2, w, nxt):
            nx = nxt[...]
            d1 = jnp.where(rid == tl - 1, nx[0:1, :], pltpu.roll(d, tl - 1, 0))
            d2 = jnp.where(rid == tl - 2, nx[0:1, :],
                           jnp.where(rid == tl - 1, nx[1:2, :], pltpu.roll(d, tl - 2, 0)))
            duu_ref[plane] = (w[2:3, :] * d + w[1:2, :] * d1 + w[0:1, :] * d2).astype(duu_ref.dtype)
            nxt[...] = d[0:SUBLANES, :]
            dw_ref[plane] += jnp.concatenate([jnp.sum(d * x2, axis=0, keepdims=True),
                                              jnp.sum(d * x1, axis=0, keepdims=True),
                                              jnp.sum(d * cur, axis=0, keepdims=True)], axis=0)
            db_ref[plane] += jnp.sum(d, axis=0, keepdims=True)

        transpose_conv(0, da * cu * (sg * (1.0 + cg * (1.0 - sg))), gcur, g1, g2, wg, nxt_g)
        transpose_conv(1, da * silu, ucur, u1, u2, wu, nxt_u)

    def halo(j, i):
        return jnp.maximum((nrb - 1 - i) * halo_per_tile - 1, 0)

    return pl.pallas_call(
        body, grid=(nfb, nrb),
        in_specs=[pl.BlockSpec((tl, tc), lambda j, i: (nrb - 1 - i, j)),
                  pl.BlockSpec((SUBLANES, tc), lambda j, i: (halo(j, i), j)),
                  pl.BlockSpec((tl, tc), lambda j, i: (nrb - 1 - i, j + nfb)),
                  pl.BlockSpec((SUBLANES, tc), lambda j, i: (halo(j, i), j + nfb)),
                  pl.BlockSpec((tl, tc), lambda j, i: (nrb - 1 - i, j)),
                  pl.BlockSpec((DT_CONV_TAPS, tc), lambda j, i: (0, j)),
                  pl.BlockSpec((DT_CONV_TAPS, tc), lambda j, i: (0, j + nfb)),
                  pl.BlockSpec((1, tc), lambda j, i: (0, j)),
                  pl.BlockSpec((1, tc), lambda j, i: (0, j + nfb))],
        out_specs=[pl.BlockSpec((2, tl, tc), lambda j, i: (0, nrb - 1 - i, j)),
                   pl.BlockSpec((2, DT_CONV_TAPS, tc), lambda j, i: (0, 0, j)),
                   pl.BlockSpec((2, 1, tc), lambda j, i: (0, 0, j))],
        out_shape=[jax.ShapeDtypeStruct((2, n_rows, f), BF16), jax.ShapeDtypeStruct((2, DT_CONV_TAPS, f), F32),
                   jax.ShapeDtypeStruct((2, 1, f), F32)],
        scratch_shapes=[pltpu.VMEM((SUBLANES, tc), F32), pltpu.VMEM((SUBLANES, tc), F32)],
        compiler_params=_cp(2), name=name)(uu, uu, uu, uu, dact, cw, cw, cb, cb)


def _log_sigmoid(x):
    t = jnp.exp(-jnp.abs(x))
    log1p_t = jnp.where(t < 1e-3, t * (1.0 - t * (0.5 - t * (1.0 / 3.0))), jnp.log(1.0 + t))
    return jnp.minimum(x, 0.0) - log1p_t


def _dlog_sigmoid(x):
    t = jnp.exp(-jnp.abs(x))
    return jnp.where(x >= 0, t, 1.0) / (1.0 + t)


def _tri_dot(tri, x):
    return jnp.dot(tri, x, precision=lax.Precision.HIGHEST, preferred_element_type=F32)


def _cum_fwd(fl, bf, *, name):
    n_rows, width = fl.shape
    tc = _tile(n_rows, 256)

    def body(fl_ref, bf_ref, o_ref, carry):
        @pl.when(pl.program_id(0) == 0)
        def _():
            carry[...] = jnp.zeros_like(carry)
        x = _log_sigmoid(fl_ref[...] + bf_ref[...])
        r = lax.broadcasted_iota(jnp.int32, (tc, tc), 0)
        c = lax.broadcasted_iota(jnp.int32, (tc, tc), 1)
        y = _tri_dot(jnp.where(r >= c, 1.0, 0.0), x) + carry[...]
        o_ref[...] = y
        carry[...] = y[tc - 1:tc, :]

    return pl.pallas_call(
        body, grid=(n_rows // tc,),
        in_specs=[pl.BlockSpec((tc, width), lambda i: (i, 0)), pl.BlockSpec((1, width), lambda i: (0, 0))],
        out_specs=pl.BlockSpec((tc, width), lambda i: (i, 0)),
        out_shape=jax.ShapeDtypeStruct(fl.shape, F32),
        scratch_shapes=[pltpu.VMEM((1, width), F32)], compiler_params=_cp(1), name=name)(fl, bf)


def _cum_bwd(dcum, fl, bf, *, name):
    n_rows, width = fl.shape
    tc = _tile(n_rows, 256)
    last = n_rows // tc - 1

    def body(dc_ref, fl_ref, bf_ref, dfl_ref, dbf_ref, carry):
        @pl.when(pl.program_id(0) == 0)
        def _():
            carry[...] = jnp.zeros_like(carry)
            dbf_ref[...] = jnp.zeros_like(dbf_ref)
        r = lax.broadcasted_iota(jnp.int32, (tc, tc), 0)
        c = lax.broadcasted_iota(jnp.int32, (tc, tc), 1)
        dls = _tri_dot(jnp.where(r <= c, 1.0, 0.0), dc_ref[...]) + carry[...]
        carry[...] = dls[0:1, :]
        dfl = dls * _dlog_sigmoid(fl_ref[...] + bf_ref[...])
        dfl_ref[...] = dfl.astype(dfl_ref.dtype)
        dbf_ref[...] += jnp.sum(dfl, axis=0, keepdims=True)

    return pl.pallas_call(
        body, grid=(n_rows // tc,),
        in_specs=[pl.BlockSpec((tc, width), lambda i: (last - i, 0)),
                  pl.BlockSpec((tc, width), lambda i: (last - i, 0)),
                  pl.BlockSpec((1, width), lambda i: (0, 0))],
        out_specs=[pl.BlockSpec((tc, width), lambda i: (last - i, 0)), pl.BlockSpec((1, width), lambda i: (0, 0))],
        out_shape=[jax.ShapeDtypeStruct(fl.shape, BF16), jax.ShapeDtypeStruct((1, width), F32)],
        scratch_shapes=[pltpu.VMEM((1, width), F32)], compiler_params=_cp(1), name=name)(dcum, fl, bf)


def _head_masks():
    lane = lax.broadcasted_iota(jnp.int32, (1, LANES), 1)
    return (lane < HEAD_DIM, lane >= HEAD_DIM)


def _flash_fwd(q, kv, cq3, ck3, *, tq, name):
    n_rows, d = q.shape
    nhp = d // LANES
    tk = tq
    nq = n_rows // tq
    nk = n_rows // tk

    rt = _tile(tq, FLASH_ROW_TILE)

    def body(q_ref, k_ref, v_ref, cq_ref, ck_ref, o_ref, lse_ref, m0, m1, l0, l1, acc, s0, s1, p0, p1):
        i = pl.program_id(1)
        j = pl.program_id(2)
        ms, ls = (m0, m1), (l0, l1)

        @pl.when(j == 0)
        def _():
            for h in range(2):
                ms[h][...] = jnp.full_like(ms[h], -jnp.inf)
                ls[h][...] = jnp.zeros_like(ls[h])
            acc[...] = jnp.zeros_like(acc)

        def block(diagonal):
            qv, kk, vv = q_ref[...], k_ref[...], v_ref[...]
            a = acc[...]
            for h, msk in enumerate(_head_masks()):
                s_sc, p_sc = ((s0, p0), (s1, p1))[h]
                s_sc[...] = _dot(jnp.where(msk, qv, jnp.zeros_like(qv)), kk, 1, 1)
                bias = cq_ref[0, 0:1, h:h + 1] - ck_ref[0, h:h + 1, :]
                m_old, l_old = ms[h][...], ls[h][...]
                m_tiles, sum_tiles = [], []
                for r in range(tq // rt):
                    rows = slice(r * rt, (r + 1) * rt)
                    s = s_sc[rows, :] + bias
                    if diagonal:
                        rr = r * rt + lax.broadcasted_iota(jnp.int32, (rt, tk), 0)
                        cc = lax.broadcasted_iota(jnp.int32, (rt, tk), 1)
                        s = jnp.where(cc <= rr, s, -jnp.inf)
                    s_sc[rows, :] = s
                    m_tiles.append(jnp.maximum(m_old[rows, :], jnp.max(s, axis=1, keepdims=True)))
                for r in range(tq // rt):
                    rows = slice(r * rt, (r + 1) * rt)
                    p = jnp.exp(s_sc[rows, :] - jnp.tile(m_tiles[r], (1, tk // LANES)))
                    sum_tiles.append(jnp.sum(p, axis=1, keepdims=True))
                    p_sc[rows, :] = p.astype(BF16)
                m_new = jnp.concatenate(m_tiles, axis=0)
                alpha = jnp.exp(m_old - m_new)
                ms[h][...] = m_new
                ls[h][...] = alpha * l_old + jnp.concatenate(sum_tiles, axis=0)
                pv = _dot(p_sc[...], jnp.where(msk, vv, jnp.zeros_like(vv)), 1, 0)
                a = a * jnp.where(msk, alpha, 1.0) + pv
            acc[...] = a

        pl.when(j < i)(functools.partial(block, False))
        pl.when(j == i)(functools.partial(block, True))

        @pl.when(j == nk - 1)
        def _():
            m_a, m_b = _head_masks()
            o_ref[...] = acc[...] * jnp.where(m_a, 1.0 / l0[...], 1.0 / l1[...])
            two = lax.broadcasted_iota(jnp.int32, (tq, 2), 1)
            lse_a = (m0[...] + jnp.log(l0[...]))[:, 0:1]
            lse_b = (m1[...] + jnp.log(l1[...]))[:, 0:1]
            lse_ref[0] = jnp.where(two == 0, lse_a, lse_b)

    def kv_blk(i, j):
        return jnp.minimum(j, i)

    col = pltpu.VMEM((tq, LANES), F32)
    return pl.pallas_call(
        body, grid=(nhp, nq, nk),
        in_specs=[pl.BlockSpec((tq, LANES), lambda hp, i, j: (i, hp)),
                  pl.BlockSpec((tk, LANES), lambda hp, i, j: (kv_blk(i, j), hp)),
                  pl.BlockSpec((tk, LANES), lambda hp, i, j: (kv_blk(i, j), nhp + hp)),
                  pl.BlockSpec((1, tq, 2), lambda hp, i, j: (hp, i, 0)),
                  pl.BlockSpec((1, 2, tk), lambda hp, i, j: (hp, 0, kv_blk(i, j)))],
        out_specs=[pl.BlockSpec((tq, LANES), lambda hp, i, j: (i, hp)),
                   pl.BlockSpec((1, tq, 2), lambda hp, i, j: (hp, i, 0))],
        out_shape=[jax.ShapeDtypeStruct((n_rows, d), F32), jax.ShapeDtypeStruct((nhp, n_rows, 2), F32)],
        scratch_shapes=[col, col, col, col, pltpu.VMEM((tq, LANES), F32), pltpu.VMEM((tq, tk), F32),
                        pltpu.VMEM((tq, tk), F32), pltpu.VMEM((tq, tk), BF16), pltpu.VMEM((tq, tk), BF16)],
        compiler_params=_cp(3), name=name)(q, kv, kv, cq3, ck3)


def _flash_bwd(q, kv, o, lse, do, cq3, ck3, *, tq, name):
    n_rows, d = q.shape
    nhp = d // LANES
    tk = tq
    nq = n_rows // tq
    nk = n_rows // tk

    rt = _tile(tq, FLASH_ROW_TILE)

    def body(q_ref, k_ref, v_ref, o_ref, lse_ref, do_ref, cq_ref, ck_ref,
             dq_ref, dk_ref, dv_ref, dck_ref, dcq_ref, s0, s1, dp0, dp1, p0, p1, ds0, ds1, dl0, dl1, lb0, lb1):
        j = pl.program_id(1)
        i = pl.program_id(2)

        @pl.when(jnp.logical_and(j == 0, i == 0))
        def _():
            dq_ref[...] = jnp.zeros_like(dq_ref)
            dcq_ref[...] = jnp.zeros_like(dcq_ref)

        @pl.when(i == 0)
        def _():
            dk_ref[...] = jnp.zeros_like(dk_ref)
            dv_ref[...] = jnp.zeros_like(dv_ref)
            dck_ref[...] = jnp.zeros_like(dck_ref)

        def block(diagonal):
            qv, kk, vv = q_ref[...], k_ref[...], v_ref[...]
            dob = do_ref[...].astype(BF16)
            prod = dob.astype(F32) * o_ref[...]
            off = pl.multiple_of(i * tq, tq)
            lse_all = lse_ref[0]
            dck_rows, row_sums = [], []
            dq_acc = jnp.zeros((tq, LANES), F32)
            dk_acc = jnp.zeros((tk, LANES), F32)
            dv_acc = jnp.zeros((tk, LANES), F32)
            for h, msk in enumerate(_head_masks()):
                s_sc, dp_sc, p_sc, ds_sc = ((s0, dp0, p0, ds0), (s1, dp1, p1, ds1))[h]
                qh = jnp.where(msk, qv, jnp.zeros_like(qv))
                kh = jnp.where(msk, kk, jnp.zeros_like(kk))
                doh = jnp.where(msk, dob, jnp.zeros_like(dob))
                s_sc[...] = _dot(qh, kk, 1, 1)
                dp_sc[...] = _dot(doh, vv, 1, 1)
                bias = cq_ref[0, 0:1, h:h + 1] - ck_ref[0, h:h + 1, :]
                delta, lse_h = ((dl0, lb0), (dl1, lb1))[h]
                delta[...] = jnp.broadcast_to(jnp.sum(jnp.where(msk, prod, 0.0), axis=1, keepdims=True),
                                              (tq, LANES))
                lse_h[...] = jnp.broadcast_to(lse_all[:, h:h + 1], (tq, LANES))
                reps = (1, tk // LANES)
                col_acc = jnp.zeros((SUBLANES, tk), F32)
                rs = []
                for r in range(tq // rt):
                    rows = slice(r * rt, (r + 1) * rt)
                    s = s_sc[rows, :] + bias
                    if diagonal:
                        rr = r * rt + lax.broadcasted_iota(jnp.int32, (rt, tk), 0)
                        cc = lax.broadcasted_iota(jnp.int32, (rt, tk), 1)
                        s = jnp.where(cc <= rr, s, -jnp.inf)
                    p = jnp.exp(s - jnp.tile(lse_h[rows, :], reps))
                    ds = p * (dp_sc[rows, :] - jnp.tile(delta[rows, :], reps))
                    rs.append(jnp.sum(ds, axis=1, keepdims=True))
                    for g in range(rt // SUBLANES):
                        col_acc = col_acc + ds[g * SUBLANES:(g + 1) * SUBLANES, :]
                    p_sc[rows, :] = p.astype(BF16)
                    ds_sc[rows, :] = ds.astype(BF16)
                dck_rows.append(-jnp.sum(col_acc, axis=0, keepdims=True))
                row_sums.append(jnp.concatenate(rs, axis=0))
                dv_acc = dv_acc + _dot(p_sc[...], doh, 0, 0)
                dsb = ds_sc[...]
                dq_acc = dq_acc + _dot(dsb, kh, 1, 0)
                dk_acc = dk_acc + _dot(dsb, qh, 0, 0)
            dq_ref[pl.ds(off, tq), :] += dq_acc
            dk_ref[...] += dk_acc
            dv_ref[...] += dv_acc
            two = lax.broadcasted_iota(jnp.int32, (tq, 2), 1)
            dcq_ref[0, pl.ds(off, tq), :] += jnp.where(two == 0, row_sums[0], row_sums[1])
            dck_ref[0] += jnp.concatenate(dck_rows, axis=0)

        pl.when(i > j)(functools.partial(block, False))
        pl.when(i == j)(functools.partial(block, True))

    def q_blk(j, i):
        return jnp.maximum(i, j)

    score = pltpu.VMEM((tq, tk), F32)
    score16 = pltpu.VMEM((tq, tk), BF16)
    rowstat = pltpu.VMEM((tq, LANES), F32)
    return pl.pallas_call(
        body, grid=(nhp, nk, nq),
        in_specs=[pl.BlockSpec((tq, LANES), lambda hp, j, i: (q_blk(j, i), hp)),
                  pl.BlockSpec((tk, LANES), lambda hp, j, i: (j, hp)),
                  pl.BlockSpec((tk, LANES), lambda hp, j, i: (j, nhp + hp)),
                  pl.BlockSpec((tq, LANES), lambda hp, j, i: (q_blk(j, i), hp)),
                  pl.BlockSpec((1, tq, 2), lambda hp, j, i: (hp, q_blk(j, i), 0)),
                  pl.BlockSpec((tq, LANES), lambda hp, j, i: (q_blk(j, i), hp)),
                  pl.BlockSpec((1, tq, 2), lambda hp, j, i: (hp, q_blk(j, i), 0)),
                  pl.BlockSpec((1, 2, tk), lambda hp, j, i: (hp, 0, j))],
        out_specs=[pl.BlockSpec((n_rows, LANES), lambda hp, j, i: (0, hp)),
                   pl.BlockSpec((tk, LANES), lambda hp, j, i: (j, hp)),
                   pl.BlockSpec((tk, LANES), lambda hp, j, i: (j, hp)),
                   pl.BlockSpec((1, 2, tk), lambda hp, j, i: (hp, 0, j)),
                   pl.BlockSpec((1, n_rows, 2), lambda hp, j, i: (hp, 0, 0))],
        out_shape=[jax.ShapeDtypeStruct((n_rows, d), F32), jax.ShapeDtypeStruct((n_rows, d), F32),
                   jax.ShapeDtypeStruct((n_rows, d), F32), jax.ShapeDtypeStruct((nhp, 2, n_rows), F32),
                   jax.ShapeDtypeStruct((nhp, n_rows, 2), F32)],
        scratch_shapes=[score, score, score, score, score16, score16, score16, score16, rowstat, rowstat,
                        rowstat, rowstat],
        compiler_params=_cp(3), name=name)(q, kv, kv, o, lse, do, cq3, ck3)


def _s5_tables(w, layer):
    g, p = w["lam_re"].shape[1:]
    h = w["ssm_b_re"].shape[3]
    n = g * p
    lr = w["lam_re"][layer].reshape(1, n)
    li = w["lam_im"][layer].reshape(1, n)
    ldt = jnp.broadcast_to(w["log_dt"][layer][:, None], (g, p)).reshape(1, n)
    br = w["ssm_b_re"][layer].transpose(2, 0, 1).reshape(h, n)
    bi = w["ssm_b_im"][layer].transpose(2, 0, 1).reshape(h, n)
    cr = w["ssm_c_re"][layer].transpose(1, 0, 2).reshape(h, n)
    ci = w["ssm_c_im"][layer].transpose(1, 0, 2).reshape(h, n)
    return (lr, li, ldt, br, bi, cr, ci), (g, p, h)


def _local_step(x, tgt, w, *, attn_tile=512):
    n_rows, d = x.shape
    n_layers = w["g_mix"].shape[0]
    n_s5 = w["lam_re"].shape[0]
    nh = w["b_f"].shape[0]
    nhp = nh // 2
    assert d == nh * HEAD_DIM
    tq = _tile(n_rows, attn_tile)
    in_slots = w["w_in"].shape[0]
    glu_slots = w["w_glu"].shape[0]
    g = {}
    saved = [dict() for _ in range(n_layers)]

    h = x
    nxt = _rowwise(lambda a, gg: _rms(a, gg), [x], [_row2(w["g_mix"][0])], [(d, F32)], name="rms_first")[0]
    kvb = fl = cum = cq3 = ck3 = hnkv = None
    bf_pad = jnp.zeros((1, LANES), F32).at[0, :nh].set(w["b_f"])
    for l in range(n_layers):
        sv = saved[l]
        sv["h"] = h
        g_ffn = _row2(w["g_ffn"][l])
        if l < n_s5:
            tabs, (_, p, _) = _s5_tables(w, l)
            prep = _s5_prep(*tabs, p, name=f"s5_prep{l}")
            dskip = w["ssm_d"][l].reshape(1, d)
            y, st_re, st_im, sb_re, sb_im = _s5_fwd(nxt, prep, dskip, name=f"s5_fwd{l}")
            z = _rowwise(_gelu, [y], [], [(d, BF16)], name=f"gelu{l}")[0]
            zz = _mm_cols(z, w["w_glu"], l, wc=0, name=f"glu_mm{l}")
            h1, hn2 = _rowwise(lambda hh, zq, gg: ((lambda t: (t, _rms(t, gg)))(hh + _glu(zq))),
                               [h, zz], [g_ffn], [(d, F32), (d, BF16)], name=f"mix_out{l}")
            sv.update(u=nxt, prep=prep, tabs=tabs, p=p, dskip=dskip, st_re=st_re, st_im=st_im, sb_re=sb_re,
                      sb_im=sb_im, y=y, z=z, zz=zz)
        else:
            j = l - n_s5
            qs = _mm_cols(nxt, w["w_q"], j, wc=0, out_dtype=BF16, scale=HEAD_DIM ** -0.5, name=f"q_mm{j}")
            o, lse = _flash_fwd(qs, kvb, cq3, ck3, tq=tq, name=f"flash_fwd{j}")
            a = _mm_cols(o, w["w_o"], j, wc=0, name=f"o_mm{j}")
            h1, hn2 = _rowwise(lambda hh, aa, gg: ((lambda t: (t, _rms(t, gg)))(hh + aa)),
                               [h, a], [g_ffn], [(d, F32), (d, BF16)], name=f"mix_out{l}")
            sv.update(hn=nxt, qs=qs, o=o, lse=lse)
        uu = _mm_cols(hn2, w["w_in"], l, wc=0, name=f"ffn_in{l}")
        cw, cb = w["conv_w"][l], _row2(w["conv_b"][l])
        act = _conv_fwd(uu, cw, cb, name=f"conv_fwd{l}")
        f = _mm_cols(act, w["w_out"], l, wc=0, name=f"ffn_out{l}")
        sv.update(h1=h1, hn2=hn2, uu=uu, act=act, cw=cw, cb=cb)
        if l == n_layers - 1:
            def loss_fn(hh, ff, tt, gg):
                yv, vjp = jax.vjp(_rms, hh + ff, gg)
                err = yv - tt
                part = 0.5 * jnp.sum(jnp.mean(err * err, axis=-1, keepdims=True), axis=0, keepdims=True)
                dh, dg = vjp(err * (1.0 / d))
                return dh, dh, jnp.broadcast_to(part, (1, LANES)), dg
            dcur, dcur16, loss_row, dgf = _rowwise(loss_fn, [h1, f, tgt], [_row2(w["g_final"])],
                                                   [(d, F32), (d, BF16)], [(1, LANES), (1, d)], name="loss")
            loss = loss_row[0, 0]
            g["g_final"] = dgf[0]
        elif l + 1 < n_s5:
            h, nxt = _rowwise(lambda hh, ff, gg: ((lambda t: (t, _rms(t, gg)))(hh + ff)), [h1, f],
                              [_row2(w["g_mix"][l + 1])], [(d, F32), (d, F32)], name=f"ffn_res{l}")
        elif l + 1 == n_s5:
            h, nxt, hnkv = _rowwise(
                lambda hh, ff, g1, g2: ((lambda t: (t, _rms(t, g1), _rms(t, g2)))(hh + ff)), [h1, f],
                [_row2(w["g_mix"][l + 1]), _row2(w["g_kv"])], [(d, F32), (d, BF16), (d, BF16)], name=f"ffn_res{l}")
            kvb = _mm_cols(hnkv, w["w_kv"], 0, wc=0, out_dtype=BF16, name="kv_mm")
            fl = _mm_cols(hnkv, w["w_f"], 0, wc=0, name="f_mm")
            cum = _cum_fwd(fl, bf_pad, name="cum_fwd")
            cq3 = cum[:, :nh].reshape(n_rows, nhp, 2).transpose(1, 0, 2)
            ck3 = cum[:, :nh].T.reshape(nhp, 2, n_rows)
        else:
            h, nxt = _rowwise(lambda hh, ff, gg: ((lambda t: (t, _rms(t, gg)))(hh + ff)), [h1, f],
                              [_row2(w["g_mix"][l + 1])], [(d, F32), (d, BF16)], name=f"ffn_res{l}")

    per_layer = {k: [None] * n_layers for k in ("g_mix", "g_ffn", "w_in", "w_out", "conv_w", "conv_b")}
    per_s5 = {k: [None] * n_s5 for k in ("lam_re", "lam_im", "log_dt", "ssm_b_re", "ssm_b_im", "ssm_c_re",
                                         "ssm_c_im", "ssm_d", "w_glu")}
    per_fox = {k: [None] * (n_layers - n_s5) for k in ("w_q", "w_o")}
    dk_parts, dv_parts, dck_parts = [], [], []
    for l in reversed(range(n_layers)):
        sv = saved[l]
        dact = _mm_cols(dcur16, w["w_out"], l, wc=1, name=f"ffn_out_dx{l}")
        per_layer["w_out"][l] = _mm_tn(sv["act"], dcur16, 1, name=f"ffn_out_dw{l}")[0]
        duu, dcw, dcb = _conv_bwd(sv["uu"], dact, sv["cw"], sv["cb"], name=f"conv_bwd{l}")
        per_layer["conv_w"][l] = jnp.concatenate([dcw[0], dcw[1]], axis=-1)
        per_layer["conv_b"][l] = jnp.concatenate([dcb[0, 0], dcb[1, 0]])
        dhn2 = _mm_acc(duu, w["w_in"], l, wc=1, name=f"ffn_in_dx{l}")
        per_layer["w_in"][l] = _mm_tn(sv["hn2"], duu, in_slots, name=f"ffn_in_dw{l}")
        d1, d1_16, (dg,) = _node_bwd(dcur, sv["h1"], [(w["g_ffn"][l], [dhn2])], name=f"ffn_norm_bwd{l}")
        per_layer["g_ffn"][l] = dg
        if l < n_s5:
            def glu_bwd(zq, dd):
                _, vjp = jax.vjp(_glu, zq)
                return vjp(dd)[0]
            dzz = _rowwise(glu_bwd, [sv["zz"], d1], [], [(2 * d, BF16)], name=f"glu_bwd{l}")[0]
            dz = _mm_acc(dzz, w["w_glu"], l, wc=1, name=f"glu_dx{l}")
            per_s5["w_glu"][l] = _mm_tn(sv["z"], dzz, glu_slots, name=f"glu_dw{l}")

            def gelu_bwd(yy, dd):
                _, vjp = jax.vjp(_gelu, yy)
                return vjp(dd)[0]
            dy = _rowwise(gelu_bwd, [sv["y"], dz], [], [(d, F32)], name=f"gelu_bwd{l}")[0]
            du, dwbr, dwbi, dwcr, dwci, dlbr, dlbi, dd = _s5_bwd(sv["u"], dy, sv["st_re"], sv["st_im"], sv["sb_re"],
                                                                 sv["sb_im"], sv["prep"], sv["dskip"],
                                                                 name=f"s5_bwd{l}")
            dlr, dli, dldt, dbr, dbi, dcr, dci = _s5_prep_bwd(*sv["tabs"], sv["p"], dlbr, dlbi, dwbr, dwbi, dwcr,
                                                              dwci, name=f"s5_prep_bwd{l}")
            gg, p = w["lam_re"].shape[1:]
            hh = w["ssm_b_re"].shape[3]
            per_s5["lam_re"][l] = dlr.reshape(gg, p)
            per_s5["lam_im"][l] = dli.reshape(gg, p)
            per_s5["log_dt"][l] = dldt.reshape(gg, p).sum(axis=1)
            per_s5["ssm_b_re"][l] = dbr.reshape(hh, gg, p).transpose(1, 2, 0)
            per_s5["ssm_b_im"][l] = dbi.reshape(hh, gg, p).transpose(1, 2, 0)
            per_s5["ssm_c_re"][l] = dcr.reshape(hh, gg, p).transpose(1, 0, 2)
            per_s5["ssm_c_im"][l] = dci.reshape(hh, gg, p).transpose(1, 0, 2)
            per_s5["ssm_d"][l] = dd.reshape(d)
            branches = [(w["g_mix"][l], [du])]
        else:
            j = l - n_s5
            do = _mm_cols(d1_16, w["w_o"], j, wc=1, name=f"o_dx{j}")
            per_fox["w_o"][j] = _mm_tn(sv["o"], d1_16, 1, name=f"o_dw{j}")[0]
            dq, dk, dv, dck, dcq = _flash_bwd(sv["qs"], kvb, sv["o"], sv["lse"], do, cq3, ck3, tq=tq,
                                              name=f"flash_bwd{j}")
            dk_parts.append(dk)
            dv_parts.append(dv)
            dck_parts.append(dck.reshape(nh, n_rows).T + dcq.transpose(1, 0, 2).reshape(n_rows, nh))
            scale = HEAD_DIM ** -0.5
            dhn = _mm_cols(dq, w["w_q"], j, wc=1, scale=scale, name=f"q_dx{j}")
            per_fox["w_q"][j] = _mm_tn(sv["hn"], dq, 1, scale=scale, name=f"q_dw{j}")[0]
            branches = [(w["g_mix"][l], [dhn])]
            if j == 0:
                def kv_sum(*parts):
                    half = len(parts) // 2
                    return jnp.concatenate([sum(parts[:half][1:], parts[0]),
                                            sum(parts[half:][1:], parts[half])], axis=1)
                dkv = _rowwise(kv_sum, dk_parts + dv_parts, [], [(2 * d, BF16)], name="dkv_sum")[0]
                dck_tot = dck_parts[0]
                for extra in dck_parts[1:]:
                    dck_tot = dck_tot + extra
                dcum = jnp.zeros((n_rows, LANES), F32).at[:, :nh].set(dck_tot)
                dfl, dbf = _cum_bwd(dcum, fl, bf_pad, name="cum_bwd")
                g["b_f"] = dbf[0, :nh]
                dhkv_a = _mm_cols(dkv, w["w_kv"], 0, wc=1, name="kv_dx")
                dhkv_b = _mm_cols(dfl, w["w_f"], 0, wc=1, name="f_dx")
                g["w_kv"] = _mm_tn(hnkv, dkv, 1, name="kv_dw")[0]
                g["w_f"] = _mm_tn(hnkv, dfl, 1, name="f_dw")[0]
                branches.append((w["g_kv"], [dhkv_a, dhkv_b]))
        dcur, dcur16, dgs = _node_bwd(d1, sv["h"], branches, name=f"mix_norm_bwd{l}")
        per_layer["g_mix"][l] = dgs[0]
        if len(dgs) > 1:
            g["g_kv"] = dgs[1]

    def by_row_shard(m):
        return m.reshape(N_CHIPS, m.shape[0] // N_CHIPS, m.shape[1])

    for k, v in (*per_layer.items(), *per_s5.items(), *per_fox.items()):
        if k in ("w_in", "w_glu"):
            g[k] = v
        elif k in ("w_out", "w_q", "w_o"):
            g[k] = [by_row_shard(m) for m in v]
        else:
            g[k] = jnp.stack(v)
    return loss, dcur, g


def _position():
    x, y, c = lax.axis_index("x"), lax.axis_index("y"), lax.axis_index("c")
    chips = [(1 - x, y), (x, 1 - y), (1 - x, 1 - y)]
    return x, y, c, chips


def _any_specs(n):
    return [pl.BlockSpec(memory_space=pl.ANY)] * n


def _all_gather(shards, kinds, *, name):
    n = len(shards)

    def out_shape(a, kind):
        shape = (N_CHIPS,) + a.shape if kind == "C" else (a.shape[0], N_CHIPS) + a.shape[1:]
        return jax.ShapeDtypeStruct(shape, a.dtype)

    def body(*refs):
        ins, outs = refs[:n], refs[n:2 * n]
        send_sems, recv_sems = refs[2 * n:]
        x, y, c, chips = _position()
        my_slot = 2 * x + y
        sibling = (x, y, 1 - c)

        def rows(t, half):
            hr = ins[t].shape[0] // 2
            return pl.ds(half * hr, hr)

        def piece(t, slot, half):
            return outs[t].at[slot, rows(t, half)] if kinds[t] == "C" else outs[t].at[rows(t, half), slot]

        def whole(t, slot):
            full = pl.ds(0, ins[t].shape[0])
            return outs[t].at[slot, full] if kinds[t] == "C" else outs[t].at[full, slot]

        def remote(k, t, src, dst, to):
            return pltpu.make_async_remote_copy(src_ref=src, dst_ref=dst, send_sem=send_sems.at[k, t],
                                                recv_sem=recv_sems.at[k, t], device_id=to, device_id_type=MESH)

        started = []
        for t in range(n):
            cp = remote(6, t, ins[t], whole(t, my_slot), sibling)
            cp.start()
            started.append(cp)
        for j, chip in enumerate(chips):
            for t in range(n):
                cp = remote(j, t, ins[t].at[rows(t, c)], piece(t, my_slot, c), (*chip, c))
                cp.start()
                started.append(cp)
        for j, chip in enumerate(chips):
            slot = 2 * chip[0] + chip[1]
            for t in range(n):
                remote(j, t, piece(t, slot, c), piece(t, slot, c), (*chip, c)).wait_recv()
            for t in range(n):
                cp = remote(3 + j, t, piece(t, slot, c), piece(t, slot, c), sibling)
                cp.start()
                started.append(cp)
        for j, chip in enumerate(chips):
            slot = 2 * chip[0] + chip[1]
            for t in range(n):
                remote(3 + j, t, piece(t, slot, 1 - c), piece(t, slot, 1 - c), sibling).wait_recv()
        for t in range(n):
            remote(6, t, ins[t], whole(t, my_slot), sibling).wait_recv()
        for cp in started:
            cp.wait_send()

    return pl.pallas_call(
        body, in_specs=_any_specs(n), out_specs=_any_specs(n),
        out_shape=[out_shape(a, k) for a, k in zip(shards, kinds)],
        scratch_shapes=[pltpu.SemaphoreType.DMA((7, n)), pltpu.SemaphoreType.DMA((7, n))],
        name=name)(*shards)


def _pair_exchange(grads, *, name):
    n = len(grads)

    def body(*refs):
        ins, outs = refs[:n], refs[n:2 * n]
        send_sems, recv_sems = refs[2 * n:]
        x, y, c, _ = _position()
        copies = [pltpu.make_async_remote_copy(src_ref=ins[t].at[:, 1 - c], dst_ref=outs[t],
                                               send_sem=send_sems.at[t], recv_sem=recv_sems.at[t],
                                               device_id=(x, y, 1 - c), device_id_type=MESH) for t in range(n)]
        for cp in copies:
            cp.start()
        for cp in copies:
            cp.wait()

    return pl.pallas_call(
        body, in_specs=_any_specs(n), out_specs=_any_specs(n),
        out_shape=[jax.ShapeDtypeStruct((a.shape[0],) + a.shape[2:], a.dtype) for a in grads],
        scratch_shapes=[pltpu.SemaphoreType.DMA((n,)), pltpu.SemaphoreType.DMA((n,))], name=name)(*grads)


def _chip_exchange(parts, *, name):
    n = len(parts)

    def body(*refs):
        ins, outs = refs[:n], refs[n:2 * n]
        send_sems, recv_sems = refs[2 * n:]
        _, _, c, chips = _position()
        copies = []
        for j, chip in enumerate(chips):
            slot = 2 * chip[0] + chip[1]
            for t in range(n):
                cp = pltpu.make_async_remote_copy(src_ref=ins[t].at[slot], dst_ref=outs[t].at[j],
                                                  send_sem=send_sems.at[j, t], recv_sem=recv_sems.at[j, t],
                                                  device_id=(*chip, c), device_id_type=MESH)
                cp.start()
                copies.append(cp)
        for cp in copies:
            cp.wait()

    return pl.pallas_call(
        body, in_specs=_any_specs(n), out_specs=_any_specs(n),
        out_shape=[jax.ShapeDtypeStruct((N_CHIPS - 1,) + a.shape[1:], a.dtype) for a in parts],
        scratch_shapes=[pltpu.SemaphoreType.DMA((N_CHIPS - 1, n)), pltpu.SemaphoreType.DMA((N_CHIPS - 1, n))],
        name=name)(*parts)


def _pair_share(both, *, name):
    n = len(both)

    def body(*refs):
        ins, outs = refs[:n], refs[n:2 * n]
        send_sems, recv_sems = refs[2 * n:]
        x, y, c, _ = _position()
        for t in range(n):
            pltpu.make_async_remote_copy(src_ref=ins[t].at[c], dst_ref=outs[t].at[c], send_sem=send_sems.at[t],
                                         recv_sem=recv_sems.at[t], device_id=(x, y, 1 - c),
                                         device_id_type=MESH).start()
        for t in range(n):
            pltpu.make_async_remote_copy(src_ref=ins[t].at[c], dst_ref=outs[t].at[1 - c], send_sem=send_sems.at[t],
                                         recv_sem=recv_sems.at[t], device_id=(x, y, 1 - c),
                                         device_id_type=MESH).wait()

    return pl.pallas_call(
        body, in_specs=_any_specs(n), out_specs=_any_specs(n),
        out_shape=[jax.ShapeDtypeStruct(a.shape, a.dtype) for a in both],
        input_output_aliases={t: t for t in range(n)},
        scratch_shapes=[pltpu.SemaphoreType.DMA((n,)), pltpu.SemaphoreType.DMA((n,))], name=name)(*both)


def _sum_pair(grad, landed, c, wire_dtype, *, name):
    slots, _, m, n = grad.shape
    tm = _tile(m, 256, 2 * SUBLANES)

    def body(c_ref, g_ref, l_ref, o_ref):
        o_ref[...] = (g_ref[0] + l_ref[...]).astype(wire_dtype)

    return pl.pallas_call(
        body,
        grid_spec=pltpu.PrefetchScalarGridSpec(
            num_scalar_prefetch=1, grid=(slots, m // tm),
            in_specs=[pl.BlockSpec((1, 1, tm, n), lambda s, i, c_ref: (s, c_ref[0], i, 0)),
                      pl.BlockSpec((1, tm, n), lambda s, i, c_ref: (s, i, 0))],
            out_specs=pl.BlockSpec((1, tm, n), lambda s, i, c_ref: (s, i, 0))),
        out_shape=jax.ShapeDtypeStruct((slots, m, n), wire_dtype), compiler_params=_cp(2), name=name)(
            c, grad, landed)


def _sum_chips(part, landed, slot_c, *, name):
    _, m, n = part.shape
    tm = _tile(m, 256, 2 * SUBLANES)

    def body(s_ref, p_ref, l_ref, o_ref):
        acc = p_ref[0].astype(F32)
        for j in range(N_CHIPS - 1):
            acc = acc + l_ref[j].astype(F32)
        o_ref[0] = acc

    return pl.pallas_call(
        body,
        grid_spec=pltpu.PrefetchScalarGridSpec(
            num_scalar_prefetch=1, grid=(m // tm,),
            in_specs=[pl.BlockSpec((1, tm, n), lambda i, s_ref: (s_ref[0], i, 0)),
                      pl.BlockSpec((N_CHIPS - 1, tm, n), lambda i, s_ref: (0, i, 0))],
            out_specs=pl.BlockSpec((1, tm, n), lambda i, s_ref: (s_ref[1], i, 0))),
        out_shape=jax.ShapeDtypeStruct((N_CORES, m, n), F32), compiler_params=_cp(1), name=name)(
            slot_c, part, landed)


def _reduce_scatter(grads, wire_dtypes):
    c = lax.axis_index("c").reshape(1).astype(jnp.int32)
    slot_c = jnp.stack([2 * lax.axis_index("x") + lax.axis_index("y"), lax.axis_index("c")]).astype(jnp.int32)
    views = []
    for a in grads:
        lead, last = a.shape[1], a.shape[-1]
        mid = 1
        for s in a.shape[2:-1]:
            mid *= s
        views.append(a.reshape(N_CHIPS, N_CORES, (lead // N_CORES) * mid, last))
    landed = _pair_exchange(views, name="rs_pair_exchange")
    parts = [_sum_pair(v, l, c, wire_dtypes[t], name=f"rs_pair_sum{t}")
             for t, (v, l) in enumerate(zip(views, landed))]
    landed = _chip_exchange(parts, name="rs_chip_exchange")
    both = [_sum_chips(p, l, slot_c, name=f"rs_chip_sum{t}") for t, (p, l) in enumerate(zip(parts, landed))]
    full = _pair_share(both, name="rs_pair_share")
    return [f.reshape(a.shape[1:]) for f, a in zip(full, grads)]


def _adamw(w, g, m, v, *, name):
    def fn(ww, gg, mm, vv):
        mm = ADAM_B1 * mm + (1.0 - ADAM_B1) * gg
        vv = ADAM_B2 * vv + (1.0 - ADAM_B2) * (gg * gg)
        m_hat = mm / (1.0 - ADAM_B1 ** ADAM_STEP)
        v_hat = vv / (1.0 - ADAM_B2 ** ADAM_STEP)
        delta = -ADAM_LR * (m_hat / (jnp.sqrt(v_hat) + ADAM_EPS) + ADAM_WD * ww)
        return delta, mm, vv

    shape = w.shape
    two_d = [a.reshape(-1, shape[-1]) for a in (w, g, m, v)]
    outs = _rowwise(fn, two_d, [], [(shape[-1], F32)] * 3, name=name)
    return [o.reshape(shape) for o in outs]


def _to_bf16(a, *, name):
    two_d = a.reshape(-1, a.shape[-1])
    return _rowwise(lambda t: t, [two_d], [], [(a.shape[-1], BF16)], name=name)[0].reshape(a.shape)


def _pack(arrays, rows_multiple):
    flat = jnp.concatenate([a.reshape(-1) for a in arrays])
    rows = -(-flat.shape[0] // LANES)
    rows = -(-rows // rows_multiple) * rows_multiple
    return jnp.pad(flat, (0, rows * LANES - flat.shape[0])).reshape(rows, LANES)


def _unpack(packed, like):
    flat = packed.reshape(-1)
    out, pos = [], 0
    for a in like:
        out.append(flat[pos:pos + a.size].reshape(a.shape))
        pos += a.size
    return out


_PARAMS = ("g_mix", "g_ffn", "lam_re", "lam_im", "log_dt", "ssm_b_re", "ssm_b_im", "ssm_c_re", "ssm_c_im", "ssm_d",
           "w_glu", "g_kv", "w_kvf", "b_f", "w_q", "w_o", "w_ffn_in", "ffn_conv_w", "ffn_conv_b", "w_ffn_out",
           "g_final")
_BIG = ("w_glu", "w_kvf", "w_q", "w_o", "w_ffn_in", "w_ffn_out")
_SMALL_SHARDED = ("ssm_d", "ffn_conv_w")


def kernel(x, g_mix, g_ffn, lam_re, lam_im, log_dt, ssm_b_re, ssm_b_im, ssm_c_re, ssm_c_im, ssm_d, w_glu, g_kv, w_kvf, b_f, w_q, w_o, w_ffn_in, ffn_conv_w, ffn_conv_b, w_ffn_out, g_final, loss_target, m_g_mix, m_g_ffn, m_lam_re, m_lam_im, m_log_dt, m_ssm_b_re, m_ssm_b_im, m_ssm_c_re, m_ssm_c_im, m_ssm_d, m_w_glu, m_g_kv, m_w_kvf, m_b_f, m_w_q, m_w_o, m_w_ffn_in, m_ffn_conv_w, m_ffn_conv_b, m_w_ffn_out, m_g_final, v_g_mix, v_g_ffn, v_lam_re, v_lam_im, v_log_dt, v_ssm_b_re, v_ssm_b_im, v_ssm_c_re, v_ssm_c_im, v_ssm_d, v_w_glu, v_g_kv, v_w_kvf, v_b_f, v_w_q, v_w_o, v_w_ffn_in, v_ffn_conv_w, v_ffn_conv_b, v_w_ffn_out, v_g_final):
    p = dict(g_mix=g_mix, g_ffn=g_ffn, lam_re=lam_re, lam_im=lam_im, log_dt=log_dt, ssm_b_re=ssm_b_re,
             ssm_b_im=ssm_b_im, ssm_c_re=ssm_c_re, ssm_c_im=ssm_c_im, ssm_d=ssm_d, w_glu=w_glu, g_kv=g_kv,
             w_kvf=w_kvf, b_f=b_f, w_q=w_q, w_o=w_o, w_ffn_in=w_ffn_in, ffn_conv_w=ffn_conv_w,
             ffn_conv_b=ffn_conv_b, w_ffn_out=w_ffn_out, g_final=g_final)
    mom1 = dict(zip(_PARAMS, (m_g_mix, m_g_ffn, m_lam_re, m_lam_im, m_log_dt, m_ssm_b_re, m_ssm_b_im, m_ssm_c_re,
                              m_ssm_c_im, m_ssm_d, m_w_glu, m_g_kv, m_w_kvf, m_b_f, m_w_q, m_w_o, m_w_ffn_in,
                              m_ffn_conv_w, m_ffn_conv_b, m_w_ffn_out, m_g_final)))
    mom2 = dict(zip(_PARAMS, (v_g_mix, v_g_ffn, v_lam_re, v_lam_im, v_log_dt, v_ssm_b_re, v_ssm_b_im, v_ssm_c_re,
                              v_ssm_c_im, v_ssm_d, v_w_glu, v_g_kv, v_w_kvf, v_b_f, v_w_q, v_w_o, v_w_ffn_in,
                              v_ffn_conv_w, v_ffn_conv_b, v_w_ffn_out, v_g_final)))
    d = x.shape[-1]
    nh = b_f.shape[0]
    slot = 2 * lax.axis_index("x") + lax.axis_index("y")

    shards = [_to_bf16(p[k], name=f"to_bf16_{k}") for k in _BIG] + [p[k] for k in _SMALL_SHARDED]
    kinds = ["C", "C", "R", "R", "C", "R", "C", "C"]
    gl, gkvf, gq, go, gin, gout, gd, gcw = _all_gather(shards, kinds, name="weights_all_gather")
    n_lay = gin.shape[1]
    kvf_full = gkvf.transpose(1, 0, 2).reshape(d, -1)
    w_f = jnp.zeros((d, LANES), BF16).at[:, :nh].set(kvf_full[:, 2 * d:])
    w = dict(p)
    w.update(
        w_glu=gl, w_kv=kvf_full[:, :2 * d][None, None], w_f=w_f[None, None],
        w_q=gq.reshape(1, gq.shape[0], d, d), w_o=go.reshape(1, go.shape[0], d, d), w_in=gin,
        w_out=gout.reshape(1, n_lay, -1, d),
        conv_w=gcw.transpose(1, 2, 0, 3).reshape(n_lay, DT_CONV_TAPS, -1), conv_b=ffn_conv_b,
        ssm_d=gd.transpose(1, 0, 2).reshape(gd.shape[1], d))

    loss_part, grad_x, g = _local_step(x[0], loss_target[0], w)
    loss = lax.psum(loss_part, ("x", "y", "c"))

    g_kvf = jnp.concatenate([g["w_kv"], g["w_f"][:, :nh]], axis=1)
    big = dict(w_glu=g["w_glu"], w_kvf=[g_kvf.reshape(d, N_CHIPS, -1).transpose(1, 0, 2)], w_q=g["w_q"],
               w_o=g["w_o"], w_ffn_in=g["w_in"], w_ffn_out=g["w_out"])
    big_list = [a for k in _BIG for a in big[k]]
    small_names = [k for k in _PARAMS if k not in _BIG]
    small_full = dict(g_mix=g["g_mix"], g_ffn=g["g_ffn"], lam_re=g["lam_re"], lam_im=g["lam_im"], log_dt=g["log_dt"],
                      ssm_b_re=g["ssm_b_re"], ssm_b_im=g["ssm_b_im"], ssm_c_re=g["ssm_c_re"], ssm_c_im=g["ssm_c_im"],
                      ssm_d=g["ssm_d"], g_kv=g["g_kv"], b_f=g["b_f"], ffn_conv_w=g["conv_w"],
                      ffn_conv_b=g["conv_b"], g_final=g["g_final"])
    small_list = [small_full[k] for k in small_names]
    pack = _pack(small_list, N_CHIPS * N_CORES * 2 * SUBLANES)
    pack4 = pack.reshape(N_CHIPS, pack.shape[0] // N_CHIPS, LANES)
    reduced = _reduce_scatter(big_list + [pack4], [BF16] * len(big_list) + [F32])
    red_big, pos = {}, 0
    for k in _BIG:
        layers = reduced[pos:pos + len(big[k])]
        pos += len(big[k])
        red_big[k] = layers[0] if p[k].ndim == 2 else jnp.stack(layers)
    pack_all = _all_gather([reduced[-1]], ["C"], name="small_grads_all_gather")[0]
    red_small = dict(zip(small_names, _unpack(pack_all, small_list)))
    for k in _SMALL_SHARDED:
        width = p[k].shape[-1]
        red_small[k] = lax.dynamic_slice_in_dim(red_small[k], slot * width, width, axis=red_small[k].ndim - 1)

    grads, deltas, new_m, new_v = {}, {}, {}, {}
    for k in _BIG:
        grads[k] = red_big[k]
        deltas[k], new_m[k], new_v[k] = _adamw(p[k], grads[k], mom1[k], mom2[k], name=f"adamw_{k}")
    packs = [_pack([src[k] for k in small_names], SUBLANES) for src in (p, red_small, mom1, mom2)]
    like = [p[k] for k in small_names]
    outs = [_unpack(o, like) for o in _adamw(*packs, name="adamw_small")]
    for i, k in enumerate(small_names):
        grads[k] = red_small[k]
        deltas[k], new_m[k], new_v[k] = outs[0][i], outs[1][i], outs[2][i]
    return (loss, grad_x[None], *[grads[k] for k in _PARAMS], *[deltas[k] for k in _PARAMS],
            *[new_m[k] for k in _PARAMS], *[new_v[k] for k in _PARAMS])
```

```python
import functools

import jax
import jax.numpy as jnp
from jax import lax
from jax.experimental import pallas as pl
from jax.experimental.pallas import tpu as pltpu

F32 = jnp.float32
BF16 = jnp.bfloat16

RMS_EPS = 1e-6
ADAM_LR = 0.001
ADAM_B1 = 0.9
ADAM_B2 = 0.999
ADAM_EPS = 1e-08
ADAM_WD = 0.01
ADAM_STEP = 10
DT_CONV_TAPS = 3

LANES = 128
SUBLANES = 8
HEAD_DIM = 64
FLASH_ROW_TILE = 32
S5_BLOCK_GROUPS = 16
VMEM_LIMIT_BYTES = 48 << 20
MM_BLOCK_BUDGET_BYTES = 30 << 20
N_CHIPS = 4
N_CORES = 2
MESH = pl.DeviceIdType.MESH


def _cp(n_grid):
    return pltpu.CompilerParams(dimension_semantics=("arbitrary",) * n_grid, vmem_limit_bytes=VMEM_LIMIT_BYTES)


def _tile(n, pref, mult=SUBLANES):
    if n <= pref:
        return n
    t = (pref // mult) * mult
    while t >= mult:
        if n % t == 0:
            return t
        t -= mult
    return n


def _row_tile(m, bytes_per_row, fixed_bytes):
    for tm in (1024, 512):
        if m % tm == 0 and 2 * (tm * bytes_per_row + fixed_bytes) <= MM_BLOCK_BUDGET_BYTES:
            return tm
    return _tile(m, 512)


def _dot(a, b, ca, cb):
    return lax.dot_general(a, b, (((ca,), (cb,)), ((), ())), preferred_element_type=F32)


def _mm_cols(x, w4, layer, *, wc, out_dtype=F32, scale=None, name):
    m, k = x.shape
    slots, _, k0, k1 = w4.shape
    nb = k1 if wc == 0 else k0
    assert (k0 if wc == 0 else k1) == k
    tm = _row_tile(m, k * x.dtype.itemsize + nb * jnp.dtype(out_dtype).itemsize, k0 * k1 * w4.dtype.itemsize)

    def body(x_ref, w_ref, o_ref):
        acc = _dot(x_ref[...].astype(BF16), w_ref[0, 0], 1, wc)
        if scale is not None:
            acc = acc * scale
        o_ref[...] = acc.astype(out_dtype)

    return pl.pallas_call(
        body, grid=(slots, m // tm),
        in_specs=[pl.BlockSpec((tm, k), lambda s, i: (i, 0)),
                  pl.BlockSpec((1, 1, k0, k1), lambda s, i: (s, layer, 0, 0))],
        out_specs=pl.BlockSpec((tm, nb), lambda s, i: (i, s)),
        out_shape=jax.ShapeDtypeStruct((m, slots * nb), out_dtype),
        compiler_params=_cp(2), name=name)(x, w4)


def _planes(a):
    return a if a.ndim == 3 else a[None]


def _mm_acc(x, w4, layer, *, wc, name):
    x = _planes(x)
    n_planes, m, width = x.shape
    slots, _, k0, k1 = w4.shape
    kb = k0 if wc == 0 else k1
    nout = k1 if wc == 0 else k0
    assert n_planes * width == slots * kb
    spp = slots // n_planes
    tm = _row_tile(m, kb * x.dtype.itemsize + nout * 4, k0 * k1 * w4.dtype.itemsize)

    def body(x_ref, w_ref, o_ref):
        @pl.when(pl.program_id(1) == 0)
        def _():
            o_ref[...] = jnp.zeros_like(o_ref)
        o_ref[...] += _dot(x_ref[0].astype(BF16), w_ref[0, 0], 1, wc)

    return pl.pallas_call(
        body, grid=(m // tm, slots),
        in_specs=[pl.BlockSpec((1, tm, kb), lambda i, s: (s // spp, i, s % spp)),
                  pl.BlockSpec((1, 1, k0, k1), lambda i, s: (s, layer, 0, 0))],
        out_specs=pl.BlockSpec((tm, nout), lambda i, s: (i, 0)),
        out_shape=jax.ShapeDtypeStruct((m, nout), F32),
        compiler_params=_cp(2), name=name)(x, w4)


def _mm_tn(x, dy, slots, *, scale=None, name):
    m, k = x.shape
    dy = _planes(dy)
    n_planes, _, width = dy.shape
    n = n_planes * width // slots
    spp = slots // n_planes
    ta = _tile(k, 512, LANES)
    tm = m
    while tm > 512 and tm % 2 == 0 and (2 * tm * (ta * x.dtype.itemsize + n * dy.dtype.itemsize)
                                         + 2 * ta * n * 4) > MM_BLOCK_BUDGET_BYTES:
        tm //= 2
    n_m = m // tm

    def body(x_ref, dy_ref, o_ref):
        @pl.when(pl.program_id(2) == 0)
        def _():
            o_ref[...] = jnp.zeros_like(o_ref)
        o_ref[0] += _dot(x_ref[...].astype(BF16), dy_ref[0].astype(BF16), 0, 0)
        if scale is not None:
            @pl.when(pl.program_id(2) == n_m - 1)
            def _():
                o_ref[...] = o_ref[...] * scale

    return pl.pallas_call(
        body, grid=(slots, k // ta, n_m),
        in_specs=[pl.BlockSpec((tm, ta), lambda s, a, i: (i, a)),
                  pl.BlockSpec((1, tm, n), lambda s, a, i: (s // spp, i, s % spp))],
        out_specs=pl.BlockSpec((1, ta, n), lambda s, a, i: (s, a, 0)),
        out_shape=jax.ShapeDtypeStruct((slots, k, n), F32),
        compiler_params=_cp(3), name=name)(x, dy)


def _rowwise(fn, rows, consts, outs, accs=(), *, tl=256, name):
    n_rows = rows[0].shape[0]
    tl = _tile(n_rows, tl)
    n_in = len(rows) + len(consts)
    n_out = len(outs)

    def body(*refs):
        res = fn(*[r[...] for r in refs[:n_in]])
        res = res if isinstance(res, (tuple, list)) else (res,)
        o_refs = refs[n_in:n_in + n_out]
        a_refs = refs[n_in + n_out:]
        for o, val in zip(o_refs, res[:n_out]):
            o[...] = val.astype(o.dtype)
        if a_refs:
            @pl.when(pl.program_id(0) == 0)
            def _():
                for a in a_refs:
                    a[...] = jnp.zeros_like(a)
            for a, val in zip(a_refs, res[n_out:]):
                a[...] += val

    in_specs = ([pl.BlockSpec((tl, r.shape[1]), lambda i: (i, 0)) for r in rows]
                + [pl.BlockSpec(c.shape, lambda i: (0, 0)) for c in consts])
    out_specs = ([pl.BlockSpec((tl, w), lambda i: (i, 0)) for w, _ in outs]
                 + [pl.BlockSpec(s, lambda i: (0, 0)) for s in accs])
    out_shape = ([jax.ShapeDtypeStruct((n_rows, w), dt) for w, dt in outs]
                 + [jax.ShapeDtypeStruct(s, F32) for s in accs])
    return pl.pallas_call(body, grid=(n_rows // tl,), in_specs=in_specs, out_specs=out_specs,
                          out_shape=out_shape, compiler_params=_cp(1), name=name)(*rows, *consts)


def _rms(x, g):
    return x * lax.rsqrt(jnp.mean(x * x, axis=-1, keepdims=True) + RMS_EPS) * g


def _sigmoid(x):
    return 1.0 / (1.0 + jnp.exp(-x))


def _glu(zz):
    d = zz.shape[1] // 2
    return zz[:, :d] * _sigmoid(zz[:, d:])


def _gelu(y):
    return jax.nn.gelu(y)


def _row2(v):
    return v.reshape(1, -1)


def _node_bwd(d_in, h, branches, *, name):
    width = h.shape[1]
    flat = [dy for _, dys in branches for dy in dys]
    counts = [len(dys) for _, dys in branches]
    gains = [_row2(g) for g, _ in branches]

    def fn(d, hh, *rest):
        dys, gs = rest[:len(flat)], rest[len(flat):]
        tot, dgs, pos = d, [], 0
        for g, cnt in zip(gs, counts):
            dy = dys[pos].astype(F32)
            for extra in dys[pos + 1:pos + cnt]:
                dy = dy + extra.astype(F32)
            pos += cnt
            _, vjp = jax.vjp(_rms, hh, g)
            dx, dg = vjp(dy)
            tot = tot + dx
            dgs.append(dg)
        return (tot, tot, *dgs)

    res = _rowwise(fn, [d_in, h, *flat], gains, [(width, F32), (width, BF16)], [(1, width)] * len(branches),
                   name=name)
    return res[0], res[1], [r[0] for r in res[2:]]


def _s5_prep_fn(lr, li, ldt, br, bi, cr, ci, *, gq, h, p):
    dt = jnp.exp(ldt)
    mag = jnp.exp(lr * dt)
    lb_re = mag * jnp.cos(li * dt)
    lb_im = mag * jnp.sin(li * dt)
    den = lr * lr + li * li
    nr = lb_re - 1.0
    fr = (nr * lr + lb_im * li) / den
    fi = (lb_im * lr - nr * li) / den
    bb_re = fr * br - fi * bi
    bb_im = fr * bi + fi * br
    shape = (gq * h, gq * p)
    r = lax.broadcasted_iota(jnp.int32, shape, 0)
    c = lax.broadcasted_iota(jnp.int32, shape, 1)
    mask = jnp.where(jnp.right_shift(r, h.bit_length() - 1) == jnp.right_shift(c, p.bit_length() - 1), 1.0, 0.0)

    def expand(t):
        return jnp.concatenate([t] * gq, axis=0) * mask

    return lb_re, lb_im, expand(bb_re), expand(bb_im), expand(cr), expand(ci)


def _s5_prep(lr, li, ldt, br, bi, cr, ci, p, *, name):
    n = lr.shape[1]
    h = br.shape[0]
    gq = S5_BLOCK_GROUPS
    nq, cq = gq * p, gq * h
    nblk = n // nq
    fn = functools.partial(_s5_prep_fn, gq=gq, h=h, p=p)

    def body(lr_r, li_r, ldt_r, br_r, bi_r, cr_r, ci_r, lbr_o, lbi_o, wbr_o, wbi_o, wcr_o, wci_o):
        lb_re, lb_im, wbr, wbi, wcr, wci = fn(lr_r[...], li_r[...], ldt_r[...], br_r[...], bi_r[...],
                                              cr_r[...], ci_r[...])
        lbr_o[...] = lb_re
        lbi_o[...] = lb_im
        wbr_o[0] = wbr.astype(BF16)
        wbi_o[0] = wbi.astype(BF16)
        wcr_o[0] = wcr.astype(BF16)
        wci_o[0] = wci.astype(BF16)

    vec = pl.BlockSpec((1, nq), lambda q: (0, q))
    tab = pl.BlockSpec((h, nq), lambda q: (0, q))
    wsp = pl.BlockSpec((1, cq, nq), lambda q: (q, 0, 0))
    wsh = jax.ShapeDtypeStruct((nblk, cq, nq), BF16)
    vsh = jax.ShapeDtypeStruct((1, n), F32)
    return pl.pallas_call(body, grid=(nblk,), in_specs=[vec, vec, vec, tab, tab, tab, tab],
                          out_specs=[vec, vec, wsp, wsp, wsp, wsp], out_shape=[vsh, vsh, wsh, wsh, wsh, wsh],
                          compiler_params=_cp(1), name=name)(lr, li, ldt, br, bi, cr, ci)


def _s5_prep_bwd(lr, li, ldt, br, bi, cr, ci, p, dlbr, dlbi, dwbr, dwbi, dwcr, dwci, *, name):
    n = lr.shape[1]
    h = br.shape[0]
    gq = S5_BLOCK_GROUPS
    nq, cq = gq * p, gq * h
    nblk = n // nq
    fn = functools.partial(_s5_prep_fn, gq=gq, h=h, p=p)

    def body(lr_r, li_r, ldt_r, br_r, bi_r, cr_r, ci_r, dlbr_r, dlbi_r, dwbr_r, dwbi_r, dwcr_r, dwci_r,
             *outs):
        _, vjp = jax.vjp(fn, lr_r[...], li_r[...], ldt_r[...], br_r[...], bi_r[...], cr_r[...], ci_r[...])
        grads = vjp((dlbr_r[0], dlbi_r[0], dwbr_r[0], dwbi_r[0], dwcr_r[0], dwci_r[0]))
        for o, g in zip(outs, grads):
            o[...] = g

    vec = pl.BlockSpec((1, nq), lambda q: (0, q))
    tab = pl.BlockSpec((h, nq), lambda q: (0, q))
    vec3 = pl.BlockSpec((1, 1, nq), lambda q: (q, 0, 0))
    wsp = pl.BlockSpec((1, cq, nq), lambda q: (q, 0, 0))
    vsh = jax.ShapeDtypeStruct((1, n), F32)
    tsh = jax.ShapeDtypeStruct((h, n), F32)
    return pl.pallas_call(body, grid=(nblk,),
                          in_specs=[vec, vec, vec, tab, tab, tab, tab, vec3, vec3, wsp, wsp, wsp, wsp],
                          out_specs=[vec, vec, vec, tab, tab, tab, tab],
                          out_shape=[vsh, vsh, vsh, tsh, tsh, tsh, tsh],
                          compiler_params=_cp(1), name=name)(lr, li, ldt, br, bi, cr, ci,
                                                             dlbr, dlbi, dwbr, dwbi, dwcr, dwci)


def _scan_rows(s_re, s_im, a_re, a_im, c_re, c_im, *, reverse):
    t_rows, n = s_re.shape
    nb = t_rows // SUBLANES
    row = lax.broadcasted_iota(jnp.int32, (SUBLANES, n), 0)

    def cmul(x, y):
        return x[0] * y[0] - x[1] * y[1], x[0] * y[1] + x[1] * y[0]

    a1 = (jnp.broadcast_to(a_re, (SUBLANES, n)), jnp.broadcast_to(a_im, (SUBLANES, n)))
    a2 = cmul(a1, a1)
    a4 = cmul(a2, a2)
    pk = (a_re, a_im)
    tab_re = jnp.zeros((SUBLANES, n), F32)
    tab_im = jnp.zeros((SUBLANES, n), F32)
    for i in range(SUBLANES):
        at = (SUBLANES - 1 - i) if reverse else i
        tab_re = jnp.where(row == at, pk[0], tab_re)
        tab_im = jnp.where(row == at, pk[1], tab_im)
        pk = cmul(pk, (a_re, a_im))

    def step(b, carry):
        cr, ci = carry
        blk = (nb - 1 - b) if reverse else b
        off = pl.multiple_of(blk * SUBLANES, SUBLANES)
        x_re = s_re[pl.ds(off, SUBLANES), :]
        x_im = s_im[pl.ds(off, SUBLANES), :]
        for d, (pr, pi) in ((1, a1), (2, a2), (4, a4)):
            if reverse:
                keep = row < SUBLANES - d
                sh = SUBLANES - d
            else:
                keep = row >= d
                sh = d
            sh_re = jnp.where(keep, pltpu.roll(x_re, sh, 0), 0.0)
            sh_im = jnp.where(keep, pltpu.roll(x_im, sh, 0), 0.0)
            x_re, x_im = x_re + pr * sh_re - pi * sh_im, x_im + pr * sh_im + pi * sh_re
        x_re, x_im = x_re + tab_re * cr - tab_im * ci, x_im + tab_re * ci + tab_im * cr
        s_re[pl.ds(off, SUBLANES), :] = x_re
        s_im[pl.ds(off, SUBLANES), :] = x_im
        edge = 0 if reverse else SUBLANES - 1
        return x_re[edge:edge + 1, :], x_im[edge:edge + 1, :]

    return lax.fori_loop(0, nb, step, (c_re, c_im))


def _s5_fwd(u, prep, dskip, *, name):
    lb_re, lb_im, wbr, wbi, wcr, wci = prep
    n_rows, _ = u.shape
    nblk, cq, nq = wbr.shape
    tt = _tile(n_rows, 512)
    nch = n_rows // tt

    def body(u_ref, wbr_r, wbi_r, wcr_r, wci_r, lbr_r, lbi_r, d_ref, y_ref, s_re, s_im, sbr_o, sbi_o, c_re, c_im):
        @pl.when(pl.program_id(1) == 0)
        def _():
            c_re[...] = jnp.zeros_like(c_re)
            c_im[...] = jnp.zeros_like(c_im)
        uf = u_ref[...]
        ub = uf.astype(BF16)
        s_re[...] = _dot(ub, wbr_r[0], 1, 0)
        s_im[...] = _dot(ub, wbi_r[0], 1, 0)
        sbr_o[0] = c_re[...]
        sbi_o[0] = c_im[...]
        cr, ci = _scan_rows(s_re, s_im, lbr_r[...], lbi_r[...], c_re[...], c_im[...], reverse=False)
        c_re[...] = cr
        c_im[...] = ci
        y = _dot(s_re[...].astype(BF16), wcr_r[0], 1, 1) - _dot(s_im[...].astype(BF16), wci_r[0], 1, 1)
        y_ref[...] = y + d_ref[...] * uf

    wsp = pl.BlockSpec((1, cq, nq), lambda q, i: (q, 0, 0))
    vec = pl.BlockSpec((1, nq), lambda q, i: (0, q))
    act = pl.BlockSpec((tt, cq), lambda q, i: (i, q))
    sb = pl.BlockSpec((1, 1, nq), lambda q, i: (i, 0, q))
    sbsh = jax.ShapeDtypeStruct((nch, 1, nblk * nq), F32)
    states = pl.BlockSpec((tt, nq), lambda q, i: (i, q))
    stsh = jax.ShapeDtypeStruct((n_rows, nblk * nq), F32)
    return pl.pallas_call(
        body, grid=(nblk, nch),
        in_specs=[act, wsp, wsp, wsp, wsp, vec, vec, pl.BlockSpec((1, cq), lambda q, i: (0, q))],
        out_specs=[act, states, states, sb, sb],
        out_shape=[jax.ShapeDtypeStruct(u.shape, F32), stsh, stsh, sbsh, sbsh],
        scratch_shapes=[pltpu.VMEM((1, nq), F32), pltpu.VMEM((1, nq), F32)],
        compiler_params=_cp(2), name=name)(u, wbr, wbi, wcr, wci, lb_re, lb_im, dskip)


def _s5_bwd(u, dy, st_re, st_im, sb_re, sb_im, prep, dskip, *, name):
    lb_re, lb_im, wbr, wbi, wcr, wci = prep
    n_rows, _ = u.shape
    nblk, cq, nq = wbr.shape
    tt = _tile(n_rows, 512)
    nch = n_rows // tt

    def body(u_ref, dy_ref, s_re, s_im, sbr_r, sbi_r, wbr_r, wbi_r, wcr_r, wci_r, lbr_r, lbi_r, d_ref,
             du_ref, dwbr, dwbi, dwcr, dwci, dlbr, dlbi, dd_ref, g_re, g_im, lc_re, lc_im):
        @pl.when(pl.program_id(1) == 0)
        def _():
            for ref in (lc_re, lc_im, dwbr, dwbi, dwcr, dwci, dlbr, dlbi, dd_ref):
                ref[...] = jnp.zeros_like(ref)
        uf = u_ref[...]
        ub = uf.astype(BF16)
        dyf = dy_ref[...]
        dyb = dyf.astype(BF16)
        sr16 = s_re[...].astype(BF16)
        si16 = s_im[...].astype(BF16)
        dwcr[0] += _dot(dyb, sr16, 0, 0)
        dwci[0] -= _dot(dyb, si16, 0, 0)
        g_re[...] = _dot(dyb, wcr_r[0], 1, 0)
        g_im[...] = -_dot(dyb, wci_r[0], 1, 0)
        lcr, lci = _scan_rows(g_re, g_im, lbr_r[...], -lbi_r[...], lc_re[...], lc_im[...], reverse=True)
        lc_re[...] = lcr
        lc_im[...] = lci
        lam_r = g_re[...]
        lam_i = g_im[...]
        first = lax.broadcasted_iota(jnp.int32, (tt, nq), 0) == 0
        prev_r = jnp.where(first, sbr_r[0], pltpu.roll(s_re[...], 1, 0))
        prev_i = jnp.where(first, sbi_r[0], pltpu.roll(s_im[...], 1, 0))
        dlbr[0] += jnp.sum(lam_r * prev_r + lam_i * prev_i, axis=0, keepdims=True)
        dlbi[0] += jnp.sum(lam_i * prev_r - lam_r * prev_i, axis=0, keepdims=True)
        lr16 = lam_r.astype(BF16)
        li16 = lam_i.astype(BF16)
        du_ref[...] = _dot(lr16, wbr_r[0], 1, 1) + _dot(li16, wbi_r[0], 1, 1) + d_ref[...] * dyf
        dwbr[0] += _dot(ub, lr16, 0, 0)
        dwbi[0] += _dot(ub, li16, 0, 0)
        dd_ref[0] += jnp.sum(dyf * uf, axis=0, keepdims=True)

    last = nch - 1
    wsp = pl.BlockSpec((1, cq, nq), lambda q, i: (q, 0, 0))
    vec = pl.BlockSpec((1, nq), lambda q, i: (0, q))
    act = pl.BlockSpec((tt, cq), lambda q, i: (last - i, q))
    sb = pl.BlockSpec((1, 1, nq), lambda q, i: (last - i, 0, q))
    vec3 = pl.BlockSpec((1, 1, nq), lambda q, i: (q, 0, 0))
    dsp = pl.BlockSpec((1, 1, cq), lambda q, i: (q, 0, 0))
    wsh = jax.ShapeDtypeStruct((nblk, cq, nq), F32)
    v3sh = jax.ShapeDtypeStruct((nblk, 1, nq), F32)
    big = pltpu.VMEM((tt, nq), F32)
    states = pl.BlockSpec((tt, nq), lambda q, i: (last - i, q))
    return pl.pallas_call(
        body, grid=(nblk, nch),
        in_specs=[act, act, states, states, sb, sb, wsp, wsp, wsp, wsp, vec, vec,
                  pl.BlockSpec((1, cq), lambda q, i: (0, q))],
        out_specs=[act, wsp, wsp, wsp, wsp, vec3, vec3, dsp],
        out_shape=[jax.ShapeDtypeStruct(u.shape, F32), wsh, wsh, wsh, wsh, v3sh, v3sh,
                   jax.ShapeDtypeStruct((nblk, 1, cq), F32)],
        scratch_shapes=[big, big, pltpu.VMEM((1, nq), F32), pltpu.VMEM((1, nq), F32)],
        compiler_params=_cp(2), name=name)(u, dy, st_re, st_im, sb_re, sb_im, wbr, wbi, wcr, wci, lb_re, lb_im,
                                           dskip)


def _conv_taps(cur, prev, w, b):
    rid = lax.broadcasted_iota(jnp.int32, cur.shape, 0)
    x1 = jnp.where(rid == 0, prev[7:8, :], pltpu.roll(cur, 1, 0))
    x2 = jnp.where(rid == 0, prev[6:7, :], jnp.where(rid == 1, prev[7:8, :], pltpu.roll(cur, 2, 0)))
    return b + x2 * w[0:1, :] + x1 * w[1:2, :] + cur * w[2:3, :], x1, x2


def _conv_fwd(uu, cw, cb, *, name):
    n_rows, f2 = uu.shape
    f = f2 // 2
    tc = _tile(f, 1408, LANES)
    tl = _tile(n_rows, 256)
    nfb = f // tc

    def body(g_ref, u_ref, wg_ref, wu_ref, bg_ref, bu_ref, o_ref, pg, pu):
        @pl.when(pl.program_id(1) == 0)
        def _():
            pg[...] = jnp.zeros_like(pg)
            pu[...] = jnp.zeros_like(pu)
        gcur = g_ref[...]
        ucur = u_ref[...]
        cg, _, _ = _conv_taps(gcur, pg[...], wg_ref[...], bg_ref[...])
        cu, _, _ = _conv_taps(ucur, pu[...], wu_ref[...], bu_ref[...])
        o_ref[...] = (cg * _sigmoid(cg) * cu).astype(o_ref.dtype)
        pg[...] = gcur[tl - SUBLANES:, :]
        pu[...] = ucur[tl - SUBLANES:, :]

    return pl.pallas_call(
        body, grid=(nfb, n_rows // tl),
        in_specs=[pl.BlockSpec((tl, tc), lambda j, i: (i, j)), pl.BlockSpec((tl, tc), lambda j, i: (i, j + nfb)),
                  pl.BlockSpec((DT_CONV_TAPS, tc), lambda j, i: (0, j)),
                  pl.BlockSpec((DT_CONV_TAPS, tc), lambda j, i: (0, j + nfb)),
                  pl.BlockSpec((1, tc), lambda j, i: (0, j)), pl.BlockSpec((1, tc), lambda j, i: (0, j + nfb))],
        out_specs=pl.BlockSpec((tl, tc), lambda j, i: (i, j)),
        out_shape=jax.ShapeDtypeStruct((n_rows, f), BF16),
        scratch_shapes=[pltpu.VMEM((SUBLANES, tc), F32), pltpu.VMEM((SUBLANES, tc), F32)],
        compiler_params=_cp(2), name=name)(uu, uu, cw, cw, cb, cb)


def _conv_bwd(uu, dact, cw, cb, *, name):
    n_rows, f2 = uu.shape
    f = f2 // 2
    tc = _tile(f, 1408, LANES)
    tl = _tile(n_rows, 256)
    nfb = f // tc
    nrb = n_rows // tl
    halo_per_tile = tl // SUBLANES

    def body(g_ref, gh_ref, u_ref, uh_ref, da_ref, wg_ref, wu_ref, bg_ref, bu_ref,
             duu_ref, dw_ref, db_ref, nxt_g, nxt_u):
        i = pl.program_id(1)
        rb = nrb - 1 - i

        @pl.when(i == 0)
        def _():
            for ref in (nxt_g, nxt_u, dw_ref, db_ref):
                ref[...] = jnp.zeros_like(ref)
        has_prev = jnp.where(rb > 0, 1.0, 0.0)
        gcur, ucur = g_ref[...], u_ref[...]
        wg, wu = wg_ref[...], wu_ref[...]
        cg, g1, g2 = _conv_taps(gcur, gh_ref[...] * has_prev, wg, bg_ref[...])
        cu, u1, u2 = _conv_taps(ucur, uh_ref[...] * has_prev, wu, bu_ref[...])
        sg = _sigmoid(cg)
        silu = cg * sg
        da = da_ref[...]
        rid = lax.broadcasted_iota(jnp.int32, da.shape, 0)

        def transpose_conv(plane, d, cur, x1, x2, w, nxt):
            nx = nxt[...]
            d1 = jnp.where(rid == tl - 1, nx[0:1, :], pltpu.roll(d, tl - 1, 0))
            d2 = jnp.where(rid == tl - 2, nx[0:1, :],
                           jnp.where(rid == tl - 1, nx[1:2, :], pltpu.roll(d, tl - 2, 0)))
            duu_ref[plane] = (w[2:3, :] * d + w[1:2, :] * d1 + w[0:1, :] * d2).astype(duu_ref.dtype)
            nxt[...] = d[0:SUBLANES, :]
            dw_ref[plane] += jnp.concatenate([jnp.sum(d * x2, axis=0, keepdims=True),
                                              jnp.sum(d * x1, axis=0, keepdims=True),
                                              jnp.sum(d * cur, axis=0, keepdims=True)], axis=0)
            db_ref[plane] += jnp.sum(d, axis=0, keepdims=True)

        transpose_conv(0, da * cu * (sg * (1.0 + cg * (1.0 - sg))), gcur, g1, g2, wg, nxt_g)
        transpose_conv(1, da * silu, ucur, u1, u2, wu, nxt_u)

    def halo(j, i):
        return jnp.maximum((nrb - 1 - i) * halo_per_tile - 1, 0)

    return pl.pallas_call(
        body, grid=(nfb, nrb),
        in_specs=[pl.BlockSpec((tl, tc), lambda j, i: (nrb - 1 - i, j)),
                  pl.BlockSpec((SUBLANES, tc), lambda j, i: (halo(j, i), j)),
                  pl.BlockSpec((tl, tc), lambda j, i: (nrb - 1 - i, j + nfb)),
                  pl.BlockSpec((SUBLANES, tc), lambda j, i: (halo(j, i), j + nfb)),
                  pl.BlockSpec((tl, tc), lambda j, i: (nrb - 1 - i, j)),
                  pl.BlockSpec((DT_CONV_TAPS, tc), lambda j, i: (0, j)),
                  pl.BlockSpec((DT_CONV_TAPS, tc), lambda j, i: (0, j + nfb)),
                  pl.BlockSpec((1, tc), lambda j, i: (0, j)),
                  pl.BlockSpec((1, tc), lambda j, i: (0, j + nfb))],
        out_specs=[pl.BlockSpec((2, tl, tc), lambda j, i: (0, nrb - 1 - i, j)),
                   pl.BlockSpec((2, DT_CONV_TAPS, tc), lambda j, i: (0, 0, j)),
                   pl.BlockSpec((2, 1, tc), lambda j, i: (0, 0, j))],
        out_shape=[jax.ShapeDtypeStruct((2, n_rows, f), BF16), jax.ShapeDtypeStruct((2, DT_CONV_TAPS, f), F32),
                   jax.ShapeDtypeStruct((2, 1, f), F32)],
        scratch_shapes=[pltpu.VMEM((SUBLANES, tc), F32), pltpu.VMEM((SUBLANES, tc), F32)],
        compiler_params=_cp(2), name=name)(uu, uu, uu, uu, dact, cw, cw, cb, cb)


def _log_sigmoid(x):
    t = jnp.exp(-jnp.abs(x))
    log1p_t = jnp.where(t < 1e-3, t * (1.0 - t * (0.5 - t * (1.0 / 3.0))), jnp.log(1.0 + t))
    return jnp.minimum(x, 0.0) - log1p_t


def _dlog_sigmoid(x):
    t = jnp.exp(-jnp.abs(x))
    return jnp.where(x >= 0, t, 1.0) / (1.0 + t)


def _tri_dot(tri, x):
    return jnp.dot(tri, x, precision=lax.Precision.HIGHEST, preferred_element_type=F32)


def _cum_fwd(fl, bf, *, name):
    n_rows, width = fl.shape
    tc = _tile(n_rows, 256)

    def body(fl_ref, bf_ref, o_ref, carry):
        @pl.when(pl.program_id(0) == 0)
        def _():
            carry[...] = jnp.zeros_like(carry)
        x = _log_sigmoid(fl_ref[...] + bf_ref[...])
        r = lax.broadcasted_iota(jnp.int32, (tc, tc), 0)
        c = lax.broadcasted_iota(jnp.int32, (tc, tc), 1)
        y = _tri_dot(jnp.where(r >= c, 1.0, 0.0), x) + carry[...]
        o_ref[...] = y
        carry[...] = y[tc - 1:tc, :]

    return pl.pallas_call(
        body, grid=(n_rows // tc,),
        in_specs=[pl.BlockSpec((tc, width), lambda i: (i, 0)), pl.BlockSpec((1, width), lambda i: (0, 0))],
        out_specs=pl.BlockSpec((tc, width), lambda i: (i, 0)),
        out_shape=jax.ShapeDtypeStruct(fl.shape, F32),
        scratch_shapes=[pltpu.VMEM((1, width), F32)], compiler_params=_cp(1), name=name)(fl, bf)


def _cum_bwd(dcum, fl, bf, *, name):
    n_rows, width = fl.shape
    tc = _tile(n_rows, 256)
    last = n_rows // tc - 1

    def body(dc_ref, fl_ref, bf_ref, dfl_ref, dbf_ref, carry):
        @pl.when(pl.program_id(0) == 0)
        def _():
            carry[...] = jnp.zeros_like(carry)
            dbf_ref[...] = jnp.zeros_like(dbf_ref)
        r = lax.broadcasted_iota(jnp.int32, (tc, tc), 0)
        c = lax.broadcasted_iota(jnp.int32, (tc, tc), 1)
        dls = _tri_dot(jnp.where(r <= c, 1.0, 0.0), dc_ref[...]) + carry[...]
        carry[...] = dls[0:1, :]
        dfl = dls * _dlog_sigmoid(fl_ref[...] + bf_ref[...])
        dfl_ref[...] = dfl.astype(dfl_ref.dtype)
        dbf_ref[...] += jnp.sum(dfl, axis=0, keepdims=True)

    return pl.pallas_call(
        body, grid=(n_rows // tc,),
        in_specs=[pl.BlockSpec((tc, width), lambda i: (last - i, 0)),
                  pl.BlockSpec((tc, width), lambda i: (last - i, 0)),
                  pl.BlockSpec((1, width), lambda i: (0, 0))],
        out_specs=[pl.BlockSpec((tc, width), lambda i: (last - i, 0)), pl.BlockSpec((1, width), lambda i: (0, 0))],
        out_shape=[jax.ShapeDtypeStruct(fl.shape, BF16), jax.ShapeDtypeStruct((1, width), F32)],
        scratch_shapes=[pltpu.VMEM((1, width), F32)], compiler_params=_cp(1), name=name)(dcum, fl, bf)


def _head_masks():
    lane = lax.broadcasted_iota(jnp.int32, (1, LANES), 1)
    return (lane < HEAD_DIM, lane >= HEAD_DIM)


def _flash_fwd(q, kv, cq3, ck3, *, tq, name):
    n_rows, d = q.shape
    nhp = d // LANES
    tk = tq
    nq = n_rows // tq
    nk = n_rows // tk

    rt = _tile(tq, FLASH_ROW_TILE)

    def body(qi_ref, kj_ref, q_ref, k_ref, v_ref, cq_ref, ck_ref, o_ref, lse_ref, m0, m1, l0, l1, acc,
             s0, s1, p0, p1):
        i = qi_ref[pl.program_id(1)]
        j = kj_ref[pl.program_id(1)]
        ms, ls = (m0, m1), (l0, l1)

        @pl.when(j == 0)
        def _():
            for h in range(2):
                ms[h][...] = jnp.full_like(ms[h], -jnp.inf)
                ls[h][...] = jnp.zeros_like(ls[h])
            acc[...] = jnp.zeros_like(acc)

        def block(diagonal):
            qv, kk, vv = q_ref[...], k_ref[...], v_ref[...]
            a = acc[...]
            for h, msk in enumerate(_head_masks()):
                s_sc, p_sc = ((s0, p0), (s1, p1))[h]
                s_sc[...] = _dot(jnp.where(msk, qv, jnp.zeros_like(qv)), kk, 1, 1)
                bias = cq_ref[0, 0:1, h:h + 1] - ck_ref[0, h:h + 1, :]
                m_old, l_old = ms[h][...], ls[h][...]
                m_tiles, sum_tiles = [], []
                for r in range(tq // rt):
                    rows = slice(r * rt, (r + 1) * rt)
                    s = s_sc[rows, :] + bias
                    if diagonal:
                        rr = r * rt + lax.broadcasted_iota(jnp.int32, (rt, tk), 0)
                        cc = lax.broadcasted_iota(jnp.int32, (rt, tk), 1)
                        s = jnp.where(cc <= rr, s, -jnp.inf)
                    s_sc[rows, :] = s
                    m_tiles.append(jnp.maximum(m_old[rows, :], jnp.max(s, axis=1, keepdims=True)))
                for r in range(tq // rt):
                    rows = slice(r * rt, (r + 1) * rt)
                    p = jnp.exp(s_sc[rows, :] - jnp.tile(m_tiles[r], (1, tk // LANES)))
                    sum_tiles.append(jnp.sum(p, axis=1, keepdims=True))
                    p_sc[rows, :] = p.astype(BF16)
                m_new = jnp.concatenate(m_tiles, axis=0)
                alpha = jnp.exp(m_old - m_new)
                ms[h][...] = m_new
                ls[h][...] = alpha * l_old + jnp.concatenate(sum_tiles, axis=0)
                pv = _dot(p_sc[...], jnp.where(msk, vv, jnp.zeros_like(vv)), 1, 0)
                a = a * jnp.where(msk, alpha, 1.0) + pv
            acc[...] = a

        pl.when(j < i)(functools.partial(block, False))
        pl.when(j == i)(functools.partial(block, True))

        @pl.when(j == i)
        def _():
            m_a, m_b = _head_masks()
            o_ref[...] = acc[...] * jnp.where(m_a, 1.0 / l0[...], 1.0 / l1[...])
            two = lax.broadcasted_iota(jnp.int32, (tq, 2), 1)
            lse_a = (m0[...] + jnp.log(l0[...]))[:, 0:1]
            lse_b = (m1[...] + jnp.log(l1[...]))[:, 0:1]
            lse_ref[0] = jnp.where(two == 0, lse_a, lse_b)

    pairs = [(i, j) for i in range(nq) for j in range(i + 1)]
    qi = jnp.asarray([i for i, _ in pairs], jnp.int32)
    kj = jnp.asarray([j for _, j in pairs], jnp.int32)
    col = pltpu.VMEM((tq, LANES), F32)
    return pl.pallas_call(
        body,
        grid_spec=pltpu.PrefetchScalarGridSpec(
            num_scalar_prefetch=2, grid=(nhp, len(pairs)),
            in_specs=[pl.BlockSpec((tq, LANES), lambda hp, t, qi, kj: (qi[t], hp)),
                      pl.BlockSpec((tk, LANES), lambda hp, t, qi, kj: (kj[t], hp)),
                      pl.BlockSpec((tk, LANES), lambda hp, t, qi, kj: (kj[t], nhp + hp)),
                      pl.BlockSpec((1, tq, 2), lambda hp, t, qi, kj: (hp, qi[t], 0)),
                      pl.BlockSpec((1, 2, tk), lambda hp, t, qi, kj: (hp, 0, kj[t]))],
            out_specs=[pl.BlockSpec((tq, LANES), lambda hp, t, qi, kj: (qi[t], hp)),
                       pl.BlockSpec((1, tq, 2), lambda hp, t, qi, kj: (hp, qi[t], 0))],
            scratch_shapes=[col, col, col, col, pltpu.VMEM((tq, LANES), F32), pltpu.VMEM((tq, tk), F32),
                            pltpu.VMEM((tq, tk), F32), pltpu.VMEM((tq, tk), BF16), pltpu.VMEM((tq, tk), BF16)]),
        out_shape=[jax.ShapeDtypeStruct((n_rows, d), F32), jax.ShapeDtypeStruct((nhp, n_rows, 2), F32)],
        compiler_params=_cp(2), name=name)(qi, kj, q, kv, kv, cq3, ck3)


def _flash_bwd(q, kv, o, lse, do, cq3, ck3, *, tq, name):
    n_rows, d = q.shape
    nhp = d // LANES
    tk = tq
    nq = n_rows // tq
    nk = n_rows // tk

    rt = _tile(tq, FLASH_ROW_TILE)

    def body(qi_ref, kj_ref, q_ref, k_ref, v_ref, o_ref, lse_ref, do_ref, cq_ref, ck_ref,
             dq_ref, dk_ref, dv_ref, dck_ref, dcq_ref, s0, s1, dp0, dp1, p0, p1, ds0, ds1, dl0, dl1, lb0, lb1):
        i = qi_ref[pl.program_id(1)]
        j = kj_ref[pl.program_id(1)]

        @pl.when(pl.program_id(1) == 0)
        def _():
            dq_ref[...] = jnp.zeros_like(dq_ref)
            dcq_ref[...] = jnp.zeros_like(dcq_ref)

        @pl.when(i == j)
        def _():
            dk_ref[...] = jnp.zeros_like(dk_ref)
            dv_ref[...] = jnp.zeros_like(dv_ref)
            dck_ref[...] = jnp.zeros_like(dck_ref)

        def block(diagonal):
            qv, kk, vv = q_ref[...], k_ref[...], v_ref[...]
            dob = do_ref[...].astype(BF16)
            prod = dob.astype(F32) * o_ref[...]
            off = pl.multiple_of(i * tq, tq)
            lse_all = lse_ref[0]
            dck_rows, row_sums = [], []
            dq_acc = jnp.zeros((tq, LANES), F32)
            dk_acc = jnp.zeros((tk, LANES), F32)
            dv_acc = jnp.zeros((tk, LANES), F32)
            for h, msk in enumerate(_head_masks()):
                s_sc, dp_sc, p_sc, ds_sc = ((s0, dp0, p0, ds0), (s1, dp1, p1, ds1))[h]
                qh = jnp.where(msk, qv, jnp.zeros_like(qv))
                kh = jnp.where(msk, kk, jnp.zeros_like(kk))
                doh = jnp.where(msk, dob, jnp.zeros_like(dob))
                s_sc[...] = _dot(qh, kk, 1, 1)
                dp_sc[...] = _dot(doh, vv, 1, 1)
                bias = cq_ref[0, 0:1, h:h + 1] - ck_ref[0, h:h + 1, :]
                delta, lse_h = ((dl0, lb0), (dl1, lb1))[h]
                delta[...] = jnp.broadcast_to(jnp.sum(jnp.where(msk, prod, 0.0), axis=1, keepdims=True),
                                              (tq, LANES))
                lse_h[...] = jnp.broadcast_to(lse_all[:, h:h + 1], (tq, LANES))
                reps = (1, tk // LANES)
                col_acc = jnp.zeros((SUBLANES, tk), F32)
                rs = []
                for r in range(tq // rt):
                    rows = slice(r * rt, (r + 1) * rt)
                    s = s_sc[rows, :] + bias
                    if diagonal:
                        rr = r * rt + lax.broadcasted_iota(jnp.int32, (rt, tk), 0)
                        cc = lax.broadcasted_iota(jnp.int32, (rt, tk), 1)
                        s = jnp.where(cc <= rr, s, -jnp.inf)
                    p = jnp.exp(s - jnp.tile(lse_h[rows, :], reps))
                    ds = p * (dp_sc[rows, :] - jnp.tile(delta[rows, :], reps))
                    rs.append(jnp.sum(ds, axis=1, keepdims=True))
                    for g in range(rt // SUBLANES):
                        col_acc = col_acc + ds[g * SUBLANES:(g + 1) * SUBLANES, :]
                    p_sc[rows, :] = p.astype(BF16)
                    ds_sc[rows, :] = ds.astype(BF16)
                dck_rows.append(-jnp.sum(col_acc, axis=0, keepdims=True))
                row_sums.append(jnp.concatenate(rs, axis=0))
                dv_acc = dv_acc + _dot(p_sc[...], doh, 0, 0)
                dsb = ds_sc[...]
                dq_acc = dq_acc + _dot(dsb, kh, 1, 0)
                dk_acc = dk_acc + _dot(dsb, qh, 0, 0)
            dq_ref[pl.ds(off, tq), :] += dq_acc
            dk_ref[...] += dk_acc
            dv_ref[...] += dv_acc
            two = lax.broadcasted_iota(jnp.int32, (tq, 2), 1)
            dcq_ref[0, pl.ds(off, tq), :] += jnp.where(two == 0, row_sums[0], row_sums[1])
            dck_ref[0] += jnp.concatenate(dck_rows, axis=0)

        pl.when(i > j)(functools.partial(block, False))
        pl.when(i == j)(functools.partial(block, True))

    pairs = [(i, j) for j in range(nk) for i in range(j, nq)]
    qi = jnp.asarray([i for i, _ in pairs], jnp.int32)
    kj = jnp.asarray([j for _, j in pairs], jnp.int32)
    score = pltpu.VMEM((tq, tk), F32)
    score16 = pltpu.VMEM((tq, tk), BF16)
    rowstat = pltpu.VMEM((tq, LANES), F32)
    return pl.pallas_call(
        body,
        grid_spec=pltpu.PrefetchScalarGridSpec(
            num_scalar_prefetch=2, grid=(nhp, len(pairs)),
            in_specs=[pl.BlockSpec((tq, LANES), lambda hp, t, qi, kj: (qi[t], hp)),
                      pl.BlockSpec((tk, LANES), lambda hp, t, qi, kj: (kj[t], hp)),
                      pl.BlockSpec((tk, LANES), lambda hp, t, qi, kj: (kj[t], nhp + hp)),
                      pl.BlockSpec((tq, LANES), lambda hp, t, qi, kj: (qi[t], hp)),
                      pl.BlockSpec((1, tq, 2), lambda hp, t, qi, kj: (hp, qi[t], 0)),
                      pl.BlockSpec((tq, LANES), lambda hp, t, qi, kj: (qi[t], hp)),
                      pl.BlockSpec((1, tq, 2), lambda hp, t, qi, kj: (hp, qi[t], 0)),
                      pl.BlockSpec((1, 2, tk), lambda hp, t, qi, kj: (hp, 0, kj[t]))],
            out_specs=[pl.BlockSpec((n_rows, LANES), lambda hp, t, qi, kj: (0, hp)),
                       pl.BlockSpec((tk, LANES), lambda hp, t, qi, kj: (kj[t], hp)),
                       pl.BlockSpec((tk, LANES), lambda hp, t, qi, kj: (kj[t], hp)),
                       pl.BlockSpec((1, 2, tk), lambda hp, t, qi, kj: (hp, 0, kj[t])),
                       pl.BlockSpec((1, n_rows, 2), lambda hp, t, qi, kj: (hp, 0, 0))],
            scratch_shapes=[score, score, score, score, score16, score16, score16, score16, rowstat, rowstat,
                            rowstat, rowstat]),
        out_shape=[jax.ShapeDtypeStruct((n_rows, d), F32), jax.ShapeDtypeStruct((n_rows, d), F32),
                   jax.ShapeDtypeStruct((n_rows, d), F32), jax.ShapeDtypeStruct((nhp, 2, n_rows), F32),
                   jax.ShapeDtypeStruct((nhp, n_rows, 2), F32)],
        compiler_params=_cp(2), name=name)(qi, kj, q, kv, kv, o, lse, do, cq3, ck3)


def _s5_tables(w, layer):
    g, p = w["lam_re"].shape[1:]
    h = w["ssm_b_re"].shape[3]
    n = g * p
    lr = w["lam_re"][layer].reshape(1, n)
    li = w["lam_im"][layer].reshape(1, n)
    ldt = jnp.broadcast_to(w["log_dt"][layer][:, None], (g, p)).reshape(1, n)
    br = w["ssm_b_re"][layer].transpose(2, 0, 1).reshape(h, n)
    bi = w["ssm_b_im"][layer].transpose(2, 0, 1).reshape(h, n)
    cr = w["ssm_c_re"][layer].transpose(1, 0, 2).reshape(h, n)
    ci = w["ssm_c_im"][layer].transpose(1, 0, 2).reshape(h, n)
    return (lr, li, ldt, br, bi, cr, ci), (g, p, h)


def _local_step(x, tgt, w, *, attn_tile=512):
    n_rows, d = x.shape
    n_layers = w["g_mix"].shape[0]
    n_s5 = w["lam_re"].shape[0]
    nh = w["b_f"].shape[0]
    nhp = nh // 2
    assert d == nh * HEAD_DIM
    tq = _tile(n_rows, attn_tile)
    in_slots = w["w_in"].shape[0]
    glu_slots = w["w_glu"].shape[0]
    g = {}
    saved = [dict() for _ in range(n_layers)]

    h = x
    nxt = _rowwise(lambda a, gg: _rms(a, gg), [x], [_row2(w["g_mix"][0])], [(d, F32)], name="rms_first")[0]
    kvb = fl = cum = cq3 = ck3 = hnkv = None
    bf_pad = jnp.zeros((1, LANES), F32).at[0, :nh].set(w["b_f"])
    for l in range(n_layers):
        sv = saved[l]
        sv["h"] = h
        g_ffn = _row2(w["g_ffn"][l])
        if l < n_s5:
            tabs, (_, p, _) = _s5_tables(w, l)
            prep = _s5_prep(*tabs, p, name=f"s5_prep{l}")
            dskip = w["ssm_d"][l].reshape(1, d)
            y, st_re, st_im, sb_re, sb_im = _s5_fwd(nxt, prep, dskip, name=f"s5_fwd{l}")
            z = _rowwise(_gelu, [y], [], [(d, BF16)], name=f"gelu{l}")[0]
            zz = _mm_cols(z, w["w_glu"], l, wc=0, name=f"glu_mm{l}")
            h1, hn2 = _rowwise(lambda hh, zq, gg: ((lambda t: (t, _rms(t, gg)))(hh + _glu(zq))),
                               [h, zz], [g_ffn], [(d, F32), (d, BF16)], name=f"mix_out{l}")
            sv.update(u=nxt, prep=prep, tabs=tabs, p=p, dskip=dskip, st_re=st_re, st_im=st_im, sb_re=sb_re,
                      sb_im=sb_im, y=y, z=z, zz=zz)
        else:
            j = l - n_s5
            qs = _mm_cols(nxt, w["w_q"], j, wc=0, out_dtype=BF16, scale=HEAD_DIM ** -0.5, name=f"q_mm{j}")
            o, lse = _flash_fwd(qs, kvb, cq3, ck3, tq=tq, name=f"flash_fwd{j}")
            a = _mm_cols(o, w["w_o"], j, wc=0, name=f"o_mm{j}")
            h1, hn2 = _rowwise(lambda hh, aa, gg: ((lambda t: (t, _rms(t, gg)))(hh + aa)),
                               [h, a], [g_ffn], [(d, F32), (d, BF16)], name=f"mix_out{l}")
            sv.update(hn=nxt, qs=qs, o=o, lse=lse)
        uu = _mm_cols(hn2, w["w_in"], l, wc=0, name=f"ffn_in{l}")
        cw, cb = w["conv_w"][l], _row2(w["conv_b"][l])
        act = _conv_fwd(uu, cw, cb, name=f"conv_fwd{l}")
        f = _mm_cols(act, w["w_out"], l, wc=0, name=f"ffn_out{l}")
        sv.update(h1=h1, hn2=hn2, uu=uu, act=act, cw=cw, cb=cb)
        if l == n_layers - 1:
            def loss_fn(hh, ff, tt, gg):
                yv, vjp = jax.vjp(_rms, hh + ff, gg)
                err = yv - tt
                part = 0.5 * jnp.sum(jnp.mean(err * err, axis=-1, keepdims=True), axis=0, keepdims=True)
                dh, dg = vjp(err * (1.0 / d))
                return dh, dh, jnp.broadcast_to(part, (1, LANES)), dg
            dcur, dcur16, loss_row, dgf = _rowwise(loss_fn, [h1, f, tgt], [_row2(w["g_final"])],
                                                   [(d, F32), (d, BF16)], [(1, LANES), (1, d)], name="loss")
            loss = loss_row[0, 0]
            g["g_final"] = dgf[0]
        elif l + 1 < n_s5:
            h, nxt = _rowwise(lambda hh, ff, gg: ((lambda t: (t, _rms(t, gg)))(hh + ff)), [h1, f],
                              [_row2(w["g_mix"][l + 1])], [(d, F32), (d, F32)], name=f"ffn_res{l}")
        elif l + 1 == n_s5:
            h, nxt, hnkv = _rowwise(
                lambda hh, ff, g1, g2: ((lambda t: (t, _rms(t, g1), _rms(t, g2)))(hh + ff)), [h1, f],
                [_row2(w["g_mix"][l + 1]), _row2(w["g_kv"])], [(d, F32), (d, BF16), (d, BF16)], name=f"ffn_res{l}")
            kvb = _mm_cols(hnkv, w["w_kv"], 0, wc=0, out_dtype=BF16, name="kv_mm")
            fl = _mm_cols(hnkv, w["w_f"], 0, wc=0, name="f_mm")
            cum = _cum_fwd(fl, bf_pad, name="cum_fwd")
            cq3 = cum[:, :nh].reshape(n_rows, nhp, 2).transpose(1, 0, 2)
            ck3 = cum[:, :nh].T.reshape(nhp, 2, n_rows)
        else:
            h, nxt = _rowwise(lambda hh, ff, gg: ((lambda t: (t, _rms(t, gg)))(hh + ff)), [h1, f],
                              [_row2(w["g_mix"][l + 1])], [(d, F32), (d, BF16)], name=f"ffn_res{l}")

    per_layer = {k: [None] * n_layers for k in ("g_mix", "g_ffn", "w_in", "w_out", "conv_w", "conv_b")}
    per_s5 = {k: [None] * n_s5 for k in ("lam_re", "lam_im", "log_dt", "ssm_b_re", "ssm_b_im", "ssm_c_re",
                                         "ssm_c_im", "ssm_d", "w_glu")}
    per_fox = {k: [None] * (n_layers - n_s5) for k in ("w_q", "w_o")}
    dk_parts, dv_parts, dck_parts = [], [], []
    for l in reversed(range(n_layers)):
        sv = saved[l]
        dact = _mm_cols(dcur16, w["w_out"], l, wc=1, name=f"ffn_out_dx{l}")
        per_layer["w_out"][l] = _mm_tn(sv["act"], dcur16, 1, name=f"ffn_out_dw{l}")[0]
        duu, dcw, dcb = _conv_bwd(sv["uu"], dact, sv["cw"], sv["cb"], name=f"conv_bwd{l}")
        per_layer["conv_w"][l] = jnp.concatenate([dcw[0], dcw[1]], axis=-1)
        per_layer["conv_b"][l] = jnp.concatenate([dcb[0, 0], dcb[1, 0]])
        dhn2 = _mm_acc(duu, w["w_in"], l, wc=1, name=f"ffn_in_dx{l}")
        per_layer["w_in"][l] = _mm_tn(sv["hn2"], duu, in_slots, name=f"ffn_in_dw{l}")
        d1, d1_16, (dg,) = _node_bwd(dcur, sv["h1"], [(w["g_ffn"][l], [dhn2])], name=f"ffn_norm_bwd{l}")
        per_layer["g_ffn"][l] = dg
        if l < n_s5:
            def glu_bwd(zq, dd):
                _, vjp = jax.vjp(_glu, zq)
                return vjp(dd)[0]
            dzz = _rowwise(glu_bwd, [sv["zz"], d1], [], [(2 * d, BF16)], name=f"glu_bwd{l}")[0]
            dz = _mm_acc(dzz, w["w_glu"], l, wc=1, name=f"glu_dx{l}")
            per_s5["w_glu"][l] = _mm_tn(sv["z"], dzz, glu_slots, name=f"glu_dw{l}")

            def gelu_bwd(yy, dd):
                _, vjp = jax.vjp(_gelu, yy)
                return vjp(dd)[0]
            dy = _rowwise(gelu_bwd, [sv["y"], dz], [], [(d, F32)], name=f"gelu_bwd{l}")[0]
            du, dwbr, dwbi, dwcr, dwci, dlbr, dlbi, dd = _s5_bwd(sv["u"], dy, sv["st_re"], sv["st_im"], sv["sb_re"],
                                                                 sv["sb_im"], sv["prep"], sv["dskip"],
                                                                 name=f"s5_bwd{l}")
            dlr, dli, dldt, dbr, dbi, dcr, dci = _s5_prep_bwd(*sv["tabs"], sv["p"], dlbr, dlbi, dwbr, dwbi, dwcr,
                                                              dwci, name=f"s5_prep_bwd{l}")
            gg, p = w["lam_re"].shape[1:]
            hh = w["ssm_b_re"].shape[3]
            per_s5["lam_re"][l] = dlr.reshape(gg, p)
            per_s5["lam_im"][l] = dli.reshape(gg, p)
            per_s5["log_dt"][l] = dldt.reshape(gg, p).sum(axis=1)
            per_s5["ssm_b_re"][l] = dbr.reshape(hh, gg, p).transpose(1, 2, 0)
            per_s5["ssm_b_im"][l] = dbi.reshape(hh, gg, p).transpose(1, 2, 0)
            per_s5["ssm_c_re"][l] = dcr.reshape(hh, gg, p).transpose(1, 0, 2)
            per_s5["ssm_c_im"][l] = dci.reshape(hh, gg, p).transpose(1, 0, 2)
            per_s5["ssm_d"][l] = dd.reshape(d)
            branches = [(w["g_mix"][l], [du])]
        else:
            j = l - n_s5
            do = _mm_cols(d1_16, w["w_o"], j, wc=1, name=f"o_dx{j}")
            per_fox["w_o"][j] = _mm_tn(sv["o"], d1_16, 1, name=f"o_dw{j}")[0]
            dq, dk, dv, dck, dcq = _flash_bwd(sv["qs"], kvb, sv["o"], sv["lse"], do, cq3, ck3, tq=tq,
                                              name=f"flash_bwd{j}")
            dk_parts.append(dk)
            dv_parts.append(dv)
            dck_parts.append(dck.reshape(nh, n_rows).T + dcq.transpose(1, 0, 2).reshape(n_rows, nh))
            scale = HEAD_DIM ** -0.5
            dhn = _mm_cols(dq, w["w_q"], j, wc=1, scale=scale, name=f"q_dx{j}")
            per_fox["w_q"][j] = _mm_tn(sv["hn"], dq, 1, scale=scale, name=f"q_dw{j}")[0]
            branches = [(w["g_mix"][l], [dhn])]
            if j == 0:
                def kv_sum(*parts):
                    half = len(parts) // 2
                    return jnp.concatenate([sum(parts[:half][1:], parts[0]),
                                            sum(parts[half:][1:], parts[half])], axis=1)
                dkv = _rowwise(kv_sum, dk_parts + dv_parts, [], [(2 * d, BF16)], name="dkv_sum")[0]
                dck_tot = dck_parts[0]
                for extra in dck_parts[1:]:
                    dck_tot = dck_tot + extra
                dcum = jnp.zeros((n_rows, LANES), F32).at[:, :nh].set(dck_tot)
                dfl, dbf = _cum_bwd(dcum, fl, bf_pad, name="cum_bwd")
                g["b_f"] = dbf[0, :nh]
                dhkv_a = _mm_cols(dkv, w["w_kv"], 0, wc=1, name="kv_dx")
                dhkv_b = _mm_cols(dfl, w["w_f"], 0, wc=1, name="f_dx")
                g["w_kv"] = _mm_tn(hnkv, dkv, 1, name="kv_dw")[0]
                g["w_f"] = _mm_tn(hnkv, dfl, 1, name="f_dw")[0]
                branches.append((w["g_kv"], [dhkv_a, dhkv_b]))
        dcur, dcur16, dgs = _node_bwd(d1, sv["h"], branches, name=f"mix_norm_bwd{l}")
        per_layer["g_mix"][l] = dgs[0]
        if len(dgs) > 1:
            g["g_kv"] = dgs[1]

    def by_row_shard(m):
        return m.reshape(N_CHIPS, m.shape[0] // N_CHIPS, m.shape[1])

    for k, v in (*per_layer.items(), *per_s5.items(), *per_fox.items()):
        if k in ("w_in", "w_glu"):
            g[k] = v
        elif k in ("w_out", "w_q", "w_o"):
            g[k] = [by_row_shard(m) for m in v]
        else:
            g[k] = jnp.stack(v)
    return loss, dcur, g


def _position():
    x, y, c = lax.axis_index("x"), lax.axis_index("y"), lax.axis_index("c")
    chips = [(1 - x, y), (x, 1 - y), (1 - x, 1 - y)]
    return x, y, c, chips


def _any_specs(n):
    return [pl.BlockSpec(memory_space=pl.ANY)] * n


def _all_gather(shards, kinds, *, name):
    n = len(shards)

    def out_shape(a, kind):
        shape = (N_CHIPS,) + a.shape if kind == "C" else (a.shape[0], N_CHIPS) + a.shape[1:]
        return jax.ShapeDtypeStruct(shape, a.dtype)

    def body(*refs):
        ins, outs = refs[:n], refs[n:2 * n]
        send_sems, recv_sems = refs[2 * n:]
        x, y, c, chips = _position()
        my_slot = 2 * x + y
        sibling = (x, y, 1 - c)

        def rows(t, half):
            hr = ins[t].shape[0] // 2
            return pl.ds(half * hr, hr)

        def piece(t, slot, half):
            return outs[t].at[slot, rows(t, half)] if kinds[t] == "C" else outs[t].at[rows(t, half), slot]

        def whole(t, slot):
            full = pl.ds(0, ins[t].shape[0])
            return outs[t].at[slot, full] if kinds[t] == "C" else outs[t].at[full, slot]

        def remote(k, t, src, dst, to):
            return pltpu.make_async_remote_copy(src_ref=src, dst_ref=dst, send_sem=send_sems.at[k, t],
                                                recv_sem=recv_sems.at[k, t], device_id=to, device_id_type=MESH)

        started = []
        for t in range(n):
            cp = remote(6, t, ins[t], whole(t, my_slot), sibling)
            cp.start()
            started.append(cp)
        for j, chip in enumerate(chips):
            for t in range(n):
                cp = remote(j, t, ins[t].at[rows(t, c)], piece(t, my_slot, c), (*chip, c))
                cp.start()
                started.append(cp)
        for j, chip in enumerate(chips):
            slot = 2 * chip[0] + chip[1]
            for t in range(n):
                remote(j, t, piece(t, slot, c), piece(t, slot, c), (*chip, c)).wait_recv()
            for t in range(n):
                cp = remote(3 + j, t, piece(t, slot, c), piece(t, slot, c), sibling)
                cp.start()
                started.append(cp)
        for j, chip in enumerate(chips):
            slot = 2 * chip[0] + chip[1]
            for t in range(n):
                remote(3 + j, t, piece(t, slot, 1 - c), piece(t, slot, 1 - c), sibling).wait_recv()
        for t in range(n):
            remote(6, t, ins[t], whole(t, my_slot), sibling).wait_recv()
        for cp in started:
            cp.wait_send()

    return pl.pallas_call(
        body, in_specs=_any_specs(n), out_specs=_any_specs(n),
        out_shape=[out_shape(a, k) for a, k in zip(shards, kinds)],
        scratch_shapes=[pltpu.SemaphoreType.DMA((7, n)), pltpu.SemaphoreType.DMA((7, n))],
        name=name)(*shards)


def _pair_exchange(grads, *, name):
    n = len(grads)

    def body(*refs):
        ins, outs = refs[:n], refs[n:2 * n]
        send_sems, recv_sems = refs[2 * n:]
        x, y, c, _ = _position()
        copies = [pltpu.make_async_remote_copy(src_ref=ins[t].at[:, 1 - c], dst_ref=outs[t],
                                               send_sem=send_sems.at[t], recv_sem=recv_sems.at[t],
                                               device_id=(x, y, 1 - c), device_id_type=MESH) for t in range(n)]
        for cp in copies:
            cp.start()
        for cp in copies:
            cp.wait()

    return pl.pallas_call(
        body, in_specs=_any_specs(n), out_specs=_any_specs(n),
        out_shape=[jax.ShapeDtypeStruct((a.shape[0],) + a.shape[2:], a.dtype) for a in grads],
        scratch_shapes=[pltpu.SemaphoreType.DMA((n,)), pltpu.SemaphoreType.DMA((n,))], name=name)(*grads)


def _chip_exchange(parts, *, name):
    n = len(parts)

    def body(*refs):
        ins, outs = refs[:n], refs[n:2 * n]
        send_sems, recv_sems = refs[2 * n:]
        _, _, c, chips = _position()
        copies = []
        for j, chip in enumerate(chips):
            slot = 2 * chip[0] + chip[1]
            for t in range(n):
                cp = pltpu.make_async_remote_copy(src_ref=ins[t].at[slot], dst_ref=outs[t].at[j],
                                                  send_sem=send_sems.at[j, t], recv_sem=recv_sems.at[j, t],
                                                  device_id=(*chip, c), device_id_type=MESH)
                cp.start()
                copies.append(cp)
        for cp in copies:
            cp.wait()

    return pl.pallas_call(
        body, in_specs=_any_specs(n), out_specs=_any_specs(n),
        out_shape=[jax.ShapeDtypeStruct((N_CHIPS - 1,) + a.shape[1:], a.dtype) for a in parts],
        scratch_shapes=[pltpu.SemaphoreType.DMA((N_CHIPS - 1, n)), pltpu.SemaphoreType.DMA((N_CHIPS - 1, n))],
        name=name)(*parts)


def _pair_share(both, *, name):
    n = len(both)

    def body(*refs):
        ins, outs = refs[:n], refs[n:2 * n]
        send_sems, recv_sems = refs[2 * n:]
        x, y, c, _ = _position()
        for t in range(n):
            pltpu.make_async_remote_copy(src_ref=ins[t].at[c], dst_ref=outs[t].at[c], send_sem=send_sems.at[t],
                                         recv_sem=recv_sems.at[t], device_id=(x, y, 1 - c),
                                         device_id_type=MESH).start()
        for t in range(n):
            pltpu.make_async_remote_copy(src_ref=ins[t].at[c], dst_ref=outs[t].at[1 - c], send_sem=send_sems.at[t],
                                         recv_sem=recv_sems.at[t], device_id=(x, y, 1 - c),
                                         device_id_type=MESH).wait()

    return pl.pallas_call(
        body, in_specs=_any_specs(n), out_specs=_any_specs(n),
        out_shape=[jax.ShapeDtypeStruct(a.shape, a.dtype) for a in both],
        input_output_aliases={t: t for t in range(n)},
        scratch_shapes=[pltpu.SemaphoreType.DMA((n,)), pltpu.SemaphoreType.DMA((n,))], name=name)(*both)


def _sum_pair(grad, landed, c, wire_dtype, *, name):
    slots, _, m, n = grad.shape
    tm = _tile(m, 256, 2 * SUBLANES)

    def body(c_ref, g_ref, l_ref, o_ref):
        o_ref[...] = (g_ref[0] + l_ref[...]).astype(wire_dtype)

    return pl.pallas_call(
        body,
        grid_spec=pltpu.PrefetchScalarGridSpec(
            num_scalar_prefetch=1, grid=(slots, m // tm),
            in_specs=[pl.BlockSpec((1, 1, tm, n), lambda s, i, c_ref: (s, c_ref[0], i, 0)),
                      pl.BlockSpec((1, tm, n), lambda s, i, c_ref: (s, i, 0))],
            out_specs=pl.BlockSpec((1, tm, n), lambda s, i, c_ref: (s, i, 0))),
        out_shape=jax.ShapeDtypeStruct((slots, m, n), wire_dtype), compiler_params=_cp(2), name=name)(
            c, grad, landed)


def _sum_chips(part, landed, slot_c, *, name):
    _, m, n = part.shape
    tm = _tile(m, 256, 2 * SUBLANES)

    def body(s_ref, p_ref, l_ref, o_ref):
        acc = p_ref[0].astype(F32)
        for j in range(N_CHIPS - 1):
            acc = acc + l_ref[j].astype(F32)
        o_ref[0] = acc

    return pl.pallas_call(
        body,
        grid_spec=pltpu.PrefetchScalarGridSpec(
            num_scalar_prefetch=1, grid=(m // tm,),
            in_specs=[pl.BlockSpec((1, tm, n), lambda i, s_ref: (s_ref[0], i, 0)),
                      pl.BlockSpec((N_CHIPS - 1, tm, n), lambda i, s_ref: (0, i, 0))],
            out_specs=pl.BlockSpec((1, tm, n), lambda i, s_ref: (s_ref[1], i, 0))),
        out_shape=jax.ShapeDtypeStruct((N_CORES, m, n), F32), compiler_params=_cp(1), name=name)(
            slot_c, part, landed)


def _reduce_scatter(grads, wire_dtypes):
    c = lax.axis_index("c").reshape(1).astype(jnp.int32)
    slot_c = jnp.stack([2 * lax.axis_index("x") + lax.axis_index("y"), lax.axis_index("c")]).astype(jnp.int32)
    views = []
    for a in grads:
        lead, last = a.shape[1], a.shape[-1]
        mid = 1
        for s in a.shape[2:-1]:
            mid *= s
        views.append(a.reshape(N_CHIPS, N_CORES, (lead // N_CORES) * mid, last))
    landed = _pair_exchange(views, name="rs_pair_exchange")
    parts = [_sum_pair(v, l, c, wire_dtypes[t], name=f"rs_pair_sum{t}")
             for t, (v, l) in enumerate(zip(views, landed))]
    landed = _chip_exchange(parts, name="rs_chip_exchange")
    both = [_sum_chips(p, l, slot_c, name=f"rs_chip_sum{t}") for t, (p, l) in enumerate(zip(parts, landed))]
    full = _pair_share(both, name="rs_pair_share")
    return [f.reshape(a.shape[1:]) for f, a in zip(full, grads)]


def _adamw(w, g, m, v, *, name):
    def fn(ww, gg, mm, vv):
        mm = ADAM_B1 * mm + (1.0 - ADAM_B1) * gg
        vv = ADAM_B2 * vv + (1.0 - ADAM_B2) * (gg * gg)
        m_hat = mm / (1.0 - ADAM_B1 ** ADAM_STEP)
        v_hat = vv / (1.0 - ADAM_B2 ** ADAM_STEP)
        delta = -ADAM_LR * (m_hat / (jnp.sqrt(v_hat) + ADAM_EPS) + ADAM_WD * ww)
        return delta, mm, vv

    shape = w.shape
    two_d = [a.reshape(-1, shape[-1]) for a in (w, g, m, v)]
    outs = _rowwise(fn, two_d, [], [(shape[-1], F32)] * 3, name=name)
    return [o.reshape(shape) for o in outs]


def _to_bf16(a, *, name):
    two_d = a.reshape(-1, a.shape[-1])
    return _rowwise(lambda t: t, [two_d], [], [(a.shape[-1], BF16)], name=name)[0].reshape(a.shape)


def _pack(arrays, rows_multiple):
    flat = jnp.concatenate([a.reshape(-1) for a in arrays])
    rows = -(-flat.shape[0] // LANES)
    rows = -(-rows // rows_multiple) * rows_multiple
    return jnp.pad(flat, (0, rows * LANES - flat.shape[0])).reshape(rows, LANES)


def _unpack(packed, like):
    flat = packed.reshape(-1)
    out, pos = [], 0
    for a in like:
        out.append(flat[pos:pos + a.size].reshape(a.shape))
        pos += a.size
    return out


_PARAMS = ("g_mix", "g_ffn", "lam_re", "lam_im", "log_dt", "ssm_b_re", "ssm_b_im", "ssm_c_re", "ssm_c_im", "ssm_d",
           "w_glu", "g_kv", "w_kvf", "b_f", "w_q", "w_o", "w_ffn_in", "ffn_conv_w", "ffn_conv_b", "w_ffn_out",
           "g_final")
_BIG = ("w_glu", "w_kvf", "w_q", "w_o", "w_ffn_in", "w_ffn_out")
_SMALL_SHARDED = ("ssm_d", "ffn_conv_w")


def kernel(x, g_mix, g_ffn, lam_re, lam_im, log_dt, ssm_b_re, ssm_b_im, ssm_c_re, ssm_c_im, ssm_d, w_glu, g_kv, w_kvf, b_f, w_q, w_o, w_ffn_in, ffn_conv_w, ffn_conv_b, w_ffn_out, g_final, loss_target, m_g_mix, m_g_ffn, m_lam_re, m_lam_im, m_log_dt, m_ssm_b_re, m_ssm_b_im, m_ssm_c_re, m_ssm_c_im, m_ssm_d, m_w_glu, m_g_kv, m_w_kvf, m_b_f, m_w_q, m_w_o, m_w_ffn_in, m_ffn_conv_w, m_ffn_conv_b, m_w_ffn_out, m_g_final, v_g_mix, v_g_ffn, v_lam_re, v_lam_im, v_log_dt, v_ssm_b_re, v_ssm_b_im, v_ssm_c_re, v_ssm_c_im, v_ssm_d, v_w_glu, v_g_kv, v_w_kvf, v_b_f, v_w_q, v_w_o, v_w_ffn_in, v_ffn_conv_w, v_ffn_conv_b, v_w_ffn_out, v_g_final):
    p = dict(g_mix=g_mix, g_ffn=g_ffn, lam_re=lam_re, lam_im=lam_im, log_dt=log_dt, ssm_b_re=ssm_b_re,
             ssm_b_im=ssm_b_im, ssm_c_re=ssm_c_re, ssm_c_im=ssm_c_im, ssm_d=ssm_d, w_glu=w_glu, g_kv=g_kv,
             w_kvf=w_kvf, b_f=b_f, w_q=w_q, w_o=w_o, w_ffn_in=w_ffn_in, ffn_conv_w=ffn_conv_w,
             ffn_conv_b=ffn_conv_b, w_ffn_out=w_ffn_out, g_final=g_final)
    mom1 = dict(zip(_PARAMS, (m_g_mix, m_g_ffn, m_lam_re, m_lam_im, m_log_dt, m_ssm_b_re, m_ssm_b_im, m_ssm_c_re,
                              m_ssm_c_im, m_ssm_d, m_w_glu, m_g_kv, m_w_kvf, m_b_f, m_w_q, m_w_o, m_w_ffn_in,
                              m_ffn_conv_w, m_ffn_conv_b, m_w_ffn_out, m_g_final)))
    mom2 = dict(zip(_PARAMS, (v_g_mix, v_g_ffn, v_lam_re, v_lam_im, v_log_dt, v_ssm_b_re, v_ssm_b_im, v_ssm_c_re,
                              v_ssm_c_im, v_ssm_d, v_w_glu, v_g_kv, v_w_kvf, v_b_f, v_w_q, v_w_o, v_w_ffn_in,
                              v_ffn_conv_w, v_ffn_conv_b, v_w_ffn_out, v_g_final)))
    d = x.shape[-1]
    nh = b_f.shape[0]
    slot = 2 * lax.axis_index("x") + lax.axis_index("y")

    shards = [_to_bf16(p[k], name=f"to_bf16_{k}") for k in _BIG] + [p[k] for k in _SMALL_SHARDED]
    kinds = ["C", "C", "R", "R", "C", "R", "C", "C"]
    gl, gkvf, gq, go, gin, gout, gd, gcw = _all_gather(shards, kinds, name="weights_all_gather")
    n_lay = gin.shape[1]
    kvf_full = gkvf.transpose(1, 0, 2).reshape(d, -1)
    w_f = jnp.zeros((d, LANES), BF16).at[:, :nh].set(kvf_full[:, 2 * d:])
    w = dict(p)
    w.update(
        w_glu=gl, w_kv=kvf_full[:, :2 * d][None, None], w_f=w_f[None, None],
        w_q=gq.reshape(1, gq.shape[0], d, d), w_o=go.reshape(1, go.shape[0], d, d), w_in=gin,
        w_out=gout.reshape(1, n_lay, -1, d),
        conv_w=gcw.transpose(1, 2, 0, 3).reshape(n_lay, DT_CONV_TAPS, -1), conv_b=ffn_conv_b,
        ssm_d=gd.transpose(1, 0, 2).reshape(gd.shape[1], d))

    loss_part, grad_x, g = _local_step(x[0], loss_target[0], w)
    loss = lax.psum(loss_part, ("x", "y", "c"))

    g_kvf = jnp.concatenate([g["w_kv"], g["w_f"][:, :nh]], axis=1)
    big = dict(w_glu=g["w_glu"], w_kvf=[g_kvf.reshape(d, N_CHIPS, -1).transpose(1, 0, 2)], w_q=g["w_q"],
               w_o=g["w_o"], w_ffn_in=g["w_in"], w_ffn_out=g["w_out"])
    big_list = [a for k in _BIG for a in big[k]]
    small_names = [k for k in _PARAMS if k not in _BIG]
    small_full = dict(g_mix=g["g_mix"], g_ffn=g["g_ffn"], lam_re=g["lam_re"], lam_im=g["lam_im"], log_dt=g["log_dt"],
                      ssm_b_re=g["ssm_b_re"], ssm_b_im=g["ssm_b_im"], ssm_c_re=g["ssm_c_re"], ssm_c_im=g["ssm_c_im"],
                      ssm_d=g["ssm_d"], g_kv=g["g_kv"], b_f=g["b_f"], ffn_conv_w=g["conv_w"],
                      ffn_conv_b=g["conv_b"], g_final=g["g_final"])
    small_list = [small_full[k] for k in small_names]
    pack = _pack(small_list, N_CHIPS * N_CORES * 2 * SUBLANES)
    pack4 = pack.reshape(N_CHIPS, pack.shape[0] // N_CHIPS, LANES)
    reduced = _reduce_scatter(big_list + [pack4], [BF16] * len(big_list) + [F32])
    red_big, pos = {}, 0
    for k in _BIG:
        layers = reduced[pos:pos + len(big[k])]
        pos += len(big[k])
        red_big[k] = layers[0] if p[k].ndim == 2 else jnp.stack(layers)
    pack_all = _all_gather([reduced[-1]], ["C"], name="small_grads_all_gather")[0]
    red_small = dict(zip(small_names, _unpack(pack_all, small_list)))
    for k in _SMALL_SHARDED:
        width = p[k].shape[-1]
        red_small[k] = lax.dynamic_slice_in_dim(red_small[k], slot * width, width, axis=red_small[k].ndim - 1)

    grads, deltas, new_m, new_v = {}, {}, {}, {}
    for k in _BIG:
        grads[k] = red_big[k]
        deltas[k], new_m[k], new_v[k] = _adamw(p[k], grads[k], mom1[k], mom2[k], name=f"adamw_{k}")
    packs = [_pack([src[k] for k in small_names], SUBLANES) for src in (p, red_small, mom1, mom2)]
    like = [p[k] for k in small_names]
    outs = [_unpack(o, like) for o in _adamw(*packs, name="adamw_small")]
    for i, k in enumerate(small_names):
        grads[k] = red_small[k]
        deltas[k], new_m[k], new_v[k] = outs[0][i], outs[1][i], outs[2][i]
    return (loss, grad_x[None], *[grads[k] for k in _PARAMS], *[deltas[k] for k in _PARAMS],
            *[new_m[k] for k in _PARAMS], *[new_v[k] for k in _PARAMS])
```

```python
import functools

import jax
import jax.numpy as jnp
from jax import lax
from jax.experimental import pallas as pl
from jax.experimental.pallas import tpu as pltpu

F32 = jnp.float32
BF16 = jnp.bfloat16

RMS_EPS = 1e-6
ADAM_LR = 0.001
ADAM_B1 = 0.9
ADAM_B2 = 0.999
ADAM_EPS = 1e-08
ADAM_WD = 0.01
ADAM_STEP = 10
DT_CONV_TAPS = 3

LANES = 128
SUBLANES = 8
HEAD_DIM = 64
FLASH_ROW_TILE = 32
S5_BLOCK_GROUPS = 16
VMEM_LIMIT_BYTES = 48 << 20
MM_BLOCK_BUDGET_BYTES = 30 << 20
N_CHIPS = 4
N_CORES = 2
MESH = pl.DeviceIdType.MESH


def _cp(n_grid):
    return pltpu.CompilerParams(dimension_semantics=("arbitrary",) * n_grid, vmem_limit_bytes=VMEM_LIMIT_BYTES)


def _tile(n, pref, mult=SUBLANES):
    if n <= pref:
        return n
    t = (pref // mult) * mult
    while t >= mult:
        if n % t == 0:
            return t
        t -= mult
    return n


class _Comm:
    def __init__(self, ins, out_shapes, sems, start, finish):
        self.ins, self.out_shapes, self.sems, self.start, self.finish = ins, out_shapes, sems, start, finish


def _any_specs(n):
    return [pl.BlockSpec(memory_space=pl.ANY)] * n


def _run_comm(comm, *, name):
    n_in, n_out = len(comm.ins), len(comm.out_shapes)

    def body(*refs):
        ins, outs, sems = refs[:n_in], refs[n_in:n_in + n_out], refs[n_in + n_out:]
        comm.start(ins, outs, sems)
        comm.finish(ins, outs, sems)

    return pl.pallas_call(body, in_specs=_any_specs(n_in), out_specs=_any_specs(n_out),
                          out_shape=list(comm.out_shapes), scratch_shapes=list(comm.sems), name=name)(*comm.ins)


def _call(body, *, grid, in_specs, out_specs, out_shape, args, name, scratch_shapes=(), prefetch=(), comm=None):
    n_pre, n_in, n_out, n_scr = len(prefetch), len(in_specs), len(out_specs), len(scratch_shapes)
    in_specs, out_specs, out_shape = list(in_specs), list(out_specs), list(out_shape)
    scratch_shapes, args = list(scratch_shapes), list(args)
    kernel_body = body
    if comm is not None:
        n_cin, n_cout = len(comm.ins), len(comm.out_shapes)

        def kernel_body(*refs):
            pos = n_pre + n_in
            c_in = refs[pos:pos + n_cin]
            main_out = refs[pos + n_cin:pos + n_cin + n_out]
            pos += n_cin + n_out
            c_out = refs[pos:pos + n_cout]
            main_scr = refs[pos + n_cout:pos + n_cout + n_scr]
            sems = refs[pos + n_cout + n_scr:]
            ids = [pl.program_id(a) for a in range(len(grid))]
            first = functools.reduce(jnp.logical_and, [i == 0 for i in ids])
            last = functools.reduce(jnp.logical_and, [i == g - 1 for i, g in zip(ids, grid)])
            pl.when(first)(lambda: comm.start(c_in, c_out, sems))
            body(*refs[:n_pre + n_in], *main_out, *main_scr)
            pl.when(last)(lambda: comm.finish(c_in, c_out, sems))

        in_specs += _any_specs(n_cin)
        out_specs += _any_specs(n_cout)
        out_shape += list(comm.out_shapes)
        scratch_shapes += list(comm.sems)
        args += list(comm.ins)
    if prefetch:
        spec = pltpu.PrefetchScalarGridSpec(num_scalar_prefetch=n_pre, grid=grid, in_specs=in_specs,
                                            out_specs=out_specs, scratch_shapes=scratch_shapes)
        res = pl.pallas_call(kernel_body, grid_spec=spec, out_shape=out_shape, compiler_params=_cp(len(grid)),
                             name=name)(*prefetch, *args)
    else:
        res = pl.pallas_call(kernel_body, grid=grid, in_specs=in_specs, out_specs=out_specs, out_shape=out_shape,
                             scratch_shapes=scratch_shapes, compiler_params=_cp(len(grid)), name=name)(*args)
    return (res[:n_out], res[n_out:]) if comm is not None else res


def _row_tile(m, bytes_per_row, fixed_bytes):
    for tm in (1024, 512):
        if m % tm == 0 and 2 * (tm * bytes_per_row + fixed_bytes) <= MM_BLOCK_BUDGET_BYTES:
            return tm
    return _tile(m, 512)


def _dot(a, b, ca, cb):
    return lax.dot_general(a, b, (((ca,), (cb,)), ((), ())), preferred_element_type=F32)


def _mm_cols(x, w4, layer, *, wc, out_dtype=F32, scale=None, name):
    m, k = x.shape
    slots, _, k0, k1 = w4.shape
    nb = k1 if wc == 0 else k0
    assert (k0 if wc == 0 else k1) == k
    tm = _row_tile(m, k * x.dtype.itemsize + nb * jnp.dtype(out_dtype).itemsize, k0 * k1 * w4.dtype.itemsize)

    def body(x_ref, w_ref, o_ref):
        acc = _dot(x_ref[...].astype(BF16), w_ref[0, 0], 1, wc)
        if scale is not None:
            acc = acc * scale
        o_ref[...] = acc.astype(out_dtype)

    return pl.pallas_call(
        body, grid=(slots, m // tm),
        in_specs=[pl.BlockSpec((tm, k), lambda s, i: (i, 0)),
                  pl.BlockSpec((1, 1, k0, k1), lambda s, i: (s, layer, 0, 0))],
        out_specs=pl.BlockSpec((tm, nb), lambda s, i: (i, s)),
        out_shape=jax.ShapeDtypeStruct((m, slots * nb), out_dtype),
        compiler_params=_cp(2), name=name)(x, w4)


def _planes(a):
    return a if a.ndim == 3 else a[None]


def _mm_acc(x, w4, layer, *, wc, name):
    x = _planes(x)
    n_planes, m, width = x.shape
    slots, _, k0, k1 = w4.shape
    kb = k0 if wc == 0 else k1
    nout = k1 if wc == 0 else k0
    assert n_planes * width == slots * kb
    spp = slots // n_planes
    tm = _row_tile(m, kb * x.dtype.itemsize + nout * 4, k0 * k1 * w4.dtype.itemsize)

    def body(x_ref, w_ref, o_ref):
        @pl.when(pl.program_id(1) == 0)
        def _():
            o_ref[...] = jnp.zeros_like(o_ref)
        o_ref[...] += _dot(x_ref[0].astype(BF16), w_ref[0, 0], 1, wc)

    return pl.pallas_call(
        body, grid=(m // tm, slots),
        in_specs=[pl.BlockSpec((1, tm, kb), lambda i, s: (s // spp, i, s % spp)),
                  pl.BlockSpec((1, 1, k0, k1), lambda i, s: (s, layer, 0, 0))],
        out_specs=pl.BlockSpec((tm, nout), lambda i, s: (i, 0)),
        out_shape=jax.ShapeDtypeStruct((m, nout), F32),
        compiler_params=_cp(2), name=name)(x, w4)


def _mm_tn(x, dy, slots, *, scale=None, name):
    m, k = x.shape
    dy = _planes(dy)
    n_planes, _, width = dy.shape
    n = n_planes * width // slots
    spp = slots // n_planes
    ta = _tile(k, 512, LANES)
    tm = m
    while tm > 512 and tm % 2 == 0 and (2 * tm * (ta * x.dtype.itemsize + n * dy.dtype.itemsize)
                                         + 2 * ta * n * 4) > MM_BLOCK_BUDGET_BYTES:
        tm //= 2
    n_m = m // tm

    def body(x_ref, dy_ref, o_ref):
        @pl.when(pl.program_id(2) == 0)
        def _():
            o_ref[...] = jnp.zeros_like(o_ref)
        o_ref[0] += _dot(x_ref[...].astype(BF16), dy_ref[0].astype(BF16), 0, 0)
        if scale is not None:
            @pl.when(pl.program_id(2) == n_m - 1)
            def _():
                o_ref[...] = o_ref[...] * scale

    return pl.pallas_call(
        body, grid=(slots, k // ta, n_m),
        in_specs=[pl.BlockSpec((tm, ta), lambda s, a, i: (i, a)),
                  pl.BlockSpec((1, tm, n), lambda s, a, i: (s // spp, i, s % spp))],
        out_specs=pl.BlockSpec((1, ta, n), lambda s, a, i: (s, a, 0)),
        out_shape=jax.ShapeDtypeStruct((slots, k, n), F32),
        compiler_params=_cp(3), name=name)(x, dy)


def _rowwise(fn, rows, consts, outs, accs=(), *, tl=256, name):
    n_rows = rows[0].shape[0]
    tl = _tile(n_rows, tl)
    n_in = len(rows) + len(consts)
    n_out = len(outs)

    def body(*refs):
        res = fn(*[r[...] for r in refs[:n_in]])
        res = res if isinstance(res, (tuple, list)) else (res,)
        o_refs = refs[n_in:n_in + n_out]
        a_refs = refs[n_in + n_out:]
        for o, val in zip(o_refs, res[:n_out]):
            o[...] = val.astype(o.dtype)
        if a_refs:
            @pl.when(pl.program_id(0) == 0)
            def _():
                for a in a_refs:
                    a[...] = jnp.zeros_like(a)
            for a, val in zip(a_refs, res[n_out:]):
                a[...] += val

    in_specs = ([pl.BlockSpec((tl, r.shape[1]), lambda i: (i, 0)) for r in rows]
                + [pl.BlockSpec(c.shape, lambda i: (0, 0)) for c in consts])
    out_specs = ([pl.BlockSpec((tl, w), lambda i: (i, 0)) for w, _ in outs]
                 + [pl.BlockSpec(s, lambda i: (0, 0)) for s in accs])
    out_shape = ([jax.ShapeDtypeStruct((n_rows, w), dt) for w, dt in outs]
                 + [jax.ShapeDtypeStruct(s, F32) for s in accs])
    return pl.pallas_call(body, grid=(n_rows // tl,), in_specs=in_specs, out_specs=out_specs,
                          out_shape=out_shape, compiler_params=_cp(1), name=name)(*rows, *consts)


def _rms(x, g):
    return x * lax.rsqrt(jnp.mean(x * x, axis=-1, keepdims=True) + RMS_EPS) * g


def _sigmoid(x):
    return 1.0 / (1.0 + jnp.exp(-x))


def _glu(zz):
    d = zz.shape[1] // 2
    return zz[:, :d] * _sigmoid(zz[:, d:])


def _gelu(y):
    return jax.nn.gelu(y)


def _row2(v):
    return v.reshape(1, -1)


def _node_bwd(d_in, h, branches, *, name):
    width = h.shape[1]
    flat = [dy for _, dys in branches for dy in dys]
    counts = [len(dys) for _, dys in branches]
    gains = [_row2(g) for g, _ in branches]

    def fn(d, hh, *rest):
        dys, gs = rest[:len(flat)], rest[len(flat):]
        tot, dgs, pos = d, [], 0
        for g, cnt in zip(gs, counts):
            dy = dys[pos].astype(F32)
            for extra in dys[pos + 1:pos + cnt]:
                dy = dy + extra.astype(F32)
            pos += cnt
            _, vjp = jax.vjp(_rms, hh, g)
            dx, dg = vjp(dy)
            tot = tot + dx
            dgs.append(dg)
        return (tot, tot, *dgs)

    res = _rowwise(fn, [d_in, h, *flat], gains, [(width, F32), (width, BF16)], [(1, width)] * len(branches),
                   name=name)
    return res[0], res[1], [r[0] for r in res[2:]]


def _s5_prep_fn(lr, li, ldt, br, bi, cr, ci, *, gq, h, p):
    dt = jnp.exp(ldt)
    mag = jnp.exp(lr * dt)
    lb_re = mag * jnp.cos(li * dt)
    lb_im = mag * jnp.sin(li * dt)
    den = lr * lr + li * li
    nr = lb_re - 1.0
    fr = (nr * lr + lb_im * li) / den
    fi = (lb_im * lr - nr * li) / den
    bb_re = fr * br - fi * bi
    bb_im = fr * bi + fi * br
    shape = (gq * h, gq * p)
    r = lax.broadcasted_iota(jnp.int32, shape, 0)
    c = lax.broadcasted_iota(jnp.int32, shape, 1)
    mask = jnp.where(jnp.right_shift(r, h.bit_length() - 1) == jnp.right_shift(c, p.bit_length() - 1), 1.0, 0.0)

    def expand(t):
        return jnp.concatenate([t] * gq, axis=0) * mask

    return lb_re, lb_im, expand(bb_re), expand(bb_im), expand(cr), expand(ci)


def _s5_prep(lr, li, ldt, br, bi, cr, ci, p, *, name):
    n = lr.shape[1]
    h = br.shape[0]
    gq = S5_BLOCK_GROUPS
    nq, cq = gq * p, gq * h
    nblk = n // nq
    fn = functools.partial(_s5_prep_fn, gq=gq, h=h, p=p)

    def body(lr_r, li_r, ldt_r, br_r, bi_r, cr_r, ci_r, lbr_o, lbi_o, wbr_o, wbi_o, wcr_o, wci_o):
        lb_re, lb_im, wbr, wbi, wcr, wci = fn(lr_r[...], li_r[...], ldt_r[...], br_r[...], bi_r[...],
                                              cr_r[...], ci_r[...])
        lbr_o[...] = lb_re
        lbi_o[...] = lb_im
        wbr_o[0] = wbr.astype(BF16)
        wbi_o[0] = wbi.astype(BF16)
        wcr_o[0] = wcr.astype(BF16)
        wci_o[0] = wci.astype(BF16)

    vec = pl.BlockSpec((1, nq), lambda q: (0, q))
    tab = pl.BlockSpec((h, nq), lambda q: (0, q))
    wsp = pl.BlockSpec((1, cq, nq), lambda q: (q, 0, 0))
    wsh = jax.ShapeDtypeStruct((nblk, cq, nq), BF16)
    vsh = jax.ShapeDtypeStruct((1, n), F32)
    return pl.pallas_call(body, grid=(nblk,), in_specs=[vec, vec, vec, tab, tab, tab, tab],
                          out_specs=[vec, vec, wsp, wsp, wsp, wsp], out_shape=[vsh, vsh, wsh, wsh, wsh, wsh],
                          compiler_params=_cp(1), name=name)(lr, li, ldt, br, bi, cr, ci)


def _s5_prep_bwd(lr, li, ldt, br, bi, cr, ci, p, dlbr, dlbi, dwbr, dwbi, dwcr, dwci, *, name):
    n = lr.shape[1]
    h = br.shape[0]
    gq = S5_BLOCK_GROUPS
    nq, cq = gq * p, gq * h
    nblk = n // nq
    fn = functools.partial(_s5_prep_fn, gq=gq, h=h, p=p)

    def body(lr_r, li_r, ldt_r, br_r, bi_r, cr_r, ci_r, dlbr_r, dlbi_r, dwbr_r, dwbi_r, dwcr_r, dwci_r,
             *outs):
        _, vjp = jax.vjp(fn, lr_r[...], li_r[...], ldt_r[...], br_r[...], bi_r[...], cr_r[...], ci_r[...])
        grads = vjp((dlbr_r[0], dlbi_r[0], dwbr_r[0], dwbi_r[0], dwcr_r[0], dwci_r[0]))
        for o, g in zip(outs, grads):
            o[...] = g

    vec = pl.BlockSpec((1, nq), lambda q: (0, q))
    tab = pl.BlockSpec((h, nq), lambda q: (0, q))
    vec3 = pl.BlockSpec((1, 1, nq), lambda q: (q, 0, 0))
    wsp = pl.BlockSpec((1, cq, nq), lambda q: (q, 0, 0))
    vsh = jax.ShapeDtypeStruct((1, n), F32)
    tsh = jax.ShapeDtypeStruct((h, n), F32)
    return pl.pallas_call(body, grid=(nblk,),
                          in_specs=[vec, vec, vec, tab, tab, tab, tab, vec3, vec3, wsp, wsp, wsp, wsp],
                          out_specs=[vec, vec, vec, tab, tab, tab, tab],
                          out_shape=[vsh, vsh, vsh, tsh, tsh, tsh, tsh],
                          compiler_params=_cp(1), name=name)(lr, li, ldt, br, bi, cr, ci,
                                                             dlbr, dlbi, dwbr, dwbi, dwcr, dwci)


def _scan_rows(s_re, s_im, a_re, a_im, c_re, c_im, *, reverse):
    t_rows, n = s_re.shape
    nb = t_rows // SUBLANES
    row = lax.broadcasted_iota(jnp.int32, (SUBLANES, n), 0)

    def cmul(x, y):
        return x[0] * y[0] - x[1] * y[1], x[0] * y[1] + x[1] * y[0]

    a1 = (jnp.broadcast_to(a_re, (SUBLANES, n)), jnp.broadcast_to(a_im, (SUBLANES, n)))
    a2 = cmul(a1, a1)
    a4 = cmul(a2, a2)
    pk = (a_re, a_im)
    tab_re = jnp.zeros((SUBLANES, n), F32)
    tab_im = jnp.zeros((SUBLANES, n), F32)
    for i in range(SUBLANES):
        at = (SUBLANES - 1 - i) if reverse else i
        tab_re = jnp.where(row == at, pk[0], tab_re)
        tab_im = jnp.where(row == at, pk[1], tab_im)
        pk = cmul(pk, (a_re, a_im))

    def step(b, carry):
        cr, ci = carry
        blk = (nb - 1 - b) if reverse else b
        off = pl.multiple_of(blk * SUBLANES, SUBLANES)
        x_re = s_re[pl.ds(off, SUBLANES), :]
        x_im = s_im[pl.ds(off, SUBLANES), :]
        for d, (pr, pi) in ((1, a1), (2, a2), (4, a4)):
            if reverse:
                keep = row < SUBLANES - d
                sh = SUBLANES - d
            else:
                keep = row >= d
                sh = d
            sh_re = jnp.where(keep, pltpu.roll(x_re, sh, 0), 0.0)
            sh_im = jnp.where(keep, pltpu.roll(x_im, sh, 0), 0.0)
            x_re, x_im = x_re + pr * sh_re - pi * sh_im, x_im + pr * sh_im + pi * sh_re
        x_re, x_im = x_re + tab_re * cr - tab_im * ci, x_im + tab_re * ci + tab_im * cr
        s_re[pl.ds(off, SUBLANES), :] = x_re
        s_im[pl.ds(off, SUBLANES), :] = x_im
        edge = 0 if reverse else SUBLANES - 1
        return x_re[edge:edge + 1, :], x_im[edge:edge + 1, :]

    return lax.fori_loop(0, nb, step, (c_re, c_im))


def _s5_fwd(u, prep, dskip, *, name, comm=None):
    lb_re, lb_im, wbr, wbi, wcr, wci = prep
    n_rows, _ = u.shape
    nblk, cq, nq = wbr.shape
    tt = _tile(n_rows, 512)
    nch = n_rows // tt

    def body(u_ref, wbr_r, wbi_r, wcr_r, wci_r, lbr_r, lbi_r, d_ref, y_ref, s_re, s_im, sbr_o, sbi_o, c_re, c_im):
        @pl.when(pl.program_id(1) == 0)
        def _():
            c_re[...] = jnp.zeros_like(c_re)
            c_im[...] = jnp.zeros_like(c_im)
        uf = u_ref[...]
        ub = uf.astype(BF16)
        s_re[...] = _dot(ub, wbr_r[0], 1, 0)
        s_im[...] = _dot(ub, wbi_r[0], 1, 0)
        sbr_o[0] = c_re[...]
        sbi_o[0] = c_im[...]
        cr, ci = _scan_rows(s_re, s_im, lbr_r[...], lbi_r[...], c_re[...], c_im[...], reverse=False)
        c_re[...] = cr
        c_im[...] = ci
        y = _dot(s_re[...].astype(BF16), wcr_r[0], 1, 1) - _dot(s_im[...].astype(BF16), wci_r[0], 1, 1)
        y_ref[...] = y + d_ref[...] * uf

    wsp = pl.BlockSpec((1, cq, nq), lambda q, i: (q, 0, 0))
    vec = pl.BlockSpec((1, nq), lambda q, i: (0, q))
    act = pl.BlockSpec((tt, cq), lambda q, i: (i, q))
    sb = pl.BlockSpec((1, 1, nq), lambda q, i: (i, 0, q))
    sbsh = jax.ShapeDtypeStruct((nch, 1, nblk * nq), F32)
    states = pl.BlockSpec((tt, nq), lambda q, i: (i, q))
    stsh = jax.ShapeDtypeStruct((n_rows, nblk * nq), F32)
    return _call(
        body, grid=(nblk, nch),
        in_specs=[act, wsp, wsp, wsp, wsp, vec, vec, pl.BlockSpec((1, cq), lambda q, i: (0, q))],
        out_specs=[act, states, states, sb, sb],
        out_shape=[jax.ShapeDtypeStruct(u.shape, F32), stsh, stsh, sbsh, sbsh],
        scratch_shapes=[pltpu.VMEM((1, nq), F32), pltpu.VMEM((1, nq), F32)],
        args=(u, wbr, wbi, wcr, wci, lb_re, lb_im, dskip), name=name, comm=comm)


def _s5_bwd(u, dy, st_re, st_im, sb_re, sb_im, prep, dskip, *, name, comm=None):
    lb_re, lb_im, wbr, wbi, wcr, wci = prep
    n_rows, _ = u.shape
    nblk, cq, nq = wbr.shape
    tt = _tile(n_rows, 512)
    nch = n_rows // tt

    def body(u_ref, dy_ref, s_re, s_im, sbr_r, sbi_r, wbr_r, wbi_r, wcr_r, wci_r, lbr_r, lbi_r, d_ref,
             du_ref, dwbr, dwbi, dwcr, dwci, dlbr, dlbi, dd_ref, g_re, g_im, lc_re, lc_im):
        @pl.when(pl.program_id(1) == 0)
        def _():
            for ref in (lc_re, lc_im, dwbr, dwbi, dwcr, dwci, dlbr, dlbi, dd_ref):
                ref[...] = jnp.zeros_like(ref)
        uf = u_ref[...]
        ub = uf.astype(BF16)
        dyf = dy_ref[...]
        dyb = dyf.astype(BF16)
        sr16 = s_re[...].astype(BF16)
        si16 = s_im[...].astype(BF16)
        dwcr[0] += _dot(dyb, sr16, 0, 0)
        dwci[0] -= _dot(dyb, si16, 0, 0)
        g_re[...] = _dot(dyb, wcr_r[0], 1, 0)
        g_im[...] = -_dot(dyb, wci_r[0], 1, 0)
        lcr, lci = _scan_rows(g_re, g_im, lbr_r[...], -lbi_r[...], lc_re[...], lc_im[...], reverse=True)
        lc_re[...] = lcr
        lc_im[...] = lci
        lam_r = g_re[...]
        lam_i = g_im[...]
        first = lax.broadcasted_iota(jnp.int32, (tt, nq), 0) == 0
        prev_r = jnp.where(first, sbr_r[0], pltpu.roll(s_re[...], 1, 0))
        prev_i = jnp.where(first, sbi_r[0], pltpu.roll(s_im[...], 1, 0))
        dlbr[0] += jnp.sum(lam_r * prev_r + lam_i * prev_i, axis=0, keepdims=True)
        dlbi[0] += jnp.sum(lam_i * prev_r - lam_r * prev_i, axis=0, keepdims=True)
        lr16 = lam_r.astype(BF16)
        li16 = lam_i.astype(BF16)
        du_ref[...] = _dot(lr16, wbr_r[0], 1, 1) + _dot(li16, wbi_r[0], 1, 1) + d_ref[...] * dyf
        dwbr[0] += _dot(ub, lr16, 0, 0)
        dwbi[0] += _dot(ub, li16, 0, 0)
        dd_ref[0] += jnp.sum(dyf * uf, axis=0, keepdims=True)

    last = nch - 1
    wsp = pl.BlockSpec((1, cq, nq), lambda q, i: (q, 0, 0))
    vec = pl.BlockSpec((1, nq), lambda q, i: (0, q))
    act = pl.BlockSpec((tt, cq), lambda q, i: (last - i, q))
    sb = pl.BlockSpec((1, 1, nq), lambda q, i: (last - i, 0, q))
    vec3 = pl.BlockSpec((1, 1, nq), lambda q, i: (q, 0, 0))
    dsp = pl.BlockSpec((1, 1, cq), lambda q, i: (q, 0, 0))
    wsh = jax.ShapeDtypeStruct((nblk, cq, nq), F32)
    v3sh = jax.ShapeDtypeStruct((nblk, 1, nq), F32)
    big = pltpu.VMEM((tt, nq), F32)
    states = pl.BlockSpec((tt, nq), lambda q, i: (last - i, q))
    return _call(
        body, grid=(nblk, nch),
        in_specs=[act, act, states, states, sb, sb, wsp, wsp, wsp, wsp, vec, vec,
                  pl.BlockSpec((1, cq), lambda q, i: (0, q))],
        out_specs=[act, wsp, wsp, wsp, wsp, vec3, vec3, dsp],
        out_shape=[jax.ShapeDtypeStruct(u.shape, F32), wsh, wsh, wsh, wsh, v3sh, v3sh,
                   jax.ShapeDtypeStruct((nblk, 1, cq), F32)],
        scratch_shapes=[big, big, pltpu.VMEM((1, nq), F32), pltpu.VMEM((1, nq), F32)],
        args=(u, dy, st_re, st_im, sb_re, sb_im, wbr, wbi, wcr, wci, lb_re, lb_im, dskip), name=name, comm=comm)


def _conv_taps(cur, prev, w, b):
    rid = lax.broadcasted_iota(jnp.int32, cur.shape, 0)
    x1 = jnp.where(rid == 0, prev[7:8, :], pltpu.roll(cur, 1, 0))
    x2 = jnp.where(rid == 0, prev[6:7, :], jnp.where(rid == 1, prev[7:8, :], pltpu.roll(cur, 2, 0)))
    return b + x2 * w[0:1, :] + x1 * w[1:2, :] + cur * w[2:3, :], x1, x2


def _conv_fwd(uu, cw, cb, *, name):
    n_rows, f2 = uu.shape
    f = f2 // 2
    tc = _tile(f, 1408, LANES)
    tl = _tile(n_rows, 256)
    nfb = f // tc

    def body(g_ref, u_ref, wg_ref, wu_ref, bg_ref, bu_ref, o_ref, pg, pu):
        @pl.when(pl.program_id(1) == 0)
        def _():
            pg[...] = jnp.zeros_like(pg)
            pu[...] = jnp.zeros_like(pu)
        gcur = g_ref[...]
        ucur = u_ref[...]
        cg, _, _ = _conv_taps(gcur, pg[...], wg_ref[...], bg_ref[...])
        cu, _, _ = _conv_taps(ucur, pu[...], wu_ref[...], bu_ref[...])
        o_ref[...] = (cg * _sigmoid(cg) * cu).astype(o_ref.dtype)
        pg[...] = gcur[tl - SUBLANES:, :]
        pu[...] = ucur[tl - SUBLANES:, :]

    return pl.pallas_call(
        body, grid=(nfb, n_rows // tl),
        in_specs=[pl.BlockSpec((tl, tc), lambda j, i: (i, j)), pl.BlockSpec((tl, tc), lambda j, i: (i, j + nfb)),
                  pl.BlockSpec((DT_CONV_TAPS, tc), lambda j, i: (0, j)),
                  pl.BlockSpec((DT_CONV_TAPS, tc), lambda j, i: (0, j + nfb)),
                  pl.BlockSpec((1, tc), lambda j, i: (0, j)), pl.BlockSpec((1, tc), lambda j, i: (0, j + nfb))],
        out_specs=pl.BlockSpec((tl, tc), lambda j, i: (i, j)),
        out_shape=jax.ShapeDtypeStruct((n_rows, f), BF16),
        scratch_shapes=[pltpu.VMEM((SUBLANES, tc), F32), pltpu.VMEM((SUBLANES, tc), F32)],
        compiler_params=_cp(2), name=name)(uu, uu, cw, cw, cb, cb)


def _conv_bwd(uu, dact, cw, cb, *, name):
    n_rows, f2 = uu.shape
    f = f2 // 2
    tc = _tile(f, 1408, LANES)
    tl = _tile(n_rows, 256)
    nfb = f // tc
    nrb = n_rows // tl
    halo_per_tile = tl // SUBLANES

    def body(g_ref, gh_ref, u_ref, uh_ref, da_ref, wg_ref, wu_ref, bg_ref, bu_ref,
             duu_ref, dw_ref, db_ref, nxt_g, nxt_u):
        i = pl.program_id(1)
        rb = nrb - 1 - i

        @pl.when(i == 0)
        def _():
            for ref in (nxt_g, nxt_u, dw_ref, db_ref):
                ref[...] = jnp.zeros_like(ref)
        has_prev = jnp.where(rb > 0, 1.0, 0.0)
        gcur, ucur = g_ref[...], u_ref[...]
        wg, wu = wg_ref[...], wu_ref[...]
        cg, g1, g2 = _conv_taps(gcur, gh_ref[...] * has_prev, wg, bg_ref[...])
        cu, u1, u2 = _conv_taps(ucur, uh_ref[...] * has_prev, wu, bu_ref[...])
        sg = _sigmoid(cg)
        silu = cg * sg
        da = da_ref[...]
        rid = lax.broadcasted_iota(jnp.int32, da.shape, 0)

        def transpose_conv(plane, d, cur, x1, x2, w, nxt):
            nx = nxt[...]
            d1 = jnp.where(rid == tl - 1, nx[0:1, :], pltpu.roll(d, tl - 1, 0))
            d2 = jnp.where(rid == tl - 2, nx[0:1, :],
                           jnp.where(rid == tl - 1, nx[1:2, :], pltpu.roll(d, tl - 2, 0)))
            duu_ref[plane] = (w[2:3, :] * d + w[1:2, :] * d1 + w[0:1, :] * d2).astype(duu_ref.dtype)
            nxt[...] = d[0:SUBLANES, :]
            dw_ref[plane] += jnp.concatenate([jnp.sum(d * x2, axis=0, keepdims=True),
                                              jnp.sum(d * x1, axis=0, keepdims=True),
                                              jnp.sum(d * cur, axis=0, keepdims=True)], axis=0)
            db_ref[plane] += jnp.sum(d, axis=0, keepdims=True)

        transpose_conv(0, da * cu * (sg * (1.0 + cg * (1.0 - sg))), gcur, g1, g2, wg, nxt_g)
        transpose_conv(1, da * silu, ucur, u1, u2, wu, nxt_u)

    def halo(j, i):
        return jnp.maximum((nrb - 1 - i) * halo_per_tile - 1, 0)

    return pl.pallas_call(
        body, grid=(nfb, nrb),
        in_specs=[pl.BlockSpec((tl, tc), lambda j, i: (nrb - 1 - i, j)),
                  pl.BlockSpec((SUBLANES, tc), lambda j, i: (halo(j, i), j)),
                  pl.BlockSpec((tl, tc), lambda j, i: (nrb - 1 - i, j + nfb)),
                  pl.BlockSpec((SUBLANES, tc), lambda j, i: (halo(j, i), j + nfb)),
                  pl.BlockSpec((tl, tc), lambda j, i: (nrb - 1 - i, j)),
                  pl.BlockSpec((DT_CONV_TAPS, tc), lambda j, i: (0, j)),
                  pl.BlockSpec((DT_CONV_TAPS, tc), lambda j, i: (0, j + nfb)),
                  pl.BlockSpec((1, tc), lambda j, i: (0, j)),
                  pl.BlockSpec((1, tc), lambda j, i: (0, j + nfb))],
        out_specs=[pl.BlockSpec((2, tl, tc), lambda j, i: (0, nrb - 1 - i, j)),
                   pl.BlockSpec((2, DT_CONV_TAPS, tc), lambda j, i: (0, 0, j)),
                   pl.BlockSpec((2, 1, tc), lambda j, i: (0, 0, j))],
        out_shape=[jax.ShapeDtypeStruct((2, n_rows, f), BF16), jax.ShapeDtypeStruct((2, DT_CONV_TAPS, f), F32),
                   jax.ShapeDtypeStruct((2, 1, f), F32)],
        scratch_shapes=[pltpu.VMEM((SUBLANES, tc), F32), pltpu.VMEM((SUBLANES, tc), F32)],
        compiler_params=_cp(2), name=name)(uu, uu, uu, uu, dact, cw, cw, cb, cb)


def _log_sigmoid(x):
    t = jnp.exp(-jnp.abs(x))
    log1p_t = jnp.where(t < 1e-3, t * (1.0 - t * (0.5 - t * (1.0 / 3.0))), jnp.log(1.0 + t))
    return jnp.minimum(x, 0.0) - log1p_t


def _dlog_sigmoid(x):
    t = jnp.exp(-jnp.abs(x))
    return jnp.where(x >= 0, t, 1.0) / (1.0 + t)


def _tri_dot(tri, x):
    return jnp.dot(tri, x, precision=lax.Precision.HIGHEST, preferred_element_type=F32)


def _cum_fwd(fl, bf, *, name):
    n_rows, width = fl.shape
    tc = _tile(n_rows, 256)

    def body(fl_ref, bf_ref, o_ref, carry):
        @pl.when(pl.program_id(0) == 0)
        def _():
            carry[...] = jnp.zeros_like(carry)
        x = _log_sigmoid(fl_ref[...] + bf_ref[...])
        r = lax.broadcasted_iota(jnp.int32, (tc, tc), 0)
        c = lax.broadcasted_iota(jnp.int32, (tc, tc), 1)
        y = _tri_dot(jnp.where(r >= c, 1.0, 0.0), x) + carry[...]
        o_ref[...] = y
        carry[...] = y[tc - 1:tc, :]

    return pl.pallas_call(
        body, grid=(n_rows // tc,),
        in_specs=[pl.BlockSpec((tc, width), lambda i: (i, 0)), pl.BlockSpec((1, width), lambda i: (0, 0))],
        out_specs=pl.BlockSpec((tc, width), lambda i: (i, 0)),
        out_shape=jax.ShapeDtypeStruct(fl.shape, F32),
        scratch_shapes=[pltpu.VMEM((1, width), F32)], compiler_params=_cp(1), name=name)(fl, bf)


def _cum_bwd(dcum, fl, bf, *, name):
    n_rows, width = fl.shape
    tc = _tile(n_rows, 256)
    last = n_rows // tc - 1

    def body(dc_ref, fl_ref, bf_ref, dfl_ref, dbf_ref, carry):
        @pl.when(pl.program_id(0) == 0)
        def _():
            carry[...] = jnp.zeros_like(carry)
            dbf_ref[...] = jnp.zeros_like(dbf_ref)
        r = lax.broadcasted_iota(jnp.int32, (tc, tc), 0)
        c = lax.broadcasted_iota(jnp.int32, (tc, tc), 1)
        dls = _tri_dot(jnp.where(r <= c, 1.0, 0.0), dc_ref[...]) + carry[...]
        carry[...] = dls[0:1, :]
        dfl = dls * _dlog_sigmoid(fl_ref[...] + bf_ref[...])
        dfl_ref[...] = dfl.astype(dfl_ref.dtype)
        dbf_ref[...] += jnp.sum(dfl, axis=0, keepdims=True)

    return pl.pallas_call(
        body, grid=(n_rows // tc,),
        in_specs=[pl.BlockSpec((tc, width), lambda i: (last - i, 0)),
                  pl.BlockSpec((tc, width), lambda i: (last - i, 0)),
                  pl.BlockSpec((1, width), lambda i: (0, 0))],
        out_specs=[pl.BlockSpec((tc, width), lambda i: (last - i, 0)), pl.BlockSpec((1, width), lambda i: (0, 0))],
        out_shape=[jax.ShapeDtypeStruct(fl.shape, BF16), jax.ShapeDtypeStruct((1, width), F32)],
        scratch_shapes=[pltpu.VMEM((1, width), F32)], compiler_params=_cp(1), name=name)(dcum, fl, bf)


def _head_masks():
    lane = lax.broadcasted_iota(jnp.int32, (1, LANES), 1)
    return (lane < HEAD_DIM, lane >= HEAD_DIM)


def _flash_fwd(q, kv, cq3, ck3, *, tq, name, comm=None):
    n_rows, d = q.shape
    nhp = d // LANES
    tk = tq
    nq = n_rows // tq
    nk = n_rows // tk

    rt = _tile(tq, FLASH_ROW_TILE)

    def body(qi_ref, kj_ref, q_ref, k_ref, v_ref, cq_ref, ck_ref, o_ref, lse_ref, m0, m1, l0, l1, acc,
             s0, s1, p0, p1):
        i = qi_ref[pl.program_id(1)]
        j = kj_ref[pl.program_id(1)]
        ms, ls = (m0, m1), (l0, l1)

        @pl.when(j == 0)
        def _():
            for h in range(2):
                ms[h][...] = jnp.full_like(ms[h], -jnp.inf)
                ls[h][...] = jnp.zeros_like(ls[h])
            acc[...] = jnp.zeros_like(acc)

        def block(diagonal):
            qv, kk, vv = q_ref[...], k_ref[...], v_ref[...]
            a = acc[...]
            for h, msk in enumerate(_head_masks()):
                s_sc, p_sc = ((s0, p0), (s1, p1))[h]
                s_sc[...] = _dot(jnp.where(msk, qv, jnp.zeros_like(qv)), kk, 1, 1)
                bias = cq_ref[0, 0:1, h:h + 1] - ck_ref[0, h:h + 1, :]
                m_old, l_old = ms[h][...], ls[h][...]
                m_tiles, sum_tiles = [], []
                for r in range(tq // rt):
                    rows = slice(r * rt, (r + 1) * rt)
                    s = s_sc[rows, :] + bias
                    if diagonal:
                        rr = r * rt + lax.broadcasted_iota(jnp.int32, (rt, tk), 0)
                        cc = lax.broadcasted_iota(jnp.int32, (rt, tk), 1)
                        s = jnp.where(cc <= rr, s, -jnp.inf)
                    s_sc[rows, :] = s
                    m_tiles.append(jnp.maximum(m_old[rows, :], jnp.max(s, axis=1, keepdims=True)))
                for r in range(tq // rt):
                    rows = slice(r * rt, (r + 1) * rt)
                    p = jnp.exp(s_sc[rows, :] - jnp.tile(m_tiles[r], (1, tk // LANES)))
                    sum_tiles.append(jnp.sum(p, axis=1, keepdims=True))
                    p_sc[rows, :] = p.astype(BF16)
                m_new = jnp.concatenate(m_tiles, axis=0)
                alpha = jnp.exp(m_old - m_new)
                ms[h][...] = m_new
                ls[h][...] = alpha * l_old + jnp.concatenate(sum_tiles, axis=0)
                pv = _dot(p_sc[...], jnp.where(msk, vv, jnp.zeros_like(vv)), 1, 0)
                a = a * jnp.where(msk, alpha, 1.0) + pv
            acc[...] = a

        pl.when(j < i)(functools.partial(block, False))
        pl.when(j == i)(functools.partial(block, True))

        @pl.when(j == i)
        def _():
            m_a, m_b = _head_masks()
            o_ref[...] = acc[...] * jnp.where(m_a, 1.0 / l0[...], 1.0 / l1[...])
            two = lax.broadcasted_iota(jnp.int32, (tq, 2), 1)
            lse_a = (m0[...] + jnp.log(l0[...]))[:, 0:1]
            lse_b = (m1[...] + jnp.log(l1[...]))[:, 0:1]
            lse_ref[0] = jnp.where(two == 0, lse_a, lse_b)

    pairs = [(i, j) for i in range(nq) for j in range(i + 1)]
    qi = jnp.asarray([i for i, _ in pairs], jnp.int32)
    kj = jnp.asarray([j for _, j in pairs], jnp.int32)
    col = pltpu.VMEM((tq, LANES), F32)
    return _call(
        body, grid=(nhp, len(pairs)), prefetch=(qi, kj),
        in_specs=[pl.BlockSpec((tq, LANES), lambda hp, t, qi, kj: (qi[t], hp)),
                  pl.BlockSpec((tk, LANES), lambda hp, t, qi, kj: (kj[t], hp)),
                  pl.BlockSpec((tk, LANES), lambda hp, t, qi, kj: (kj[t], nhp + hp)),
                  pl.BlockSpec((1, tq, 2), lambda hp, t, qi, kj: (hp, qi[t], 0)),
                  pl.BlockSpec((1, 2, tk), lambda hp, t, qi, kj: (hp, 0, kj[t]))],
        out_specs=[pl.BlockSpec((tq, LANES), lambda hp, t, qi, kj: (qi[t], hp)),
                   pl.BlockSpec((1, tq, 2), lambda hp, t, qi, kj: (hp, qi[t], 0))],
        scratch_shapes=[col, col, col, col, pltpu.VMEM((tq, LANES), F32), pltpu.VMEM((tq, tk), F32),
                        pltpu.VMEM((tq, tk), F32), pltpu.VMEM((tq, tk), BF16), pltpu.VMEM((tq, tk), BF16)],
        out_shape=[jax.ShapeDtypeStruct((n_rows, d), F32), jax.ShapeDtypeStruct((nhp, n_rows, 2), F32)],
        args=(q, kv, kv, cq3, ck3), name=name, comm=comm)


def _flash_bwd(q, kv, o, lse, do, cq3, ck3, *, tq, name, comm=None):
    n_rows, d = q.shape
    nhp = d // LANES
    tk = tq
    nq = n_rows // tq
    nk = n_rows // tk

    rt = _tile(tq, FLASH_ROW_TILE)

    def body(qi_ref, kj_ref, q_ref, k_ref, v_ref, o_ref, lse_ref, do_ref, cq_ref, ck_ref,
             dq_ref, dk_ref, dv_ref, dck_ref, dcq_ref, s0, s1, dp0, dp1, p0, p1, ds0, ds1, dl0, dl1, lb0, lb1):
        i = qi_ref[pl.program_id(1)]
        j = kj_ref[pl.program_id(1)]

        @pl.when(pl.program_id(1) == 0)
        def _():
            dq_ref[...] = jnp.zeros_like(dq_ref)
            dcq_ref[...] = jnp.zeros_like(dcq_ref)

        @pl.when(i == j)
        def _():
            dk_ref[...] = jnp.zeros_like(dk_ref)
            dv_ref[...] = jnp.zeros_like(dv_ref)
            dck_ref[...] = jnp.zeros_like(dck_ref)

        def block(diagonal):
            qv, kk, vv = q_ref[...], k_ref[...], v_ref[...]
            dob = do_ref[...].astype(BF16)
            prod = dob.astype(F32) * o_ref[...]
            off = pl.multiple_of(i * tq, tq)
            lse_all = lse_ref[0]
            dck_rows, row_sums = [], []
            dq_acc = jnp.zeros((tq, LANES), F32)
            dk_acc = jnp.zeros((tk, LANES), F32)
            dv_acc = jnp.zeros((tk, LANES), F32)
            for h, msk in enumerate(_head_masks()):
                s_sc, dp_sc, p_sc, ds_sc = ((s0, dp0, p0, ds0), (s1, dp1, p1, ds1))[h]
                qh = jnp.where(msk, qv, jnp.zeros_like(qv))
                kh = jnp.where(msk, kk, jnp.zeros_like(kk))
                doh = jnp.where(msk, dob, jnp.zeros_like(dob))
                s_sc[...] = _dot(qh, kk, 1, 1)
                dp_sc[...] = _dot(doh, vv, 1, 1)
                bias = cq_ref[0, 0:1, h:h + 1] - ck_ref[0, h:h + 1, :]
                delta, lse_h = ((dl0, lb0), (dl1, lb1))[h]
                delta[...] = jnp.broadcast_to(jnp.sum(jnp.where(msk, prod, 0.0), axis=1, keepdims=True),
                                              (tq, LANES))
                lse_h[...] = jnp.broadcast_to(lse_all[:, h:h + 1], (tq, LANES))
                reps = (1, tk // LANES)
                col_acc = jnp.zeros((SUBLANES, tk), F32)
                rs = []
                for r in range(tq // rt):
                    rows = slice(r * rt, (r + 1) * rt)
                    s = s_sc[rows, :] + bias
                    if diagonal:
                        rr = r * rt + lax.broadcasted_iota(jnp.int32, (rt, tk), 0)
                        cc = lax.broadcasted_iota(jnp.int32, (rt, tk), 1)
                        s = jnp.where(cc <= rr, s, -jnp.inf)
                    p = jnp.exp(s - jnp.tile(lse_h[rows, :], reps))
                    ds = p * (dp_sc[rows, :] - jnp.tile(delta[rows, :], reps))
                    rs.append(jnp.sum(ds, axis=1, keepdims=True))
                    for g in range(rt // SUBLANES):
                        col_acc = col_acc + ds[g * SUBLANES:(g + 1) * SUBLANES, :]
                    p_sc[rows, :] = p.astype(BF16)
                    ds_sc[rows, :] = ds.astype(BF16)
                dck_rows.append(-jnp.sum(col_acc, axis=0, keepdims=True))
                row_sums.append(jnp.concatenate(rs, axis=0))
                dv_acc = dv_acc + _dot(p_sc[...], doh, 0, 0)
                dsb = ds_sc[...]
                dq_acc = dq_acc + _dot(dsb, kh, 1, 0)
                dk_acc = dk_acc + _dot(dsb, qh, 0, 0)
            dq_ref[pl.ds(off, tq), :] += dq_acc
            dk_ref[...] += dk_acc
            dv_ref[...] += dv_acc
            two = lax.broadcasted_iota(jnp.int32, (tq, 2), 1)
            dcq_ref[0, pl.ds(off, tq), :] += jnp.where(two == 0, row_sums[0], row_sums[1])
            dck_ref[0] += jnp.concatenate(dck_rows, axis=0)

        pl.when(i > j)(functools.partial(block, False))
        pl.when(i == j)(functools.partial(block, True))

    pairs = [(i, j) for j in range(nk) for i in range(j, nq)]
    qi = jnp.asarray([i for i, _ in pairs], jnp.int32)
    kj = jnp.asarray([j for _, j in pairs], jnp.int32)
    score = pltpu.VMEM((tq, tk), F32)
    score16 = pltpu.VMEM((tq, tk), BF16)
    rowstat = pltpu.VMEM((tq, LANES), F32)
    return _call(
        body, grid=(nhp, len(pairs)), prefetch=(qi, kj),
        in_specs=[pl.BlockSpec((tq, LANES), lambda hp, t, qi, kj: (qi[t], hp)),
                  pl.BlockSpec((tk, LANES), lambda hp, t, qi, kj: (kj[t], hp)),
                  pl.BlockSpec((tk, LANES), lambda hp, t, qi, kj: (kj[t], nhp + hp)),
                  pl.BlockSpec((tq, LANES), lambda hp, t, qi, kj: (qi[t], hp)),
                  pl.BlockSpec((1, tq, 2), lambda hp, t, qi, kj: (hp, qi[t], 0)),
                  pl.BlockSpec((tq, LANES), lambda hp, t, qi, kj: (qi[t], hp)),
                  pl.BlockSpec((1, tq, 2), lambda hp, t, qi, kj: (hp, qi[t], 0)),
                  pl.BlockSpec((1, 2, tk), lambda hp, t, qi, kj: (hp, 0, kj[t]))],
        out_specs=[pl.BlockSpec((n_rows, LANES), lambda hp, t, qi, kj: (0, hp)),
                   pl.BlockSpec((tk, LANES), lambda hp, t, qi, kj: (kj[t], hp)),
                   pl.BlockSpec((tk, LANES), lambda hp, t, qi, kj: (kj[t], hp)),
                   pl.BlockSpec((1, 2, tk), lambda hp, t, qi, kj: (hp, 0, kj[t])),
                   pl.BlockSpec((1, n_rows, 2), lambda hp, t, qi, kj: (hp, 0, 0))],
        scratch_shapes=[score, score, score, score, score16, score16, score16, score16, rowstat, rowstat,
                        rowstat, rowstat],
        out_shape=[jax.ShapeDtypeStruct((n_rows, d), F32), jax.ShapeDtypeStruct((n_rows, d), F32),
                   jax.ShapeDtypeStruct((n_rows, d), F32), jax.ShapeDtypeStruct((nhp, 2, n_rows), F32),
                   jax.ShapeDtypeStruct((nhp, n_rows, 2), F32)],
        args=(q, kv, kv, o, lse, do, cq3, ck3), name=name, comm=comm)


def _s5_tables(w, layer):
    g, p = w["lam_re"].shape[1:]
    h = w["ssm_b_re"].shape[3]
    n = g * p
    lr = w["lam_re"][layer].reshape(1, n)
    li = w["lam_im"][layer].reshape(1, n)
    ldt = jnp.broadcast_to(w["log_dt"][layer][:, None], (g, p)).reshape(1, n)
    br = w["ssm_b_re"][layer].transpose(2, 0, 1).reshape(h, n)
    bi = w["ssm_b_im"][layer].transpose(2, 0, 1).reshape(h, n)
    cr = w["ssm_c_re"][layer].transpose(1, 0, 2).reshape(h, n)
    ci = w["ssm_c_im"][layer].transpose(1, 0, 2).reshape(h, n)
    return (lr, li, ldt, br, bi, cr, ci), (g, p, h)


def _local_step(x, tgt, w, net=None, *, attn_tile=512):
    n_rows, d = x.shape
    n_layers = w["g_mix"].shape[0]
    n_s5 = w["lam_re"].shape[0]
    nh = w["b_f"].shape[0]
    nhp = nh // 2
    assert d == nh * HEAD_DIM
    tq = _tile(n_rows, attn_tile)
    g = {}
    saved = [dict() for _ in range(n_layers)]
    big = {}
    pending = {}

    def wt(name, layer):
        return w[name][layer]

    def carry_gather(group, run):
        if net is None or not net.has_group(group):
            return run(None)
        outs, got = run(net.gather_comm(group))
        net.store_gathered(group, got, w)
        return outs

    def carry_reduce(tag, run):
        if net is None or not pending:
            big.update(pending)
            pending.clear()
            return run(None)
        keys = list(pending)
        parts = net.reduce_prepare([pending[k] for k in keys], tag)
        outs, landed = run(_chip_exchange_comm(parts))
        big.update(zip(keys, net.reduce_finish(parts, landed, [pending[k] for k in keys], tag)))
        pending.clear()
        return outs

    h = x
    nxt = _rowwise(lambda a, gg: _rms(a, gg), [x], [_row2(w["g_mix"][0])], [(d, F32)], name="rms_first")[0]
    kvb = fl = cum = cq3 = ck3 = hnkv = None
    bf_pad = jnp.zeros((1, LANES), F32).at[0, :nh].set(w["b_f"])
    for l in range(n_layers):
        sv = saved[l]
        sv["h"] = h
        g_ffn = _row2(w["g_ffn"][l])
        if l < n_s5:
            tabs, (_, p, _) = _s5_tables(w, l)
            prep = _s5_prep(*tabs, p, name=f"s5_prep{l}")
            dskip = w["ssm_d"][l].reshape(1, d)
            y, st_re, st_im, sb_re, sb_im = carry_gather(
                f"stage{l}", lambda comm, u=nxt, pr=prep, ds=dskip: _s5_fwd(u, pr, ds, name=f"s5_fwd{l}", comm=comm))
            z = _rowwise(_gelu, [y], [], [(d, BF16)], name=f"gelu{l}")[0]
            zz = _mm_cols(z, *wt("w_glu", l), wc=0, name=f"glu_mm{l}")
            h1, hn2 = _rowwise(lambda hh, zq, gg: ((lambda t: (t, _rms(t, gg)))(hh + _glu(zq))),
                               [h, zz], [g_ffn], [(d, F32), (d, BF16)], name=f"mix_out{l}")
            sv.update(u=nxt, prep=prep, tabs=tabs, p=p, dskip=dskip, st_re=st_re, st_im=st_im, sb_re=sb_re,
                      sb_im=sb_im, y=y, z=z, zz=zz)
        else:
            j = l - n_s5
            qs = _mm_cols(nxt, *wt("w_q", j), wc=0, out_dtype=BF16, scale=HEAD_DIM ** -0.5, name=f"q_mm{j}")
            o, lse = carry_gather(
                f"stage{l}", lambda comm, q_=qs: _flash_fwd(q_, kvb, cq3, ck3, tq=tq, name=f"flash_fwd{j}", comm=comm))
            a = _mm_cols(o, *wt("w_o", j), wc=0, name=f"o_mm{j}")
            h1, hn2 = _rowwise(lambda hh, aa, gg: ((lambda t: (t, _rms(t, gg)))(hh + aa)),
                               [h, a], [g_ffn], [(d, F32), (d, BF16)], name=f"mix_out{l}")
            sv.update(hn=nxt, qs=qs, o=o, lse=lse)
        uu = _mm_cols(hn2, *wt("w_in", l), wc=0, name=f"ffn_in{l}")
        cw, cb = w["conv_w"][l], _row2(w["conv_b"][l])
        act = _conv_fwd(uu, cw, cb, name=f"conv_fwd{l}")
        f = _mm_cols(act, *wt("w_out", l), wc=0, name=f"ffn_out{l}")
        sv.update(h1=h1, hn2=hn2, uu=uu, act=act, cw=cw, cb=cb)
        if l == n_layers - 1:
            def loss_fn(hh, ff, tt, gg):
                yv, vjp = jax.vjp(_rms, hh + ff, gg)
                err = yv - tt
                part = 0.5 * jnp.sum(jnp.mean(err * err, axis=-1, keepdims=True), axis=0, keepdims=True)
                dh, dg = vjp(err * (1.0 / d))
                return dh, dh, jnp.broadcast_to(part, (1, LANES)), dg
            dcur, dcur16, loss_row, dgf = _rowwise(loss_fn, [h1, f, tgt], [_row2(w["g_final"])],
                                                   [(d, F32), (d, BF16)], [(1, LANES), (1, d)], name="loss")
            loss = loss_row[0, 0]
            g["g_final"] = dgf[0]
        elif l + 1 < n_s5:
            h, nxt = _rowwise(lambda hh, ff, gg: ((lambda t: (t, _rms(t, gg)))(hh + ff)), [h1, f],
                              [_row2(w["g_mix"][l + 1])], [(d, F32), (d, F32)], name=f"ffn_res{l}")
        elif l + 1 == n_s5:
            h, nxt, hnkv = _rowwise(
                lambda hh, ff, g1, g2: ((lambda t: (t, _rms(t, g1), _rms(t, g2)))(hh + ff)), [h1, f],
                [_row2(w["g_mix"][l + 1]), _row2(w["g_kv"])], [(d, F32), (d, BF16), (d, BF16)], name=f"ffn_res{l}")
            kvb = _mm_cols(hnkv, *wt("w_kv", 0), wc=0, out_dtype=BF16, name="kv_mm")
            fl = _mm_cols(hnkv, *wt("w_f", 0), wc=0, name="f_mm")
            cum = _cum_fwd(fl, bf_pad, name="cum_fwd")
            cq3 = cum[:, :nh].reshape(n_rows, nhp, 2).transpose(1, 0, 2)
            ck3 = cum[:, :nh].T.reshape(nhp, 2, n_rows)
        else:
            h, nxt = _rowwise(lambda hh, ff, gg: ((lambda t: (t, _rms(t, gg)))(hh + ff)), [h1, f],
                              [_row2(w["g_mix"][l + 1])], [(d, F32), (d, BF16)], name=f"ffn_res{l}")

    per_layer = {k: [None] * n_layers for k in ("g_mix", "g_ffn", "conv_w", "conv_b")}
    per_s5 = {k: [None] * n_s5 for k in ("lam_re", "lam_im", "log_dt", "ssm_b_re", "ssm_b_im", "ssm_c_re",
                                         "ssm_c_im", "ssm_d")}
    dk_parts, dv_parts, dck_parts = [], [], []

    def by_row_shard(m):
        return m.reshape(N_CHIPS, m.shape[0] // N_CHIPS, m.shape[1])

    for l in reversed(range(n_layers)):
        sv = saved[l]
        dact = _mm_cols(dcur16, *wt("w_out", l), wc=1, name=f"ffn_out_dx{l}")
        pending["w_ffn_out", l] = by_row_shard(_mm_tn(sv["act"], dcur16, 1, name=f"ffn_out_dw{l}")[0])
        duu, dcw, dcb = _conv_bwd(sv["uu"], dact, sv["cw"], sv["cb"], name=f"conv_bwd{l}")
        per_layer["conv_w"][l] = jnp.concatenate([dcw[0], dcw[1]], axis=-1)
        per_layer["conv_b"][l] = jnp.concatenate([dcb[0, 0], dcb[1, 0]])
        dhn2 = _mm_acc(duu, *wt("w_in", l), wc=1, name=f"ffn_in_dx{l}")
        pending["w_ffn_in", l] = _mm_tn(sv["hn2"], duu, wt("w_in", l)[0].shape[0], name=f"ffn_in_dw{l}")
        d1, d1_16, (dg,) = _node_bwd(dcur, sv["h1"], [(w["g_ffn"][l], [dhn2])], name=f"ffn_norm_bwd{l}")
        per_layer["g_ffn"][l] = dg
        if l < n_s5:
            def glu_bwd(zq, dd):
                _, vjp = jax.vjp(_glu, zq)
                return vjp(dd)[0]
            dzz = _rowwise(glu_bwd, [sv["zz"], d1], [], [(2 * d, BF16)], name=f"glu_bwd{l}")[0]
            dz = _mm_acc(dzz, *wt("w_glu", l), wc=1, name=f"glu_dx{l}")
            pending["w_glu", l] = _mm_tn(sv["z"], dzz, wt("w_glu", l)[0].shape[0], name=f"glu_dw{l}")

            def gelu_bwd(yy, dd):
                _, vjp = jax.vjp(_gelu, yy)
                return vjp(dd)[0]
            dy = _rowwise(gelu_bwd, [sv["y"], dz], [], [(d, F32)], name=f"gelu_bwd{l}")[0]
            du, dwbr, dwbi, dwcr, dwci, dlbr, dlbi, dd = carry_reduce(
                f"stage{l}", lambda comm, dy_=dy: _s5_bwd(sv["u"], dy_, sv["st_re"], sv["st_im"], sv["sb_re"],
                                                          sv["sb_im"], sv["prep"], sv["dskip"], name=f"s5_bwd{l}",
                                                          comm=comm))
            dlr, dli, dldt, dbr, dbi, dcr, dci = _s5_prep_bwd(*sv["tabs"], sv["p"], dlbr, dlbi, dwbr, dwbi, dwcr,
                                                              dwci, name=f"s5_prep_bwd{l}")
            gg, p = w["lam_re"].shape[1:]
            hh = w["ssm_b_re"].shape[3]
            per_s5["lam_re"][l] = dlr.reshape(gg, p)
            per_s5["lam_im"][l] = dli.reshape(gg, p)
            per_s5["log_dt"][l] = dldt.reshape(gg, p).sum(axis=1)
            per_s5["ssm_b_re"][l] = dbr.reshape(hh, gg, p).transpose(1, 2, 0)
            per_s5["ssm_b_im"][l] = dbi.reshape(hh, gg, p).transpose(1, 2, 0)
            per_s5["ssm_c_re"][l] = dcr.reshape(hh, gg, p).transpose(1, 0, 2)
            per_s5["ssm_c_im"][l] = dci.reshape(hh, gg, p).transpose(1, 0, 2)
            per_s5["ssm_d"][l] = dd.reshape(d)
            branches = [(w["g_mix"][l], [du])]
        else:
            j = l - n_s5
            do = _mm_cols(d1_16, *wt("w_o", j), wc=1, name=f"o_dx{j}")
            pending["w_o", j] = by_row_shard(_mm_tn(sv["o"], d1_16, 1, name=f"o_dw{j}")[0])
            dq, dk, dv, dck, dcq = carry_reduce(
                f"stage{l}", lambda comm, do_=do: _flash_bwd(sv["qs"], kvb, sv["o"], sv["lse"], do_, cq3, ck3, tq=tq,
                                                          name=f"flash_bwd{j}", comm=comm))
            dk_parts.append(dk)
            dv_parts.append(dv)
            dck_parts.append(dck.reshape(nh, n_rows).T + dcq.transpose(1, 0, 2).reshape(n_rows, nh))
            scale = HEAD_DIM ** -0.5
            dhn = _mm_cols(dq, *wt("w_q", j), wc=1, scale=scale, name=f"q_dx{j}")
            pending["w_q", j] = by_row_shard(_mm_tn(sv["hn"], dq, 1, scale=scale, name=f"q_dw{j}")[0])
            branches = [(w["g_mix"][l], [dhn])]
            if j == 0:
                def kv_sum(*parts):
                    half = len(parts) // 2
                    return jnp.concatenate([sum(parts[:half][1:], parts[0]),
                                            sum(parts[half:][1:], parts[half])], axis=1)
                dkv = _rowwise(kv_sum, dk_parts + dv_parts, [], [(2 * d, BF16)], name="dkv_sum")[0]
                dck_tot = dck_parts[0]
                for extra in dck_parts[1:]:
                    dck_tot = dck_tot + extra
                dcum = jnp.zeros((n_rows, LANES), F32).at[:, :nh].set(dck_tot)
                dfl, dbf = _cum_bwd(dcum, fl, bf_pad, name="cum_bwd")
                g["b_f"] = dbf[0, :nh]
                dhkv_a = _mm_cols(dkv, *wt("w_kv", 0), wc=1, name="kv_dx")
                dhkv_b = _mm_cols(dfl, *wt("w_f", 0), wc=1, name="f_dx")
                d_kvf = jnp.concatenate([_mm_tn(hnkv, dkv, 1, name="kv_dw")[0],
                                         _mm_tn(hnkv, dfl, 1, name="f_dw")[0][:, :nh]], axis=1)
                pending["w_kvf", 0] = d_kvf.reshape(d, N_CHIPS, -1).transpose(1, 0, 2)
                branches.append((w["g_kv"], [dhkv_a, dhkv_b]))
        dcur, dcur16, dgs = _node_bwd(d1, sv["h"], branches, name=f"mix_norm_bwd{l}")
        per_layer["g_mix"][l] = dgs[0]
        if len(dgs) > 1:
            g["g_kv"] = dgs[1]

    if pending:
        big.update(pending if net is None else
                   zip(list(pending), net.reduce_blocking([pending[k] for k in pending], "tail")))
    for k, v in (*per_layer.items(), *per_s5.items()):
        g[k] = jnp.stack(v)
    g["big"] = big
    return loss, dcur, g


def _position():
    x, y, c = lax.axis_index("x"), lax.axis_index("y"), lax.axis_index("c")
    chips = [(1 - x, y), (x, 1 - y), (1 - x, 1 - y)]
    return x, y, c, chips


def _all_gather_comm(shards):
    n = len(shards)

    def descriptors(ins, outs, sems):
        send_sems, recv_sems = sems
        x, y, c, chips = _position()
        my_slot = 2 * x + y
        sibling = (x, y, 1 - c)

        def rows(t, half):
            hr = ins[t].shape[0] // 2
            return pl.ds(half * hr, hr)

        def remote(k, t, src, dst, to):
            return pltpu.make_async_remote_copy(src_ref=src, dst_ref=dst, send_sem=send_sems.at[k, t],
                                                recv_sem=recv_sems.at[k, t], device_id=to, device_id_type=MESH)

        own = [remote(6, t, ins[t], outs[t].at[my_slot], sibling) for t in range(n)]
        ici = [remote(j, t, ins[t].at[rows(t, c)], outs[t].at[my_slot, rows(t, c)], (*chip, c))
               for j, chip in enumerate(chips) for t in range(n)]
        slots = [2 * chip[0] + chip[1] for chip in chips]
        fwd = [[remote(3 + j, t, outs[t].at[slots[j], rows(t, c)], outs[t].at[slots[j], rows(t, c)], sibling)
                for t in range(n)] for j in range(len(chips))]
        landed = [[remote(j, t, outs[t].at[slots[j], rows(t, c)], outs[t].at[slots[j], rows(t, c)], (*chips[j], c))
                   for t in range(n)] for j in range(len(chips))]
        from_sibling = [remote(3 + j, t, outs[t].at[slots[j], rows(t, 1 - c)], outs[t].at[slots[j], rows(t, 1 - c)],
                               sibling) for j in range(len(chips)) for t in range(n)]
        return own, ici, fwd, landed, from_sibling

    def start(ins, outs, sems):
        own, ici, _, _, _ = descriptors(ins, outs, sems)
        for cp in own + ici:
            cp.start()

    def finish(ins, outs, sems):
        own, ici, fwd, landed, from_sibling = descriptors(ins, outs, sems)
        for j in range(len(fwd)):
            for cp in landed[j]:
                cp.wait_recv()
            for cp in fwd[j]:
                cp.start()
        for cp in from_sibling + own:
            cp.wait_recv()
        for cp in own + ici + [cp for group in fwd for cp in group]:
            cp.wait_send()

    return _Comm(list(shards), [jax.ShapeDtypeStruct((N_CHIPS,) + a.shape, a.dtype) for a in shards],
                 [pltpu.SemaphoreType.DMA((7, n)), pltpu.SemaphoreType.DMA((7, n))], start, finish)


def _all_gather(shards, *, name):
    return _run_comm(_all_gather_comm(shards), name=name)


def _pair_exchange(grads, *, name):
    n = len(grads)

    def body(*refs):
        ins, outs = refs[:n], refs[n:2 * n]
        send_sems, recv_sems = refs[2 * n:]
        x, y, c, _ = _position()
        copies = [pltpu.make_async_remote_copy(src_ref=ins[t].at[:, 1 - c], dst_ref=outs[t],
                                               send_sem=send_sems.at[t], recv_sem=recv_sems.at[t],
                                               device_id=(x, y, 1 - c), device_id_type=MESH) for t in range(n)]
        for cp in copies:
            cp.start()
        for cp in copies:
            cp.wait()

    return pl.pallas_call(
        body, in_specs=_any_specs(n), out_specs=_any_specs(n),
        out_shape=[jax.ShapeDtypeStruct((a.shape[0],) + a.shape[2:], a.dtype) for a in grads],
        scratch_shapes=[pltpu.SemaphoreType.DMA((n,)), pltpu.SemaphoreType.DMA((n,))], name=name)(*grads)


def _chip_exchange_comm(parts):
    n = len(parts)

    def copies(ins, outs, sems):
        send_sems, recv_sems = sems
        _, _, c, chips = _position()
        return [pltpu.make_async_remote_copy(src_ref=ins[t].at[2 * chip[0] + chip[1]], dst_ref=outs[t].at[j],
                                             send_sem=send_sems.at[j, t], recv_sem=recv_sems.at[j, t],
                                             device_id=(*chip, c), device_id_type=MESH)
                for j, chip in enumerate(chips) for t in range(n)]

    def start(ins, outs, sems):
        for cp in copies(ins, outs, sems):
            cp.start()

    def finish(ins, outs, sems):
        for cp in copies(ins, outs, sems):
            cp.wait()

    return _Comm(list(parts), [jax.ShapeDtypeStruct((N_CHIPS - 1,) + a.shape[1:], a.dtype) for a in parts],
                 [pltpu.SemaphoreType.DMA((N_CHIPS - 1, n)), pltpu.SemaphoreType.DMA((N_CHIPS - 1, n))],
                 start, finish)


def _pair_share(both, *, name):
    n = len(both)

    def body(*refs):
        ins, outs = refs[:n], refs[n:2 * n]
        send_sems, recv_sems = refs[2 * n:]
        x, y, c, _ = _position()
        for t in range(n):
            pltpu.make_async_remote_copy(src_ref=ins[t].at[c], dst_ref=outs[t].at[c], send_sem=send_sems.at[t],
                                         recv_sem=recv_sems.at[t], device_id=(x, y, 1 - c),
                                         device_id_type=MESH).start()
        for t in range(n):
            pltpu.make_async_remote_copy(src_ref=ins[t].at[c], dst_ref=outs[t].at[1 - c], send_sem=send_sems.at[t],
                                         recv_sem=recv_sems.at[t], device_id=(x, y, 1 - c),
                                         device_id_type=MESH).wait()

    return pl.pallas_call(
        body, in_specs=_any_specs(n), out_specs=_any_specs(n),
        out_shape=[jax.ShapeDtypeStruct(a.shape, a.dtype) for a in both],
        input_output_aliases={t: t for t in range(n)},
        scratch_shapes=[pltpu.SemaphoreType.DMA((n,)), pltpu.SemaphoreType.DMA((n,))], name=name)(*both)


def _sum_pair(grad, landed, c, wire_dtype, *, name):
    slots, _, m, n = grad.shape
    tm = _tile(m, 256, 2 * SUBLANES)

    def body(c_ref, g_ref, l_ref, o_ref):
        o_ref[...] = (g_ref[0] + l_ref[...]).astype(wire_dtype)

    return pl.pallas_call(
        body,
        grid_spec=pltpu.PrefetchScalarGridSpec(
            num_scalar_prefetch=1, grid=(slots, m // tm),
            in_specs=[pl.BlockSpec((1, 1, tm, n), lambda s, i, c_ref: (s, c_ref[0], i, 0)),
                      pl.BlockSpec((1, tm, n), lambda s, i, c_ref: (s, i, 0))],
            out_specs=pl.BlockSpec((1, tm, n), lambda s, i, c_ref: (s, i, 0))),
        out_shape=jax.ShapeDtypeStruct((slots, m, n), wire_dtype), compiler_params=_cp(2), name=name)(
            c, grad, landed)


def _sum_chips(part, landed, slot_c, *, name):
    _, m, n = part.shape
    tm = _tile(m, 256, 2 * SUBLANES)

    def body(s_ref, p_ref, l_ref, o_ref):
        acc = p_ref[0].astype(F32)
        for j in range(N_CHIPS - 1):
            acc = acc + l_ref[j].astype(F32)
        o_ref[0] = acc

    return pl.pallas_call(
        body,
        grid_spec=pltpu.PrefetchScalarGridSpec(
            num_scalar_prefetch=1, grid=(m // tm,),
            in_specs=[pl.BlockSpec((1, tm, n), lambda i, s_ref: (s_ref[0], i, 0)),
                      pl.BlockSpec((N_CHIPS - 1, tm, n), lambda i, s_ref: (0, i, 0))],
            out_specs=pl.BlockSpec((1, tm, n), lambda i, s_ref: (s_ref[1], i, 0))),
        out_shape=jax.ShapeDtypeStruct((N_CORES, m, n), F32), compiler_params=_cp(1), name=name)(
            slot_c, part, landed)


def _reduce_prepare(grads, wire_dtypes, tag):
    c = lax.axis_index("c").reshape(1).astype(jnp.int32)
    views = []
    for a in grads:
        lead, last = a.shape[1], a.shape[-1]
        mid = 1
        for s in a.shape[2:-1]:
            mid *= s
        views.append(a.reshape(N_CHIPS, N_CORES, (lead // N_CORES) * mid, last))
    landed = _pair_exchange(views, name=f"rs_pair_exchange_{tag}")
    return [_sum_pair(v, l, c, wire_dtypes[t], name=f"rs_pair_sum_{tag}_{t}")
            for t, (v, l) in enumerate(zip(views, landed))]


def _reduce_finish(parts, landed, grads, tag):
    slot_c = jnp.stack([2 * lax.axis_index("x") + lax.axis_index("y"), lax.axis_index("c")]).astype(jnp.int32)
    both = [_sum_chips(p, l, slot_c, name=f"rs_chip_sum_{tag}_{t}") for t, (p, l) in enumerate(zip(parts, landed))]
    full = _pair_share(both, name=f"rs_pair_share_{tag}")
    return [f.reshape(a.shape[1:]) for f, a in zip(full, grads)]


def _reduce_scatter(grads, wire_dtypes, tag):
    parts = _reduce_prepare(grads, wire_dtypes, tag)
    landed = _run_comm(_chip_exchange_comm(parts), name=f"rs_chip_exchange_{tag}")
    return _reduce_finish(parts, landed, grads, tag)


class _Net:
    def __init__(self, groups, d, nh):
        self.groups, self.d, self.nh = groups, d, nh

    def has_group(self, group):
        return bool(self.groups.get(group))

    def gather_comm(self, group):
        return _all_gather_comm([shard for _, _, shard in self.groups[group]])

    def store_gathered(self, group, got, w):
        d, nh = self.d, self.nh
        for (name, layer, _), full in zip(self.groups[group], got):
            if name == "w_kvf":
                mat = full.transpose(1, 0, 2).reshape(d, -1)
                w["w_kv"][0] = (mat[:, :2 * d][None, None], 0)
                w["w_f"][0] = (jnp.zeros((d, LANES), BF16).at[:, :nh].set(mat[:, 2 * d:])[None, None], 0)
            elif name == "w_in":
                w[name][layer] = (full[:, None], 0)
            else:
                w[name][layer] = (full.reshape(1, 1, -1, full.shape[-1]), 0)

    def reduce_prepare(self, grads, tag):
        return _reduce_prepare(grads, [BF16] * len(grads), tag)

    def reduce_finish(self, parts, landed, grads, tag):
        return _reduce_finish(parts, landed, grads, tag)

    def reduce_blocking(self, grads, tag):
        return _reduce_scatter(grads, [BF16] * len(grads), tag)


def _adamw(w, g, m, v, *, name):
    def fn(ww, gg, mm, vv):
        mm = ADAM_B1 * mm + (1.0 - ADAM_B1) * gg
        vv = ADAM_B2 * vv + (1.0 - ADAM_B2) * (gg * gg)
        m_hat = mm / (1.0 - ADAM_B1 ** ADAM_STEP)
        v_hat = vv / (1.0 - ADAM_B2 ** ADAM_STEP)
        delta = -ADAM_LR * (m_hat / (jnp.sqrt(v_hat) + ADAM_EPS) + ADAM_WD * ww)
        return delta, mm, vv

    shape = w.shape
    two_d = [a.reshape(-1, shape[-1]) for a in (w, g, m, v)]
    outs = _rowwise(fn, two_d, [], [(shape[-1], F32)] * 3, name=name)
    return [o.reshape(shape) for o in outs]


def _to_bf16(a, *, name):
    two_d = a.reshape(-1, a.shape[-1])
    return _rowwise(lambda t: t, [two_d], [], [(a.shape[-1], BF16)], name=name)[0].reshape(a.shape)


def _pack(arrays, rows_multiple):
    flat = jnp.concatenate([a.reshape(-1) for a in arrays])
    rows = -(-flat.shape[0] // LANES)
    rows = -(-rows // rows_multiple) * rows_multiple
    return jnp.pad(flat, (0, rows * LANES - flat.shape[0])).reshape(rows, LANES)


def _unpack(packed, like):
    flat = packed.reshape(-1)
    out, pos = [], 0
    for a in like:
        out.append(flat[pos:pos + a.size].reshape(a.shape))
        pos += a.size
    return out


_PARAMS = ("g_mix", "g_ffn", "lam_re", "lam_im", "log_dt", "ssm_b_re", "ssm_b_im", "ssm_c_re", "ssm_c_im", "ssm_d",
           "w_glu", "g_kv", "w_kvf", "b_f", "w_q", "w_o", "w_ffn_in", "ffn_conv_w", "ffn_conv_b", "w_ffn_out",
           "g_final")
_BIG = ("w_glu", "w_kvf", "w_q", "w_o", "w_ffn_in", "w_ffn_out")
_SMALL_SHARDED = ("ssm_d", "ffn_conv_w")


def kernel(x, g_mix, g_ffn, lam_re, lam_im, log_dt, ssm_b_re, ssm_b_im, ssm_c_re, ssm_c_im, ssm_d, w_glu, g_kv, w_kvf, b_f, w_q, w_o, w_ffn_in, ffn_conv_w, ffn_conv_b, w_ffn_out, g_final, loss_target, m_g_mix, m_g_ffn, m_lam_re, m_lam_im, m_log_dt, m_ssm_b_re, m_ssm_b_im, m_ssm_c_re, m_ssm_c_im, m_ssm_d, m_w_glu, m_g_kv, m_w_kvf, m_b_f, m_w_q, m_w_o, m_w_ffn_in, m_ffn_conv_w, m_ffn_conv_b, m_w_ffn_out, m_g_final, v_g_mix, v_g_ffn, v_lam_re, v_lam_im, v_log_dt, v_ssm_b_re, v_ssm_b_im, v_ssm_c_re, v_ssm_c_im, v_ssm_d, v_w_glu, v_g_kv, v_w_kvf, v_b_f, v_w_q, v_w_o, v_w_ffn_in, v_ffn_conv_w, v_ffn_conv_b, v_w_ffn_out, v_g_final):
    p = dict(g_mix=g_mix, g_ffn=g_ffn, lam_re=lam_re, lam_im=lam_im, log_dt=log_dt, ssm_b_re=ssm_b_re,
             ssm_b_im=ssm_b_im, ssm_c_re=ssm_c_re, ssm_c_im=ssm_c_im, ssm_d=ssm_d, w_glu=w_glu, g_kv=g_kv,
             w_kvf=w_kvf, b_f=b_f, w_q=w_q, w_o=w_o, w_ffn_in=w_ffn_in, ffn_conv_w=ffn_conv_w,
             ffn_conv_b=ffn_conv_b, w_ffn_out=w_ffn_out, g_final=g_final)
    mom1 = dict(zip(_PARAMS, (m_g_mix, m_g_ffn, m_lam_re, m_lam_im, m_log_dt, m_ssm_b_re, m_ssm_b_im, m_ssm_c_re,
                              m_ssm_c_im, m_ssm_d, m_w_glu, m_g_kv, m_w_kvf, m_b_f, m_w_q, m_w_o, m_w_ffn_in,
                              m_ffn_conv_w, m_ffn_conv_b, m_w_ffn_out, m_g_final)))
    mom2 = dict(zip(_PARAMS, (v_g_mix, v_g_ffn, v_lam_re, v_lam_im, v_log_dt, v_ssm_b_re, v_ssm_b_im, v_ssm_c_re,
                              v_ssm_c_im, v_ssm_d, v_w_glu, v_g_kv, v_w_kvf, v_b_f, v_w_q, v_w_o, v_w_ffn_in,
                              v_ffn_conv_w, v_ffn_conv_b, v_w_ffn_out, v_g_final)))
    d = x.shape[-1]
    nh = b_f.shape[0]
    slot = 2 * lax.axis_index("x") + lax.axis_index("y")

    wb = {k: _to_bf16(p[k], name=f"to_bf16_{k}") for k in _BIG}
    gd, gcw, gl = _all_gather([ssm_d, ffn_conv_w, wb["w_glu"]], name="first_all_gather")
    n_lay, n_s5 = w_ffn_in.shape[0], lam_re.shape[0]
    n_fox = n_lay - n_s5
    groups = {f"stage{l}": [("w_in", l, wb["w_ffn_in"][l]), ("w_out", l, wb["w_ffn_out"][l])] for l in range(n_lay)}
    groups[f"stage{n_s5 - 1}"] += [("w_kvf", 0, wb["w_kvf"])] + [(k, j, wb[k][j]) for k in ("w_q", "w_o")
                                                                for j in range(n_fox)]
    w = dict(p)
    w.update(w_glu=[(gl, l) for l in range(n_s5)], w_in=[None] * n_lay, w_out=[None] * n_lay, w_q=[None] * n_fox,
             w_o=[None] * n_fox, w_kv=[None], w_f=[None],
             conv_w=gcw.transpose(1, 2, 0, 3).reshape(n_lay, DT_CONV_TAPS, -1), conv_b=ffn_conv_b,
             ssm_d=gd.transpose(1, 0, 2).reshape(gd.shape[1], d))

    loss_part, grad_x, g = _local_step(x[0], loss_target[0], w, _Net(groups, d, nh))
    loss = lax.psum(loss_part, ("x", "y", "c"))

    small_names = [k for k in _PARAMS if k not in _BIG]
    small_full = dict(g_mix=g["g_mix"], g_ffn=g["g_ffn"], lam_re=g["lam_re"], lam_im=g["lam_im"], log_dt=g["log_dt"],
                      ssm_b_re=g["ssm_b_re"], ssm_b_im=g["ssm_b_im"], ssm_c_re=g["ssm_c_re"], ssm_c_im=g["ssm_c_im"],
                      ssm_d=g["ssm_d"], g_kv=g["g_kv"], b_f=g["b_f"], ffn_conv_w=g["conv_w"],
                      ffn_conv_b=g["conv_b"], g_final=g["g_final"])
    small_list = [small_full[k] for k in small_names]
    pack = _pack(small_list, N_CHIPS * N_CORES * 2 * SUBLANES)
    pack4 = pack.reshape(N_CHIPS, pack.shape[0] // N_CHIPS, LANES)
    pack_shard = _reduce_scatter([pack4], [F32], "small")[0]
    red_big = {k: g["big"][k, 0] if p[k].ndim == 2 else jnp.stack([g["big"][k, l] for l in range(p[k].shape[0])])
               for k in _BIG}
    pack_all = _all_gather([pack_shard], name="small_grads_all_gather")[0]
    red_small = dict(zip(small_names, _unpack(pack_all, small_list)))
    for k in _SMALL_SHARDED:
        width = p[k].shape[-1]
        red_small[k] = lax.dynamic_slice_in_dim(red_small[k], slot * width, width, axis=red_small[k].ndim - 1)

    grads, deltas, new_m, new_v = {}, {}, {}, {}
    for k in _BIG:
        grads[k] = red_big[k]
        deltas[k], new_m[k], new_v[k] = _adamw(p[k], grads[k], mom1[k], mom2[k], name=f"adamw_{k}")
    packs = [_pack([src[k] for k in small_names], SUBLANES) for src in (p, red_small, mom1, mom2)]
    like = [p[k] for k in small_names]
    outs = [_unpack(o, like) for o in _adamw(*packs, name="adamw_small")]
    for i, k in enumerate(small_names):
        grads[k] = red_small[k]
        deltas[k], new_m[k], new_v[k] = outs[0][i], outs[1][i], outs[2][i]
    return (loss, grad_x[None], *[grads[k] for k in _PARAMS], *[deltas[k] for k in _PARAMS],
            *[new_m[k] for k in _PARAMS], *[new_v[k] for k in _PARAMS])
```

```python
import functools

import jax
import jax.numpy as jnp
from jax import lax
from jax.experimental import pallas as pl
from jax.experimental.pallas import tpu as pltpu

F32 = jnp.float32
BF16 = jnp.bfloat16

RMS_EPS = 1e-6
ADAM_LR = 0.001
ADAM_B1 = 0.9
ADAM_B2 = 0.999
ADAM_EPS = 1e-08
ADAM_WD = 0.01
ADAM_STEP = 10
DT_CONV_TAPS = 3

LANES = 128
SUBLANES = 8
HEAD_DIM = 64
FLASH_ROW_TILE = 32
S5_BLOCK_GROUPS = 16
VMEM_LIMIT_BYTES = 48 << 20
MM_BLOCK_BUDGET_BYTES = 30 << 20
N_CHIPS = 4
N_CORES = 2
MESH = pl.DeviceIdType.MESH


def _cp(n_grid):
    return pltpu.CompilerParams(dimension_semantics=("arbitrary",) * n_grid, vmem_limit_bytes=VMEM_LIMIT_BYTES)


def _tile(n, pref, mult=SUBLANES):
    if n <= pref:
        return n
    t = (pref // mult) * mult
    while t >= mult:
        if n % t == 0:
            return t
        t -= mult
    return n


class _Comm:
    def __init__(self, ins, out_shapes, sems, start, finish):
        self.ins, self.out_shapes, self.sems, self.start, self.finish = ins, out_shapes, sems, start, finish


def _any_specs(n):
    return [pl.BlockSpec(memory_space=pl.ANY)] * n


def _run_comm(comm, *, name):
    n_in, n_out = len(comm.ins), len(comm.out_shapes)

    def body(*refs):
        ins, outs, sems = refs[:n_in], refs[n_in:n_in + n_out], refs[n_in + n_out:]
        comm.start(ins, outs, sems)
        comm.finish(ins, outs, sems)

    return pl.pallas_call(body, in_specs=_any_specs(n_in), out_specs=_any_specs(n_out),
                          out_shape=list(comm.out_shapes), scratch_shapes=list(comm.sems), name=name)(*comm.ins)


def _call(body, *, grid, in_specs, out_specs, out_shape, args, name, scratch_shapes=(), prefetch=(), comm=None):
    n_pre, n_in, n_out, n_scr = len(prefetch), len(in_specs), len(out_specs), len(scratch_shapes)
    in_specs, out_specs, out_shape = list(in_specs), list(out_specs), list(out_shape)
    scratch_shapes, args = list(scratch_shapes), list(args)
    kernel_body = body
    if comm is not None:
        n_cin, n_cout = len(comm.ins), len(comm.out_shapes)

        def kernel_body(*refs):
            pos = n_pre + n_in
            c_in = refs[pos:pos + n_cin]
            main_out = refs[pos + n_cin:pos + n_cin + n_out]
            pos += n_cin + n_out
            c_out = refs[pos:pos + n_cout]
            main_scr = refs[pos + n_cout:pos + n_cout + n_scr]
            sems = refs[pos + n_cout + n_scr:]
            ids = [pl.program_id(a) for a in range(len(grid))]
            first = functools.reduce(jnp.logical_and, [i == 0 for i in ids])
            last = functools.reduce(jnp.logical_and, [i == g - 1 for i, g in zip(ids, grid)])
            pl.when(first)(lambda: comm.start(c_in, c_out, sems))
            body(*refs[:n_pre + n_in], *main_out, *main_scr)
            pl.when(last)(lambda: comm.finish(c_in, c_out, sems))

        in_specs += _any_specs(n_cin)
        out_specs += _any_specs(n_cout)
        out_shape += list(comm.out_shapes)
        scratch_shapes += list(comm.sems)
        args += list(comm.ins)
    if prefetch:
        spec = pltpu.PrefetchScalarGridSpec(num_scalar_prefetch=n_pre, grid=grid, in_specs=in_specs,
                                            out_specs=out_specs, scratch_shapes=scratch_shapes)
        res = pl.pallas_call(kernel_body, grid_spec=spec, out_shape=out_shape, compiler_params=_cp(len(grid)),
                             name=name)(*prefetch, *args)
    else:
        res = pl.pallas_call(kernel_body, grid=grid, in_specs=in_specs, out_specs=out_specs, out_shape=out_shape,
                             scratch_shapes=scratch_shapes, compiler_params=_cp(len(grid)), name=name)(*args)
    return (res[:n_out], res[n_out:]) if comm is not None else res


def _row_tile(m, bytes_per_row, fixed_bytes):
    for tm in (1024, 512):
        if m % tm == 0 and 2 * (tm * bytes_per_row + fixed_bytes) <= MM_BLOCK_BUDGET_BYTES:
            return tm
    return _tile(m, 512)


def _dot(a, b, ca, cb):
    return lax.dot_general(a, b, (((ca,), (cb,)), ((), ())), preferred_element_type=F32)


def _mm_cols(x, w4, layer, *, wc, out_dtype=F32, scale=None, name):
    m, k = x.shape
    slots, _, k0, k1 = w4.shape
    nb = k1 if wc == 0 else k0
    assert (k0 if wc == 0 else k1) == k
    tm = _row_tile(m, k * x.dtype.itemsize + nb * jnp.dtype(out_dtype).itemsize, k0 * k1 * w4.dtype.itemsize)

    def body(x_ref, w_ref, o_ref):
        acc = _dot(x_ref[...].astype(BF16), w_ref[0, 0], 1, wc)
        if scale is not None:
            acc = acc * scale
        o_ref[...] = acc.astype(out_dtype)

    return pl.pallas_call(
        body, grid=(slots, m // tm),
        in_specs=[pl.BlockSpec((tm, k), lambda s, i: (i, 0)),
                  pl.BlockSpec((1, 1, k0, k1), lambda s, i: (s, layer, 0, 0))],
        out_specs=pl.BlockSpec((tm, nb), lambda s, i: (i, s)),
        out_shape=jax.ShapeDtypeStruct((m, slots * nb), out_dtype),
        compiler_params=_cp(2), name=name)(x, w4)


def _planes(a):
    return a if a.ndim == 3 else a[None]


def _mm_acc(x, w4, layer, *, wc, name):
    x = _planes(x)
    n_planes, m, width = x.shape
    slots, _, k0, k1 = w4.shape
    kb = k0 if wc == 0 else k1
    nout = k1 if wc == 0 else k0
    assert n_planes * width == slots * kb
    spp = slots // n_planes
    tm = _row_tile(m, kb * x.dtype.itemsize + nout * 4, k0 * k1 * w4.dtype.itemsize)

    def body(x_ref, w_ref, o_ref):
        @pl.when(pl.program_id(1) == 0)
        def _():
            o_ref[...] = jnp.zeros_like(o_ref)
        o_ref[...] += _dot(x_ref[0].astype(BF16), w_ref[0, 0], 1, wc)

    return pl.pallas_call(
        body, grid=(m // tm, slots),
        in_specs=[pl.BlockSpec((1, tm, kb), lambda i, s: (s // spp, i, s % spp)),
                  pl.BlockSpec((1, 1, k0, k1), lambda i, s: (s, layer, 0, 0))],
        out_specs=pl.BlockSpec((tm, nout), lambda i, s: (i, 0)),
        out_shape=jax.ShapeDtypeStruct((m, nout), F32),
        compiler_params=_cp(2), name=name)(x, w4)


def _mm_tn(x, dy, slots, *, scale=None, name):
    m, k = x.shape
    dy = _planes(dy)
    n_planes, _, width = dy.shape
    n = n_planes * width // slots
    spp = slots // n_planes
    ta = _tile(k, 512, LANES)
    tm = m
    while tm > 512 and tm % 2 == 0 and (2 * tm * (ta * x.dtype.itemsize + n * dy.dtype.itemsize)
                                         + 2 * ta * n * 4) > MM_BLOCK_BUDGET_BYTES:
        tm //= 2
    n_m = m // tm

    def body(x_ref, dy_ref, o_ref):
        @pl.when(pl.program_id(2) == 0)
        def _():
            o_ref[...] = jnp.zeros_like(o_ref)
        o_ref[0] += _dot(x_ref[...].astype(BF16), dy_ref[0].astype(BF16), 0, 0)
        if scale is not None:
            @pl.when(pl.program_id(2) == n_m - 1)
            def _():
                o_ref[...] = o_ref[...] * scale

    return pl.pallas_call(
        body, grid=(slots, k // ta, n_m),
        in_specs=[pl.BlockSpec((tm, ta), lambda s, a, i: (i, a)),
                  pl.BlockSpec((1, tm, n), lambda s, a, i: (s // spp, i, s % spp))],
        out_specs=pl.BlockSpec((1, ta, n), lambda s, a, i: (s, a, 0)),
        out_shape=jax.ShapeDtypeStruct((slots, k, n), F32),
        compiler_params=_cp(3), name=name)(x, dy)


def _rowwise(fn, rows, consts, outs, accs=(), *, tl=256, name):
    n_rows = rows[0].shape[0]
    tl = _tile(n_rows, tl)
    n_in = len(rows) + len(consts)
    n_out = len(outs)

    def body(*refs):
        res = fn(*[r[...] for r in refs[:n_in]])
        res = res if isinstance(res, (tuple, list)) else (res,)
        o_refs = refs[n_in:n_in + n_out]
        a_refs = refs[n_in + n_out:]
        for o, val in zip(o_refs, res[:n_out]):
            o[...] = val.astype(o.dtype)
        if a_refs:
            @pl.when(pl.program_id(0) == 0)
            def _():
                for a in a_refs:
                    a[...] = jnp.zeros_like(a)
            for a, val in zip(a_refs, res[n_out:]):
                a[...] += val

    in_specs = ([pl.BlockSpec((tl, r.shape[1]), lambda i: (i, 0)) for r in rows]
                + [pl.BlockSpec(c.shape, lambda i: (0, 0)) for c in consts])
    out_specs = ([pl.BlockSpec((tl, w), lambda i: (i, 0)) for w, _ in outs]
                 + [pl.BlockSpec(s, lambda i: (0, 0)) for s in accs])
    out_shape = ([jax.ShapeDtypeStruct((n_rows, w), dt) for w, dt in outs]
                 + [jax.ShapeDtypeStruct(s, F32) for s in accs])
    return pl.pallas_call(body, grid=(n_rows // tl,), in_specs=in_specs, out_specs=out_specs,
                          out_shape=out_shape, compiler_params=_cp(1), name=name)(*rows, *consts)


def _rms(x, g):
    return x * lax.rsqrt(jnp.mean(x * x, axis=-1, keepdims=True) + RMS_EPS) * g


def _sigmoid(x):
    return 1.0 / (1.0 + jnp.exp(-x))


def _glu(zz):
    d = zz.shape[1] // 2
    return zz[:, :d] * _sigmoid(zz[:, d:])


def _gelu(y):
    return jax.nn.gelu(y)


def _row2(v):
    return v.reshape(1, -1)


def _node_bwd(d_in, h, branches, *, name):
    width = h.shape[1]
    flat = [dy for _, dys in branches for dy in dys]
    counts = [len(dys) for _, dys in branches]
    gains = [_row2(g) for g, _ in branches]

    def fn(d, hh, *rest):
        dys, gs = rest[:len(flat)], rest[len(flat):]
        tot, dgs, pos = d, [], 0
        for g, cnt in zip(gs, counts):
            dy = dys[pos].astype(F32)
            for extra in dys[pos + 1:pos + cnt]:
                dy = dy + extra.astype(F32)
            pos += cnt
            _, vjp = jax.vjp(_rms, hh, g)
            dx, dg = vjp(dy)
            tot = tot + dx
            dgs.append(dg)
        return (tot, tot, *dgs)

    res = _rowwise(fn, [d_in, h, *flat], gains, [(width, F32), (width, BF16)], [(1, width)] * len(branches),
                   name=name)
    return res[0], res[1], [r[0] for r in res[2:]]


def _s5_prep_fn(lr, li, ldt, br, bi, cr, ci, *, gq, h, p):
    dt = jnp.exp(ldt)
    mag = jnp.exp(lr * dt)
    lb_re = mag * jnp.cos(li * dt)
    lb_im = mag * jnp.sin(li * dt)
    den = lr * lr + li * li
    nr = lb_re - 1.0
    fr = (nr * lr + lb_im * li) / den
    fi = (lb_im * lr - nr * li) / den
    bb_re = fr * br - fi * bi
    bb_im = fr * bi + fi * br
    shape = (gq * h, gq * p)
    r = lax.broadcasted_iota(jnp.int32, shape, 0)
    c = lax.broadcasted_iota(jnp.int32, shape, 1)
    mask = jnp.where(jnp.right_shift(r, h.bit_length() - 1) == jnp.right_shift(c, p.bit_length() - 1), 1.0, 0.0)

    def expand(t):
        return jnp.concatenate([t] * gq, axis=0) * mask

    return lb_re, lb_im, expand(bb_re), expand(bb_im), expand(cr), expand(ci)


def _s5_prep(lr, li, ldt, br, bi, cr, ci, p, *, name):
    n = lr.shape[1]
    h = br.shape[0]
    gq = S5_BLOCK_GROUPS
    nq, cq = gq * p, gq * h
    nblk = n // nq
    fn = functools.partial(_s5_prep_fn, gq=gq, h=h, p=p)

    def body(lr_r, li_r, ldt_r, br_r, bi_r, cr_r, ci_r, lbr_o, lbi_o, wbr_o, wbi_o, wcr_o, wci_o):
        lb_re, lb_im, wbr, wbi, wcr, wci = fn(lr_r[...], li_r[...], ldt_r[...], br_r[...], bi_r[...],
                                              cr_r[...], ci_r[...])
        lbr_o[...] = lb_re
        lbi_o[...] = lb_im
        wbr_o[0] = wbr.astype(BF16)
        wbi_o[0] = wbi.astype(BF16)
        wcr_o[0] = wcr.astype(BF16)
        wci_o[0] = wci.astype(BF16)

    vec = pl.BlockSpec((1, nq), lambda q: (0, q))
    tab = pl.BlockSpec((h, nq), lambda q: (0, q))
    wsp = pl.BlockSpec((1, cq, nq), lambda q: (q, 0, 0))
    wsh = jax.ShapeDtypeStruct((nblk, cq, nq), BF16)
    vsh = jax.ShapeDtypeStruct((1, n), F32)
    return pl.pallas_call(body, grid=(nblk,), in_specs=[vec, vec, vec, tab, tab, tab, tab],
                          out_specs=[vec, vec, wsp, wsp, wsp, wsp], out_shape=[vsh, vsh, wsh, wsh, wsh, wsh],
                          compiler_params=_cp(1), name=name)(lr, li, ldt, br, bi, cr, ci)


def _s5_prep_bwd(lr, li, ldt, br, bi, cr, ci, p, dlbr, dlbi, dwbr, dwbi, dwcr, dwci, *, name):
    n = lr.shape[1]
    h = br.shape[0]
    gq = S5_BLOCK_GROUPS
    nq, cq = gq * p, gq * h
    nblk = n // nq
    fn = functools.partial(_s5_prep_fn, gq=gq, h=h, p=p)

    def body(lr_r, li_r, ldt_r, br_r, bi_r, cr_r, ci_r, dlbr_r, dlbi_r, dwbr_r, dwbi_r, dwcr_r, dwci_r,
             *outs):
        _, vjp = jax.vjp(fn, lr_r[...], li_r[...], ldt_r[...], br_r[...], bi_r[...], cr_r[...], ci_r[...])
        grads = vjp((dlbr_r[0], dlbi_r[0], dwbr_r[0], dwbi_r[0], dwcr_r[0], dwci_r[0]))
        for o, g in zip(outs, grads):
            o[...] = g

    vec = pl.BlockSpec((1, nq), lambda q: (0, q))
    tab = pl.BlockSpec((h, nq), lambda q: (0, q))
    vec3 = pl.BlockSpec((1, 1, nq), lambda q: (q, 0, 0))
    wsp = pl.BlockSpec((1, cq, nq), lambda q: (q, 0, 0))
    vsh = jax.ShapeDtypeStruct((1, n), F32)
    tsh = jax.ShapeDtypeStruct((h, n), F32)
    return pl.pallas_call(body, grid=(nblk,),
                          in_specs=[vec, vec, vec, tab, tab, tab, tab, vec3, vec3, wsp, wsp, wsp, wsp],
                          out_specs=[vec, vec, vec, tab, tab, tab, tab],
                          out_shape=[vsh, vsh, vsh, tsh, tsh, tsh, tsh],
                          compiler_params=_cp(1), name=name)(lr, li, ldt, br, bi, cr, ci,
                                                             dlbr, dlbi, dwbr, dwbi, dwcr, dwci)


def _scan_rows(s_re, s_im, a_re, a_im, c_re, c_im, *, reverse):
    t_rows, n = s_re.shape
    nb = t_rows // SUBLANES
    row = lax.broadcasted_iota(jnp.int32, (SUBLANES, n), 0)

    def cmul(x, y):
        return x[0] * y[0] - x[1] * y[1], x[0] * y[1] + x[1] * y[0]

    a1 = (jnp.broadcast_to(a_re, (SUBLANES, n)), jnp.broadcast_to(a_im, (SUBLANES, n)))
    a2 = cmul(a1, a1)
    a4 = cmul(a2, a2)
    pk = (a_re, a_im)
    tab_re = jnp.zeros((SUBLANES, n), F32)
    tab_im = jnp.zeros((SUBLANES, n), F32)
    for i in range(SUBLANES):
        at = (SUBLANES - 1 - i) if reverse else i
        tab_re = jnp.where(row == at, pk[0], tab_re)
        tab_im = jnp.where(row == at, pk[1], tab_im)
        pk = cmul(pk, (a_re, a_im))

    def step(b, carry):
        cr, ci = carry
        blk = (nb - 1 - b) if reverse else b
        off = pl.multiple_of(blk * SUBLANES, SUBLANES)
        x_re = s_re[pl.ds(off, SUBLANES), :]
        x_im = s_im[pl.ds(off, SUBLANES), :]
        for d, (pr, pi) in ((1, a1), (2, a2), (4, a4)):
            if reverse:
                keep = row < SUBLANES - d
                sh = SUBLANES - d
            else:
                keep = row >= d
                sh = d
            sh_re = jnp.where(keep, pltpu.roll(x_re, sh, 0), 0.0)
            sh_im = jnp.where(keep, pltpu.roll(x_im, sh, 0), 0.0)
            x_re, x_im = x_re + pr * sh_re - pi * sh_im, x_im + pr * sh_im + pi * sh_re
        x_re, x_im = x_re + tab_re * cr - tab_im * ci, x_im + tab_re * ci + tab_im * cr
        s_re[pl.ds(off, SUBLANES), :] = x_re
        s_im[pl.ds(off, SUBLANES), :] = x_im
        edge = 0 if reverse else SUBLANES - 1
        return x_re[edge:edge + 1, :], x_im[edge:edge + 1, :]

    return lax.fori_loop(0, nb, step, (c_re, c_im))


def _s5_fwd(u, prep, dskip, *, name, comm=None):
    lb_re, lb_im, wbr, wbi, wcr, wci = prep
    n_rows, _ = u.shape
    nblk, cq, nq = wbr.shape
    tt = _tile(n_rows, 512)
    nch = n_rows // tt

    def body(u_ref, wbr_r, wbi_r, wcr_r, wci_r, lbr_r, lbi_r, d_ref, y_ref, s_re, s_im, sbr_o, sbi_o, c_re, c_im):
        @pl.when(pl.program_id(1) == 0)
        def _():
            c_re[...] = jnp.zeros_like(c_re)
            c_im[...] = jnp.zeros_like(c_im)
        uf = u_ref[...]
        ub = uf.astype(BF16)
        s_re[...] = _dot(ub, wbr_r[0], 1, 0)
        s_im[...] = _dot(ub, wbi_r[0], 1, 0)
        sbr_o[0] = c_re[...]
        sbi_o[0] = c_im[...]
        cr, ci = _scan_rows(s_re, s_im, lbr_r[...], lbi_r[...], c_re[...], c_im[...], reverse=False)
        c_re[...] = cr
        c_im[...] = ci
        y = _dot(s_re[...].astype(BF16), wcr_r[0], 1, 1) - _dot(s_im[...].astype(BF16), wci_r[0], 1, 1)
        y_ref[...] = y + d_ref[...] * uf

    wsp = pl.BlockSpec((1, cq, nq), lambda q, i: (q, 0, 0))
    vec = pl.BlockSpec((1, nq), lambda q, i: (0, q))
    act = pl.BlockSpec((tt, cq), lambda q, i: (i, q))
    sb = pl.BlockSpec((1, 1, nq), lambda q, i: (i, 0, q))
    sbsh = jax.ShapeDtypeStruct((nch, 1, nblk * nq), F32)
    states = pl.BlockSpec((tt, nq), lambda q, i: (i, q))
    stsh = jax.ShapeDtypeStruct((n_rows, nblk * nq), F32)
    return _call(
        body, grid=(nblk, nch),
        in_specs=[act, wsp, wsp, wsp, wsp, vec, vec, pl.BlockSpec((1, cq), lambda q, i: (0, q))],
        out_specs=[act, states, states, sb, sb],
        out_shape=[jax.ShapeDtypeStruct(u.shape, F32), stsh, stsh, sbsh, sbsh],
        scratch_shapes=[pltpu.VMEM((1, nq), F32), pltpu.VMEM((1, nq), F32)],
        args=(u, wbr, wbi, wcr, wci, lb_re, lb_im, dskip), name=name, comm=comm)


def _s5_bwd(u, dy, st_re, st_im, sb_re, sb_im, prep, dskip, *, name, comm=None):
    lb_re, lb_im, wbr, wbi, wcr, wci = prep
    n_rows, _ = u.shape
    nblk, cq, nq = wbr.shape
    tt = _tile(n_rows, 512)
    nch = n_rows // tt

    def body(u_ref, dy_ref, s_re, s_im, sbr_r, sbi_r, wbr_r, wbi_r, wcr_r, wci_r, lbr_r, lbi_r, d_ref,
             du_ref, dwbr, dwbi, dwcr, dwci, dlbr, dlbi, dd_ref, g_re, g_im, lc_re, lc_im):
        @pl.when(pl.program_id(1) == 0)
        def _():
            for ref in (lc_re, lc_im, dwbr, dwbi, dwcr, dwci, dlbr, dlbi, dd_ref):
                ref[...] = jnp.zeros_like(ref)
        uf = u_ref[...]
        ub = uf.astype(BF16)
        dyf = dy_ref[...]
        dyb = dyf.astype(BF16)
        sr16 = s_re[...].astype(BF16)
        si16 = s_im[...].astype(BF16)
        dwcr[0] += _dot(dyb, sr16, 0, 0)
        dwci[0] -= _dot(dyb, si16, 0, 0)
        g_re[...] = _dot(dyb, wcr_r[0], 1, 0)
        g_im[...] = -_dot(dyb, wci_r[0], 1, 0)
        lcr, lci = _scan_rows(g_re, g_im, lbr_r[...], -lbi_r[...], lc_re[...], lc_im[...], reverse=True)
        lc_re[...] = lcr
        lc_im[...] = lci
        lam_r = g_re[...]
        lam_i = g_im[...]
        first = lax.broadcasted_iota(jnp.int32, (tt, nq), 0) == 0
        prev_r = jnp.where(first, sbr_r[0], pltpu.roll(s_re[...], 1, 0))
        prev_i = jnp.where(first, sbi_r[0], pltpu.roll(s_im[...], 1, 0))
        dlbr[0] += jnp.sum(lam_r * prev_r + lam_i * prev_i, axis=0, keepdims=True)
        dlbi[0] += jnp.sum(lam_i * prev_r - lam_r * prev_i, axis=0, keepdims=True)
        lr16 = lam_r.astype(BF16)
        li16 = lam_i.astype(BF16)
        du_ref[...] = _dot(lr16, wbr_r[0], 1, 1) + _dot(li16, wbi_r[0], 1, 1) + d_ref[...] * dyf
        dwbr[0] += _dot(ub, lr16, 0, 0)
        dwbi[0] += _dot(ub, li16, 0, 0)
        dd_ref[0] += jnp.sum(dyf * uf, axis=0, keepdims=True)

    last = nch - 1
    wsp = pl.BlockSpec((1, cq, nq), lambda q, i: (q, 0, 0))
    vec = pl.BlockSpec((1, nq), lambda q, i: (0, q))
    act = pl.BlockSpec((tt, cq), lambda q, i: (last - i, q))
    sb = pl.BlockSpec((1, 1, nq), lambda q, i: (last - i, 0, q))
    vec3 = pl.BlockSpec((1, 1, nq), lambda q, i: (q, 0, 0))
    dsp = pl.BlockSpec((1, 1, cq), lambda q, i: (q, 0, 0))
    wsh = jax.ShapeDtypeStruct((nblk, cq, nq), F32)
    v3sh = jax.ShapeDtypeStruct((nblk, 1, nq), F32)
    big = pltpu.VMEM((tt, nq), F32)
    states = pl.BlockSpec((tt, nq), lambda q, i: (last - i, q))
    return _call(
        body, grid=(nblk, nch),
        in_specs=[act, act, states, states, sb, sb, wsp, wsp, wsp, wsp, vec, vec,
                  pl.BlockSpec((1, cq), lambda q, i: (0, q))],
        out_specs=[act, wsp, wsp, wsp, wsp, vec3, vec3, dsp],
        out_shape=[jax.ShapeDtypeStruct(u.shape, F32), wsh, wsh, wsh, wsh, v3sh, v3sh,
                   jax.ShapeDtypeStruct((nblk, 1, cq), F32)],
        scratch_shapes=[big, big, pltpu.VMEM((1, nq), F32), pltpu.VMEM((1, nq), F32)],
        args=(u, dy, st_re, st_im, sb_re, sb_im, wbr, wbi, wcr, wci, lb_re, lb_im, dskip), name=name, comm=comm)


def _conv_taps(cur, prev, w, b):
    rid = lax.broadcasted_iota(jnp.int32, cur.shape, 0)
    x1 = jnp.where(rid == 0, prev[7:8, :], pltpu.roll(cur, 1, 0))
    x2 = jnp.where(rid == 0, prev[6:7, :], jnp.where(rid == 1, prev[7:8, :], pltpu.roll(cur, 2, 0)))
    return b + x2 * w[0:1, :] + x1 * w[1:2, :] + cur * w[2:3, :], x1, x2


def _conv_fwd(uu, cw, cb, *, name):
    n_rows, f2 = uu.shape
    f = f2 // 2
    tc = _tile(f, 1408, LANES)
    tl = _tile(n_rows, 256)
    nfb = f // tc

    def body(g_ref, u_ref, wg_ref, wu_ref, bg_ref, bu_ref, o_ref, pg, pu):
        @pl.when(pl.program_id(1) == 0)
        def _():
            pg[...] = jnp.zeros_like(pg)
            pu[...] = jnp.zeros_like(pu)
        gcur = g_ref[...]
        ucur = u_ref[...]
        cg, _, _ = _conv_taps(gcur, pg[...], wg_ref[...], bg_ref[...])
        cu, _, _ = _conv_taps(ucur, pu[...], wu_ref[...], bu_ref[...])
        o_ref[...] = (cg * _sigmoid(cg) * cu).astype(o_ref.dtype)
        pg[...] = gcur[tl - SUBLANES:, :]
        pu[...] = ucur[tl - SUBLANES:, :]

    return pl.pallas_call(
        body, grid=(nfb, n_rows // tl),
        in_specs=[pl.BlockSpec((tl, tc), lambda j, i: (i, j)), pl.BlockSpec((tl, tc), lambda j, i: (i, j + nfb)),
                  pl.BlockSpec((DT_CONV_TAPS, tc), lambda j, i: (0, j)),
                  pl.BlockSpec((DT_CONV_TAPS, tc), lambda j, i: (0, j + nfb)),
                  pl.BlockSpec((1, tc), lambda j, i: (0, j)), pl.BlockSpec((1, tc), lambda j, i: (0, j + nfb))],
        out_specs=pl.BlockSpec((tl, tc), lambda j, i: (i, j)),
        out_shape=jax.ShapeDtypeStruct((n_rows, f), BF16),
        scratch_shapes=[pltpu.VMEM((SUBLANES, tc), F32), pltpu.VMEM((SUBLANES, tc), F32)],
        compiler_params=_cp(2), name=name)(uu, uu, cw, cw, cb, cb)


def _conv_bwd(uu, dact, cw, cb, *, name):
    n_rows, f2 = uu.shape
    f = f2 // 2
    tc = _tile(f, 1408, LANES)
    tl = _tile(n_rows, 256)
    nfb = f // tc
    nrb = n_rows // tl
    halo_per_tile = tl // SUBLANES

    def body(g_ref, gh_ref, u_ref, uh_ref, da_ref, wg_ref, wu_ref, bg_ref, bu_ref,
             duu_ref, dw_ref, db_ref, nxt_g, nxt_u):
        i = pl.program_id(1)
        rb = nrb - 1 - i

        @pl.when(i == 0)
        def _():
            for ref in (nxt_g, nxt_u, dw_ref, db_ref):
                ref[...] = jnp.zeros_like(ref)
        has_prev = jnp.where(rb > 0, 1.0, 0.0)
        gcur, ucur = g_ref[...], u_ref[...]
        wg, wu = wg_ref[...], wu_ref[...]
        cg, g1, g2 = _conv_taps(gcur, gh_ref[...] * has_prev, wg, bg_ref[...])
        cu, u1, u2 = _conv_taps(ucur, uh_ref[...] * has_prev, wu, bu_ref[...])
        sg = _sigmoid(cg)
        silu = cg * sg
        da = da_ref[...]
        rid = lax.broadcasted_iota(jnp.int32, da.shape, 0)

        def transpose_conv(plane, d, cur, x1, x2, w, nxt):
            nx = nxt[...]
            d1 = jnp.where(rid == tl - 1, nx[0:1, :], pltpu.roll(d, tl - 1, 0))
            d2 = jnp.where(rid == tl - 2, nx[0:1, :],
                           jnp.where(rid == tl - 1, nx[1:2, :], pltpu.roll(d, tl - 2, 0)))
            duu_ref[plane] = (w[2:3, :] * d + w[1:2, :] * d1 + w[0:1, :] * d2).astype(duu_ref.dtype)
            nxt[...] = d[0:SUBLANES, :]
            dw_ref[plane] += jnp.concatenate([jnp.sum(d * x2, axis=0, keepdims=True),
                                              jnp.sum(d * x1, axis=0, keepdims=True),
                                              jnp.sum(d * cur, axis=0, keepdims=True)], axis=0)
            db_ref[plane] += jnp.sum(d, axis=0, keepdims=True)

        transpose_conv(0, da * cu * (sg * (1.0 + cg * (1.0 - sg))), gcur, g1, g2, wg, nxt_g)
        transpose_conv(1, da * silu, ucur, u1, u2, wu, nxt_u)

    def halo(j, i):
        return jnp.maximum((nrb - 1 - i) * halo_per_tile - 1, 0)

    return pl.pallas_call(
        body, grid=(nfb, nrb),
        in_specs=[pl.BlockSpec((tl, tc), lambda j, i: (nrb - 1 - i, j)),
                  pl.BlockSpec((SUBLANES, tc), lambda j, i: (halo(j, i), j)),
                  pl.BlockSpec((tl, tc), lambda j, i: (nrb - 1 - i, j + nfb)),
                  pl.BlockSpec((SUBLANES, tc), lambda j, i: (halo(j, i), j + nfb)),
                  pl.BlockSpec((tl, tc), lambda j, i: (nrb - 1 - i, j)),
                  pl.BlockSpec((DT_CONV_TAPS, tc), lambda j, i: (0, j)),
                  pl.BlockSpec((DT_CONV_TAPS, tc), lambda j, i: (0, j + nfb)),
                  pl.BlockSpec((1, tc), lambda j, i: (0, j)),
                  pl.BlockSpec((1, tc), lambda j, i: (0, j + nfb))],
        out_specs=[pl.BlockSpec((2, tl, tc), lambda j, i: (0, nrb - 1 - i, j)),
                   pl.BlockSpec((2, DT_CONV_TAPS, tc), lambda j, i: (0, 0, j)),
                   pl.BlockSpec((2, 1, tc), lambda j, i: (0, 0, j))],
        out_shape=[jax.ShapeDtypeStruct((2, n_rows, f), BF16), jax.ShapeDtypeStruct((2, DT_CONV_TAPS, f), F32),
                   jax.ShapeDtypeStruct((2, 1, f), F32)],
        scratch_shapes=[pltpu.VMEM((SUBLANES, tc), F32), pltpu.VMEM((SUBLANES, tc), F32)],
        compiler_params=_cp(2), name=name)(uu, uu, uu, uu, dact, cw, cw, cb, cb)


def _log_sigmoid(x):
    t = jnp.exp(-jnp.abs(x))
    log1p_t = jnp.where(t < 1e-3, t * (1.0 - t * (0.5 - t * (1.0 / 3.0))), jnp.log(1.0 + t))
    return jnp.minimum(x, 0.0) - log1p_t


def _dlog_sigmoid(x):
    t = jnp.exp(-jnp.abs(x))
    return jnp.where(x >= 0, t, 1.0) / (1.0 + t)


def _tri_dot(tri, x):
    return jnp.dot(tri, x, precision=lax.Precision.HIGHEST, preferred_element_type=F32)


def _cum_fwd(fl, bf, *, name):
    n_rows, width = fl.shape
    tc = _tile(n_rows, 256)

    def body(fl_ref, bf_ref, o_ref, carry):
        @pl.when(pl.program_id(0) == 0)
        def _():
            carry[...] = jnp.zeros_like(carry)
        x = _log_sigmoid(fl_ref[...] + bf_ref[...])
        r = lax.broadcasted_iota(jnp.int32, (tc, tc), 0)
        c = lax.broadcasted_iota(jnp.int32, (tc, tc), 1)
        y = _tri_dot(jnp.where(r >= c, 1.0, 0.0), x) + carry[...]
        o_ref[...] = y
        carry[...] = y[tc - 1:tc, :]

    return pl.pallas_call(
        body, grid=(n_rows // tc,),
        in_specs=[pl.BlockSpec((tc, width), lambda i: (i, 0)), pl.BlockSpec((1, width), lambda i: (0, 0))],
        out_specs=pl.BlockSpec((tc, width), lambda i: (i, 0)),
        out_shape=jax.ShapeDtypeStruct(fl.shape, F32),
        scratch_shapes=[pltpu.VMEM((1, width), F32)], compiler_params=_cp(1), name=name)(fl, bf)


def _cum_bwd(dcum, fl, bf, *, name):
    n_rows, width = fl.shape
    tc = _tile(n_rows, 256)
    last = n_rows // tc - 1

    def body(dc_ref, fl_ref, bf_ref, dfl_ref, dbf_ref, carry):
        @pl.when(pl.program_id(0) == 0)
        def _():
            carry[...] = jnp.zeros_like(carry)
            dbf_ref[...] = jnp.zeros_like(dbf_ref)
        r = lax.broadcasted_iota(jnp.int32, (tc, tc), 0)
        c = lax.broadcasted_iota(jnp.int32, (tc, tc), 1)
        dls = _tri_dot(jnp.where(r <= c, 1.0, 0.0), dc_ref[...]) + carry[...]
        carry[...] = dls[0:1, :]
        dfl = dls * _dlog_sigmoid(fl_ref[...] + bf_ref[...])
        dfl_ref[...] = dfl.astype(dfl_ref.dtype)
        dbf_ref[...] += jnp.sum(dfl, axis=0, keepdims=True)

    return pl.pallas_call(
        body, grid=(n_rows // tc,),
        in_specs=[pl.BlockSpec((tc, width), lambda i: (last - i, 0)),
                  pl.BlockSpec((tc, width), lambda i: (last - i, 0)),
                  pl.BlockSpec((1, width), lambda i: (0, 0))],
        out_specs=[pl.BlockSpec((tc, width), lambda i: (last - i, 0)), pl.BlockSpec((1, width), lambda i: (0, 0))],
        out_shape=[jax.ShapeDtypeStruct(fl.shape, BF16), jax.ShapeDtypeStruct((1, width), F32)],
        scratch_shapes=[pltpu.VMEM((1, width), F32)], compiler_params=_cp(1), name=name)(dcum, fl, bf)


def _head_masks():
    lane = lax.broadcasted_iota(jnp.int32, (1, LANES), 1)
    return (lane < HEAD_DIM, lane >= HEAD_DIM)


def _flash_fwd(q, kv, cum_c, cum_r, *, tq, name, comm=None):
    n_rows, d = q.shape
    nhp = d // LANES
    tk = tq
    nq = n_rows // tq
    rt = _tile(tk, FLASH_ROW_TILE)
    reps = (1, tq // LANES)

    def body(qi_ref, kj_ref, q_ref, k_ref, v_ref, cq_ref, ck_ref, ot_ref, lse_ref, m0, m1, l0, l1, acc,
             s0, s1, p0, p1, b0, b1):
        i = qi_ref[pl.program_id(1)]
        j = kj_ref[pl.program_id(1)]
        ms, ls = (m0, m1), (l0, l1)
        head_rows = lax.broadcasted_iota(jnp.int32, (LANES, 1), 0) < HEAD_DIM

        @pl.when(j == 0)
        def _():
            for h in range(2):
                ms[h][...] = jnp.full_like(ms[h], -jnp.inf)
                ls[h][...] = jnp.zeros_like(ls[h])
            acc[...] = jnp.zeros_like(acc)

        def block(diagonal):
            qv, kk, vv = q_ref[...], k_ref[...], v_ref[...]
            a = acc[...]
            for h, msk in enumerate(_head_masks()):
                st_sc, pt_sc, bias_sc = ((s0, p0, b0), (s1, p1, b1))[h]
                st_sc[...] = _dot(kk, jnp.where(msk, qv, jnp.zeros_like(qv)), 1, 1)
                bias_sc[...] = jnp.broadcast_to(cq_ref[0, h:h + 1, 0:1] - ck_ref[0, :, h:h + 1], (tk, LANES))
                m_old, l_old = ms[h][...], ls[h][...]
                col_max = jnp.full((SUBLANES, tq), -jnp.inf, F32)
                for r in range(tk // rt):
                    rows = slice(r * rt, (r + 1) * rt)
                    s = st_sc[rows, :] + jnp.tile(bias_sc[rows, :], reps)
                    if diagonal:
                        key = r * rt + lax.broadcasted_iota(jnp.int32, (rt, tq), 0)
                        qry = lax.broadcasted_iota(jnp.int32, (rt, tq), 1)
                        s = jnp.where(key <= qry, s, -jnp.inf)
                    st_sc[rows, :] = s
                    for g in range(rt // SUBLANES):
                        col_max = jnp.maximum(col_max, s[g * SUBLANES:(g + 1) * SUBLANES, :])
                m_new = jnp.maximum(m_old, jnp.max(col_max, axis=0, keepdims=True))
                col_sum = jnp.zeros((SUBLANES, tq), F32)
                for r in range(tk // rt):
                    rows = slice(r * rt, (r + 1) * rt)
                    p = jnp.exp(st_sc[rows, :] - m_new)
                    for g in range(rt // SUBLANES):
                        col_sum = col_sum + p[g * SUBLANES:(g + 1) * SUBLANES, :]
                    pt_sc[rows, :] = p.astype(BF16)
                alpha = jnp.exp(m_old - m_new)
                ms[h][...] = m_new
                ls[h][...] = alpha * l_old + jnp.sum(col_sum, axis=0, keepdims=True)
                pv_t = _dot(jnp.where(msk, vv, jnp.zeros_like(vv)), pt_sc[...], 0, 0)
                a = a * jnp.where(head_rows == (h == 0), alpha, 1.0) + pv_t
            acc[...] = a

        pl.when(j < i)(functools.partial(block, False))
        pl.when(j == i)(functools.partial(block, True))

        @pl.when(j == i)
        def _():
            ot_ref[...] = acc[...] * jnp.where(head_rows, 1.0 / l0[...], 1.0 / l1[...])
            lse_ref[0] = jnp.concatenate([m0[...] + jnp.log(l0[...]), m1[...] + jnp.log(l1[...])], axis=0)

    pairs = [(i, j) for i in range(nq) for j in range(i + 1)]
    qi = jnp.asarray([i for i, _ in pairs], jnp.int32)
    kj = jnp.asarray([j for _, j in pairs], jnp.int32)
    stat = pltpu.VMEM((1, tq), F32)
    return _call(
        body, grid=(nhp, len(pairs)), prefetch=(qi, kj),
        in_specs=[pl.BlockSpec((tq, LANES), lambda hp, t, qi, kj: (qi[t], hp)),
                  pl.BlockSpec((tk, LANES), lambda hp, t, qi, kj: (kj[t], hp)),
                  pl.BlockSpec((tk, LANES), lambda hp, t, qi, kj: (kj[t], nhp + hp)),
                  pl.BlockSpec((1, 2, tq), lambda hp, t, qi, kj: (hp, 0, qi[t])),
                  pl.BlockSpec((1, tk, 2), lambda hp, t, qi, kj: (hp, kj[t], 0))],
        out_specs=[pl.BlockSpec((LANES, tq), lambda hp, t, qi, kj: (hp, qi[t])),
                   pl.BlockSpec((1, 2, tq), lambda hp, t, qi, kj: (hp, 0, qi[t]))],
        scratch_shapes=[stat, stat, stat, stat, pltpu.VMEM((LANES, tq), F32), pltpu.VMEM((tk, tq), F32),
                        pltpu.VMEM((tk, tq), F32), pltpu.VMEM((tk, tq), BF16), pltpu.VMEM((tk, tq), BF16),
                        pltpu.VMEM((tk, LANES), F32), pltpu.VMEM((tk, LANES), F32)],
        out_shape=[jax.ShapeDtypeStruct((d, n_rows), F32), jax.ShapeDtypeStruct((nhp, 2, n_rows), F32)],
        args=(q, kv, kv, cum_r, cum_c), name=name, comm=comm)


def _head_delta(do, o, *, name):
    d = o.shape[1]

    def fn(dd, oo):
        prod = dd.astype(BF16).astype(F32) * oo
        r = lax.broadcasted_iota(jnp.int32, (d, LANES), 0)
        c = lax.broadcasted_iota(jnp.int32, (d, LANES), 1)
        return _tri_dot(prod, jnp.where(jnp.right_shift(r, HEAD_DIM.bit_length() - 1) == c, 1.0, 0.0))

    return _rowwise(fn, [do, o], [], [(LANES, F32)], name=name)[0]


def _flash_bwd(q, kv, lse_r, delta_r, do, cum_c, cum_r, *, tq, name, comm=None):
    n_rows, d = q.shape
    nhp = d // LANES
    tk = tq
    nq = n_rows // tq
    nk = n_rows // tk
    rt = _tile(tk, FLASH_ROW_TILE)
    reps = (1, tq // LANES)

    def body(qi_ref, kj_ref, q_ref, k_ref, v_ref, lse_ref, dl_ref, do_ref, cq_ref, ck_ref,
             dq_ref, dk_ref, dv_ref, dck_ref, dcq_ref, s0, s1, dp0, dp1, p0, p1, ds0, ds1, b0, b1, ck0, ck1):
        i = qi_ref[pl.program_id(1)]
        j = kj_ref[pl.program_id(1)]

        @pl.when(pl.program_id(1) == 0)
        def _():
            dq_ref[...] = jnp.zeros_like(dq_ref)
            dcq_ref[...] = jnp.zeros_like(dcq_ref)

        @pl.when(i == j)
        def _():
            for ref in (dk_ref, dv_ref, ck0, ck1):
                ref[...] = jnp.zeros_like(ref)

        def block(diagonal):
            qv, kk, vv = q_ref[...], k_ref[...], v_ref[...]
            dob = do_ref[...].astype(BF16)
            dq_acc = jnp.zeros((tq, LANES), F32)
            dk_acc = jnp.zeros((tk, LANES), F32)
            dv_acc = jnp.zeros((tk, LANES), F32)
            query_sums = []
            for h, msk in enumerate(_head_masks()):
                st_sc, dpt_sc, pt_sc, dst_sc, bias_sc, key_part = ((s0, dp0, p0, ds0, b0, ck0),
                                                                  (s1, dp1, p1, ds1, b1, ck1))[h]
                qh = jnp.where(msk, qv, jnp.zeros_like(qv))
                kh = jnp.where(msk, kk, jnp.zeros_like(kk))
                doh = jnp.where(msk, dob, jnp.zeros_like(dob))
                st_sc[...] = _dot(kk, qh, 1, 1)
                dpt_sc[...] = _dot(vv, doh, 1, 1)
                bias_sc[...] = jnp.broadcast_to(cq_ref[0, h:h + 1, 0:1] - ck_ref[0, :, h:h + 1], (tk, LANES))
                lse_row = lse_ref[0, h:h + 1, :]
                delta_row = dl_ref[0, h:h + 1, :]
                col_acc = jnp.zeros((SUBLANES, tq), F32)
                parts = []
                for r in range(tk // rt):
                    rows = slice(r * rt, (r + 1) * rt)
                    s = st_sc[rows, :] + jnp.tile(bias_sc[rows, :], reps)
                    if diagonal:
                        key = r * rt + lax.broadcasted_iota(jnp.int32, (rt, tq), 0)
                        qry = lax.broadcasted_iota(jnp.int32, (rt, tq), 1)
                        s = jnp.where(key <= qry, s, -jnp.inf)
                    p = jnp.exp(s - lse_row)
                    ds = p * (dpt_sc[rows, :] - delta_row)
                    for g in range(rt // SUBLANES):
                        col_acc = col_acc + ds[g * SUBLANES:(g + 1) * SUBLANES, :]
                    part = ds[:, 0:LANES]
                    for g in range(1, tq // LANES):
                        part = part + ds[:, g * LANES:(g + 1) * LANES]
                    parts.append(part)
                    pt_sc[rows, :] = p.astype(BF16)
                    dst_sc[rows, :] = ds.astype(BF16)
                key_part[...] += jnp.concatenate(parts, axis=0)
                query_sums.append(jnp.sum(col_acc, axis=0, keepdims=True))
                dv_acc = dv_acc + _dot(pt_sc[...], doh, 1, 0)
                dsb = dst_sc[...]
                dk_acc = dk_acc + _dot(dsb, qh, 1, 0)
                dq_acc = dq_acc + _dot(dsb, kh, 0, 0)
            off = pl.multiple_of(i * tq, tq)
            dq_ref[pl.ds(off, tq), :] += dq_acc
            dk_ref[...] += dk_acc
            dv_ref[...] += dv_acc
            dcq_ref[0, i] += jnp.concatenate(query_sums, axis=0)

        pl.when(i > j)(functools.partial(block, False))
        pl.when(i == j)(functools.partial(block, True))

        @pl.when(i == nq - 1)
        def _():
            two = lax.broadcasted_iota(jnp.int32, (tk, 2), 1)
            dck_ref[0] = jnp.where(two == 0, -jnp.sum(ck0[...], axis=1, keepdims=True),
                                   -jnp.sum(ck1[...], axis=1, keepdims=True))

    pairs = [(i, j) for j in range(nk) for i in range(j, nq)]
    qi = jnp.asarray([i for i, _ in pairs], jnp.int32)
    kj = jnp.asarray([j for _, j in pairs], jnp.int32)
    score = pltpu.VMEM((tk, tq), F32)
    score16 = pltpu.VMEM((tk, tq), BF16)
    keystat = pltpu.VMEM((tk, LANES), F32)
    return _call(
        body, grid=(nhp, len(pairs)), prefetch=(qi, kj),
        in_specs=[pl.BlockSpec((tq, LANES), lambda hp, t, qi, kj: (qi[t], hp)),
                  pl.BlockSpec((tk, LANES), lambda hp, t, qi, kj: (kj[t], hp)),
                  pl.BlockSpec((tk, LANES), lambda hp, t, qi, kj: (kj[t], nhp + hp)),
                  pl.BlockSpec((1, 2, tq), lambda hp, t, qi, kj: (hp, 0, qi[t])),
                  pl.BlockSpec((1, 2, tq), lambda hp, t, qi, kj: (hp, 0, qi[t])),
                  pl.BlockSpec((tq, LANES), lambda hp, t, qi, kj: (qi[t], hp)),
                  pl.BlockSpec((1, 2, tq), lambda hp, t, qi, kj: (hp, 0, qi[t])),
                  pl.BlockSpec((1, tk, 2), lambda hp, t, qi, kj: (hp, kj[t], 0))],
        out_specs=[pl.BlockSpec((n_rows, LANES), lambda hp, t, qi, kj: (0, hp)),
                   pl.BlockSpec((tk, LANES), lambda hp, t, qi, kj: (kj[t], hp)),
                   pl.BlockSpec((tk, LANES), lambda hp, t, qi, kj: (kj[t], hp)),
                   pl.BlockSpec((1, tk, 2), lambda hp, t, qi, kj: (hp, kj[t], 0)),
                   pl.BlockSpec((1, nq, 2, tq), lambda hp, t, qi, kj: (hp, 0, 0, 0))],
        scratch_shapes=[score, score, score, score, score16, score16, score16, score16, keystat, keystat,
                        keystat, keystat],
        out_shape=[jax.ShapeDtypeStruct((n_rows, d), F32), jax.ShapeDtypeStruct((n_rows, d), F32),
                   jax.ShapeDtypeStruct((n_rows, d), F32), jax.ShapeDtypeStruct((nhp, n_rows, 2), F32),
                   jax.ShapeDtypeStruct((nhp, nq, 2, tq), F32)],
        args=(q, kv, kv, lse_r, delta_r, do, cum_r, cum_c), name=name, comm=comm)


def _s5_tables(w, layer):
    g, p = w["lam_re"].shape[1:]
    h = w["ssm_b_re"].shape[3]
    n = g * p
    lr = w["lam_re"][layer].reshape(1, n)
    li = w["lam_im"][layer].reshape(1, n)
    ldt = jnp.broadcast_to(w["log_dt"][layer][:, None], (g, p)).reshape(1, n)
    br = w["ssm_b_re"][layer].transpose(2, 0, 1).reshape(h, n)
    bi = w["ssm_b_im"][layer].transpose(2, 0, 1).reshape(h, n)
    cr = w["ssm_c_re"][layer].transpose(1, 0, 2).reshape(h, n)
    ci = w["ssm_c_im"][layer].transpose(1, 0, 2).reshape(h, n)
    return (lr, li, ldt, br, bi, cr, ci), (g, p, h)


def _local_step(x, tgt, w, net=None, *, attn_tile=512):
    n_rows, d = x.shape
    n_layers = w["g_mix"].shape[0]
    n_s5 = w["lam_re"].shape[0]
    nh = w["b_f"].shape[0]
    nhp = nh // 2
    assert d == nh * HEAD_DIM
    tq = _tile(n_rows, attn_tile)
    g = {}
    saved = [dict() for _ in range(n_layers)]
    big = {}
    pending = {}

    def wt(name, layer):
        return w[name][layer]

    def carry_gather(group, run):
        if net is None or not net.has_group(group):
            return run(None)
        outs, got = run(net.gather_comm(group))
        net.store_gathered(group, got, w)
        return outs

    def carry_reduce(tag, run):
        if net is None or not pending:
            big.update(pending)
            pending.clear()
            return run(None)
        keys = list(pending)
        parts = net.reduce_prepare([pending[k] for k in keys], tag)
        outs, landed = run(_chip_exchange_comm(parts))
        big.update(zip(keys, net.reduce_finish(parts, landed, [pending[k] for k in keys], tag)))
        pending.clear()
        return outs

    h = x
    nxt = _rowwise(lambda a, gg: _rms(a, gg), [x], [_row2(w["g_mix"][0])], [(d, F32)], name="rms_first")[0]
    kvb = fl = cum = cq3 = ck3 = hnkv = None
    bf_pad = jnp.zeros((1, LANES), F32).at[0, :nh].set(w["b_f"])
    for l in range(n_layers):
        sv = saved[l]
        sv["h"] = h
        g_ffn = _row2(w["g_ffn"][l])
        if l < n_s5:
            tabs, (_, p, _) = _s5_tables(w, l)
            prep = _s5_prep(*tabs, p, name=f"s5_prep{l}")
            dskip = w["ssm_d"][l].reshape(1, d)
            y, st_re, st_im, sb_re, sb_im = carry_gather(
                f"stage{l}", lambda comm, u=nxt, pr=prep, ds=dskip: _s5_fwd(u, pr, ds, name=f"s5_fwd{l}", comm=comm))
            z = _rowwise(_gelu, [y], [], [(d, BF16)], name=f"gelu{l}")[0]
            zz = _mm_cols(z, *wt("w_glu", l), wc=0, name=f"glu_mm{l}")
            h1, hn2 = _rowwise(lambda hh, zq, gg: ((lambda t: (t, _rms(t, gg)))(hh + _glu(zq))),
                               [h, zz], [g_ffn], [(d, F32), (d, BF16)], name=f"mix_out{l}")
            sv.update(u=nxt, prep=prep, tabs=tabs, p=p, dskip=dskip, st_re=st_re, st_im=st_im, sb_re=sb_re,
                      sb_im=sb_im, y=y, z=z, zz=zz)
        else:
            j = l - n_s5
            qs = _mm_cols(nxt, *wt("w_q", j), wc=0, out_dtype=BF16, scale=HEAD_DIM ** -0.5, name=f"q_mm{j}")
            o_t, lse = carry_gather(
                f"stage{l}", lambda comm, q_=qs: _flash_fwd(q_, kvb, cq3, ck3, tq=tq, name=f"flash_fwd{j}", comm=comm))
            o = o_t.T
            a = _mm_cols(o, *wt("w_o", j), wc=0, name=f"o_mm{j}")
            h1, hn2 = _rowwise(lambda hh, aa, gg: ((lambda t: (t, _rms(t, gg)))(hh + aa)),
                               [h, a], [g_ffn], [(d, F32), (d, BF16)], name=f"mix_out{l}")
            sv.update(hn=nxt, qs=qs, o=o, lse=lse)
        uu = _mm_cols(hn2, *wt("w_in", l), wc=0, name=f"ffn_in{l}")
        cw, cb = w["conv_w"][l], _row2(w["conv_b"][l])
        act = _conv_fwd(uu, cw, cb, name=f"conv_fwd{l}")
        f = _mm_cols(act, *wt("w_out", l), wc=0, name=f"ffn_out{l}")
        sv.update(h1=h1, hn2=hn2, uu=uu, act=act, cw=cw, cb=cb)
        if l == n_layers - 1:
            def loss_fn(hh, ff, tt, gg):
                yv, vjp = jax.vjp(_rms, hh + ff, gg)
                err = yv - tt
                part = 0.5 * jnp.sum(jnp.mean(err * err, axis=-1, keepdims=True), axis=0, keepdims=True)
                dh, dg = vjp(err * (1.0 / d))
                return dh, dh, jnp.broadcast_to(part, (1, LANES)), dg
            dcur, dcur16, loss_row, dgf = _rowwise(loss_fn, [h1, f, tgt], [_row2(w["g_final"])],
                                                   [(d, F32), (d, BF16)], [(1, LANES), (1, d)], name="loss")
            loss = loss_row[0, 0]
            g["g_final"] = dgf[0]
        elif l + 1 < n_s5:
            h, nxt = _rowwise(lambda hh, ff, gg: ((lambda t: (t, _rms(t, gg)))(hh + ff)), [h1, f],
                              [_row2(w["g_mix"][l + 1])], [(d, F32), (d, F32)], name=f"ffn_res{l}")
        elif l + 1 == n_s5:
            h, nxt, hnkv = _rowwise(
                lambda hh, ff, g1, g2: ((lambda t: (t, _rms(t, g1), _rms(t, g2)))(hh + ff)), [h1, f],
                [_row2(w["g_mix"][l + 1]), _row2(w["g_kv"])], [(d, F32), (d, BF16), (d, BF16)], name=f"ffn_res{l}")
            kvb = _mm_cols(hnkv, *wt("w_kv", 0), wc=0, out_dtype=BF16, name="kv_mm")
            fl = _mm_cols(hnkv, *wt("w_f", 0), wc=0, name="f_mm")
            cum = _cum_fwd(fl, bf_pad, name="cum_fwd")
            cq3 = cum[:, :nh].reshape(n_rows, nhp, 2).transpose(1, 0, 2)
            ck3 = cum[:, :nh].T.reshape(nhp, 2, n_rows)
        else:
            h, nxt = _rowwise(lambda hh, ff, gg: ((lambda t: (t, _rms(t, gg)))(hh + ff)), [h1, f],
                              [_row2(w["g_mix"][l + 1])], [(d, F32), (d, BF16)], name=f"ffn_res{l}")

    per_layer = {k: [None] * n_layers for k in ("g_mix", "g_ffn", "conv_w", "conv_b")}
    per_s5 = {k: [None] * n_s5 for k in ("lam_re", "lam_im", "log_dt", "ssm_b_re", "ssm_b_im", "ssm_c_re",
                                         "ssm_c_im", "ssm_d")}
    dk_parts, dv_parts, dck_parts = [], [], []

    def by_row_shard(m):
        return m.reshape(N_CHIPS, m.shape[0] // N_CHIPS, m.shape[1])

    for l in reversed(range(n_layers)):
        sv = saved[l]
        dact = _mm_cols(dcur16, *wt("w_out", l), wc=1, name=f"ffn_out_dx{l}")
        pending["w_ffn_out", l] = by_row_shard(_mm_tn(sv["act"], dcur16, 1, name=f"ffn_out_dw{l}")[0])
        duu, dcw, dcb = _conv_bwd(sv["uu"], dact, sv["cw"], sv["cb"], name=f"conv_bwd{l}")
        per_layer["conv_w"][l] = jnp.concatenate([dcw[0], dcw[1]], axis=-1)
        per_layer["conv_b"][l] = jnp.concatenate([dcb[0, 0], dcb[1, 0]])
        dhn2 = _mm_acc(duu, *wt("w_in", l), wc=1, name=f"ffn_in_dx{l}")
        pending["w_ffn_in", l] = _mm_tn(sv["hn2"], duu, wt("w_in", l)[0].shape[0], name=f"ffn_in_dw{l}")
        d1, d1_16, (dg,) = _node_bwd(dcur, sv["h1"], [(w["g_ffn"][l], [dhn2])], name=f"ffn_norm_bwd{l}")
        per_layer["g_ffn"][l] = dg
        if l < n_s5:
            def glu_bwd(zq, dd):
                _, vjp = jax.vjp(_glu, zq)
                return vjp(dd)[0]
            dzz = _rowwise(glu_bwd, [sv["zz"], d1], [], [(2 * d, BF16)], name=f"glu_bwd{l}")[0]
            dz = _mm_acc(dzz, *wt("w_glu", l), wc=1, name=f"glu_dx{l}")
            pending["w_glu", l] = _mm_tn(sv["z"], dzz, wt("w_glu", l)[0].shape[0], name=f"glu_dw{l}")

            def gelu_bwd(yy, dd):
                _, vjp = jax.vjp(_gelu, yy)
                return vjp(dd)[0]
            dy = _rowwise(gelu_bwd, [sv["y"], dz], [], [(d, F32)], name=f"gelu_bwd{l}")[0]
            du, dwbr, dwbi, dwcr, dwci, dlbr, dlbi, dd = carry_reduce(
                f"stage{l}", lambda comm, dy_=dy: _s5_bwd(sv["u"], dy_, sv["st_re"], sv["st_im"], sv["sb_re"],
                                                          sv["sb_im"], sv["prep"], sv["dskip"], name=f"s5_bwd{l}",
                                                          comm=comm))
            dlr, dli, dldt, dbr, dbi, dcr, dci = _s5_prep_bwd(*sv["tabs"], sv["p"], dlbr, dlbi, dwbr, dwbi, dwcr,
                                                              dwci, name=f"s5_prep_bwd{l}")
            gg, p = w["lam_re"].shape[1:]
            hh = w["ssm_b_re"].shape[3]
            per_s5["lam_re"][l] = dlr.reshape(gg, p)
            per_s5["lam_im"][l] = dli.reshape(gg, p)
            per_s5["log_dt"][l] = dldt.reshape(gg, p).sum(axis=1)
            per_s5["ssm_b_re"][l] = dbr.reshape(hh, gg, p).transpose(1, 2, 0)
            per_s5["ssm_b_im"][l] = dbi.reshape(hh, gg, p).transpose(1, 2, 0)
            per_s5["ssm_c_re"][l] = dcr.reshape(hh, gg, p).transpose(1, 0, 2)
            per_s5["ssm_c_im"][l] = dci.reshape(hh, gg, p).transpose(1, 0, 2)
            per_s5["ssm_d"][l] = dd.reshape(d)
            branches = [(w["g_mix"][l], [du])]
        else:
            j = l - n_s5
            do = _mm_cols(d1_16, *wt("w_o", j), wc=1, name=f"o_dx{j}")
            pending["w_o", j] = by_row_shard(_mm_tn(sv["o"], d1_16, 1, name=f"o_dw{j}")[0])
            delta_r = _head_delta(do, sv["o"], name=f"head_delta{j}")[:, :nh].T.reshape(nhp, 2, n_rows)
            dq, dk, dv, dck, dcq = carry_reduce(
                f"stage{l}", lambda comm, do_=do: _flash_bwd(sv["qs"], kvb, sv["lse"], delta_r, do_, cq3, ck3, tq=tq,
                                                          name=f"flash_bwd{j}", comm=comm))
            dk_parts.append(dk)
            dv_parts.append(dv)
            dck_parts.append(dck.transpose(1, 0, 2).reshape(n_rows, nh)
                             + dcq.transpose(0, 2, 1, 3).reshape(nh, n_rows).T)
            scale = HEAD_DIM ** -0.5
            dhn = _mm_cols(dq, *wt("w_q", j), wc=1, scale=scale, name=f"q_dx{j}")
            pending["w_q", j] = by_row_shard(_mm_tn(sv["hn"], dq, 1, scale=scale, name=f"q_dw{j}")[0])
            branches = [(w["g_mix"][l], [dhn])]
            if j == 0:
                def kv_sum(*parts):
                    half = len(parts) // 2
                    return jnp.concatenate([sum(parts[:half][1:], parts[0]),
                                            sum(parts[half:][1:], parts[half])], axis=1)
                dkv = _rowwise(kv_sum, dk_parts + dv_parts, [], [(2 * d, BF16)], name="dkv_sum")[0]
                dck_tot = dck_parts[0]
                for extra in dck_parts[1:]:
                    dck_tot = dck_tot + extra
                dcum = jnp.zeros((n_rows, LANES), F32).at[:, :nh].set(dck_tot)
                dfl, dbf = _cum_bwd(dcum, fl, bf_pad, name="cum_bwd")
                g["b_f"] = dbf[0, :nh]
                dhkv_a = _mm_cols(dkv, *wt("w_kv", 0), wc=1, name="kv_dx")
                dhkv_b = _mm_cols(dfl, *wt("w_f", 0), wc=1, name="f_dx")
                d_kvf = jnp.concatenate([_mm_tn(hnkv, dkv, 1, name="kv_dw")[0],
                                         _mm_tn(hnkv, dfl, 1, name="f_dw")[0][:, :nh]], axis=1)
                pending["w_kvf", 0] = d_kvf.reshape(d, N_CHIPS, -1).transpose(1, 0, 2)
                branches.append((w["g_kv"], [dhkv_a, dhkv_b]))
        dcur, dcur16, dgs = _node_bwd(d1, sv["h"], branches, name=f"mix_norm_bwd{l}")
        per_layer["g_mix"][l] = dgs[0]
        if len(dgs) > 1:
            g["g_kv"] = dgs[1]

    if pending:
        big.update(pending if net is None else
                   zip(list(pending), net.reduce_blocking([pending[k] for k in pending], "tail")))
    for k, v in (*per_layer.items(), *per_s5.items()):
        g[k] = jnp.stack(v)
    g["big"] = big
    return loss, dcur, g


def _position():
    x, y, c = lax.axis_index("x"), lax.axis_index("y"), lax.axis_index("c")
    chips = [(1 - x, y), (x, 1 - y), (1 - x, 1 - y)]
    return x, y, c, chips


def _all_gather_comm(shards):
    n = len(shards)

    def descriptors(ins, outs, sems):
        send_sems, recv_sems = sems
        x, y, c, chips = _position()
        my_slot = 2 * x + y
        sibling = (x, y, 1 - c)

        def rows(t, half):
            hr = ins[t].shape[0] // 2
            return pl.ds(half * hr, hr)

        def remote(k, t, src, dst, to):
            return pltpu.make_async_remote_copy(src_ref=src, dst_ref=dst, send_sem=send_sems.at[k, t],
                                                recv_sem=recv_sems.at[k, t], device_id=to, device_id_type=MESH)

        own = [remote(6, t, ins[t], outs[t].at[my_slot], sibling) for t in range(n)]
        ici = [remote(j, t, ins[t].at[rows(t, c)], outs[t].at[my_slot, rows(t, c)], (*chip, c))
               for j, chip in enumerate(chips) for t in range(n)]
        slots = [2 * chip[0] + chip[1] for chip in chips]
        fwd = [[remote(3 + j, t, outs[t].at[slots[j], rows(t, c)], outs[t].at[slots[j], rows(t, c)], sibling)
                for t in range(n)] for j in range(len(chips))]
        landed = [[remote(j, t, outs[t].at[slots[j], rows(t, c)], outs[t].at[slots[j], rows(t, c)], (*chips[j], c))
                   for t in range(n)] for j in range(len(chips))]
        from_sibling = [remote(3 + j, t, outs[t].at[slots[j], rows(t, 1 - c)], outs[t].at[slots[j], rows(t, 1 - c)],
                               sibling) for j in range(len(chips)) for t in range(n)]
        return own, ici, fwd, landed, from_sibling

    def start(ins, outs, sems):
        own, ici, _, _, _ = descriptors(ins, outs, sems)
        for cp in own + ici:
            cp.start()

    def finish(ins, outs, sems):
        own, ici, fwd, landed, from_sibling = descriptors(ins, outs, sems)
        for j in range(len(fwd)):
            for cp in landed[j]:
                cp.wait_recv()
            for cp in fwd[j]:
                cp.start()
        for cp in from_sibling + own:
            cp.wait_recv()
        for cp in own + ici + [cp for group in fwd for cp in group]:
            cp.wait_send()

    return _Comm(list(shards), [jax.ShapeDtypeStruct((N_CHIPS,) + a.shape, a.dtype) for a in shards],
                 [pltpu.SemaphoreType.DMA((7, n)), pltpu.SemaphoreType.DMA((7, n))], start, finish)


def _all_gather(shards, *, name):
    return _run_comm(_all_gather_comm(shards), name=name)


def _pair_exchange(grads, *, name):
    n = len(grads)

    def body(*refs):
        ins, outs = refs[:n], refs[n:2 * n]
        send_sems, recv_sems = refs[2 * n:]
        x, y, c, _ = _position()
        copies = [pltpu.make_async_remote_copy(src_ref=ins[t].at[:, 1 - c], dst_ref=outs[t],
                                               send_sem=send_sems.at[t], recv_sem=recv_sems.at[t],
                                               device_id=(x, y, 1 - c), device_id_type=MESH) for t in range(n)]
        for cp in copies:
            cp.start()
        for cp in copies:
            cp.wait()

    return pl.pallas_call(
        body, in_specs=_any_specs(n), out_specs=_any_specs(n),
        out_shape=[jax.ShapeDtypeStruct((a.shape[0],) + a.shape[2:], a.dtype) for a in grads],
        scratch_shapes=[pltpu.SemaphoreType.DMA((n,)), pltpu.SemaphoreType.DMA((n,))], name=name)(*grads)


def _chip_exchange_comm(parts):
    n = len(parts)

    def copies(ins, outs, sems):
        send_sems, recv_sems = sems
        _, _, c, chips = _position()
        return [pltpu.make_async_remote_copy(src_ref=ins[t].at[2 * chip[0] + chip[1]], dst_ref=outs[t].at[j],
                                             send_sem=send_sems.at[j, t], recv_sem=recv_sems.at[j, t],
                                             device_id=(*chip, c), device_id_type=MESH)
                for j, chip in enumerate(chips) for t in range(n)]

    def start(ins, outs, sems):
        for cp in copies(ins, outs, sems):
            cp.start()

    def finish(ins, outs, sems):
        for cp in copies(ins, outs, sems):
            cp.wait()

    return _Comm(list(parts), [jax.ShapeDtypeStruct((N_CHIPS - 1,) + a.shape[1:], a.dtype) for a in parts],
                 [pltpu.SemaphoreType.DMA((N_CHIPS - 1, n)), pltpu.SemaphoreType.DMA((N_CHIPS - 1, n))],
                 start, finish)


def _pair_share(both, *, name):
    n = len(both)

    def body(*refs):
        ins, outs = refs[:n], refs[n:2 * n]
        send_sems, recv_sems = refs[2 * n:]
        x, y, c, _ = _position()
        for t in range(n):
            pltpu.make_async_remote_copy(src_ref=ins[t].at[c], dst_ref=outs[t].at[c], send_sem=send_sems.at[t],
                                         recv_sem=recv_sems.at[t], device_id=(x, y, 1 - c),
                                         device_id_type=MESH).start()
        for t in range(n):
            pltpu.make_async_remote_copy(src_ref=ins[t].at[c], dst_ref=outs[t].at[1 - c], send_sem=send_sems.at[t],
                                         recv_sem=recv_sems.at[t], device_id=(x, y, 1 - c),
                                         device_id_type=MESH).wait()

    return pl.pallas_call(
        body, in_specs=_any_specs(n), out_specs=_any_specs(n),
        out_shape=[jax.ShapeDtypeStruct(a.shape, a.dtype) for a in both],
        input_output_aliases={t: t for t in range(n)},
        scratch_shapes=[pltpu.SemaphoreType.DMA((n,)), pltpu.SemaphoreType.DMA((n,))], name=name)(*both)


def _sum_pair(grad, landed, c, wire_dtype, *, name):
    slots, _, m, n = grad.shape
    tm = _tile(m, 256, 2 * SUBLANES)

    def body(c_ref, g_ref, l_ref, o_ref):
        o_ref[...] = (g_ref[0] + l_ref[...]).astype(wire_dtype)

    return pl.pallas_call(
        body,
        grid_spec=pltpu.PrefetchScalarGridSpec(
            num_scalar_prefetch=1, grid=(slots, m // tm),
            in_specs=[pl.BlockSpec((1, 1, tm, n), lambda s, i, c_ref: (s, c_ref[0], i, 0)),
                      pl.BlockSpec((1, tm, n), lambda s, i, c_ref: (s, i, 0))],
            out_specs=pl.BlockSpec((1, tm, n), lambda s, i, c_ref: (s, i, 0))),
        out_shape=jax.ShapeDtypeStruct((slots, m, n), wire_dtype), compiler_params=_cp(2), name=name)(
            c, grad, landed)


def _sum_chips(part, landed, slot_c, *, name):
    _, m, n = part.shape
    tm = _tile(m, 256, 2 * SUBLANES)

    def body(s_ref, p_ref, l_ref, o_ref):
        acc = p_ref[0].astype(F32)
        for j in range(N_CHIPS - 1):
            acc = acc + l_ref[j].astype(F32)
        o_ref[0] = acc

    return pl.pallas_call(
        body,
        grid_spec=pltpu.PrefetchScalarGridSpec(
            num_scalar_prefetch=1, grid=(m // tm,),
            in_specs=[pl.BlockSpec((1, tm, n), lambda i, s_ref: (s_ref[0], i, 0)),
                      pl.BlockSpec((N_CHIPS - 1, tm, n), lambda i, s_ref: (0, i, 0))],
            out_specs=pl.BlockSpec((1, tm, n), lambda i, s_ref: (s_ref[1], i, 0))),
        out_shape=jax.ShapeDtypeStruct((N_CORES, m, n), F32), compiler_params=_cp(1), name=name)(
            slot_c, part, landed)


def _reduce_prepare(grads, wire_dtypes, tag):
    c = lax.axis_index("c").reshape(1).astype(jnp.int32)
    views = []
    for a in grads:
        lead, last = a.shape[1], a.shape[-1]
        mid = 1
        for s in a.shape[2:-1]:
            mid *= s
        views.append(a.reshape(N_CHIPS, N_CORES, (lead // N_CORES) * mid, last))
    landed = _pair_exchange(views, name=f"rs_pair_exchange_{tag}")
    return [_sum_pair(v, l, c, wire_dtypes[t], name=f"rs_pair_sum_{tag}_{t}")
            for t, (v, l) in enumerate(zip(views, landed))]


def _reduce_finish(parts, landed, grads, tag):
    slot_c = jnp.stack([2 * lax.axis_index("x") + lax.axis_index("y"), lax.axis_index("c")]).astype(jnp.int32)
    both = [_sum_chips(p, l, slot_c, name=f"rs_chip_sum_{tag}_{t}") for t, (p, l) in enumerate(zip(parts, landed))]
    full = _pair_share(both, name=f"rs_pair_share_{tag}")
    return [f.reshape(a.shape[1:]) for f, a in zip(full, grads)]


def _reduce_scatter(grads, wire_dtypes, tag):
    parts = _reduce_prepare(grads, wire_dtypes, tag)
    landed = _run_comm(_chip_exchange_comm(parts), name=f"rs_chip_exchange_{tag}")
    return _reduce_finish(parts, landed, grads, tag)


class _Net:
    def __init__(self, groups, d, nh):
        self.groups, self.d, self.nh = groups, d, nh

    def has_group(self, group):
        return bool(self.groups.get(group))

    def gather_comm(self, group):
        return _all_gather_comm([shard for _, _, shard in self.groups[group]])

    def store_gathered(self, group, got, w):
        d, nh = self.d, self.nh
        for (name, layer, _), full in zip(self.groups[group], got):
            if name == "w_kvf":
                mat = full.transpose(1, 0, 2).reshape(d, -1)
                w["w_kv"][0] = (mat[:, :2 * d][None, None], 0)
                w["w_f"][0] = (jnp.zeros((d, LANES), BF16).at[:, :nh].set(mat[:, 2 * d:])[None, None], 0)
            elif name == "w_in":
                w[name][layer] = (full[:, None], 0)
            else:
                w[name][layer] = (full.reshape(1, 1, -1, full.shape[-1]), 0)

    def reduce_prepare(self, grads, tag):
        return _reduce_prepare(grads, [BF16] * len(grads), tag)

    def reduce_finish(self, parts, landed, grads, tag):
        return _reduce_finish(parts, landed, grads, tag)

    def reduce_blocking(self, grads, tag):
        return _reduce_scatter(grads, [BF16] * len(grads), tag)


def _adamw(w, g, m, v, *, name):
    def fn(ww, gg, mm, vv):
        mm = ADAM_B1 * mm + (1.0 - ADAM_B1) * gg
        vv = ADAM_B2 * vv + (1.0 - ADAM_B2) * (gg * gg)
        m_hat = mm / (1.0 - ADAM_B1 ** ADAM_STEP)
        v_hat = vv / (1.0 - ADAM_B2 ** ADAM_STEP)
        delta = -ADAM_LR * (m_hat / (jnp.sqrt(v_hat) + ADAM_EPS) + ADAM_WD * ww)
        return delta, mm, vv

    shape = w.shape
    two_d = [a.reshape(-1, shape[-1]) for a in (w, g, m, v)]
    outs = _rowwise(fn, two_d, [], [(shape[-1], F32)] * 3, name=name)
    return [o.reshape(shape) for o in outs]


def _to_bf16(a, *, name):
    two_d = a.reshape(-1, a.shape[-1])
    return _rowwise(lambda t: t, [two_d], [], [(a.shape[-1], BF16)], name=name)[0].reshape(a.shape)


def _pack(arrays, rows_multiple):
    flat = jnp.concatenate([a.reshape(-1) for a in arrays])
    rows = -(-flat.shape[0] // LANES)
    rows = -(-rows // rows_multiple) * rows_multiple
    return jnp.pad(flat, (0, rows * LANES - flat.shape[0])).reshape(rows, LANES)


def _unpack(packed, like):
    flat = packed.reshape(-1)
    out, pos = [], 0
    for a in like:
        out.append(flat[pos:pos + a.size].reshape(a.shape))
        pos += a.size
    return out


_PARAMS = ("g_mix", "g_ffn", "lam_re", "lam_im", "log_dt", "ssm_b_re", "ssm_b_im", "ssm_c_re", "ssm_c_im", "ssm_d",
           "w_glu", "g_kv", "w_kvf", "b_f", "w_q", "w_o", "w_ffn_in", "ffn_conv_w", "ffn_conv_b", "w_ffn_out",
           "g_final")
_BIG = ("w_glu", "w_kvf", "w_q", "w_o", "w_ffn_in", "w_ffn_out")
_SMALL_SHARDED = ("ssm_d", "ffn_conv_w")


def kernel(x, g_mix, g_ffn, lam_re, lam_im, log_dt, ssm_b_re, ssm_b_im, ssm_c_re, ssm_c_im, ssm_d, w_glu, g_kv, w_kvf, b_f, w_q, w_o, w_ffn_in, ffn_conv_w, ffn_conv_b, w_ffn_out, g_final, loss_target, m_g_mix, m_g_ffn, m_lam_re, m_lam_im, m_log_dt, m_ssm_b_re, m_ssm_b_im, m_ssm_c_re, m_ssm_c_im, m_ssm_d, m_w_glu, m_g_kv, m_w_kvf, m_b_f, m_w_q, m_w_o, m_w_ffn_in, m_ffn_conv_w, m_ffn_conv_b, m_w_ffn_out, m_g_final, v_g_mix, v_g_ffn, v_lam_re, v_lam_im, v_log_dt, v_ssm_b_re, v_ssm_b_im, v_ssm_c_re, v_ssm_c_im, v_ssm_d, v_w_glu, v_g_kv, v_w_kvf, v_b_f, v_w_q, v_w_o, v_w_ffn_in, v_ffn_conv_w, v_ffn_conv_b, v_w_ffn_out, v_g_final):
    p = dict(g_mix=g_mix, g_ffn=g_ffn, lam_re=lam_re, lam_im=lam_im, log_dt=log_dt, ssm_b_re=ssm_b_re,
             ssm_b_im=ssm_b_im, ssm_c_re=ssm_c_re, ssm_c_im=ssm_c_im, ssm_d=ssm_d, w_glu=w_glu, g_kv=g_kv,
             w_kvf=w_kvf, b_f=b_f, w_q=w_q, w_o=w_o, w_ffn_in=w_ffn_in, ffn_conv_w=ffn_conv_w,
             ffn_conv_b=ffn_conv_b, w_ffn_out=w_ffn_out, g_final=g_final)
    mom1 = dict(zip(_PARAMS, (m_g_mix, m_g_ffn, m_lam_re, m_lam_im, m_log_dt, m_ssm_b_re, m_ssm_b_im, m_ssm_c_re,
                              m_ssm_c_im, m_ssm_d, m_w_glu, m_g_kv, m_w_kvf, m_b_f, m_w_q, m_w_o, m_w_ffn_in,
                              m_ffn_conv_w, m_ffn_conv_b, m_w_ffn_out, m_g_final)))
    mom2 = dict(zip(_PARAMS, (v_g_mix, v_g_ffn, v_lam_re, v_lam_im, v_log_dt, v_ssm_b_re, v_ssm_b_im, v_ssm_c_re,
                              v_ssm_c_im, v_ssm_d, v_w_glu, v_g_kv, v_w_kvf, v_b_f, v_w_q, v_w_o, v_w_ffn_in,
                              v_ffn_conv_w, v_ffn_conv_b, v_w_ffn_out, v_g_final)))
    d = x.shape[-1]
    nh = b_f.shape[0]
    slot = 2 * lax.axis_index("x") + lax.axis_index("y")

    wb = {k: _to_bf16(p[k], name=f"to_bf16_{k}") for k in _BIG}
    gd, gcw, gl = _all_gather([ssm_d, ffn_conv_w, wb["w_glu"]], name="first_all_gather")
    n_lay, n_s5 = w_ffn_in.shape[0], lam_re.shape[0]
    n_fox = n_lay - n_s5
    groups = {f"stage{l}": [("w_in", l, wb["w_ffn_in"][l]), ("w_out", l, wb["w_ffn_out"][l])] for l in range(n_lay)}
    groups[f"stage{n_s5 - 1}"] += [("w_kvf", 0, wb["w_kvf"])] + [(k, j, wb[k][j]) for k in ("w_q", "w_o")
                                                                for j in range(n_fox)]
    w = dict(p)
    w.update(w_glu=[(gl, l) for l in range(n_s5)], w_in=[None] * n_lay, w_out=[None] * n_lay, w_q=[None] * n_fox,
             w_o=[None] * n_fox, w_kv=[None], w_f=[None],
             conv_w=gcw.transpose(1, 2, 0, 3).reshape(n_lay, DT_CONV_TAPS, -1), conv_b=ffn_conv_b,
             ssm_d=gd.transpose(1, 0, 2).reshape(gd.shape[1], d))

    loss_part, grad_x, g = _local_step(x[0], loss_target[0], w, _Net(groups, d, nh))
    loss = lax.psum(loss_part, ("x", "y", "c"))

    small_names = [k for k in _PARAMS if k not in _BIG]
    small_full = dict(g_mix=g["g_mix"], g_ffn=g["g_ffn"], lam_re=g["lam_re"], lam_im=g["lam_im"], log_dt=g["log_dt"],
                      ssm_b_re=g["ssm_b_re"], ssm_b_im=g["ssm_b_im"], ssm_c_re=g["ssm_c_re"], ssm_c_im=g["ssm_c_im"],
                      ssm_d=g["ssm_d"], g_kv=g["g_kv"], b_f=g["b_f"], ffn_conv_w=g["conv_w"],
                      ffn_conv_b=g["conv_b"], g_final=g["g_final"])
    small_list = [small_full[k] for k in small_names]
    pack = _pack(small_list, N_CHIPS * N_CORES * 2 * SUBLANES)
    pack4 = pack.reshape(N_CHIPS, pack.shape[0] // N_CHIPS, LANES)
    pack_shard = _reduce_scatter([pack4], [F32], "small")[0]
    red_big = {k: g["big"][k, 0] if p[k].ndim == 2 else jnp.stack([g["big"][k, l] for l in range(p[k].shape[0])])
               for k in _BIG}
    pack_all = _all_gather([pack_shard], name="small_grads_all_gather")[0]
    red_small = dict(zip(small_names, _unpack(pack_all, small_list)))
    for k in _SMALL_SHARDED:
        width = p[k].shape[-1]
        red_small[k] = lax.dynamic_slice_in_dim(red_small[k], slot * width, width, axis=red_small[k].ndim - 1)

    grads, deltas, new_m, new_v = {}, {}, {}, {}
    for k in _BIG:
        grads[k] = red_big[k]
        deltas[k], new_m[k], new_v[k] = _adamw(p[k], grads[k], mom1[k], mom2[k], name=f"adamw_{k}")
    packs = [_pack([src[k] for k in small_names], SUBLANES) for src in (p, red_small, mom1, mom2)]
    like = [p[k] for k in small_names]
    outs = [_unpack(o, like) for o in _adamw(*packs, name="adamw_small")]
    for i, k in enumerate(small_names):
        grads[k] = red_small[k]
        deltas[k], new_m[k], new_v[k] = outs[0][i], outs[1][i], outs[2][i]
    return (loss, grad_x[None], *[grads[k] for k in _PARAMS], *[deltas[k] for k in _PARAMS],
            *[new_m[k] for k in _PARAMS], *[new_v[k] for k in _PARAMS])
```

```python
import functools

import jax
import jax.numpy as jnp
from jax import lax
from jax.experimental import pallas as pl
from jax.experimental.pallas import tpu as pltpu

F32 = jnp.float32
BF16 = jnp.bfloat16

RMS_EPS = 1e-6
ADAM_LR = 0.001
ADAM_B1 = 0.9
ADAM_B2 = 0.999
ADAM_EPS = 1e-08
ADAM_WD = 0.01
ADAM_STEP = 10
DT_CONV_TAPS = 3

LANES = 128
SUBLANES = 8
HEAD_DIM = 64
FLASH_ROW_TILE = 32
S5_BLOCK_GROUPS = 16
VMEM_LIMIT_BYTES = 48 << 20
MM_BLOCK_BUDGET_BYTES = 30 << 20
N_CHIPS = 4
N_CORES = 2
MESH = pl.DeviceIdType.MESH


def _cp(n_grid):
    return pltpu.CompilerParams(dimension_semantics=("arbitrary",) * n_grid, vmem_limit_bytes=VMEM_LIMIT_BYTES)


def _tile(n, pref, mult=SUBLANES):
    if n <= pref:
        return n
    t = (pref // mult) * mult
    while t >= mult:
        if n % t == 0:
            return t
        t -= mult
    return n


class _Comm:
    def __init__(self, ins, out_shapes, sems, start, finish):
        self.ins, self.out_shapes, self.sems, self.start, self.finish = ins, out_shapes, sems, start, finish


def _any_specs(n):
    return [pl.BlockSpec(memory_space=pl.ANY)] * n


def _run_comm(comm, *, name):
    n_in, n_out = len(comm.ins), len(comm.out_shapes)

    def body(*refs):
        ins, outs, sems = refs[:n_in], refs[n_in:n_in + n_out], refs[n_in + n_out:]
        comm.start(ins, outs, sems)
        comm.finish(ins, outs, sems)

    return pl.pallas_call(body, in_specs=_any_specs(n_in), out_specs=_any_specs(n_out),
                          out_shape=list(comm.out_shapes), scratch_shapes=list(comm.sems), name=name)(*comm.ins)


def _call(body, *, grid, in_specs, out_specs, out_shape, args, name, scratch_shapes=(), prefetch=(), comm=None):
    n_pre, n_in, n_out, n_scr = len(prefetch), len(in_specs), len(out_specs), len(scratch_shapes)
    in_specs, out_specs, out_shape = list(in_specs), list(out_specs), list(out_shape)
    scratch_shapes, args = list(scratch_shapes), list(args)
    kernel_body = body
    if comm is not None:
        n_cin, n_cout = len(comm.ins), len(comm.out_shapes)

        def kernel_body(*refs):
            pos = n_pre + n_in
            c_in = refs[pos:pos + n_cin]
            main_out = refs[pos + n_cin:pos + n_cin + n_out]
            pos += n_cin + n_out
            c_out = refs[pos:pos + n_cout]
            main_scr = refs[pos + n_cout:pos + n_cout + n_scr]
            sems = refs[pos + n_cout + n_scr:]
            ids = [pl.program_id(a) for a in range(len(grid))]
            first = functools.reduce(jnp.logical_and, [i == 0 for i in ids])
            last = functools.reduce(jnp.logical_and, [i == g - 1 for i, g in zip(ids, grid)])
            pl.when(first)(lambda: comm.start(c_in, c_out, sems))
            body(*refs[:n_pre + n_in], *main_out, *main_scr)
            pl.when(last)(lambda: comm.finish(c_in, c_out, sems))

        in_specs += _any_specs(n_cin)
        out_specs += _any_specs(n_cout)
        out_shape += list(comm.out_shapes)
        scratch_shapes += list(comm.sems)
        args += list(comm.ins)
    if prefetch:
        spec = pltpu.PrefetchScalarGridSpec(num_scalar_prefetch=n_pre, grid=grid, in_specs=in_specs,
                                            out_specs=out_specs, scratch_shapes=scratch_shapes)
        res = pl.pallas_call(kernel_body, grid_spec=spec, out_shape=out_shape, compiler_params=_cp(len(grid)),
                             name=name)(*prefetch, *args)
    else:
        res = pl.pallas_call(kernel_body, grid=grid, in_specs=in_specs, out_specs=out_specs, out_shape=out_shape,
                             scratch_shapes=scratch_shapes, compiler_params=_cp(len(grid)), name=name)(*args)
    return (res[:n_out], res[n_out:]) if comm is not None else res


def _row_tile(m, bytes_per_row, fixed_bytes):
    for tm in (1024, 512):
        if m % tm == 0 and 2 * (tm * bytes_per_row + fixed_bytes) <= MM_BLOCK_BUDGET_BYTES:
            return tm
    return _tile(m, 512)


def _dot(a, b, ca, cb):
    return lax.dot_general(a, b, (((ca,), (cb,)), ((), ())), preferred_element_type=F32)


def _epilogue_io(epilogue, m, tm, rows_axis, grid_rank):
    _, rows, consts, outs, accs = epilogue

    def at_rows(width):
        return pl.BlockSpec((tm, width), lambda *g: (g[rows_axis], 0))

    def whole(shape):
        return pl.BlockSpec(shape, lambda *g: (0,) * len(shape))

    in_specs = [at_rows(r.shape[1]) for r in rows] + [whole(c.shape) for c in consts]
    out_specs = [at_rows(wd) for wd, _ in outs] + [whole(s) for s in accs]
    out_shape = ([jax.ShapeDtypeStruct((m, wd), dt) for wd, dt in outs]
                 + [jax.ShapeDtypeStruct(s, F32) for s in accs])
    bytes_per_row = (sum(r.shape[1] * r.dtype.itemsize for r in rows)
                     + sum(wd * jnp.dtype(dt).itemsize for wd, dt in outs))
    return in_specs, out_specs, out_shape, bytes_per_row


def _epilogue_apply(epilogue, block, refs, first_row_tile):
    fn, rows, consts, outs, _ = epilogue
    n_in, n_out = len(rows) + len(consts), len(outs)
    res = fn(block, *[r[...] for r in refs[:n_in]])
    for o, val in zip(refs[n_in:n_in + n_out], res[:n_out]):
        o[...] = val.astype(o.dtype)
    a_refs = refs[n_in + n_out:]
    if a_refs:
        @pl.when(first_row_tile)
        def _():
            for a in a_refs:
                a[...] = jnp.zeros_like(a)
        for a, val in zip(a_refs, res[n_out:]):
            a[...] += val


def _mm_cols(x, w4, layer, *, wc, out_dtype=F32, scale=None, epilogue=None, name):
    m, k = x.shape
    slots, _, k0, k1 = w4.shape
    nb = k1 if wc == 0 else k0
    assert (k0 if wc == 0 else k1) == k
    if epilogue is None:
        tm = _row_tile(m, k * x.dtype.itemsize + nb * jnp.dtype(out_dtype).itemsize, k0 * k1 * w4.dtype.itemsize)
        extra_in, out_specs = [], pl.BlockSpec((tm, nb), lambda s, i: (i, s))
        out_shape = jax.ShapeDtypeStruct((m, slots * nb), out_dtype)
    else:
        assert slots == 1
        bytes_per_row = _epilogue_io(epilogue, m, SUBLANES, 1, 2)[3]
        tm = _row_tile(m, k * x.dtype.itemsize + bytes_per_row, k0 * k1 * w4.dtype.itemsize)
        extra_in, out_specs, out_shape, _ = _epilogue_io(epilogue, m, tm, 1, 2)

    def body(x_ref, w_ref, *refs):
        acc = _dot(x_ref[...].astype(BF16), w_ref[0, 0], 1, wc)
        if scale is not None:
            acc = acc * scale
        if epilogue is None:
            refs[0][...] = acc.astype(out_dtype)
        else:
            _epilogue_apply(epilogue, acc, refs, pl.program_id(1) == 0)

    extra_args = [] if epilogue is None else [*epilogue[1], *epilogue[2]]
    return pl.pallas_call(
        body, grid=(slots, m // tm),
        in_specs=[pl.BlockSpec((tm, k), lambda s, i: (i, 0)),
                  pl.BlockSpec((1, 1, k0, k1), lambda s, i: (s, layer, 0, 0)), *extra_in],
        out_specs=out_specs, out_shape=out_shape,
        compiler_params=_cp(2), name=name)(x, w4, *extra_args)


def _planes(a):
    return a if a.ndim == 3 else a[None]


def _mm_acc(x, w4, layer, *, wc, epilogue=None, name):
    x = _planes(x)
    n_planes, m, width = x.shape
    slots, _, k0, k1 = w4.shape
    kb = k0 if wc == 0 else k1
    nout = k1 if wc == 0 else k0
    assert n_planes * width == slots * kb
    spp = slots // n_planes
    x_spec_w = pl.BlockSpec((1, 1, k0, k1), lambda i, s: (s, layer, 0, 0))
    if epilogue is None:
        tm = _row_tile(m, kb * x.dtype.itemsize + nout * 4, k0 * k1 * w4.dtype.itemsize)

        def body(x_ref, w_ref, o_ref):
            @pl.when(pl.program_id(1) == 0)
            def _():
                o_ref[...] = jnp.zeros_like(o_ref)
            o_ref[...] += _dot(x_ref[0].astype(BF16), w_ref[0, 0], 1, wc)

        return pl.pallas_call(
            body, grid=(m // tm, slots),
            in_specs=[pl.BlockSpec((1, tm, kb), lambda i, s: (s // spp, i, s % spp)), x_spec_w],
            out_specs=pl.BlockSpec((tm, nout), lambda i, s: (i, 0)),
            out_shape=jax.ShapeDtypeStruct((m, nout), F32),
            compiler_params=_cp(2), name=name)(x, w4)

    bytes_per_row = _epilogue_io(epilogue, m, SUBLANES, 0, 2)[3]
    tm = _row_tile(m, kb * x.dtype.itemsize + nout * 2 + bytes_per_row, k0 * k1 * w4.dtype.itemsize)
    extra_in, out_specs, out_shape, _ = _epilogue_io(epilogue, m, tm, 0, 2)

    def body(x_ref, w_ref, *refs):
        acc = refs[-1]

        @pl.when(pl.program_id(1) == 0)
        def _():
            acc[...] = jnp.zeros_like(acc)
        acc[...] += _dot(x_ref[0].astype(BF16), w_ref[0, 0], 1, wc)

        @pl.when(pl.program_id(1) == slots - 1)
        def _():
            _epilogue_apply(epilogue, acc[...], refs[:-1], pl.program_id(0) == 0)

    return pl.pallas_call(
        body, grid=(m // tm, slots),
        in_specs=[pl.BlockSpec((1, tm, kb), lambda i, s: (s // spp, i, s % spp)), x_spec_w, *extra_in],
        out_specs=out_specs, out_shape=out_shape, scratch_shapes=[pltpu.VMEM((tm, nout), F32)],
        compiler_params=_cp(2), name=name)(x, w4, *epilogue[1], *epilogue[2])


def _mm_tn(x, dy, slots, *, scale=None, name):
    m, k = x.shape
    dy = _planes(dy)
    n_planes, _, width = dy.shape
    n = n_planes * width // slots
    spp = slots // n_planes
    ta = _tile(k, 512, LANES)
    tm = m
    while tm > 512 and tm % 2 == 0 and (2 * tm * (ta * x.dtype.itemsize + n * dy.dtype.itemsize)
                                         + 2 * ta * n * 4) > MM_BLOCK_BUDGET_BYTES:
        tm //= 2
    n_m = m // tm

    def body(x_ref, dy_ref, o_ref):
        @pl.when(pl.program_id(2) == 0)
        def _():
            o_ref[...] = jnp.zeros_like(o_ref)
        o_ref[0] += _dot(x_ref[...].astype(BF16), dy_ref[0].astype(BF16), 0, 0)
        if scale is not None:
            @pl.when(pl.program_id(2) == n_m - 1)
            def _():
                o_ref[...] = o_ref[...] * scale

    return pl.pallas_call(
        body, grid=(slots, k // ta, n_m),
        in_specs=[pl.BlockSpec((tm, ta), lambda s, a, i: (i, a)),
                  pl.BlockSpec((1, tm, n), lambda s, a, i: (s // spp, i, s % spp))],
        out_specs=pl.BlockSpec((1, ta, n), lambda s, a, i: (s, a, 0)),
        out_shape=jax.ShapeDtypeStruct((slots, k, n), F32),
        compiler_params=_cp(3), name=name)(x, dy)


def _rowwise(fn, rows, consts, outs, accs=(), *, tl=256, name):
    n_rows = rows[0].shape[0]
    tl = _tile(n_rows, tl)
    n_in = len(rows) + len(consts)
    n_out = len(outs)

    def body(*refs):
        res = fn(*[r[...] for r in refs[:n_in]])
        res = res if isinstance(res, (tuple, list)) else (res,)
        o_refs = refs[n_in:n_in + n_out]
        a_refs = refs[n_in + n_out:]
        for o, val in zip(o_refs, res[:n_out]):
            o[...] = val.astype(o.dtype)
        if a_refs:
            @pl.when(pl.program_id(0) == 0)
            def _():
                for a in a_refs:
                    a[...] = jnp.zeros_like(a)
            for a, val in zip(a_refs, res[n_out:]):
                a[...] += val

    in_specs = ([pl.BlockSpec((tl, r.shape[1]), lambda i: (i, 0)) for r in rows]
                + [pl.BlockSpec(c.shape, lambda i: (0, 0)) for c in consts])
    out_specs = ([pl.BlockSpec((tl, w), lambda i: (i, 0)) for w, _ in outs]
                 + [pl.BlockSpec(s, lambda i: (0, 0)) for s in accs])
    out_shape = ([jax.ShapeDtypeStruct((n_rows, w), dt) for w, dt in outs]
                 + [jax.ShapeDtypeStruct(s, F32) for s in accs])
    return pl.pallas_call(body, grid=(n_rows // tl,), in_specs=in_specs, out_specs=out_specs,
                          out_shape=out_shape, compiler_params=_cp(1), name=name)(*rows, *consts)


def _rms(x, g):
    return x * lax.rsqrt(jnp.mean(x * x, axis=-1, keepdims=True) + RMS_EPS) * g


def _sigmoid(x):
    return 1.0 / (1.0 + jnp.exp(-x))


def _glu(zz):
    d = zz.shape[1] // 2
    return zz[:, :d] * _sigmoid(zz[:, d:])


def _gelu(y):
    return jax.nn.gelu(y)


def _row2(v):
    return v.reshape(1, -1)


def _node_bwd_fn(counts):
    n_dy = sum(counts)

    def fn(d, hh, *rest):
        dys, gs = rest[:n_dy], rest[n_dy:]
        tot, dgs, pos = d, [], 0
        for g, cnt in zip(gs, counts):
            dy = dys[pos].astype(F32)
            for extra in dys[pos + 1:pos + cnt]:
                dy = dy + extra.astype(F32)
            pos += cnt
            _, vjp = jax.vjp(_rms, hh, g)
            dx, dg = vjp(dy)
            tot = tot + dx
            dgs.append(dg)
        return (tot, tot, *dgs)

    return fn


def _node_bwd(d_in, h, branches, *, name):
    width = h.shape[1]
    flat = [dy for _, dys in branches for dy in dys]
    res = _rowwise(_node_bwd_fn([len(dys) for _, dys in branches]), [d_in, h, *flat],
                   [_row2(g) for g, _ in branches], [(width, F32), (width, BF16)], [(1, width)] * len(branches),
                   name=name)
    return res[0], res[1], [r[0] for r in res[2:]]


def _s5_prep_fn(lr, li, ldt, br, bi, cr, ci, *, gq, h, p):
    dt = jnp.exp(ldt)
    mag = jnp.exp(lr * dt)
    lb_re = mag * jnp.cos(li * dt)
    lb_im = mag * jnp.sin(li * dt)
    den = lr * lr + li * li
    nr = lb_re - 1.0
    fr = (nr * lr + lb_im * li) / den
    fi = (lb_im * lr - nr * li) / den
    bb_re = fr * br - fi * bi
    bb_im = fr * bi + fi * br
    shape = (gq * h, gq * p)
    r = lax.broadcasted_iota(jnp.int32, shape, 0)
    c = lax.broadcasted_iota(jnp.int32, shape, 1)
    mask = jnp.where(jnp.right_shift(r, h.bit_length() - 1) == jnp.right_shift(c, p.bit_length() - 1), 1.0, 0.0)

    def expand(t):
        return jnp.concatenate([t] * gq, axis=0) * mask

    return lb_re, lb_im, expand(bb_re), expand(bb_im), expand(cr), expand(ci)


def _s5_prep(lr, li, ldt, br, bi, cr, ci, p, *, name):
    n = lr.shape[1]
    h = br.shape[0]
    gq = S5_BLOCK_GROUPS
    nq, cq = gq * p, gq * h
    nblk = n // nq
    fn = functools.partial(_s5_prep_fn, gq=gq, h=h, p=p)

    def body(lr_r, li_r, ldt_r, br_r, bi_r, cr_r, ci_r, lbr_o, lbi_o, wbr_o, wbi_o, wcr_o, wci_o):
        lb_re, lb_im, wbr, wbi, wcr, wci = fn(lr_r[...], li_r[...], ldt_r[...], br_r[...], bi_r[...],
                                              cr_r[...], ci_r[...])
        lbr_o[...] = lb_re
        lbi_o[...] = lb_im
        wbr_o[0] = wbr.astype(BF16)
        wbi_o[0] = wbi.astype(BF16)
        wcr_o[0] = wcr.astype(BF16)
        wci_o[0] = wci.astype(BF16)

    vec = pl.BlockSpec((1, nq), lambda q: (0, q))
    tab = pl.BlockSpec((h, nq), lambda q: (0, q))
    wsp = pl.BlockSpec((1, cq, nq), lambda q: (q, 0, 0))
    wsh = jax.ShapeDtypeStruct((nblk, cq, nq), BF16)
    vsh = jax.ShapeDtypeStruct((1, n), F32)
    return pl.pallas_call(body, grid=(nblk,), in_specs=[vec, vec, vec, tab, tab, tab, tab],
                          out_specs=[vec, vec, wsp, wsp, wsp, wsp], out_shape=[vsh, vsh, wsh, wsh, wsh, wsh],
                          compiler_params=_cp(1), name=name)(lr, li, ldt, br, bi, cr, ci)


def _s5_prep_bwd(lr, li, ldt, br, bi, cr, ci, p, dlbr, dlbi, dwbr, dwbi, dwcr, dwci, *, name):
    n = lr.shape[1]
    h = br.shape[0]
    gq = S5_BLOCK_GROUPS
    nq, cq = gq * p, gq * h
    nblk = n // nq
    fn = functools.partial(_s5_prep_fn, gq=gq, h=h, p=p)

    def body(lr_r, li_r, ldt_r, br_r, bi_r, cr_r, ci_r, dlbr_r, dlbi_r, dwbr_r, dwbi_r, dwcr_r, dwci_r,
             *outs):
        _, vjp = jax.vjp(fn, lr_r[...], li_r[...], ldt_r[...], br_r[...], bi_r[...], cr_r[...], ci_r[...])
        grads = vjp((dlbr_r[0], dlbi_r[0], dwbr_r[0], dwbi_r[0], dwcr_r[0], dwci_r[0]))
        for o, g in zip(outs, grads):
            o[...] = g

    vec = pl.BlockSpec((1, nq), lambda q: (0, q))
    tab = pl.BlockSpec((h, nq), lambda q: (0, q))
    vec3 = pl.BlockSpec((1, 1, nq), lambda q: (q, 0, 0))
    wsp = pl.BlockSpec((1, cq, nq), lambda q: (q, 0, 0))
    vsh = jax.ShapeDtypeStruct((1, n), F32)
    tsh = jax.ShapeDtypeStruct((h, n), F32)
    return pl.pallas_call(body, grid=(nblk,),
                          in_specs=[vec, vec, vec, tab, tab, tab, tab, vec3, vec3, wsp, wsp, wsp, wsp],
                          out_specs=[vec, vec, vec, tab, tab, tab, tab],
                          out_shape=[vsh, vsh, vsh, tsh, tsh, tsh, tsh],
                          compiler_params=_cp(1), name=name)(lr, li, ldt, br, bi, cr, ci,
                                                             dlbr, dlbi, dwbr, dwbi, dwcr, dwci)


def _scan_rows(s_re, s_im, a_re, a_im, c_re, c_im, *, reverse):
    t_rows, n = s_re.shape
    nb = t_rows // SUBLANES
    row = lax.broadcasted_iota(jnp.int32, (SUBLANES, n), 0)

    def cmul(x, y):
        return x[0] * y[0] - x[1] * y[1], x[0] * y[1] + x[1] * y[0]

    a1 = (jnp.broadcast_to(a_re, (SUBLANES, n)), jnp.broadcast_to(a_im, (SUBLANES, n)))
    a2 = cmul(a1, a1)
    a4 = cmul(a2, a2)
    steps = []
    for dist, (pr, pi) in ((1, a1), (2, a2), (4, a4)):
        keep = (row < SUBLANES - dist) if reverse else (row >= dist)
        steps.append((SUBLANES - dist if reverse else dist, (jnp.where(keep, pr, 0.0), jnp.where(keep, pi, 0.0))))
    pk = (a_re, a_im)
    tab_re = jnp.zeros((SUBLANES, n), F32)
    tab_im = jnp.zeros((SUBLANES, n), F32)
    for i in range(SUBLANES):
        at = (SUBLANES - 1 - i) if reverse else i
        tab_re = jnp.where(row == at, pk[0], tab_re)
        tab_im = jnp.where(row == at, pk[1], tab_im)
        pk = cmul(pk, (a_re, a_im))

    def step(b, carry):
        cr, ci = carry
        blk = (nb - 1 - b) if reverse else b
        off = pl.multiple_of(blk * SUBLANES, SUBLANES)
        x_re = s_re[pl.ds(off, SUBLANES), :]
        x_im = s_im[pl.ds(off, SUBLANES), :]
        for sh, (pr, pi) in steps:
            sh_re = pltpu.roll(x_re, sh, 0)
            sh_im = pltpu.roll(x_im, sh, 0)
            x_re, x_im = x_re + pr * sh_re - pi * sh_im, x_im + pr * sh_im + pi * sh_re
        x_re, x_im = x_re + tab_re * cr - tab_im * ci, x_im + tab_re * ci + tab_im * cr
        s_re[pl.ds(off, SUBLANES), :] = x_re
        s_im[pl.ds(off, SUBLANES), :] = x_im
        edge = 0 if reverse else SUBLANES - 1
        return x_re[edge:edge + 1, :], x_im[edge:edge + 1, :]

    return lax.fori_loop(0, nb, step, (c_re, c_im))


def _s5_fwd(u, prep, dskip, *, name, comm=None):
    lb_re, lb_im, wbr, wbi, wcr, wci = prep
    n_rows, _ = u.shape
    nblk, cq, nq = wbr.shape
    tt = _tile(n_rows, 512)
    nch = n_rows // tt

    def body(u_ref, wbr_r, wbi_r, wcr_r, wci_r, lbr_r, lbi_r, d_ref, y_ref, s_re, s_im, sbr_o, sbi_o, c_re, c_im):
        @pl.when(pl.program_id(1) == 0)
        def _():
            c_re[...] = jnp.zeros_like(c_re)
            c_im[...] = jnp.zeros_like(c_im)
        uf = u_ref[...]
        ub = uf.astype(BF16)
        s_re[...] = _dot(ub, wbr_r[0], 1, 0)
        s_im[...] = _dot(ub, wbi_r[0], 1, 0)
        sbr_o[0] = c_re[...]
        sbi_o[0] = c_im[...]
        cr, ci = _scan_rows(s_re, s_im, lbr_r[...], lbi_r[...], c_re[...], c_im[...], reverse=False)
        c_re[...] = cr
        c_im[...] = ci
        y = _dot(s_re[...].astype(BF16), wcr_r[0], 1, 1) - _dot(s_im[...].astype(BF16), wci_r[0], 1, 1)
        y_ref[...] = y + d_ref[...] * uf

    wsp = pl.BlockSpec((1, cq, nq), lambda q, i: (q, 0, 0))
    vec = pl.BlockSpec((1, nq), lambda q, i: (0, q))
    act = pl.BlockSpec((tt, cq), lambda q, i: (i, q))
    sb = pl.BlockSpec((1, 1, nq), lambda q, i: (i, 0, q))
    sbsh = jax.ShapeDtypeStruct((nch, 1, nblk * nq), F32)
    states = pl.BlockSpec((tt, nq), lambda q, i: (i, q))
    stsh = jax.ShapeDtypeStruct((n_rows, nblk * nq), F32)
    return _call(
        body, grid=(nblk, nch),
        in_specs=[act, wsp, wsp, wsp, wsp, vec, vec, pl.BlockSpec((1, cq), lambda q, i: (0, q))],
        out_specs=[act, states, states, sb, sb],
        out_shape=[jax.ShapeDtypeStruct(u.shape, F32), stsh, stsh, sbsh, sbsh],
        scratch_shapes=[pltpu.VMEM((1, nq), F32), pltpu.VMEM((1, nq), F32)],
        args=(u, wbr, wbi, wcr, wci, lb_re, lb_im, dskip), name=name, comm=comm)


def _s5_bwd(u, dy, st_re, st_im, sb_re, sb_im, prep, dskip, *, name, comm=None):
    lb_re, lb_im, wbr, wbi, wcr, wci = prep
    n_rows, _ = u.shape
    nblk, cq, nq = wbr.shape
    tt = _tile(n_rows, 512)
    nch = n_rows // tt

    def body(u_ref, dy_ref, s_re, s_im, sbr_r, sbi_r, wbr_r, wbi_r, wcr_r, wci_r, lbr_r, lbi_r, d_ref,
             du_ref, dwbr, dwbi, dwcr, dwci, dlbr, dlbi, dd_ref, g_re, g_im, lc_re, lc_im):
        @pl.when(pl.program_id(1) == 0)
        def _():
            for ref in (lc_re, lc_im, dwbr, dwbi, dwcr, dwci, dlbr, dlbi, dd_ref):
                ref[...] = jnp.zeros_like(ref)
        uf = u_ref[...]
        ub = uf.astype(BF16)
        dyf = dy_ref[...]
        dyb = dyf.astype(BF16)
        sr16 = s_re[...].astype(BF16)
        si16 = s_im[...].astype(BF16)
        dwcr[0] += _dot(dyb, sr16, 0, 0)
        dwci[0] -= _dot(dyb, si16, 0, 0)
        g_re[...] = _dot(dyb, wcr_r[0], 1, 0)
        g_im[...] = -_dot(dyb, wci_r[0], 1, 0)
        lcr, lci = _scan_rows(g_re, g_im, lbr_r[...], -lbi_r[...], lc_re[...], lc_im[...], reverse=True)
        lc_re[...] = lcr
        lc_im[...] = lci
        lam_r = g_re[...]
        lam_i = g_im[...]
        first = lax.broadcasted_iota(jnp.int32, (tt, nq), 0) == 0
        prev_r = jnp.where(first, sbr_r[0], pltpu.roll(s_re[...], 1, 0))
        prev_i = jnp.where(first, sbi_r[0], pltpu.roll(s_im[...], 1, 0))
        dlbr[0] += jnp.sum(lam_r * prev_r + lam_i * prev_i, axis=0, keepdims=True)
        dlbi[0] += jnp.sum(lam_i * prev_r - lam_r * prev_i, axis=0, keepdims=True)
        lr16 = lam_r.astype(BF16)
        li16 = lam_i.astype(BF16)
        du_ref[...] = _dot(lr16, wbr_r[0], 1, 1) + _dot(li16, wbi_r[0], 1, 1) + d_ref[...] * dyf
        dwbr[0] += _dot(ub, lr16, 0, 0)
        dwbi[0] += _dot(ub, li16, 0, 0)
        dd_ref[0] += jnp.sum(dyf * uf, axis=0, keepdims=True)

    last = nch - 1
    wsp = pl.BlockSpec((1, cq, nq), lambda q, i: (q, 0, 0))
    vec = pl.BlockSpec((1, nq), lambda q, i: (0, q))
    act = pl.BlockSpec((tt, cq), lambda q, i: (last - i, q))
    sb = pl.BlockSpec((1, 1, nq), lambda q, i: (last - i, 0, q))
    vec3 = pl.BlockSpec((1, 1, nq), lambda q, i: (q, 0, 0))
    dsp = pl.BlockSpec((1, 1, cq), lambda q, i: (q, 0, 0))
    wsh = jax.ShapeDtypeStruct((nblk, cq, nq), F32)
    v3sh = jax.ShapeDtypeStruct((nblk, 1, nq), F32)
    big = pltpu.VMEM((tt, nq), F32)
    states = pl.BlockSpec((tt, nq), lambda q, i: (last - i, q))
    return _call(
        body, grid=(nblk, nch),
        in_specs=[act, act, states, states, sb, sb, wsp, wsp, wsp, wsp, vec, vec,
                  pl.BlockSpec((1, cq), lambda q, i: (0, q))],
        out_specs=[act, wsp, wsp, wsp, wsp, vec3, vec3, dsp],
        out_shape=[jax.ShapeDtypeStruct(u.shape, F32), wsh, wsh, wsh, wsh, v3sh, v3sh,
                   jax.ShapeDtypeStruct((nblk, 1, cq), F32)],
        scratch_shapes=[big, big, pltpu.VMEM((1, nq), F32), pltpu.VMEM((1, nq), F32)],
        args=(u, dy, st_re, st_im, sb_re, sb_im, wbr, wbi, wcr, wci, lb_re, lb_im, dskip), name=name, comm=comm)


def _conv_taps(cur, prev, w, b):
    rid = lax.broadcasted_iota(jnp.int32, cur.shape, 0)
    x1 = jnp.where(rid == 0, prev[7:8, :], pltpu.roll(cur, 1, 0))
    x2 = jnp.where(rid == 0, prev[6:7, :], jnp.where(rid == 1, prev[7:8, :], pltpu.roll(cur, 2, 0)))
    return b + x2 * w[0:1, :] + x1 * w[1:2, :] + cur * w[2:3, :], x1, x2


def _conv_fwd(uu, cw, cb, *, name):
    n_rows, f2 = uu.shape
    f = f2 // 2
    tc = _tile(f, 1408, LANES)
    tl = _tile(n_rows, 256)
    nfb = f // tc

    def body(g_ref, u_ref, wg_ref, wu_ref, bg_ref, bu_ref, o_ref, pg, pu):
        @pl.when(pl.program_id(1) == 0)
        def _():
            pg[...] = jnp.zeros_like(pg)
            pu[...] = jnp.zeros_like(pu)
        gcur = g_ref[...]
        ucur = u_ref[...]
        cg, _, _ = _conv_taps(gcur, pg[...], wg_ref[...], bg_ref[...])
        cu, _, _ = _conv_taps(ucur, pu[...], wu_ref[...], bu_ref[...])
        o_ref[...] = (cg * _sigmoid(cg) * cu).astype(o_ref.dtype)
        pg[...] = gcur[tl - SUBLANES:, :]
        pu[...] = ucur[tl - SUBLANES:, :]

    return pl.pallas_call(
        body, grid=(nfb, n_rows // tl),
        in_specs=[pl.BlockSpec((tl, tc), lambda j, i: (i, j)), pl.BlockSpec((tl, tc), lambda j, i: (i, j + nfb)),
                  pl.BlockSpec((DT_CONV_TAPS, tc), lambda j, i: (0, j)),
                  pl.BlockSpec((DT_CONV_TAPS, tc), lambda j, i: (0, j + nfb)),
                  pl.BlockSpec((1, tc), lambda j, i: (0, j)), pl.BlockSpec((1, tc), lambda j, i: (0, j + nfb))],
        out_specs=pl.BlockSpec((tl, tc), lambda j, i: (i, j)),
        out_shape=jax.ShapeDtypeStruct((n_rows, f), BF16),
        scratch_shapes=[pltpu.VMEM((SUBLANES, tc), F32), pltpu.VMEM((SUBLANES, tc), F32)],
        compiler_params=_cp(2), name=name)(uu, uu, cw, cw, cb, cb)


def _conv_bwd(uu, dact, cw, cb, *, name):
    n_rows, f2 = uu.shape
    f = f2 // 2
    tc = _tile(f, 1408, LANES)
    tl = _tile(n_rows, 256)
    nfb = f // tc
    nrb = n_rows // tl
    halo_per_tile = tl // SUBLANES

    def body(g_ref, gh_ref, u_ref, uh_ref, da_ref, wg_ref, wu_ref, bg_ref, bu_ref,
             duu_ref, dw_ref, db_ref, nxt_g, nxt_u):
        i = pl.program_id(1)
        rb = nrb - 1 - i

        @pl.when(i == 0)
        def _():
            for ref in (nxt_g, nxt_u, dw_ref, db_ref):
                ref[...] = jnp.zeros_like(ref)
        has_prev = jnp.where(rb > 0, 1.0, 0.0)
        gcur, ucur = g_ref[...], u_ref[...]
        wg, wu = wg_ref[...], wu_ref[...]
        cg, g1, g2 = _conv_taps(gcur, gh_ref[...] * has_prev, wg, bg_ref[...])
        cu, u1, u2 = _conv_taps(ucur, uh_ref[...] * has_prev, wu, bu_ref[...])
        sg = _sigmoid(cg)
        silu = cg * sg
        da = da_ref[...]
        rid = lax.broadcasted_iota(jnp.int32, da.shape, 0)

        def transpose_conv(plane, d, cur, x1, x2, w, nxt):
            nx = nxt[...]
            d1 = jnp.where(rid == tl - 1, nx[0:1, :], pltpu.roll(d, tl - 1, 0))
            d2 = jnp.where(rid == tl - 2, nx[0:1, :],
                           jnp.where(rid == tl - 1, nx[1:2, :], pltpu.roll(d, tl - 2, 0)))
            duu_ref[plane] = (w[2:3, :] * d + w[1:2, :] * d1 + w[0:1, :] * d2).astype(duu_ref.dtype)
            nxt[...] = d[0:SUBLANES, :]
            dw_ref[plane] += jnp.concatenate([jnp.sum(d * x2, axis=0, keepdims=True),
                                              jnp.sum(d * x1, axis=0, keepdims=True),
                                              jnp.sum(d * cur, axis=0, keepdims=True)], axis=0)
            db_ref[plane] += jnp.sum(d, axis=0, keepdims=True)

        transpose_conv(0, da * cu * (sg * (1.0 + cg * (1.0 - sg))), gcur, g1, g2, wg, nxt_g)
        transpose_conv(1, da * silu, ucur, u1, u2, wu, nxt_u)

    def halo(j, i):
        return jnp.maximum((nrb - 1 - i) * halo_per_tile - 1, 0)

    return pl.pallas_call(
        body, grid=(nfb, nrb),
        in_specs=[pl.BlockSpec((tl, tc), lambda j, i: (nrb - 1 - i, j)),
                  pl.BlockSpec((SUBLANES, tc), lambda j, i: (halo(j, i), j)),
                  pl.BlockSpec((tl, tc), lambda j, i: (nrb - 1 - i, j + nfb)),
                  pl.BlockSpec((SUBLANES, tc), lambda j, i: (halo(j, i), j + nfb)),
                  pl.BlockSpec((tl, tc), lambda j, i: (nrb - 1 - i, j)),
                  pl.BlockSpec((DT_CONV_TAPS, tc), lambda j, i: (0, j)),
                  pl.BlockSpec((DT_CONV_TAPS, tc), lambda j, i: (0, j + nfb)),
                  pl.BlockSpec((1, tc), lambda j, i: (0, j)),
                  pl.BlockSpec((1, tc), lambda j, i: (0, j + nfb))],
        out_specs=[pl.BlockSpec((2, tl, tc), lambda j, i: (0, nrb - 1 - i, j)),
                   pl.BlockSpec((2, DT_CONV_TAPS, tc), lambda j, i: (0, 0, j)),
                   pl.BlockSpec((2, 1, tc), lambda j, i: (0, 0, j))],
        out_shape=[jax.ShapeDtypeStruct((2, n_rows, f), BF16), jax.ShapeDtypeStruct((2, DT_CONV_TAPS, f), F32),
                   jax.ShapeDtypeStruct((2, 1, f), F32)],
        scratch_shapes=[pltpu.VMEM((SUBLANES, tc), F32), pltpu.VMEM((SUBLANES, tc), F32)],
        compiler_params=_cp(2), name=name)(uu, uu, uu, uu, dact, cw, cw, cb, cb)


def _log_sigmoid(x):
    t = jnp.exp(-jnp.abs(x))
    log1p_t = jnp.where(t < 1e-3, t * (1.0 - t * (0.5 - t * (1.0 / 3.0))), jnp.log(1.0 + t))
    return jnp.minimum(x, 0.0) - log1p_t


def _dlog_sigmoid(x):
    t = jnp.exp(-jnp.abs(x))
    return jnp.where(x >= 0, t, 1.0) / (1.0 + t)


def _tri_dot(tri, x):
    return jnp.dot(tri, x, precision=lax.Precision.HIGHEST, preferred_element_type=F32)


def _cum_fwd(fl, bf, *, name):
    n_rows, width = fl.shape
    tc = _tile(n_rows, 256)

    def body(fl_ref, bf_ref, o_ref, carry):
        @pl.when(pl.program_id(0) == 0)
        def _():
            carry[...] = jnp.zeros_like(carry)
        x = _log_sigmoid(fl_ref[...] + bf_ref[...])
        r = lax.broadcasted_iota(jnp.int32, (tc, tc), 0)
        c = lax.broadcasted_iota(jnp.int32, (tc, tc), 1)
        y = _tri_dot(jnp.where(r >= c, 1.0, 0.0), x) + carry[...]
        o_ref[...] = y
        carry[...] = y[tc - 1:tc, :]

    return pl.pallas_call(
        body, grid=(n_rows // tc,),
        in_specs=[pl.BlockSpec((tc, width), lambda i: (i, 0)), pl.BlockSpec((1, width), lambda i: (0, 0))],
        out_specs=pl.BlockSpec((tc, width), lambda i: (i, 0)),
        out_shape=jax.ShapeDtypeStruct(fl.shape, F32),
        scratch_shapes=[pltpu.VMEM((1, width), F32)], compiler_params=_cp(1), name=name)(fl, bf)


def _cum_bwd(dcum, fl, bf, *, name):
    n_rows, width = fl.shape
    tc = _tile(n_rows, 256)
    last = n_rows // tc - 1

    def body(dc_ref, fl_ref, bf_ref, dfl_ref, dbf_ref, carry):
        @pl.when(pl.program_id(0) == 0)
        def _():
            carry[...] = jnp.zeros_like(carry)
            dbf_ref[...] = jnp.zeros_like(dbf_ref)
        r = lax.broadcasted_iota(jnp.int32, (tc, tc), 0)
        c = lax.broadcasted_iota(jnp.int32, (tc, tc), 1)
        dls = _tri_dot(jnp.where(r <= c, 1.0, 0.0), dc_ref[...]) + carry[...]
        carry[...] = dls[0:1, :]
        dfl = dls * _dlog_sigmoid(fl_ref[...] + bf_ref[...])
        dfl_ref[...] = dfl.astype(dfl_ref.dtype)
        dbf_ref[...] += jnp.sum(dfl, axis=0, keepdims=True)

    return pl.pallas_call(
        body, grid=(n_rows // tc,),
        in_specs=[pl.BlockSpec((tc, width), lambda i: (last - i, 0)),
                  pl.BlockSpec((tc, width), lambda i: (last - i, 0)),
                  pl.BlockSpec((1, width), lambda i: (0, 0))],
        out_specs=[pl.BlockSpec((tc, width), lambda i: (last - i, 0)), pl.BlockSpec((1, width), lambda i: (0, 0))],
        out_shape=[jax.ShapeDtypeStruct(fl.shape, BF16), jax.ShapeDtypeStruct((1, width), F32)],
        scratch_shapes=[pltpu.VMEM((1, width), F32)], compiler_params=_cp(1), name=name)(dcum, fl, bf)


def _head_masks():
    lane = lax.broadcasted_iota(jnp.int32, (1, LANES), 1)
    return (lane < HEAD_DIM, lane >= HEAD_DIM)


def _flash_fwd(q, kv, cum_c, cum_r, *, tq, name, comm=None):
    n_rows, d = q.shape
    nhp = d // LANES
    tk = tq
    nq = n_rows // tq
    rt = _tile(tk, FLASH_ROW_TILE)
    reps = (1, tq // LANES)

    def body(qi_ref, kj_ref, q_ref, k_ref, v_ref, cq_ref, ck_ref, ot_ref, lse_ref, m0, m1, l0, l1, acc,
             s0, s1, p0, p1, b0, b1):
        i = qi_ref[pl.program_id(1)]
        j = kj_ref[pl.program_id(1)]
        ms, ls = (m0, m1), (l0, l1)
        head_rows = lax.broadcasted_iota(jnp.int32, (LANES, 1), 0) < HEAD_DIM

        @pl.when(j == 0)
        def _():
            for h in range(2):
                ms[h][...] = jnp.full_like(ms[h], -jnp.inf)
                ls[h][...] = jnp.zeros_like(ls[h])
            acc[...] = jnp.zeros_like(acc)

        def block(diagonal):
            qv, kk, vv = q_ref[...], k_ref[...], v_ref[...]
            a = acc[...]
            for h, msk in enumerate(_head_masks()):
                st_sc, pt_sc, bias_sc = ((s0, p0, b0), (s1, p1, b1))[h]
                st_sc[...] = _dot(kk, jnp.where(msk, qv, jnp.zeros_like(qv)), 1, 1)
                bias_sc[...] = jnp.broadcast_to(cq_ref[0, h:h + 1, 0:1] - ck_ref[0, :, h:h + 1], (tk, LANES))
                m_old, l_old = ms[h][...], ls[h][...]
                col_max = jnp.full((SUBLANES, tq), -jnp.inf, F32)
                for r in range(tk // rt):
                    rows = slice(r * rt, (r + 1) * rt)
                    s = st_sc[rows, :] + jnp.tile(bias_sc[rows, :], reps)
                    if diagonal:
                        key = r * rt + lax.broadcasted_iota(jnp.int32, (rt, tq), 0)
                        qry = lax.broadcasted_iota(jnp.int32, (rt, tq), 1)
                        s = jnp.where(key <= qry, s, -jnp.inf)
                    st_sc[rows, :] = s
                    for g in range(rt // SUBLANES):
                        col_max = jnp.maximum(col_max, s[g * SUBLANES:(g + 1) * SUBLANES, :])
                m_new = jnp.maximum(m_old, jnp.max(col_max, axis=0, keepdims=True))
                col_sum = jnp.zeros((SUBLANES, tq), F32)
                for r in range(tk // rt):
                    rows = slice(r * rt, (r + 1) * rt)
                    p = jnp.exp(st_sc[rows, :] - m_new)
                    for g in range(rt // SUBLANES):
                        col_sum = col_sum + p[g * SUBLANES:(g + 1) * SUBLANES, :]
                    pt_sc[rows, :] = p.astype(BF16)
                alpha = jnp.exp(m_old - m_new)
                ms[h][...] = m_new
                ls[h][...] = alpha * l_old + jnp.sum(col_sum, axis=0, keepdims=True)
                pv_t = _dot(jnp.where(msk, vv, jnp.zeros_like(vv)), pt_sc[...], 0, 0)
                a = a * jnp.where(head_rows == (h == 0), alpha, 1.0) + pv_t
            acc[...] = a

        pl.when(j < i)(functools.partial(block, False))
        pl.when(j == i)(functools.partial(block, True))

        @pl.when(j == i)
        def _():
            ot_ref[...] = acc[...] * jnp.where(head_rows, 1.0 / l0[...], 1.0 / l1[...])
            lse_ref[0] = jnp.concatenate([m0[...] + jnp.log(l0[...]), m1[...] + jnp.log(l1[...])], axis=0)

    pairs = [(i, j) for i in range(nq) for j in range(i + 1)]
    qi = jnp.asarray([i for i, _ in pairs], jnp.int32)
    kj = jnp.asarray([j for _, j in pairs], jnp.int32)
    stat = pltpu.VMEM((1, tq), F32)
    return _call(
        body, grid=(nhp, len(pairs)), prefetch=(qi, kj),
        in_specs=[pl.BlockSpec((tq, LANES), lambda hp, t, qi, kj: (qi[t], hp)),
                  pl.BlockSpec((tk, LANES), lambda hp, t, qi, kj: (kj[t], hp)),
                  pl.BlockSpec((tk, LANES), lambda hp, t, qi, kj: (kj[t], nhp + hp)),
                  pl.BlockSpec((1, 2, tq), lambda hp, t, qi, kj: (hp, 0, qi[t])),
                  pl.BlockSpec((1, tk, 2), lambda hp, t, qi, kj: (hp, kj[t], 0))],
        out_specs=[pl.BlockSpec((LANES, tq), lambda hp, t, qi, kj: (hp, qi[t])),
                   pl.BlockSpec((1, 2, tq), lambda hp, t, qi, kj: (hp, 0, qi[t]))],
        scratch_shapes=[stat, stat, stat, stat, pltpu.VMEM((LANES, tq), F32), pltpu.VMEM((tk, tq), F32),
                        pltpu.VMEM((tk, tq), F32), pltpu.VMEM((tk, tq), BF16), pltpu.VMEM((tk, tq), BF16),
                        pltpu.VMEM((tk, LANES), F32), pltpu.VMEM((tk, LANES), F32)],
        out_shape=[jax.ShapeDtypeStruct((d, n_rows), F32), jax.ShapeDtypeStruct((nhp, 2, n_rows), F32)],
        args=(q, kv, kv, cum_r, cum_c), name=name, comm=comm)


def _head_delta(do, o, *, name):
    d = o.shape[1]

    def fn(dd, oo):
        prod = dd.astype(BF16).astype(F32) * oo
        r = lax.broadcasted_iota(jnp.int32, (d, LANES), 0)
        c = lax.broadcasted_iota(jnp.int32, (d, LANES), 1)
        return _tri_dot(prod, jnp.where(jnp.right_shift(r, HEAD_DIM.bit_length() - 1) == c, 1.0, 0.0))

    return _rowwise(fn, [do, o], [], [(LANES, F32)], name=name)[0]


def _flash_bwd(q, kv, lse_r, delta_r, do, cum_c, cum_r, *, tq, name, comm=None):
    n_rows, d = q.shape
    nhp = d // LANES
    tk = tq
    nq = n_rows // tq
    nk = n_rows // tk
    rt = _tile(tk, FLASH_ROW_TILE)
    reps = (1, tq // LANES)

    def body(qi_ref, kj_ref, q_ref, k_ref, v_ref, lse_ref, dl_ref, do_ref, cq_ref, ck_ref,
             dq_ref, dk_ref, dv_ref, dck_ref, dcq_ref, s0, s1, dp0, dp1, p0, p1, ds0, ds1, b0, b1, ck0, ck1):
        i = qi_ref[pl.program_id(1)]
        j = kj_ref[pl.program_id(1)]

        @pl.when(pl.program_id(1) == 0)
        def _():
            dq_ref[...] = jnp.zeros_like(dq_ref)
            dcq_ref[...] = jnp.zeros_like(dcq_ref)

        @pl.when(i == j)
        def _():
            for ref in (dk_ref, dv_ref, ck0, ck1):
                ref[...] = jnp.zeros_like(ref)

        def block(diagonal):
            qv, kk, vv = q_ref[...], k_ref[...], v_ref[...]
            dob = do_ref[...].astype(BF16)
            dq_acc = jnp.zeros((tq, LANES), F32)
            dk_acc = jnp.zeros((tk, LANES), F32)
            dv_acc = jnp.zeros((tk, LANES), F32)
            query_sums = []
            for h, msk in enumerate(_head_masks()):
                st_sc, dpt_sc, pt_sc, dst_sc, bias_sc, key_part = ((s0, dp0, p0, ds0, b0, ck0),
                                                                  (s1, dp1, p1, ds1, b1, ck1))[h]
                qh = jnp.where(msk, qv, jnp.zeros_like(qv))
                kh = jnp.where(msk, kk, jnp.zeros_like(kk))
                doh = jnp.where(msk, dob, jnp.zeros_like(dob))
                st_sc[...] = _dot(kk, qh, 1, 1)
                dpt_sc[...] = _dot(vv, doh, 1, 1)
                bias_sc[...] = jnp.broadcast_to(cq_ref[0, h:h + 1, 0:1] - ck_ref[0, :, h:h + 1], (tk, LANES))
                lse_row = lse_ref[0, h:h + 1, :]
                delta_row = dl_ref[0, h:h + 1, :]
                col_acc = jnp.zeros((SUBLANES, tq), F32)
                parts = []
                for r in range(tk // rt):
                    rows = slice(r * rt, (r + 1) * rt)
                    s = st_sc[rows, :] + jnp.tile(bias_sc[rows, :], reps)
                    if diagonal:
                        key = r * rt + lax.broadcasted_iota(jnp.int32, (rt, tq), 0)
                        qry = lax.broadcasted_iota(jnp.int32, (rt, tq), 1)
                        s = jnp.where(key <= qry, s, -jnp.inf)
                    p = jnp.exp(s - lse_row)
                    ds = p * (dpt_sc[rows, :] - delta_row)
                    for g in range(rt // SUBLANES):
                        col_acc = col_acc + ds[g * SUBLANES:(g + 1) * SUBLANES, :]
                    part = ds[:, 0:LANES]
                    for g in range(1, tq // LANES):
                        part = part + ds[:, g * LANES:(g + 1) * LANES]
                    parts.append(part)
                    pt_sc[rows, :] = p.astype(BF16)
                    dst_sc[rows, :] = ds.astype(BF16)
                key_part[...] += jnp.concatenate(parts, axis=0)
                query_sums.append(jnp.sum(col_acc, axis=0, keepdims=True))
                dv_acc = dv_acc + _dot(pt_sc[...], doh, 1, 0)
                dsb = dst_sc[...]
                dk_acc = dk_acc + _dot(dsb, qh, 1, 0)
                dq_acc = dq_acc + _dot(dsb, kh, 0, 0)
            off = pl.multiple_of(i * tq, tq)
            dq_ref[pl.ds(off, tq), :] += dq_acc
            dk_ref[...] += dk_acc
            dv_ref[...] += dv_acc
            dcq_ref[0, i] += jnp.concatenate(query_sums, axis=0)

        pl.when(i > j)(functools.partial(block, False))
        pl.when(i == j)(functools.partial(block, True))

        @pl.when(i == nq - 1)
        def _():
            two = lax.broadcasted_iota(jnp.int32, (tk, 2), 1)
            dck_ref[0] = jnp.where(two == 0, -jnp.sum(ck0[...], axis=1, keepdims=True),
                                   -jnp.sum(ck1[...], axis=1, keepdims=True))

    pairs = [(i, j) for j in range(nk) for i in range(j, nq)]
    qi = jnp.asarray([i for i, _ in pairs], jnp.int32)
    kj = jnp.asarray([j for _, j in pairs], jnp.int32)
    score = pltpu.VMEM((tk, tq), F32)
    score16 = pltpu.VMEM((tk, tq), BF16)
    keystat = pltpu.VMEM((tk, LANES), F32)
    return _call(
        body, grid=(nhp, len(pairs)), prefetch=(qi, kj),
        in_specs=[pl.BlockSpec((tq, LANES), lambda hp, t, qi, kj: (qi[t], hp)),
                  pl.BlockSpec((tk, LANES), lambda hp, t, qi, kj: (kj[t], hp)),
                  pl.BlockSpec((tk, LANES), lambda hp, t, qi, kj: (kj[t], nhp + hp)),
                  pl.BlockSpec((1, 2, tq), lambda hp, t, qi, kj: (hp, 0, qi[t])),
                  pl.BlockSpec((1, 2, tq), lambda hp, t, qi, kj: (hp, 0, qi[t])),
                  pl.BlockSpec((tq, LANES), lambda hp, t, qi, kj: (qi[t], hp)),
                  pl.BlockSpec((1, 2, tq), lambda hp, t, qi, kj: (hp, 0, qi[t])),
                  pl.BlockSpec((1, tk, 2), lambda hp, t, qi, kj: (hp, kj[t], 0))],
        out_specs=[pl.BlockSpec((n_rows, LANES), lambda hp, t, qi, kj: (0, hp)),
                   pl.BlockSpec((tk, LANES), lambda hp, t, qi, kj: (kj[t], hp)),
                   pl.BlockSpec((tk, LANES), lambda hp, t, qi, kj: (kj[t], hp)),
                   pl.BlockSpec((1, tk, 2), lambda hp, t, qi, kj: (hp, kj[t], 0)),
                   pl.BlockSpec((1, nq, 2, tq), lambda hp, t, qi, kj: (hp, 0, 0, 0))],
        scratch_shapes=[score, score, score, score, score16, score16, score16, score16, keystat, keystat,
                        keystat, keystat],
        out_shape=[jax.ShapeDtypeStruct((n_rows, d), F32), jax.ShapeDtypeStruct((n_rows, d), F32),
                   jax.ShapeDtypeStruct((n_rows, d), F32), jax.ShapeDtypeStruct((nhp, n_rows, 2), F32),
                   jax.ShapeDtypeStruct((nhp, nq, 2, tq), F32)],
        args=(q, kv, kv, lse_r, delta_r, do, cum_r, cum_c), name=name, comm=comm)


def _s5_tables(w, layer):
    g, p = w["lam_re"].shape[1:]
    h = w["ssm_b_re"].shape[3]
    n = g * p
    lr = w["lam_re"][layer].reshape(1, n)
    li = w["lam_im"][layer].reshape(1, n)
    ldt = jnp.broadcast_to(w["log_dt"][layer][:, None], (g, p)).reshape(1, n)
    br = w["ssm_b_re"][layer].transpose(2, 0, 1).reshape(h, n)
    bi = w["ssm_b_im"][layer].transpose(2, 0, 1).reshape(h, n)
    cr = w["ssm_c_re"][layer].transpose(1, 0, 2).reshape(h, n)
    ci = w["ssm_c_im"][layer].transpose(1, 0, 2).reshape(h, n)
    return (lr, li, ldt, br, bi, cr, ci), (g, p, h)


def _local_step(x, tgt, w, net=None, *, attn_tile=512):
    n_rows, d = x.shape
    n_layers = w["g_mix"].shape[0]
    n_s5 = w["lam_re"].shape[0]
    nh = w["b_f"].shape[0]
    nhp = nh // 2
    assert d == nh * HEAD_DIM
    tq = _tile(n_rows, attn_tile)
    g = {}
    saved = [dict() for _ in range(n_layers)]
    big = {}
    pending = {}

    def wt(name, layer):
        return w[name][layer]

    def carry_gather(group, run):
        if net is None or not net.has_group(group):
            return run(None)
        outs, got = run(net.gather_comm(group))
        net.store_gathered(group, got, w)
        return outs

    def carry_reduce(tag, run):
        if net is None or not pending:
            big.update(pending)
            pending.clear()
            return run(None)
        keys = list(pending)
        parts = net.reduce_prepare([pending[k] for k in keys], tag)
        outs, landed = run(_chip_exchange_comm(parts))
        big.update(zip(keys, net.reduce_finish(parts, landed, [pending[k] for k in keys], tag)))
        pending.clear()
        return outs

    h = x
    nxt = _rowwise(lambda a, gg: _rms(a, gg), [x], [_row2(w["g_mix"][0])], [(d, F32)], name="rms_first")[0]
    kvb = fl = cum = cq3 = ck3 = hnkv = None
    bf_pad = jnp.zeros((1, LANES), F32).at[0, :nh].set(w["b_f"])
    for l in range(n_layers):
        sv = saved[l]
        sv["h"] = h
        g_ffn = _row2(w["g_ffn"][l])
        if l < n_s5:
            tabs, (_, p, _) = _s5_tables(w, l)
            prep = _s5_prep(*tabs, p, name=f"s5_prep{l}")
            dskip = w["ssm_d"][l].reshape(1, d)
            y, st_re, st_im, sb_re, sb_im = carry_gather(
                f"stage{l}", lambda comm, u=nxt, pr=prep, ds=dskip: _s5_fwd(u, pr, ds, name=f"s5_fwd{l}", comm=comm))
            z = _rowwise(_gelu, [y], [], [(d, BF16)], name=f"gelu{l}")[0]
            zz = _mm_cols(z, *wt("w_glu", l), wc=0, name=f"glu_mm{l}")
            h1, hn2 = _rowwise(lambda hh, zq, gg: ((lambda t: (t, _rms(t, gg)))(hh + _glu(zq))),
                               [h, zz], [g_ffn], [(d, F32), (d, BF16)], name=f"mix_out{l}")
            sv.update(u=nxt, prep=prep, tabs=tabs, p=p, dskip=dskip, st_re=st_re, st_im=st_im, sb_re=sb_re,
                      sb_im=sb_im, y=y, z=z, zz=zz)
        else:
            j = l - n_s5
            qs = _mm_cols(nxt, *wt("w_q", j), wc=0, out_dtype=BF16, scale=HEAD_DIM ** -0.5, name=f"q_mm{j}")
            o_t, lse = carry_gather(
                f"stage{l}", lambda comm, q_=qs: _flash_fwd(q_, kvb, cq3, ck3, tq=tq, name=f"flash_fwd{j}", comm=comm))
            o = o_t.T
            a = _mm_cols(o, *wt("w_o", j), wc=0, name=f"o_mm{j}")
            h1, hn2 = _rowwise(lambda hh, aa, gg: ((lambda t: (t, _rms(t, gg)))(hh + aa)),
                               [h, a], [g_ffn], [(d, F32), (d, BF16)], name=f"mix_out{l}")
            sv.update(hn=nxt, qs=qs, o=o, lse=lse)
        uu = _mm_cols(hn2, *wt("w_in", l), wc=0, name=f"ffn_in{l}")
        cw, cb = w["conv_w"][l], _row2(w["conv_b"][l])
        act = _conv_fwd(uu, cw, cb, name=f"conv_fwd{l}")
        sv.update(h1=h1, hn2=hn2, uu=uu, act=act, cw=cw, cb=cb)

        def ffn_out(fn, rows, consts, outs, accs=()):
            return _mm_cols(act, *wt("w_out", l), wc=0, name=f"ffn_out{l}", epilogue=(fn, rows, consts, outs, accs))

        if l == n_layers - 1:
            def loss_fn(ff, hh, tt, gg):
                yv, vjp = jax.vjp(_rms, hh + ff, gg)
                err = yv - tt
                part = 0.5 * jnp.sum(jnp.mean(err * err, axis=-1, keepdims=True), axis=0, keepdims=True)
                dh, dg = vjp(err * (1.0 / d))
                return dh, dh, jnp.broadcast_to(part, (1, LANES)), dg
            dcur, dcur16, loss_row, dgf = ffn_out(loss_fn, [h1, tgt], [_row2(w["g_final"])],
                                                  [(d, F32), (d, BF16)], [(1, LANES), (1, d)])
            loss = loss_row[0, 0]
            g["g_final"] = dgf[0]
        elif l + 1 < n_s5:
            h, nxt = ffn_out(lambda ff, hh, gg: ((lambda t: (t, _rms(t, gg)))(hh + ff)), [h1],
                             [_row2(w["g_mix"][l + 1])], [(d, F32), (d, F32)])
        elif l + 1 == n_s5:
            h, nxt, hnkv = ffn_out(lambda ff, hh, g1, g2: ((lambda t: (t, _rms(t, g1), _rms(t, g2)))(hh + ff)), [h1],
                                   [_row2(w["g_mix"][l + 1]), _row2(w["g_kv"])],
                                   [(d, F32), (d, BF16), (d, BF16)])
            kvb = _mm_cols(hnkv, *wt("w_kv", 0), wc=0, out_dtype=BF16, name="kv_mm")
            fl = _mm_cols(hnkv, *wt("w_f", 0), wc=0, name="f_mm")
            cum = _cum_fwd(fl, bf_pad, name="cum_fwd")
            cq3 = cum[:, :nh].reshape(n_rows, nhp, 2).transpose(1, 0, 2)
            ck3 = cum[:, :nh].T.reshape(nhp, 2, n_rows)
        else:
            h, nxt = ffn_out(lambda ff, hh, gg: ((lambda t: (t, _rms(t, gg)))(hh + ff)), [h1],
                             [_row2(w["g_mix"][l + 1])], [(d, F32), (d, BF16)])

    per_layer = {k: [None] * n_layers for k in ("g_mix", "g_ffn", "conv_w", "conv_b")}
    per_s5 = {k: [None] * n_s5 for k in ("lam_re", "lam_im", "log_dt", "ssm_b_re", "ssm_b_im", "ssm_c_re",
                                         "ssm_c_im", "ssm_d")}
    dk_parts, dv_parts, dck_parts = [], [], []

    def by_row_shard(m):
        return m.reshape(N_CHIPS, m.shape[0] // N_CHIPS, m.shape[1])

    for l in reversed(range(n_layers)):
        sv = saved[l]
        dact = _mm_cols(dcur16, *wt("w_out", l), wc=1, name=f"ffn_out_dx{l}")
        pending["w_ffn_out", l] = by_row_shard(_mm_tn(sv["act"], dcur16, 1, name=f"ffn_out_dw{l}")[0])
        duu, dcw, dcb = _conv_bwd(sv["uu"], dact, sv["cw"], sv["cb"], name=f"conv_bwd{l}")
        per_layer["conv_w"][l] = jnp.concatenate([dcw[0], dcw[1]], axis=-1)
        per_layer["conv_b"][l] = jnp.concatenate([dcb[0, 0], dcb[1, 0]])
        node = _node_bwd_fn([1])
        d1, d1_16, dg = _mm_acc(duu, *wt("w_in", l), wc=1, name=f"ffn_in_dx{l}",
                                epilogue=(lambda dhn2, dd, hh, gg: node(dd, hh, dhn2, gg), [dcur, sv["h1"]],
                                          [_row2(w["g_ffn"][l])], [(d, F32), (d, BF16)], [(1, d)]))
        pending["w_ffn_in", l] = _mm_tn(sv["hn2"], duu, wt("w_in", l)[0].shape[0], name=f"ffn_in_dw{l}")
        per_layer["g_ffn"][l] = dg[0]
        if l < n_s5:
            def glu_bwd(zq, dd):
                _, vjp = jax.vjp(_glu, zq)
                return vjp(dd)[0]
            dzz = _rowwise(glu_bwd, [sv["zz"], d1], [], [(2 * d, BF16)], name=f"glu_bwd{l}")[0]
            dz = _mm_acc(dzz, *wt("w_glu", l), wc=1, name=f"glu_dx{l}")
            pending["w_glu", l] = _mm_tn(sv["z"], dzz, wt("w_glu", l)[0].shape[0], name=f"glu_dw{l}")

            def gelu_bwd(yy, dd):
                _, vjp = jax.vjp(_gelu, yy)
                return vjp(dd)[0]
            dy = _rowwise(gelu_bwd, [sv["y"], dz], [], [(d, F32)], name=f"gelu_bwd{l}")[0]
            du, dwbr, dwbi, dwcr, dwci, dlbr, dlbi, dd = carry_reduce(
                f"stage{l}", lambda comm, dy_=dy: _s5_bwd(sv["u"], dy_, sv["st_re"], sv["st_im"], sv["sb_re"],
                                                          sv["sb_im"], sv["prep"], sv["dskip"], name=f"s5_bwd{l}",
                                                          comm=comm))
            dlr, dli, dldt, dbr, dbi, dcr, dci = _s5_prep_bwd(*sv["tabs"], sv["p"], dlbr, dlbi, dwbr, dwbi, dwcr,
                                                              dwci, name=f"s5_prep_bwd{l}")
            gg, p = w["lam_re"].shape[1:]
            hh = w["ssm_b_re"].shape[3]
            per_s5["lam_re"][l] = dlr.reshape(gg, p)
            per_s5["lam_im"][l] = dli.reshape(gg, p)
            per_s5["log_dt"][l] = dldt.reshape(gg, p).sum(axis=1)
            per_s5["ssm_b_re"][l] = dbr.reshape(hh, gg, p).transpose(1, 2, 0)
            per_s5["ssm_b_im"][l] = dbi.reshape(hh, gg, p).transpose(1, 2, 0)
            per_s5["ssm_c_re"][l] = dcr.reshape(hh, gg, p).transpose(1, 0, 2)
            per_s5["ssm_c_im"][l] = dci.reshape(hh, gg, p).transpose(1, 0, 2)
            per_s5["ssm_d"][l] = dd.reshape(d)
            branches = [(w["g_mix"][l], [du])]
        else:
            j = l - n_s5
            do = _mm_cols(d1_16, *wt("w_o", j), wc=1, name=f"o_dx{j}")
            pending["w_o", j] = by_row_shard(_mm_tn(sv["o"], d1_16, 1, name=f"o_dw{j}")[0])
            delta_r = _head_delta(do, sv["o"], name=f"head_delta{j}")[:, :nh].T.reshape(nhp, 2, n_rows)
            dq, dk, dv, dck, dcq = carry_reduce(
                f"stage{l}", lambda comm, do_=do: _flash_bwd(sv["qs"], kvb, sv["lse"], delta_r, do_, cq3, ck3, tq=tq,
                                                          name=f"flash_bwd{j}", comm=comm))
            dk_parts.append(dk)
            dv_parts.append(dv)
            dck_parts.append(dck.transpose(1, 0, 2).reshape(n_rows, nh)
                             + dcq.transpose(0, 2, 1, 3).reshape(nh, n_rows).T)
            scale = HEAD_DIM ** -0.5
            dhn = _mm_cols(dq, *wt("w_q", j), wc=1, scale=scale, name=f"q_dx{j}")
            pending["w_q", j] = by_row_shard(_mm_tn(sv["hn"], dq, 1, scale=scale, name=f"q_dw{j}")[0])
            branches = [(w["g_mix"][l], [dhn])]
            if j == 0:
                def kv_sum(*parts):
                    half = len(parts) // 2
                    return jnp.concatenate([sum(parts[:half][1:], parts[0]),
                                            sum(parts[half:][1:], parts[half])], axis=1)
                dkv = _rowwise(kv_sum, dk_parts + dv_parts, [], [(2 * d, BF16)], name="dkv_sum")[0]
                dck_tot = dck_parts[0]
                for extra in dck_parts[1:]:
                    dck_tot = dck_tot + extra
                dcum = jnp.zeros((n_rows, LANES), F32).at[:, :nh].set(dck_tot)
                dfl, dbf = _cum_bwd(dcum, fl, bf_pad, name="cum_bwd")
                g["b_f"] = dbf[0, :nh]
                dhkv_a = _mm_cols(dkv, *wt("w_kv", 0), wc=1, name="kv_dx")
                dhkv_b = _mm_cols(dfl, *wt("w_f", 0), wc=1, name="f_dx")
                d_kvf = jnp.concatenate([_mm_tn(hnkv, dkv, 1, name="kv_dw")[0],
                                         _mm_tn(hnkv, dfl, 1, name="f_dw")[0][:, :nh]], axis=1)
                pending["w_kvf", 0] = d_kvf.reshape(d, N_CHIPS, -1).transpose(1, 0, 2)
                branches.append((w["g_kv"], [dhkv_a, dhkv_b]))
        dcur, dcur16, dgs = _node_bwd(d1, sv["h"], branches, name=f"mix_norm_bwd{l}")
        per_layer["g_mix"][l] = dgs[0]
        if len(dgs) > 1:
            g["g_kv"] = dgs[1]

    if pending:
        big.update(pending if net is None else
                   zip(list(pending), net.reduce_blocking([pending[k] for k in pending], "tail")))
    for k, v in (*per_layer.items(), *per_s5.items()):
        g[k] = jnp.stack(v)
    g["big"] = big
    return loss, dcur, g


def _position():
    x, y, c = lax.axis_index("x"), lax.axis_index("y"), lax.axis_index("c")
    chips = [(1 - x, y), (x, 1 - y), (1 - x, 1 - y)]
    return x, y, c, chips


def _all_gather_comm(shards):
    n = len(shards)

    def descriptors(ins, outs, sems):
        send_sems, recv_sems = sems
        x, y, c, chips = _position()
        my_slot = 2 * x + y
        sibling = (x, y, 1 - c)

        def rows(t, half):
            hr = ins[t].shape[0] // 2
            return pl.ds(half * hr, hr)

        def remote(k, t, src, dst, to):
            return pltpu.make_async_remote_copy(src_ref=src, dst_ref=dst, send_sem=send_sems.at[k, t],
                                                recv_sem=recv_sems.at[k, t], device_id=to, device_id_type=MESH)

        own = [remote(6, t, ins[t], outs[t].at[my_slot], sibling) for t in range(n)]
        ici = [remote(j, t, ins[t].at[rows(t, c)], outs[t].at[my_slot, rows(t, c)], (*chip, c))
               for j, chip in enumerate(chips) for t in range(n)]
        slots = [2 * chip[0] + chip[1] for chip in chips]
        fwd = [[remote(3 + j, t, outs[t].at[slots[j], rows(t, c)], outs[t].at[slots[j], rows(t, c)], sibling)
                for t in range(n)] for j in range(len(chips))]
        landed = [[remote(j, t, outs[t].at[slots[j], rows(t, c)], outs[t].at[slots[j], rows(t, c)], (*chips[j], c))
                   for t in range(n)] for j in range(len(chips))]
        from_sibling = [remote(3 + j, t, outs[t].at[slots[j], rows(t, 1 - c)], outs[t].at[slots[j], rows(t, 1 - c)],
                               sibling) for j in range(len(chips)) for t in range(n)]
        return own, ici, fwd, landed, from_sibling

    def start(ins, outs, sems):
        own, ici, _, _, _ = descriptors(ins, outs, sems)
        for cp in own + ici:
            cp.start()

    def finish(ins, outs, sems):
        own, ici, fwd, landed, from_sibling = descriptors(ins, outs, sems)
        for j in range(len(fwd)):
            for cp in landed[j]:
                cp.wait_recv()
            for cp in fwd[j]:
                cp.start()
        for cp in from_sibling + own:
            cp.wait_recv()
        for cp in own + ici + [cp for group in fwd for cp in group]:
            cp.wait_send()

    return _Comm(list(shards), [jax.ShapeDtypeStruct((N_CHIPS,) + a.shape, a.dtype) for a in shards],
                 [pltpu.SemaphoreType.DMA((7, n)), pltpu.SemaphoreType.DMA((7, n))], start, finish)


def _all_gather(shards, *, name):
    return _run_comm(_all_gather_comm(shards), name=name)


def _pair_exchange(grads, *, name):
    n = len(grads)

    def body(*refs):
        ins, outs = refs[:n], refs[n:2 * n]
        send_sems, recv_sems = refs[2 * n:]
        x, y, c, _ = _position()
        copies = [pltpu.make_async_remote_copy(src_ref=ins[t].at[:, 1 - c], dst_ref=outs[t],
                                               send_sem=send_sems.at[t], recv_sem=recv_sems.at[t],
                                               device_id=(x, y, 1 - c), device_id_type=MESH) for t in range(n)]
        for cp in copies:
            cp.start()
        for cp in copies:
            cp.wait()

    return pl.pallas_call(
        body, in_specs=_any_specs(n), out_specs=_any_specs(n),
        out_shape=[jax.ShapeDtypeStruct((a.shape[0],) + a.shape[2:], a.dtype) for a in grads],
        scratch_shapes=[pltpu.SemaphoreType.DMA((n,)), pltpu.SemaphoreType.DMA((n,))], name=name)(*grads)


def _chip_exchange_comm(parts):
    n = len(parts)

    def copies(ins, outs, sems):
        send_sems, recv_sems = sems
        _, _, c, chips = _position()
        return [pltpu.make_async_remote_copy(src_ref=ins[t].at[2 * chip[0] + chip[1]], dst_ref=outs[t].at[j],
                                             send_sem=send_sems.at[j, t], recv_sem=recv_sems.at[j, t],
                                             device_id=(*chip, c), device_id_type=MESH)
                for j, chip in enumerate(chips) for t in range(n)]

    def start(ins, outs, sems):
        for cp in copies(ins, outs, sems):
            cp.start()

    def finish(ins, outs, sems):
        for cp in copies(ins, outs, sems):
            cp.wait()

    return _Comm(list(parts), [jax.ShapeDtypeStruct((N_CHIPS - 1,) + a.shape[1:], a.dtype) for a in parts],
                 [pltpu.SemaphoreType.DMA((N_CHIPS - 1, n)), pltpu.SemaphoreType.DMA((N_CHIPS - 1, n))],
                 start, finish)


def _pair_share(both, *, name):
    n = len(both)

    def body(*refs):
        ins, outs = refs[:n], refs[n:2 * n]
        send_sems, recv_sems = refs[2 * n:]
        x, y, c, _ = _position()
        for t in range(n):
            pltpu.make_async_remote_copy(src_ref=ins[t].at[c], dst_ref=outs[t].at[c], send_sem=send_sems.at[t],
                                         recv_sem=recv_sems.at[t], device_id=(x, y, 1 - c),
                                         device_id_type=MESH).start()
        for t in range(n):
            pltpu.make_async_remote_copy(src_ref=ins[t].at[c], dst_ref=outs[t].at[1 - c], send_sem=send_sems.at[t],
                                         recv_sem=recv_sems.at[t], device_id=(x, y, 1 - c),
                                         device_id_type=MESH).wait()

    return pl.pallas_call(
        body, in_specs=_any_specs(n), out_specs=_any_specs(n),
        out_shape=[jax.ShapeDtypeStruct(a.shape, a.dtype) for a in both],
        input_output_aliases={t: t for t in range(n)},
        scratch_shapes=[pltpu.SemaphoreType.DMA((n,)), pltpu.SemaphoreType.DMA((n,))], name=name)(*both)


def _sum_pair(grad, landed, c, wire_dtype, *, name):
    slots, _, m, n = grad.shape
    tm = _tile(m, 256, 2 * SUBLANES)

    def body(c_ref, g_ref, l_ref, o_ref):
        o_ref[...] = (g_ref[0] + l_ref[...]).astype(wire_dtype)

    return pl.pallas_call(
        body,
        grid_spec=pltpu.PrefetchScalarGridSpec(
            num_scalar_prefetch=1, grid=(slots, m // tm),
            in_specs=[pl.BlockSpec((1, 1, tm, n), lambda s, i, c_ref: (s, c_ref[0], i, 0)),
                      pl.BlockSpec((1, tm, n), lambda s, i, c_ref: (s, i, 0))],
            out_specs=pl.BlockSpec((1, tm, n), lambda s, i, c_ref: (s, i, 0))),
        out_shape=jax.ShapeDtypeStruct((slots, m, n), wire_dtype), compiler_params=_cp(2), name=name)(
            c, grad, landed)


def _sum_chips(part, landed, slot_c, *, name):
    _, m, n = part.shape
    tm = _tile(m, 256, 2 * SUBLANES)

    def body(s_ref, p_ref, l_ref, o_ref):
        acc = p_ref[0].astype(F32)
        for j in range(N_CHIPS - 1):
            acc = acc + l_ref[j].astype(F32)
        o_ref[0] = acc

    return pl.pallas_call(
        body,
        grid_spec=pltpu.PrefetchScalarGridSpec(
            num_scalar_prefetch=1, grid=(m // tm,),
            in_specs=[pl.BlockSpec((1, tm, n), lambda i, s_ref: (s_ref[0], i, 0)),
                      pl.BlockSpec((N_CHIPS - 1, tm, n), lambda i, s_ref: (0, i, 0))],
            out_specs=pl.BlockSpec((1, tm, n), lambda i, s_ref: (s_ref[1], i, 0))),
        out_shape=jax.ShapeDtypeStruct((N_CORES, m, n), F32), compiler_params=_cp(1), name=name)(
            slot_c, part, landed)


def _reduce_prepare(grads, wire_dtypes, tag):
    c = lax.axis_index("c").reshape(1).astype(jnp.int32)
    views = []
    for a in grads:
        lead, last = a.shape[1], a.shape[-1]
        mid = 1
        for s in a.shape[2:-1]:
            mid *= s
        views.append(a.reshape(N_CHIPS, N_CORES, (lead // N_CORES) * mid, last))
    landed = _pair_exchange(views, name=f"rs_pair_exchange_{tag}")
    return [_sum_pair(v, l, c, wire_dtypes[t], name=f"rs_pair_sum_{tag}_{t}")
            for t, (v, l) in enumerate(zip(views, landed))]


def _reduce_finish(parts, landed, grads, tag):
    slot_c = jnp.stack([2 * lax.axis_index("x") + lax.axis_index("y"), lax.axis_index("c")]).astype(jnp.int32)
    both = [_sum_chips(p, l, slot_c, name=f"rs_chip_sum_{tag}_{t}") for t, (p, l) in enumerate(zip(parts, landed))]
    full = _pair_share(both, name=f"rs_pair_share_{tag}")
    return [f.reshape(a.shape[1:]) for f, a in zip(full, grads)]


def _reduce_scatter(grads, wire_dtypes, tag):
    parts = _reduce_prepare(grads, wire_dtypes, tag)
    landed = _run_comm(_chip_exchange_comm(parts), name=f"rs_chip_exchange_{tag}")
    return _reduce_finish(parts, landed, grads, tag)


class _Net:
    def __init__(self, groups, d, nh):
        self.groups, self.d, self.nh = groups, d, nh

    def has_group(self, group):
        return bool(self.groups.get(group))

    def gather_comm(self, group):
        return _all_gather_comm([shard for _, _, shard in self.groups[group]])

    def store_gathered(self, group, got, w):
        d, nh = self.d, self.nh
        for (name, layer, _), full in zip(self.groups[group], got):
            if name == "w_kvf":
                mat = full.transpose(1, 0, 2).reshape(d, -1)
                w["w_kv"][0] = (mat[:, :2 * d][None, None], 0)
                w["w_f"][0] = (jnp.zeros((d, LANES), BF16).at[:, :nh].set(mat[:, 2 * d:])[None, None], 0)
            elif name == "w_in":
                w[name][layer] = (full[:, None], 0)
            else:
                w[name][layer] = (full.reshape(1, 1, -1, full.shape[-1]), 0)

    def reduce_prepare(self, grads, tag):
        return _reduce_prepare(grads, [BF16] * len(grads), tag)

    def reduce_finish(self, parts, landed, grads, tag):
        return _reduce_finish(parts, landed, grads, tag)

    def reduce_blocking(self, grads, tag):
        return _reduce_scatter(grads, [BF16] * len(grads), tag)


def _adamw(w, g, m, v, *, name):
    def fn(ww, gg, mm, vv):
        mm = ADAM_B1 * mm + (1.0 - ADAM_B1) * gg
        vv = ADAM_B2 * vv + (1.0 - ADAM_B2) * (gg * gg)
        m_hat = mm / (1.0 - ADAM_B1 ** ADAM_STEP)
        v_hat = vv / (1.0 - ADAM_B2 ** ADAM_STEP)
        delta = -ADAM_LR * (m_hat / (jnp.sqrt(v_hat) + ADAM_EPS) + ADAM_WD * ww)
        return delta, mm, vv

    shape = w.shape
    two_d = [a.reshape(-1, shape[-1]) for a in (w, g, m, v)]
    outs = _rowwise(fn, two_d, [], [(shape[-1], F32)] * 3, name=name)
    return [o.reshape(shape) for o in outs]


def _to_bf16(a, *, name):
    two_d = a.reshape(-1, a.shape[-1])
    return _rowwise(lambda t: t, [two_d], [], [(a.shape[-1], BF16)], name=name)[0].reshape(a.shape)


def _pack(arrays, rows_multiple):
    flat = jnp.concatenate([a.reshape(-1) for a in arrays])
    rows = -(-flat.shape[0] // LANES)
    rows = -(-rows // rows_multiple) * rows_multiple
    return jnp.pad(flat, (0, rows * LANES - flat.shape[0])).reshape(rows, LANES)


def _unpack(packed, like):
    flat = packed.reshape(-1)
    out, pos = [], 0
    for a in like:
        out.append(flat[pos:pos + a.size].reshape(a.shape))
        pos += a.size
    return out


_PARAMS = ("g_mix", "g_ffn", "lam_re", "lam_im", "log_dt", "ssm_b_re", "ssm_b_im", "ssm_c_re", "ssm_c_im", "ssm_d",
           "w_glu", "g_kv", "w_kvf", "b_f", "w_q", "w_o", "w_ffn_in", "ffn_conv_w", "ffn_conv_b", "w_ffn_out",
           "g_final")
_BIG = ("w_glu", "w_kvf", "w_q", "w_o", "w_ffn_in", "w_ffn_out")
_SMALL_SHARDED = ("ssm_d", "ffn_conv_w")


def kernel(x, g_mix, g_ffn, lam_re, lam_im, log_dt, ssm_b_re, ssm_b_im, ssm_c_re, ssm_c_im, ssm_d, w_glu, g_kv, w_kvf, b_f, w_q, w_o, w_ffn_in, ffn_conv_w, ffn_conv_b, w_ffn_out, g_final, loss_target, m_g_mix, m_g_ffn, m_lam_re, m_lam_im, m_log_dt, m_ssm_b_re, m_ssm_b_im, m_ssm_c_re, m_ssm_c_im, m_ssm_d, m_w_glu, m_g_kv, m_w_kvf, m_b_f, m_w_q, m_w_o, m_w_ffn_in, m_ffn_conv_w, m_ffn_conv_b, m_w_ffn_out, m_g_final, v_g_mix, v_g_ffn, v_lam_re, v_lam_im, v_log_dt, v_ssm_b_re, v_ssm_b_im, v_ssm_c_re, v_ssm_c_im, v_ssm_d, v_w_glu, v_g_kv, v_w_kvf, v_b_f, v_w_q, v_w_o, v_w_ffn_in, v_ffn_conv_w, v_ffn_conv_b, v_w_ffn_out, v_g_final):
    p = dict(g_mix=g_mix, g_ffn=g_ffn, lam_re=lam_re, lam_im=lam_im, log_dt=log_dt, ssm_b_re=ssm_b_re,
             ssm_b_im=ssm_b_im, ssm_c_re=ssm_c_re, ssm_c_im=ssm_c_im, ssm_d=ssm_d, w_glu=w_glu, g_kv=g_kv,
             w_kvf=w_kvf, b_f=b_f, w_q=w_q, w_o=w_o, w_ffn_in=w_ffn_in, ffn_conv_w=ffn_conv_w,
             ffn_conv_b=ffn_conv_b, w_ffn_out=w_ffn_out, g_final=g_final)
    mom1 = dict(zip(_PARAMS, (m_g_mix, m_g_ffn, m_lam_re, m_lam_im, m_log_dt, m_ssm_b_re, m_ssm_b_im, m_ssm_c_re,
                              m_ssm_c_im, m_ssm_d, m_w_glu, m_g_kv, m_w_kvf, m_b_f, m_w_q, m_w_o, m_w_ffn_in,
                              m_ffn_conv_w, m_ffn_conv_b, m_w_ffn_out, m_g_final)))
    mom2 = dict(zip(_PARAMS, (v_g_mix, v_g_ffn, v_lam_re, v_lam_im, v_log_dt, v_ssm_b_re, v_ssm_b_im, v_ssm_c_re,
                              v_ssm_c_im, v_ssm_d, v_w_glu, v_g_kv, v_w_kvf, v_b_f, v_w_q, v_w_o, v_w_ffn_in,
                              v_ffn_conv_w, v_ffn_conv_b, v_w_ffn_out, v_g_final)))
    d = x.shape[-1]
    nh = b_f.shape[0]
    slot = 2 * lax.axis_index("x") + lax.axis_index("y")

    wb = {k: _to_bf16(p[k], name=f"to_bf16_{k}") for k in _BIG}
    gd, gcw, gl = _all_gather([ssm_d, ffn_conv_w, wb["w_glu"]], name="first_all_gather")
    n_lay, n_s5 = w_ffn_in.shape[0], lam_re.shape[0]
    n_fox = n_lay - n_s5
    groups = {f"stage{l}": [("w_in", l, wb["w_ffn_in"][l]), ("w_out", l, wb["w_ffn_out"][l])] for l in range(n_lay)}
    groups[f"stage{n_s5 - 1}"] += [("w_kvf", 0, wb["w_kvf"])] + [(k, j, wb[k][j]) for k in ("w_q", "w_o")
                                                                for j in range(n_fox)]
    w = dict(p)
    w.update(w_glu=[(gl, l) for l in range(n_s5)], w_in=[None] * n_lay, w_out=[None] * n_lay, w_q=[None] * n_fox,
             w_o=[None] * n_fox, w_kv=[None], w_f=[None],
             conv_w=gcw.transpose(1, 2, 0, 3).reshape(n_lay, DT_CONV_TAPS, -1), conv_b=ffn_conv_b,
             ssm_d=gd.transpose(1, 0, 2).reshape(gd.shape[1], d))

    loss_part, grad_x, g = _local_step(x[0], loss_target[0], w, _Net(groups, d, nh))
    loss = lax.psum(loss_part, ("x", "y", "c"))

    small_names = [k for k in _PARAMS if k not in _BIG]
    small_full = dict(g_mix=g["g_mix"], g_ffn=g["g_ffn"], lam_re=g["lam_re"], lam_im=g["lam_im"], log_dt=g["log_dt"],
                      ssm_b_re=g["ssm_b_re"], ssm_b_im=g["ssm_b_im"], ssm_c_re=g["ssm_c_re"], ssm_c_im=g["ssm_c_im"],
                      ssm_d=g["ssm_d"], g_kv=g["g_kv"], b_f=g["b_f"], ffn_conv_w=g["conv_w"],
                      ffn_conv_b=g["conv_b"], g_final=g["g_final"])
    small_list = [small_full[k] for k in small_names]
    pack = _pack(small_list, N_CHIPS * N_CORES * 2 * SUBLANES)
    pack4 = pack.reshape(N_CHIPS, pack.shape[0] // N_CHIPS, LANES)
    pack_shard = _reduce_scatter([pack4], [F32], "small")[0]
    red_big = {k: g["big"][k, 0] if p[k].ndim == 2 else jnp.stack([g["big"][k, l] for l in range(p[k].shape[0])])
               for k in _BIG}
    pack_all = _all_gather([pack_shard], name="small_grads_all_gather")[0]
    red_small = dict(zip(small_names, _unpack(pack_all, small_list)))
    for k in _SMALL_SHARDED:
        width = p[k].shape[-1]
        red_small[k] = lax.dynamic_slice_in_dim(red_small[k], slot * width, width, axis=red_small[k].ndim - 1)

    grads, deltas, new_m, new_v = {}, {}, {}, {}
    for k in _BIG:
        grads[k] = red_big[k]
        deltas[k], new_m[k], new_v[k] = _adamw(p[k], grads[k], mom1[k], mom2[k], name=f"adamw_{k}")
    packs = [_pack([src[k] for k in small_names], SUBLANES) for src in (p, red_small, mom1, mom2)]
    like = [p[k] for k in small_names]
    outs = [_unpack(o, like) for o in _adamw(*packs, name="adamw_small")]
    for i, k in enumerate(small_names):
        grads[k] = red_small[k]
        deltas[k], new_m[k], new_v[k] = outs[0][i], outs[1][i], outs[2][i]
    return (loss, grad_x[None], *[grads[k] for k in _PARAMS], *[deltas[k] for k in _PARAMS],
            *[new_m[k] for k in _PARAMS], *[new_v[k] for k in _PARAMS])
```

```python
import functools

import jax
import jax.numpy as jnp
from jax import lax
from jax.experimental import pallas as pl
from jax.experimental.pallas import tpu as pltpu

F32 = jnp.float32
BF16 = jnp.bfloat16

RMS_EPS = 1e-6
ADAM_LR = 0.001
ADAM_B1 = 0.9
ADAM_B2 = 0.999
ADAM_EPS = 1e-08
ADAM_WD = 0.01
ADAM_STEP = 10
DT_CONV_TAPS = 3

LANES = 128
SUBLANES = 8
HEAD_DIM = 64
FLASH_ROW_TILE = 32
S5_BLOCK_GROUPS = 16
VMEM_LIMIT_BYTES = 48 << 20
MM_BLOCK_BUDGET_BYTES = 30 << 20
N_CHIPS = 4
N_CORES = 2
MESH = pl.DeviceIdType.MESH


def _cp(n_grid):
    return pltpu.CompilerParams(dimension_semantics=("arbitrary",) * n_grid, vmem_limit_bytes=VMEM_LIMIT_BYTES)


def _tile(n, pref, mult=SUBLANES):
    if n <= pref:
        return n
    t = (pref // mult) * mult
    while t >= mult:
        if n % t == 0:
            return t
        t -= mult
    return n


class _Comm:
    def __init__(self, ins, out_shapes, sems, start, finish):
        self.ins, self.out_shapes, self.sems, self.start, self.finish = ins, out_shapes, sems, start, finish


def _any_specs(n):
    return [pl.BlockSpec(memory_space=pl.ANY)] * n


def _run_comm(comm, *, name):
    n_in, n_out = len(comm.ins), len(comm.out_shapes)

    def body(*refs):
        ins, outs, sems = refs[:n_in], refs[n_in:n_in + n_out], refs[n_in + n_out:]
        comm.start(ins, outs, sems)
        comm.finish(ins, outs, sems)

    return pl.pallas_call(body, in_specs=_any_specs(n_in), out_specs=_any_specs(n_out),
                          out_shape=list(comm.out_shapes), scratch_shapes=list(comm.sems), name=name)(*comm.ins)


def _call(body, *, grid, in_specs, out_specs, out_shape, args, name, scratch_shapes=(), prefetch=(), comm=None):
    n_pre, n_in, n_out, n_scr = len(prefetch), len(in_specs), len(out_specs), len(scratch_shapes)
    in_specs, out_specs, out_shape = list(in_specs), list(out_specs), list(out_shape)
    scratch_shapes, args = list(scratch_shapes), list(args)
    kernel_body = body
    if comm is not None:
        n_cin, n_cout = len(comm.ins), len(comm.out_shapes)

        def kernel_body(*refs):
            pos = n_pre + n_in
            c_in = refs[pos:pos + n_cin]
            main_out = refs[pos + n_cin:pos + n_cin + n_out]
            pos += n_cin + n_out
            c_out = refs[pos:pos + n_cout]
            main_scr = refs[pos + n_cout:pos + n_cout + n_scr]
            sems = refs[pos + n_cout + n_scr:]
            ids = [pl.program_id(a) for a in range(len(grid))]
            first = functools.reduce(jnp.logical_and, [i == 0 for i in ids])
            last = functools.reduce(jnp.logical_and, [i == g - 1 for i, g in zip(ids, grid)])
            pl.when(first)(lambda: comm.start(c_in, c_out, sems))
            body(*refs[:n_pre + n_in], *main_out, *main_scr)
            pl.when(last)(lambda: comm.finish(c_in, c_out, sems))

        in_specs += _any_specs(n_cin)
        out_specs += _any_specs(n_cout)
        out_shape += list(comm.out_shapes)
        scratch_shapes += list(comm.sems)
        args += list(comm.ins)
    if prefetch:
        spec = pltpu.PrefetchScalarGridSpec(num_scalar_prefetch=n_pre, grid=grid, in_specs=in_specs,
                                            out_specs=out_specs, scratch_shapes=scratch_shapes)
        res = pl.pallas_call(kernel_body, grid_spec=spec, out_shape=out_shape, compiler_params=_cp(len(grid)),
                             name=name)(*prefetch, *args)
    else:
        res = pl.pallas_call(kernel_body, grid=grid, in_specs=in_specs, out_specs=out_specs, out_shape=out_shape,
                             scratch_shapes=scratch_shapes, compiler_params=_cp(len(grid)), name=name)(*args)
    return (res[:n_out], res[n_out:]) if comm is not None else res


def _row_tile(m, bytes_per_row, fixed_bytes):
    for tm in (1024, 512):
        if m % tm == 0 and 2 * (tm * bytes_per_row + fixed_bytes) <= MM_BLOCK_BUDGET_BYTES:
            return tm
    return _tile(m, 512)


def _dot(a, b, ca, cb):
    return lax.dot_general(a, b, (((ca,), (cb,)), ((), ())), preferred_element_type=F32)


def _epilogue_io(epilogue, m, tm, rows_axis, grid_rank):
    _, rows, consts, outs, accs = epilogue

    def at_rows(width):
        return pl.BlockSpec((tm, width), lambda *g: (g[rows_axis], 0))

    def whole(shape):
        return pl.BlockSpec(shape, lambda *g: (0,) * len(shape))

    in_specs = [at_rows(r.shape[1]) for r in rows] + [whole(c.shape) for c in consts]
    out_specs = [at_rows(wd) for wd, _ in outs] + [whole(s) for s in accs]
    out_shape = ([jax.ShapeDtypeStruct((m, wd), dt) for wd, dt in outs]
                 + [jax.ShapeDtypeStruct(s, F32) for s in accs])
    bytes_per_row = (sum(r.shape[1] * r.dtype.itemsize for r in rows)
                     + sum(wd * jnp.dtype(dt).itemsize for wd, dt in outs))
    return in_specs, out_specs, out_shape, bytes_per_row


def _epilogue_apply(epilogue, block, refs, first_row_tile):
    fn, rows, consts, outs, _ = epilogue
    n_in, n_out = len(rows) + len(consts), len(outs)
    res = fn(block, *[r[...] for r in refs[:n_in]])
    for o, val in zip(refs[n_in:n_in + n_out], res[:n_out]):
        o[...] = val.astype(o.dtype)
    a_refs = refs[n_in + n_out:]
    if a_refs:
        @pl.when(first_row_tile)
        def _():
            for a in a_refs:
                a[...] = jnp.zeros_like(a)
        for a, val in zip(a_refs, res[n_out:]):
            a[...] += val


def _mm_cols(x, w4, layer, *, wc, out_dtype=F32, scale=None, epilogue=None, name):
    m, k = x.shape
    slots, _, k0, k1 = w4.shape
    nb = k1 if wc == 0 else k0
    assert (k0 if wc == 0 else k1) == k
    if epilogue is None:
        tm = _row_tile(m, k * x.dtype.itemsize + nb * jnp.dtype(out_dtype).itemsize, k0 * k1 * w4.dtype.itemsize)
        extra_in, out_specs = [], pl.BlockSpec((tm, nb), lambda s, i: (i, s))
        out_shape = jax.ShapeDtypeStruct((m, slots * nb), out_dtype)
    else:
        assert slots == 1
        bytes_per_row = _epilogue_io(epilogue, m, SUBLANES, 1, 2)[3]
        tm = _row_tile(m, k * x.dtype.itemsize + bytes_per_row, k0 * k1 * w4.dtype.itemsize)
        extra_in, out_specs, out_shape, _ = _epilogue_io(epilogue, m, tm, 1, 2)

    def body(x_ref, w_ref, *refs):
        acc = _dot(x_ref[...].astype(BF16), w_ref[0, 0], 1, wc)
        if scale is not None:
            acc = acc * scale
        if epilogue is None:
            refs[0][...] = acc.astype(out_dtype)
        else:
            _epilogue_apply(epilogue, acc, refs, pl.program_id(1) == 0)

    extra_args = [] if epilogue is None else [*epilogue[1], *epilogue[2]]
    return pl.pallas_call(
        body, grid=(slots, m // tm),
        in_specs=[pl.BlockSpec((tm, k), lambda s, i: (i, 0)),
                  pl.BlockSpec((1, 1, k0, k1), lambda s, i: (s, layer, 0, 0)), *extra_in],
        out_specs=out_specs, out_shape=out_shape,
        compiler_params=_cp(2), name=name)(x, w4, *extra_args)


def _planes(a):
    return a if a.ndim == 3 else a[None]


def _mm_acc(x, w4, layer, *, wc, epilogue=None, name):
    x = _planes(x)
    n_planes, m, width = x.shape
    slots, _, k0, k1 = w4.shape
    kb = k0 if wc == 0 else k1
    nout = k1 if wc == 0 else k0
    assert n_planes * width == slots * kb
    spp = slots // n_planes
    x_spec_w = pl.BlockSpec((1, 1, k0, k1), lambda i, s: (s, layer, 0, 0))
    if epilogue is None:
        tm = _row_tile(m, kb * x.dtype.itemsize + nout * 4, k0 * k1 * w4.dtype.itemsize)

        def body(x_ref, w_ref, o_ref):
            @pl.when(pl.program_id(1) == 0)
            def _():
                o_ref[...] = jnp.zeros_like(o_ref)
            o_ref[...] += _dot(x_ref[0].astype(BF16), w_ref[0, 0], 1, wc)

        return pl.pallas_call(
            body, grid=(m // tm, slots),
            in_specs=[pl.BlockSpec((1, tm, kb), lambda i, s: (s // spp, i, s % spp)), x_spec_w],
            out_specs=pl.BlockSpec((tm, nout), lambda i, s: (i, 0)),
            out_shape=jax.ShapeDtypeStruct((m, nout), F32),
            compiler_params=_cp(2), name=name)(x, w4)

    bytes_per_row = _epilogue_io(epilogue, m, SUBLANES, 0, 2)[3]
    tm = _row_tile(m, kb * x.dtype.itemsize + nout * 2 + bytes_per_row, k0 * k1 * w4.dtype.itemsize)
    extra_in, out_specs, out_shape, _ = _epilogue_io(epilogue, m, tm, 0, 2)

    def body(x_ref, w_ref, *refs):
        acc = refs[-1]

        @pl.when(pl.program_id(1) == 0)
        def _():
            acc[...] = jnp.zeros_like(acc)
        acc[...] += _dot(x_ref[0].astype(BF16), w_ref[0, 0], 1, wc)

        @pl.when(pl.program_id(1) == slots - 1)
        def _():
            _epilogue_apply(epilogue, acc[...], refs[:-1], pl.program_id(0) == 0)

    return pl.pallas_call(
        body, grid=(m // tm, slots),
        in_specs=[pl.BlockSpec((1, tm, kb), lambda i, s: (s // spp, i, s % spp)), x_spec_w, *extra_in],
        out_specs=out_specs, out_shape=out_shape, scratch_shapes=[pltpu.VMEM((tm, nout), F32)],
        compiler_params=_cp(2), name=name)(x, w4, *epilogue[1], *epilogue[2])


def _mm_tn(x, dy, slots, *, scale=None, name):
    m, k = x.shape
    dy = _planes(dy)
    n_planes, _, width = dy.shape
    n = n_planes * width // slots
    spp = slots // n_planes
    ta = _tile(k, 512, LANES)
    tm = m
    while tm > 512 and tm % 2 == 0 and (2 * tm * (ta * x.dtype.itemsize + n * dy.dtype.itemsize)
                                         + 2 * ta * n * 4) > MM_BLOCK_BUDGET_BYTES:
        tm //= 2
    n_m = m // tm

    def body(x_ref, dy_ref, o_ref):
        @pl.when(pl.program_id(2) == 0)
        def _():
            o_ref[...] = jnp.zeros_like(o_ref)
        o_ref[0] += _dot(x_ref[...].astype(BF16), dy_ref[0].astype(BF16), 0, 0)
        if scale is not None:
            @pl.when(pl.program_id(2) == n_m - 1)
            def _():
                o_ref[...] = o_ref[...] * scale

    return pl.pallas_call(
        body, grid=(slots, k // ta, n_m),
        in_specs=[pl.BlockSpec((tm, ta), lambda s, a, i: (i, a)),
                  pl.BlockSpec((1, tm, n), lambda s, a, i: (s // spp, i, s % spp))],
        out_specs=pl.BlockSpec((1, ta, n), lambda s, a, i: (s, a, 0)),
        out_shape=jax.ShapeDtypeStruct((slots, k, n), F32),
        compiler_params=_cp(3), name=name)(x, dy)


def _rowwise(fn, rows, consts, outs, accs=(), *, tl=256, name):
    n_rows = rows[0].shape[0]
    tl = _tile(n_rows, tl)
    n_in = len(rows) + len(consts)
    n_out = len(outs)

    def body(*refs):
        res = fn(*[r[...] for r in refs[:n_in]])
        res = res if isinstance(res, (tuple, list)) else (res,)
        o_refs = refs[n_in:n_in + n_out]
        a_refs = refs[n_in + n_out:]
        for o, val in zip(o_refs, res[:n_out]):
            o[...] = val.astype(o.dtype)
        if a_refs:
            @pl.when(pl.program_id(0) == 0)
            def _():
                for a in a_refs:
                    a[...] = jnp.zeros_like(a)
            for a, val in zip(a_refs, res[n_out:]):
                a[...] += val

    in_specs = ([pl.BlockSpec((tl, r.shape[1]), lambda i: (i, 0)) for r in rows]
                + [pl.BlockSpec(c.shape, lambda i: (0, 0)) for c in consts])
    out_specs = ([pl.BlockSpec((tl, w), lambda i: (i, 0)) for w, _ in outs]
                 + [pl.BlockSpec(s, lambda i: (0, 0)) for s in accs])
    out_shape = ([jax.ShapeDtypeStruct((n_rows, w), dt) for w, dt in outs]
                 + [jax.ShapeDtypeStruct(s, F32) for s in accs])
    return pl.pallas_call(body, grid=(n_rows // tl,), in_specs=in_specs, out_specs=out_specs,
                          out_shape=out_shape, compiler_params=_cp(1), name=name)(*rows, *consts)


def _rms(x, g):
    return x * lax.rsqrt(jnp.mean(x * x, axis=-1, keepdims=True) + RMS_EPS) * g


def _sigmoid(x):
    return 1.0 / (1.0 + jnp.exp(-x))


def _glu(zz):
    d = zz.shape[1] // 2
    return zz[:, :d] * _sigmoid(zz[:, d:])


def _gelu(y):
    return jax.nn.gelu(y)


def _row2(v):
    return v.reshape(1, -1)


def _node_bwd_fn(counts):
    n_dy = sum(counts)

    def fn(d, hh, *rest):
        dys, gs = rest[:n_dy], rest[n_dy:]
        tot, dgs, pos = d, [], 0
        for g, cnt in zip(gs, counts):
            dy = dys[pos].astype(F32)
            for extra in dys[pos + 1:pos + cnt]:
                dy = dy + extra.astype(F32)
            pos += cnt
            _, vjp = jax.vjp(_rms, hh, g)
            dx, dg = vjp(dy)
            tot = tot + dx
            dgs.append(dg)
        return (tot, tot, *dgs)

    return fn


def _node_bwd(d_in, h, branches, *, name):
    width = h.shape[1]
    flat = [dy for _, dys in branches for dy in dys]
    res = _rowwise(_node_bwd_fn([len(dys) for _, dys in branches]), [d_in, h, *flat],
                   [_row2(g) for g, _ in branches], [(width, F32), (width, BF16)], [(1, width)] * len(branches),
                   name=name)
    return res[0], res[1], [r[0] for r in res[2:]]


def _s5_prep_fn(lr, li, ldt, br, bi, cr, ci, *, gq, h, p):
    dt = jnp.exp(ldt)
    mag = jnp.exp(lr * dt)
    lb_re = mag * jnp.cos(li * dt)
    lb_im = mag * jnp.sin(li * dt)
    den = lr * lr + li * li
    nr = lb_re - 1.0
    fr = (nr * lr + lb_im * li) / den
    fi = (lb_im * lr - nr * li) / den
    bb_re = fr * br - fi * bi
    bb_im = fr * bi + fi * br
    shape = (gq * h, gq * p)
    r = lax.broadcasted_iota(jnp.int32, shape, 0)
    c = lax.broadcasted_iota(jnp.int32, shape, 1)
    mask = jnp.where(jnp.right_shift(r, h.bit_length() - 1) == jnp.right_shift(c, p.bit_length() - 1), 1.0, 0.0)

    def expand(t):
        return jnp.concatenate([t] * gq, axis=0) * mask

    return lb_re, lb_im, expand(bb_re), expand(bb_im), expand(cr), expand(ci)


def _s5_prep(lr, li, ldt, br, bi, cr, ci, p, *, name):
    n = lr.shape[1]
    h = br.shape[0]
    gq = S5_BLOCK_GROUPS
    nq, cq = gq * p, gq * h
    nblk = n // nq
    fn = functools.partial(_s5_prep_fn, gq=gq, h=h, p=p)

    def body(lr_r, li_r, ldt_r, br_r, bi_r, cr_r, ci_r, lbr_o, lbi_o, wbr_o, wbi_o, wcr_o, wci_o):
        lb_re, lb_im, wbr, wbi, wcr, wci = fn(lr_r[...], li_r[...], ldt_r[...], br_r[...], bi_r[...],
                                              cr_r[...], ci_r[...])
        lbr_o[...] = lb_re
        lbi_o[...] = lb_im
        wbr_o[0] = wbr.astype(BF16)
        wbi_o[0] = wbi.astype(BF16)
        wcr_o[0] = wcr.astype(BF16)
        wci_o[0] = wci.astype(BF16)

    vec = pl.BlockSpec((1, nq), lambda q: (0, q))
    tab = pl.BlockSpec((h, nq), lambda q: (0, q))
    wsp = pl.BlockSpec((1, cq, nq), lambda q: (q, 0, 0))
    wsh = jax.ShapeDtypeStruct((nblk, cq, nq), BF16)
    vsh = jax.ShapeDtypeStruct((1, n), F32)
    return pl.pallas_call(body, grid=(nblk,), in_specs=[vec, vec, vec, tab, tab, tab, tab],
                          out_specs=[vec, vec, wsp, wsp, wsp, wsp], out_shape=[vsh, vsh, wsh, wsh, wsh, wsh],
                          compiler_params=_cp(1), name=name)(lr, li, ldt, br, bi, cr, ci)


def _s5_prep_bwd(lr, li, ldt, br, bi, cr, ci, p, dlbr, dlbi, dwbr, dwbi, dwcr, dwci, *, name):
    n = lr.shape[1]
    h = br.shape[0]
    gq = S5_BLOCK_GROUPS
    nq, cq = gq * p, gq * h
    nblk = n // nq
    fn = functools.partial(_s5_prep_fn, gq=gq, h=h, p=p)

    def body(lr_r, li_r, ldt_r, br_r, bi_r, cr_r, ci_r, dlbr_r, dlbi_r, dwbr_r, dwbi_r, dwcr_r, dwci_r,
             *outs):
        _, vjp = jax.vjp(fn, lr_r[...], li_r[...], ldt_r[...], br_r[...], bi_r[...], cr_r[...], ci_r[...])
        grads = vjp((dlbr_r[0], dlbi_r[0], dwbr_r[0], dwbi_r[0], dwcr_r[0], dwci_r[0]))
        for o, g in zip(outs, grads):
            o[...] = g

    vec = pl.BlockSpec((1, nq), lambda q: (0, q))
    tab = pl.BlockSpec((h, nq), lambda q: (0, q))
    vec3 = pl.BlockSpec((1, 1, nq), lambda q: (q, 0, 0))
    wsp = pl.BlockSpec((1, cq, nq), lambda q: (q, 0, 0))
    vsh = jax.ShapeDtypeStruct((1, n), F32)
    tsh = jax.ShapeDtypeStruct((h, n), F32)
    return pl.pallas_call(body, grid=(nblk,),
                          in_specs=[vec, vec, vec, tab, tab, tab, tab, vec3, vec3, wsp, wsp, wsp, wsp],
                          out_specs=[vec, vec, vec, tab, tab, tab, tab],
                          out_shape=[vsh, vsh, vsh, tsh, tsh, tsh, tsh],
                          compiler_params=_cp(1), name=name)(lr, li, ldt, br, bi, cr, ci,
                                                             dlbr, dlbi, dwbr, dwbi, dwcr, dwci)


def _scan_rows(s_re, s_im, a_re, a_im, c_re, c_im, *, reverse):
    t_rows, n = s_re.shape
    nb = t_rows // SUBLANES
    row = lax.broadcasted_iota(jnp.int32, (SUBLANES, n), 0)

    def cmul(x, y):
        return x[0] * y[0] - x[1] * y[1], x[0] * y[1] + x[1] * y[0]

    a1 = (jnp.broadcast_to(a_re, (SUBLANES, n)), jnp.broadcast_to(a_im, (SUBLANES, n)))
    a2 = cmul(a1, a1)
    a4 = cmul(a2, a2)
    steps = []
    for dist, (pr, pi) in ((1, a1), (2, a2), (4, a4)):
        keep = (row < SUBLANES - dist) if reverse else (row >= dist)
        steps.append((SUBLANES - dist if reverse else dist, (jnp.where(keep, pr, 0.0), jnp.where(keep, pi, 0.0))))
    pk = (a_re, a_im)
    tab_re = jnp.zeros((SUBLANES, n), F32)
    tab_im = jnp.zeros((SUBLANES, n), F32)
    for i in range(SUBLANES):
        at = (SUBLANES - 1 - i) if reverse else i
        tab_re = jnp.where(row == at, pk[0], tab_re)
        tab_im = jnp.where(row == at, pk[1], tab_im)
        pk = cmul(pk, (a_re, a_im))

    def step(b, carry):
        cr, ci = carry
        blk = (nb - 1 - b) if reverse else b
        off = pl.multiple_of(blk * SUBLANES, SUBLANES)
        x_re = s_re[pl.ds(off, SUBLANES), :]
        x_im = s_im[pl.ds(off, SUBLANES), :]
        for sh, (pr, pi) in steps:
            sh_re = pltpu.roll(x_re, sh, 0)
            sh_im = pltpu.roll(x_im, sh, 0)
            x_re, x_im = x_re + pr * sh_re - pi * sh_im, x_im + pr * sh_im + pi * sh_re
        x_re, x_im = x_re + tab_re * cr - tab_im * ci, x_im + tab_re * ci + tab_im * cr
        s_re[pl.ds(off, SUBLANES), :] = x_re
        s_im[pl.ds(off, SUBLANES), :] = x_im
        edge = 0 if reverse else SUBLANES - 1
        return x_re[edge:edge + 1, :], x_im[edge:edge + 1, :]

    return lax.fori_loop(0, nb, step, (c_re, c_im))


def _s5_fwd(u, prep, dskip, *, name, comm=None):
    lb_re, lb_im, wbr, wbi, wcr, wci = prep
    n_rows, _ = u.shape
    nblk, cq, nq = wbr.shape
    tt = _tile(n_rows, 512)
    nch = n_rows // tt

    def body(u_ref, wbr_r, wbi_r, wcr_r, wci_r, lbr_r, lbi_r, d_ref, y_ref, s_re, s_im, sbr_o, sbi_o, c_re, c_im):
        @pl.when(pl.program_id(1) == 0)
        def _():
            c_re[...] = jnp.zeros_like(c_re)
            c_im[...] = jnp.zeros_like(c_im)
        uf = u_ref[...]
        ub = uf.astype(BF16)
        s_re[...] = _dot(ub, wbr_r[0], 1, 0)
        s_im[...] = _dot(ub, wbi_r[0], 1, 0)
        sbr_o[0] = c_re[...]
        sbi_o[0] = c_im[...]
        cr, ci = _scan_rows(s_re, s_im, lbr_r[...], lbi_r[...], c_re[...], c_im[...], reverse=False)
        c_re[...] = cr
        c_im[...] = ci
        y = _dot(s_re[...].astype(BF16), wcr_r[0], 1, 1) - _dot(s_im[...].astype(BF16), wci_r[0], 1, 1)
        y_ref[...] = y + d_ref[...] * uf

    wsp = pl.BlockSpec((1, cq, nq), lambda q, i: (q, 0, 0))
    vec = pl.BlockSpec((1, nq), lambda q, i: (0, q))
    act = pl.BlockSpec((tt, cq), lambda q, i: (i, q))
    sb = pl.BlockSpec((1, 1, nq), lambda q, i: (i, 0, q))
    sbsh = jax.ShapeDtypeStruct((nch, 1, nblk * nq), F32)
    states = pl.BlockSpec((tt, nq), lambda q, i: (i, q))
    stsh = jax.ShapeDtypeStruct((n_rows, nblk * nq), F32)
    return _call(
        body, grid=(nblk, nch),
        in_specs=[act, wsp, wsp, wsp, wsp, vec, vec, pl.BlockSpec((1, cq), lambda q, i: (0, q))],
        out_specs=[act, states, states, sb, sb],
        out_shape=[jax.ShapeDtypeStruct(u.shape, F32), stsh, stsh, sbsh, sbsh],
        scratch_shapes=[pltpu.VMEM((1, nq), F32), pltpu.VMEM((1, nq), F32)],
        args=(u, wbr, wbi, wcr, wci, lb_re, lb_im, dskip), name=name, comm=comm)


def _s5_bwd(u, dy, st_re, st_im, sb_re, sb_im, prep, dskip, *, name, comm=None):
    lb_re, lb_im, wbr, wbi, wcr, wci = prep
    n_rows, _ = u.shape
    nblk, cq, nq = wbr.shape
    tt = _tile(n_rows, 512)
    nch = n_rows // tt

    def body(u_ref, dy_ref, s_re, s_im, sbr_r, sbi_r, wbr_r, wbi_r, wcr_r, wci_r, lbr_r, lbi_r, d_ref,
             du_ref, dwbr, dwbi, dwcr, dwci, dlbr, dlbi, dd_ref, g_re, g_im, lc_re, lc_im):
        @pl.when(pl.program_id(1) == 0)
        def _():
            for ref in (lc_re, lc_im, dwbr, dwbi, dwcr, dwci, dlbr, dlbi, dd_ref):
                ref[...] = jnp.zeros_like(ref)
        uf = u_ref[...]
        ub = uf.astype(BF16)
        dyf = dy_ref[...]
        dyb = dyf.astype(BF16)
        sr16 = s_re[...].astype(BF16)
        si16 = s_im[...].astype(BF16)
        dwcr[0] += _dot(dyb, sr16, 0, 0)
        dwci[0] -= _dot(dyb, si16, 0, 0)
        g_re[...] = _dot(dyb, wcr_r[0], 1, 0)
        g_im[...] = -_dot(dyb, wci_r[0], 1, 0)
        lcr, lci = _scan_rows(g_re, g_im, lbr_r[...], -lbi_r[...], lc_re[...], lc_im[...], reverse=True)
        lc_re[...] = lcr
        lc_im[...] = lci
        lam_r = g_re[...]
        lam_i = g_im[...]
        first = lax.broadcasted_iota(jnp.int32, (tt, nq), 0) == 0
        prev_r = jnp.where(first, sbr_r[0], pltpu.roll(s_re[...], 1, 0))
        prev_i = jnp.where(first, sbi_r[0], pltpu.roll(s_im[...], 1, 0))
        dlbr[0] += jnp.sum(lam_r * prev_r + lam_i * prev_i, axis=0, keepdims=True)
        dlbi[0] += jnp.sum(lam_i * prev_r - lam_r * prev_i, axis=0, keepdims=True)
        lr16 = lam_r.astype(BF16)
        li16 = lam_i.astype(BF16)
        du_ref[...] = _dot(lr16, wbr_r[0], 1, 1) + _dot(li16, wbi_r[0], 1, 1) + d_ref[...] * dyf
        dwbr[0] += _dot(ub, lr16, 0, 0)
        dwbi[0] += _dot(ub, li16, 0, 0)
        dd_ref[0] += jnp.sum(dyf * uf, axis=0, keepdims=True)

    last = nch - 1
    wsp = pl.BlockSpec((1, cq, nq), lambda q, i: (q, 0, 0))
    vec = pl.BlockSpec((1, nq), lambda q, i: (0, q))
    act = pl.BlockSpec((tt, cq), lambda q, i: (last - i, q))
    sb = pl.BlockSpec((1, 1, nq), lambda q, i: (last - i, 0, q))
    vec3 = pl.BlockSpec((1, 1, nq), lambda q, i: (q, 0, 0))
    dsp = pl.BlockSpec((1, 1, cq), lambda q, i: (q, 0, 0))
    wsh = jax.ShapeDtypeStruct((nblk, cq, nq), F32)
    v3sh = jax.ShapeDtypeStruct((nblk, 1, nq), F32)
    big = pltpu.VMEM((tt, nq), F32)
    states = pl.BlockSpec((tt, nq), lambda q, i: (last - i, q))
    return _call(
        body, grid=(nblk, nch),
        in_specs=[act, act, states, states, sb, sb, wsp, wsp, wsp, wsp, vec, vec,
                  pl.BlockSpec((1, cq), lambda q, i: (0, q))],
        out_specs=[act, wsp, wsp, wsp, wsp, vec3, vec3, dsp],
        out_shape=[jax.ShapeDtypeStruct(u.shape, F32), wsh, wsh, wsh, wsh, v3sh, v3sh,
                   jax.ShapeDtypeStruct((nblk, 1, cq), F32)],
        scratch_shapes=[big, big, pltpu.VMEM((1, nq), F32), pltpu.VMEM((1, nq), F32)],
        args=(u, dy, st_re, st_im, sb_re, sb_im, wbr, wbi, wcr, wci, lb_re, lb_im, dskip), name=name, comm=comm)


def _conv_taps(cur, prev, w, b):
    ext = jnp.concatenate([prev, cur], axis=0)
    x1 = pltpu.roll(ext, 1, 0)[SUBLANES:, :]
    x2 = pltpu.roll(ext, 2, 0)[SUBLANES:, :]
    return b + x2 * w[0:1, :] + x1 * w[1:2, :] + cur * w[2:3, :], x1, x2


def _conv_fwd(uu, cw, cb, *, name):
    n_rows, f2 = uu.shape
    f = f2 // 2
    tc = _tile(f, 1408, LANES)
    tl = _tile(n_rows, 256)
    nfb = f // tc

    def body(g_ref, u_ref, wg_ref, wu_ref, bg_ref, bu_ref, o_ref, pg, pu):
        @pl.when(pl.program_id(1) == 0)
        def _():
            pg[...] = jnp.zeros_like(pg)
            pu[...] = jnp.zeros_like(pu)
        gcur = g_ref[...]
        ucur = u_ref[...]
        cg, _, _ = _conv_taps(gcur, pg[...], wg_ref[...], bg_ref[...])
        cu, _, _ = _conv_taps(ucur, pu[...], wu_ref[...], bu_ref[...])
        o_ref[...] = (cg * _sigmoid(cg) * cu).astype(o_ref.dtype)
        pg[...] = gcur[tl - SUBLANES:, :]
        pu[...] = ucur[tl - SUBLANES:, :]

    return pl.pallas_call(
        body, grid=(nfb, n_rows // tl),
        in_specs=[pl.BlockSpec((tl, tc), lambda j, i: (i, j)), pl.BlockSpec((tl, tc), lambda j, i: (i, j + nfb)),
                  pl.BlockSpec((DT_CONV_TAPS, tc), lambda j, i: (0, j)),
                  pl.BlockSpec((DT_CONV_TAPS, tc), lambda j, i: (0, j + nfb)),
                  pl.BlockSpec((1, tc), lambda j, i: (0, j)), pl.BlockSpec((1, tc), lambda j, i: (0, j + nfb))],
        out_specs=pl.BlockSpec((tl, tc), lambda j, i: (i, j)),
        out_shape=jax.ShapeDtypeStruct((n_rows, f), BF16),
        scratch_shapes=[pltpu.VMEM((SUBLANES, tc), F32), pltpu.VMEM((SUBLANES, tc), F32)],
        compiler_params=_cp(2), name=name)(uu, uu, cw, cw, cb, cb)


def _conv_bwd(uu, dact, cw, cb, *, name):
    n_rows, f2 = uu.shape
    f = f2 // 2
    tc = _tile(f, 1408, LANES)
    tl = _tile(n_rows, 256)
    nfb = f // tc
    nrb = n_rows // tl
    halo_per_tile = tl // SUBLANES

    def body(g_ref, gh_ref, u_ref, uh_ref, da_ref, wg_ref, wu_ref, bg_ref, bu_ref,
             duu_ref, dw_ref, db_ref, nxt_g, nxt_u):
        i = pl.program_id(1)
        rb = nrb - 1 - i

        @pl.when(i == 0)
        def _():
            for ref in (nxt_g, nxt_u, dw_ref, db_ref):
                ref[...] = jnp.zeros_like(ref)
        has_prev = jnp.where(rb > 0, 1.0, 0.0)
        gcur, ucur = g_ref[...], u_ref[...]
        wg, wu = wg_ref[...], wu_ref[...]
        cg, g1, g2 = _conv_taps(gcur, gh_ref[...] * has_prev, wg, bg_ref[...])
        cu, u1, u2 = _conv_taps(ucur, uh_ref[...] * has_prev, wu, bu_ref[...])
        sg = _sigmoid(cg)
        silu = cg * sg
        da = da_ref[...]

        def transpose_conv(plane, d, cur, x1, x2, w, nxt):
            ext = jnp.concatenate([d, nxt[...]], axis=0)
            d1 = pltpu.roll(ext, tl + SUBLANES - 1, 0)[:tl, :]
            d2 = pltpu.roll(ext, tl + SUBLANES - 2, 0)[:tl, :]
            duu_ref[plane] = (w[2:3, :] * d + w[1:2, :] * d1 + w[0:1, :] * d2).astype(duu_ref.dtype)
            nxt[...] = d[0:SUBLANES, :]
            dw_ref[plane] += jnp.concatenate([jnp.sum(d * x2, axis=0, keepdims=True),
                                              jnp.sum(d * x1, axis=0, keepdims=True),
                                              jnp.sum(d * cur, axis=0, keepdims=True)], axis=0)
            db_ref[plane] += jnp.sum(d, axis=0, keepdims=True)

        transpose_conv(0, da * cu * (sg * (1.0 + cg * (1.0 - sg))), gcur, g1, g2, wg, nxt_g)
        transpose_conv(1, da * silu, ucur, u1, u2, wu, nxt_u)

    def halo(j, i):
        return jnp.maximum((nrb - 1 - i) * halo_per_tile - 1, 0)

    return pl.pallas_call(
        body, grid=(nfb, nrb),
        in_specs=[pl.BlockSpec((tl, tc), lambda j, i: (nrb - 1 - i, j)),
                  pl.BlockSpec((SUBLANES, tc), lambda j, i: (halo(j, i), j)),
                  pl.BlockSpec((tl, tc), lambda j, i: (nrb - 1 - i, j + nfb)),
                  pl.BlockSpec((SUBLANES, tc), lambda j, i: (halo(j, i), j + nfb)),
                  pl.BlockSpec((tl, tc), lambda j, i: (nrb - 1 - i, j)),
                  pl.BlockSpec((DT_CONV_TAPS, tc), lambda j, i: (0, j)),
                  pl.BlockSpec((DT_CONV_TAPS, tc), lambda j, i: (0, j + nfb)),
                  pl.BlockSpec((1, tc), lambda j, i: (0, j)),
                  pl.BlockSpec((1, tc), lambda j, i: (0, j + nfb))],
        out_specs=[pl.BlockSpec((2, tl, tc), lambda j, i: (0, nrb - 1 - i, j)),
                   pl.BlockSpec((2, DT_CONV_TAPS, tc), lambda j, i: (0, 0, j)),
                   pl.BlockSpec((2, 1, tc), lambda j, i: (0, 0, j))],
        out_shape=[jax.ShapeDtypeStruct((2, n_rows, f), BF16), jax.ShapeDtypeStruct((2, DT_CONV_TAPS, f), F32),
                   jax.ShapeDtypeStruct((2, 1, f), F32)],
        scratch_shapes=[pltpu.VMEM((SUBLANES, tc), F32), pltpu.VMEM((SUBLANES, tc), F32)],
        compiler_params=_cp(2), name=name)(uu, uu, uu, uu, dact, cw, cw, cb, cb)


def _log_sigmoid(x):
    t = jnp.exp(-jnp.abs(x))
    log1p_t = jnp.where(t < 1e-3, t * (1.0 - t * (0.5 - t * (1.0 / 3.0))), jnp.log(1.0 + t))
    return jnp.minimum(x, 0.0) - log1p_t


def _dlog_sigmoid(x):
    t = jnp.exp(-jnp.abs(x))
    return jnp.where(x >= 0, t, 1.0) / (1.0 + t)


def _tri_dot(tri, x):
    return jnp.dot(tri, x, precision=lax.Precision.HIGHEST, preferred_element_type=F32)


def _cum_fwd(fl, bf, *, name):
    n_rows, width = fl.shape
    tc = _tile(n_rows, 256)

    def body(fl_ref, bf_ref, o_ref, carry):
        @pl.when(pl.program_id(0) == 0)
        def _():
            carry[...] = jnp.zeros_like(carry)
        x = _log_sigmoid(fl_ref[...] + bf_ref[...])
        r = lax.broadcasted_iota(jnp.int32, (tc, tc), 0)
        c = lax.broadcasted_iota(jnp.int32, (tc, tc), 1)
        y = _tri_dot(jnp.where(r >= c, 1.0, 0.0), x) + carry[...]
        o_ref[...] = y
        carry[...] = y[tc - 1:tc, :]

    return pl.pallas_call(
        body, grid=(n_rows // tc,),
        in_specs=[pl.BlockSpec((tc, width), lambda i: (i, 0)), pl.BlockSpec((1, width), lambda i: (0, 0))],
        out_specs=pl.BlockSpec((tc, width), lambda i: (i, 0)),
        out_shape=jax.ShapeDtypeStruct(fl.shape, F32),
        scratch_shapes=[pltpu.VMEM((1, width), F32)], compiler_params=_cp(1), name=name)(fl, bf)


def _cum_bwd(dcum, fl, bf, *, name):
    n_rows, width = fl.shape
    tc = _tile(n_rows, 256)
    last = n_rows // tc - 1

    def body(dc_ref, fl_ref, bf_ref, dfl_ref, dbf_ref, carry):
        @pl.when(pl.program_id(0) == 0)
        def _():
            carry[...] = jnp.zeros_like(carry)
            dbf_ref[...] = jnp.zeros_like(dbf_ref)
        r = lax.broadcasted_iota(jnp.int32, (tc, tc), 0)
        c = lax.broadcasted_iota(jnp.int32, (tc, tc), 1)
        dls = _tri_dot(jnp.where(r <= c, 1.0, 0.0), dc_ref[...]) + carry[...]
        carry[...] = dls[0:1, :]
        dfl = dls * _dlog_sigmoid(fl_ref[...] + bf_ref[...])
        dfl_ref[...] = dfl.astype(dfl_ref.dtype)
        dbf_ref[...] += jnp.sum(dfl, axis=0, keepdims=True)

    return pl.pallas_call(
        body, grid=(n_rows // tc,),
        in_specs=[pl.BlockSpec((tc, width), lambda i: (last - i, 0)),
                  pl.BlockSpec((tc, width), lambda i: (last - i, 0)),
                  pl.BlockSpec((1, width), lambda i: (0, 0))],
        out_specs=[pl.BlockSpec((tc, width), lambda i: (last - i, 0)), pl.BlockSpec((1, width), lambda i: (0, 0))],
        out_shape=[jax.ShapeDtypeStruct(fl.shape, BF16), jax.ShapeDtypeStruct((1, width), F32)],
        scratch_shapes=[pltpu.VMEM((1, width), F32)], compiler_params=_cp(1), name=name)(dcum, fl, bf)


def _head_masks():
    lane = lax.broadcasted_iota(jnp.int32, (1, LANES), 1)
    return (lane < HEAD_DIM, lane >= HEAD_DIM)


def _flash_fwd(q, kv, cum_c, cum_r, *, tq, name, comm=None):
    n_rows, d = q.shape
    nhp = d // LANES
    tk = tq
    nq = n_rows // tq
    rt = _tile(tk, FLASH_ROW_TILE)
    reps = (1, tq // LANES)

    def body(qi_ref, kj_ref, q_ref, k_ref, v_ref, cq_ref, ck_ref, ot_ref, lse_ref, m0, m1, l0, l1, acc,
             s0, s1, p0, p1, b0, b1):
        i = qi_ref[pl.program_id(1)]
        j = kj_ref[pl.program_id(1)]
        ms, ls = (m0, m1), (l0, l1)
        head_rows = lax.broadcasted_iota(jnp.int32, (LANES, 1), 0) < HEAD_DIM

        @pl.when(j == 0)
        def _():
            for h in range(2):
                ms[h][...] = jnp.full_like(ms[h], -jnp.inf)
                ls[h][...] = jnp.zeros_like(ls[h])
            acc[...] = jnp.zeros_like(acc)

        def block(diagonal):
            qv, kk, vv = q_ref[...], k_ref[...], v_ref[...]
            a = acc[...]
            for h, msk in enumerate(_head_masks()):
                st_sc, pt_sc, bias_sc = ((s0, p0, b0), (s1, p1, b1))[h]
                st_sc[...] = _dot(kk, jnp.where(msk, qv, jnp.zeros_like(qv)), 1, 1)
                bias_sc[...] = jnp.broadcast_to(cq_ref[0, h:h + 1, 0:1] - ck_ref[0, :, h:h + 1], (tk, LANES))
                m_old, l_old = ms[h][...], ls[h][...]
                col_max = jnp.full((SUBLANES, tq), -jnp.inf, F32)
                for r in range(tk // rt):
                    rows = slice(r * rt, (r + 1) * rt)
                    s = st_sc[rows, :] + jnp.tile(bias_sc[rows, :], reps)
                    if diagonal:
                        key = r * rt + lax.broadcasted_iota(jnp.int32, (rt, tq), 0)
                        qry = lax.broadcasted_iota(jnp.int32, (rt, tq), 1)
                        s = jnp.where(key <= qry, s, -jnp.inf)
                    st_sc[rows, :] = s
                    for g in range(rt // SUBLANES):
                        col_max = jnp.maximum(col_max, s[g * SUBLANES:(g + 1) * SUBLANES, :])
                m_new = jnp.maximum(m_old, jnp.max(col_max, axis=0, keepdims=True))
                col_sum = jnp.zeros((SUBLANES, tq), F32)
                for r in range(tk // rt):
                    rows = slice(r * rt, (r + 1) * rt)
                    p = jnp.exp(st_sc[rows, :] - m_new)
                    for g in range(rt // SUBLANES):
                        col_sum = col_sum + p[g * SUBLANES:(g + 1) * SUBLANES, :]
                    pt_sc[rows, :] = p.astype(BF16)
                alpha = jnp.exp(m_old - m_new)
                ms[h][...] = m_new
                ls[h][...] = alpha * l_old + jnp.sum(col_sum, axis=0, keepdims=True)
                pv_t = _dot(jnp.where(msk, vv, jnp.zeros_like(vv)), pt_sc[...], 0, 0)
                a = a * jnp.where(head_rows == (h == 0), alpha, 1.0) + pv_t
            acc[...] = a

        pl.when(j < i)(functools.partial(block, False))
        pl.when(j == i)(functools.partial(block, True))

        @pl.when(j == i)
        def _():
            ot_ref[...] = acc[...] * jnp.where(head_rows, 1.0 / l0[...], 1.0 / l1[...])
            lse_ref[0] = jnp.concatenate([m0[...] + jnp.log(l0[...]), m1[...] + jnp.log(l1[...])], axis=0)

    pairs = [(i, j) for i in range(nq) for j in range(i + 1)]
    qi = jnp.asarray([i for i, _ in pairs], jnp.int32)
    kj = jnp.asarray([j for _, j in pairs], jnp.int32)
    stat = pltpu.VMEM((1, tq), F32)
    return _call(
        body, grid=(nhp, len(pairs)), prefetch=(qi, kj),
        in_specs=[pl.BlockSpec((tq, LANES), lambda hp, t, qi, kj: (qi[t], hp)),
                  pl.BlockSpec((tk, LANES), lambda hp, t, qi, kj: (kj[t], hp)),
                  pl.BlockSpec((tk, LANES), lambda hp, t, qi, kj: (kj[t], nhp + hp)),
                  pl.BlockSpec((1, 2, tq), lambda hp, t, qi, kj: (hp, 0, qi[t])),
                  pl.BlockSpec((1, tk, 2), lambda hp, t, qi, kj: (hp, kj[t], 0))],
        out_specs=[pl.BlockSpec((LANES, tq), lambda hp, t, qi, kj: (hp, qi[t])),
                   pl.BlockSpec((1, 2, tq), lambda hp, t, qi, kj: (hp, 0, qi[t]))],
        scratch_shapes=[stat, stat, stat, stat, pltpu.VMEM((LANES, tq), F32), pltpu.VMEM((tk, tq), F32),
                        pltpu.VMEM((tk, tq), F32), pltpu.VMEM((tk, tq), BF16), pltpu.VMEM((tk, tq), BF16),
                        pltpu.VMEM((tk, LANES), F32), pltpu.VMEM((tk, LANES), F32)],
        out_shape=[jax.ShapeDtypeStruct((d, n_rows), F32), jax.ShapeDtypeStruct((nhp, 2, n_rows), F32)],
        args=(q, kv, kv, cum_r, cum_c), name=name, comm=comm)


def _head_delta(do, o, *, name):
    d = o.shape[1]

    def fn(dd, oo):
        prod = dd.astype(BF16).astype(F32) * oo
        r = lax.broadcasted_iota(jnp.int32, (d, LANES), 0)
        c = lax.broadcasted_iota(jnp.int32, (d, LANES), 1)
        return _tri_dot(prod, jnp.where(jnp.right_shift(r, HEAD_DIM.bit_length() - 1) == c, 1.0, 0.0))

    return _rowwise(fn, [do, o], [], [(LANES, F32)], name=name)[0]


def _flash_bwd(q, kv, lse_r, delta_r, do, cum_c, cum_r, *, tq, name, comm=None):
    n_rows, d = q.shape
    nhp = d // LANES
    tk = tq
    nq = n_rows // tq
    nk = n_rows // tk
    rt = _tile(tk, FLASH_ROW_TILE)
    reps = (1, tq // LANES)

    def body(qi_ref, kj_ref, q_ref, k_ref, v_ref, lse_ref, dl_ref, do_ref, cq_ref, ck_ref,
             dq_ref, dk_ref, dv_ref, dck_ref, dcq_ref, s0, dp0, p0, ds0, b0, b1, ck0, ck1):
        s1, dp1, p1, ds1 = s0, dp0, p0, ds0
        i = qi_ref[pl.program_id(1)]
        j = kj_ref[pl.program_id(1)]

        @pl.when(pl.program_id(1) == 0)
        def _():
            dq_ref[...] = jnp.zeros_like(dq_ref)
            dcq_ref[...] = jnp.zeros_like(dcq_ref)

        @pl.when(i == j)
        def _():
            for ref in (dk_ref, dv_ref, ck0, ck1):
                ref[...] = jnp.zeros_like(ref)

        def block(diagonal):
            qv, kk, vv = q_ref[...], k_ref[...], v_ref[...]
            dob = do_ref[...].astype(BF16)
            dq_acc = jnp.zeros((tq, LANES), F32)
            dk_acc = jnp.zeros((tk, LANES), F32)
            dv_acc = jnp.zeros((tk, LANES), F32)
            query_sums = []
            for h, msk in enumerate(_head_masks()):
                st_sc, dpt_sc, pt_sc, dst_sc, bias_sc, key_part = ((s0, dp0, p0, ds0, b0, ck0),
                                                                  (s1, dp1, p1, ds1, b1, ck1))[h]
                qh = jnp.where(msk, qv, jnp.zeros_like(qv))
                kh = jnp.where(msk, kk, jnp.zeros_like(kk))
                doh = jnp.where(msk, dob, jnp.zeros_like(dob))
                st_sc[...] = _dot(kk, qh, 1, 1)
                dpt_sc[...] = _dot(vv, doh, 1, 1)
                bias_sc[...] = jnp.broadcast_to(cq_ref[0, h:h + 1, 0:1] - ck_ref[0, :, h:h + 1], (tk, LANES))
                lse_row = lse_ref[0, h:h + 1, :]
                delta_row = dl_ref[0, h:h + 1, :]
                col_acc = jnp.zeros((SUBLANES, tq), F32)
                parts = []
                for r in range(tk // rt):
                    rows = slice(r * rt, (r + 1) * rt)
                    s = st_sc[rows, :] + jnp.tile(bias_sc[rows, :], reps)
                    if diagonal:
                        key = r * rt + lax.broadcasted_iota(jnp.int32, (rt, tq), 0)
                        qry = lax.broadcasted_iota(jnp.int32, (rt, tq), 1)
                        s = jnp.where(key <= qry, s, -jnp.inf)
                    p = jnp.exp(s - lse_row)
                    ds = p * (dpt_sc[rows, :] - delta_row)
                    for g in range(rt // SUBLANES):
                        col_acc = col_acc + ds[g * SUBLANES:(g + 1) * SUBLANES, :]
                    part = ds[:, 0:LANES]
                    for g in range(1, tq // LANES):
                        part = part + ds[:, g * LANES:(g + 1) * LANES]
                    parts.append(part)
                    pt_sc[rows, :] = p.astype(BF16)
                    dst_sc[rows, :] = ds.astype(BF16)
                key_part[...] += jnp.concatenate(parts, axis=0)
                query_sums.append(jnp.sum(col_acc, axis=0, keepdims=True))
                dv_acc = dv_acc + _dot(pt_sc[...], doh, 1, 0)
                dsb = dst_sc[...]
                dk_acc = dk_acc + _dot(dsb, qh, 1, 0)
                dq_acc = dq_acc + _dot(dsb, kh, 0, 0)
            off = pl.multiple_of(i * tq, tq)
            dq_ref[pl.ds(off, tq), :] += dq_acc
            dk_ref[...] += dk_acc
            dv_ref[...] += dv_acc
            dcq_ref[0, i] += jnp.concatenate(query_sums, axis=0)

        pl.when(i > j)(functools.partial(block, False))
        pl.when(i == j)(functools.partial(block, True))

        @pl.when(i == nq - 1)
        def _():
            two = lax.broadcasted_iota(jnp.int32, (tk, 2), 1)
            dck_ref[0] = jnp.where(two == 0, -jnp.sum(ck0[...], axis=1, keepdims=True),
                                   -jnp.sum(ck1[...], axis=1, keepdims=True))

    pairs = [(i, j) for j in range(nk) for i in range(j, nq)]
    qi = jnp.asarray([i for i, _ in pairs], jnp.int32)
    kj = jnp.asarray([j for _, j in pairs], jnp.int32)
    score = pltpu.VMEM((tk, tq), F32)
    score16 = pltpu.VMEM((tk, tq), BF16)
    keystat = pltpu.VMEM((tk, LANES), F32)
    return _call(
        body, grid=(nhp, len(pairs)), prefetch=(qi, kj),
        in_specs=[pl.BlockSpec((tq, LANES), lambda hp, t, qi, kj: (qi[t], hp)),
                  pl.BlockSpec((tk, LANES), lambda hp, t, qi, kj: (kj[t], hp)),
                  pl.BlockSpec((tk, LANES), lambda hp, t, qi, kj: (kj[t], nhp + hp)),
                  pl.BlockSpec((1, 2, tq), lambda hp, t, qi, kj: (hp, 0, qi[t])),
                  pl.BlockSpec((1, 2, tq), lambda hp, t, qi, kj: (hp, 0, qi[t])),
                  pl.BlockSpec((tq, LANES), lambda hp, t, qi, kj: (qi[t], hp)),
                  pl.BlockSpec((1, 2, tq), lambda hp, t, qi, kj: (hp, 0, qi[t])),
                  pl.BlockSpec((1, tk, 2), lambda hp, t, qi, kj: (hp, kj[t], 0))],
        out_specs=[pl.BlockSpec((n_rows, LANES), lambda hp, t, qi, kj: (0, hp)),
                   pl.BlockSpec((tk, LANES), lambda hp, t, qi, kj: (kj[t], hp)),
                   pl.BlockSpec((tk, LANES), lambda hp, t, qi, kj: (kj[t], hp)),
                   pl.BlockSpec((1, tk, 2), lambda hp, t, qi, kj: (hp, kj[t], 0)),
                   pl.BlockSpec((1, nq, 2, tq), lambda hp, t, qi, kj: (hp, 0, 0, 0))],
        scratch_shapes=[score, score, score16, score16, keystat, keystat, keystat, keystat],
        out_shape=[jax.ShapeDtypeStruct((n_rows, d), F32), jax.ShapeDtypeStruct((n_rows, d), F32),
                   jax.ShapeDtypeStruct((n_rows, d), F32), jax.ShapeDtypeStruct((nhp, n_rows, 2), F32),
                   jax.ShapeDtypeStruct((nhp, nq, 2, tq), F32)],
        args=(q, kv, kv, lse_r, delta_r, do, cum_r, cum_c), name=name, comm=comm)


def _s5_tables(w, layer):
    g, p = w["lam_re"].shape[1:]
    h = w["ssm_b_re"].shape[3]
    n = g * p
    lr = w["lam_re"][layer].reshape(1, n)
    li = w["lam_im"][layer].reshape(1, n)
    ldt = jnp.broadcast_to(w["log_dt"][layer][:, None], (g, p)).reshape(1, n)
    br = w["ssm_b_re"][layer].transpose(2, 0, 1).reshape(h, n)
    bi = w["ssm_b_im"][layer].transpose(2, 0, 1).reshape(h, n)
    cr = w["ssm_c_re"][layer].transpose(1, 0, 2).reshape(h, n)
    ci = w["ssm_c_im"][layer].transpose(1, 0, 2).reshape(h, n)
    return (lr, li, ldt, br, bi, cr, ci), (g, p, h)


def _local_step(x, tgt, w, net=None, *, attn_tile=1024):
    n_rows, d = x.shape
    n_layers = w["g_mix"].shape[0]
    n_s5 = w["lam_re"].shape[0]
    nh = w["b_f"].shape[0]
    nhp = nh // 2
    assert d == nh * HEAD_DIM
    tq = _tile(n_rows, attn_tile)
    g = {}
    saved = [dict() for _ in range(n_layers)]
    big = {}
    pending = {}

    def wt(name, layer):
        return w[name][layer]

    def carry_gather(group, run):
        if net is None or not net.has_group(group):
            return run(None)
        outs, got = run(net.gather_comm(group))
        net.store_gathered(group, got, w)
        return outs

    def carry_reduce(tag, run):
        if net is None or not pending:
            big.update(pending)
            pending.clear()
            return run(None)
        keys = list(pending)
        parts = net.reduce_prepare([pending[k] for k in keys], tag)
        outs, landed = run(_chip_exchange_comm(parts))
        big.update(zip(keys, net.reduce_finish(parts, landed, [pending[k] for k in keys], tag)))
        pending.clear()
        return outs

    h = x
    nxt = _rowwise(lambda a, gg: _rms(a, gg), [x], [_row2(w["g_mix"][0])], [(d, F32)], name="rms_first")[0]
    kvb = fl = cum = cq3 = ck3 = hnkv = None
    bf_pad = jnp.zeros((1, LANES), F32).at[0, :nh].set(w["b_f"])
    for l in range(n_layers):
        sv = saved[l]
        sv["h"] = h
        g_ffn = _row2(w["g_ffn"][l])
        if l < n_s5:
            tabs, (_, p, _) = _s5_tables(w, l)
            prep = _s5_prep(*tabs, p, name=f"s5_prep{l}")
            dskip = w["ssm_d"][l].reshape(1, d)
            y, st_re, st_im, sb_re, sb_im = carry_gather(
                f"stage{l}", lambda comm, u=nxt, pr=prep, ds=dskip: _s5_fwd(u, pr, ds, name=f"s5_fwd{l}", comm=comm))
            z = _rowwise(_gelu, [y], [], [(d, BF16)], name=f"gelu{l}")[0]
            zz = _mm_cols(z, *wt("w_glu", l), wc=0, name=f"glu_mm{l}")
            h1, hn2 = _rowwise(lambda hh, zq, gg: ((lambda t: (t, _rms(t, gg)))(hh + _glu(zq))),
                               [h, zz], [g_ffn], [(d, F32), (d, BF16)], name=f"mix_out{l}")
            sv.update(u=nxt, prep=prep, tabs=tabs, p=p, dskip=dskip, st_re=st_re, st_im=st_im, sb_re=sb_re,
                      sb_im=sb_im, y=y, z=z, zz=zz)
        else:
            j = l - n_s5
            qs = _mm_cols(nxt, *wt("w_q", j), wc=0, out_dtype=BF16, scale=HEAD_DIM ** -0.5, name=f"q_mm{j}")
            o_t, lse = carry_gather(
                f"stage{l}", lambda comm, q_=qs: _flash_fwd(q_, kvb, cq3, ck3, tq=tq, name=f"flash_fwd{j}", comm=comm))
            o = o_t.T
            a = _mm_cols(o, *wt("w_o", j), wc=0, name=f"o_mm{j}")
            h1, hn2 = _rowwise(lambda hh, aa, gg: ((lambda t: (t, _rms(t, gg)))(hh + aa)),
                               [h, a], [g_ffn], [(d, F32), (d, BF16)], name=f"mix_out{l}")
            sv.update(hn=nxt, qs=qs, o=o, lse=lse)
        uu = _mm_cols(hn2, *wt("w_in", l), wc=0, name=f"ffn_in{l}")
        cw, cb = w["conv_w"][l], _row2(w["conv_b"][l])
        act = _conv_fwd(uu, cw, cb, name=f"conv_fwd{l}")
        sv.update(h1=h1, hn2=hn2, uu=uu, act=act, cw=cw, cb=cb)

        def ffn_out(fn, rows, consts, outs, accs=()):
            return _mm_cols(act, *wt("w_out", l), wc=0, name=f"ffn_out{l}", epilogue=(fn, rows, consts, outs, accs))

        if l == n_layers - 1:
            def loss_fn(ff, hh, tt, gg):
                yv, vjp = jax.vjp(_rms, hh + ff, gg)
                err = yv - tt
                part = 0.5 * jnp.sum(jnp.mean(err * err, axis=-1, keepdims=True), axis=0, keepdims=True)
                dh, dg = vjp(err * (1.0 / d))
                return dh, dh, jnp.broadcast_to(part, (1, LANES)), dg
            dcur, dcur16, loss_row, dgf = ffn_out(loss_fn, [h1, tgt], [_row2(w["g_final"])],
                                                  [(d, F32), (d, BF16)], [(1, LANES), (1, d)])
            loss = loss_row[0, 0]
            g["g_final"] = dgf[0]
        elif l + 1 < n_s5:
            h, nxt = ffn_out(lambda ff, hh, gg: ((lambda t: (t, _rms(t, gg)))(hh + ff)), [h1],
                             [_row2(w["g_mix"][l + 1])], [(d, F32), (d, F32)])
        elif l + 1 == n_s5:
            h, nxt, hnkv = ffn_out(lambda ff, hh, g1, g2: ((lambda t: (t, _rms(t, g1), _rms(t, g2)))(hh + ff)), [h1],
                                   [_row2(w["g_mix"][l + 1]), _row2(w["g_kv"])],
                                   [(d, F32), (d, BF16), (d, BF16)])
            kvb = _mm_cols(hnkv, *wt("w_kv", 0), wc=0, out_dtype=BF16, name="kv_mm")
            fl = _mm_cols(hnkv, *wt("w_f", 0), wc=0, name="f_mm")
            cum = _cum_fwd(fl, bf_pad, name="cum_fwd")
            cq3 = cum[:, :nh].reshape(n_rows, nhp, 2).transpose(1, 0, 2)
            ck3 = cum[:, :nh].T.reshape(nhp, 2, n_rows)
        else:
            h, nxt = ffn_out(lambda ff, hh, gg: ((lambda t: (t, _rms(t, gg)))(hh + ff)), [h1],
                             [_row2(w["g_mix"][l + 1])], [(d, F32), (d, BF16)])

    per_layer = {k: [None] * n_layers for k in ("g_mix", "g_ffn", "conv_w", "conv_b")}
    per_s5 = {k: [None] * n_s5 for k in ("lam_re", "lam_im", "log_dt", "ssm_b_re", "ssm_b_im", "ssm_c_re",
                                         "ssm_c_im", "ssm_d")}
    dk_parts, dv_parts, dck_parts = [], [], []

    def by_row_shard(m):
        return m.reshape(N_CHIPS, m.shape[0] // N_CHIPS, m.shape[1])

    for l in reversed(range(n_layers)):
        sv = saved[l]
        dact = _mm_cols(dcur16, *wt("w_out", l), wc=1, name=f"ffn_out_dx{l}")
        pending["w_ffn_out", l] = by_row_shard(_mm_tn(sv["act"], dcur16, 1, name=f"ffn_out_dw{l}")[0])
        duu, dcw, dcb = _conv_bwd(sv["uu"], dact, sv["cw"], sv["cb"], name=f"conv_bwd{l}")
        per_layer["conv_w"][l] = jnp.concatenate([dcw[0], dcw[1]], axis=-1)
        per_layer["conv_b"][l] = jnp.concatenate([dcb[0, 0], dcb[1, 0]])
        node = _node_bwd_fn([1])
        d1, d1_16, dg = _mm_acc(duu, *wt("w_in", l), wc=1, name=f"ffn_in_dx{l}",
                                epilogue=(lambda dhn2, dd, hh, gg: node(dd, hh, dhn2, gg), [dcur, sv["h1"]],
                                          [_row2(w["g_ffn"][l])], [(d, F32), (d, BF16)], [(1, d)]))
        pending["w_ffn_in", l] = _mm_tn(sv["hn2"], duu, wt("w_in", l)[0].shape[0], name=f"ffn_in_dw{l}")
        per_layer["g_ffn"][l] = dg[0]
        if l < n_s5:
            def glu_bwd(zq, dd):
                _, vjp = jax.vjp(_glu, zq)
                return vjp(dd)[0]
            dzz = _rowwise(glu_bwd, [sv["zz"], d1], [], [(2 * d, BF16)], name=f"glu_bwd{l}")[0]
            dz = _mm_acc(dzz, *wt("w_glu", l), wc=1, name=f"glu_dx{l}")
            pending["w_glu", l] = _mm_tn(sv["z"], dzz, wt("w_glu", l)[0].shape[0], name=f"glu_dw{l}")

            def gelu_bwd(yy, dd):
                _, vjp = jax.vjp(_gelu, yy)
                return vjp(dd)[0]
            dy = _rowwise(gelu_bwd, [sv["y"], dz], [], [(d, F32)], name=f"gelu_bwd{l}")[0]
            du, dwbr, dwbi, dwcr, dwci, dlbr, dlbi, dd = carry_reduce(
                f"stage{l}", lambda comm, dy_=dy: _s5_bwd(sv["u"], dy_, sv["st_re"], sv["st_im"], sv["sb_re"],
                                                          sv["sb_im"], sv["prep"], sv["dskip"], name=f"s5_bwd{l}",
                                                          comm=comm))
            dlr, dli, dldt, dbr, dbi, dcr, dci = _s5_prep_bwd(*sv["tabs"], sv["p"], dlbr, dlbi, dwbr, dwbi, dwcr,
                                                              dwci, name=f"s5_prep_bwd{l}")
            gg, p = w["lam_re"].shape[1:]
            hh = w["ssm_b_re"].shape[3]
            per_s5["lam_re"][l] = dlr.reshape(gg, p)
            per_s5["lam_im"][l] = dli.reshape(gg, p)
            per_s5["log_dt"][l] = dldt.reshape(gg, p).sum(axis=1)
            per_s5["ssm_b_re"][l] = dbr.reshape(hh, gg, p).transpose(1, 2, 0)
            per_s5["ssm_b_im"][l] = dbi.reshape(hh, gg, p).transpose(1, 2, 0)
            per_s5["ssm_c_re"][l] = dcr.reshape(hh, gg, p).transpose(1, 0, 2)
            per_s5["ssm_c_im"][l] = dci.reshape(hh, gg, p).transpose(1, 0, 2)
            per_s5["ssm_d"][l] = dd.reshape(d)
            branches = [(w["g_mix"][l], [du])]
        else:
            j = l - n_s5
            do = _mm_cols(d1_16, *wt("w_o", j), wc=1, name=f"o_dx{j}")
            pending["w_o", j] = by_row_shard(_mm_tn(sv["o"], d1_16, 1, name=f"o_dw{j}")[0])
            delta_r = _head_delta(do, sv["o"], name=f"head_delta{j}")[:, :nh].T.reshape(nhp, 2, n_rows)
            dq, dk, dv, dck, dcq = carry_reduce(
                f"stage{l}", lambda comm, do_=do: _flash_bwd(sv["qs"], kvb, sv["lse"], delta_r, do_, cq3, ck3, tq=tq,
                                                          name=f"flash_bwd{j}", comm=comm))
            dk_parts.append(dk)
            dv_parts.append(dv)
            dck_parts.append(dck.transpose(1, 0, 2).reshape(n_rows, nh)
                             + dcq.transpose(0, 2, 1, 3).reshape(nh, n_rows).T)
            scale = HEAD_DIM ** -0.5
            dhn = _mm_cols(dq, *wt("w_q", j), wc=1, scale=scale, name=f"q_dx{j}")
            pending["w_q", j] = by_row_shard(_mm_tn(sv["hn"], dq, 1, scale=scale, name=f"q_dw{j}")[0])
            branches = [(w["g_mix"][l], [dhn])]
            if j == 0:
                def kv_sum(*parts):
                    half = len(parts) // 2
                    return jnp.concatenate([sum(parts[:half][1:], parts[0]),
                                            sum(parts[half:][1:], parts[half])], axis=1)
                dkv = _rowwise(kv_sum, dk_parts + dv_parts, [], [(2 * d, BF16)], name="dkv_sum")[0]
                dck_tot = dck_parts[0]
                for extra in dck_parts[1:]:
                    dck_tot = dck_tot + extra
                dcum = jnp.zeros((n_rows, LANES), F32).at[:, :nh].set(dck_tot)
                dfl, dbf = _cum_bwd(dcum, fl, bf_pad, name="cum_bwd")
                g["b_f"] = dbf[0, :nh]
                dhkv_a = _mm_cols(dkv, *wt("w_kv", 0), wc=1, name="kv_dx")
                dhkv_b = _mm_cols(dfl, *wt("w_f", 0), wc=1, name="f_dx")
                d_kvf = jnp.concatenate([_mm_tn(hnkv, dkv, 1, name="kv_dw")[0],
                                         _mm_tn(hnkv, dfl, 1, name="f_dw")[0][:, :nh]], axis=1)
                pending["w_kvf", 0] = d_kvf.reshape(d, N_CHIPS, -1).transpose(1, 0, 2)
                branches.append((w["g_kv"], [dhkv_a, dhkv_b]))
        dcur, dcur16, dgs = _node_bwd(d1, sv["h"], branches, name=f"mix_norm_bwd{l}")
        per_layer["g_mix"][l] = dgs[0]
        if len(dgs) > 1:
            g["g_kv"] = dgs[1]

    if pending:
        big.update(pending if net is None else
                   zip(list(pending), net.reduce_blocking([pending[k] for k in pending], "tail")))
    for k, v in (*per_layer.items(), *per_s5.items()):
        g[k] = jnp.stack(v)
    g["big"] = big
    return loss, dcur, g


def _position():
    x, y, c = lax.axis_index("x"), lax.axis_index("y"), lax.axis_index("c")
    chips = [(1 - x, y), (x, 1 - y), (1 - x, 1 - y)]
    return x, y, c, chips


def _all_gather_comm(shards):
    n = len(shards)

    def descriptors(ins, outs, sems):
        send_sems, recv_sems = sems
        x, y, c, chips = _position()
        my_slot = 2 * x + y
        sibling = (x, y, 1 - c)

        def rows(t, half):
            hr = ins[t].shape[0] // 2
            return pl.ds(half * hr, hr)

        def remote(k, t, src, dst, to):
            return pltpu.make_async_remote_copy(src_ref=src, dst_ref=dst, send_sem=send_sems.at[k, t],
                                                recv_sem=recv_sems.at[k, t], device_id=to, device_id_type=MESH)

        own = [remote(6, t, ins[t], outs[t].at[my_slot], sibling) for t in range(n)]
        ici = [remote(j, t, ins[t].at[rows(t, c)], outs[t].at[my_slot, rows(t, c)], (*chip, c))
               for j, chip in enumerate(chips) for t in range(n)]
        slots = [2 * chip[0] + chip[1] for chip in chips]
        fwd = [[remote(3 + j, t, outs[t].at[slots[j], rows(t, c)], outs[t].at[slots[j], rows(t, c)], sibling)
                for t in range(n)] for j in range(len(chips))]
        landed = [[remote(j, t, outs[t].at[slots[j], rows(t, c)], outs[t].at[slots[j], rows(t, c)], (*chips[j], c))
                   for t in range(n)] for j in range(len(chips))]
        from_sibling = [remote(3 + j, t, outs[t].at[slots[j], rows(t, 1 - c)], outs[t].at[slots[j], rows(t, 1 - c)],
                               sibling) for j in range(len(chips)) for t in range(n)]
        return own, ici, fwd, landed, from_sibling

    def start(ins, outs, sems):
        own, ici, _, _, _ = descriptors(ins, outs, sems)
        for cp in own + ici:
            cp.start()

    def finish(ins, outs, sems):
        own, ici, fwd, landed, from_sibling = descriptors(ins, outs, sems)
        for j in range(len(fwd)):
            for cp in landed[j]:
                cp.wait_recv()
            for cp in fwd[j]:
                cp.start()
        for cp in from_sibling + own:
            cp.wait_recv()
        for cp in own + ici + [cp for group in fwd for cp in group]:
            cp.wait_send()

    return _Comm(list(shards), [jax.ShapeDtypeStruct((N_CHIPS,) + a.shape, a.dtype) for a in shards],
                 [pltpu.SemaphoreType.DMA((7, n)), pltpu.SemaphoreType.DMA((7, n))], start, finish)


def _all_gather(shards, *, name):
    return _run_comm(_all_gather_comm(shards), name=name)


def _pair_exchange(grads, *, name):
    n = len(grads)

    def body(*refs):
        ins, outs = refs[:n], refs[n:2 * n]
        send_sems, recv_sems = refs[2 * n:]
        x, y, c, _ = _position()
        copies = [pltpu.make_async_remote_copy(src_ref=ins[t].at[:, 1 - c], dst_ref=outs[t],
                                               send_sem=send_sems.at[t], recv_sem=recv_sems.at[t],
                                               device_id=(x, y, 1 - c), device_id_type=MESH) for t in range(n)]
        for cp in copies:
            cp.start()
        for cp in copies:
            cp.wait()

    return pl.pallas_call(
        body, in_specs=_any_specs(n), out_specs=_any_specs(n),
        out_shape=[jax.ShapeDtypeStruct((a.shape[0],) + a.shape[2:], a.dtype) for a in grads],
        scratch_shapes=[pltpu.SemaphoreType.DMA((n,)), pltpu.SemaphoreType.DMA((n,))], name=name)(*grads)


def _chip_exchange_comm(parts):
    n = len(parts)

    def copies(ins, outs, sems):
        send_sems, recv_sems = sems
        _, _, c, chips = _position()
        return [pltpu.make_async_remote_copy(src_ref=ins[t].at[2 * chip[0] + chip[1]], dst_ref=outs[t].at[j],
                                             send_sem=send_sems.at[j, t], recv_sem=recv_sems.at[j, t],
                                             device_id=(*chip, c), device_id_type=MESH)
                for j, chip in enumerate(chips) for t in range(n)]

    def start(ins, outs, sems):
        for cp in copies(ins, outs, sems):
            cp.start()

    def finish(ins, outs, sems):
        for cp in copies(ins, outs, sems):
            cp.wait()

    return _Comm(list(parts), [jax.ShapeDtypeStruct((N_CHIPS - 1,) + a.shape[1:], a.dtype) for a in parts],
                 [pltpu.SemaphoreType.DMA((N_CHIPS - 1, n)), pltpu.SemaphoreType.DMA((N_CHIPS - 1, n))],
                 start, finish)


def _pair_share(both, *, name):
    n = len(both)

    def body(*refs):
        ins, outs = refs[:n], refs[n:2 * n]
        send_sems, recv_sems = refs[2 * n:]
        x, y, c, _ = _position()
        for t in range(n):
            pltpu.make_async_remote_copy(src_ref=ins[t].at[c], dst_ref=outs[t].at[c], send_sem=send_sems.at[t],
                                         recv_sem=recv_sems.at[t], device_id=(x, y, 1 - c),
                                         device_id_type=MESH).start()
        for t in range(n):
            pltpu.make_async_remote_copy(src_ref=ins[t].at[c], dst_ref=outs[t].at[1 - c], send_sem=send_sems.at[t],
                                         recv_sem=recv_sems.at[t], device_id=(x, y, 1 - c),
                                         device_id_type=MESH).wait()

    return pl.pallas_call(
        body, in_specs=_any_specs(n), out_specs=_any_specs(n),
        out_shape=[jax.ShapeDtypeStruct(a.shape, a.dtype) for a in both],
        input_output_aliases={t: t for t in range(n)},
        scratch_shapes=[pltpu.SemaphoreType.DMA((n,)), pltpu.SemaphoreType.DMA((n,))], name=name)(*both)


def _sum_pair(grad, landed, c, wire_dtype, *, name):
    slots, _, m, n = grad.shape
    tm = _tile(m, 256, 2 * SUBLANES)

    def body(c_ref, g_ref, l_ref, o_ref):
        o_ref[...] = (g_ref[0] + l_ref[...]).astype(wire_dtype)

    return pl.pallas_call(
        body,
        grid_spec=pltpu.PrefetchScalarGridSpec(
            num_scalar_prefetch=1, grid=(slots, m // tm),
            in_specs=[pl.BlockSpec((1, 1, tm, n), lambda s, i, c_ref: (s, c_ref[0], i, 0)),
                      pl.BlockSpec((1, tm, n), lambda s, i, c_ref: (s, i, 0))],
            out_specs=pl.BlockSpec((1, tm, n), lambda s, i, c_ref: (s, i, 0))),
        out_shape=jax.ShapeDtypeStruct((slots, m, n), wire_dtype), compiler_params=_cp(2), name=name)(
            c, grad, landed)


def _sum_chips(part, landed, slot_c, *, name):
    _, m, n = part.shape
    tm = _tile(m, 256, 2 * SUBLANES)

    def body(s_ref, p_ref, l_ref, o_ref):
        acc = p_ref[0].astype(F32)
        for j in range(N_CHIPS - 1):
            acc = acc + l_ref[j].astype(F32)
        o_ref[0] = acc

    return pl.pallas_call(
        body,
        grid_spec=pltpu.PrefetchScalarGridSpec(
            num_scalar_prefetch=1, grid=(m // tm,),
            in_specs=[pl.BlockSpec((1, tm, n), lambda i, s_ref: (s_ref[0], i, 0)),
                      pl.BlockSpec((N_CHIPS - 1, tm, n), lambda i, s_ref: (0, i, 0))],
            out_specs=pl.BlockSpec((1, tm, n), lambda i, s_ref: (s_ref[1], i, 0))),
        out_shape=jax.ShapeDtypeStruct((N_CORES, m, n), F32), compiler_params=_cp(1), name=name)(
            slot_c, part, landed)


def _reduce_prepare(grads, wire_dtypes, tag):
    c = lax.axis_index("c").reshape(1).astype(jnp.int32)
    views = []
    for a in grads:
        lead, last = a.shape[1], a.shape[-1]
        mid = 1
        for s in a.shape[2:-1]:
            mid *= s
        views.append(a.reshape(N_CHIPS, N_CORES, (lead // N_CORES) * mid, last))
    landed = _pair_exchange(views, name=f"rs_pair_exchange_{tag}")
    return [_sum_pair(v, l, c, wire_dtypes[t], name=f"rs_pair_sum_{tag}_{t}")
            for t, (v, l) in enumerate(zip(views, landed))]


def _reduce_finish(parts, landed, grads, tag):
    slot_c = jnp.stack([2 * lax.axis_index("x") + lax.axis_index("y"), lax.axis_index("c")]).astype(jnp.int32)
    both = [_sum_chips(p, l, slot_c, name=f"rs_chip_sum_{tag}_{t}") for t, (p, l) in enumerate(zip(parts, landed))]
    full = _pair_share(both, name=f"rs_pair_share_{tag}")
    return [f.reshape(a.shape[1:]) for f, a in zip(full, grads)]


def _reduce_scatter(grads, wire_dtypes, tag):
    parts = _reduce_prepare(grads, wire_dtypes, tag)
    landed = _run_comm(_chip_exchange_comm(parts), name=f"rs_chip_exchange_{tag}")
    return _reduce_finish(parts, landed, grads, tag)


class _Net:
    def __init__(self, groups, d, nh):
        self.groups, self.d, self.nh = groups, d, nh

    def has_group(self, group):
        return bool(self.groups.get(group))

    def gather_comm(self, group):
        return _all_gather_comm([shard for _, _, shard in self.groups[group]])

    def store_gathered(self, group, got, w):
        d, nh = self.d, self.nh
        for (name, layer, _), full in zip(self.groups[group], got):
            if name == "w_kvf":
                mat = full.transpose(1, 0, 2).reshape(d, -1)
                w["w_kv"][0] = (mat[:, :2 * d][None, None], 0)
                w["w_f"][0] = (jnp.zeros((d, LANES), BF16).at[:, :nh].set(mat[:, 2 * d:])[None, None], 0)
            elif name == "w_in":
                w[name][layer] = (full[:, None], 0)
            else:
                w[name][layer] = (full.reshape(1, 1, -1, full.shape[-1]), 0)

    def reduce_prepare(self, grads, tag):
        return _reduce_prepare(grads, [BF16] * len(grads), tag)

    def reduce_finish(self, parts, landed, grads, tag):
        return _reduce_finish(parts, landed, grads, tag)

    def reduce_blocking(self, grads, tag):
        return _reduce_scatter(grads, [BF16] * len(grads), tag)


def _adamw(w, g, m, v, *, name):
    def fn(ww, gg, mm, vv):
        mm = ADAM_B1 * mm + (1.0 - ADAM_B1) * gg
        vv = ADAM_B2 * vv + (1.0 - ADAM_B2) * (gg * gg)
        m_hat = mm / (1.0 - ADAM_B1 ** ADAM_STEP)
        v_hat = vv / (1.0 - ADAM_B2 ** ADAM_STEP)
        delta = -ADAM_LR * (m_hat / (jnp.sqrt(v_hat) + ADAM_EPS) + ADAM_WD * ww)
        return delta, mm, vv

    shape = w.shape
    two_d = [a.reshape(-1, shape[-1]) for a in (w, g, m, v)]
    outs = _rowwise(fn, two_d, [], [(shape[-1], F32)] * 3, name=name)
    return [o.reshape(shape) for o in outs]


def _to_bf16(a, *, name):
    two_d = a.reshape(-1, a.shape[-1])
    return _rowwise(lambda t: t, [two_d], [], [(a.shape[-1], BF16)], name=name)[0].reshape(a.shape)


def _pack(arrays, rows_multiple):
    flat = jnp.concatenate([a.reshape(-1) for a in arrays])
    rows = -(-flat.shape[0] // LANES)
    rows = -(-rows // rows_multiple) * rows_multiple
    return jnp.pad(flat, (0, rows * LANES - flat.shape[0])).reshape(rows, LANES)


def _unpack(packed, like):
    flat = packed.reshape(-1)
    out, pos = [], 0
    for a in like:
        out.append(flat[pos:pos + a.size].reshape(a.shape))
        pos += a.size
    return out


_PARAMS = ("g_mix", "g_ffn", "lam_re", "lam_im", "log_dt", "ssm_b_re", "ssm_b_im", "ssm_c_re", "ssm_c_im", "ssm_d",
           "w_glu", "g_kv", "w_kvf", "b_f", "w_q", "w_o", "w_ffn_in", "ffn_conv_w", "ffn_conv_b", "w_ffn_out",
           "g_final")
_BIG = ("w_glu", "w_kvf", "w_q", "w_o", "w_ffn_in", "w_ffn_out")
_SMALL_SHARDED = ("ssm_d", "ffn_conv_w")


def kernel(x, g_mix, g_ffn, lam_re, lam_im, log_dt, ssm_b_re, ssm_b_im, ssm_c_re, ssm_c_im, ssm_d, w_glu, g_kv, w_kvf, b_f, w_q, w_o, w_ffn_in, ffn_conv_w, ffn_conv_b, w_ffn_out, g_final, loss_target, m_g_mix, m_g_ffn, m_lam_re, m_lam_im, m_log_dt, m_ssm_b_re, m_ssm_b_im, m_ssm_c_re, m_ssm_c_im, m_ssm_d, m_w_glu, m_g_kv, m_w_kvf, m_b_f, m_w_q, m_w_o, m_w_ffn_in, m_ffn_conv_w, m_ffn_conv_b, m_w_ffn_out, m_g_final, v_g_mix, v_g_ffn, v_lam_re, v_lam_im, v_log_dt, v_ssm_b_re, v_ssm_b_im, v_ssm_c_re, v_ssm_c_im, v_ssm_d, v_w_glu, v_g_kv, v_w_kvf, v_b_f, v_w_q, v_w_o, v_w_ffn_in, v_ffn_conv_w, v_ffn_conv_b, v_w_ffn_out, v_g_final):
    p = dict(g_mix=g_mix, g_ffn=g_ffn, lam_re=lam_re, lam_im=lam_im, log_dt=log_dt, ssm_b_re=ssm_b_re,
             ssm_b_im=ssm_b_im, ssm_c_re=ssm_c_re, ssm_c_im=ssm_c_im, ssm_d=ssm_d, w_glu=w_glu, g_kv=g_kv,
             w_kvf=w_kvf, b_f=b_f, w_q=w_q, w_o=w_o, w_ffn_in=w_ffn_in, ffn_conv_w=ffn_conv_w,
             ffn_conv_b=ffn_conv_b, w_ffn_out=w_ffn_out, g_final=g_final)
    mom1 = dict(zip(_PARAMS, (m_g_mix, m_g_ffn, m_lam_re, m_lam_im, m_log_dt, m_ssm_b_re, m_ssm_b_im, m_ssm_c_re,
                              m_ssm_c_im, m_ssm_d, m_w_glu, m_g_kv, m_w_kvf, m_b_f, m_w_q, m_w_o, m_w_ffn_in,
                              m_ffn_conv_w, m_ffn_conv_b, m_w_ffn_out, m_g_final)))
    mom2 = dict(zip(_PARAMS, (v_g_mix, v_g_ffn, v_lam_re, v_lam_im, v_log_dt, v_ssm_b_re, v_ssm_b_im, v_ssm_c_re,
                              v_ssm_c_im, v_ssm_d, v_w_glu, v_g_kv, v_w_kvf, v_b_f, v_w_q, v_w_o, v_w_ffn_in,
                              v_ffn_conv_w, v_ffn_conv_b, v_w_ffn_out, v_g_final)))
    d = x.shape[-1]
    nh = b_f.shape[0]
    slot = 2 * lax.axis_index("x") + lax.axis_index("y")

    wb = {k: _to_bf16(p[k], name=f"to_bf16_{k}") for k in _BIG}
    gd, gcw, gl = _all_gather([ssm_d, ffn_conv_w, wb["w_glu"]], name="first_all_gather")
    n_lay, n_s5 = w_ffn_in.shape[0], lam_re.shape[0]
    n_fox = n_lay - n_s5
    groups = {f"stage{l}": [("w_in", l, wb["w_ffn_in"][l]), ("w_out", l, wb["w_ffn_out"][l])] for l in range(n_lay)}
    groups[f"stage{n_s5 - 1}"] += [("w_kvf", 0, wb["w_kvf"])] + [(k, j, wb[k][j]) for k in ("w_q", "w_o")
                                                                for j in range(n_fox)]
    w = dict(p)
    w.update(w_glu=[(gl, l) for l in range(n_s5)], w_in=[None] * n_lay, w_out=[None] * n_lay, w_q=[None] * n_fox,
             w_o=[None] * n_fox, w_kv=[None], w_f=[None],
             conv_w=gcw.transpose(1, 2, 0, 3).reshape(n_lay, DT_CONV_TAPS, -1), conv_b=ffn_conv_b,
             ssm_d=gd.transpose(1, 0, 2).reshape(gd.shape[1], d))

    loss_part, grad_x, g = _local_step(x[0], loss_target[0], w, _Net(groups, d, nh))
    loss = lax.psum(loss_part, ("x", "y", "c"))

    small_names = [k for k in _PARAMS if k not in _BIG]
    small_full = dict(g_mix=g["g_mix"], g_ffn=g["g_ffn"], lam_re=g["lam_re"], lam_im=g["lam_im"], log_dt=g["log_dt"],
                      ssm_b_re=g["ssm_b_re"], ssm_b_im=g["ssm_b_im"], ssm_c_re=g["ssm_c_re"], ssm_c_im=g["ssm_c_im"],
                      ssm_d=g["ssm_d"], g_kv=g["g_kv"], b_f=g["b_f"], ffn_conv_w=g["conv_w"],
                      ffn_conv_b=g["conv_b"], g_final=g["g_final"])
    small_list = [small_full[k] for k in small_names]
    pack = _pack(small_list, N_CHIPS * N_CORES * 2 * SUBLANES)
    pack4 = pack.reshape(N_CHIPS, pack.shape[0] // N_CHIPS, LANES)
    pack_shard = _reduce_scatter([pack4], [F32], "small")[0]
    red_big = {k: g["big"][k, 0] if p[k].ndim == 2 else jnp.stack([g["big"][k, l] for l in range(p[k].shape[0])])
               for k in _BIG}
    pack_all = _all_gather([pack_shard], name="small_grads_all_gather")[0]
    red_small = dict(zip(small_names, _unpack(pack_all, small_list)))
    for k in _SMALL_SHARDED:
        width = p[k].shape[-1]
        red_small[k] = lax.dynamic_slice_in_dim(red_small[k], slot * width, width, axis=red_small[k].ndim - 1)

    grads, deltas, new_m, new_v = {}, {}, {}, {}
    for k in _BIG:
        grads[k] = red_big[k]
        deltas[k], new_m[k], new_v[k] = _adamw(p[k], grads[k], mom1[k], mom2[k], name=f"adamw_{k}")
    packs = [_pack([src[k] for k in small_names], SUBLANES) for src in (p, red_small, mom1, mom2)]
    like = [p[k] for k in small_names]
    outs = [_unpack(o, like) for o in _adamw(*packs, name="adamw_small")]
    for i, k in enumerate(small_names):
        grads[k] = red_small[k]
        deltas[k], new_m[k], new_v[k] = outs[0][i], outs[1][i], outs[2][i]
    return (loss, grad_x[None], *[grads[k] for k in _PARAMS], *[deltas[k] for k in _PARAMS],
            *[new_m[k] for k in _PARAMS], *[new_v[k] for k in _PARAMS])
```

```python
import functools

import jax
import jax.numpy as jnp
from jax import lax
from jax.experimental import pallas as pl
from jax.experimental.pallas import tpu as pltpu

F32 = jnp.float32
BF16 = jnp.bfloat16

RMS_EPS = 1e-6
ADAM_LR = 0.001
ADAM_B1 = 0.9
ADAM_B2 = 0.999
ADAM_EPS = 1e-08
ADAM_WD = 0.01
ADAM_STEP = 10
CONV_TAPS = 3

LANES = 128
SUBLANES = 8
HEAD_DIM = 64
FLASH_ROW_TILE = 32
S5_BLOCK_GROUPS = 16
VMEM_LIMIT_BYTES = 48 << 20
MM_BLOCK_BUDGET_BYTES = 30 << 20
N_CHIPS = 4
N_CORES = 2
MESH = pl.DeviceIdType.MESH


def _cp(n_grid):
    return pltpu.CompilerParams(dimension_semantics=("arbitrary",) * n_grid, vmem_limit_bytes=VMEM_LIMIT_BYTES)


def _tile(n, pref, mult=SUBLANES):
    if n <= pref:
        return n
    t = (pref // mult) * mult
    while t >= mult:
        if n % t == 0:
            return t
        t -= mult
    return n


class _Comm:
    def __init__(self, ins, out_shapes, sems, start, finish):
        self.ins, self.out_shapes, self.sems, self.start, self.finish = ins, out_shapes, sems, start, finish


def _any_specs(n):
    return [pl.BlockSpec(memory_space=pl.ANY)] * n


def _run_comm(comm, *, name):
    n_in, n_out = len(comm.ins), len(comm.out_shapes)

    def body(*refs):
        ins, outs, sems = refs[:n_in], refs[n_in:n_in + n_out], refs[n_in + n_out:]
        comm.start(ins, outs, sems)
        comm.finish(ins, outs, sems)

    return pl.pallas_call(body, in_specs=_any_specs(n_in), out_specs=_any_specs(n_out),
                          out_shape=list(comm.out_shapes), scratch_shapes=list(comm.sems), name=name)(*comm.ins)


def _call(body, *, grid, in_specs, out_specs, out_shape, args, name, scratch_shapes=(), prefetch=(), comm=None):
    n_pre, n_in, n_out, n_scr = len(prefetch), len(in_specs), len(out_specs), len(scratch_shapes)
    in_specs, out_specs, out_shape = list(in_specs), list(out_specs), list(out_shape)
    scratch_shapes, args = list(scratch_shapes), list(args)
    kernel_body = body
    if comm is not None:
        n_cin, n_cout = len(comm.ins), len(comm.out_shapes)

        def kernel_body(*refs):
            pos = n_pre + n_in
            c_in = refs[pos:pos + n_cin]
            main_out = refs[pos + n_cin:pos + n_cin + n_out]
            pos += n_cin + n_out
            c_out = refs[pos:pos + n_cout]
            main_scr = refs[pos + n_cout:pos + n_cout + n_scr]
            sems = refs[pos + n_cout + n_scr:]
            ids = [pl.program_id(a) for a in range(len(grid))]
            first = functools.reduce(jnp.logical_and, [i == 0 for i in ids])
            last = functools.reduce(jnp.logical_and, [i == g - 1 for i, g in zip(ids, grid)])
            pl.when(first)(lambda: comm.start(c_in, c_out, sems))
            body(*refs[:n_pre + n_in], *main_out, *main_scr)
            pl.when(last)(lambda: comm.finish(c_in, c_out, sems))

        in_specs += _any_specs(n_cin)
        out_specs += _any_specs(n_cout)
        out_shape += list(comm.out_shapes)
        scratch_shapes += list(comm.sems)
        args += list(comm.ins)
    if prefetch:
        spec = pltpu.PrefetchScalarGridSpec(num_scalar_prefetch=n_pre, grid=grid, in_specs=in_specs,
                                            out_specs=out_specs, scratch_shapes=scratch_shapes)
        res = pl.pallas_call(kernel_body, grid_spec=spec, out_shape=out_shape, compiler_params=_cp(len(grid)),
                             name=name)(*prefetch, *args)
    else:
        res = pl.pallas_call(kernel_body, grid=grid, in_specs=in_specs, out_specs=out_specs, out_shape=out_shape,
                             scratch_shapes=scratch_shapes, compiler_params=_cp(len(grid)), name=name)(*args)
    return (res[:n_out], res[n_out:]) if comm is not None else res


def _row_tile(m, bytes_per_row, fixed_bytes):
    for tm in (1024, 512):
        if m % tm == 0 and 2 * (tm * bytes_per_row + fixed_bytes) <= MM_BLOCK_BUDGET_BYTES:
            return tm
    return _tile(m, 512)


def _dot(a, b, ca, cb):
    return lax.dot_general(a, b, (((ca,), (cb,)), ((), ())), preferred_element_type=F32)


def _epilogue_io(epilogue, m, tm, rows_axis, grid_rank):
    _, rows, consts, outs, accs = epilogue

    def at_rows(width):
        return pl.BlockSpec((tm, width), lambda *g: (g[rows_axis], 0))

    def whole(shape):
        return pl.BlockSpec(shape, lambda *g: (0,) * len(shape))

    in_specs = [at_rows(r.shape[1]) for r in rows] + [whole(c.shape) for c in consts]
    out_specs = [at_rows(wd) for wd, _ in outs] + [whole(s) for s in accs]
    out_shape = ([jax.ShapeDtypeStruct((m, wd), dt) for wd, dt in outs]
                 + [jax.ShapeDtypeStruct(s, F32) for s in accs])
    bytes_per_row = (sum(r.shape[1] * r.dtype.itemsize for r in rows)
                     + sum(wd * jnp.dtype(dt).itemsize for wd, dt in outs))
    return in_specs, out_specs, out_shape, bytes_per_row


def _epilogue_apply(epilogue, block, refs, first_row_tile):
    fn, rows, consts, outs, _ = epilogue
    n_in, n_out = len(rows) + len(consts), len(outs)
    res = fn(block, *[r[...] for r in refs[:n_in]])
    for o, val in zip(refs[n_in:n_in + n_out], res[:n_out]):
        o[...] = val.astype(o.dtype)
    a_refs = refs[n_in + n_out:]
    if a_refs:
        @pl.when(first_row_tile)
        def _():
            for a in a_refs:
                a[...] = jnp.zeros_like(a)
        for a, val in zip(a_refs, res[n_out:]):
            a[...] += val


def _mm_cols(x, w4, layer, *, wc, out_dtype=F32, scale=None, epilogue=None, name):
    m, k = x.shape
    slots, _, k0, k1 = w4.shape
    nb = k1 if wc == 0 else k0
    assert (k0 if wc == 0 else k1) == k
    if epilogue is None:
        tm = _row_tile(m, k * x.dtype.itemsize + nb * jnp.dtype(out_dtype).itemsize, k0 * k1 * w4.dtype.itemsize)
        extra_in, out_specs = [], pl.BlockSpec((tm, nb), lambda s, i: (i, s))
        out_shape = jax.ShapeDtypeStruct((m, slots * nb), out_dtype)
    else:
        assert slots == 1
        bytes_per_row = _epilogue_io(epilogue, m, SUBLANES, 1, 2)[3]
        tm = _row_tile(m, k * x.dtype.itemsize + bytes_per_row, k0 * k1 * w4.dtype.itemsize)
        extra_in, out_specs, out_shape, _ = _epilogue_io(epilogue, m, tm, 1, 2)

    def body(x_ref, w_ref, *refs):
        acc = _dot(x_ref[...].astype(BF16), w_ref[0, 0], 1, wc)
        if scale is not None:
            acc = acc * scale
        if epilogue is None:
            refs[0][...] = acc.astype(out_dtype)
        else:
            _epilogue_apply(epilogue, acc, refs, pl.program_id(1) == 0)

    extra_args = [] if epilogue is None else [*epilogue[1], *epilogue[2]]
    return pl.pallas_call(
        body, grid=(slots, m // tm),
        in_specs=[pl.BlockSpec((tm, k), lambda s, i: (i, 0)),
                  pl.BlockSpec((1, 1, k0, k1), lambda s, i: (s, layer, 0, 0)), *extra_in],
        out_specs=out_specs, out_shape=out_shape,
        compiler_params=_cp(2), name=name)(x, w4, *extra_args)


def _planes(a):
    return a if a.ndim == 3 else a[None]


def _mm_acc(x, w4, layer, *, wc, epilogue=None, name):
    x = _planes(x)
    n_planes, m, width = x.shape
    slots, _, k0, k1 = w4.shape
    kb = k0 if wc == 0 else k1
    nout = k1 if wc == 0 else k0
    assert n_planes * width == slots * kb
    spp = slots // n_planes
    x_spec_w = pl.BlockSpec((1, 1, k0, k1), lambda i, s: (s, layer, 0, 0))
    if epilogue is None:
        tm = _row_tile(m, kb * x.dtype.itemsize + nout * 4, k0 * k1 * w4.dtype.itemsize)

        def body(x_ref, w_ref, o_ref):
            @pl.when(pl.program_id(1) == 0)
            def _():
                o_ref[...] = jnp.zeros_like(o_ref)
            o_ref[...] += _dot(x_ref[0].astype(BF16), w_ref[0, 0], 1, wc)

        return pl.pallas_call(
            body, grid=(m // tm, slots),
            in_specs=[pl.BlockSpec((1, tm, kb), lambda i, s: (s // spp, i, s % spp)), x_spec_w],
            out_specs=pl.BlockSpec((tm, nout), lambda i, s: (i, 0)),
            out_shape=jax.ShapeDtypeStruct((m, nout), F32),
            compiler_params=_cp(2), name=name)(x, w4)

    bytes_per_row = _epilogue_io(epilogue, m, SUBLANES, 0, 2)[3]
    tm = _row_tile(m, kb * x.dtype.itemsize + nout * 2 + bytes_per_row, k0 * k1 * w4.dtype.itemsize)
    extra_in, out_specs, out_shape, _ = _epilogue_io(epilogue, m, tm, 0, 2)

    def body(x_ref, w_ref, *refs):
        acc = refs[-1]

        @pl.when(pl.program_id(1) == 0)
        def _():
            acc[...] = jnp.zeros_like(acc)
        acc[...] += _dot(x_ref[0].astype(BF16), w_ref[0, 0], 1, wc)

        @pl.when(pl.program_id(1) == slots - 1)
        def _():
            _epilogue_apply(epilogue, acc[...], refs[:-1], pl.program_id(0) == 0)

    return pl.pallas_call(
        body, grid=(m // tm, slots),
        in_specs=[pl.BlockSpec((1, tm, kb), lambda i, s: (s // spp, i, s % spp)), x_spec_w, *extra_in],
        out_specs=out_specs, out_shape=out_shape, scratch_shapes=[pltpu.VMEM((tm, nout), F32)],
        compiler_params=_cp(2), name=name)(x, w4, *epilogue[1], *epilogue[2])


def _mm_tn(x, dy, slots, *, scale=None, name):
    m, k = x.shape
    dy = _planes(dy)
    n_planes, _, width = dy.shape
    n = n_planes * width // slots
    spp = slots // n_planes
    ta = _tile(k, 512, LANES)
    tm = m
    while tm > 512 and tm % 2 == 0 and (2 * tm * (ta * x.dtype.itemsize + n * dy.dtype.itemsize)
                                         + 2 * ta * n * 4) > MM_BLOCK_BUDGET_BYTES:
        tm //= 2
    n_m = m // tm

    def body(x_ref, dy_ref, o_ref):
        @pl.when(pl.program_id(2) == 0)
        def _():
            o_ref[...] = jnp.zeros_like(o_ref)
        o_ref[0] += _dot(x_ref[...].astype(BF16), dy_ref[0].astype(BF16), 0, 0)
        if scale is not None:
            @pl.when(pl.program_id(2) == n_m - 1)
            def _():
                o_ref[...] = o_ref[...] * scale

    return pl.pallas_call(
        body, grid=(slots, k // ta, n_m),
        in_specs=[pl.BlockSpec((tm, ta), lambda s, a, i: (i, a)),
                  pl.BlockSpec((1, tm, n), lambda s, a, i: (s // spp, i, s % spp))],
        out_specs=pl.BlockSpec((1, ta, n), lambda s, a, i: (s, a, 0)),
        out_shape=jax.ShapeDtypeStruct((slots, k, n), F32),
        compiler_params=_cp(3), name=name)(x, dy)


def _rowwise(fn, rows, consts, outs, accs=(), *, tl=256, name):
    n_rows = rows[0].shape[0]
    tl = _tile(n_rows, tl)
    n_in = len(rows) + len(consts)
    n_out = len(outs)

    def body(*refs):
        res = fn(*[r[...] for r in refs[:n_in]])
        res = res if isinstance(res, (tuple, list)) else (res,)
        o_refs = refs[n_in:n_in + n_out]
        a_refs = refs[n_in + n_out:]
        for o, val in zip(o_refs, res[:n_out]):
            o[...] = val.astype(o.dtype)
        if a_refs:
            @pl.when(pl.program_id(0) == 0)
            def _():
                for a in a_refs:
                    a[...] = jnp.zeros_like(a)
            for a, val in zip(a_refs, res[n_out:]):
                a[...] += val

    in_specs = ([pl.BlockSpec((tl, r.shape[1]), lambda i: (i, 0)) for r in rows]
                + [pl.BlockSpec(c.shape, lambda i: (0, 0)) for c in consts])
    out_specs = ([pl.BlockSpec((tl, w), lambda i: (i, 0)) for w, _ in outs]
                 + [pl.BlockSpec(s, lambda i: (0, 0)) for s in accs])
    out_shape = ([jax.ShapeDtypeStruct((n_rows, w), dt) for w, dt in outs]
                 + [jax.ShapeDtypeStruct(s, F32) for s in accs])
    return pl.pallas_call(body, grid=(n_rows // tl,), in_specs=in_specs, out_specs=out_specs,
                          out_shape=out_shape, compiler_params=_cp(1), name=name)(*rows, *consts)


def _rms(x, g):
    return x * lax.rsqrt(jnp.mean(x * x, axis=-1, keepdims=True) + RMS_EPS) * g


def _sigmoid(x):
    return 1.0 / (1.0 + jnp.exp(-x))


def _glu(zz):
    d = zz.shape[1] // 2
    return zz[:, :d] * _sigmoid(zz[:, d:])


def _gelu(y):
    return jax.nn.gelu(y)


def _row2(v):
    return v.reshape(1, -1)


def _node_bwd_fn(counts):
    n_dy = sum(counts)

    def fn(d, hh, *rest):
        dys, gs = rest[:n_dy], rest[n_dy:]
        tot, dgs, pos = d, [], 0
        for g, cnt in zip(gs, counts):
            dy = dys[pos].astype(F32)
            for extra in dys[pos + 1:pos + cnt]:
                dy = dy + extra.astype(F32)
            pos += cnt
            _, vjp = jax.vjp(_rms, hh, g)
            dx, dg = vjp(dy)
            tot = tot + dx
            dgs.append(dg)
        return (tot, tot, *dgs)

    return fn


def _node_bwd(d_in, h, branches, *, name):
    width = h.shape[1]
    flat = [dy for _, dys in branches for dy in dys]
    res = _rowwise(_node_bwd_fn([len(dys) for _, dys in branches]), [d_in, h, *flat],
                   [_row2(g) for g, _ in branches], [(width, F32), (width, BF16)], [(1, width)] * len(branches),
                   name=name)
    return res[0], res[1], [r[0] for r in res[2:]]


def _s5_prep_fn(lr, li, ldt, br, bi, cr, ci, *, gq, h, p):
    dt = jnp.exp(ldt)
    mag = jnp.exp(lr * dt)
    lb_re = mag * jnp.cos(li * dt)
    lb_im = mag * jnp.sin(li * dt)
    den = lr * lr + li * li
    nr = lb_re - 1.0
    fr = (nr * lr + lb_im * li) / den
    fi = (lb_im * lr - nr * li) / den
    bb_re = fr * br - fi * bi
    bb_im = fr * bi + fi * br
    shape = (gq * h, gq * p)
    r = lax.broadcasted_iota(jnp.int32, shape, 0)
    c = lax.broadcasted_iota(jnp.int32, shape, 1)
    mask = jnp.where(jnp.right_shift(r, h.bit_length() - 1) == jnp.right_shift(c, p.bit_length() - 1), 1.0, 0.0)

    def expand(t):
        return jnp.concatenate([t] * gq, axis=0) * mask

    return lb_re, lb_im, expand(bb_re), expand(bb_im), expand(cr), expand(ci)


def _s5_prep(lr, li, ldt, br, bi, cr, ci, p, *, name):
    n = lr.shape[1]
    h = br.shape[0]
    gq = S5_BLOCK_GROUPS
    nq, cq = gq * p, gq * h
    nblk = n // nq
    fn = functools.partial(_s5_prep_fn, gq=gq, h=h, p=p)

    def body(lr_r, li_r, ldt_r, br_r, bi_r, cr_r, ci_r, lbr_o, lbi_o, wbr_o, wbi_o, wcr_o, wci_o):
        lb_re, lb_im, wbr, wbi, wcr, wci = fn(lr_r[...], li_r[...], ldt_r[...], br_r[...], bi_r[...],
                                              cr_r[...], ci_r[...])
        lbr_o[...] = lb_re
        lbi_o[...] = lb_im
        wbr_o[0] = wbr.astype(BF16)
        wbi_o[0] = wbi.astype(BF16)
        wcr_o[0] = wcr.astype(BF16)
        wci_o[0] = wci.astype(BF16)

    vec = pl.BlockSpec((1, nq), lambda q: (0, q))
    tab = pl.BlockSpec((h, nq), lambda q: (0, q))
    wsp = pl.BlockSpec((1, cq, nq), lambda q: (q, 0, 0))
    wsh = jax.ShapeDtypeStruct((nblk, cq, nq), BF16)
    vsh = jax.ShapeDtypeStruct((1, n), F32)
    return pl.pallas_call(body, grid=(nblk,), in_specs=[vec, vec, vec, tab, tab, tab, tab],
                          out_specs=[vec, vec, wsp, wsp, wsp, wsp], out_shape=[vsh, vsh, wsh, wsh, wsh, wsh],
                          compiler_params=_cp(1), name=name)(lr, li, ldt, br, bi, cr, ci)


def _s5_prep_bwd(lr, li, ldt, br, bi, cr, ci, p, dlbr, dlbi, dwbr, dwbi, dwcr, dwci, *, name):
    n = lr.shape[1]
    h = br.shape[0]
    gq = S5_BLOCK_GROUPS
    nq, cq = gq * p, gq * h
    nblk = n // nq
    fn = functools.partial(_s5_prep_fn, gq=gq, h=h, p=p)

    def body(lr_r, li_r, ldt_r, br_r, bi_r, cr_r, ci_r, dlbr_r, dlbi_r, dwbr_r, dwbi_r, dwcr_r, dwci_r,
             *outs):
        _, vjp = jax.vjp(fn, lr_r[...], li_r[...], ldt_r[...], br_r[...], bi_r[...], cr_r[...], ci_r[...])
        grads = vjp((dlbr_r[0], dlbi_r[0], dwbr_r[0], dwbi_r[0], dwcr_r[0], dwci_r[0]))
        for o, g in zip(outs, grads):
            o[...] = g

    vec = pl.BlockSpec((1, nq), lambda q: (0, q))
    tab = pl.BlockSpec((h, nq), lambda q: (0, q))
    vec3 = pl.BlockSpec((1, 1, nq), lambda q: (q, 0, 0))
    wsp = pl.BlockSpec((1, cq, nq), lambda q: (q, 0, 0))
    vsh = jax.ShapeDtypeStruct((1, n), F32)
    tsh = jax.ShapeDtypeStruct((h, n), F32)
    return pl.pallas_call(body, grid=(nblk,),
                          in_specs=[vec, vec, vec, tab, tab, tab, tab, vec3, vec3, wsp, wsp, wsp, wsp],
                          out_specs=[vec, vec, vec, tab, tab, tab, tab],
                          out_shape=[vsh, vsh, vsh, tsh, tsh, tsh, tsh],
                          compiler_params=_cp(1), name=name)(lr, li, ldt, br, bi, cr, ci,
                                                             dlbr, dlbi, dwbr, dwbi, dwcr, dwci)


def _scan_rows(s_re, s_im, a_re, a_im, c_re, c_im, *, reverse):
    t_rows, n = s_re.shape
    nb = t_rows // SUBLANES
    row = lax.broadcasted_iota(jnp.int32, (SUBLANES, n), 0)

    def cmul(x, y):
        return x[0] * y[0] - x[1] * y[1], x[0] * y[1] + x[1] * y[0]

    a1 = (jnp.broadcast_to(a_re, (SUBLANES, n)), jnp.broadcast_to(a_im, (SUBLANES, n)))
    a2 = cmul(a1, a1)
    a4 = cmul(a2, a2)
    steps = []
    for dist, (pr, pi) in ((1, a1), (2, a2), (4, a4)):
        keep = (row < SUBLANES - dist) if reverse else (row >= dist)
        steps.append((SUBLANES - dist if reverse else dist, (jnp.where(keep, pr, 0.0), jnp.where(keep, pi, 0.0))))
    pk = (a_re, a_im)
    tab_re = jnp.zeros((SUBLANES, n), F32)
    tab_im = jnp.zeros((SUBLANES, n), F32)
    for i in range(SUBLANES):
        at = (SUBLANES - 1 - i) if reverse else i
        tab_re = jnp.where(row == at, pk[0], tab_re)
        tab_im = jnp.where(row == at, pk[1], tab_im)
        pk = cmul(pk, (a_re, a_im))

    def step(b, carry):
        cr, ci = carry
        blk = (nb - 1 - b) if reverse else b
        off = pl.multiple_of(blk * SUBLANES, SUBLANES)
        x_re = s_re[pl.ds(off, SUBLANES), :]
        x_im = s_im[pl.ds(off, SUBLANES), :]
        for sh, (pr, pi) in steps:
            sh_re = pltpu.roll(x_re, sh, 0)
            sh_im = pltpu.roll(x_im, sh, 0)
            x_re, x_im = x_re + pr * sh_re - pi * sh_im, x_im + pr * sh_im + pi * sh_re
        x_re, x_im = x_re + tab_re * cr - tab_im * ci, x_im + tab_re * ci + tab_im * cr
        s_re[pl.ds(off, SUBLANES), :] = x_re
        s_im[pl.ds(off, SUBLANES), :] = x_im
        edge = 0 if reverse else SUBLANES - 1
        return x_re[edge:edge + 1, :], x_im[edge:edge + 1, :]

    return lax.fori_loop(0, nb, step, (c_re, c_im))


def _s5_fwd(u, prep, dskip, *, name, comm=None):
    lb_re, lb_im, wbr, wbi, wcr, wci = prep
    n_rows, _ = u.shape
    nblk, cq, nq = wbr.shape
    tt = _tile(n_rows, 512)
    nch = n_rows // tt

    def body(u_ref, wbr_r, wbi_r, wcr_r, wci_r, lbr_r, lbi_r, d_ref, y_ref, s_re, s_im, sbr_o, sbi_o, c_re, c_im):
        @pl.when(pl.program_id(1) == 0)
        def _():
            c_re[...] = jnp.zeros_like(c_re)
            c_im[...] = jnp.zeros_like(c_im)
        uf = u_ref[...]
        ub = uf.astype(BF16)
        s_re[...] = _dot(ub, wbr_r[0], 1, 0)
        s_im[...] = _dot(ub, wbi_r[0], 1, 0)
        sbr_o[0] = c_re[...]
        sbi_o[0] = c_im[...]
        cr, ci = _scan_rows(s_re, s_im, lbr_r[...], lbi_r[...], c_re[...], c_im[...], reverse=False)
        c_re[...] = cr
        c_im[...] = ci
        y = _dot(s_re[...].astype(BF16), wcr_r[0], 1, 1) - _dot(s_im[...].astype(BF16), wci_r[0], 1, 1)
        y_ref[...] = y + d_ref[...] * uf

    wsp = pl.BlockSpec((1, cq, nq), lambda q, i: (q, 0, 0))
    vec = pl.BlockSpec((1, nq), lambda q, i: (0, q))
    act = pl.BlockSpec((tt, cq), lambda q, i: (i, q))
    sb = pl.BlockSpec((1, 1, nq), lambda q, i: (i, 0, q))
    sbsh = jax.ShapeDtypeStruct((nch, 1, nblk * nq), F32)
    states = pl.BlockSpec((tt, nq), lambda q, i: (i, q))
    stsh = jax.ShapeDtypeStruct((n_rows, nblk * nq), F32)
    return _call(
        body, grid=(nblk, nch),
        in_specs=[act, wsp, wsp, wsp, wsp, vec, vec, pl.BlockSpec((1, cq), lambda q, i: (0, q))],
        out_specs=[act, states, states, sb, sb],
        out_shape=[jax.ShapeDtypeStruct(u.shape, F32), stsh, stsh, sbsh, sbsh],
        scratch_shapes=[pltpu.VMEM((1, nq), F32), pltpu.VMEM((1, nq), F32)],
        args=(u, wbr, wbi, wcr, wci, lb_re, lb_im, dskip), name=name, comm=comm)


def _s5_bwd(u, dy, st_re, st_im, sb_re, sb_im, prep, dskip, *, name, comm=None):
    lb_re, lb_im, wbr, wbi, wcr, wci = prep
    n_rows, _ = u.shape
    nblk, cq, nq = wbr.shape
    tt = _tile(n_rows, 512)
    nch = n_rows // tt

    def body(u_ref, dy_ref, s_re, s_im, sbr_r, sbi_r, wbr_r, wbi_r, wcr_r, wci_r, lbr_r, lbi_r, d_ref,
             du_ref, dwbr, dwbi, dwcr, dwci, dlbr, dlbi, dd_ref, g_re, g_im, lc_re, lc_im):
        @pl.when(pl.program_id(1) == 0)
        def _():
            for ref in (lc_re, lc_im, dwbr, dwbi, dwcr, dwci, dlbr, dlbi, dd_ref):
                ref[...] = jnp.zeros_like(ref)
        uf = u_ref[...]
        ub = uf.astype(BF16)
        dyf = dy_ref[...]
        dyb = dyf.astype(BF16)
        sr16 = s_re[...].astype(BF16)
        si16 = s_im[...].astype(BF16)
        dwcr[0] += _dot(dyb, sr16, 0, 0)
        dwci[0] -= _dot(dyb, si16, 0, 0)
        g_re[...] = _dot(dyb, wcr_r[0], 1, 0)
        g_im[...] = -_dot(dyb, wci_r[0], 1, 0)
        lcr, lci = _scan_rows(g_re, g_im, lbr_r[...], -lbi_r[...], lc_re[...], lc_im[...], reverse=True)
        lc_re[...] = lcr
        lc_im[...] = lci
        lam_r = g_re[...]
        lam_i = g_im[...]
        first = lax.broadcasted_iota(jnp.int32, (tt, nq), 0) == 0
        prev_r = jnp.where(first, sbr_r[0], pltpu.roll(s_re[...], 1, 0))
        prev_i = jnp.where(first, sbi_r[0], pltpu.roll(s_im[...], 1, 0))
        dlbr[0] += jnp.sum(lam_r * prev_r + lam_i * prev_i, axis=0, keepdims=True)
        dlbi[0] += jnp.sum(lam_i * prev_r - lam_r * prev_i, axis=0, keepdims=True)
        lr16 = lam_r.astype(BF16)
        li16 = lam_i.astype(BF16)
        du_ref[...] = _dot(lr16, wbr_r[0], 1, 1) + _dot(li16, wbi_r[0], 1, 1) + d_ref[...] * dyf
        dwbr[0] += _dot(ub, lr16, 0, 0)
        dwbi[0] += _dot(ub, li16, 0, 0)
        dd_ref[0] += jnp.sum(dyf * uf, axis=0, keepdims=True)

    last = nch - 1
    wsp = pl.BlockSpec((1, cq, nq), lambda q, i: (q, 0, 0))
    vec = pl.BlockSpec((1, nq), lambda q, i: (0, q))
    act = pl.BlockSpec((tt, cq), lambda q, i: (last - i, q))
    sb = pl.BlockSpec((1, 1, nq), lambda q, i: (last - i, 0, q))
    vec3 = pl.BlockSpec((1, 1, nq), lambda q, i: (q, 0, 0))
    dsp = pl.BlockSpec((1, 1, cq), lambda q, i: (q, 0, 0))
    wsh = jax.ShapeDtypeStruct((nblk, cq, nq), F32)
    v3sh = jax.ShapeDtypeStruct((nblk, 1, nq), F32)
    big = pltpu.VMEM((tt, nq), F32)
    states = pl.BlockSpec((tt, nq), lambda q, i: (last - i, q))
    return _call(
        body, grid=(nblk, nch),
        in_specs=[act, act, states, states, sb, sb, wsp, wsp, wsp, wsp, vec, vec,
                  pl.BlockSpec((1, cq), lambda q, i: (0, q))],
        out_specs=[act, wsp, wsp, wsp, wsp, vec3, vec3, dsp],
        out_shape=[jax.ShapeDtypeStruct(u.shape, F32), wsh, wsh, wsh, wsh, v3sh, v3sh,
                   jax.ShapeDtypeStruct((nblk, 1, cq), F32)],
        scratch_shapes=[big, big, pltpu.VMEM((1, nq), F32), pltpu.VMEM((1, nq), F32)],
        args=(u, dy, st_re, st_im, sb_re, sb_im, wbr, wbi, wcr, wci, lb_re, lb_im, dskip), name=name, comm=comm)


def _conv_taps(cur, prev, w, b):
    ext = jnp.concatenate([prev, cur], axis=0)
    x1 = pltpu.roll(ext, 1, 0)[SUBLANES:, :]
    x2 = pltpu.roll(ext, 2, 0)[SUBLANES:, :]
    return b + x2 * w[0:1, :] + x1 * w[1:2, :] + cur * w[2:3, :], x1, x2


def _conv_fwd(uu, cw, cb, *, name):
    n_rows, f2 = uu.shape
    f = f2 // 2
    tc = _tile(f, 1408, LANES)
    tl = _tile(n_rows, 256)
    nfb = f // tc

    def body(g_ref, u_ref, wg_ref, wu_ref, bg_ref, bu_ref, o_ref, pg, pu):
        @pl.when(pl.program_id(1) == 0)
        def _():
            pg[...] = jnp.zeros_like(pg)
            pu[...] = jnp.zeros_like(pu)
        gcur = g_ref[...]
        ucur = u_ref[...]
        cg, _, _ = _conv_taps(gcur, pg[...], wg_ref[...], bg_ref[...])
        cu, _, _ = _conv_taps(ucur, pu[...], wu_ref[...], bu_ref[...])
        o_ref[...] = (cg * _sigmoid(cg) * cu).astype(o_ref.dtype)
        pg[...] = gcur[tl - SUBLANES:, :]
        pu[...] = ucur[tl - SUBLANES:, :]

    return pl.pallas_call(
        body, grid=(nfb, n_rows // tl),
        in_specs=[pl.BlockSpec((tl, tc), lambda j, i: (i, j)), pl.BlockSpec((tl, tc), lambda j, i: (i, j + nfb)),
                  pl.BlockSpec((CONV_TAPS, tc), lambda j, i: (0, j)),
                  pl.BlockSpec((CONV_TAPS, tc), lambda j, i: (0, j + nfb)),
                  pl.BlockSpec((1, tc), lambda j, i: (0, j)), pl.BlockSpec((1, tc), lambda j, i: (0, j + nfb))],
        out_specs=pl.BlockSpec((tl, tc), lambda j, i: (i, j)),
        out_shape=jax.ShapeDtypeStruct((n_rows, f), BF16),
        scratch_shapes=[pltpu.VMEM((SUBLANES, tc), F32), pltpu.VMEM((SUBLANES, tc), F32)],
        compiler_params=_cp(2), name=name)(uu, uu, cw, cw, cb, cb)


def _conv_bwd(uu, dact, cw, cb, *, name):
    n_rows, f2 = uu.shape
    f = f2 // 2
    tc = _tile(f, 1408, LANES)
    tl = _tile(n_rows, 256)
    nfb = f // tc
    nrb = n_rows // tl
    halo_per_tile = tl // SUBLANES

    def body(g_ref, gh_ref, u_ref, uh_ref, da_ref, wg_ref, wu_ref, bg_ref, bu_ref,
             duu_ref, dw_ref, db_ref, nxt_g, nxt_u):
        i = pl.program_id(1)
        rb = nrb - 1 - i

        @pl.when(i == 0)
        def _():
            for ref in (nxt_g, nxt_u, dw_ref, db_ref):
                ref[...] = jnp.zeros_like(ref)
        has_prev = jnp.where(rb > 0, 1.0, 0.0)
        gcur, ucur = g_ref[...], u_ref[...]
        wg, wu = wg_ref[...], wu_ref[...]
        cg, g1, g2 = _conv_taps(gcur, gh_ref[...] * has_prev, wg, bg_ref[...])
        cu, u1, u2 = _conv_taps(ucur, uh_ref[...] * has_prev, wu, bu_ref[...])
        sg = _sigmoid(cg)
        silu = cg * sg
        da = da_ref[...]

        def transpose_conv(plane, d, cur, x1, x2, w, nxt):
            ext = jnp.concatenate([d, nxt[...]], axis=0)
            d1 = pltpu.roll(ext, tl + SUBLANES - 1, 0)[:tl, :]
            d2 = pltpu.roll(ext, tl + SUBLANES - 2, 0)[:tl, :]
            duu_ref[plane] = (w[2:3, :] * d + w[1:2, :] * d1 + w[0:1, :] * d2).astype(duu_ref.dtype)
            nxt[...] = d[0:SUBLANES, :]
            dw_ref[plane] += jnp.concatenate([jnp.sum(d * x2, axis=0, keepdims=True),
                                              jnp.sum(d * x1, axis=0, keepdims=True),
                                              jnp.sum(d * cur, axis=0, keepdims=True)], axis=0)
            db_ref[plane] += jnp.sum(d, axis=0, keepdims=True)

        transpose_conv(0, da * cu * (sg * (1.0 + cg * (1.0 - sg))), gcur, g1, g2, wg, nxt_g)
        transpose_conv(1, da * silu, ucur, u1, u2, wu, nxt_u)

    def halo(j, i):
        return jnp.maximum((nrb - 1 - i) * halo_per_tile - 1, 0)

    return pl.pallas_call(
        body, grid=(nfb, nrb),
        in_specs=[pl.BlockSpec((tl, tc), lambda j, i: (nrb - 1 - i, j)),
                  pl.BlockSpec((SUBLANES, tc), lambda j, i: (halo(j, i), j)),
                  pl.BlockSpec((tl, tc), lambda j, i: (nrb - 1 - i, j + nfb)),
                  pl.BlockSpec((SUBLANES, tc), lambda j, i: (halo(j, i), j + nfb)),
                  pl.BlockSpec((tl, tc), lambda j, i: (nrb - 1 - i, j)),
                  pl.BlockSpec((CONV_TAPS, tc), lambda j, i: (0, j)),
                  pl.BlockSpec((CONV_TAPS, tc), lambda j, i: (0, j + nfb)),
                  pl.BlockSpec((1, tc), lambda j, i: (0, j)),
                  pl.BlockSpec((1, tc), lambda j, i: (0, j + nfb))],
        out_specs=[pl.BlockSpec((2, tl, tc), lambda j, i: (0, nrb - 1 - i, j)),
                   pl.BlockSpec((2, CONV_TAPS, tc), lambda j, i: (0, 0, j)),
                   pl.BlockSpec((2, 1, tc), lambda j, i: (0, 0, j))],
        out_shape=[jax.ShapeDtypeStruct((2, n_rows, f), BF16), jax.ShapeDtypeStruct((2, CONV_TAPS, f), F32),
                   jax.ShapeDtypeStruct((2, 1, f), F32)],
        scratch_shapes=[pltpu.VMEM((SUBLANES, tc), F32), pltpu.VMEM((SUBLANES, tc), F32)],
        compiler_params=_cp(2), name=name)(uu, uu, uu, uu, dact, cw, cw, cb, cb)


def _log_sigmoid(x):
    t = jnp.exp(-jnp.abs(x))
    log1p_t = jnp.where(t < 1e-3, t * (1.0 - t * (0.5 - t * (1.0 / 3.0))), jnp.log(1.0 + t))
    return jnp.minimum(x, 0.0) - log1p_t


def _dlog_sigmoid(x):
    t = jnp.exp(-jnp.abs(x))
    return jnp.where(x >= 0, t, 1.0) / (1.0 + t)


def _tri_dot(tri, x):
    return jnp.dot(tri, x, precision=lax.Precision.HIGHEST, preferred_element_type=F32)


def _cum_fwd(fl, bf, *, name):
    n_rows, width = fl.shape
    tc = _tile(n_rows, 256)

    def body(fl_ref, bf_ref, o_ref, carry):
        @pl.when(pl.program_id(0) == 0)
        def _():
            carry[...] = jnp.zeros_like(carry)
        x = _log_sigmoid(fl_ref[...] + bf_ref[...])
        r = lax.broadcasted_iota(jnp.int32, (tc, tc), 0)
        c = lax.broadcasted_iota(jnp.int32, (tc, tc), 1)
        y = _tri_dot(jnp.where(r >= c, 1.0, 0.0), x) + carry[...]
        o_ref[...] = y
        carry[...] = y[tc - 1:tc, :]

    return pl.pallas_call(
        body, grid=(n_rows // tc,),
        in_specs=[pl.BlockSpec((tc, width), lambda i: (i, 0)), pl.BlockSpec((1, width), lambda i: (0, 0))],
        out_specs=pl.BlockSpec((tc, width), lambda i: (i, 0)),
        out_shape=jax.ShapeDtypeStruct(fl.shape, F32),
        scratch_shapes=[pltpu.VMEM((1, width), F32)], compiler_params=_cp(1), name=name)(fl, bf)


def _cum_bwd(dcum, fl, bf, *, name):
    n_rows, width = fl.shape
    tc = _tile(n_rows, 256)
    last = n_rows // tc - 1

    def body(dc_ref, fl_ref, bf_ref, dfl_ref, dbf_ref, carry):
        @pl.when(pl.program_id(0) == 0)
        def _():
            carry[...] = jnp.zeros_like(carry)
            dbf_ref[...] = jnp.zeros_like(dbf_ref)
        r = lax.broadcasted_iota(jnp.int32, (tc, tc), 0)
        c = lax.broadcasted_iota(jnp.int32, (tc, tc), 1)
        dls = _tri_dot(jnp.where(r <= c, 1.0, 0.0), dc_ref[...]) + carry[...]
        carry[...] = dls[0:1, :]
        dfl = dls * _dlog_sigmoid(fl_ref[...] + bf_ref[...])
        dfl_ref[...] = dfl.astype(dfl_ref.dtype)
        dbf_ref[...] += jnp.sum(dfl, axis=0, keepdims=True)

    return pl.pallas_call(
        body, grid=(n_rows // tc,),
        in_specs=[pl.BlockSpec((tc, width), lambda i: (last - i, 0)),
                  pl.BlockSpec((tc, width), lambda i: (last - i, 0)),
                  pl.BlockSpec((1, width), lambda i: (0, 0))],
        out_specs=[pl.BlockSpec((tc, width), lambda i: (last - i, 0)), pl.BlockSpec((1, width), lambda i: (0, 0))],
        out_shape=[jax.ShapeDtypeStruct(fl.shape, BF16), jax.ShapeDtypeStruct((1, width), F32)],
        scratch_shapes=[pltpu.VMEM((1, width), F32)], compiler_params=_cp(1), name=name)(dcum, fl, bf)


def _head_masks():
    lane = lax.broadcasted_iota(jnp.int32, (1, LANES), 1)
    return (lane < HEAD_DIM, lane >= HEAD_DIM)


def _flash_fwd(q, kv, cum_c, cum_r, *, tq, name, comm=None):
    n_rows, d = q.shape
    nhp = d // LANES
    tk = tq
    nq = n_rows // tq
    rt = _tile(tk, FLASH_ROW_TILE)
    reps = (1, tq // LANES)

    def body(qi_ref, kj_ref, q_ref, k_ref, v_ref, cq_ref, ck_ref, ot_ref, lse_ref, m0, m1, l0, l1, acc,
             s0, s1, p0, p1, b0, b1):
        i = qi_ref[pl.program_id(1)]
        j = kj_ref[pl.program_id(1)]
        ms, ls = (m0, m1), (l0, l1)
        head_rows = lax.broadcasted_iota(jnp.int32, (LANES, 1), 0) < HEAD_DIM

        @pl.when(j == 0)
        def _():
            for h in range(2):
                ms[h][...] = jnp.full_like(ms[h], -jnp.inf)
                ls[h][...] = jnp.zeros_like(ls[h])
            acc[...] = jnp.zeros_like(acc)

        def block(diagonal):
            qv, kk, vv = q_ref[...], k_ref[...], v_ref[...]
            a = acc[...]
            for h, msk in enumerate(_head_masks()):
                st_sc, pt_sc, bias_sc = ((s0, p0, b0), (s1, p1, b1))[h]
                st_sc[...] = _dot(kk, jnp.where(msk, qv, jnp.zeros_like(qv)), 1, 1)
                bias_sc[...] = jnp.broadcast_to(cq_ref[0, h:h + 1, 0:1] - ck_ref[0, :, h:h + 1], (tk, LANES))
                m_old, l_old = ms[h][...], ls[h][...]
                col_max = jnp.full((SUBLANES, tq), -jnp.inf, F32)
                for r in range(tk // rt):
                    rows = slice(r * rt, (r + 1) * rt)
                    s = st_sc[rows, :] + jnp.tile(bias_sc[rows, :], reps)
                    if diagonal:
                        key = r * rt + lax.broadcasted_iota(jnp.int32, (rt, tq), 0)
                        qry = lax.broadcasted_iota(jnp.int32, (rt, tq), 1)
                        s = jnp.where(key <= qry, s, -jnp.inf)
                    st_sc[rows, :] = s
                    for g in range(rt // SUBLANES):
                        col_max = jnp.maximum(col_max, s[g * SUBLANES:(g + 1) * SUBLANES, :])
                m_new = jnp.maximum(m_old, jnp.max(col_max, axis=0, keepdims=True))
                col_sum = jnp.zeros((SUBLANES, tq), F32)
                for r in range(tk // rt):
                    rows = slice(r * rt, (r + 1) * rt)
                    p = jnp.exp(st_sc[rows, :] - m_new)
                    for g in range(rt // SUBLANES):
                        col_sum = col_sum + p[g * SUBLANES:(g + 1) * SUBLANES, :]
                    pt_sc[rows, :] = p.astype(BF16)
                alpha = jnp.exp(m_old - m_new)
                ms[h][...] = m_new
                ls[h][...] = alpha * l_old + jnp.sum(col_sum, axis=0, keepdims=True)
                pv_t = _dot(jnp.where(msk, vv, jnp.zeros_like(vv)), pt_sc[...], 0, 0)
                a = a * jnp.where(head_rows == (h == 0), alpha, 1.0) + pv_t
            acc[...] = a

        pl.when(j < i)(functools.partial(block, False))
        pl.when(j == i)(functools.partial(block, True))

        @pl.when(j == i)
        def _():
            ot_ref[...] = acc[...] * jnp.where(head_rows, 1.0 / l0[...], 1.0 / l1[...])
            lse_ref[0] = jnp.concatenate([m0[...] + jnp.log(l0[...]), m1[...] + jnp.log(l1[...])], axis=0)

    pairs = [(i, j) for i in range(nq) for j in range(i + 1)]
    qi = jnp.asarray([i for i, _ in pairs], jnp.int32)
    kj = jnp.asarray([j for _, j in pairs], jnp.int32)
    stat = pltpu.VMEM((1, tq), F32)
    return _call(
        body, grid=(nhp, len(pairs)), prefetch=(qi, kj),
        in_specs=[pl.BlockSpec((tq, LANES), lambda hp, t, qi, kj: (qi[t], hp)),
                  pl.BlockSpec((tk, LANES), lambda hp, t, qi, kj: (kj[t], hp)),
                  pl.BlockSpec((tk, LANES), lambda hp, t, qi, kj: (kj[t], nhp + hp)),
                  pl.BlockSpec((1, 2, tq), lambda hp, t, qi, kj: (hp, 0, qi[t])),
                  pl.BlockSpec((1, tk, 2), lambda hp, t, qi, kj: (hp, kj[t], 0))],
        out_specs=[pl.BlockSpec((LANES, tq), lambda hp, t, qi, kj: (hp, qi[t])),
                   pl.BlockSpec((1, 2, tq), lambda hp, t, qi, kj: (hp, 0, qi[t]))],
        scratch_shapes=[stat, stat, stat, stat, pltpu.VMEM((LANES, tq), F32), pltpu.VMEM((tk, tq), F32),
                        pltpu.VMEM((tk, tq), F32), pltpu.VMEM((tk, tq), BF16), pltpu.VMEM((tk, tq), BF16),
                        pltpu.VMEM((tk, LANES), F32), pltpu.VMEM((tk, LANES), F32)],
        out_shape=[jax.ShapeDtypeStruct((d, n_rows), F32), jax.ShapeDtypeStruct((nhp, 2, n_rows), F32)],
        args=(q, kv, kv, cum_r, cum_c), name=name, comm=comm)


def _head_delta(do, o, *, name):
    d = o.shape[1]

    def fn(dd, oo):
        prod = dd.astype(BF16).astype(F32) * oo
        r = lax.broadcasted_iota(jnp.int32, (d, LANES), 0)
        c = lax.broadcasted_iota(jnp.int32, (d, LANES), 1)
        return _tri_dot(prod, jnp.where(jnp.right_shift(r, HEAD_DIM.bit_length() - 1) == c, 1.0, 0.0))

    return _rowwise(fn, [do, o], [], [(LANES, F32)], name=name)[0]


def _flash_bwd(q, kv, lse_r, delta_r, do, cum_c, cum_r, *, tq, name, comm=None):
    n_rows, d = q.shape
    nhp = d // LANES
    tk = tq
    nq = n_rows // tq
    rt = _tile(tk, FLASH_ROW_TILE)
    reps = (1, tq // LANES)

    def body(qi_ref, kj_ref, q_ref, k_ref, v_ref, lse_ref, dl_ref, do_ref, cq_ref, ck_ref,
             dq_ref, dk_ref, dv_ref, dck_ref, dcq_ref, s0, dp0, p0, ds0, b0, b1, ck0, ck1):
        s1, dp1, p1, ds1 = s0, dp0, p0, ds0
        i = qi_ref[pl.program_id(1)]
        j = kj_ref[pl.program_id(1)]

        @pl.when(pl.program_id(1) == 0)
        def _():
            dq_ref[...] = jnp.zeros_like(dq_ref)
            dcq_ref[...] = jnp.zeros_like(dcq_ref)

        @pl.when(i == j)
        def _():
            for ref in (dk_ref, dv_ref, ck0, ck1):
                ref[...] = jnp.zeros_like(ref)

        def block(diagonal):
            qv, kk, vv = q_ref[...], k_ref[...], v_ref[...]
            dob = do_ref[...].astype(BF16)
            dq_acc = jnp.zeros((tq, LANES), F32)
            dk_acc = jnp.zeros((tk, LANES), F32)
            dv_acc = jnp.zeros((tk, LANES), F32)
            query_sums = []
            for h, msk in enumerate(_head_masks()):
                st_sc, dpt_sc, pt_sc, dst_sc, bias_sc, key_part = ((s0, dp0, p0, ds0, b0, ck0),
                                                                  (s1, dp1, p1, ds1, b1, ck1))[h]
                qh = jnp.where(msk, qv, jnp.zeros_like(qv))
                kh = jnp.where(msk, kk, jnp.zeros_like(kk))
                doh = jnp.where(msk, dob, jnp.zeros_like(dob))
                st_sc[...] = _dot(kk, qh, 1, 1)
                dpt_sc[...] = _dot(vv, doh, 1, 1)
                bias_sc[...] = jnp.broadcast_to(cq_ref[0, h:h + 1, 0:1] - ck_ref[0, :, h:h + 1], (tk, LANES))
                lse_row = lse_ref[0, h:h + 1, :]
                delta_row = dl_ref[0, h:h + 1, :]
                col_acc = jnp.zeros((SUBLANES, tq), F32)
                parts = []
                for r in range(tk // rt):
                    rows = slice(r * rt, (r + 1) * rt)
                    s = st_sc[rows, :] + jnp.tile(bias_sc[rows, :], reps)
                    if diagonal:
                        key = r * rt + lax.broadcasted_iota(jnp.int32, (rt, tq), 0)
                        qry = lax.broadcasted_iota(jnp.int32, (rt, tq), 1)
                        s = jnp.where(key <= qry, s, -jnp.inf)
                    p = jnp.exp(s - lse_row)
                    ds = p * (dpt_sc[rows, :] - delta_row)
                    for g in range(rt // SUBLANES):
                        col_acc = col_acc + ds[g * SUBLANES:(g + 1) * SUBLANES, :]
                    part = ds[:, 0:LANES]
                    for g in range(1, tq // LANES):
                        part = part + ds[:, g * LANES:(g + 1) * LANES]
                    parts.append(part)
                    pt_sc[rows, :] = p.astype(BF16)
                    dst_sc[rows, :] = ds.astype(BF16)
                key_part[...] += jnp.concatenate(parts, axis=0)
                query_sums.append(jnp.sum(col_acc, axis=0, keepdims=True))
                dv_acc = dv_acc + _dot(pt_sc[...], doh, 1, 0)
                dsb = dst_sc[...]
                dk_acc = dk_acc + _dot(dsb, qh, 1, 0)
                dq_acc = dq_acc + _dot(dsb, kh, 0, 0)
            off = pl.multiple_of(i * tq, tq)
            dq_ref[pl.ds(off, tq), :] += dq_acc
            dk_ref[...] += dk_acc
            dv_ref[...] += dv_acc
            dcq_ref[0, i] += jnp.concatenate(query_sums, axis=0)

        pl.when(i > j)(functools.partial(block, False))
        pl.when(i == j)(functools.partial(block, True))

        @pl.when(i == nq - 1)
        def _():
            two = lax.broadcasted_iota(jnp.int32, (tk, 2), 1)
            dck_ref[0] = jnp.where(two == 0, -jnp.sum(ck0[...], axis=1, keepdims=True),
                                   -jnp.sum(ck1[...], axis=1, keepdims=True))

    pairs = [(i, j) for j in range(nq) for i in range(j, nq)]
    qi = jnp.asarray([i for i, _ in pairs], jnp.int32)
    kj = jnp.asarray([j for _, j in pairs], jnp.int32)
    score = pltpu.VMEM((tk, tq), F32)
    score16 = pltpu.VMEM((tk, tq), BF16)
    keystat = pltpu.VMEM((tk, LANES), F32)
    return _call(
        body, grid=(nhp, len(pairs)), prefetch=(qi, kj),
        in_specs=[pl.BlockSpec((tq, LANES), lambda hp, t, qi, kj: (qi[t], hp)),
                  pl.BlockSpec((tk, LANES), lambda hp, t, qi, kj: (kj[t], hp)),
                  pl.BlockSpec((tk, LANES), lambda hp, t, qi, kj: (kj[t], nhp + hp)),
                  pl.BlockSpec((1, 2, tq), lambda hp, t, qi, kj: (hp, 0, qi[t])),
                  pl.BlockSpec((1, 2, tq), lambda hp, t, qi, kj: (hp, 0, qi[t])),
                  pl.BlockSpec((tq, LANES), lambda hp, t, qi, kj: (qi[t], hp)),
                  pl.BlockSpec((1, 2, tq), lambda hp, t, qi, kj: (hp, 0, qi[t])),
                  pl.BlockSpec((1, tk, 2), lambda hp, t, qi, kj: (hp, kj[t], 0))],
        out_specs=[pl.BlockSpec((n_rows, LANES), lambda hp, t, qi, kj: (0, hp)),
                   pl.BlockSpec((tk, LANES), lambda hp, t, qi, kj: (kj[t], hp)),
                   pl.BlockSpec((tk, LANES), lambda hp, t, qi, kj: (kj[t], hp)),
                   pl.BlockSpec((1, tk, 2), lambda hp, t, qi, kj: (hp, kj[t], 0)),
                   pl.BlockSpec((1, nq, 2, tq), lambda hp, t, qi, kj: (hp, 0, 0, 0))],
        scratch_shapes=[score, score, score16, score16, keystat, keystat, keystat, keystat],
        out_shape=[jax.ShapeDtypeStruct((n_rows, d), F32), jax.ShapeDtypeStruct((n_rows, d), F32),
                   jax.ShapeDtypeStruct((n_rows, d), F32), jax.ShapeDtypeStruct((nhp, n_rows, 2), F32),
                   jax.ShapeDtypeStruct((nhp, nq, 2, tq), F32)],
        args=(q, kv, kv, lse_r, delta_r, do, cum_r, cum_c), name=name, comm=comm)


def _s5_tables(w, layer):
    g, p = w["lam_re"].shape[1:]
    h = w["ssm_b_re"].shape[3]
    n = g * p
    lr = w["lam_re"][layer].reshape(1, n)
    li = w["lam_im"][layer].reshape(1, n)
    ldt = jnp.broadcast_to(w["log_dt"][layer][:, None], (g, p)).reshape(1, n)
    br = w["ssm_b_re"][layer].transpose(2, 0, 1).reshape(h, n)
    bi = w["ssm_b_im"][layer].transpose(2, 0, 1).reshape(h, n)
    cr = w["ssm_c_re"][layer].transpose(1, 0, 2).reshape(h, n)
    ci = w["ssm_c_im"][layer].transpose(1, 0, 2).reshape(h, n)
    return (lr, li, ldt, br, bi, cr, ci), (g, p, h)


def _local_step(x, tgt, w, net=None, *, attn_tile=1024):
    n_rows, d = x.shape
    n_layers = w["g_mix"].shape[0]
    n_s5 = w["lam_re"].shape[0]
    nh = w["b_f"].shape[0]
    nhp = nh // 2
    assert d == nh * HEAD_DIM
    tq = _tile(n_rows, attn_tile)
    g = {}
    saved = [dict() for _ in range(n_layers)]
    big = {}
    pending = {}

    def wt(name, layer):
        return w[name][layer]

    def carry_gather(group, run):
        if net is None or not net.has_group(group):
            return run(None)
        outs, got = run(net.gather_comm(group))
        net.store_gathered(group, got, w)
        return outs

    def carry_reduce(tag, run):
        if net is None or not pending:
            big.update(pending)
            pending.clear()
            return run(None)
        keys = list(pending)
        parts = net.reduce_prepare([pending[k] for k in keys], tag)
        outs, landed = run(_chip_exchange_comm(parts))
        big.update(zip(keys, net.reduce_finish(parts, landed, [pending[k] for k in keys], tag)))
        pending.clear()
        return outs

    h = x
    nxt = _rowwise(lambda a, gg: _rms(a, gg), [x], [_row2(w["g_mix"][0])], [(d, F32)], name="rms_first")[0]
    kvb = fl = cum = cq3 = ck3 = hnkv = None
    bf_pad = jnp.zeros((1, LANES), F32).at[0, :nh].set(w["b_f"])
    for l in range(n_layers):
        sv = saved[l]
        sv["h"] = h
        g_ffn = _row2(w["g_ffn"][l])
        if l < n_s5:
            tabs, (_, p, _) = _s5_tables(w, l)
            prep = _s5_prep(*tabs, p, name=f"s5_prep{l}")
            dskip = w["ssm_d"][l].reshape(1, d)
            y, st_re, st_im, sb_re, sb_im = carry_gather(
                f"stage{l}", lambda comm, u=nxt, pr=prep, ds=dskip: _s5_fwd(u, pr, ds, name=f"s5_fwd{l}", comm=comm))
            z = _rowwise(_gelu, [y], [], [(d, BF16)], name=f"gelu{l}")[0]
            zz = _mm_cols(z, *wt("w_glu", l), wc=0, name=f"glu_mm{l}")
            h1, hn2 = _rowwise(lambda hh, zq, gg: ((lambda t: (t, _rms(t, gg)))(hh + _glu(zq))),
                               [h, zz], [g_ffn], [(d, F32), (d, BF16)], name=f"mix_out{l}")
            sv.update(u=nxt, prep=prep, tabs=tabs, p=p, dskip=dskip, st_re=st_re, st_im=st_im, sb_re=sb_re,
                      sb_im=sb_im, y=y, z=z, zz=zz)
        else:
            j = l - n_s5
            qs = _mm_cols(nxt, *wt("w_q", j), wc=0, out_dtype=BF16, scale=HEAD_DIM ** -0.5, name=f"q_mm{j}")
            o_t, lse = carry_gather(
                f"stage{l}", lambda comm, q_=qs: _flash_fwd(q_, kvb, cq3, ck3, tq=tq, name=f"flash_fwd{j}", comm=comm))
            o = o_t.T
            h1, hn2 = _mm_cols(o, *wt("w_o", j), wc=0, name=f"o_mm{j}",
                               epilogue=(lambda aa, hh, gg: ((lambda t: (t, _rms(t, gg)))(hh + aa)), [h], [g_ffn],
                                         [(d, F32), (d, BF16)], ()))
            sv.update(hn=nxt, qs=qs, o=o, lse=lse)
        uu = _mm_cols(hn2, *wt("w_in", l), wc=0, name=f"ffn_in{l}")
        cw, cb = w["conv_w"][l], _row2(w["conv_b"][l])
        act = _conv_fwd(uu, cw, cb, name=f"conv_fwd{l}")
        sv.update(h1=h1, hn2=hn2, uu=uu, act=act, cw=cw, cb=cb)

        def ffn_out(fn, rows, consts, outs, accs=()):
            return _mm_cols(act, *wt("w_out", l), wc=0, name=f"ffn_out{l}", epilogue=(fn, rows, consts, outs, accs))

        if l == n_layers - 1:
            def loss_fn(ff, hh, tt, gg):
                yv, vjp = jax.vjp(_rms, hh + ff, gg)
                err = yv - tt
                part = 0.5 * jnp.sum(jnp.mean(err * err, axis=-1, keepdims=True), axis=0, keepdims=True)
                dh, dg = vjp(err * (1.0 / d))
                return dh, dh, jnp.broadcast_to(part, (1, LANES)), dg
            dcur, dcur16, loss_row, dgf = ffn_out(loss_fn, [h1, tgt], [_row2(w["g_final"])],
                                                  [(d, F32), (d, BF16)], [(1, LANES), (1, d)])
            loss = loss_row[0, 0]
            g["g_final"] = dgf[0]
        elif l + 1 < n_s5:
            h, nxt = ffn_out(lambda ff, hh, gg: ((lambda t: (t, _rms(t, gg)))(hh + ff)), [h1],
                             [_row2(w["g_mix"][l + 1])], [(d, F32), (d, F32)])
        elif l + 1 == n_s5:
            h, nxt, hnkv = ffn_out(lambda ff, hh, g1, g2: ((lambda t: (t, _rms(t, g1), _rms(t, g2)))(hh + ff)), [h1],
                                   [_row2(w["g_mix"][l + 1]), _row2(w["g_kv"])],
                                   [(d, F32), (d, BF16), (d, BF16)])
            kvb = _mm_cols(hnkv, *wt("w_kv", 0), wc=0, out_dtype=BF16, name="kv_mm")
            fl = _mm_cols(hnkv, *wt("w_f", 0), wc=0, name="f_mm")
            cum = _cum_fwd(fl, bf_pad, name="cum_fwd")
            cq3 = cum[:, :nh].reshape(n_rows, nhp, 2).transpose(1, 0, 2)
            ck3 = cum[:, :nh].T.reshape(nhp, 2, n_rows)
        else:
            h, nxt = ffn_out(lambda ff, hh, gg: ((lambda t: (t, _rms(t, gg)))(hh + ff)), [h1],
                             [_row2(w["g_mix"][l + 1])], [(d, F32), (d, BF16)])

    per_layer = {k: [None] * n_layers for k in ("g_mix", "g_ffn", "conv_w", "conv_b")}
    per_s5 = {k: [None] * n_s5 for k in ("lam_re", "lam_im", "log_dt", "ssm_b_re", "ssm_b_im", "ssm_c_re",
                                         "ssm_c_im", "ssm_d")}
    dk_parts, dv_parts, dck_parts = [], [], []

    def by_row_shard(m):
        return m.reshape(N_CHIPS, m.shape[0] // N_CHIPS, m.shape[1])

    for l in reversed(range(n_layers)):
        sv = saved[l]
        dact = _mm_cols(dcur16, *wt("w_out", l), wc=1, name=f"ffn_out_dx{l}")
        pending["w_ffn_out", l] = by_row_shard(_mm_tn(sv["act"], dcur16, 1, name=f"ffn_out_dw{l}")[0])
        duu, dcw, dcb = _conv_bwd(sv["uu"], dact, sv["cw"], sv["cb"], name=f"conv_bwd{l}")
        per_layer["conv_w"][l] = jnp.concatenate([dcw[0], dcw[1]], axis=-1)
        per_layer["conv_b"][l] = jnp.concatenate([dcb[0, 0], dcb[1, 0]])
        node = _node_bwd_fn([1])
        d1, d1_16, dg = _mm_acc(duu, *wt("w_in", l), wc=1, name=f"ffn_in_dx{l}",
                                epilogue=(lambda dhn2, dd, hh, gg: node(dd, hh, dhn2, gg), [dcur, sv["h1"]],
                                          [_row2(w["g_ffn"][l])], [(d, F32), (d, BF16)], [(1, d)]))
        pending["w_ffn_in", l] = _mm_tn(sv["hn2"], duu, wt("w_in", l)[0].shape[0], name=f"ffn_in_dw{l}")
        per_layer["g_ffn"][l] = dg[0]
        if l < n_s5:
            def glu_bwd(zq, dd):
                _, vjp = jax.vjp(_glu, zq)
                return vjp(dd)[0]
            dzz = _rowwise(glu_bwd, [sv["zz"], d1], [], [(2 * d, BF16)], name=f"glu_bwd{l}")[0]
            dz = _mm_acc(dzz, *wt("w_glu", l), wc=1, name=f"glu_dx{l}")
            pending["w_glu", l] = _mm_tn(sv["z"], dzz, wt("w_glu", l)[0].shape[0], name=f"glu_dw{l}")

            def gelu_bwd(yy, dd):
                _, vjp = jax.vjp(_gelu, yy)
                return vjp(dd)[0]
            dy = _rowwise(gelu_bwd, [sv["y"], dz], [], [(d, F32)], name=f"gelu_bwd{l}")[0]
            du, dwbr, dwbi, dwcr, dwci, dlbr, dlbi, dd = carry_reduce(
                f"stage{l}", lambda comm, dy_=dy: _s5_bwd(sv["u"], dy_, sv["st_re"], sv["st_im"], sv["sb_re"],
                                                          sv["sb_im"], sv["prep"], sv["dskip"], name=f"s5_bwd{l}",
                                                          comm=comm))
            dlr, dli, dldt, dbr, dbi, dcr, dci = _s5_prep_bwd(*sv["tabs"], sv["p"], dlbr, dlbi, dwbr, dwbi, dwcr,
                                                              dwci, name=f"s5_prep_bwd{l}")
            gg, p = w["lam_re"].shape[1:]
            hh = w["ssm_b_re"].shape[3]
            per_s5["lam_re"][l] = dlr.reshape(gg, p)
            per_s5["lam_im"][l] = dli.reshape(gg, p)
            per_s5["log_dt"][l] = dldt.reshape(gg, p).sum(axis=1)
            per_s5["ssm_b_re"][l] = dbr.reshape(hh, gg, p).transpose(1, 2, 0)
            per_s5["ssm_b_im"][l] = dbi.reshape(hh, gg, p).transpose(1, 2, 0)
            per_s5["ssm_c_re"][l] = dcr.reshape(hh, gg, p).transpose(1, 0, 2)
            per_s5["ssm_c_im"][l] = dci.reshape(hh, gg, p).transpose(1, 0, 2)
            per_s5["ssm_d"][l] = dd.reshape(d)
            branches = [(w["g_mix"][l], [du])]
        else:
            j = l - n_s5
            do = _mm_cols(d1_16, *wt("w_o", j), wc=1, name=f"o_dx{j}")
            pending["w_o", j] = by_row_shard(_mm_tn(sv["o"], d1_16, 1, name=f"o_dw{j}")[0])
            delta_r = _head_delta(do, sv["o"], name=f"head_delta{j}")[:, :nh].T.reshape(nhp, 2, n_rows)
            dq, dk, dv, dck, dcq = carry_reduce(
                f"stage{l}", lambda comm, do_=do: _flash_bwd(sv["qs"], kvb, sv["lse"], delta_r, do_, cq3, ck3, tq=tq,
                                                          name=f"flash_bwd{j}", comm=comm))
            dk_parts.append(dk)
            dv_parts.append(dv)
            dck_parts.append(dck.transpose(1, 0, 2).reshape(n_rows, nh)
                             + dcq.transpose(0, 2, 1, 3).reshape(nh, n_rows).T)
            scale = HEAD_DIM ** -0.5
            pending["w_q", j] = by_row_shard(_mm_tn(sv["hn"], dq, 1, scale=scale, name=f"q_dw{j}")[0])
            branches = []
            if j == 0:
                def kv_sum(*parts):
                    half = len(parts) // 2
                    return jnp.concatenate([sum(parts[:half][1:], parts[0]),
                                            sum(parts[half:][1:], parts[half])], axis=1)
                dkv = _rowwise(kv_sum, dk_parts + dv_parts, [], [(2 * d, BF16)], name="dkv_sum")[0]
                dck_tot = dck_parts[0]
                for extra in dck_parts[1:]:
                    dck_tot = dck_tot + extra
                dcum = jnp.zeros((n_rows, LANES), F32).at[:, :nh].set(dck_tot)
                dfl, dbf = _cum_bwd(dcum, fl, bf_pad, name="cum_bwd")
                g["b_f"] = dbf[0, :nh]
                dhkv_a = _mm_cols(dkv, *wt("w_kv", 0), wc=1, name="kv_dx")
                dhkv_b = _mm_cols(dfl, *wt("w_f", 0), wc=1, name="f_dx")
                d_kvf = jnp.concatenate([_mm_tn(hnkv, dkv, 1, name="kv_dw")[0],
                                         _mm_tn(hnkv, dfl, 1, name="f_dw")[0][:, :nh]], axis=1)
                pending["w_kvf", 0] = d_kvf.reshape(d, N_CHIPS, -1).transpose(1, 0, 2)
                branches.append((w["g_kv"], [dhkv_a, dhkv_b]))
            node = _node_bwd_fn([1] + [len(dys) for _, dys in branches])
            gains = [_row2(w["g_mix"][l])] + [_row2(gn) for gn, _ in branches]
            res = _mm_cols(dq, *wt("w_q", j), wc=1, scale=scale, name=f"q_dx{j}",
                           epilogue=(lambda dhn, dd, hh, *rest: node(dd, hh, dhn, *rest),
                                     [d1, sv["h"], *[dy for _, dys in branches for dy in dys]], gains,
                                     [(d, F32), (d, BF16)], [(1, d)] * len(gains)))
            dcur, dcur16, dgs = res[0], res[1], [r[0] for r in res[2:]]
        if l < n_s5:
            dcur, dcur16, dgs = _node_bwd(d1, sv["h"], branches, name=f"mix_norm_bwd{l}")
        per_layer["g_mix"][l] = dgs[0]
        if len(dgs) > 1:
            g["g_kv"] = dgs[1]

    if pending:
        big.update(pending if net is None else
                   zip(list(pending), net.reduce_blocking([pending[k] for k in pending], "tail")))
    for k, v in (*per_layer.items(), *per_s5.items()):
        g[k] = jnp.stack(v)
    g["big"] = big
    return loss, dcur, g


def _position():
    x, y, c = lax.axis_index("x"), lax.axis_index("y"), lax.axis_index("c")
    chips = [(1 - x, y), (x, 1 - y), (1 - x, 1 - y)]
    return x, y, c, chips


def _all_gather_comm(shards):
    n = len(shards)

    def descriptors(ins, outs, sems):
        send_sems, recv_sems = sems
        x, y, c, chips = _position()
        my_slot = 2 * x + y
        sibling = (x, y, 1 - c)

        def rows(t, half):
            hr = ins[t].shape[0] // 2
            return pl.ds(half * hr, hr)

        def remote(k, t, src, dst, to):
            return pltpu.make_async_remote_copy(src_ref=src, dst_ref=dst, send_sem=send_sems.at[k, t],
                                                recv_sem=recv_sems.at[k, t], device_id=to, device_id_type=MESH)

        own = [remote(6, t, ins[t], outs[t].at[my_slot], sibling) for t in range(n)]
        ici = [remote(j, t, ins[t].at[rows(t, c)], outs[t].at[my_slot, rows(t, c)], (*chip, c))
               for j, chip in enumerate(chips) for t in range(n)]
        slots = [2 * chip[0] + chip[1] for chip in chips]
        fwd = [[remote(3 + j, t, outs[t].at[slots[j], rows(t, c)], outs[t].at[slots[j], rows(t, c)], sibling)
                for t in range(n)] for j in range(len(chips))]
        landed = [[remote(j, t, outs[t].at[slots[j], rows(t, c)], outs[t].at[slots[j], rows(t, c)], (*chips[j], c))
                   for t in range(n)] for j in range(len(chips))]
        from_sibling = [remote(3 + j, t, outs[t].at[slots[j], rows(t, 1 - c)], outs[t].at[slots[j], rows(t, 1 - c)],
                               sibling) for j in range(len(chips)) for t in range(n)]
        return own, ici, fwd, landed, from_sibling

    def start(ins, outs, sems):
        own, ici, _, _, _ = descriptors(ins, outs, sems)
        for cp in own + ici:
            cp.start()

    def finish(ins, outs, sems):
        own, ici, fwd, landed, from_sibling = descriptors(ins, outs, sems)
        for j in range(len(fwd)):
            for cp in landed[j]:
                cp.wait_recv()
            for cp in fwd[j]:
                cp.start()
        for cp in from_sibling + own:
            cp.wait_recv()
        for cp in own + ici + [cp for group in fwd for cp in group]:
            cp.wait_send()

    return _Comm(list(shards), [jax.ShapeDtypeStruct((N_CHIPS,) + a.shape, a.dtype) for a in shards],
                 [pltpu.SemaphoreType.DMA((7, n)), pltpu.SemaphoreType.DMA((7, n))], start, finish)


def _all_gather(shards, *, name):
    return _run_comm(_all_gather_comm(shards), name=name)


def _pair_exchange(grads, *, name):
    n = len(grads)

    def body(*refs):
        ins, outs = refs[:n], refs[n:2 * n]
        send_sems, recv_sems = refs[2 * n:]
        x, y, c, _ = _position()
        copies = [pltpu.make_async_remote_copy(src_ref=ins[t].at[:, 1 - c], dst_ref=outs[t],
                                               send_sem=send_sems.at[t], recv_sem=recv_sems.at[t],
                                               device_id=(x, y, 1 - c), device_id_type=MESH) for t in range(n)]
        for cp in copies:
            cp.start()
        for cp in copies:
            cp.wait()

    return pl.pallas_call(
        body, in_specs=_any_specs(n), out_specs=_any_specs(n),
        out_shape=[jax.ShapeDtypeStruct((a.shape[0],) + a.shape[2:], a.dtype) for a in grads],
        scratch_shapes=[pltpu.SemaphoreType.DMA((n,)), pltpu.SemaphoreType.DMA((n,))], name=name)(*grads)


def _chip_exchange_comm(parts):
    n = len(parts)

    def copies(ins, outs, sems):
        send_sems, recv_sems = sems
        _, _, c, chips = _position()
        return [pltpu.make_async_remote_copy(src_ref=ins[t].at[2 * chip[0] + chip[1]], dst_ref=outs[t].at[j],
                                             send_sem=send_sems.at[j, t], recv_sem=recv_sems.at[j, t],
                                             device_id=(*chip, c), device_id_type=MESH)
                for j, chip in enumerate(chips) for t in range(n)]

    def start(ins, outs, sems):
        for cp in copies(ins, outs, sems):
            cp.start()

    def finish(ins, outs, sems):
        for cp in copies(ins, outs, sems):
            cp.wait()

    return _Comm(list(parts), [jax.ShapeDtypeStruct((N_CHIPS - 1,) + a.shape[1:], a.dtype) for a in parts],
                 [pltpu.SemaphoreType.DMA((N_CHIPS - 1, n)), pltpu.SemaphoreType.DMA((N_CHIPS - 1, n))],
                 start, finish)


def _pair_share(both, *, name):
    n = len(both)

    def body(*refs):
        ins, outs = refs[:n], refs[n:2 * n]
        send_sems, recv_sems = refs[2 * n:]
        x, y, c, _ = _position()
        for t in range(n):
            pltpu.make_async_remote_copy(src_ref=ins[t].at[c], dst_ref=outs[t].at[c], send_sem=send_sems.at[t],
                                         recv_sem=recv_sems.at[t], device_id=(x, y, 1 - c),
                                         device_id_type=MESH).start()
        for t in range(n):
            pltpu.make_async_remote_copy(src_ref=ins[t].at[c], dst_ref=outs[t].at[1 - c], send_sem=send_sems.at[t],
                                         recv_sem=recv_sems.at[t], device_id=(x, y, 1 - c),
                                         device_id_type=MESH).wait()

    return pl.pallas_call(
        body, in_specs=_any_specs(n), out_specs=_any_specs(n),
        out_shape=[jax.ShapeDtypeStruct(a.shape, a.dtype) for a in both],
        input_output_aliases={t: t for t in range(n)},
        scratch_shapes=[pltpu.SemaphoreType.DMA((n,)), pltpu.SemaphoreType.DMA((n,))], name=name)(*both)


def _sum_pair(grad, landed, c, wire_dtype, *, name):
    slots, _, m, n = grad.shape
    tm = _tile(m, 256, 2 * SUBLANES)

    def body(c_ref, g_ref, l_ref, o_ref):
        o_ref[...] = (g_ref[0] + l_ref[...]).astype(wire_dtype)

    return pl.pallas_call(
        body,
        grid_spec=pltpu.PrefetchScalarGridSpec(
            num_scalar_prefetch=1, grid=(slots, m // tm),
            in_specs=[pl.BlockSpec((1, 1, tm, n), lambda s, i, c_ref: (s, c_ref[0], i, 0)),
                      pl.BlockSpec((1, tm, n), lambda s, i, c_ref: (s, i, 0))],
            out_specs=pl.BlockSpec((1, tm, n), lambda s, i, c_ref: (s, i, 0))),
        out_shape=jax.ShapeDtypeStruct((slots, m, n), wire_dtype), compiler_params=_cp(2), name=name)(
            c, grad, landed)


def _sum_chips(part, landed, slot_c, *, name):
    _, m, n = part.shape
    tm = _tile(m, 256, 2 * SUBLANES)

    def body(s_ref, p_ref, l_ref, o_ref):
        acc = p_ref[0].astype(F32)
        for j in range(N_CHIPS - 1):
            acc = acc + l_ref[j].astype(F32)
        o_ref[0] = acc

    return pl.pallas_call(
        body,
        grid_spec=pltpu.PrefetchScalarGridSpec(
            num_scalar_prefetch=1, grid=(m // tm,),
            in_specs=[pl.BlockSpec((1, tm, n), lambda i, s_ref: (s_ref[0], i, 0)),
                      pl.BlockSpec((N_CHIPS - 1, tm, n), lambda i, s_ref: (0, i, 0))],
            out_specs=pl.BlockSpec((1, tm, n), lambda i, s_ref: (s_ref[1], i, 0))),
        out_shape=jax.ShapeDtypeStruct((N_CORES, m, n), F32), compiler_params=_cp(1), name=name)(
            slot_c, part, landed)


def _reduce_prepare(grads, wire_dtypes, tag):
    c = lax.axis_index("c").reshape(1).astype(jnp.int32)
    views = []
    for a in grads:
        lead, last = a.shape[1], a.shape[-1]
        mid = 1
        for s in a.shape[2:-1]:
            mid *= s
        views.append(a.reshape(N_CHIPS, N_CORES, (lead // N_CORES) * mid, last))
    landed = _pair_exchange(views, name=f"rs_pair_exchange_{tag}")
    return [_sum_pair(v, l, c, wire_dtypes[t], name=f"rs_pair_sum_{tag}_{t}")
            for t, (v, l) in enumerate(zip(views, landed))]


def _reduce_finish(parts, landed, grads, tag):
    slot_c = jnp.stack([2 * lax.axis_index("x") + lax.axis_index("y"), lax.axis_index("c")]).astype(jnp.int32)
    both = [_sum_chips(p, l, slot_c, name=f"rs_chip_sum_{tag}_{t}") for t, (p, l) in enumerate(zip(parts, landed))]
    full = _pair_share(both, name=f"rs_pair_share_{tag}")
    return [f.reshape(a.shape[1:]) for f, a in zip(full, grads)]


def _reduce_scatter(grads, wire_dtypes, tag):
    parts = _reduce_prepare(grads, wire_dtypes, tag)
    landed = _run_comm(_chip_exchange_comm(parts), name=f"rs_chip_exchange_{tag}")
    return _reduce_finish(parts, landed, grads, tag)


class _Net:
    def __init__(self, groups, d, nh):
        self.groups, self.d, self.nh = groups, d, nh

    def has_group(self, group):
        return bool(self.groups.get(group))

    def gather_comm(self, group):
        return _all_gather_comm([shard for _, _, shard in self.groups[group]])

    def store_gathered(self, group, got, w):
        d, nh = self.d, self.nh
        for (name, layer, _), full in zip(self.groups[group], got):
            if name == "w_kvf":
                mat = full.transpose(1, 0, 2).reshape(d, -1)
                w["w_kv"][0] = (mat[:, :2 * d][None, None], 0)
                w["w_f"][0] = (jnp.zeros((d, LANES), BF16).at[:, :nh].set(mat[:, 2 * d:])[None, None], 0)
            elif name in ("w_in", "w_glu"):
                w[name][layer] = (full[:, None], 0)
            else:
                w[name][layer] = (full.reshape(1, 1, -1, full.shape[-1]), 0)

    def reduce_prepare(self, grads, tag):
        return _reduce_prepare(grads, [BF16] * len(grads), tag)

    def reduce_finish(self, parts, landed, grads, tag):
        return _reduce_finish(parts, landed, grads, tag)

    def reduce_blocking(self, grads, tag):
        return _reduce_scatter(grads, [BF16] * len(grads), tag)


def _adamw(w, g, m, v, *, name):
    def fn(ww, gg, mm, vv):
        mm = ADAM_B1 * mm + (1.0 - ADAM_B1) * gg
        vv = ADAM_B2 * vv + (1.0 - ADAM_B2) * (gg * gg)
        m_hat = mm / (1.0 - ADAM_B1 ** ADAM_STEP)
        v_hat = vv / (1.0 - ADAM_B2 ** ADAM_STEP)
        delta = -ADAM_LR * (m_hat / (jnp.sqrt(v_hat) + ADAM_EPS) + ADAM_WD * ww)
        return delta, mm, vv

    shape = w.shape
    two_d = [a.reshape(-1, shape[-1]) for a in (w, g, m, v)]
    outs = _rowwise(fn, two_d, [], [(shape[-1], F32)] * 3, name=name)
    return [o.reshape(shape) for o in outs]


def _to_bf16(a, *, name):
    two_d = a.reshape(-1, a.shape[-1])
    return _rowwise(lambda t: t, [two_d], [], [(a.shape[-1], BF16)], name=name)[0].reshape(a.shape)


def _pack(arrays, rows_multiple):
    flat = jnp.concatenate([a.reshape(-1) for a in arrays])
    rows = -(-flat.shape[0] // LANES)
    rows = -(-rows // rows_multiple) * rows_multiple
    return jnp.pad(flat, (0, rows * LANES - flat.shape[0])).reshape(rows, LANES)


def _unpack(packed, like):
    flat = packed.reshape(-1)
    out, pos = [], 0
    for a in like:
        out.append(flat[pos:pos + a.size].reshape(a.shape))
        pos += a.size
    return out


_PARAMS = ("g_mix", "g_ffn", "lam_re", "lam_im", "log_dt", "ssm_b_re", "ssm_b_im", "ssm_c_re", "ssm_c_im", "ssm_d",
           "w_glu", "g_kv", "w_kvf", "b_f", "w_q", "w_o", "w_ffn_in", "ffn_conv_w", "ffn_conv_b", "w_ffn_out",
           "g_final")
_BIG = ("w_glu", "w_kvf", "w_q", "w_o", "w_ffn_in", "w_ffn_out")
_SMALL_SHARDED = ("ssm_d", "ffn_conv_w")


def kernel(x, g_mix, g_ffn, lam_re, lam_im, log_dt, ssm_b_re, ssm_b_im, ssm_c_re, ssm_c_im, ssm_d, w_glu, g_kv, w_kvf, b_f, w_q, w_o, w_ffn_in, ffn_conv_w, ffn_conv_b, w_ffn_out, g_final, loss_target, m_g_mix, m_g_ffn, m_lam_re, m_lam_im, m_log_dt, m_ssm_b_re, m_ssm_b_im, m_ssm_c_re, m_ssm_c_im, m_ssm_d, m_w_glu, m_g_kv, m_w_kvf, m_b_f, m_w_q, m_w_o, m_w_ffn_in, m_ffn_conv_w, m_ffn_conv_b, m_w_ffn_out, m_g_final, v_g_mix, v_g_ffn, v_lam_re, v_lam_im, v_log_dt, v_ssm_b_re, v_ssm_b_im, v_ssm_c_re, v_ssm_c_im, v_ssm_d, v_w_glu, v_g_kv, v_w_kvf, v_b_f, v_w_q, v_w_o, v_w_ffn_in, v_ffn_conv_w, v_ffn_conv_b, v_w_ffn_out, v_g_final):
    p = dict(g_mix=g_mix, g_ffn=g_ffn, lam_re=lam_re, lam_im=lam_im, log_dt=log_dt, ssm_b_re=ssm_b_re,
             ssm_b_im=ssm_b_im, ssm_c_re=ssm_c_re, ssm_c_im=ssm_c_im, ssm_d=ssm_d, w_glu=w_glu, g_kv=g_kv,
             w_kvf=w_kvf, b_f=b_f, w_q=w_q, w_o=w_o, w_ffn_in=w_ffn_in, ffn_conv_w=ffn_conv_w,
             ffn_conv_b=ffn_conv_b, w_ffn_out=w_ffn_out, g_final=g_final)
    mom1 = dict(zip(_PARAMS, (m_g_mix, m_g_ffn, m_lam_re, m_lam_im, m_log_dt, m_ssm_b_re, m_ssm_b_im, m_ssm_c_re,
                              m_ssm_c_im, m_ssm_d, m_w_glu, m_g_kv, m_w_kvf, m_b_f, m_w_q, m_w_o, m_w_ffn_in,
                              m_ffn_conv_w, m_ffn_conv_b, m_w_ffn_out, m_g_final)))
    mom2 = dict(zip(_PARAMS, (v_g_mix, v_g_ffn, v_lam_re, v_lam_im, v_log_dt, v_ssm_b_re, v_ssm_b_im, v_ssm_c_re,
                              v_ssm_c_im, v_ssm_d, v_w_glu, v_g_kv, v_w_kvf, v_b_f, v_w_q, v_w_o, v_w_ffn_in,
                              v_ffn_conv_w, v_ffn_conv_b, v_w_ffn_out, v_g_final)))
    d = x.shape[-1]
    nh = b_f.shape[0]
    slot = 2 * lax.axis_index("x") + lax.axis_index("y")

    wb = {k: _to_bf16(p[k], name=f"to_bf16_{k}") for k in _BIG}
    gd, gcw = _all_gather([ssm_d, ffn_conv_w], name="first_all_gather")
    n_lay, n_s5 = w_ffn_in.shape[0], lam_re.shape[0]
    n_fox = n_lay - n_s5
    groups = {f"stage{l}": [("w_in", l, wb["w_ffn_in"][l]), ("w_out", l, wb["w_ffn_out"][l])] for l in range(n_lay)}
    for l in range(n_s5):
        groups[f"stage{l}"].append(("w_glu", l, wb["w_glu"][l]))
    groups[f"stage{n_s5 - 1}"] += [("w_kvf", 0, wb["w_kvf"])] + [(k, j, wb[k][j]) for k in ("w_q", "w_o")
                                                                for j in range(n_fox)]
    w = dict(p)
    w.update(w_glu=[None] * n_s5, w_in=[None] * n_lay, w_out=[None] * n_lay, w_q=[None] * n_fox,
             w_o=[None] * n_fox, w_kv=[None], w_f=[None],
             conv_w=gcw.transpose(1, 2, 0, 3).reshape(n_lay, CONV_TAPS, -1), conv_b=ffn_conv_b,
             ssm_d=gd.transpose(1, 0, 2).reshape(gd.shape[1], d))

    loss_part, grad_x, g = _local_step(x[0], loss_target[0], w, _Net(groups, d, nh))
    loss = lax.psum(loss_part, ("x", "y", "c"))

    small_names = [k for k in _PARAMS if k not in _BIG]
    small_full = dict(g_mix=g["g_mix"], g_ffn=g["g_ffn"], lam_re=g["lam_re"], lam_im=g["lam_im"], log_dt=g["log_dt"],
                      ssm_b_re=g["ssm_b_re"], ssm_b_im=g["ssm_b_im"], ssm_c_re=g["ssm_c_re"], ssm_c_im=g["ssm_c_im"],
                      ssm_d=g["ssm_d"], g_kv=g["g_kv"], b_f=g["b_f"], ffn_conv_w=g["conv_w"],
                      ffn_conv_b=g["conv_b"], g_final=g["g_final"])
    small_list = [small_full[k] for k in small_names]
    pack = _pack(small_list, N_CHIPS * N_CORES * 2 * SUBLANES)
    pack4 = pack.reshape(N_CHIPS, pack.shape[0] // N_CHIPS, LANES)
    pack_shard = _reduce_scatter([pack4], [F32], "small")[0]
    red_big = {k: g["big"][k, 0] if p[k].ndim == 2 else jnp.stack([g["big"][k, l] for l in range(p[k].shape[0])])
               for k in _BIG}
    pack_all = _all_gather([pack_shard], name="small_grads_all_gather")[0]
    red_small = dict(zip(small_names, _unpack(pack_all, small_list)))
    for k in _SMALL_SHARDED:
        width = p[k].shape[-1]
        red_small[k] = lax.dynamic_slice_in_dim(red_small[k], slot * width, width, axis=red_small[k].ndim - 1)

    grads, deltas, new_m, new_v = {}, {}, {}, {}
    for k in _BIG:
        grads[k] = red_big[k]
        deltas[k], new_m[k], new_v[k] = _adamw(p[k], grads[k], mom1[k], mom2[k], name=f"adamw_{k}")
    packs = [_pack([src[k] for k in small_names], SUBLANES) for src in (p, red_small, mom1, mom2)]
    like = [p[k] for k in small_names]
    outs = [_unpack(o, like) for o in _adamw(*packs, name="adamw_small")]
    for i, k in enumerate(small_names):
        grads[k] = red_small[k]
        deltas[k], new_m[k], new_v[k] = outs[0][i], outs[1][i], outs[2][i]
    return (loss, grad_x[None], *[grads[k] for k in _PARAMS], *[deltas[k] for k in _PARAMS],
            *[new_m[k] for k in _PARAMS], *[new_v[k] for k in _PARAMS])
```

```python
import functools

import jax
import jax.numpy as jnp
from jax import lax
from jax.experimental import pallas as pl
from jax.experimental.pallas import tpu as pltpu

F32 = jnp.float32
BF16 = jnp.bfloat16

RMS_EPS = 1e-6
ADAM_LR = 0.001
ADAM_B1 = 0.9
ADAM_B2 = 0.999
ADAM_EPS = 1e-08
ADAM_WD = 0.01
ADAM_STEP = 10
CONV_TAPS = 3

LANES = 128
SUBLANES = 8
HEAD_DIM = 64
FLASH_ROW_TILE = 32
S5_BLOCK_GROUPS = 16
VMEM_LIMIT_BYTES = 48 << 20
MM_BLOCK_BUDGET_BYTES = 30 << 20
N_CHIPS = 4
N_CORES = 2
MESH = pl.DeviceIdType.MESH


def _cp(n_grid):
    return pltpu.CompilerParams(dimension_semantics=("arbitrary",) * n_grid, vmem_limit_bytes=VMEM_LIMIT_BYTES)


def _tile(n, pref, mult=SUBLANES):
    if n <= pref:
        return n
    t = (pref // mult) * mult
    while t >= mult:
        if n % t == 0:
            return t
        t -= mult
    return n


class _Comm:
    def __init__(self, ins, out_shapes, sems, start, finish):
        self.ins, self.out_shapes, self.sems, self.start, self.finish = ins, out_shapes, sems, start, finish


def _any_specs(n):
    return [pl.BlockSpec(memory_space=pl.ANY)] * n


def _run_comm(comm, *, name):
    n_in, n_out = len(comm.ins), len(comm.out_shapes)

    def body(*refs):
        ins, outs, sems = refs[:n_in], refs[n_in:n_in + n_out], refs[n_in + n_out:]
        comm.start(ins, outs, sems)
        comm.finish(ins, outs, sems)

    return pl.pallas_call(body, in_specs=_any_specs(n_in), out_specs=_any_specs(n_out),
                          out_shape=list(comm.out_shapes), scratch_shapes=list(comm.sems), name=name)(*comm.ins)


def _call(body, *, grid, in_specs, out_specs, out_shape, args, name, scratch_shapes=(), prefetch=(), comm=None):
    n_pre, n_in, n_out, n_scr = len(prefetch), len(in_specs), len(out_specs), len(scratch_shapes)
    in_specs, out_specs, out_shape = list(in_specs), list(out_specs), list(out_shape)
    scratch_shapes, args = list(scratch_shapes), list(args)
    kernel_body = body
    if comm is not None:
        n_cin, n_cout = len(comm.ins), len(comm.out_shapes)

        def kernel_body(*refs):
            pos = n_pre + n_in
            c_in = refs[pos:pos + n_cin]
            main_out = refs[pos + n_cin:pos + n_cin + n_out]
            pos += n_cin + n_out
            c_out = refs[pos:pos + n_cout]
            main_scr = refs[pos + n_cout:pos + n_cout + n_scr]
            sems = refs[pos + n_cout + n_scr:]
            ids = [pl.program_id(a) for a in range(len(grid))]
            first = functools.reduce(jnp.logical_and, [i == 0 for i in ids])
            last = functools.reduce(jnp.logical_and, [i == g - 1 for i, g in zip(ids, grid)])
            pl.when(first)(lambda: comm.start(c_in, c_out, sems))
            body(*refs[:n_pre + n_in], *main_out, *main_scr)
            pl.when(last)(lambda: comm.finish(c_in, c_out, sems))

        in_specs += _any_specs(n_cin)
        out_specs += _any_specs(n_cout)
        out_shape += list(comm.out_shapes)
        scratch_shapes += list(comm.sems)
        args += list(comm.ins)
    if prefetch:
        spec = pltpu.PrefetchScalarGridSpec(num_scalar_prefetch=n_pre, grid=grid, in_specs=in_specs,
                                            out_specs=out_specs, scratch_shapes=scratch_shapes)
        res = pl.pallas_call(kernel_body, grid_spec=spec, out_shape=out_shape, compiler_params=_cp(len(grid)),
                             name=name)(*prefetch, *args)
    else:
        res = pl.pallas_call(kernel_body, grid=grid, in_specs=in_specs, out_specs=out_specs, out_shape=out_shape,
                             scratch_shapes=scratch_shapes, compiler_params=_cp(len(grid)), name=name)(*args)
    return (res[:n_out], res[n_out:]) if comm is not None else res


def _row_tile(m, bytes_per_row, fixed_bytes):
    for tm in (1024, 512):
        if m % tm == 0 and 2 * (tm * bytes_per_row + fixed_bytes) <= MM_BLOCK_BUDGET_BYTES:
            return tm
    return _tile(m, 512)


def _dot(a, b, ca, cb):
    return lax.dot_general(a, b, (((ca,), (cb,)), ((), ())), preferred_element_type=F32)


def _epilogue_io(epilogue, m, tm, rows_axis, grid_rank):
    _, rows, consts, outs, accs = epilogue

    def at_rows(width):
        return pl.BlockSpec((tm, width), lambda *g: (g[rows_axis], 0))

    def whole(shape):
        return pl.BlockSpec(shape, lambda *g: (0,) * len(shape))

    in_specs = [at_rows(r.shape[1]) for r in rows] + [whole(c.shape) for c in consts]
    out_specs = [at_rows(wd) for wd, _ in outs] + [whole(s) for s in accs]
    out_shape = ([jax.ShapeDtypeStruct((m, wd), dt) for wd, dt in outs]
                 + [jax.ShapeDtypeStruct(s, F32) for s in accs])
    bytes_per_row = (sum(r.shape[1] * r.dtype.itemsize for r in rows)
                     + sum(wd * jnp.dtype(dt).itemsize for wd, dt in outs))
    return in_specs, out_specs, out_shape, bytes_per_row


def _epilogue_apply(epilogue, block, refs, first_row_tile):
    fn, rows, consts, outs, _ = epilogue
    n_in, n_out = len(rows) + len(consts), len(outs)
    res = fn(block, *[r[...] for r in refs[:n_in]])
    for o, val in zip(refs[n_in:n_in + n_out], res[:n_out]):
        o[...] = val.astype(o.dtype)
    a_refs = refs[n_in + n_out:]
    if a_refs:
        @pl.when(first_row_tile)
        def _():
            for a in a_refs:
                a[...] = jnp.zeros_like(a)
        for a, val in zip(a_refs, res[n_out:]):
            a[...] += val


def _mm_cols(x, w4, layer, *, wc, out_dtype=F32, scale=None, epilogue=None, name):
    m, k = x.shape
    slots, _, k0, k1 = w4.shape
    nb = k1 if wc == 0 else k0
    assert (k0 if wc == 0 else k1) == k
    if epilogue is None:
        tm = _row_tile(m, k * x.dtype.itemsize + nb * jnp.dtype(out_dtype).itemsize, k0 * k1 * w4.dtype.itemsize)
        extra_in, out_specs = [], pl.BlockSpec((tm, nb), lambda s, i: (i, s))
        out_shape = jax.ShapeDtypeStruct((m, slots * nb), out_dtype)
    else:
        assert slots == 1
        bytes_per_row = _epilogue_io(epilogue, m, SUBLANES, 1, 2)[3]
        tm = _row_tile(m, k * x.dtype.itemsize + bytes_per_row, k0 * k1 * w4.dtype.itemsize)
        extra_in, out_specs, out_shape, _ = _epilogue_io(epilogue, m, tm, 1, 2)

    def body(x_ref, w_ref, *refs):
        acc = _dot(x_ref[...].astype(BF16), w_ref[0, 0], 1, wc)
        if scale is not None:
            acc = acc * scale
        if epilogue is None:
            refs[0][...] = acc.astype(out_dtype)
        else:
            _epilogue_apply(epilogue, acc, refs, pl.program_id(1) == 0)

    extra_args = [] if epilogue is None else [*epilogue[1], *epilogue[2]]
    return pl.pallas_call(
        body, grid=(slots, m // tm),
        in_specs=[pl.BlockSpec((tm, k), lambda s, i: (i, 0)),
                  pl.BlockSpec((1, 1, k0, k1), lambda s, i: (s, layer, 0, 0)), *extra_in],
        out_specs=out_specs, out_shape=out_shape,
        compiler_params=_cp(2), name=name)(x, w4, *extra_args)


def _planes(a):
    return a if a.ndim == 3 else a[None]


def _mm_acc(x, w4, layer, *, wc, epilogue=None, name):
    x = _planes(x)
    n_planes, m, width = x.shape
    slots, _, k0, k1 = w4.shape
    kb = k0 if wc == 0 else k1
    nout = k1 if wc == 0 else k0
    assert n_planes * width == slots * kb
    spp = slots // n_planes
    x_spec_w = pl.BlockSpec((1, 1, k0, k1), lambda i, s: (s, layer, 0, 0))
    if epilogue is None:
        tm = _row_tile(m, kb * x.dtype.itemsize + nout * 4, k0 * k1 * w4.dtype.itemsize)

        def body(x_ref, w_ref, o_ref):
            @pl.when(pl.program_id(1) == 0)
            def _():
                o_ref[...] = jnp.zeros_like(o_ref)
            o_ref[...] += _dot(x_ref[0].astype(BF16), w_ref[0, 0], 1, wc)

        return pl.pallas_call(
            body, grid=(m // tm, slots),
            in_specs=[pl.BlockSpec((1, tm, kb), lambda i, s: (s // spp, i, s % spp)), x_spec_w],
            out_specs=pl.BlockSpec((tm, nout), lambda i, s: (i, 0)),
            out_shape=jax.ShapeDtypeStruct((m, nout), F32),
            compiler_params=_cp(2), name=name)(x, w4)

    bytes_per_row = _epilogue_io(epilogue, m, SUBLANES, 0, 2)[3]
    tm = _row_tile(m, kb * x.dtype.itemsize + nout * 2 + bytes_per_row, k0 * k1 * w4.dtype.itemsize)
    extra_in, out_specs, out_shape, _ = _epilogue_io(epilogue, m, tm, 0, 2)

    def body(x_ref, w_ref, *refs):
        acc = refs[-1]

        @pl.when(pl.program_id(1) == 0)
        def _():
            acc[...] = jnp.zeros_like(acc)
        acc[...] += _dot(x_ref[0].astype(BF16), w_ref[0, 0], 1, wc)

        @pl.when(pl.program_id(1) == slots - 1)
        def _():
            _epilogue_apply(epilogue, acc[...], refs[:-1], pl.program_id(0) == 0)

    return pl.pallas_call(
        body, grid=(m // tm, slots),
        in_specs=[pl.BlockSpec((1, tm, kb), lambda i, s: (s // spp, i, s % spp)), x_spec_w, *extra_in],
        out_specs=out_specs, out_shape=out_shape, scratch_shapes=[pltpu.VMEM((tm, nout), F32)],
        compiler_params=_cp(2), name=name)(x, w4, *epilogue[1], *epilogue[2])


def _mm_tn(x, dy, slots, *, scale=None, wire=False, name):
    m, k = x.shape
    dy = _planes(dy)
    n_planes, _, width = dy.shape
    n = n_planes * width // slots
    spp = slots // n_planes
    ta = _tile(k, 512, LANES)
    tm = m
    while tm > 512 and tm % 2 == 0 and (2 * tm * (ta * x.dtype.itemsize + n * dy.dtype.itemsize)
                                         + 2 * ta * n * 4) > MM_BLOCK_BUDGET_BYTES:
        tm //= 2
    n_m = m // tm

    def body(x_ref, dy_ref, o_ref, *wire_ref):
        @pl.when(pl.program_id(2) == 0)
        def _():
            o_ref[...] = jnp.zeros_like(o_ref)
        o_ref[0] += _dot(x_ref[...].astype(BF16), dy_ref[0].astype(BF16), 0, 0)
        if scale is not None or wire:
            @pl.when(pl.program_id(2) == n_m - 1)
            def _():
                if scale is not None:
                    o_ref[...] = o_ref[...] * scale
                if wire:
                    wire_ref[0][...] = o_ref[...].astype(BF16)

    out_spec = pl.BlockSpec((1, ta, n), lambda s, a, i: (s, a, 0))
    return pl.pallas_call(
        body, grid=(slots, k // ta, n_m),
        in_specs=[pl.BlockSpec((tm, ta), lambda s, a, i: (i, a)),
                  pl.BlockSpec((1, tm, n), lambda s, a, i: (s // spp, i, s % spp))],
        out_specs=[out_spec, out_spec] if wire else out_spec,
        out_shape=([jax.ShapeDtypeStruct((slots, k, n), dt) for dt in (F32, BF16)] if wire
                   else jax.ShapeDtypeStruct((slots, k, n), F32)),
        compiler_params=_cp(3), name=name)(x, dy)


def _rowwise(fn, rows, consts, outs, accs=(), *, tl=256, name):
    n_rows = rows[0].shape[0]
    tl = _tile(n_rows, tl)
    n_in = len(rows) + len(consts)
    n_out = len(outs)

    def body(*refs):
        res = fn(*[r[...] for r in refs[:n_in]])
        res = res if isinstance(res, (tuple, list)) else (res,)
        o_refs = refs[n_in:n_in + n_out]
        a_refs = refs[n_in + n_out:]
        for o, val in zip(o_refs, res[:n_out]):
            o[...] = val.astype(o.dtype)
        if a_refs:
            @pl.when(pl.program_id(0) == 0)
            def _():
                for a in a_refs:
                    a[...] = jnp.zeros_like(a)
            for a, val in zip(a_refs, res[n_out:]):
                a[...] += val

    in_specs = ([pl.BlockSpec((tl, r.shape[1]), lambda i: (i, 0)) for r in rows]
                + [pl.BlockSpec(c.shape, lambda i: (0, 0)) for c in consts])
    out_specs = ([pl.BlockSpec((tl, w), lambda i: (i, 0)) for w, _ in outs]
                 + [pl.BlockSpec(s, lambda i: (0, 0)) for s in accs])
    out_shape = ([jax.ShapeDtypeStruct((n_rows, w), dt) for w, dt in outs]
                 + [jax.ShapeDtypeStruct(s, F32) for s in accs])
    return pl.pallas_call(body, grid=(n_rows // tl,), in_specs=in_specs, out_specs=out_specs,
                          out_shape=out_shape, compiler_params=_cp(1), name=name)(*rows, *consts)


def _rms(x, g):
    return x * lax.rsqrt(jnp.mean(x * x, axis=-1, keepdims=True) + RMS_EPS) * g


def _sigmoid(x):
    return 1.0 / (1.0 + jnp.exp(-x))


def _glu(zz):
    d = zz.shape[1] // 2
    return zz[:, :d] * _sigmoid(zz[:, d:])


def _gelu(y):
    return jax.nn.gelu(y)


def _row2(v):
    return v.reshape(1, -1)


def _node_bwd_fn(counts):
    n_dy = sum(counts)

    def fn(d, hh, *rest):
        dys, gs = rest[:n_dy], rest[n_dy:]
        tot, dgs, pos = d, [], 0
        for g, cnt in zip(gs, counts):
            dy = dys[pos].astype(F32)
            for extra in dys[pos + 1:pos + cnt]:
                dy = dy + extra.astype(F32)
            pos += cnt
            _, vjp = jax.vjp(_rms, hh, g)
            dx, dg = vjp(dy)
            tot = tot + dx
            dgs.append(dg)
        return (tot, tot, *dgs)

    return fn


def _node_bwd(d_in, h, branches, *, name):
    width = h.shape[1]
    flat = [dy for _, dys in branches for dy in dys]
    res = _rowwise(_node_bwd_fn([len(dys) for _, dys in branches]), [d_in, h, *flat],
                   [_row2(g) for g, _ in branches], [(width, F32), (width, BF16)], [(1, width)] * len(branches),
                   name=name)
    return res[0], res[1], [r[0] for r in res[2:]]


def _s5_prep_fn(lr, li, ldt, br, bi, cr, ci, *, gq, h, p):
    dt = jnp.exp(ldt)
    mag = jnp.exp(lr * dt)
    lb_re = mag * jnp.cos(li * dt)
    lb_im = mag * jnp.sin(li * dt)
    den = lr * lr + li * li
    nr = lb_re - 1.0
    fr = (nr * lr + lb_im * li) / den
    fi = (lb_im * lr - nr * li) / den
    bb_re = fr * br - fi * bi
    bb_im = fr * bi + fi * br
    shape = (gq * h, gq * p)
    r = lax.broadcasted_iota(jnp.int32, shape, 0)
    c = lax.broadcasted_iota(jnp.int32, shape, 1)
    mask = jnp.where(jnp.right_shift(r, h.bit_length() - 1) == jnp.right_shift(c, p.bit_length() - 1), 1.0, 0.0)

    def expand(t):
        return jnp.concatenate([t] * gq, axis=0) * mask

    return lb_re, lb_im, expand(bb_re), expand(bb_im), expand(cr), expand(ci)


def _s5_prep(lr, li, ldt, br, bi, cr, ci, p, *, name):
    n = lr.shape[1]
    h = br.shape[0]
    gq = S5_BLOCK_GROUPS
    nq, cq = gq * p, gq * h
    nblk = n // nq
    fn = functools.partial(_s5_prep_fn, gq=gq, h=h, p=p)

    def body(lr_r, li_r, ldt_r, br_r, bi_r, cr_r, ci_r, lbr_o, lbi_o, wbr_o, wbi_o, wcr_o, wci_o):
        lb_re, lb_im, wbr, wbi, wcr, wci = fn(lr_r[...], li_r[...], ldt_r[...], br_r[...], bi_r[...],
                                              cr_r[...], ci_r[...])
        lbr_o[...] = lb_re
        lbi_o[...] = lb_im
        wbr_o[0] = wbr.astype(BF16)
        wbi_o[0] = wbi.astype(BF16)
        wcr_o[0] = wcr.astype(BF16)
        wci_o[0] = wci.astype(BF16)

    vec = pl.BlockSpec((1, nq), lambda q: (0, q))
    tab = pl.BlockSpec((h, nq), lambda q: (0, q))
    wsp = pl.BlockSpec((1, cq, nq), lambda q: (q, 0, 0))
    wsh = jax.ShapeDtypeStruct((nblk, cq, nq), BF16)
    vsh = jax.ShapeDtypeStruct((1, n), F32)
    return pl.pallas_call(body, grid=(nblk,), in_specs=[vec, vec, vec, tab, tab, tab, tab],
                          out_specs=[vec, vec, wsp, wsp, wsp, wsp], out_shape=[vsh, vsh, wsh, wsh, wsh, wsh],
                          compiler_params=_cp(1), name=name)(lr, li, ldt, br, bi, cr, ci)


def _s5_prep_bwd(lr, li, ldt, br, bi, cr, ci, p, dlbr, dlbi, dwbr, dwbi, dwcr, dwci, *, name):
    n = lr.shape[1]
    h = br.shape[0]
    gq = S5_BLOCK_GROUPS
    nq, cq = gq * p, gq * h
    nblk = n // nq
    fn = functools.partial(_s5_prep_fn, gq=gq, h=h, p=p)

    def body(lr_r, li_r, ldt_r, br_r, bi_r, cr_r, ci_r, dlbr_r, dlbi_r, dwbr_r, dwbi_r, dwcr_r, dwci_r,
             *outs):
        _, vjp = jax.vjp(fn, lr_r[...], li_r[...], ldt_r[...], br_r[...], bi_r[...], cr_r[...], ci_r[...])
        grads = vjp((dlbr_r[0], dlbi_r[0], dwbr_r[0], dwbi_r[0], dwcr_r[0], dwci_r[0]))
        for o, g in zip(outs, grads):
            o[...] = g

    vec = pl.BlockSpec((1, nq), lambda q: (0, q))
    tab = pl.BlockSpec((h, nq), lambda q: (0, q))
    vec3 = pl.BlockSpec((1, 1, nq), lambda q: (q, 0, 0))
    wsp = pl.BlockSpec((1, cq, nq), lambda q: (q, 0, 0))
    vsh = jax.ShapeDtypeStruct((1, n), F32)
    tsh = jax.ShapeDtypeStruct((h, n), F32)
    return pl.pallas_call(body, grid=(nblk,),
                          in_specs=[vec, vec, vec, tab, tab, tab, tab, vec3, vec3, wsp, wsp, wsp, wsp],
                          out_specs=[vec, vec, vec, tab, tab, tab, tab],
                          out_shape=[vsh, vsh, vsh, tsh, tsh, tsh, tsh],
                          compiler_params=_cp(1), name=name)(lr, li, ldt, br, bi, cr, ci,
                                                             dlbr, dlbi, dwbr, dwbi, dwcr, dwci)


def _scan_rows(s_re, s_im, a_re, a_im, c_re, c_im, *, reverse):
    t_rows, n = s_re.shape
    nb = t_rows // SUBLANES
    row = lax.broadcasted_iota(jnp.int32, (SUBLANES, n), 0)

    def cmul(x, y):
        return x[0] * y[0] - x[1] * y[1], x[0] * y[1] + x[1] * y[0]

    a1 = (jnp.broadcast_to(a_re, (SUBLANES, n)), jnp.broadcast_to(a_im, (SUBLANES, n)))
    a2 = cmul(a1, a1)
    a4 = cmul(a2, a2)
    steps = []
    for dist, (pr, pi) in ((1, a1), (2, a2), (4, a4)):
        keep = (row < SUBLANES - dist) if reverse else (row >= dist)
        steps.append((SUBLANES - dist if reverse else dist, (jnp.where(keep, pr, 0.0), jnp.where(keep, pi, 0.0))))
    pk = (a_re, a_im)
    tab_re = jnp.zeros((SUBLANES, n), F32)
    tab_im = jnp.zeros((SUBLANES, n), F32)
    for i in range(SUBLANES):
        at = (SUBLANES - 1 - i) if reverse else i
        tab_re = jnp.where(row == at, pk[0], tab_re)
        tab_im = jnp.where(row == at, pk[1], tab_im)
        pk = cmul(pk, (a_re, a_im))

    def step(b, carry):
        cr, ci = carry
        blk = (nb - 1 - b) if reverse else b
        off = pl.multiple_of(blk * SUBLANES, SUBLANES)
        x_re = s_re[pl.ds(off, SUBLANES), :]
        x_im = s_im[pl.ds(off, SUBLANES), :]
        for sh, (pr, pi) in steps:
            sh_re = pltpu.roll(x_re, sh, 0)
            sh_im = pltpu.roll(x_im, sh, 0)
            x_re, x_im = x_re + pr * sh_re - pi * sh_im, x_im + pr * sh_im + pi * sh_re
        x_re, x_im = x_re + tab_re * cr - tab_im * ci, x_im + tab_re * ci + tab_im * cr
        s_re[pl.ds(off, SUBLANES), :] = x_re
        s_im[pl.ds(off, SUBLANES), :] = x_im
        edge = 0 if reverse else SUBLANES - 1
        return x_re[edge:edge + 1, :], x_im[edge:edge + 1, :]

    return lax.fori_loop(0, nb, step, (c_re, c_im))


def _s5_fwd(u, prep, dskip, *, name, comm=None):
    lb_re, lb_im, wbr, wbi, wcr, wci = prep
    n_rows, _ = u.shape
    nblk, cq, nq = wbr.shape
    tt = _tile(n_rows, 512)
    nch = n_rows // tt

    def body(u_ref, wbr_r, wbi_r, wcr_r, wci_r, lbr_r, lbi_r, d_ref, y_ref, z_ref, s_re, s_im, sbr_o, sbi_o,
             c_re, c_im):
        @pl.when(pl.program_id(1) == 0)
        def _():
            c_re[...] = jnp.zeros_like(c_re)
            c_im[...] = jnp.zeros_like(c_im)
        uf = u_ref[...]
        ub = uf.astype(BF16)
        s_re[...] = _dot(ub, wbr_r[0], 1, 0)
        s_im[...] = _dot(ub, wbi_r[0], 1, 0)
        sbr_o[0] = c_re[...]
        sbi_o[0] = c_im[...]
        cr, ci = _scan_rows(s_re, s_im, lbr_r[...], lbi_r[...], c_re[...], c_im[...], reverse=False)
        c_re[...] = cr
        c_im[...] = ci
        y = _dot(s_re[...].astype(BF16), wcr_r[0], 1, 1) - _dot(s_im[...].astype(BF16), wci_r[0], 1, 1)
        y = y + d_ref[...] * uf
        y_ref[...] = y
        z_ref[...] = _gelu(y).astype(BF16)

    wsp = pl.BlockSpec((1, cq, nq), lambda q, i: (q, 0, 0))
    vec = pl.BlockSpec((1, nq), lambda q, i: (0, q))
    act = pl.BlockSpec((tt, cq), lambda q, i: (i, q))
    sb = pl.BlockSpec((1, 1, nq), lambda q, i: (i, 0, q))
    sbsh = jax.ShapeDtypeStruct((nch, 1, nblk * nq), F32)
    states = pl.BlockSpec((tt, nq), lambda q, i: (i, q))
    stsh = jax.ShapeDtypeStruct((n_rows, nblk * nq), F32)
    return _call(
        body, grid=(nblk, nch),
        in_specs=[act, wsp, wsp, wsp, wsp, vec, vec, pl.BlockSpec((1, cq), lambda q, i: (0, q))],
        out_specs=[act, act, states, states, sb, sb],
        out_shape=[jax.ShapeDtypeStruct(u.shape, F32), jax.ShapeDtypeStruct(u.shape, BF16), stsh, stsh, sbsh, sbsh],
        scratch_shapes=[pltpu.VMEM((1, nq), F32), pltpu.VMEM((1, nq), F32)],
        args=(u, wbr, wbi, wcr, wci, lb_re, lb_im, dskip), name=name, comm=comm)


def _s5_bwd(u, dy, st_re, st_im, sb_re, sb_im, prep, dskip, *, name, comm=None):
    lb_re, lb_im, wbr, wbi, wcr, wci = prep
    n_rows, _ = u.shape
    nblk, cq, nq = wbr.shape
    tt = _tile(n_rows, 512)
    nch = n_rows // tt

    def body(u_ref, dy_ref, s_re, s_im, sbr_r, sbi_r, wbr_r, wbi_r, wcr_r, wci_r, lbr_r, lbi_r, d_ref,
             du_ref, dwbr, dwbi, dwcr, dwci, dlbr, dlbi, dd_ref, g_re, g_im, lc_re, lc_im):
        @pl.when(pl.program_id(1) == 0)
        def _():
            for ref in (lc_re, lc_im, dwbr, dwbi, dwcr, dwci, dlbr, dlbi, dd_ref):
                ref[...] = jnp.zeros_like(ref)
        uf = u_ref[...]
        ub = uf.astype(BF16)
        dyf = dy_ref[...]
        dyb = dyf.astype(BF16)
        sr16 = s_re[...].astype(BF16)
        si16 = s_im[...].astype(BF16)
        dwcr[0] += _dot(dyb, sr16, 0, 0)
        dwci[0] -= _dot(dyb, si16, 0, 0)
        g_re[...] = _dot(dyb, wcr_r[0], 1, 0)
        g_im[...] = -_dot(dyb, wci_r[0], 1, 0)
        lcr, lci = _scan_rows(g_re, g_im, lbr_r[...], -lbi_r[...], lc_re[...], lc_im[...], reverse=True)
        lc_re[...] = lcr
        lc_im[...] = lci
        lam_r = g_re[...]
        lam_i = g_im[...]
        first = lax.broadcasted_iota(jnp.int32, (tt, nq), 0) == 0
        prev_r = jnp.where(first, sbr_r[0], pltpu.roll(s_re[...], 1, 0))
        prev_i = jnp.where(first, sbi_r[0], pltpu.roll(s_im[...], 1, 0))
        dlbr[0] += jnp.sum(lam_r * prev_r + lam_i * prev_i, axis=0, keepdims=True)
        dlbi[0] += jnp.sum(lam_i * prev_r - lam_r * prev_i, axis=0, keepdims=True)
        lr16 = lam_r.astype(BF16)
        li16 = lam_i.astype(BF16)
        du_ref[...] = _dot(lr16, wbr_r[0], 1, 1) + _dot(li16, wbi_r[0], 1, 1) + d_ref[...] * dyf
        dwbr[0] += _dot(ub, lr16, 0, 0)
        dwbi[0] += _dot(ub, li16, 0, 0)
        dd_ref[0] += jnp.sum(dyf * uf, axis=0, keepdims=True)

    last = nch - 1
    wsp = pl.BlockSpec((1, cq, nq), lambda q, i: (q, 0, 0))
    vec = pl.BlockSpec((1, nq), lambda q, i: (0, q))
    act = pl.BlockSpec((tt, cq), lambda q, i: (last - i, q))
    sb = pl.BlockSpec((1, 1, nq), lambda q, i: (last - i, 0, q))
    vec3 = pl.BlockSpec((1, 1, nq), lambda q, i: (q, 0, 0))
    dsp = pl.BlockSpec((1, 1, cq), lambda q, i: (q, 0, 0))
    wsh = jax.ShapeDtypeStruct((nblk, cq, nq), F32)
    v3sh = jax.ShapeDtypeStruct((nblk, 1, nq), F32)
    big = pltpu.VMEM((tt, nq), F32)
    states = pl.BlockSpec((tt, nq), lambda q, i: (last - i, q))
    return _call(
        body, grid=(nblk, nch),
        in_specs=[act, act, states, states, sb, sb, wsp, wsp, wsp, wsp, vec, vec,
                  pl.BlockSpec((1, cq), lambda q, i: (0, q))],
        out_specs=[act, wsp, wsp, wsp, wsp, vec3, vec3, dsp],
        out_shape=[jax.ShapeDtypeStruct(u.shape, F32), wsh, wsh, wsh, wsh, v3sh, v3sh,
                   jax.ShapeDtypeStruct((nblk, 1, cq), F32)],
        scratch_shapes=[big, big, pltpu.VMEM((1, nq), F32), pltpu.VMEM((1, nq), F32)],
        args=(u, dy, st_re, st_im, sb_re, sb_im, wbr, wbi, wcr, wci, lb_re, lb_im, dskip), name=name, comm=comm)


def _conv_taps(cur, prev, w, b):
    ext = jnp.concatenate([prev, cur], axis=0)
    x1 = pltpu.roll(ext, 1, 0)[SUBLANES:, :]
    x2 = pltpu.roll(ext, 2, 0)[SUBLANES:, :]
    return b + x2 * w[0:1, :] + x1 * w[1:2, :] + cur * w[2:3, :], x1, x2


def _conv_fwd(uu, cw, cb, *, name):
    n_rows, f2 = uu.shape
    f = f2 // 2
    tc = _tile(f, 1408, LANES)
    tl = _tile(n_rows, 256)
    nfb = f // tc

    def body(g_ref, u_ref, wg_ref, wu_ref, bg_ref, bu_ref, o_ref, pg, pu):
        @pl.when(pl.program_id(1) == 0)
        def _():
            pg[...] = jnp.zeros_like(pg)
            pu[...] = jnp.zeros_like(pu)
        gcur = g_ref[...]
        ucur = u_ref[...]
        cg, _, _ = _conv_taps(gcur, pg[...], wg_ref[...], bg_ref[...])
        cu, _, _ = _conv_taps(ucur, pu[...], wu_ref[...], bu_ref[...])
        o_ref[...] = (cg * _sigmoid(cg) * cu).astype(o_ref.dtype)
        pg[...] = gcur[tl - SUBLANES:, :]
        pu[...] = ucur[tl - SUBLANES:, :]

    return pl.pallas_call(
        body, grid=(nfb, n_rows // tl),
        in_specs=[pl.BlockSpec((tl, tc), lambda j, i: (i, j)), pl.BlockSpec((tl, tc), lambda j, i: (i, j + nfb)),
                  pl.BlockSpec((CONV_TAPS, tc), lambda j, i: (0, j)),
                  pl.BlockSpec((CONV_TAPS, tc), lambda j, i: (0, j + nfb)),
                  pl.BlockSpec((1, tc), lambda j, i: (0, j)), pl.BlockSpec((1, tc), lambda j, i: (0, j + nfb))],
        out_specs=pl.BlockSpec((tl, tc), lambda j, i: (i, j)),
        out_shape=jax.ShapeDtypeStruct((n_rows, f), BF16),
        scratch_shapes=[pltpu.VMEM((SUBLANES, tc), F32), pltpu.VMEM((SUBLANES, tc), F32)],
        compiler_params=_cp(2), name=name)(uu, uu, cw, cw, cb, cb)


def _conv_bwd(uu, dact, cw, cb, *, name):
    n_rows, f2 = uu.shape
    f = f2 // 2
    tc = _tile(f, 1408, LANES)
    tl = _tile(n_rows, 256)
    nfb = f // tc
    nrb = n_rows // tl
    halo_per_tile = tl // SUBLANES

    def body(g_ref, gh_ref, u_ref, uh_ref, da_ref, wg_ref, wu_ref, bg_ref, bu_ref,
             duu_ref, dw_ref, db_ref, nxt_g, nxt_u):
        i = pl.program_id(1)
        rb = nrb - 1 - i

        @pl.when(i == 0)
        def _():
            for ref in (nxt_g, nxt_u, dw_ref, db_ref):
                ref[...] = jnp.zeros_like(ref)
        has_prev = jnp.where(rb > 0, 1.0, 0.0)
        gcur, ucur = g_ref[...], u_ref[...]
        wg, wu = wg_ref[...], wu_ref[...]
        cg, g1, g2 = _conv_taps(gcur, gh_ref[...] * has_prev, wg, bg_ref[...])
        cu, u1, u2 = _conv_taps(ucur, uh_ref[...] * has_prev, wu, bu_ref[...])
        sg = _sigmoid(cg)
        silu = cg * sg
        da = da_ref[...]

        def transpose_conv(plane, d, cur, x1, x2, w, nxt):
            ext = jnp.concatenate([d, nxt[...]], axis=0)
            d1 = pltpu.roll(ext, tl + SUBLANES - 1, 0)[:tl, :]
            d2 = pltpu.roll(ext, tl + SUBLANES - 2, 0)[:tl, :]
            duu_ref[plane] = (w[2:3, :] * d + w[1:2, :] * d1 + w[0:1, :] * d2).astype(duu_ref.dtype)
            nxt[...] = d[0:SUBLANES, :]
            dw_ref[plane] += jnp.concatenate([jnp.sum(d * x2, axis=0, keepdims=True),
                                              jnp.sum(d * x1, axis=0, keepdims=True),
                                              jnp.sum(d * cur, axis=0, keepdims=True)], axis=0)
            db_ref[plane] += jnp.sum(d, axis=0, keepdims=True)

        transpose_conv(0, da * cu * (sg * (1.0 + cg * (1.0 - sg))), gcur, g1, g2, wg, nxt_g)
        transpose_conv(1, da * silu, ucur, u1, u2, wu, nxt_u)

    def halo(j, i):
        return jnp.maximum((nrb - 1 - i) * halo_per_tile - 1, 0)

    return pl.pallas_call(
        body, grid=(nfb, nrb),
        in_specs=[pl.BlockSpec((tl, tc), lambda j, i: (nrb - 1 - i, j)),
                  pl.BlockSpec((SUBLANES, tc), lambda j, i: (halo(j, i), j)),
                  pl.BlockSpec((tl, tc), lambda j, i: (nrb - 1 - i, j + nfb)),
                  pl.BlockSpec((SUBLANES, tc), lambda j, i: (halo(j, i), j + nfb)),
                  pl.BlockSpec((tl, tc), lambda j, i: (nrb - 1 - i, j)),
                  pl.BlockSpec((CONV_TAPS, tc), lambda j, i: (0, j)),
                  pl.BlockSpec((CONV_TAPS, tc), lambda j, i: (0, j + nfb)),
                  pl.BlockSpec((1, tc), lambda j, i: (0, j)),
                  pl.BlockSpec((1, tc), lambda j, i: (0, j + nfb))],
        out_specs=[pl.BlockSpec((2, tl, tc), lambda j, i: (0, nrb - 1 - i, j)),
                   pl.BlockSpec((2, CONV_TAPS, tc), lambda j, i: (0, 0, j)),
                   pl.BlockSpec((2, 1, tc), lambda j, i: (0, 0, j))],
        out_shape=[jax.ShapeDtypeStruct((2, n_rows, f), BF16), jax.ShapeDtypeStruct((2, CONV_TAPS, f), F32),
                   jax.ShapeDtypeStruct((2, 1, f), F32)],
        scratch_shapes=[pltpu.VMEM((SUBLANES, tc), F32), pltpu.VMEM((SUBLANES, tc), F32)],
        compiler_params=_cp(2), name=name)(uu, uu, uu, uu, dact, cw, cw, cb, cb)


def _log_sigmoid(x):
    t = jnp.exp(-jnp.abs(x))
    log1p_t = jnp.where(t < 1e-3, t * (1.0 - t * (0.5 - t * (1.0 / 3.0))), jnp.log(1.0 + t))
    return jnp.minimum(x, 0.0) - log1p_t


def _dlog_sigmoid(x):
    t = jnp.exp(-jnp.abs(x))
    return jnp.where(x >= 0, t, 1.0) / (1.0 + t)


def _tri_dot(tri, x):
    return jnp.dot(tri, x, precision=lax.Precision.HIGHEST, preferred_element_type=F32)


def _cum_fwd(fl, bf, *, name):
    n_rows, width = fl.shape
    tc = _tile(n_rows, 256)

    def body(fl_ref, bf_ref, o_ref, carry):
        @pl.when(pl.program_id(0) == 0)
        def _():
            carry[...] = jnp.zeros_like(carry)
        x = _log_sigmoid(fl_ref[...] + bf_ref[...])
        r = lax.broadcasted_iota(jnp.int32, (tc, tc), 0)
        c = lax.broadcasted_iota(jnp.int32, (tc, tc), 1)
        y = _tri_dot(jnp.where(r >= c, 1.0, 0.0), x) + carry[...]
        o_ref[...] = y
        carry[...] = y[tc - 1:tc, :]

    return pl.pallas_call(
        body, grid=(n_rows // tc,),
        in_specs=[pl.BlockSpec((tc, width), lambda i: (i, 0)), pl.BlockSpec((1, width), lambda i: (0, 0))],
        out_specs=pl.BlockSpec((tc, width), lambda i: (i, 0)),
        out_shape=jax.ShapeDtypeStruct(fl.shape, F32),
        scratch_shapes=[pltpu.VMEM((1, width), F32)], compiler_params=_cp(1), name=name)(fl, bf)


def _cum_bwd(dcum, fl, bf, *, name):
    n_rows, width = fl.shape
    tc = _tile(n_rows, 256)
    last = n_rows // tc - 1

    def body(dc_ref, fl_ref, bf_ref, dfl_ref, dbf_ref, carry):
        @pl.when(pl.program_id(0) == 0)
        def _():
            carry[...] = jnp.zeros_like(carry)
            dbf_ref[...] = jnp.zeros_like(dbf_ref)
        r = lax.broadcasted_iota(jnp.int32, (tc, tc), 0)
        c = lax.broadcasted_iota(jnp.int32, (tc, tc), 1)
        dls = _tri_dot(jnp.where(r <= c, 1.0, 0.0), dc_ref[...]) + carry[...]
        carry[...] = dls[0:1, :]
        dfl = dls * _dlog_sigmoid(fl_ref[...] + bf_ref[...])
        dfl_ref[...] = dfl.astype(dfl_ref.dtype)
        dbf_ref[...] += jnp.sum(dfl, axis=0, keepdims=True)

    return pl.pallas_call(
        body, grid=(n_rows // tc,),
        in_specs=[pl.BlockSpec((tc, width), lambda i: (last - i, 0)),
                  pl.BlockSpec((tc, width), lambda i: (last - i, 0)),
                  pl.BlockSpec((1, width), lambda i: (0, 0))],
        out_specs=[pl.BlockSpec((tc, width), lambda i: (last - i, 0)), pl.BlockSpec((1, width), lambda i: (0, 0))],
        out_shape=[jax.ShapeDtypeStruct(fl.shape, BF16), jax.ShapeDtypeStruct((1, width), F32)],
        scratch_shapes=[pltpu.VMEM((1, width), F32)], compiler_params=_cp(1), name=name)(dcum, fl, bf)


def _head_masks():
    lane = lax.broadcasted_iota(jnp.int32, (1, LANES), 1)
    return (lane < HEAD_DIM, lane >= HEAD_DIM)


def _flash_fwd(q, kv, cum_c, cum_r, *, tq, name, comm=None):
    n_rows, d = q.shape
    nhp = d // LANES
    tk = tq
    nq = n_rows // tq
    rt = _tile(tk, FLASH_ROW_TILE)
    reps = (1, tq // LANES)

    def body(qi_ref, kj_ref, q_ref, k_ref, v_ref, cq_ref, ck_ref, ot_ref, lse_ref, m0, m1, l0, l1, acc,
             s0, s1, p0, p1, b0, b1):
        i = qi_ref[pl.program_id(1)]
        j = kj_ref[pl.program_id(1)]
        ms, ls = (m0, m1), (l0, l1)
        head_rows = lax.broadcasted_iota(jnp.int32, (LANES, 1), 0) < HEAD_DIM

        @pl.when(j == 0)
        def _():
            for h in range(2):
                ms[h][...] = jnp.full_like(ms[h], -jnp.inf)
                ls[h][...] = jnp.zeros_like(ls[h])
            acc[...] = jnp.zeros_like(acc)

        def block(diagonal):
            qv, kk, vv = q_ref[...], k_ref[...], v_ref[...]
            a = acc[...]
            for h, msk in enumerate(_head_masks()):
                st_sc, pt_sc, bias_sc = ((s0, p0, b0), (s1, p1, b1))[h]
                st_sc[...] = _dot(kk, jnp.where(msk, qv, jnp.zeros_like(qv)), 1, 1)
                bias_sc[...] = jnp.broadcast_to(cq_ref[0, h:h + 1, 0:1] - ck_ref[0, :, h:h + 1], (tk, LANES))
                m_old, l_old = ms[h][...], ls[h][...]
                col_max = jnp.full((SUBLANES, tq), -jnp.inf, F32)
                for r in range(tk // rt):
                    rows = slice(r * rt, (r + 1) * rt)
                    s = st_sc[rows, :] + jnp.tile(bias_sc[rows, :], reps)
                    if diagonal:
                        key = r * rt + lax.broadcasted_iota(jnp.int32, (rt, tq), 0)
                        qry = lax.broadcasted_iota(jnp.int32, (rt, tq), 1)
                        s = jnp.where(key <= qry, s, -jnp.inf)
                    st_sc[rows, :] = s
                    for g in range(rt // SUBLANES):
                        col_max = jnp.maximum(col_max, s[g * SUBLANES:(g + 1) * SUBLANES, :])
                m_new = jnp.maximum(m_old, jnp.max(col_max, axis=0, keepdims=True))
                col_sum = jnp.zeros((SUBLANES, tq), F32)
                for r in range(tk // rt):
                    rows = slice(r * rt, (r + 1) * rt)
                    p = jnp.exp(st_sc[rows, :] - m_new)
                    for g in range(rt // SUBLANES):
                        col_sum = col_sum + p[g * SUBLANES:(g + 1) * SUBLANES, :]
                    pt_sc[rows, :] = p.astype(BF16)
                alpha = jnp.exp(m_old - m_new)
                ms[h][...] = m_new
                ls[h][...] = alpha * l_old + jnp.sum(col_sum, axis=0, keepdims=True)
                pv_t = _dot(jnp.where(msk, vv, jnp.zeros_like(vv)), pt_sc[...], 0, 0)
                a = a * jnp.where(head_rows == (h == 0), alpha, 1.0) + pv_t
            acc[...] = a

        pl.when(j < i)(functools.partial(block, False))
        pl.when(j == i)(functools.partial(block, True))

        @pl.when(j == i)
        def _():
            ot_ref[...] = acc[...] * jnp.where(head_rows, 1.0 / l0[...], 1.0 / l1[...])
            lse_ref[0] = jnp.concatenate([m0[...] + jnp.log(l0[...]), m1[...] + jnp.log(l1[...])], axis=0)

    pairs = [(i, j) for i in range(nq) for j in range(i + 1)]
    qi = jnp.asarray([i for i, _ in pairs], jnp.int32)
    kj = jnp.asarray([j for _, j in pairs], jnp.int32)
    stat = pltpu.VMEM((1, tq), F32)
    return _call(
        body, grid=(nhp, len(pairs)), prefetch=(qi, kj),
        in_specs=[pl.BlockSpec((tq, LANES), lambda hp, t, qi, kj: (qi[t], hp)),
                  pl.BlockSpec((tk, LANES), lambda hp, t, qi, kj: (kj[t], hp)),
                  pl.BlockSpec((tk, LANES), lambda hp, t, qi, kj: (kj[t], nhp + hp)),
                  pl.BlockSpec((1, 2, tq), lambda hp, t, qi, kj: (hp, 0, qi[t])),
                  pl.BlockSpec((1, tk, 2), lambda hp, t, qi, kj: (hp, kj[t], 0))],
        out_specs=[pl.BlockSpec((LANES, tq), lambda hp, t, qi, kj: (hp, qi[t])),
                   pl.BlockSpec((1, 2, tq), lambda hp, t, qi, kj: (hp, 0, qi[t]))],
        scratch_shapes=[stat, stat, stat, stat, pltpu.VMEM((LANES, tq), F32), pltpu.VMEM((tk, tq), F32),
                        pltpu.VMEM((tk, tq), F32), pltpu.VMEM((tk, tq), BF16), pltpu.VMEM((tk, tq), BF16),
                        pltpu.VMEM((tk, LANES), F32), pltpu.VMEM((tk, LANES), F32)],
        out_shape=[jax.ShapeDtypeStruct((d, n_rows), F32), jax.ShapeDtypeStruct((nhp, 2, n_rows), F32)],
        args=(q, kv, kv, cum_r, cum_c), name=name, comm=comm)


def _head_delta(do, o, *, name):
    d = o.shape[1]

    def fn(dd, oo):
        prod = dd.astype(BF16).astype(F32) * oo
        r = lax.broadcasted_iota(jnp.int32, (d, LANES), 0)
        c = lax.broadcasted_iota(jnp.int32, (d, LANES), 1)
        return _tri_dot(prod, jnp.where(jnp.right_shift(r, HEAD_DIM.bit_length() - 1) == c, 1.0, 0.0))

    return _rowwise(fn, [do, o], [], [(LANES, F32)], name=name)[0]


def _flash_bwd(q, kv, lse_r, delta_r, do, cum_c, cum_r, *, tq, name, comm=None):
    n_rows, d = q.shape
    nhp = d // LANES
    tk = tq
    nq = n_rows // tq
    rt = _tile(tk, FLASH_ROW_TILE)
    reps = (1, tq // LANES)

    def body(qi_ref, kj_ref, q_ref, k_ref, v_ref, lse_ref, dl_ref, do_ref, cq_ref, ck_ref,
             dq_ref, dk_ref, dv_ref, dck_ref, dcq_ref, s0, dp0, p0, ds0, b0, b1, ck0, ck1):
        s1, dp1, p1, ds1 = s0, dp0, p0, ds0
        i = qi_ref[pl.program_id(1)]
        j = kj_ref[pl.program_id(1)]

        @pl.when(pl.program_id(1) == 0)
        def _():
            dq_ref[...] = jnp.zeros_like(dq_ref)
            dcq_ref[...] = jnp.zeros_like(dcq_ref)

        @pl.when(i == j)
        def _():
            for ref in (dk_ref, dv_ref, ck0, ck1):
                ref[...] = jnp.zeros_like(ref)

        def block(diagonal):
            qv, kk, vv = q_ref[...], k_ref[...], v_ref[...]
            dob = do_ref[...].astype(BF16)
            dq_acc = jnp.zeros((tq, LANES), F32)
            dk_acc = jnp.zeros((tk, LANES), F32)
            dv_acc = jnp.zeros((tk, LANES), F32)
            query_sums = []
            for h, msk in enumerate(_head_masks()):
                st_sc, dpt_sc, pt_sc, dst_sc, bias_sc, key_part = ((s0, dp0, p0, ds0, b0, ck0),
                                                                  (s1, dp1, p1, ds1, b1, ck1))[h]
                qh = jnp.where(msk, qv, jnp.zeros_like(qv))
                kh = jnp.where(msk, kk, jnp.zeros_like(kk))
                doh = jnp.where(msk, dob, jnp.zeros_like(dob))
                st_sc[...] = _dot(kk, qh, 1, 1)
                dpt_sc[...] = _dot(vv, doh, 1, 1)
                bias_sc[...] = jnp.broadcast_to(cq_ref[0, h:h + 1, 0:1] - ck_ref[0, :, h:h + 1], (tk, LANES))
                lse_row = lse_ref[0, h:h + 1, :]
                delta_row = dl_ref[0, h:h + 1, :]
                col_acc = jnp.zeros((SUBLANES, tq), F32)
                parts = []
                for r in range(tk // rt):
                    rows = slice(r * rt, (r + 1) * rt)
                    s = st_sc[rows, :] + jnp.tile(bias_sc[rows, :], reps)
                    if diagonal:
                        key = r * rt + lax.broadcasted_iota(jnp.int32, (rt, tq), 0)
                        qry = lax.broadcasted_iota(jnp.int32, (rt, tq), 1)
                        s = jnp.where(key <= qry, s, -jnp.inf)
                    p = jnp.exp(s - lse_row)
                    ds = p * (dpt_sc[rows, :] - delta_row)
                    for g in range(rt // SUBLANES):
                        col_acc = col_acc + ds[g * SUBLANES:(g + 1) * SUBLANES, :]
                    part = ds[:, 0:LANES]
                    for g in range(1, tq // LANES):
                        part = part + ds[:, g * LANES:(g + 1) * LANES]
                    parts.append(part)
                    pt_sc[rows, :] = p.astype(BF16)
                    dst_sc[rows, :] = ds.astype(BF16)
                key_part[...] += jnp.concatenate(parts, axis=0)
                query_sums.append(jnp.sum(col_acc, axis=0, keepdims=True))
                dv_acc = dv_acc + _dot(pt_sc[...], doh, 1, 0)
                dsb = dst_sc[...]
                dk_acc = dk_acc + _dot(dsb, qh, 1, 0)
                dq_acc = dq_acc + _dot(dsb, kh, 0, 0)
            off = pl.multiple_of(i * tq, tq)
            dq_ref[pl.ds(off, tq), :] += dq_acc
            dk_ref[...] += dk_acc
            dv_ref[...] += dv_acc
            dcq_ref[0, i] += jnp.concatenate(query_sums, axis=0)

        pl.when(i > j)(functools.partial(block, False))
        pl.when(i == j)(functools.partial(block, True))

        @pl.when(i == nq - 1)
        def _():
            two = lax.broadcasted_iota(jnp.int32, (tk, 2), 1)
            dck_ref[0] = jnp.where(two == 0, -jnp.sum(ck0[...], axis=1, keepdims=True),
                                   -jnp.sum(ck1[...], axis=1, keepdims=True))

    pairs = [(i, j) for j in range(nq) for i in range(j, nq)]
    qi = jnp.asarray([i for i, _ in pairs], jnp.int32)
    kj = jnp.asarray([j for _, j in pairs], jnp.int32)
    score = pltpu.VMEM((tk, tq), F32)
    score16 = pltpu.VMEM((tk, tq), BF16)
    keystat = pltpu.VMEM((tk, LANES), F32)
    return _call(
        body, grid=(nhp, len(pairs)), prefetch=(qi, kj),
        in_specs=[pl.BlockSpec((tq, LANES), lambda hp, t, qi, kj: (qi[t], hp)),
                  pl.BlockSpec((tk, LANES), lambda hp, t, qi, kj: (kj[t], hp)),
                  pl.BlockSpec((tk, LANES), lambda hp, t, qi, kj: (kj[t], nhp + hp)),
                  pl.BlockSpec((1, 2, tq), lambda hp, t, qi, kj: (hp, 0, qi[t])),
                  pl.BlockSpec((1, 2, tq), lambda hp, t, qi, kj: (hp, 0, qi[t])),
                  pl.BlockSpec((tq, LANES), lambda hp, t, qi, kj: (qi[t], hp)),
                  pl.BlockSpec((1, 2, tq), lambda hp, t, qi, kj: (hp, 0, qi[t])),
                  pl.BlockSpec((1, tk, 2), lambda hp, t, qi, kj: (hp, kj[t], 0))],
        out_specs=[pl.BlockSpec((n_rows, LANES), lambda hp, t, qi, kj: (0, hp)),
                   pl.BlockSpec((tk, LANES), lambda hp, t, qi, kj: (kj[t], hp)),
                   pl.BlockSpec((tk, LANES), lambda hp, t, qi, kj: (kj[t], hp)),
                   pl.BlockSpec((1, tk, 2), lambda hp, t, qi, kj: (hp, kj[t], 0)),
                   pl.BlockSpec((1, nq, 2, tq), lambda hp, t, qi, kj: (hp, 0, 0, 0))],
        scratch_shapes=[score, score, score16, score16, keystat, keystat, keystat, keystat],
        out_shape=[jax.ShapeDtypeStruct((n_rows, d), F32), jax.ShapeDtypeStruct((n_rows, d), F32),
                   jax.ShapeDtypeStruct((n_rows, d), F32), jax.ShapeDtypeStruct((nhp, n_rows, 2), F32),
                   jax.ShapeDtypeStruct((nhp, nq, 2, tq), F32)],
        args=(q, kv, kv, lse_r, delta_r, do, cum_r, cum_c), name=name, comm=comm)


def _s5_tables(w, layer):
    g, p = w["lam_re"].shape[1:]
    h = w["ssm_b_re"].shape[3]
    n = g * p
    lr = w["lam_re"][layer].reshape(1, n)
    li = w["lam_im"][layer].reshape(1, n)
    ldt = jnp.broadcast_to(w["log_dt"][layer][:, None], (g, p)).reshape(1, n)
    br = w["ssm_b_re"][layer].transpose(2, 0, 1).reshape(h, n)
    bi = w["ssm_b_im"][layer].transpose(2, 0, 1).reshape(h, n)
    cr = w["ssm_c_re"][layer].transpose(1, 0, 2).reshape(h, n)
    ci = w["ssm_c_im"][layer].transpose(1, 0, 2).reshape(h, n)
    return (lr, li, ldt, br, bi, cr, ci), (g, p, h)


def _local_step(x, tgt, w, net=None, *, attn_tile=1024):
    n_rows, d = x.shape
    n_layers = w["g_mix"].shape[0]
    n_s5 = w["lam_re"].shape[0]
    nh = w["b_f"].shape[0]
    nhp = nh // 2
    assert d == nh * HEAD_DIM
    tq = _tile(n_rows, attn_tile)
    g = {}
    saved = [dict() for _ in range(n_layers)]
    big = {}
    pending = {}

    def wt(name, layer):
        return w[name][layer]

    def carry_gather(group, run):
        if net is None or not net.has_group(group):
            return run(None)
        outs, got = run(net.gather_comm(group))
        net.store_gathered(group, got, w)
        return outs

    def carry_reduce(tag, run):
        keys = list(pending)
        grads = [pending[k][0] for k in keys]
        if net is None or not pending:
            big.update(zip(keys, grads))
            pending.clear()
            return run(None)
        parts = net.reduce_prepare(grads, [pending[k][1] for k in keys], tag)
        outs, landed = run(_chip_exchange_comm(parts))
        big.update(zip(keys, net.reduce_finish(parts, landed, grads, tag)))
        pending.clear()
        return outs

    def by_row_shard(m):
        return m.reshape(N_CHIPS, m.shape[0] // N_CHIPS, m.shape[1])

    def grad_and_copy(x_, dy_, name, *, col_slots=None, scale=None):
        g32, g16 = _mm_tn(x_, dy_, col_slots or 1, scale=scale, wire=True, name=name)
        return (g32, g16) if col_slots else (by_row_shard(g32[0]), by_row_shard(g16[0]))

    h = x
    nxt = _rowwise(lambda a, gg: _rms(a, gg), [x], [_row2(w["g_mix"][0])], [(d, F32)], name="rms_first")[0]
    kvb = fl = cum = cq3 = ck3 = hnkv = None
    bf_pad = jnp.zeros((1, LANES), F32).at[0, :nh].set(w["b_f"])
    for l in range(n_layers):
        sv = saved[l]
        sv["h"] = h
        g_ffn = _row2(w["g_ffn"][l])
        if l < n_s5:
            tabs, (_, p, _) = _s5_tables(w, l)
            prep = _s5_prep(*tabs, p, name=f"s5_prep{l}")
            dskip = w["ssm_d"][l].reshape(1, d)
            y, z, st_re, st_im, sb_re, sb_im = carry_gather(
                f"stage{l}", lambda comm, u=nxt, pr=prep, ds=dskip: _s5_fwd(u, pr, ds, name=f"s5_fwd{l}", comm=comm))
            zz = _mm_cols(z, *wt("w_glu", l), wc=0, name=f"glu_mm{l}")
            h1, hn2 = _rowwise(lambda hh, zq, gg: ((lambda t: (t, _rms(t, gg)))(hh + _glu(zq))),
                               [h, zz], [g_ffn], [(d, F32), (d, BF16)], name=f"mix_out{l}")
            sv.update(u=nxt, prep=prep, tabs=tabs, p=p, dskip=dskip, st_re=st_re, st_im=st_im, sb_re=sb_re,
                      sb_im=sb_im, y=y, z=z, zz=zz)
        else:
            j = l - n_s5
            qs = _mm_cols(nxt, *wt("w_q", j), wc=0, out_dtype=BF16, scale=HEAD_DIM ** -0.5, name=f"q_mm{j}")
            o_t, lse = carry_gather(
                f"stage{l}", lambda comm, q_=qs: _flash_fwd(q_, kvb, cq3, ck3, tq=tq, name=f"flash_fwd{j}", comm=comm))
            o = o_t.T
            h1, hn2 = _mm_cols(o, *wt("w_o", j), wc=0, name=f"o_mm{j}",
                               epilogue=(lambda aa, hh, gg: ((lambda t: (t, _rms(t, gg)))(hh + aa)), [h], [g_ffn],
                                         [(d, F32), (d, BF16)], ()))
            sv.update(hn=nxt, qs=qs, o=o, lse=lse)
        uu = _mm_cols(hn2, *wt("w_in", l), wc=0, name=f"ffn_in{l}")
        cw, cb = w["conv_w"][l], _row2(w["conv_b"][l])
        act = _conv_fwd(uu, cw, cb, name=f"conv_fwd{l}")
        sv.update(h1=h1, hn2=hn2, uu=uu, act=act, cw=cw, cb=cb)

        def ffn_out(fn, rows, consts, outs, accs=()):
            return _mm_cols(act, *wt("w_out", l), wc=0, name=f"ffn_out{l}", epilogue=(fn, rows, consts, outs, accs))

        if l == n_layers - 1:
            def loss_fn(ff, hh, tt, gg):
                yv, vjp = jax.vjp(_rms, hh + ff, gg)
                err = yv - tt
                part = 0.5 * jnp.sum(jnp.mean(err * err, axis=-1, keepdims=True), axis=0, keepdims=True)
                dh, dg = vjp(err * (1.0 / d))
                return dh, dh, jnp.broadcast_to(part, (1, LANES)), dg
            dcur, dcur16, loss_row, dgf = ffn_out(loss_fn, [h1, tgt], [_row2(w["g_final"])],
                                                  [(d, F32), (d, BF16)], [(1, LANES), (1, d)])
            loss = loss_row[0, 0]
            g["g_final"] = dgf[0]
        elif l + 1 < n_s5:
            h, nxt = ffn_out(lambda ff, hh, gg: ((lambda t: (t, _rms(t, gg)))(hh + ff)), [h1],
                             [_row2(w["g_mix"][l + 1])], [(d, F32), (d, F32)])
        elif l + 1 == n_s5:
            h, nxt, hnkv = ffn_out(lambda ff, hh, g1, g2: ((lambda t: (t, _rms(t, g1), _rms(t, g2)))(hh + ff)), [h1],
                                   [_row2(w["g_mix"][l + 1]), _row2(w["g_kv"])],
                                   [(d, F32), (d, BF16), (d, BF16)])
            kvb = _mm_cols(hnkv, *wt("w_kv", 0), wc=0, out_dtype=BF16, name="kv_mm")
            fl = _mm_cols(hnkv, *wt("w_f", 0), wc=0, name="f_mm")
            cum = _cum_fwd(fl, bf_pad, name="cum_fwd")
            cq3 = cum[:, :nh].reshape(n_rows, nhp, 2).transpose(1, 0, 2)
            ck3 = cum[:, :nh].T.reshape(nhp, 2, n_rows)
        else:
            h, nxt = ffn_out(lambda ff, hh, gg: ((lambda t: (t, _rms(t, gg)))(hh + ff)), [h1],
                             [_row2(w["g_mix"][l + 1])], [(d, F32), (d, BF16)])

    per_layer = {k: [None] * n_layers for k in ("g_mix", "g_ffn", "conv_w", "conv_b")}
    per_s5 = {k: [None] * n_s5 for k in ("lam_re", "lam_im", "log_dt", "ssm_b_re", "ssm_b_im", "ssm_c_re",
                                         "ssm_c_im", "ssm_d")}
    dk_parts, dv_parts, dck_parts = [], [], []
    for l in reversed(range(n_layers)):
        sv = saved[l]
        dact = _mm_cols(dcur16, *wt("w_out", l), wc=1, name=f"ffn_out_dx{l}")
        pending["w_ffn_out", l] = grad_and_copy(sv["act"], dcur16, f"ffn_out_dw{l}")
        duu, dcw, dcb = _conv_bwd(sv["uu"], dact, sv["cw"], sv["cb"], name=f"conv_bwd{l}")
        per_layer["conv_w"][l] = jnp.concatenate([dcw[0], dcw[1]], axis=-1)
        per_layer["conv_b"][l] = jnp.concatenate([dcb[0, 0], dcb[1, 0]])
        node = _node_bwd_fn([1])
        d1, d1_16, dg = _mm_acc(duu, *wt("w_in", l), wc=1, name=f"ffn_in_dx{l}",
                                epilogue=(lambda dhn2, dd, hh, gg: node(dd, hh, dhn2, gg), [dcur, sv["h1"]],
                                          [_row2(w["g_ffn"][l])], [(d, F32), (d, BF16)], [(1, d)]))
        pending["w_ffn_in", l] = grad_and_copy(sv["hn2"], duu, f"ffn_in_dw{l}", col_slots=wt("w_in", l)[0].shape[0])
        per_layer["g_ffn"][l] = dg[0]
        if l < n_s5:
            def glu_bwd(zq, dd):
                _, vjp = jax.vjp(_glu, zq)
                return vjp(dd)[0]
            dzz = _rowwise(glu_bwd, [sv["zz"], d1], [], [(2 * d, BF16)], name=f"glu_bwd{l}")[0]
            def gelu_bwd(dd, yy):
                _, vjp = jax.vjp(_gelu, yy)
                return (vjp(dd)[0],)
            dy = _mm_acc(dzz, *wt("w_glu", l), wc=1, name=f"glu_dx{l}",
                         epilogue=(gelu_bwd, [sv["y"]], [], [(d, F32)], ()))[0]
            pending["w_glu", l] = grad_and_copy(sv["z"], dzz, f"glu_dw{l}", col_slots=wt("w_glu", l)[0].shape[0])
            du, dwbr, dwbi, dwcr, dwci, dlbr, dlbi, dd = carry_reduce(
                f"stage{l}", lambda comm, dy_=dy: _s5_bwd(sv["u"], dy_, sv["st_re"], sv["st_im"], sv["sb_re"],
                                                          sv["sb_im"], sv["prep"], sv["dskip"], name=f"s5_bwd{l}",
                                                          comm=comm))
            dlr, dli, dldt, dbr, dbi, dcr, dci = _s5_prep_bwd(*sv["tabs"], sv["p"], dlbr, dlbi, dwbr, dwbi, dwcr,
                                                              dwci, name=f"s5_prep_bwd{l}")
            gg, p = w["lam_re"].shape[1:]
            hh = w["ssm_b_re"].shape[3]
            per_s5["lam_re"][l] = dlr.reshape(gg, p)
            per_s5["lam_im"][l] = dli.reshape(gg, p)
            per_s5["log_dt"][l] = dldt.reshape(gg, p).sum(axis=1)
            per_s5["ssm_b_re"][l] = dbr.reshape(hh, gg, p).transpose(1, 2, 0)
            per_s5["ssm_b_im"][l] = dbi.reshape(hh, gg, p).transpose(1, 2, 0)
            per_s5["ssm_c_re"][l] = dcr.reshape(hh, gg, p).transpose(1, 0, 2)
            per_s5["ssm_c_im"][l] = dci.reshape(hh, gg, p).transpose(1, 0, 2)
            per_s5["ssm_d"][l] = dd.reshape(d)
            branches = [(w["g_mix"][l], [du])]
        else:
            j = l - n_s5
            do = _mm_cols(d1_16, *wt("w_o", j), wc=1, name=f"o_dx{j}")
            pending["w_o", j] = grad_and_copy(sv["o"], d1_16, f"o_dw{j}")
            delta_r = _head_delta(do, sv["o"], name=f"head_delta{j}")[:, :nh].T.reshape(nhp, 2, n_rows)
            dq, dk, dv, dck, dcq = carry_reduce(
                f"stage{l}", lambda comm, do_=do: _flash_bwd(sv["qs"], kvb, sv["lse"], delta_r, do_, cq3, ck3, tq=tq,
                                                          name=f"flash_bwd{j}", comm=comm))
            dk_parts.append(dk)
            dv_parts.append(dv)
            dck_parts.append(dck.transpose(1, 0, 2).reshape(n_rows, nh)
                             + dcq.transpose(0, 2, 1, 3).reshape(nh, n_rows).T)
            scale = HEAD_DIM ** -0.5
            pending["w_q", j] = grad_and_copy(sv["hn"], dq, f"q_dw{j}", scale=scale)
            branches = []
            if j == 0:
                def kv_sum(*parts):
                    half = len(parts) // 2
                    return jnp.concatenate([sum(parts[:half][1:], parts[0]),
                                            sum(parts[half:][1:], parts[half])], axis=1)
                dkv = _rowwise(kv_sum, dk_parts + dv_parts, [], [(2 * d, BF16)], name="dkv_sum")[0]
                dck_tot = dck_parts[0]
                for extra in dck_parts[1:]:
                    dck_tot = dck_tot + extra
                dcum = jnp.zeros((n_rows, LANES), F32).at[:, :nh].set(dck_tot)
                dfl, dbf = _cum_bwd(dcum, fl, bf_pad, name="cum_bwd")
                g["b_f"] = dbf[0, :nh]
                dhkv_a = _mm_cols(dkv, *wt("w_kv", 0), wc=1, name="kv_dx")
                dhkv_b = _mm_cols(dfl, *wt("w_f", 0), wc=1, name="f_dx")
                d_kvf = jnp.concatenate([_mm_tn(hnkv, dkv, 1, name="kv_dw")[0],
                                         _mm_tn(hnkv, dfl, 1, name="f_dw")[0][:, :nh]], axis=1)
                d_kvf = d_kvf.reshape(d, N_CHIPS, -1).transpose(1, 0, 2)
                pending["w_kvf", 0] = (d_kvf, d_kvf.astype(BF16))
                branches.append((w["g_kv"], [dhkv_a, dhkv_b]))
            node = _node_bwd_fn([1] + [len(dys) for _, dys in branches])
            gains = [_row2(w["g_mix"][l])] + [_row2(gn) for gn, _ in branches]
            res = _mm_cols(dq, *wt("w_q", j), wc=1, scale=scale, name=f"q_dx{j}",
                           epilogue=(lambda dhn, dd, hh, *rest: node(dd, hh, dhn, *rest),
                                     [d1, sv["h"], *[dy for _, dys in branches for dy in dys]], gains,
                                     [(d, F32), (d, BF16)], [(1, d)] * len(gains)))
            dcur, dcur16, dgs = res[0], res[1], [r[0] for r in res[2:]]
        if l < n_s5:
            dcur, dcur16, dgs = _node_bwd(d1, sv["h"], branches, name=f"mix_norm_bwd{l}")
        per_layer["g_mix"][l] = dgs[0]
        if len(dgs) > 1:
            g["g_kv"] = dgs[1]

    if pending:
        grads = [g32 for g32, _ in pending.values()]
        big.update(zip(list(pending), grads if net is None else net.reduce_blocking(grads, "tail")))
    for k, v in (*per_layer.items(), *per_s5.items()):
        g[k] = jnp.stack(v)
    g["big"] = big
    return loss, dcur, g


def _position():
    x, y, c = lax.axis_index("x"), lax.axis_index("y"), lax.axis_index("c")
    chips = [(1 - x, y), (x, 1 - y), (1 - x, 1 - y)]
    return x, y, c, chips


def _all_gather_comm(shards):
    n = len(shards)

    def descriptors(ins, outs, sems):
        send_sems, recv_sems = sems
        x, y, c, chips = _position()
        my_slot = 2 * x + y
        sibling = (x, y, 1 - c)

        def rows(t, half):
            hr = ins[t].shape[0] // 2
            return pl.ds(half * hr, hr)

        def remote(k, t, src, dst, to):
            return pltpu.make_async_remote_copy(src_ref=src, dst_ref=dst, send_sem=send_sems.at[k, t],
                                                recv_sem=recv_sems.at[k, t], device_id=to, device_id_type=MESH)

        own = [remote(6, t, ins[t], outs[t].at[my_slot], sibling) for t in range(n)]
        ici = [remote(j, t, ins[t].at[rows(t, c)], outs[t].at[my_slot, rows(t, c)], (*chip, c))
               for j, chip in enumerate(chips) for t in range(n)]
        slots = [2 * chip[0] + chip[1] for chip in chips]
        fwd = [[remote(3 + j, t, outs[t].at[slots[j], rows(t, c)], outs[t].at[slots[j], rows(t, c)], sibling)
                for t in range(n)] for j in range(len(chips))]
        landed = [[remote(j, t, outs[t].at[slots[j], rows(t, c)], outs[t].at[slots[j], rows(t, c)], (*chips[j], c))
                   for t in range(n)] for j in range(len(chips))]
        from_sibling = [remote(3 + j, t, outs[t].at[slots[j], rows(t, 1 - c)], outs[t].at[slots[j], rows(t, 1 - c)],
                               sibling) for j in range(len(chips)) for t in range(n)]
        return own, ici, fwd, landed, from_sibling

    def start(ins, outs, sems):
        own, ici, _, _, _ = descriptors(ins, outs, sems)
        for cp in own + ici:
            cp.start()

    def finish(ins, outs, sems):
        own, ici, fwd, landed, from_sibling = descriptors(ins, outs, sems)
        for j in range(len(fwd)):
            for cp in landed[j]:
                cp.wait_recv()
            for cp in fwd[j]:
                cp.start()
        for cp in from_sibling + own:
            cp.wait_recv()
        for cp in own + ici + [cp for group in fwd for cp in group]:
            cp.wait_send()

    return _Comm(list(shards), [jax.ShapeDtypeStruct((N_CHIPS,) + a.shape, a.dtype) for a in shards],
                 [pltpu.SemaphoreType.DMA((7, n)), pltpu.SemaphoreType.DMA((7, n))], start, finish)


def _all_gather(shards, *, name):
    return _run_comm(_all_gather_comm(shards), name=name)


def _pair_exchange(grads, *, name):
    n = len(grads)

    def body(*refs):
        ins, outs = refs[:n], refs[n:2 * n]
        send_sems, recv_sems = refs[2 * n:]
        x, y, c, _ = _position()
        copies = [pltpu.make_async_remote_copy(src_ref=ins[t].at[:, 1 - c], dst_ref=outs[t],
                                               send_sem=send_sems.at[t], recv_sem=recv_sems.at[t],
                                               device_id=(x, y, 1 - c), device_id_type=MESH) for t in range(n)]
        for cp in copies:
            cp.start()
        for cp in copies:
            cp.wait()

    return pl.pallas_call(
        body, in_specs=_any_specs(n), out_specs=_any_specs(n),
        out_shape=[jax.ShapeDtypeStruct((a.shape[0],) + a.shape[2:], a.dtype) for a in grads],
        scratch_shapes=[pltpu.SemaphoreType.DMA((n,)), pltpu.SemaphoreType.DMA((n,))], name=name)(*grads)


def _chip_exchange_comm(parts):
    n = len(parts)

    def copies(ins, outs, sems):
        send_sems, recv_sems = sems
        _, _, c, chips = _position()
        return [pltpu.make_async_remote_copy(src_ref=ins[t].at[2 * chip[0] + chip[1]], dst_ref=outs[t].at[j],
                                             send_sem=send_sems.at[j, t], recv_sem=recv_sems.at[j, t],
                                             device_id=(*chip, c), device_id_type=MESH)
                for j, chip in enumerate(chips) for t in range(n)]

    def start(ins, outs, sems):
        for cp in copies(ins, outs, sems):
            cp.start()

    def finish(ins, outs, sems):
        for cp in copies(ins, outs, sems):
            cp.wait()

    return _Comm(list(parts), [jax.ShapeDtypeStruct((N_CHIPS - 1,) + a.shape[1:], a.dtype) for a in parts],
                 [pltpu.SemaphoreType.DMA((N_CHIPS - 1, n)), pltpu.SemaphoreType.DMA((N_CHIPS - 1, n))],
                 start, finish)


def _pair_share(both, *, name):
    n = len(both)

    def body(*refs):
        ins, outs = refs[:n], refs[n:2 * n]
        send_sems, recv_sems = refs[2 * n:]
        x, y, c, _ = _position()
        for t in range(n):
            pltpu.make_async_remote_copy(src_ref=ins[t].at[c], dst_ref=outs[t].at[c], send_sem=send_sems.at[t],
                                         recv_sem=recv_sems.at[t], device_id=(x, y, 1 - c),
                                         device_id_type=MESH).start()
        for t in range(n):
            pltpu.make_async_remote_copy(src_ref=ins[t].at[c], dst_ref=outs[t].at[1 - c], send_sem=send_sems.at[t],
                                         recv_sem=recv_sems.at[t], device_id=(x, y, 1 - c),
                                         device_id_type=MESH).wait()

    return pl.pallas_call(
        body, in_specs=_any_specs(n), out_specs=_any_specs(n),
        out_shape=[jax.ShapeDtypeStruct(a.shape, a.dtype) for a in both],
        input_output_aliases={t: t for t in range(n)},
        scratch_shapes=[pltpu.SemaphoreType.DMA((n,)), pltpu.SemaphoreType.DMA((n,))], name=name)(*both)


def _sum_pair(grad, landed, c, wire_dtype, *, name):
    slots, _, m, n = grad.shape
    tm = _tile(m, 256, 2 * SUBLANES)

    def body(c_ref, g_ref, l_ref, o_ref):
        o_ref[...] = (g_ref[0] + l_ref[...]).astype(wire_dtype)

    return pl.pallas_call(
        body,
        grid_spec=pltpu.PrefetchScalarGridSpec(
            num_scalar_prefetch=1, grid=(slots, m // tm),
            in_specs=[pl.BlockSpec((1, 1, tm, n), lambda s, i, c_ref: (s, c_ref[0], i, 0)),
                      pl.BlockSpec((1, tm, n), lambda s, i, c_ref: (s, i, 0))],
            out_specs=pl.BlockSpec((1, tm, n), lambda s, i, c_ref: (s, i, 0))),
        out_shape=jax.ShapeDtypeStruct((slots, m, n), wire_dtype), compiler_params=_cp(2), name=name)(
            c, grad, landed)


def _sum_chips(part, landed, slot_c, *, name):
    _, m, n = part.shape
    tm = _tile(m, 256, 2 * SUBLANES)

    def body(s_ref, p_ref, l_ref, o_ref):
        acc = p_ref[0].astype(F32)
        for j in range(N_CHIPS - 1):
            acc = acc + l_ref[j].astype(F32)
        o_ref[0] = acc

    return pl.pallas_call(
        body,
        grid_spec=pltpu.PrefetchScalarGridSpec(
            num_scalar_prefetch=1, grid=(m // tm,),
            in_specs=[pl.BlockSpec((1, tm, n), lambda i, s_ref: (s_ref[0], i, 0)),
                      pl.BlockSpec((N_CHIPS - 1, tm, n), lambda i, s_ref: (0, i, 0))],
            out_specs=pl.BlockSpec((1, tm, n), lambda i, s_ref: (s_ref[1], i, 0))),
        out_shape=jax.ShapeDtypeStruct((N_CORES, m, n), F32), compiler_params=_cp(1), name=name)(
            slot_c, part, landed)


def _reduce_prepare(grads, wire_dtypes, tag, copies=None):
    c = lax.axis_index("c").reshape(1).astype(jnp.int32)

    def halves(a):
        lead, last = a.shape[1], a.shape[-1]
        mid = 1
        for s in a.shape[2:-1]:
            mid *= s
        return a.reshape(N_CHIPS, N_CORES, (lead // N_CORES) * mid, last)

    views = [halves(a) for a in grads]
    landed = _pair_exchange(views if copies is None else [halves(a) for a in copies],
                            name=f"rs_pair_exchange_{tag}")
    return [_sum_pair(v, l, c, wire_dtypes[t], name=f"rs_pair_sum_{tag}_{t}")
            for t, (v, l) in enumerate(zip(views, landed))]


def _reduce_finish(parts, landed, grads, tag):
    slot_c = jnp.stack([2 * lax.axis_index("x") + lax.axis_index("y"), lax.axis_index("c")]).astype(jnp.int32)
    both = [_sum_chips(p, l, slot_c, name=f"rs_chip_sum_{tag}_{t}") for t, (p, l) in enumerate(zip(parts, landed))]
    full = _pair_share(both, name=f"rs_pair_share_{tag}")
    return [f.reshape(a.shape[1:]) for f, a in zip(full, grads)]


def _reduce_scatter(grads, wire_dtypes, tag):
    parts = _reduce_prepare(grads, wire_dtypes, tag)
    landed = _run_comm(_chip_exchange_comm(parts), name=f"rs_chip_exchange_{tag}")
    return _reduce_finish(parts, landed, grads, tag)


class _Net:
    def __init__(self, groups, d, nh):
        self.groups, self.d, self.nh = groups, d, nh

    def has_group(self, group):
        return bool(self.groups.get(group))

    def gather_comm(self, group):
        return _all_gather_comm([shard for _, _, shard in self.groups[group]])

    def store_gathered(self, group, got, w):
        d, nh = self.d, self.nh
        for (name, layer, _), full in zip(self.groups[group], got):
            if name == "w_kvf":
                mat = full.transpose(1, 0, 2).reshape(d, -1)
                w["w_kv"][0] = (mat[:, :2 * d][None, None], 0)
                w["w_f"][0] = (jnp.zeros((d, LANES), BF16).at[:, :nh].set(mat[:, 2 * d:])[None, None], 0)
            elif name in ("w_in", "w_glu"):
                w[name][layer] = (full[:, None], 0)
            else:
                w[name][layer] = (full.reshape(1, 1, -1, full.shape[-1]), 0)

    def reduce_prepare(self, grads, copies, tag):
        return _reduce_prepare(grads, [BF16] * len(grads), tag, copies)

    def reduce_finish(self, parts, landed, grads, tag):
        return _reduce_finish(parts, landed, grads, tag)

    def reduce_blocking(self, grads, tag):
        return _reduce_scatter(grads, [BF16] * len(grads), tag)


def _adamw(w, g, m, v, *, name):
    def fn(ww, gg, mm, vv):
        mm = ADAM_B1 * mm + (1.0 - ADAM_B1) * gg
        vv = ADAM_B2 * vv + (1.0 - ADAM_B2) * (gg * gg)
        m_hat = mm / (1.0 - ADAM_B1 ** ADAM_STEP)
        v_hat = vv / (1.0 - ADAM_B2 ** ADAM_STEP)
        delta = -ADAM_LR * (m_hat / (jnp.sqrt(v_hat) + ADAM_EPS) + ADAM_WD * ww)
        return delta, mm, vv

    shape = w.shape
    two_d = [a.reshape(-1, shape[-1]) for a in (w, g, m, v)]
    outs = _rowwise(fn, two_d, [], [(shape[-1], F32)] * 3, name=name)
    return [o.reshape(shape) for o in outs]


def _to_bf16(a, *, name):
    two_d = a.reshape(-1, a.shape[-1])
    return _rowwise(lambda t: t, [two_d], [], [(a.shape[-1], BF16)], name=name)[0].reshape(a.shape)


def _pack(arrays, rows_multiple):
    flat = jnp.concatenate([a.reshape(-1) for a in arrays])
    rows = -(-flat.shape[0] // LANES)
    rows = -(-rows // rows_multiple) * rows_multiple
    return jnp.pad(flat, (0, rows * LANES - flat.shape[0])).reshape(rows, LANES)


def _unpack(packed, like):
    flat = packed.reshape(-1)
    out, pos = [], 0
    for a in like:
        out.append(flat[pos:pos + a.size].reshape(a.shape))
        pos += a.size
    return out


_PARAMS = ("g_mix", "g_ffn", "lam_re", "lam_im", "log_dt", "ssm_b_re", "ssm_b_im", "ssm_c_re", "ssm_c_im", "ssm_d",
           "w_glu", "g_kv", "w_kvf", "b_f", "w_q", "w_o", "w_ffn_in", "ffn_conv_w", "ffn_conv_b", "w_ffn_out",
           "g_final")
_BIG = ("w_glu", "w_kvf", "w_q", "w_o", "w_ffn_in", "w_ffn_out")
_SMALL_SHARDED = ("ssm_d", "ffn_conv_w")


def kernel(x, g_mix, g_ffn, lam_re, lam_im, log_dt, ssm_b_re, ssm_b_im, ssm_c_re, ssm_c_im, ssm_d, w_glu, g_kv, w_kvf, b_f, w_q, w_o, w_ffn_in, ffn_conv_w, ffn_conv_b, w_ffn_out, g_final, loss_target, m_g_mix, m_g_ffn, m_lam_re, m_lam_im, m_log_dt, m_ssm_b_re, m_ssm_b_im, m_ssm_c_re, m_ssm_c_im, m_ssm_d, m_w_glu, m_g_kv, m_w_kvf, m_b_f, m_w_q, m_w_o, m_w_ffn_in, m_ffn_conv_w, m_ffn_conv_b, m_w_ffn_out, m_g_final, v_g_mix, v_g_ffn, v_lam_re, v_lam_im, v_log_dt, v_ssm_b_re, v_ssm_b_im, v_ssm_c_re, v_ssm_c_im, v_ssm_d, v_w_glu, v_g_kv, v_w_kvf, v_b_f, v_w_q, v_w_o, v_w_ffn_in, v_ffn_conv_w, v_ffn_conv_b, v_w_ffn_out, v_g_final):
    p = dict(g_mix=g_mix, g_ffn=g_ffn, lam_re=lam_re, lam_im=lam_im, log_dt=log_dt, ssm_b_re=ssm_b_re,
             ssm_b_im=ssm_b_im, ssm_c_re=ssm_c_re, ssm_c_im=ssm_c_im, ssm_d=ssm_d, w_glu=w_glu, g_kv=g_kv,
             w_kvf=w_kvf, b_f=b_f, w_q=w_q, w_o=w_o, w_ffn_in=w_ffn_in, ffn_conv_w=ffn_conv_w,
             ffn_conv_b=ffn_conv_b, w_ffn_out=w_ffn_out, g_final=g_final)
    mom1 = dict(zip(_PARAMS, (m_g_mix, m_g_ffn, m_lam_re, m_lam_im, m_log_dt, m_ssm_b_re, m_ssm_b_im, m_ssm_c_re,
                              m_ssm_c_im, m_ssm_d, m_w_glu, m_g_kv, m_w_kvf, m_b_f, m_w_q, m_w_o, m_w_ffn_in,
                              m_ffn_conv_w, m_ffn_conv_b, m_w_ffn_out, m_g_final)))
    mom2 = dict(zip(_PARAMS, (v_g_mix, v_g_ffn, v_lam_re, v_lam_im, v_log_dt, v_ssm_b_re, v_ssm_b_im, v_ssm_c_re,
                              v_ssm_c_im, v_ssm_d, v_w_glu, v_g_kv, v_w_kvf, v_b_f, v_w_q, v_w_o, v_w_ffn_in,
                              v_ffn_conv_w, v_ffn_conv_b, v_w_ffn_out, v_g_final)))
    d = x.shape[-1]
    nh = b_f.shape[0]
    slot = 2 * lax.axis_index("x") + lax.axis_index("y")

    wb = {k: _to_bf16(p[k], name=f"to_bf16_{k}") for k in _BIG}
    gd, gcw = _all_gather([ssm_d, ffn_conv_w], name="first_all_gather")
    n_lay, n_s5 = w_ffn_in.shape[0], lam_re.shape[0]
    n_fox = n_lay - n_s5
    groups = {f"stage{l}": [("w_in", l, wb["w_ffn_in"][l]), ("w_out", l, wb["w_ffn_out"][l])] for l in range(n_lay)}
    for l in range(n_s5):
        groups[f"stage{l}"].append(("w_glu", l, wb["w_glu"][l]))
    groups[f"stage{n_s5 - 1}"] += [("w_kvf", 0, wb["w_kvf"])] + [(k, j, wb[k][j]) for k in ("w_q", "w_o")
                                                                for j in range(n_fox)]
    w = dict(p)
    w.update(w_glu=[None] * n_s5, w_in=[None] * n_lay, w_out=[None] * n_lay, w_q=[None] * n_fox,
             w_o=[None] * n_fox, w_kv=[None], w_f=[None],
             conv_w=gcw.transpose(1, 2, 0, 3).reshape(n_lay, CONV_TAPS, -1), conv_b=ffn_conv_b,
             ssm_d=gd.transpose(1, 0, 2).reshape(gd.shape[1], d))

    loss_part, grad_x, g = _local_step(x[0], loss_target[0], w, _Net(groups, d, nh))
    loss = lax.psum(loss_part, ("x", "y", "c"))

    small_names = [k for k in _PARAMS if k not in _BIG]
    small_full = dict(g_mix=g["g_mix"], g_ffn=g["g_ffn"], lam_re=g["lam_re"], lam_im=g["lam_im"], log_dt=g["log_dt"],
                      ssm_b_re=g["ssm_b_re"], ssm_b_im=g["ssm_b_im"], ssm_c_re=g["ssm_c_re"], ssm_c_im=g["ssm_c_im"],
                      ssm_d=g["ssm_d"], g_kv=g["g_kv"], b_f=g["b_f"], ffn_conv_w=g["conv_w"],
                      ffn_conv_b=g["conv_b"], g_final=g["g_final"])
    small_list = [small_full[k] for k in small_names]
    pack = _pack(small_list, N_CHIPS * N_CORES * 2 * SUBLANES)
    pack4 = pack.reshape(N_CHIPS, pack.shape[0] // N_CHIPS, LANES)
    pack_shard = _reduce_scatter([pack4], [F32], "small")[0]
    red_big = {k: g["big"][k, 0] if p[k].ndim == 2 else jnp.stack([g["big"][k, l] for l in range(p[k].shape[0])])
               for k in _BIG}
    pack_all = _all_gather([pack_shard], name="small_grads_all_gather")[0]
    red_small = dict(zip(small_names, _unpack(pack_all, small_list)))
    for k in _SMALL_SHARDED:
        width = p[k].shape[-1]
        red_small[k] = lax.dynamic_slice_in_dim(red_small[k], slot * width, width, axis=red_small[k].ndim - 1)

    grads, deltas, new_m, new_v = {}, {}, {}, {}
    for k in _BIG:
        grads[k] = red_big[k]
        deltas[k], new_m[k], new_v[k] = _adamw(p[k], grads[k], mom1[k], mom2[k], name=f"adamw_{k}")
    packs = [_pack([src[k] for k in small_names], SUBLANES) for src in (p, red_small, mom1, mom2)]
    like = [p[k] for k in small_names]
    outs = [_unpack(o, like) for o in _adamw(*packs, name="adamw_small")]
    for i, k in enumerate(small_names):
        grads[k] = red_small[k]
        deltas[k], new_m[k], new_v[k] = outs[0][i], outs[1][i], outs[2][i]
    return (loss, grad_x[None], *[grads[k] for k in _PARAMS], *[deltas[k] for k in _PARAMS],
            *[new_m[k] for k in _PARAMS], *[new_v[k] for k in _PARAMS])
```

```python
import functools

import jax
import jax.numpy as jnp
from jax import lax
from jax.experimental import pallas as pl
from jax.experimental.pallas import tpu as pltpu

F32 = jnp.float32
BF16 = jnp.bfloat16

RMS_EPS = 1e-6
ADAM_LR = 0.001
ADAM_B1 = 0.9
ADAM_B2 = 0.999
ADAM_EPS = 1e-08
ADAM_WD = 0.01
ADAM_STEP = 10
CONV_TAPS = 3

LANES = 128
SUBLANES = 8
HEAD_DIM = 64
FLASH_ROW_TILE = 32
S5_BLOCK_GROUPS = 16
VMEM_LIMIT_BYTES = 48 << 20
MM_BLOCK_BUDGET_BYTES = 30 << 20
N_CHIPS = 4
N_CORES = 2
MESH = pl.DeviceIdType.MESH


def _cp(n_grid):
    return pltpu.CompilerParams(dimension_semantics=("arbitrary",) * n_grid, vmem_limit_bytes=VMEM_LIMIT_BYTES)


def _tile(n, pref, mult=SUBLANES):
    if n <= pref:
        return n
    t = (pref // mult) * mult
    while t >= mult:
        if n % t == 0:
            return t
        t -= mult
    return n


class _Comm:
    def __init__(self, ins, out_shapes, sems, start, finish):
        self.ins, self.out_shapes, self.sems, self.start, self.finish = ins, out_shapes, sems, start, finish


def _any_specs(n):
    return [pl.BlockSpec(memory_space=pl.ANY)] * n


def _run_comm(comm, *, name):
    n_in, n_out = len(comm.ins), len(comm.out_shapes)

    def body(*refs):
        ins, outs, sems = refs[:n_in], refs[n_in:n_in + n_out], refs[n_in + n_out:]
        comm.start(ins, outs, sems)
        comm.finish(ins, outs, sems)

    return pl.pallas_call(body, in_specs=_any_specs(n_in), out_specs=_any_specs(n_out),
                          out_shape=list(comm.out_shapes), scratch_shapes=list(comm.sems), name=name)(*comm.ins)


def _call(body, *, grid, in_specs, out_specs, out_shape, args, name, scratch_shapes=(), prefetch=(), comm=None):
    n_pre, n_in, n_out, n_scr = len(prefetch), len(in_specs), len(out_specs), len(scratch_shapes)
    in_specs, out_specs, out_shape = list(in_specs), list(out_specs), list(out_shape)
    scratch_shapes, args = list(scratch_shapes), list(args)
    kernel_body = body
    if comm is not None:
        n_cin, n_cout = len(comm.ins), len(comm.out_shapes)

        def kernel_body(*refs):
            pos = n_pre + n_in
            c_in = refs[pos:pos + n_cin]
            main_out = refs[pos + n_cin:pos + n_cin + n_out]
            pos += n_cin + n_out
            c_out = refs[pos:pos + n_cout]
            main_scr = refs[pos + n_cout:pos + n_cout + n_scr]
            sems = refs[pos + n_cout + n_scr:]
            ids = [pl.program_id(a) for a in range(len(grid))]
            first = functools.reduce(jnp.logical_and, [i == 0 for i in ids])
            last = functools.reduce(jnp.logical_and, [i == g - 1 for i, g in zip(ids, grid)])
            pl.when(first)(lambda: comm.start(c_in, c_out, sems))
            body(*refs[:n_pre + n_in], *main_out, *main_scr)
            pl.when(last)(lambda: comm.finish(c_in, c_out, sems))

        in_specs += _any_specs(n_cin)
        out_specs += _any_specs(n_cout)
        out_shape += list(comm.out_shapes)
        scratch_shapes += list(comm.sems)
        args += list(comm.ins)
    if prefetch:
        spec = pltpu.PrefetchScalarGridSpec(num_scalar_prefetch=n_pre, grid=grid, in_specs=in_specs,
                                            out_specs=out_specs, scratch_shapes=scratch_shapes)
        res = pl.pallas_call(kernel_body, grid_spec=spec, out_shape=out_shape, compiler_params=_cp(len(grid)),
                             name=name)(*prefetch, *args)
    else:
        res = pl.pallas_call(kernel_body, grid=grid, in_specs=in_specs, out_specs=out_specs, out_shape=out_shape,
                             scratch_shapes=scratch_shapes, compiler_params=_cp(len(grid)), name=name)(*args)
    return (res[:n_out], res[n_out:]) if comm is not None else res


def _row_tile(m, bytes_per_row, fixed_bytes):
    for tm in (1024, 512):
        if m % tm == 0 and 2 * (tm * bytes_per_row + fixed_bytes) <= MM_BLOCK_BUDGET_BYTES:
            return tm
    return _tile(m, 512)


def _dot(a, b, ca, cb):
    return lax.dot_general(a, b, (((ca,), (cb,)), ((), ())), preferred_element_type=F32)


def _epilogue_io(epilogue, m, tm, rows_axis, grid_rank):
    _, rows, consts, outs, accs = epilogue

    def at_rows(width):
        return pl.BlockSpec((tm, width), lambda *g: (g[rows_axis], 0))

    def whole(shape):
        return pl.BlockSpec(shape, lambda *g: (0,) * len(shape))

    in_specs = [at_rows(r.shape[1]) for r in rows] + [whole(c.shape) for c in consts]
    out_specs = [at_rows(wd) for wd, _ in outs] + [whole(s) for s in accs]
    out_shape = ([jax.ShapeDtypeStruct((m, wd), dt) for wd, dt in outs]
                 + [jax.ShapeDtypeStruct(s, F32) for s in accs])
    bytes_per_row = (sum(r.shape[1] * r.dtype.itemsize for r in rows)
                     + sum(wd * jnp.dtype(dt).itemsize for wd, dt in outs))
    return in_specs, out_specs, out_shape, bytes_per_row


def _epilogue_apply(epilogue, block, refs, first_row_tile):
    fn, rows, consts, outs, _ = epilogue
    n_in, n_out = len(rows) + len(consts), len(outs)
    res = fn(block, *[r[...] for r in refs[:n_in]])
    for o, val in zip(refs[n_in:n_in + n_out], res[:n_out]):
        o[...] = val.astype(o.dtype)
    a_refs = refs[n_in + n_out:]
    if a_refs:
        @pl.when(first_row_tile)
        def _():
            for a in a_refs:
                a[...] = jnp.zeros_like(a)
        for a, val in zip(a_refs, res[n_out:]):
            a[...] += val


def _mm_cols(x, w4, layer, *, wc, out_dtype=F32, scale=None, epilogue=None, name):
    m, k = x.shape
    slots, _, k0, k1 = w4.shape
    nb = k1 if wc == 0 else k0
    assert (k0 if wc == 0 else k1) == k
    if epilogue is None:
        tm = _row_tile(m, k * x.dtype.itemsize + nb * jnp.dtype(out_dtype).itemsize, k0 * k1 * w4.dtype.itemsize)
        extra_in, out_specs = [], pl.BlockSpec((tm, nb), lambda s, i: (i, s))
        out_shape = jax.ShapeDtypeStruct((m, slots * nb), out_dtype)
    else:
        assert slots == 1
        bytes_per_row = _epilogue_io(epilogue, m, SUBLANES, 1, 2)[3]
        tm = _row_tile(m, k * x.dtype.itemsize + bytes_per_row, k0 * k1 * w4.dtype.itemsize)
        extra_in, out_specs, out_shape, _ = _epilogue_io(epilogue, m, tm, 1, 2)

    def body(x_ref, w_ref, *refs):
        acc = _dot(x_ref[...].astype(BF16), w_ref[0, 0], 1, wc)
        if scale is not None:
            acc = acc * scale
        if epilogue is None:
            refs[0][...] = acc.astype(out_dtype)
        else:
            _epilogue_apply(epilogue, acc, refs, pl.program_id(1) == 0)

    extra_args = [] if epilogue is None else [*epilogue[1], *epilogue[2]]
    return pl.pallas_call(
        body, grid=(slots, m // tm),
        in_specs=[pl.BlockSpec((tm, k), lambda s, i: (i, 0)),
                  pl.BlockSpec((1, 1, k0, k1), lambda s, i: (s, layer, 0, 0)), *extra_in],
        out_specs=out_specs, out_shape=out_shape,
        compiler_params=_cp(2), name=name)(x, w4, *extra_args)


def _planes(a):
    return a if a.ndim == 3 else a[None]


def _mm_acc(x, w4, layer, *, wc, epilogue=None, name):
    x = _planes(x)
    n_planes, m, width = x.shape
    slots, _, k0, k1 = w4.shape
    kb = k0 if wc == 0 else k1
    nout = k1 if wc == 0 else k0
    assert n_planes * width == slots * kb
    spp = slots // n_planes
    x_spec_w = pl.BlockSpec((1, 1, k0, k1), lambda i, s: (s, layer, 0, 0))
    if epilogue is None:
        tm = _row_tile(m, kb * x.dtype.itemsize + nout * 4, k0 * k1 * w4.dtype.itemsize)

        def body(x_ref, w_ref, o_ref):
            @pl.when(pl.program_id(1) == 0)
            def _():
                o_ref[...] = jnp.zeros_like(o_ref)
            o_ref[...] += _dot(x_ref[0].astype(BF16), w_ref[0, 0], 1, wc)

        return pl.pallas_call(
            body, grid=(m // tm, slots),
            in_specs=[pl.BlockSpec((1, tm, kb), lambda i, s: (s // spp, i, s % spp)), x_spec_w],
            out_specs=pl.BlockSpec((tm, nout), lambda i, s: (i, 0)),
            out_shape=jax.ShapeDtypeStruct((m, nout), F32),
            compiler_params=_cp(2), name=name)(x, w4)

    bytes_per_row = _epilogue_io(epilogue, m, SUBLANES, 0, 2)[3]
    tm = _row_tile(m, kb * x.dtype.itemsize + nout * 2 + bytes_per_row, k0 * k1 * w4.dtype.itemsize)
    extra_in, out_specs, out_shape, _ = _epilogue_io(epilogue, m, tm, 0, 2)

    def body(x_ref, w_ref, *refs):
        acc = refs[-1]

        @pl.when(pl.program_id(1) == 0)
        def _():
            acc[...] = jnp.zeros_like(acc)
        acc[...] += _dot(x_ref[0].astype(BF16), w_ref[0, 0], 1, wc)

        @pl.when(pl.program_id(1) == slots - 1)
        def _():
            _epilogue_apply(epilogue, acc[...], refs[:-1], pl.program_id(0) == 0)

    return pl.pallas_call(
        body, grid=(m // tm, slots),
        in_specs=[pl.BlockSpec((1, tm, kb), lambda i, s: (s // spp, i, s % spp)), x_spec_w, *extra_in],
        out_specs=out_specs, out_shape=out_shape, scratch_shapes=[pltpu.VMEM((tm, nout), F32)],
        compiler_params=_cp(2), name=name)(x, w4, *epilogue[1], *epilogue[2])


def _mm_tn(x, dy, slots, *, scale=None, wire=False, name):
    m, k = x.shape
    dy = _planes(dy)
    n_planes, _, width = dy.shape
    n = n_planes * width // slots
    spp = slots // n_planes
    ta = _tile(k, 512, LANES)
    tm = m
    while tm > 512 and tm % 2 == 0 and (2 * tm * (ta * x.dtype.itemsize + n * dy.dtype.itemsize)
                                         + 2 * ta * n * 4) > MM_BLOCK_BUDGET_BYTES:
        tm //= 2
    n_m = m // tm

    def body(x_ref, dy_ref, o_ref, *wire_ref):
        @pl.when(pl.program_id(2) == 0)
        def _():
            o_ref[...] = jnp.zeros_like(o_ref)
        o_ref[0] += _dot(x_ref[...].astype(BF16), dy_ref[0].astype(BF16), 0, 0)
        if scale is not None or wire:
            @pl.when(pl.program_id(2) == n_m - 1)
            def _():
                if scale is not None:
                    o_ref[...] = o_ref[...] * scale
                if wire:
                    wire_ref[0][...] = o_ref[...].astype(BF16)

    out_spec = pl.BlockSpec((1, ta, n), lambda s, a, i: (s, a, 0))
    return pl.pallas_call(
        body, grid=(slots, k // ta, n_m),
        in_specs=[pl.BlockSpec((tm, ta), lambda s, a, i: (i, a)),
                  pl.BlockSpec((1, tm, n), lambda s, a, i: (s // spp, i, s % spp))],
        out_specs=[out_spec, out_spec] if wire else out_spec,
        out_shape=([jax.ShapeDtypeStruct((slots, k, n), dt) for dt in (F32, BF16)] if wire
                   else jax.ShapeDtypeStruct((slots, k, n), F32)),
        compiler_params=_cp(3), name=name)(x, dy)


def _rowwise(fn, rows, consts, outs, accs=(), *, tl=256, name):
    n_rows = rows[0].shape[0]
    tl = _tile(n_rows, tl)
    n_in = len(rows) + len(consts)
    n_out = len(outs)

    def body(*refs):
        res = fn(*[r[...] for r in refs[:n_in]])
        res = res if isinstance(res, (tuple, list)) else (res,)
        o_refs = refs[n_in:n_in + n_out]
        a_refs = refs[n_in + n_out:]
        for o, val in zip(o_refs, res[:n_out]):
            o[...] = val.astype(o.dtype)
        if a_refs:
            @pl.when(pl.program_id(0) == 0)
            def _():
                for a in a_refs:
                    a[...] = jnp.zeros_like(a)
            for a, val in zip(a_refs, res[n_out:]):
                a[...] += val

    in_specs = ([pl.BlockSpec((tl, r.shape[1]), lambda i: (i, 0)) for r in rows]
                + [pl.BlockSpec(c.shape, lambda i: (0, 0)) for c in consts])
    out_specs = ([pl.BlockSpec((tl, w), lambda i: (i, 0)) for w, _ in outs]
                 + [pl.BlockSpec(s, lambda i: (0, 0)) for s in accs])
    out_shape = ([jax.ShapeDtypeStruct((n_rows, w), dt) for w, dt in outs]
                 + [jax.ShapeDtypeStruct(s, F32) for s in accs])
    return pl.pallas_call(body, grid=(n_rows // tl,), in_specs=in_specs, out_specs=out_specs,
                          out_shape=out_shape, compiler_params=_cp(1), name=name)(*rows, *consts)


def _rms(x, g):
    return x * lax.rsqrt(jnp.mean(x * x, axis=-1, keepdims=True) + RMS_EPS) * g


def _sigmoid(x):
    return 1.0 / (1.0 + jnp.exp(-x))


def _glu(zz):
    d = zz.shape[1] // 2
    return zz[:, :d] * _sigmoid(zz[:, d:])


def _gelu(y):
    return jax.nn.gelu(y)


def _row2(v):
    return v.reshape(1, -1)


def _node_bwd_fn(counts):
    n_dy = sum(counts)

    def fn(d, hh, *rest):
        dys, gs = rest[:n_dy], rest[n_dy:]
        tot, dgs, pos = d, [], 0
        for g, cnt in zip(gs, counts):
            dy = dys[pos].astype(F32)
            for extra in dys[pos + 1:pos + cnt]:
                dy = dy + extra.astype(F32)
            pos += cnt
            _, vjp = jax.vjp(_rms, hh, g)
            dx, dg = vjp(dy)
            tot = tot + dx
            dgs.append(dg)
        return (tot, tot, *dgs)

    return fn


def _node_bwd(d_in, h, branches, *, name):
    width = h.shape[1]
    flat = [dy for _, dys in branches for dy in dys]
    res = _rowwise(_node_bwd_fn([len(dys) for _, dys in branches]), [d_in, h, *flat],
                   [_row2(g) for g, _ in branches], [(width, F32), (width, BF16)], [(1, width)] * len(branches),
                   name=name)
    return res[0], res[1], [r[0] for r in res[2:]]


def _s5_prep_fn(lr, li, ldt, br, bi, cr, ci, *, gq, h, p):
    dt = jnp.exp(ldt)
    mag = jnp.exp(lr * dt)
    lb_re = mag * jnp.cos(li * dt)
    lb_im = mag * jnp.sin(li * dt)
    den = lr * lr + li * li
    nr = lb_re - 1.0
    fr = (nr * lr + lb_im * li) / den
    fi = (lb_im * lr - nr * li) / den
    bb_re = fr * br - fi * bi
    bb_im = fr * bi + fi * br
    shape = (gq * h, gq * p)
    r = lax.broadcasted_iota(jnp.int32, shape, 0)
    c = lax.broadcasted_iota(jnp.int32, shape, 1)
    mask = jnp.where(jnp.right_shift(r, h.bit_length() - 1) == jnp.right_shift(c, p.bit_length() - 1), 1.0, 0.0)

    def expand(t):
        return jnp.concatenate([t] * gq, axis=0) * mask

    return lb_re, lb_im, expand(bb_re), expand(bb_im), expand(cr), expand(ci)


def _s5_prep(lr, li, ldt, br, bi, cr, ci, p, *, name):
    n = lr.shape[1]
    h = br.shape[0]
    gq = S5_BLOCK_GROUPS
    nq, cq = gq * p, gq * h
    nblk = n // nq
    fn = functools.partial(_s5_prep_fn, gq=gq, h=h, p=p)

    def body(lr_r, li_r, ldt_r, br_r, bi_r, cr_r, ci_r, lbr_o, lbi_o, wbr_o, wbi_o, wcr_o, wci_o):
        lb_re, lb_im, wbr, wbi, wcr, wci = fn(lr_r[...], li_r[...], ldt_r[...], br_r[...], bi_r[...],
                                              cr_r[...], ci_r[...])
        lbr_o[...] = lb_re
        lbi_o[...] = lb_im
        wbr_o[0] = wbr.astype(BF16)
        wbi_o[0] = wbi.astype(BF16)
        wcr_o[0] = wcr.astype(BF16)
        wci_o[0] = wci.astype(BF16)

    vec = pl.BlockSpec((1, nq), lambda q: (0, q))
    tab = pl.BlockSpec((h, nq), lambda q: (0, q))
    wsp = pl.BlockSpec((1, cq, nq), lambda q: (q, 0, 0))
    wsh = jax.ShapeDtypeStruct((nblk, cq, nq), BF16)
    vsh = jax.ShapeDtypeStruct((1, n), F32)
    return pl.pallas_call(body, grid=(nblk,), in_specs=[vec, vec, vec, tab, tab, tab, tab],
                          out_specs=[vec, vec, wsp, wsp, wsp, wsp], out_shape=[vsh, vsh, wsh, wsh, wsh, wsh],
                          compiler_params=_cp(1), name=name)(lr, li, ldt, br, bi, cr, ci)


def _s5_prep_bwd(lr, li, ldt, br, bi, cr, ci, p, dlbr, dlbi, dwbr, dwbi, dwcr, dwci, *, name):
    n = lr.shape[1]
    h = br.shape[0]
    gq = S5_BLOCK_GROUPS
    nq, cq = gq * p, gq * h
    nblk = n // nq
    fn = functools.partial(_s5_prep_fn, gq=gq, h=h, p=p)

    def body(lr_r, li_r, ldt_r, br_r, bi_r, cr_r, ci_r, dlbr_r, dlbi_r, dwbr_r, dwbi_r, dwcr_r, dwci_r,
             *outs):
        _, vjp = jax.vjp(fn, lr_r[...], li_r[...], ldt_r[...], br_r[...], bi_r[...], cr_r[...], ci_r[...])
        grads = vjp((dlbr_r[0], dlbi_r[0], dwbr_r[0], dwbi_r[0], dwcr_r[0], dwci_r[0]))
        for o, g in zip(outs, grads):
            o[...] = g

    vec = pl.BlockSpec((1, nq), lambda q: (0, q))
    tab = pl.BlockSpec((h, nq), lambda q: (0, q))
    vec3 = pl.BlockSpec((1, 1, nq), lambda q: (q, 0, 0))
    wsp = pl.BlockSpec((1, cq, nq), lambda q: (q, 0, 0))
    vsh = jax.ShapeDtypeStruct((1, n), F32)
    tsh = jax.ShapeDtypeStruct((h, n), F32)
    return pl.pallas_call(body, grid=(nblk,),
                          in_specs=[vec, vec, vec, tab, tab, tab, tab, vec3, vec3, wsp, wsp, wsp, wsp],
                          out_specs=[vec, vec, vec, tab, tab, tab, tab],
                          out_shape=[vsh, vsh, vsh, tsh, tsh, tsh, tsh],
                          compiler_params=_cp(1), name=name)(lr, li, ldt, br, bi, cr, ci,
                                                             dlbr, dlbi, dwbr, dwbi, dwcr, dwci)


def _scan_rows(s_re, s_im, a_re, a_im, c_re, c_im, *, reverse):
    t_rows, n = s_re.shape
    nb = t_rows // SUBLANES
    row = lax.broadcasted_iota(jnp.int32, (SUBLANES, n), 0)

    def cmul(x, y):
        return x[0] * y[0] - x[1] * y[1], x[0] * y[1] + x[1] * y[0]

    a1 = (jnp.broadcast_to(a_re, (SUBLANES, n)), jnp.broadcast_to(a_im, (SUBLANES, n)))
    a2 = cmul(a1, a1)
    a4 = cmul(a2, a2)
    steps = []
    for dist, (pr, pi) in ((1, a1), (2, a2), (4, a4)):
        keep = (row < SUBLANES - dist) if reverse else (row >= dist)
        steps.append((SUBLANES - dist if reverse else dist, (jnp.where(keep, pr, 0.0), jnp.where(keep, pi, 0.0))))
    pk = (a_re, a_im)
    tab_re = jnp.zeros((SUBLANES, n), F32)
    tab_im = jnp.zeros((SUBLANES, n), F32)
    for i in range(SUBLANES):
        at = (SUBLANES - 1 - i) if reverse else i
        tab_re = jnp.where(row == at, pk[0], tab_re)
        tab_im = jnp.where(row == at, pk[1], tab_im)
        pk = cmul(pk, (a_re, a_im))

    def step(b, carry):
        cr, ci = carry
        blk = (nb - 1 - b) if reverse else b
        off = pl.multiple_of(blk * SUBLANES, SUBLANES)
        x_re = s_re[pl.ds(off, SUBLANES), :]
        x_im = s_im[pl.ds(off, SUBLANES), :]
        for sh, (pr, pi) in steps:
            sh_re = pltpu.roll(x_re, sh, 0)
            sh_im = pltpu.roll(x_im, sh, 0)
            x_re, x_im = x_re + pr * sh_re - pi * sh_im, x_im + pr * sh_im + pi * sh_re
        x_re, x_im = x_re + tab_re * cr - tab_im * ci, x_im + tab_re * ci + tab_im * cr
        s_re[pl.ds(off, SUBLANES), :] = x_re
        s_im[pl.ds(off, SUBLANES), :] = x_im
        edge = 0 if reverse else SUBLANES - 1
        return x_re[edge:edge + 1, :], x_im[edge:edge + 1, :]

    return lax.fori_loop(0, nb, step, (c_re, c_im))


def _s5_fwd(u, prep, dskip, *, name, comm=None):
    lb_re, lb_im, wbr, wbi, wcr, wci = prep
    n_rows, _ = u.shape
    nblk, cq, nq = wbr.shape
    tt = _tile(n_rows, 512)
    nch = n_rows // tt

    def body(u_ref, wbr_r, wbi_r, wcr_r, wci_r, lbr_r, lbi_r, d_ref, y_ref, z_ref, s_re, s_im, sbr_o, sbi_o,
             c_re, c_im):
        @pl.when(pl.program_id(1) == 0)
        def _():
            c_re[...] = jnp.zeros_like(c_re)
            c_im[...] = jnp.zeros_like(c_im)
        uf = u_ref[...]
        ub = uf.astype(BF16)
        s_re[...] = _dot(ub, wbr_r[0], 1, 0)
        s_im[...] = _dot(ub, wbi_r[0], 1, 0)
        sbr_o[0] = c_re[...]
        sbi_o[0] = c_im[...]
        cr, ci = _scan_rows(s_re, s_im, lbr_r[...], lbi_r[...], c_re[...], c_im[...], reverse=False)
        c_re[...] = cr
        c_im[...] = ci
        y = _dot(s_re[...].astype(BF16), wcr_r[0], 1, 1) - _dot(s_im[...].astype(BF16), wci_r[0], 1, 1)
        y = y + d_ref[...] * uf
        y_ref[...] = y
        z_ref[...] = _gelu(y).astype(BF16)

    wsp = pl.BlockSpec((1, cq, nq), lambda q, i: (q, 0, 0))
    vec = pl.BlockSpec((1, nq), lambda q, i: (0, q))
    act = pl.BlockSpec((tt, cq), lambda q, i: (i, q))
    sb = pl.BlockSpec((1, 1, nq), lambda q, i: (i, 0, q))
    sbsh = jax.ShapeDtypeStruct((nch, 1, nblk * nq), F32)
    states = pl.BlockSpec((tt, nq), lambda q, i: (i, q))
    stsh = jax.ShapeDtypeStruct((n_rows, nblk * nq), F32)
    return _call(
        body, grid=(nblk, nch),
        in_specs=[act, wsp, wsp, wsp, wsp, vec, vec, pl.BlockSpec((1, cq), lambda q, i: (0, q))],
        out_specs=[act, act, states, states, sb, sb],
        out_shape=[jax.ShapeDtypeStruct(u.shape, F32), jax.ShapeDtypeStruct(u.shape, BF16), stsh, stsh, sbsh, sbsh],
        scratch_shapes=[pltpu.VMEM((1, nq), F32), pltpu.VMEM((1, nq), F32)],
        args=(u, wbr, wbi, wcr, wci, lb_re, lb_im, dskip), name=name, comm=comm)


def _s5_bwd(u, dy, st_re, st_im, sb_re, sb_im, prep, dskip, *, name, comm=None):
    lb_re, lb_im, wbr, wbi, wcr, wci = prep
    n_rows, _ = u.shape
    nblk, cq, nq = wbr.shape
    tt = _tile(n_rows, 512)
    nch = n_rows // tt

    def body(u_ref, dy_ref, s_re, s_im, sbr_r, sbi_r, wbr_r, wbi_r, wcr_r, wci_r, lbr_r, lbi_r, d_ref,
             du_ref, dwbr, dwbi, dwcr, dwci, dlbr, dlbi, dd_ref, g_re, g_im, lc_re, lc_im):
        @pl.when(pl.program_id(1) == 0)
        def _():
            for ref in (lc_re, lc_im, dwbr, dwbi, dwcr, dwci, dlbr, dlbi, dd_ref):
                ref[...] = jnp.zeros_like(ref)
        uf = u_ref[...]
        ub = uf.astype(BF16)
        dyf = dy_ref[...]
        dyb = dyf.astype(BF16)
        sr16 = s_re[...].astype(BF16)
        si16 = s_im[...].astype(BF16)
        dwcr[0] += _dot(dyb, sr16, 0, 0)
        dwci[0] -= _dot(dyb, si16, 0, 0)
        g_re[...] = _dot(dyb, wcr_r[0], 1, 0)
        g_im[...] = -_dot(dyb, wci_r[0], 1, 0)
        lcr, lci = _scan_rows(g_re, g_im, lbr_r[...], -lbi_r[...], lc_re[...], lc_im[...], reverse=True)
        lc_re[...] = lcr
        lc_im[...] = lci
        lam_r = g_re[...]
        lam_i = g_im[...]
        first = lax.broadcasted_iota(jnp.int32, (tt, nq), 0) == 0
        prev_r = jnp.where(first, sbr_r[0], pltpu.roll(s_re[...], 1, 0))
        prev_i = jnp.where(first, sbi_r[0], pltpu.roll(s_im[...], 1, 0))
        dlbr[0] += jnp.sum(lam_r * prev_r + lam_i * prev_i, axis=0, keepdims=True)
        dlbi[0] += jnp.sum(lam_i * prev_r - lam_r * prev_i, axis=0, keepdims=True)
        lr16 = lam_r.astype(BF16)
        li16 = lam_i.astype(BF16)
        du_ref[...] = _dot(lr16, wbr_r[0], 1, 1) + _dot(li16, wbi_r[0], 1, 1) + d_ref[...] * dyf
        dwbr[0] += _dot(ub, lr16, 0, 0)
        dwbi[0] += _dot(ub, li16, 0, 0)
        dd_ref[0] += jnp.sum(dyf * uf, axis=0, keepdims=True)

    last = nch - 1
    wsp = pl.BlockSpec((1, cq, nq), lambda q, i: (q, 0, 0))
    vec = pl.BlockSpec((1, nq), lambda q, i: (0, q))
    act = pl.BlockSpec((tt, cq), lambda q, i: (last - i, q))
    sb = pl.BlockSpec((1, 1, nq), lambda q, i: (last - i, 0, q))
    vec3 = pl.BlockSpec((1, 1, nq), lambda q, i: (q, 0, 0))
    dsp = pl.BlockSpec((1, 1, cq), lambda q, i: (q, 0, 0))
    wsh = jax.ShapeDtypeStruct((nblk, cq, nq), F32)
    v3sh = jax.ShapeDtypeStruct((nblk, 1, nq), F32)
    big = pltpu.VMEM((tt, nq), F32)
    states = pl.BlockSpec((tt, nq), lambda q, i: (last - i, q))
    return _call(
        body, grid=(nblk, nch),
        in_specs=[act, act, states, states, sb, sb, wsp, wsp, wsp, wsp, vec, vec,
                  pl.BlockSpec((1, cq), lambda q, i: (0, q))],
        out_specs=[act, wsp, wsp, wsp, wsp, vec3, vec3, dsp],
        out_shape=[jax.ShapeDtypeStruct(u.shape, F32), wsh, wsh, wsh, wsh, v3sh, v3sh,
                   jax.ShapeDtypeStruct((nblk, 1, cq), F32)],
        scratch_shapes=[big, big, pltpu.VMEM((1, nq), F32), pltpu.VMEM((1, nq), F32)],
        args=(u, dy, st_re, st_im, sb_re, sb_im, wbr, wbi, wcr, wci, lb_re, lb_im, dskip), name=name, comm=comm)


def _conv_taps(cur, prev, w, b):
    ext = jnp.concatenate([prev, cur], axis=0)
    x1 = pltpu.roll(ext, 1, 0)[SUBLANES:, :]
    x2 = pltpu.roll(ext, 2, 0)[SUBLANES:, :]
    return b + x2 * w[0:1, :] + x1 * w[1:2, :] + cur * w[2:3, :], x1, x2


def _conv_fwd(uu, cw, cb, *, name):
    n_rows, f2 = uu.shape
    f = f2 // 2
    tc = _tile(f, 1408, LANES)
    tl = _tile(n_rows, 256)
    nfb = f // tc

    def body(g_ref, u_ref, wg_ref, wu_ref, bg_ref, bu_ref, o_ref, pg, pu):
        @pl.when(pl.program_id(1) == 0)
        def _():
            pg[...] = jnp.zeros_like(pg)
            pu[...] = jnp.zeros_like(pu)
        gcur = g_ref[...]
        ucur = u_ref[...]
        cg, _, _ = _conv_taps(gcur, pg[...], wg_ref[...], bg_ref[...])
        cu, _, _ = _conv_taps(ucur, pu[...], wu_ref[...], bu_ref[...])
        o_ref[...] = (cg * _sigmoid(cg) * cu).astype(o_ref.dtype)
        pg[...] = gcur[tl - SUBLANES:, :]
        pu[...] = ucur[tl - SUBLANES:, :]

    return pl.pallas_call(
        body, grid=(nfb, n_rows // tl),
        in_specs=[pl.BlockSpec((tl, tc), lambda j, i: (i, j)), pl.BlockSpec((tl, tc), lambda j, i: (i, j + nfb)),
                  pl.BlockSpec((CONV_TAPS, tc), lambda j, i: (0, j)),
                  pl.BlockSpec((CONV_TAPS, tc), lambda j, i: (0, j + nfb)),
                  pl.BlockSpec((1, tc), lambda j, i: (0, j)), pl.BlockSpec((1, tc), lambda j, i: (0, j + nfb))],
        out_specs=pl.BlockSpec((tl, tc), lambda j, i: (i, j)),
        out_shape=jax.ShapeDtypeStruct((n_rows, f), BF16),
        scratch_shapes=[pltpu.VMEM((SUBLANES, tc), F32), pltpu.VMEM((SUBLANES, tc), F32)],
        compiler_params=_cp(2), name=name)(uu, uu, cw, cw, cb, cb)


def _conv_bwd(uu, dact, cw, cb, *, name):
    n_rows, f2 = uu.shape
    f = f2 // 2
    tc = _tile(f, 1408, LANES)
    tl = _tile(n_rows, 256)
    nfb = f // tc
    nrb = n_rows // tl
    halo_per_tile = tl // SUBLANES

    def body(g_ref, gh_ref, u_ref, uh_ref, da_ref, wg_ref, wu_ref, bg_ref, bu_ref,
             duu_ref, dw_ref, db_ref, nxt_g, nxt_u):
        i = pl.program_id(1)
        rb = nrb - 1 - i

        @pl.when(i == 0)
        def _():
            for ref in (nxt_g, nxt_u, dw_ref, db_ref):
                ref[...] = jnp.zeros_like(ref)
        has_prev = jnp.where(rb > 0, 1.0, 0.0)
        gcur, ucur = g_ref[...], u_ref[...]
        wg, wu = wg_ref[...], wu_ref[...]
        cg, g1, g2 = _conv_taps(gcur, gh_ref[...] * has_prev, wg, bg_ref[...])
        cu, u1, u2 = _conv_taps(ucur, uh_ref[...] * has_prev, wu, bu_ref[...])
        sg = _sigmoid(cg)
        silu = cg * sg
        da = da_ref[...]

        def transpose_conv(plane, d, cur, x1, x2, w, nxt):
            ext = jnp.concatenate([d, nxt[...]], axis=0)
            d1 = pltpu.roll(ext, tl + SUBLANES - 1, 0)[:tl, :]
            d2 = pltpu.roll(ext, tl + SUBLANES - 2, 0)[:tl, :]
            duu_ref[plane] = (w[2:3, :] * d + w[1:2, :] * d1 + w[0:1, :] * d2).astype(duu_ref.dtype)
            nxt[...] = d[0:SUBLANES, :]
            dw_ref[plane] += jnp.concatenate([jnp.sum(d * x2, axis=0, keepdims=True),
                                              jnp.sum(d * x1, axis=0, keepdims=True),
                                              jnp.sum(d * cur, axis=0, keepdims=True)], axis=0)
            db_ref[plane] += jnp.sum(d, axis=0, keepdims=True)

        transpose_conv(0, da * cu * (sg * (1.0 + cg * (1.0 - sg))), gcur, g1, g2, wg, nxt_g)
        transpose_conv(1, da * silu, ucur, u1, u2, wu, nxt_u)

    def halo(j, i):
        return jnp.maximum((nrb - 1 - i) * halo_per_tile - 1, 0)

    return pl.pallas_call(
        body, grid=(nfb, nrb),
        in_specs=[pl.BlockSpec((tl, tc), lambda j, i: (nrb - 1 - i, j)),
                  pl.BlockSpec((SUBLANES, tc), lambda j, i: (halo(j, i), j)),
                  pl.BlockSpec((tl, tc), lambda j, i: (nrb - 1 - i, j + nfb)),
                  pl.BlockSpec((SUBLANES, tc), lambda j, i: (halo(j, i), j + nfb)),
                  pl.BlockSpec((tl, tc), lambda j, i: (nrb - 1 - i, j)),
                  pl.BlockSpec((CONV_TAPS, tc), lambda j, i: (0, j)),
                  pl.BlockSpec((CONV_TAPS, tc), lambda j, i: (0, j + nfb)),
                  pl.BlockSpec((1, tc), lambda j, i: (0, j)),
                  pl.BlockSpec((1, tc), lambda j, i: (0, j + nfb))],
        out_specs=[pl.BlockSpec((2, tl, tc), lambda j, i: (0, nrb - 1 - i, j)),
                   pl.BlockSpec((2, CONV_TAPS, tc), lambda j, i: (0, 0, j)),
                   pl.BlockSpec((2, 1, tc), lambda j, i: (0, 0, j))],
        out_shape=[jax.ShapeDtypeStruct((2, n_rows, f), BF16), jax.ShapeDtypeStruct((2, CONV_TAPS, f), F32),
                   jax.ShapeDtypeStruct((2, 1, f), F32)],
        scratch_shapes=[pltpu.VMEM((SUBLANES, tc), F32), pltpu.VMEM((SUBLANES, tc), F32)],
        compiler_params=_cp(2), name=name)(uu, uu, uu, uu, dact, cw, cw, cb, cb)


def _log_sigmoid(x):
    t = jnp.exp(-jnp.abs(x))
    log1p_t = jnp.where(t < 1e-3, t * (1.0 - t * (0.5 - t * (1.0 / 3.0))), jnp.log(1.0 + t))
    return jnp.minimum(x, 0.0) - log1p_t


def _dlog_sigmoid(x):
    t = jnp.exp(-jnp.abs(x))
    return jnp.where(x >= 0, t, 1.0) / (1.0 + t)


def _tri_dot(tri, x):
    return jnp.dot(tri, x, precision=lax.Precision.HIGHEST, preferred_element_type=F32)


def _cum_fwd(fl, bf, *, name):
    n_rows, width = fl.shape
    tc = _tile(n_rows, 256)

    def body(fl_ref, bf_ref, o_ref, carry):
        @pl.when(pl.program_id(0) == 0)
        def _():
            carry[...] = jnp.zeros_like(carry)
        x = _log_sigmoid(fl_ref[...] + bf_ref[...])
        r = lax.broadcasted_iota(jnp.int32, (tc, tc), 0)
        c = lax.broadcasted_iota(jnp.int32, (tc, tc), 1)
        y = _tri_dot(jnp.where(r >= c, 1.0, 0.0), x) + carry[...]
        o_ref[...] = y
        carry[...] = y[tc - 1:tc, :]

    return pl.pallas_call(
        body, grid=(n_rows // tc,),
        in_specs=[pl.BlockSpec((tc, width), lambda i: (i, 0)), pl.BlockSpec((1, width), lambda i: (0, 0))],
        out_specs=pl.BlockSpec((tc, width), lambda i: (i, 0)),
        out_shape=jax.ShapeDtypeStruct(fl.shape, F32),
        scratch_shapes=[pltpu.VMEM((1, width), F32)], compiler_params=_cp(1), name=name)(fl, bf)


def _cum_bwd(dcum, fl, bf, *, name):
    n_rows, width = fl.shape
    tc = _tile(n_rows, 256)
    last = n_rows // tc - 1

    def body(dc_ref, fl_ref, bf_ref, dfl_ref, dbf_ref, carry):
        @pl.when(pl.program_id(0) == 0)
        def _():
            carry[...] = jnp.zeros_like(carry)
            dbf_ref[...] = jnp.zeros_like(dbf_ref)
        r = lax.broadcasted_iota(jnp.int32, (tc, tc), 0)
        c = lax.broadcasted_iota(jnp.int32, (tc, tc), 1)
        dls = _tri_dot(jnp.where(r <= c, 1.0, 0.0), dc_ref[...]) + carry[...]
        carry[...] = dls[0:1, :]
        dfl = dls * _dlog_sigmoid(fl_ref[...] + bf_ref[...])
        dfl_ref[...] = dfl.astype(dfl_ref.dtype)
        dbf_ref[...] += jnp.sum(dfl, axis=0, keepdims=True)

    return pl.pallas_call(
        body, grid=(n_rows // tc,),
        in_specs=[pl.BlockSpec((tc, width), lambda i: (last - i, 0)),
                  pl.BlockSpec((tc, width), lambda i: (last - i, 0)),
                  pl.BlockSpec((1, width), lambda i: (0, 0))],
        out_specs=[pl.BlockSpec((tc, width), lambda i: (last - i, 0)), pl.BlockSpec((1, width), lambda i: (0, 0))],
        out_shape=[jax.ShapeDtypeStruct(fl.shape, BF16), jax.ShapeDtypeStruct((1, width), F32)],
        scratch_shapes=[pltpu.VMEM((1, width), F32)], compiler_params=_cp(1), name=name)(dcum, fl, bf)


def _head_masks():
    lane = lax.broadcasted_iota(jnp.int32, (1, LANES), 1)
    return (lane < HEAD_DIM, lane >= HEAD_DIM)


def _flash_fwd(q, kv, cum_c, cum_r, *, tq, name, comm=None):
    n_rows, d = q.shape
    nhp = d // LANES
    tk = tq
    nq = n_rows // tq
    rt = _tile(tk, FLASH_ROW_TILE)
    reps = (1, tq // LANES)

    def body(qi_ref, kj_ref, q_ref, k_ref, v_ref, cq_ref, ck_ref, ot_ref, lse_ref, m0, m1, l0, l1, acc,
             s0, s1, p0, p1, b0, b1):
        i = qi_ref[pl.program_id(1)]
        j = kj_ref[pl.program_id(1)]
        ms, ls = (m0, m1), (l0, l1)
        head_rows = lax.broadcasted_iota(jnp.int32, (LANES, 1), 0) < HEAD_DIM

        @pl.when(j == 0)
        def _():
            for h in range(2):
                ms[h][...] = jnp.full_like(ms[h], -jnp.inf)
                ls[h][...] = jnp.zeros_like(ls[h])
            acc[...] = jnp.zeros_like(acc)

        def block(diagonal):
            qv, kk, vv = q_ref[...], k_ref[...], v_ref[...]
            a = acc[...]
            for h, msk in enumerate(_head_masks()):
                st_sc, pt_sc, bias_sc = ((s0, p0, b0), (s1, p1, b1))[h]
                st_sc[...] = _dot(kk, jnp.where(msk, qv, jnp.zeros_like(qv)), 1, 1)
                bias_sc[...] = jnp.broadcast_to(cq_ref[0, h:h + 1, 0:1] - ck_ref[0, :, h:h + 1], (tk, LANES))
                m_old, l_old = ms[h][...], ls[h][...]
                col_max = jnp.full((SUBLANES, tq), -jnp.inf, F32)
                for r in range(tk // rt):
                    rows = slice(r * rt, (r + 1) * rt)
                    s = st_sc[rows, :] + jnp.tile(bias_sc[rows, :], reps)
                    if diagonal:
                        key = r * rt + lax.broadcasted_iota(jnp.int32, (rt, tq), 0)
                        qry = lax.broadcasted_iota(jnp.int32, (rt, tq), 1)
                        s = jnp.where(key <= qry, s, -jnp.inf)
                    st_sc[rows, :] = s
                    for g in range(rt // SUBLANES):
                        col_max = jnp.maximum(col_max, s[g * SUBLANES:(g + 1) * SUBLANES, :])
                m_new = jnp.maximum(m_old, jnp.max(col_max, axis=0, keepdims=True))
                col_sum = jnp.zeros((SUBLANES, tq), F32)
                for r in range(tk // rt):
                    rows = slice(r * rt, (r + 1) * rt)
                    p = jnp.exp(st_sc[rows, :] - m_new)
                    for g in range(rt // SUBLANES):
                        col_sum = col_sum + p[g * SUBLANES:(g + 1) * SUBLANES, :]
                    pt_sc[rows, :] = p.astype(BF16)
                alpha = jnp.exp(m_old - m_new)
                ms[h][...] = m_new
                ls[h][...] = alpha * l_old + jnp.sum(col_sum, axis=0, keepdims=True)
                pv_t = _dot(jnp.where(msk, vv, jnp.zeros_like(vv)), pt_sc[...], 0, 0)
                a = a * jnp.where(head_rows == (h == 0), alpha, 1.0) + pv_t
            acc[...] = a

        pl.when(j < i)(functools.partial(block, False))
        pl.when(j == i)(functools.partial(block, True))

        @pl.when(j == i)
        def _():
            ot_ref[...] = (acc[...] * jnp.where(head_rows, 1.0 / l0[...], 1.0 / l1[...])).T
            lse_ref[0] = jnp.concatenate([m0[...] + jnp.log(l0[...]), m1[...] + jnp.log(l1[...])], axis=0)

    pairs = [(i, j) for i in range(nq) for j in range(i + 1)]
    qi = jnp.asarray([i for i, _ in pairs], jnp.int32)
    kj = jnp.asarray([j for _, j in pairs], jnp.int32)
    stat = pltpu.VMEM((1, tq), F32)
    return _call(
        body, grid=(nhp, len(pairs)), prefetch=(qi, kj),
        in_specs=[pl.BlockSpec((tq, LANES), lambda hp, t, qi, kj: (qi[t], hp)),
                  pl.BlockSpec((tk, LANES), lambda hp, t, qi, kj: (kj[t], hp)),
                  pl.BlockSpec((tk, LANES), lambda hp, t, qi, kj: (kj[t], nhp + hp)),
                  pl.BlockSpec((1, 2, tq), lambda hp, t, qi, kj: (hp, 0, qi[t])),
                  pl.BlockSpec((1, tk, 2), lambda hp, t, qi, kj: (hp, kj[t], 0))],
        out_specs=[pl.BlockSpec((tq, LANES), lambda hp, t, qi, kj: (qi[t], hp)),
                   pl.BlockSpec((1, 2, tq), lambda hp, t, qi, kj: (hp, 0, qi[t]))],
        scratch_shapes=[stat, stat, stat, stat, pltpu.VMEM((LANES, tq), F32), pltpu.VMEM((tk, tq), F32),
                        pltpu.VMEM((tk, tq), F32), pltpu.VMEM((tk, tq), BF16), pltpu.VMEM((tk, tq), BF16),
                        pltpu.VMEM((tk, LANES), F32), pltpu.VMEM((tk, LANES), F32)],
        out_shape=[jax.ShapeDtypeStruct((n_rows, d), F32), jax.ShapeDtypeStruct((nhp, 2, n_rows), F32)],
        args=(q, kv, kv, cum_r, cum_c), name=name, comm=comm)


def _head_delta(do, o, *, name):
    d = o.shape[1]

    def fn(dd, oo):
        prod = dd.astype(BF16).astype(F32) * oo
        r = lax.broadcasted_iota(jnp.int32, (d, LANES), 0)
        c = lax.broadcasted_iota(jnp.int32, (d, LANES), 1)
        return _tri_dot(prod, jnp.where(jnp.right_shift(r, HEAD_DIM.bit_length() - 1) == c, 1.0, 0.0))

    return _rowwise(fn, [do, o], [], [(LANES, F32)], name=name)[0]


def _flash_bwd(q, kv, lse_r, delta_r, do, cum_c, cum_r, *, tq, name, comm=None):
    n_rows, d = q.shape
    nhp = d // LANES
    tk = tq
    nq = n_rows // tq
    rt = _tile(tk, FLASH_ROW_TILE)
    reps = (1, tq // LANES)

    def body(qi_ref, kj_ref, q_ref, k_ref, v_ref, lse_ref, dl_ref, do_ref, cq_ref, ck_ref,
             dq_ref, dk_ref, dv_ref, dck_ref, dcq_ref, s0, dp0, p0, ds0, b0, b1, ck0, ck1):
        s1, dp1, p1, ds1 = s0, dp0, p0, ds0
        i = qi_ref[pl.program_id(1)]
        j = kj_ref[pl.program_id(1)]

        @pl.when(pl.program_id(1) == 0)
        def _():
            dq_ref[...] = jnp.zeros_like(dq_ref)
            dcq_ref[...] = jnp.zeros_like(dcq_ref)

        @pl.when(i == j)
        def _():
            for ref in (dk_ref, dv_ref, ck0, ck1):
                ref[...] = jnp.zeros_like(ref)

        def block(diagonal):
            qv, kk, vv = q_ref[...], k_ref[...], v_ref[...]
            dob = do_ref[...].astype(BF16)
            dq_acc = jnp.zeros((tq, LANES), F32)
            dk_acc = jnp.zeros((tk, LANES), F32)
            dv_acc = jnp.zeros((tk, LANES), F32)
            query_sums = []
            for h, msk in enumerate(_head_masks()):
                st_sc, dpt_sc, pt_sc, dst_sc, bias_sc, key_part = ((s0, dp0, p0, ds0, b0, ck0),
                                                                  (s1, dp1, p1, ds1, b1, ck1))[h]
                qh = jnp.where(msk, qv, jnp.zeros_like(qv))
                kh = jnp.where(msk, kk, jnp.zeros_like(kk))
                doh = jnp.where(msk, dob, jnp.zeros_like(dob))
                st_sc[...] = _dot(kk, qh, 1, 1)
                dpt_sc[...] = _dot(vv, doh, 1, 1)
                bias_sc[...] = jnp.broadcast_to(cq_ref[0, h:h + 1, 0:1] - ck_ref[0, :, h:h + 1], (tk, LANES))
                lse_row = lse_ref[0, h:h + 1, :]
                delta_row = dl_ref[0, h:h + 1, :]
                col_acc = jnp.zeros((SUBLANES, tq), F32)
                parts = []
                for r in range(tk // rt):
                    rows = slice(r * rt, (r + 1) * rt)
                    s = st_sc[rows, :] + jnp.tile(bias_sc[rows, :], reps)
                    if diagonal:
                        key = r * rt + lax.broadcasted_iota(jnp.int32, (rt, tq), 0)
                        qry = lax.broadcasted_iota(jnp.int32, (rt, tq), 1)
                        s = jnp.where(key <= qry, s, -jnp.inf)
                    p = jnp.exp(s - lse_row)
                    ds = p * (dpt_sc[rows, :] - delta_row)
                    for g in range(rt // SUBLANES):
                        col_acc = col_acc + ds[g * SUBLANES:(g + 1) * SUBLANES, :]
                    part = ds[:, 0:LANES]
                    for g in range(1, tq // LANES):
                        part = part + ds[:, g * LANES:(g + 1) * LANES]
                    parts.append(part)
                    pt_sc[rows, :] = p.astype(BF16)
                    dst_sc[rows, :] = ds.astype(BF16)
                key_part[...] += jnp.concatenate(parts, axis=0)
                query_sums.append(jnp.sum(col_acc, axis=0, keepdims=True))
                dv_acc = dv_acc + _dot(pt_sc[...], doh, 1, 0)
                dsb = dst_sc[...]
                dk_acc = dk_acc + _dot(dsb, qh, 1, 0)
                dq_acc = dq_acc + _dot(dsb, kh, 0, 0)
            off = pl.multiple_of(i * tq, tq)
            dq_ref[pl.ds(off, tq), :] += dq_acc
            dk_ref[...] += dk_acc
            dv_ref[...] += dv_acc
            dcq_ref[0, i] += jnp.concatenate(query_sums, axis=0)

        pl.when(i > j)(functools.partial(block, False))
        pl.when(i == j)(functools.partial(block, True))

        @pl.when(i == nq - 1)
        def _():
            two = lax.broadcasted_iota(jnp.int32, (tk, 2), 1)
            dck_ref[0] = jnp.where(two == 0, -jnp.sum(ck0[...], axis=1, keepdims=True),
                                   -jnp.sum(ck1[...], axis=1, keepdims=True))

    pairs = [(i, j) for j in range(nq) for i in range(j, nq)]
    qi = jnp.asarray([i for i, _ in pairs], jnp.int32)
    kj = jnp.asarray([j for _, j in pairs], jnp.int32)
    score = pltpu.VMEM((tk, tq), F32)
    score16 = pltpu.VMEM((tk, tq), BF16)
    keystat = pltpu.VMEM((tk, LANES), F32)
    return _call(
        body, grid=(nhp, len(pairs)), prefetch=(qi, kj),
        in_specs=[pl.BlockSpec((tq, LANES), lambda hp, t, qi, kj: (qi[t], hp)),
                  pl.BlockSpec((tk, LANES), lambda hp, t, qi, kj: (kj[t], hp)),
                  pl.BlockSpec((tk, LANES), lambda hp, t, qi, kj: (kj[t], nhp + hp)),
                  pl.BlockSpec((1, 2, tq), lambda hp, t, qi, kj: (hp, 0, qi[t])),
                  pl.BlockSpec((1, 2, tq), lambda hp, t, qi, kj: (hp, 0, qi[t])),
                  pl.BlockSpec((tq, LANES), lambda hp, t, qi, kj: (qi[t], hp)),
                  pl.BlockSpec((1, 2, tq), lambda hp, t, qi, kj: (hp, 0, qi[t])),
                  pl.BlockSpec((1, tk, 2), lambda hp, t, qi, kj: (hp, kj[t], 0))],
        out_specs=[pl.BlockSpec((n_rows, LANES), lambda hp, t, qi, kj: (0, hp)),
                   pl.BlockSpec((tk, LANES), lambda hp, t, qi, kj: (kj[t], hp)),
                   pl.BlockSpec((tk, LANES), lambda hp, t, qi, kj: (kj[t], hp)),
                   pl.BlockSpec((1, tk, 2), lambda hp, t, qi, kj: (hp, kj[t], 0)),
                   pl.BlockSpec((1, nq, 2, tq), lambda hp, t, qi, kj: (hp, 0, 0, 0))],
        scratch_shapes=[score, score, score16, score16, keystat, keystat, keystat, keystat],
        out_shape=[jax.ShapeDtypeStruct((n_rows, d), F32), jax.ShapeDtypeStruct((n_rows, d), F32),
                   jax.ShapeDtypeStruct((n_rows, d), F32), jax.ShapeDtypeStruct((nhp, n_rows, 2), F32),
                   jax.ShapeDtypeStruct((nhp, nq, 2, tq), F32)],
        args=(q, kv, kv, lse_r, delta_r, do, cum_r, cum_c), name=name, comm=comm)


def _s5_tables(w, layer):
    g, p = w["lam_re"].shape[1:]
    h = w["ssm_b_re"].shape[3]
    n = g * p
    lr = w["lam_re"][layer].reshape(1, n)
    li = w["lam_im"][layer].reshape(1, n)
    ldt = jnp.broadcast_to(w["log_dt"][layer][:, None], (g, p)).reshape(1, n)
    br = w["ssm_b_re"][layer].transpose(2, 0, 1).reshape(h, n)
    bi = w["ssm_b_im"][layer].transpose(2, 0, 1).reshape(h, n)
    cr = w["ssm_c_re"][layer].transpose(1, 0, 2).reshape(h, n)
    ci = w["ssm_c_im"][layer].transpose(1, 0, 2).reshape(h, n)
    return (lr, li, ldt, br, bi, cr, ci), (g, p, h)


def _local_step(x, tgt, w, net=None, *, attn_tile=1024):
    n_rows, d = x.shape
    n_layers = w["g_mix"].shape[0]
    n_s5 = w["lam_re"].shape[0]
    nh = w["b_f"].shape[0]
    nhp = nh // 2
    assert d == nh * HEAD_DIM
    tq = _tile(n_rows, attn_tile)
    g = {}
    saved = [dict() for _ in range(n_layers)]
    big = {}
    pending = {}

    def wt(name, layer):
        return w[name][layer]

    def carry_gather(group, run):
        if net is None or not net.has_group(group):
            return run(None)
        outs, got = run(net.gather_comm(group))
        net.store_gathered(group, got, w)
        return outs

    def carry_reduce(tag, run):
        keys = list(pending)
        grads = [pending[k][0] for k in keys]
        if net is None or not pending:
            big.update(zip(keys, grads))
            pending.clear()
            return run(None)
        parts = net.reduce_prepare(grads, [pending[k][1] for k in keys], tag)
        outs, landed = run(_chip_exchange_comm(parts))
        big.update(zip(keys, net.reduce_finish(parts, landed, grads, tag)))
        pending.clear()
        return outs

    def by_row_shard(m):
        return m.reshape(N_CHIPS, m.shape[0] // N_CHIPS, m.shape[1])

    def grad_and_copy(x_, dy_, name, *, col_slots=None, scale=None):
        g32, g16 = _mm_tn(x_, dy_, col_slots or 1, scale=scale, wire=True, name=name)
        return (g32, g16) if col_slots else (by_row_shard(g32[0]), by_row_shard(g16[0]))

    h = x
    nxt = _rowwise(lambda a, gg: _rms(a, gg), [x], [_row2(w["g_mix"][0])], [(d, F32)], name="rms_first")[0]
    kvb = fl = cum = cq3 = ck3 = hnkv = None
    bf_pad = jnp.zeros((1, LANES), F32).at[0, :nh].set(w["b_f"])
    for l in range(n_layers):
        sv = saved[l]
        sv["h"] = h
        g_ffn = _row2(w["g_ffn"][l])
        if l < n_s5:
            tabs, (_, p, _) = _s5_tables(w, l)
            prep = _s5_prep(*tabs, p, name=f"s5_prep{l}")
            dskip = w["ssm_d"][l].reshape(1, d)
            y, z, st_re, st_im, sb_re, sb_im = carry_gather(
                f"stage{l}", lambda comm, u=nxt, pr=prep, ds=dskip: _s5_fwd(u, pr, ds, name=f"s5_fwd{l}", comm=comm))
            zz = _mm_cols(z, *wt("w_glu", l), wc=0, name=f"glu_mm{l}")
            h1, hn2 = _rowwise(lambda hh, zq, gg: ((lambda t: (t, _rms(t, gg)))(hh + _glu(zq))),
                               [h, zz], [g_ffn], [(d, F32), (d, BF16)], name=f"mix_out{l}")
            sv.update(u=nxt, prep=prep, tabs=tabs, p=p, dskip=dskip, st_re=st_re, st_im=st_im, sb_re=sb_re,
                      sb_im=sb_im, y=y, z=z, zz=zz)
        else:
            j = l - n_s5
            qs = _mm_cols(nxt, *wt("w_q", j), wc=0, out_dtype=BF16, scale=HEAD_DIM ** -0.5, name=f"q_mm{j}")
            o, lse = carry_gather(
                f"stage{l}", lambda comm, q_=qs: _flash_fwd(q_, kvb, cq3, ck3, tq=tq, name=f"flash_fwd{j}", comm=comm))
            h1, hn2 = _mm_cols(o, *wt("w_o", j), wc=0, name=f"o_mm{j}",
                               epilogue=(lambda aa, hh, gg: ((lambda t: (t, _rms(t, gg)))(hh + aa)), [h], [g_ffn],
                                         [(d, F32), (d, BF16)], ()))
            sv.update(hn=nxt, qs=qs, o=o, lse=lse)
        uu = _mm_cols(hn2, *wt("w_in", l), wc=0, name=f"ffn_in{l}")
        cw, cb = w["conv_w"][l], _row2(w["conv_b"][l])
        act = _conv_fwd(uu, cw, cb, name=f"conv_fwd{l}")
        sv.update(h1=h1, hn2=hn2, uu=uu, act=act, cw=cw, cb=cb)

        def ffn_out(fn, rows, consts, outs, accs=()):
            return _mm_cols(act, *wt("w_out", l), wc=0, name=f"ffn_out{l}", epilogue=(fn, rows, consts, outs, accs))

        if l == n_layers - 1:
            def loss_fn(ff, hh, tt, gg):
                yv, vjp = jax.vjp(_rms, hh + ff, gg)
                err = yv - tt
                part = 0.5 * jnp.sum(jnp.mean(err * err, axis=-1, keepdims=True), axis=0, keepdims=True)
                dh, dg = vjp(err * (1.0 / d))
                return dh, dh, jnp.broadcast_to(part, (1, LANES)), dg
            dcur, dcur16, loss_row, dgf = ffn_out(loss_fn, [h1, tgt], [_row2(w["g_final"])],
                                                  [(d, F32), (d, BF16)], [(1, LANES), (1, d)])
            loss = loss_row[0, 0]
            g["g_final"] = dgf[0]
        elif l + 1 < n_s5:
            h, nxt = ffn_out(lambda ff, hh, gg: ((lambda t: (t, _rms(t, gg)))(hh + ff)), [h1],
                             [_row2(w["g_mix"][l + 1])], [(d, F32), (d, F32)])
        elif l + 1 == n_s5:
            h, nxt, hnkv = ffn_out(lambda ff, hh, g1, g2: ((lambda t: (t, _rms(t, g1), _rms(t, g2)))(hh + ff)), [h1],
                                   [_row2(w["g_mix"][l + 1]), _row2(w["g_kv"])],
                                   [(d, F32), (d, BF16), (d, BF16)])
            kvb = _mm_cols(hnkv, *wt("w_kv", 0), wc=0, out_dtype=BF16, name="kv_mm")
            fl = _mm_cols(hnkv, *wt("w_f", 0), wc=0, name="f_mm")
            cum = _cum_fwd(fl, bf_pad, name="cum_fwd")
            cq3 = cum[:, :nh].reshape(n_rows, nhp, 2).transpose(1, 0, 2)
            ck3 = cum[:, :nh].T.reshape(nhp, 2, n_rows)
        else:
            h, nxt = ffn_out(lambda ff, hh, gg: ((lambda t: (t, _rms(t, gg)))(hh + ff)), [h1],
                             [_row2(w["g_mix"][l + 1])], [(d, F32), (d, BF16)])

    per_layer = {k: [None] * n_layers for k in ("g_mix", "g_ffn", "conv_w", "conv_b")}
    per_s5 = {k: [None] * n_s5 for k in ("lam_re", "lam_im", "log_dt", "ssm_b_re", "ssm_b_im", "ssm_c_re",
                                         "ssm_c_im", "ssm_d")}
    dk_parts, dv_parts, dck_parts = [], [], []
    for l in reversed(range(n_layers)):
        sv = saved[l]
        dact = _mm_cols(dcur16, *wt("w_out", l), wc=1, name=f"ffn_out_dx{l}")
        pending["w_ffn_out", l] = grad_and_copy(sv["act"], dcur16, f"ffn_out_dw{l}")
        duu, dcw, dcb = _conv_bwd(sv["uu"], dact, sv["cw"], sv["cb"], name=f"conv_bwd{l}")
        per_layer["conv_w"][l] = jnp.concatenate([dcw[0], dcw[1]], axis=-1)
        per_layer["conv_b"][l] = jnp.concatenate([dcb[0, 0], dcb[1, 0]])
        node = _node_bwd_fn([1])
        d1, d1_16, dg = _mm_acc(duu, *wt("w_in", l), wc=1, name=f"ffn_in_dx{l}",
                                epilogue=(lambda dhn2, dd, hh, gg: node(dd, hh, dhn2, gg), [dcur, sv["h1"]],
                                          [_row2(w["g_ffn"][l])], [(d, F32), (d, BF16)], [(1, d)]))
        pending["w_ffn_in", l] = grad_and_copy(sv["hn2"], duu, f"ffn_in_dw{l}", col_slots=wt("w_in", l)[0].shape[0])
        per_layer["g_ffn"][l] = dg[0]
        if l < n_s5:
            def glu_bwd(zq, dd):
                _, vjp = jax.vjp(_glu, zq)
                return vjp(dd)[0]
            dzz = _rowwise(glu_bwd, [sv["zz"], d1], [], [(2 * d, BF16)], name=f"glu_bwd{l}")[0]
            def gelu_bwd(dd, yy):
                _, vjp = jax.vjp(_gelu, yy)
                return (vjp(dd)[0],)
            dy = _mm_acc(dzz, *wt("w_glu", l), wc=1, name=f"glu_dx{l}",
                         epilogue=(gelu_bwd, [sv["y"]], [], [(d, F32)], ()))[0]
            pending["w_glu", l] = grad_and_copy(sv["z"], dzz, f"glu_dw{l}", col_slots=wt("w_glu", l)[0].shape[0])
            du, dwbr, dwbi, dwcr, dwci, dlbr, dlbi, dd = carry_reduce(
                f"stage{l}", lambda comm, dy_=dy: _s5_bwd(sv["u"], dy_, sv["st_re"], sv["st_im"], sv["sb_re"],
                                                          sv["sb_im"], sv["prep"], sv["dskip"], name=f"s5_bwd{l}",
                                                          comm=comm))
            dlr, dli, dldt, dbr, dbi, dcr, dci = _s5_prep_bwd(*sv["tabs"], sv["p"], dlbr, dlbi, dwbr, dwbi, dwcr,
                                                              dwci, name=f"s5_prep_bwd{l}")
            gg, p = w["lam_re"].shape[1:]
            hh = w["ssm_b_re"].shape[3]
            per_s5["lam_re"][l] = dlr.reshape(gg, p)
            per_s5["lam_im"][l] = dli.reshape(gg, p)
            per_s5["log_dt"][l] = dldt.reshape(gg, p).sum(axis=1)
            per_s5["ssm_b_re"][l] = dbr.reshape(hh, gg, p).transpose(1, 2, 0)
            per_s5["ssm_b_im"][l] = dbi.reshape(hh, gg, p).transpose(1, 2, 0)
            per_s5["ssm_c_re"][l] = dcr.reshape(hh, gg, p).transpose(1, 0, 2)
            per_s5["ssm_c_im"][l] = dci.reshape(hh, gg, p).transpose(1, 0, 2)
            per_s5["ssm_d"][l] = dd.reshape(d)
            branches = [(w["g_mix"][l], [du])]
        else:
            j = l - n_s5
            do = _mm_cols(d1_16, *wt("w_o", j), wc=1, name=f"o_dx{j}")
            pending["w_o", j] = grad_and_copy(sv["o"], d1_16, f"o_dw{j}")
            delta_r = _head_delta(do, sv["o"], name=f"head_delta{j}")[:, :nh].T.reshape(nhp, 2, n_rows)
            dq, dk, dv, dck, dcq = carry_reduce(
                f"stage{l}", lambda comm, do_=do: _flash_bwd(sv["qs"], kvb, sv["lse"], delta_r, do_, cq3, ck3, tq=tq,
                                                          name=f"flash_bwd{j}", comm=comm))
            dk_parts.append(dk)
            dv_parts.append(dv)
            dck_parts.append(dck.transpose(1, 0, 2).reshape(n_rows, nh)
                             + dcq.transpose(0, 2, 1, 3).reshape(nh, n_rows).T)
            scale = HEAD_DIM ** -0.5
            pending["w_q", j] = grad_and_copy(sv["hn"], dq, f"q_dw{j}", scale=scale)
            branches = []
            if j == 0:
                def kv_sum(*parts):
                    half = len(parts) // 2
                    return jnp.concatenate([sum(parts[:half][1:], parts[0]),
                                            sum(parts[half:][1:], parts[half])], axis=1)
                dkv = _rowwise(kv_sum, dk_parts + dv_parts, [], [(2 * d, BF16)], name="dkv_sum")[0]
                dck_tot = dck_parts[0]
                for extra in dck_parts[1:]:
                    dck_tot = dck_tot + extra
                dcum = jnp.zeros((n_rows, LANES), F32).at[:, :nh].set(dck_tot)
                dfl, dbf = _cum_bwd(dcum, fl, bf_pad, name="cum_bwd")
                g["b_f"] = dbf[0, :nh]
                dhkv_a = _mm_cols(dkv, *wt("w_kv", 0), wc=1, name="kv_dx")
                dhkv_b = _mm_cols(dfl, *wt("w_f", 0), wc=1, name="f_dx")
                d_kvf = jnp.concatenate([_mm_tn(hnkv, dkv, 1, name="kv_dw")[0],
                                         _mm_tn(hnkv, dfl, 1, name="f_dw")[0][:, :nh]], axis=1)
                d_kvf = d_kvf.reshape(d, N_CHIPS, -1).transpose(1, 0, 2)
                pending["w_kvf", 0] = (d_kvf, d_kvf.astype(BF16))
                branches.append((w["g_kv"], [dhkv_a, dhkv_b]))
            node = _node_bwd_fn([1] + [len(dys) for _, dys in branches])
            gains = [_row2(w["g_mix"][l])] + [_row2(gn) for gn, _ in branches]
            res = _mm_cols(dq, *wt("w_q", j), wc=1, scale=scale, name=f"q_dx{j}",
                           epilogue=(lambda dhn, dd, hh, *rest: node(dd, hh, dhn, *rest),
                                     [d1, sv["h"], *[dy for _, dys in branches for dy in dys]], gains,
                                     [(d, F32), (d, BF16)], [(1, d)] * len(gains)))
            dcur, dcur16, dgs = res[0], res[1], [r[0] for r in res[2:]]
        if l < n_s5:
            dcur, dcur16, dgs = _node_bwd(d1, sv["h"], branches, name=f"mix_norm_bwd{l}")
        per_layer["g_mix"][l] = dgs[0]
        if len(dgs) > 1:
            g["g_kv"] = dgs[1]

    if pending:
        grads = [g32 for g32, _ in pending.values()]
        big.update(zip(list(pending), grads if net is None else net.reduce_blocking(grads, "tail")))
    for k, v in (*per_layer.items(), *per_s5.items()):
        g[k] = jnp.stack(v)
    g["big"] = big
    return loss, dcur, g


def _position():
    x, y, c = lax.axis_index("x"), lax.axis_index("y"), lax.axis_index("c")
    chips = [(1 - x, y), (x, 1 - y), (1 - x, 1 - y)]
    return x, y, c, chips


def _all_gather_comm(shards):
    n = len(shards)

    def descriptors(ins, outs, sems):
        send_sems, recv_sems = sems
        x, y, c, chips = _position()
        my_slot = 2 * x + y
        sibling = (x, y, 1 - c)

        def rows(t, half):
            hr = ins[t].shape[0] // 2
            return pl.ds(half * hr, hr)

        def remote(k, t, src, dst, to):
            return pltpu.make_async_remote_copy(src_ref=src, dst_ref=dst, send_sem=send_sems.at[k, t],
                                                recv_sem=recv_sems.at[k, t], device_id=to, device_id_type=MESH)

        own = [remote(6, t, ins[t], outs[t].at[my_slot], sibling) for t in range(n)]
        ici = [remote(j, t, ins[t].at[rows(t, c)], outs[t].at[my_slot, rows(t, c)], (*chip, c))
               for j, chip in enumerate(chips) for t in range(n)]
        slots = [2 * chip[0] + chip[1] for chip in chips]
        fwd = [[remote(3 + j, t, outs[t].at[slots[j], rows(t, c)], outs[t].at[slots[j], rows(t, c)], sibling)
                for t in range(n)] for j in range(len(chips))]
        landed = [[remote(j, t, outs[t].at[slots[j], rows(t, c)], outs[t].at[slots[j], rows(t, c)], (*chips[j], c))
                   for t in range(n)] for j in range(len(chips))]
        from_sibling = [remote(3 + j, t, outs[t].at[slots[j], rows(t, 1 - c)], outs[t].at[slots[j], rows(t, 1 - c)],
                               sibling) for j in range(len(chips)) for t in range(n)]
        return own, ici, fwd, landed, from_sibling

    def start(ins, outs, sems):
        own, ici, _, _, _ = descriptors(ins, outs, sems)
        for cp in own + ici:
            cp.start()

    def finish(ins, outs, sems):
        own, ici, fwd, landed, from_sibling = descriptors(ins, outs, sems)
        for j in range(len(fwd)):
            for cp in landed[j]:
                cp.wait_recv()
            for cp in fwd[j]:
                cp.start()
        for cp in from_sibling + own:
            cp.wait_recv()
        for cp in own + ici + [cp for group in fwd for cp in group]:
            cp.wait_send()

    return _Comm(list(shards), [jax.ShapeDtypeStruct((N_CHIPS,) + a.shape, a.dtype) for a in shards],
                 [pltpu.SemaphoreType.DMA((7, n)), pltpu.SemaphoreType.DMA((7, n))], start, finish)


def _all_gather(shards, *, name):
    return _run_comm(_all_gather_comm(shards), name=name)


def _pair_exchange(grads, *, name):
    n = len(grads)

    def body(*refs):
        ins, outs = refs[:n], refs[n:2 * n]
        send_sems, recv_sems = refs[2 * n:]
        x, y, c, _ = _position()
        copies = [pltpu.make_async_remote_copy(src_ref=ins[t].at[:, 1 - c], dst_ref=outs[t],
                                               send_sem=send_sems.at[t], recv_sem=recv_sems.at[t],
                                               device_id=(x, y, 1 - c), device_id_type=MESH) for t in range(n)]
        for cp in copies:
            cp.start()
        for cp in copies:
            cp.wait()

    return pl.pallas_call(
        body, in_specs=_any_specs(n), out_specs=_any_specs(n),
        out_shape=[jax.ShapeDtypeStruct((a.shape[0],) + a.shape[2:], a.dtype) for a in grads],
        scratch_shapes=[pltpu.SemaphoreType.DMA((n,)), pltpu.SemaphoreType.DMA((n,))], name=name)(*grads)


def _chip_exchange_comm(parts):
    n = len(parts)

    def copies(ins, outs, sems):
        send_sems, recv_sems = sems
        _, _, c, chips = _position()
        return [pltpu.make_async_remote_copy(src_ref=ins[t].at[2 * chip[0] + chip[1]], dst_ref=outs[t].at[j],
                                             send_sem=send_sems.at[j, t], recv_sem=recv_sems.at[j, t],
                                             device_id=(*chip, c), device_id_type=MESH)
                for j, chip in enumerate(chips) for t in range(n)]

    def start(ins, outs, sems):
        for cp in copies(ins, outs, sems):
            cp.start()

    def finish(ins, outs, sems):
        for cp in copies(ins, outs, sems):
            cp.wait()

    return _Comm(list(parts), [jax.ShapeDtypeStruct((N_CHIPS - 1,) + a.shape[1:], a.dtype) for a in parts],
                 [pltpu.SemaphoreType.DMA((N_CHIPS - 1, n)), pltpu.SemaphoreType.DMA((N_CHIPS - 1, n))],
                 start, finish)


def _pair_share(both, *, name):
    n = len(both)

    def body(*refs):
        ins, outs = refs[:n], refs[n:2 * n]
        send_sems, recv_sems = refs[2 * n:]
        x, y, c, _ = _position()
        for t in range(n):
            pltpu.make_async_remote_copy(src_ref=ins[t].at[c], dst_ref=outs[t].at[c], send_sem=send_sems.at[t],
                                         recv_sem=recv_sems.at[t], device_id=(x, y, 1 - c),
                                         device_id_type=MESH).start()
        for t in range(n):
            pltpu.make_async_remote_copy(src_ref=ins[t].at[c], dst_ref=outs[t].at[1 - c], send_sem=send_sems.at[t],
                                         recv_sem=recv_sems.at[t], device_id=(x, y, 1 - c),
                                         device_id_type=MESH).wait()

    return pl.pallas_call(
        body, in_specs=_any_specs(n), out_specs=_any_specs(n),
        out_shape=[jax.ShapeDtypeStruct(a.shape, a.dtype) for a in both],
        input_output_aliases={t: t for t in range(n)},
        scratch_shapes=[pltpu.SemaphoreType.DMA((n,)), pltpu.SemaphoreType.DMA((n,))], name=name)(*both)


def _sum_pair(grad, landed, c, wire_dtype, *, name):
    slots, _, m, n = grad.shape
    tm = _tile(m, 256, 2 * SUBLANES)

    def body(c_ref, g_ref, l_ref, o_ref):
        o_ref[...] = (g_ref[0] + l_ref[...]).astype(wire_dtype)

    return pl.pallas_call(
        body,
        grid_spec=pltpu.PrefetchScalarGridSpec(
            num_scalar_prefetch=1, grid=(slots, m // tm),
            in_specs=[pl.BlockSpec((1, 1, tm, n), lambda s, i, c_ref: (s, c_ref[0], i, 0)),
                      pl.BlockSpec((1, tm, n), lambda s, i, c_ref: (s, i, 0))],
            out_specs=pl.BlockSpec((1, tm, n), lambda s, i, c_ref: (s, i, 0))),
        out_shape=jax.ShapeDtypeStruct((slots, m, n), wire_dtype), compiler_params=_cp(2), name=name)(
            c, grad, landed)


def _sum_chips(part, landed, slot_c, *, name):
    _, m, n = part.shape
    tm = _tile(m, 256, 2 * SUBLANES)

    def body(s_ref, p_ref, l_ref, o_ref):
        acc = p_ref[0].astype(F32)
        for j in range(N_CHIPS - 1):
            acc = acc + l_ref[j].astype(F32)
        o_ref[0] = acc

    return pl.pallas_call(
        body,
        grid_spec=pltpu.PrefetchScalarGridSpec(
            num_scalar_prefetch=1, grid=(m // tm,),
            in_specs=[pl.BlockSpec((1, tm, n), lambda i, s_ref: (s_ref[0], i, 0)),
                      pl.BlockSpec((N_CHIPS - 1, tm, n), lambda i, s_ref: (0, i, 0))],
            out_specs=pl.BlockSpec((1, tm, n), lambda i, s_ref: (s_ref[1], i, 0))),
        out_shape=jax.ShapeDtypeStruct((N_CORES, m, n), F32), compiler_params=_cp(1), name=name)(
            slot_c, part, landed)


def _reduce_prepare(grads, wire_dtypes, tag, copies=None):
    c = lax.axis_index("c").reshape(1).astype(jnp.int32)

    def halves(a):
        lead, last = a.shape[1], a.shape[-1]
        mid = 1
        for s in a.shape[2:-1]:
            mid *= s
        return a.reshape(N_CHIPS, N_CORES, (lead // N_CORES) * mid, last)

    views = [halves(a) for a in grads]
    landed = _pair_exchange(views if copies is None else [halves(a) for a in copies],
                            name=f"rs_pair_exchange_{tag}")
    return [_sum_pair(v, l, c, wire_dtypes[t], name=f"rs_pair_sum_{tag}_{t}")
            for t, (v, l) in enumerate(zip(views, landed))]


def _reduce_finish(parts, landed, grads, tag):
    slot_c = jnp.stack([2 * lax.axis_index("x") + lax.axis_index("y"), lax.axis_index("c")]).astype(jnp.int32)
    both = [_sum_chips(p, l, slot_c, name=f"rs_chip_sum_{tag}_{t}") for t, (p, l) in enumerate(zip(parts, landed))]
    full = _pair_share(both, name=f"rs_pair_share_{tag}")
    return [f.reshape(a.shape[1:]) for f, a in zip(full, grads)]


def _reduce_scatter(grads, wire_dtypes, tag):
    parts = _reduce_prepare(grads, wire_dtypes, tag)
    landed = _run_comm(_chip_exchange_comm(parts), name=f"rs_chip_exchange_{tag}")
    return _reduce_finish(parts, landed, grads, tag)


class _Net:
    def __init__(self, groups, d, nh):
        self.groups, self.d, self.nh = groups, d, nh

    def has_group(self, group):
        return bool(self.groups.get(group))

    def gather_comm(self, group):
        return _all_gather_comm([shard for _, _, shard in self.groups[group]])

    def store_gathered(self, group, got, w):
        d, nh = self.d, self.nh
        for (name, layer, _), full in zip(self.groups[group], got):
            if name == "w_kvf":
                mat = full.transpose(1, 0, 2).reshape(d, -1)
                w["w_kv"][0] = (mat[:, :2 * d][None, None], 0)
                w["w_f"][0] = (jnp.zeros((d, LANES), BF16).at[:, :nh].set(mat[:, 2 * d:])[None, None], 0)
            elif name in ("w_in", "w_glu"):
                w[name][layer] = (full[:, None], 0)
            else:
                w[name][layer] = (full.reshape(1, 1, -1, full.shape[-1]), 0)

    def reduce_prepare(self, grads, copies, tag):
        return _reduce_prepare(grads, [BF16] * len(grads), tag, copies)

    def reduce_finish(self, parts, landed, grads, tag):
        return _reduce_finish(parts, landed, grads, tag)

    def reduce_blocking(self, grads, tag):
        return _reduce_scatter(grads, [BF16] * len(grads), tag)


def _adamw(w, g, m, v, *, name):
    def fn(ww, gg, mm, vv):
        mm = ADAM_B1 * mm + (1.0 - ADAM_B1) * gg
        vv = ADAM_B2 * vv + (1.0 - ADAM_B2) * (gg * gg)
        m_hat = mm / (1.0 - ADAM_B1 ** ADAM_STEP)
        v_hat = vv / (1.0 - ADAM_B2 ** ADAM_STEP)
        delta = -ADAM_LR * (m_hat / (jnp.sqrt(v_hat) + ADAM_EPS) + ADAM_WD * ww)
        return delta, mm, vv

    shape = w.shape
    two_d = [a.reshape(-1, shape[-1]) for a in (w, g, m, v)]
    outs = _rowwise(fn, two_d, [], [(shape[-1], F32)] * 3, name=name)
    return [o.reshape(shape) for o in outs]


def _to_bf16(a, *, name):
    two_d = a.reshape(-1, a.shape[-1])
    return _rowwise(lambda t: t, [two_d], [], [(a.shape[-1], BF16)], name=name)[0].reshape(a.shape)


def _pack(arrays, rows_multiple):
    flat = jnp.concatenate([a.reshape(-1) for a in arrays])
    rows = -(-flat.shape[0] // LANES)
    rows = -(-rows // rows_multiple) * rows_multiple
    return jnp.pad(flat, (0, rows * LANES - flat.shape[0])).reshape(rows, LANES)


def _unpack(packed, like):
    flat = packed.reshape(-1)
    out, pos = [], 0
    for a in like:
        out.append(flat[pos:pos + a.size].reshape(a.shape))
        pos += a.size
    return out


_PARAMS = ("g_mix", "g_ffn", "lam_re", "lam_im", "log_dt", "ssm_b_re", "ssm_b_im", "ssm_c_re", "ssm_c_im", "ssm_d",
           "w_glu", "g_kv", "w_kvf", "b_f", "w_q", "w_o", "w_ffn_in", "ffn_conv_w", "ffn_conv_b", "w_ffn_out",
           "g_final")
_BIG = ("w_glu", "w_kvf", "w_q", "w_o", "w_ffn_in", "w_ffn_out")
_SMALL_SHARDED = ("ssm_d", "ffn_conv_w")


def kernel(x, g_mix, g_ffn, lam_re, lam_im, log_dt, ssm_b_re, ssm_b_im, ssm_c_re, ssm_c_im, ssm_d, w_glu, g_kv, w_kvf, b_f, w_q, w_o, w_ffn_in, ffn_conv_w, ffn_conv_b, w_ffn_out, g_final, loss_target, m_g_mix, m_g_ffn, m_lam_re, m_lam_im, m_log_dt, m_ssm_b_re, m_ssm_b_im, m_ssm_c_re, m_ssm_c_im, m_ssm_d, m_w_glu, m_g_kv, m_w_kvf, m_b_f, m_w_q, m_w_o, m_w_ffn_in, m_ffn_conv_w, m_ffn_conv_b, m_w_ffn_out, m_g_final, v_g_mix, v_g_ffn, v_lam_re, v_lam_im, v_log_dt, v_ssm_b_re, v_ssm_b_im, v_ssm_c_re, v_ssm_c_im, v_ssm_d, v_w_glu, v_g_kv, v_w_kvf, v_b_f, v_w_q, v_w_o, v_w_ffn_in, v_ffn_conv_w, v_ffn_conv_b, v_w_ffn_out, v_g_final):
    p = dict(g_mix=g_mix, g_ffn=g_ffn, lam_re=lam_re, lam_im=lam_im, log_dt=log_dt, ssm_b_re=ssm_b_re,
             ssm_b_im=ssm_b_im, ssm_c_re=ssm_c_re, ssm_c_im=ssm_c_im, ssm_d=ssm_d, w_glu=w_glu, g_kv=g_kv,
             w_kvf=w_kvf, b_f=b_f, w_q=w_q, w_o=w_o, w_ffn_in=w_ffn_in, ffn_conv_w=ffn_conv_w,
             ffn_conv_b=ffn_conv_b, w_ffn_out=w_ffn_out, g_final=g_final)
    mom1 = dict(zip(_PARAMS, (m_g_mix, m_g_ffn, m_lam_re, m_lam_im, m_log_dt, m_ssm_b_re, m_ssm_b_im, m_ssm_c_re,
                              m_ssm_c_im, m_ssm_d, m_w_glu, m_g_kv, m_w_kvf, m_b_f, m_w_q, m_w_o, m_w_ffn_in,
                              m_ffn_conv_w, m_ffn_conv_b, m_w_ffn_out, m_g_final)))
    mom2 = dict(zip(_PARAMS, (v_g_mix, v_g_ffn, v_lam_re, v_lam_im, v_log_dt, v_ssm_b_re, v_ssm_b_im, v_ssm_c_re,
                              v_ssm_c_im, v_ssm_d, v_w_glu, v_g_kv, v_w_kvf, v_b_f, v_w_q, v_w_o, v_w_ffn_in,
                              v_ffn_conv_w, v_ffn_conv_b, v_w_ffn_out, v_g_final)))
    d = x.shape[-1]
    nh = b_f.shape[0]
    slot = 2 * lax.axis_index("x") + lax.axis_index("y")

    wb = {k: _to_bf16(p[k], name=f"to_bf16_{k}") for k in _BIG}
    gd, gcw = _all_gather([ssm_d, ffn_conv_w], name="first_all_gather")
    n_lay, n_s5 = w_ffn_in.shape[0], lam_re.shape[0]
    n_fox = n_lay - n_s5
    groups = {f"stage{l}": [("w_in", l, wb["w_ffn_in"][l]), ("w_out", l, wb["w_ffn_out"][l])] for l in range(n_lay)}
    for l in range(n_s5):
        groups[f"stage{l}"].append(("w_glu", l, wb["w_glu"][l]))
    groups[f"stage{n_s5 - 1}"] += [("w_kvf", 0, wb["w_kvf"])] + [(k, j, wb[k][j]) for k in ("w_q", "w_o")
                                                                for j in range(n_fox)]
    w = dict(p)
    w.update(w_glu=[None] * n_s5, w_in=[None] * n_lay, w_out=[None] * n_lay, w_q=[None] * n_fox,
             w_o=[None] * n_fox, w_kv=[None], w_f=[None],
             conv_w=gcw.transpose(1, 2, 0, 3).reshape(n_lay, CONV_TAPS, -1), conv_b=ffn_conv_b,
             ssm_d=gd.transpose(1, 0, 2).reshape(gd.shape[1], d))

    loss_part, grad_x, g = _local_step(x[0], loss_target[0], w, _Net(groups, d, nh))
    loss = lax.psum(loss_part, ("x", "y", "c"))

    small_names = [k for k in _PARAMS if k not in _BIG]
    small_full = dict(g_mix=g["g_mix"], g_ffn=g["g_ffn"], lam_re=g["lam_re"], lam_im=g["lam_im"], log_dt=g["log_dt"],
                      ssm_b_re=g["ssm_b_re"], ssm_b_im=g["ssm_b_im"], ssm_c_re=g["ssm_c_re"], ssm_c_im=g["ssm_c_im"],
                      ssm_d=g["ssm_d"], g_kv=g["g_kv"], b_f=g["b_f"], ffn_conv_w=g["conv_w"],
                      ffn_conv_b=g["conv_b"], g_final=g["g_final"])
    small_list = [small_full[k] for k in small_names]
    pack = _pack(small_list, N_CHIPS * N_CORES * 2 * SUBLANES)
    pack4 = pack.reshape(N_CHIPS, pack.shape[0] // N_CHIPS, LANES)
    pack_shard = _reduce_scatter([pack4], [F32], "small")[0]
    red_big = {k: g["big"][k, 0] if p[k].ndim == 2 else jnp.stack([g["big"][k, l] for l in range(p[k].shape[0])])
               for k in _BIG}
    pack_all = _all_gather([pack_shard], name="small_grads_all_gather")[0]
    red_small = dict(zip(small_names, _unpack(pack_all, small_list)))
    for k in _SMALL_SHARDED:
        width = p[k].shape[-1]
        red_small[k] = lax.dynamic_slice_in_dim(red_small[k], slot * width, width, axis=red_small[k].ndim - 1)

    grads, deltas, new_m, new_v = {}, {}, {}, {}
    for k in _BIG:
        grads[k] = red_big[k]
        deltas[k], new_m[k], new_v[k] = _adamw(p[k], grads[k], mom1[k], mom2[k], name=f"adamw_{k}")
    packs = [_pack([src[k] for k in small_names], SUBLANES) for src in (p, red_small, mom1, mom2)]
    like = [p[k] for k in small_names]
    outs = [_unpack(o, like) for o in _adamw(*packs, name="adamw_small")]
    for i, k in enumerate(small_names):
        grads[k] = red_small[k]
        deltas[k], new_m[k], new_v[k] = outs[0][i], outs[1][i], outs[2][i]
    return (loss, grad_x[None], *[grads[k] for k in _PARAMS], *[deltas[k] for k in _PARAMS],
            *[new_m[k] for k in _PARAMS], *[new_v[k] for k in _PARAMS])
```

```python
import functools

import jax
import jax.numpy as jnp
from jax import lax
from jax.experimental import pallas as pl
from jax.experimental.pallas import tpu as pltpu

F32 = jnp.float32
BF16 = jnp.bfloat16

RMS_EPS = 1e-6
ADAM_LR = 0.001
ADAM_B1 = 0.9
ADAM_B2 = 0.999
ADAM_EPS = 1e-08
ADAM_WD = 0.01
ADAM_STEP = 10
CONV_TAPS = 3

LANES = 128
SUBLANES = 8
HEAD_DIM = 64
FLASH_ROW_TILE = 32
S5_TIME_CHUNK = 1024
S5_BLOCK_GROUPS = 16
VMEM_LIMIT_BYTES = 48 << 20
MM_BLOCK_BUDGET_BYTES = 30 << 20
N_CHIPS = 4
N_CORES = 2
MESH = pl.DeviceIdType.MESH


def _cp(n_grid):
    return pltpu.CompilerParams(dimension_semantics=("arbitrary",) * n_grid, vmem_limit_bytes=VMEM_LIMIT_BYTES)


def _tile(n, pref, mult=SUBLANES):
    if n <= pref:
        return n
    t = (pref // mult) * mult
    while t >= mult:
        if n % t == 0:
            return t
        t -= mult
    return n


class _Comm:
    def __init__(self, ins, out_shapes, sems, start, finish):
        self.ins, self.out_shapes, self.sems, self.start, self.finish = ins, out_shapes, sems, start, finish


def _any_specs(n):
    return [pl.BlockSpec(memory_space=pl.ANY)] * n


def _run_comm(comm, *, name):
    n_in, n_out = len(comm.ins), len(comm.out_shapes)

    def body(*refs):
        ins, outs, sems = refs[:n_in], refs[n_in:n_in + n_out], refs[n_in + n_out:]
        comm.start(ins, outs, sems)
        comm.finish(ins, outs, sems)

    return pl.pallas_call(body, in_specs=_any_specs(n_in), out_specs=_any_specs(n_out),
                          out_shape=list(comm.out_shapes), scratch_shapes=list(comm.sems), name=name)(*comm.ins)


def _call(body, *, grid, in_specs, out_specs, out_shape, args, name, scratch_shapes=(), prefetch=(), comm=None):
    n_pre, n_in, n_out, n_scr = len(prefetch), len(in_specs), len(out_specs), len(scratch_shapes)
    in_specs, out_specs, out_shape = list(in_specs), list(out_specs), list(out_shape)
    scratch_shapes, args = list(scratch_shapes), list(args)
    kernel_body = body
    if comm is not None:
        n_cin, n_cout = len(comm.ins), len(comm.out_shapes)

        def kernel_body(*refs):
            pos = n_pre + n_in
            c_in = refs[pos:pos + n_cin]
            main_out = refs[pos + n_cin:pos + n_cin + n_out]
            pos += n_cin + n_out
            c_out = refs[pos:pos + n_cout]
            main_scr = refs[pos + n_cout:pos + n_cout + n_scr]
            sems = refs[pos + n_cout + n_scr:]
            ids = [pl.program_id(a) for a in range(len(grid))]
            first = functools.reduce(jnp.logical_and, [i == 0 for i in ids])
            last = functools.reduce(jnp.logical_and, [i == g - 1 for i, g in zip(ids, grid)])
            pl.when(first)(lambda: comm.start(c_in, c_out, sems))
            body(*refs[:n_pre + n_in], *main_out, *main_scr)
            pl.when(last)(lambda: comm.finish(c_in, c_out, sems))

        in_specs += _any_specs(n_cin)
        out_specs += _any_specs(n_cout)
        out_shape += list(comm.out_shapes)
        scratch_shapes += list(comm.sems)
        args += list(comm.ins)
    if prefetch:
        spec = pltpu.PrefetchScalarGridSpec(num_scalar_prefetch=n_pre, grid=grid, in_specs=in_specs,
                                            out_specs=out_specs, scratch_shapes=scratch_shapes)
        res = pl.pallas_call(kernel_body, grid_spec=spec, out_shape=out_shape, compiler_params=_cp(len(grid)),
                             name=name)(*prefetch, *args)
    else:
        res = pl.pallas_call(kernel_body, grid=grid, in_specs=in_specs, out_specs=out_specs, out_shape=out_shape,
                             scratch_shapes=scratch_shapes, compiler_params=_cp(len(grid)), name=name)(*args)
    return (res[:n_out], res[n_out:]) if comm is not None else res


def _row_tile(m, bytes_per_row, fixed_bytes):
    for tm in (1024, 512):
        if m % tm == 0 and 2 * (tm * bytes_per_row + fixed_bytes) <= MM_BLOCK_BUDGET_BYTES:
            return tm
    return _tile(m, 512)


def _dot(a, b, ca, cb):
    return lax.dot_general(a, b, (((ca,), (cb,)), ((), ())), preferred_element_type=F32)


def _epilogue_io(epilogue, m, tm, rows_axis, grid_rank):
    _, rows, consts, outs, accs = epilogue

    def at_rows(width):
        return pl.BlockSpec((tm, width), lambda *g: (g[rows_axis], 0))

    def whole(shape):
        return pl.BlockSpec(shape, lambda *g: (0,) * len(shape))

    in_specs = [at_rows(r.shape[1]) for r in rows] + [whole(c.shape) for c in consts]
    out_specs = [at_rows(wd) for wd, _ in outs] + [whole(s) for s in accs]
    out_shape = ([jax.ShapeDtypeStruct((m, wd), dt) for wd, dt in outs]
                 + [jax.ShapeDtypeStruct(s, F32) for s in accs])
    bytes_per_row = (sum(r.shape[1] * r.dtype.itemsize for r in rows)
                     + sum(wd * jnp.dtype(dt).itemsize for wd, dt in outs))
    return in_specs, out_specs, out_shape, bytes_per_row


def _epilogue_apply(epilogue, block, refs, first_row_tile):
    fn, rows, consts, outs, _ = epilogue
    n_in, n_out = len(rows) + len(consts), len(outs)
    res = fn(block, *[r[...] for r in refs[:n_in]])
    for o, val in zip(refs[n_in:n_in + n_out], res[:n_out]):
        o[...] = val.astype(o.dtype)
    a_refs = refs[n_in + n_out:]
    if a_refs:
        @pl.when(first_row_tile)
        def _():
            for a in a_refs:
                a[...] = jnp.zeros_like(a)
        for a, val in zip(a_refs, res[n_out:]):
            a[...] += val


def _mm_cols(x, w4, layer, *, wc, out_dtype=F32, scale=None, epilogue=None, name):
    m, k = x.shape
    slots, _, k0, k1 = w4.shape
    nb = k1 if wc == 0 else k0
    assert (k0 if wc == 0 else k1) == k
    if epilogue is None:
        tm = _row_tile(m, k * x.dtype.itemsize + nb * jnp.dtype(out_dtype).itemsize, k0 * k1 * w4.dtype.itemsize)
        extra_in, out_specs = [], pl.BlockSpec((tm, nb), lambda s, i: (i, s))
        out_shape = jax.ShapeDtypeStruct((m, slots * nb), out_dtype)
    else:
        assert slots == 1
        bytes_per_row = _epilogue_io(epilogue, m, SUBLANES, 1, 2)[3]
        tm = _row_tile(m, k * x.dtype.itemsize + bytes_per_row, k0 * k1 * w4.dtype.itemsize)
        extra_in, out_specs, out_shape, _ = _epilogue_io(epilogue, m, tm, 1, 2)

    def body(x_ref, w_ref, *refs):
        acc = _dot(x_ref[...].astype(BF16), w_ref[0, 0], 1, wc)
        if scale is not None:
            acc = acc * scale
        if epilogue is None:
            refs[0][...] = acc.astype(out_dtype)
        else:
            _epilogue_apply(epilogue, acc, refs, pl.program_id(1) == 0)

    extra_args = [] if epilogue is None else [*epilogue[1], *epilogue[2]]
    return pl.pallas_call(
        body, grid=(slots, m // tm),
        in_specs=[pl.BlockSpec((tm, k), lambda s, i: (i, 0)),
                  pl.BlockSpec((1, 1, k0, k1), lambda s, i: (s, layer, 0, 0)), *extra_in],
        out_specs=out_specs, out_shape=out_shape,
        compiler_params=_cp(2), name=name)(x, w4, *extra_args)


def _planes(a):
    return a if a.ndim == 3 else a[None]


def _mm_acc(x, w4, layer, *, wc, epilogue=None, name):
    x = _planes(x)
    n_planes, m, width = x.shape
    slots, _, k0, k1 = w4.shape
    kb = k0 if wc == 0 else k1
    nout = k1 if wc == 0 else k0
    assert n_planes * width == slots * kb
    spp = slots // n_planes
    x_spec_w = pl.BlockSpec((1, 1, k0, k1), lambda i, s: (s, layer, 0, 0))
    if epilogue is None:
        tm = _row_tile(m, kb * x.dtype.itemsize + nout * 4, k0 * k1 * w4.dtype.itemsize)

        def body(x_ref, w_ref, o_ref):
            @pl.when(pl.program_id(1) == 0)
            def _():
                o_ref[...] = jnp.zeros_like(o_ref)
            o_ref[...] += _dot(x_ref[0].astype(BF16), w_ref[0, 0], 1, wc)

        return pl.pallas_call(
            body, grid=(m // tm, slots),
            in_specs=[pl.BlockSpec((1, tm, kb), lambda i, s: (s // spp, i, s % spp)), x_spec_w],
            out_specs=pl.BlockSpec((tm, nout), lambda i, s: (i, 0)),
            out_shape=jax.ShapeDtypeStruct((m, nout), F32),
            compiler_params=_cp(2), name=name)(x, w4)

    bytes_per_row = _epilogue_io(epilogue, m, SUBLANES, 0, 2)[3]
    tm = _row_tile(m, kb * x.dtype.itemsize + nout * 2 + bytes_per_row, k0 * k1 * w4.dtype.itemsize)
    extra_in, out_specs, out_shape, _ = _epilogue_io(epilogue, m, tm, 0, 2)

    def body(x_ref, w_ref, *refs):
        acc = refs[-1]

        @pl.when(pl.program_id(1) == 0)
        def _():
            acc[...] = jnp.zeros_like(acc)
        acc[...] += _dot(x_ref[0].astype(BF16), w_ref[0, 0], 1, wc)

        @pl.when(pl.program_id(1) == slots - 1)
        def _():
            _epilogue_apply(epilogue, acc[...], refs[:-1], pl.program_id(0) == 0)

    return pl.pallas_call(
        body, grid=(m // tm, slots),
        in_specs=[pl.BlockSpec((1, tm, kb), lambda i, s: (s // spp, i, s % spp)), x_spec_w, *extra_in],
        out_specs=out_specs, out_shape=out_shape, scratch_shapes=[pltpu.VMEM((tm, nout), F32)],
        compiler_params=_cp(2), name=name)(x, w4, *epilogue[1], *epilogue[2])


def _mm_tn(x, dy, slots, *, scale=None, wire=False, name):
    m, k = x.shape
    dy = _planes(dy)
    n_planes, _, width = dy.shape
    n = n_planes * width // slots
    spp = slots // n_planes
    ta = _tile(k, 512, LANES)
    tm = m
    while tm > 512 and tm % 2 == 0 and (2 * tm * (ta * x.dtype.itemsize + n * dy.dtype.itemsize)
                                         + 2 * ta * n * 4) > MM_BLOCK_BUDGET_BYTES:
        tm //= 2
    n_m = m // tm

    def body(x_ref, dy_ref, o_ref, *wire_ref):
        @pl.when(pl.program_id(2) == 0)
        def _():
            o_ref[...] = jnp.zeros_like(o_ref)
        o_ref[0] += _dot(x_ref[...].astype(BF16), dy_ref[0].astype(BF16), 0, 0)
        if scale is not None or wire:
            @pl.when(pl.program_id(2) == n_m - 1)
            def _():
                if scale is not None:
                    o_ref[...] = o_ref[...] * scale
                if wire:
                    wire_ref[0][...] = o_ref[...].astype(BF16)

    out_spec = pl.BlockSpec((1, ta, n), lambda s, a, i: (s, a, 0))
    return pl.pallas_call(
        body, grid=(slots, k // ta, n_m),
        in_specs=[pl.BlockSpec((tm, ta), lambda s, a, i: (i, a)),
                  pl.BlockSpec((1, tm, n), lambda s, a, i: (s // spp, i, s % spp))],
        out_specs=[out_spec, out_spec] if wire else out_spec,
        out_shape=([jax.ShapeDtypeStruct((slots, k, n), dt) for dt in (F32, BF16)] if wire
                   else jax.ShapeDtypeStruct((slots, k, n), F32)),
        compiler_params=_cp(3), name=name)(x, dy)


def _rowwise(fn, rows, consts, outs, accs=(), *, tl=256, name):
    n_rows = rows[0].shape[0]
    tl = _tile(n_rows, tl)
    n_in = len(rows) + len(consts)
    n_out = len(outs)

    def body(*refs):
        res = fn(*[r[...] for r in refs[:n_in]])
        res = res if isinstance(res, (tuple, list)) else (res,)
        o_refs = refs[n_in:n_in + n_out]
        a_refs = refs[n_in + n_out:]
        for o, val in zip(o_refs, res[:n_out]):
            o[...] = val.astype(o.dtype)
        if a_refs:
            @pl.when(pl.program_id(0) == 0)
            def _():
                for a in a_refs:
                    a[...] = jnp.zeros_like(a)
            for a, val in zip(a_refs, res[n_out:]):
                a[...] += val

    in_specs = ([pl.BlockSpec((tl, r.shape[1]), lambda i: (i, 0)) for r in rows]
                + [pl.BlockSpec(c.shape, lambda i: (0, 0)) for c in consts])
    out_specs = ([pl.BlockSpec((tl, w), lambda i: (i, 0)) for w, _ in outs]
                 + [pl.BlockSpec(s, lambda i: (0, 0)) for s in accs])
    out_shape = ([jax.ShapeDtypeStruct((n_rows, w), dt) for w, dt in outs]
                 + [jax.ShapeDtypeStruct(s, F32) for s in accs])
    return pl.pallas_call(body, grid=(n_rows // tl,), in_specs=in_specs, out_specs=out_specs,
                          out_shape=out_shape, compiler_params=_cp(1), name=name)(*rows, *consts)


def _rms(x, g):
    return x * lax.rsqrt(jnp.mean(x * x, axis=-1, keepdims=True) + RMS_EPS) * g


def _sigmoid(x):
    return 1.0 / (1.0 + jnp.exp(-x))


def _glu(zz):
    d = zz.shape[1] // 2
    return zz[:, :d] * _sigmoid(zz[:, d:])


def _gelu(y):
    return jax.nn.gelu(y)


def _row2(v):
    return v.reshape(1, -1)


def _node_bwd_fn(counts):
    n_dy = sum(counts)

    def fn(d, hh, *rest):
        dys, gs = rest[:n_dy], rest[n_dy:]
        tot, dgs, pos = d, [], 0
        for g, cnt in zip(gs, counts):
            dy = dys[pos].astype(F32)
            for extra in dys[pos + 1:pos + cnt]:
                dy = dy + extra.astype(F32)
            pos += cnt
            _, vjp = jax.vjp(_rms, hh, g)
            dx, dg = vjp(dy)
            tot = tot + dx
            dgs.append(dg)
        return (tot, tot, *dgs)

    return fn


def _node_bwd(d_in, h, branches, *, name):
    width = h.shape[1]
    flat = [dy for _, dys in branches for dy in dys]
    res = _rowwise(_node_bwd_fn([len(dys) for _, dys in branches]), [d_in, h, *flat],
                   [_row2(g) for g, _ in branches], [(width, F32), (width, BF16)], [(1, width)] * len(branches),
                   name=name)
    return res[0], res[1], [r[0] for r in res[2:]]


def _s5_prep_fn(lr, li, ldt, br, bi, cr, ci, *, gq, h, p):
    dt = jnp.exp(ldt)
    mag = jnp.exp(lr * dt)
    lb_re = mag * jnp.cos(li * dt)
    lb_im = mag * jnp.sin(li * dt)
    den = lr * lr + li * li
    nr = lb_re - 1.0
    fr = (nr * lr + lb_im * li) / den
    fi = (lb_im * lr - nr * li) / den
    bb_re = fr * br - fi * bi
    bb_im = fr * bi + fi * br
    shape = (gq * h, gq * p)
    r = lax.broadcasted_iota(jnp.int32, shape, 0)
    c = lax.broadcasted_iota(jnp.int32, shape, 1)
    mask = jnp.where(jnp.right_shift(r, h.bit_length() - 1) == jnp.right_shift(c, p.bit_length() - 1), 1.0, 0.0)

    def expand(t):
        return jnp.concatenate([t] * gq, axis=0) * mask

    return lb_re, lb_im, expand(bb_re), expand(bb_im), expand(cr), expand(ci)


def _s5_prep(lr, li, ldt, br, bi, cr, ci, p, *, name):
    n = lr.shape[1]
    h = br.shape[0]
    gq = S5_BLOCK_GROUPS
    nq, cq = gq * p, gq * h
    nblk = n // nq
    fn = functools.partial(_s5_prep_fn, gq=gq, h=h, p=p)

    def body(lr_r, li_r, ldt_r, br_r, bi_r, cr_r, ci_r, lbr_o, lbi_o, wbr_o, wbi_o, wcr_o, wci_o):
        lb_re, lb_im, wbr, wbi, wcr, wci = fn(lr_r[...], li_r[...], ldt_r[...], br_r[...], bi_r[...],
                                              cr_r[...], ci_r[...])
        lbr_o[...] = lb_re
        lbi_o[...] = lb_im
        wbr_o[0] = wbr.astype(BF16)
        wbi_o[0] = wbi.astype(BF16)
        wcr_o[0] = wcr.astype(BF16)
        wci_o[0] = wci.astype(BF16)

    vec = pl.BlockSpec((1, nq), lambda q: (0, q))
    tab = pl.BlockSpec((h, nq), lambda q: (0, q))
    wsp = pl.BlockSpec((1, cq, nq), lambda q: (q, 0, 0))
    wsh = jax.ShapeDtypeStruct((nblk, cq, nq), BF16)
    vsh = jax.ShapeDtypeStruct((1, n), F32)
    return pl.pallas_call(body, grid=(nblk,), in_specs=[vec, vec, vec, tab, tab, tab, tab],
                          out_specs=[vec, vec, wsp, wsp, wsp, wsp], out_shape=[vsh, vsh, wsh, wsh, wsh, wsh],
                          compiler_params=_cp(1), name=name)(lr, li, ldt, br, bi, cr, ci)


def _s5_prep_bwd(lr, li, ldt, br, bi, cr, ci, p, dlbr, dlbi, dwbr, dwbi, dwcr, dwci, *, name):
    n = lr.shape[1]
    h = br.shape[0]
    gq = S5_BLOCK_GROUPS
    nq, cq = gq * p, gq * h
    nblk = n // nq
    fn = functools.partial(_s5_prep_fn, gq=gq, h=h, p=p)

    def body(lr_r, li_r, ldt_r, br_r, bi_r, cr_r, ci_r, dlbr_r, dlbi_r, dwbr_r, dwbi_r, dwcr_r, dwci_r,
             *outs):
        _, vjp = jax.vjp(fn, lr_r[...], li_r[...], ldt_r[...], br_r[...], bi_r[...], cr_r[...], ci_r[...])
        grads = vjp((dlbr_r[0], dlbi_r[0], dwbr_r[0], dwbi_r[0], dwcr_r[0], dwci_r[0]))
        for o, g in zip(outs, grads):
            o[...] = g

    vec = pl.BlockSpec((1, nq), lambda q: (0, q))
    tab = pl.BlockSpec((h, nq), lambda q: (0, q))
    vec3 = pl.BlockSpec((1, 1, nq), lambda q: (q, 0, 0))
    wsp = pl.BlockSpec((1, cq, nq), lambda q: (q, 0, 0))
    vsh = jax.ShapeDtypeStruct((1, n), F32)
    tsh = jax.ShapeDtypeStruct((h, n), F32)
    return pl.pallas_call(body, grid=(nblk,),
                          in_specs=[vec, vec, vec, tab, tab, tab, tab, vec3, vec3, wsp, wsp, wsp, wsp],
                          out_specs=[vec, vec, vec, tab, tab, tab, tab],
                          out_shape=[vsh, vsh, vsh, tsh, tsh, tsh, tsh],
                          compiler_params=_cp(1), name=name)(lr, li, ldt, br, bi, cr, ci,
                                                             dlbr, dlbi, dwbr, dwbi, dwcr, dwci)


def _scan_rows(s_re, s_im, a_re, a_im, c_re, c_im, *, reverse):
    t_rows, n = s_re.shape
    nb = t_rows // SUBLANES
    row = lax.broadcasted_iota(jnp.int32, (SUBLANES, n), 0)

    def cmul(x, y):
        return x[0] * y[0] - x[1] * y[1], x[0] * y[1] + x[1] * y[0]

    a1 = (jnp.broadcast_to(a_re, (SUBLANES, n)), jnp.broadcast_to(a_im, (SUBLANES, n)))
    a2 = cmul(a1, a1)
    a4 = cmul(a2, a2)
    steps = []
    for dist, (pr, pi) in ((1, a1), (2, a2), (4, a4)):
        keep = (row < SUBLANES - dist) if reverse else (row >= dist)
        steps.append((SUBLANES - dist if reverse else dist, (jnp.where(keep, pr, 0.0), jnp.where(keep, pi, 0.0))))
    pk = (a_re, a_im)
    tab_re = jnp.zeros((SUBLANES, n), F32)
    tab_im = jnp.zeros((SUBLANES, n), F32)
    for i in range(SUBLANES):
        at = (SUBLANES - 1 - i) if reverse else i
        tab_re = jnp.where(row == at, pk[0], tab_re)
        tab_im = jnp.where(row == at, pk[1], tab_im)
        pk = cmul(pk, (a_re, a_im))

    def step(b, carry):
        cr, ci = carry
        blk = (nb - 1 - b) if reverse else b
        off = pl.multiple_of(blk * SUBLANES, SUBLANES)
        x_re = s_re[pl.ds(off, SUBLANES), :]
        x_im = s_im[pl.ds(off, SUBLANES), :]
        for sh, (pr, pi) in steps:
            sh_re = pltpu.roll(x_re, sh, 0)
            sh_im = pltpu.roll(x_im, sh, 0)
            x_re, x_im = x_re + pr * sh_re - pi * sh_im, x_im + pr * sh_im + pi * sh_re
        x_re, x_im = x_re + tab_re * cr - tab_im * ci, x_im + tab_re * ci + tab_im * cr
        s_re[pl.ds(off, SUBLANES), :] = x_re
        s_im[pl.ds(off, SUBLANES), :] = x_im
        edge = 0 if reverse else SUBLANES - 1
        return x_re[edge:edge + 1, :], x_im[edge:edge + 1, :]

    return lax.fori_loop(0, nb, step, (c_re, c_im))


def _s5_fwd(u, prep, dskip, *, name, comm=None):
    lb_re, lb_im, wbr, wbi, wcr, wci = prep
    n_rows, _ = u.shape
    nblk, cq, nq = wbr.shape
    tt = _tile(n_rows, S5_TIME_CHUNK)
    nch = n_rows // tt

    def body(u_ref, wbr_r, wbi_r, wcr_r, wci_r, lbr_r, lbi_r, d_ref, y_ref, z_ref, s_re, s_im, sbr_o, sbi_o,
             c_re, c_im):
        @pl.when(pl.program_id(1) == 0)
        def _():
            c_re[...] = jnp.zeros_like(c_re)
            c_im[...] = jnp.zeros_like(c_im)
        uf = u_ref[...]
        ub = uf.astype(BF16)
        s_re[...] = _dot(ub, wbr_r[0], 1, 0)
        s_im[...] = _dot(ub, wbi_r[0], 1, 0)
        sbr_o[0] = c_re[...]
        sbi_o[0] = c_im[...]
        cr, ci = _scan_rows(s_re, s_im, lbr_r[...], lbi_r[...], c_re[...], c_im[...], reverse=False)
        c_re[...] = cr
        c_im[...] = ci
        y = _dot(s_re[...].astype(BF16), wcr_r[0], 1, 1) - _dot(s_im[...].astype(BF16), wci_r[0], 1, 1)
        y = y + d_ref[...] * uf
        y_ref[...] = y
        z_ref[...] = _gelu(y).astype(BF16)

    wsp = pl.BlockSpec((1, cq, nq), lambda q, i: (q, 0, 0))
    vec = pl.BlockSpec((1, nq), lambda q, i: (0, q))
    act = pl.BlockSpec((tt, cq), lambda q, i: (i, q))
    sb = pl.BlockSpec((1, 1, nq), lambda q, i: (i, 0, q))
    sbsh = jax.ShapeDtypeStruct((nch, 1, nblk * nq), F32)
    states = pl.BlockSpec((tt, nq), lambda q, i: (i, q))
    stsh = jax.ShapeDtypeStruct((n_rows, nblk * nq), F32)
    return _call(
        body, grid=(nblk, nch),
        in_specs=[act, wsp, wsp, wsp, wsp, vec, vec, pl.BlockSpec((1, cq), lambda q, i: (0, q))],
        out_specs=[act, act, states, states, sb, sb],
        out_shape=[jax.ShapeDtypeStruct(u.shape, F32), jax.ShapeDtypeStruct(u.shape, BF16), stsh, stsh, sbsh, sbsh],
        scratch_shapes=[pltpu.VMEM((1, nq), F32), pltpu.VMEM((1, nq), F32)],
        args=(u, wbr, wbi, wcr, wci, lb_re, lb_im, dskip), name=name, comm=comm)


def _s5_bwd(u, dy, st_re, st_im, sb_re, sb_im, prep, dskip, *, name, comm=None):
    lb_re, lb_im, wbr, wbi, wcr, wci = prep
    n_rows, _ = u.shape
    nblk, cq, nq = wbr.shape
    tt = _tile(n_rows, S5_TIME_CHUNK)
    nch = n_rows // tt

    def body(u_ref, dy_ref, s_re, s_im, sbr_r, sbi_r, wbr_r, wbi_r, wcr_r, wci_r, lbr_r, lbi_r, d_ref,
             du_ref, dwbr, dwbi, dwcr, dwci, dlbr, dlbi, dd_ref, g_re, g_im, lc_re, lc_im):
        @pl.when(pl.program_id(1) == 0)
        def _():
            for ref in (lc_re, lc_im, dwbr, dwbi, dwcr, dwci, dlbr, dlbi, dd_ref):
                ref[...] = jnp.zeros_like(ref)
        uf = u_ref[...]
        ub = uf.astype(BF16)
        dyf = dy_ref[...]
        dyb = dyf.astype(BF16)
        sr16 = s_re[...].astype(BF16)
        si16 = s_im[...].astype(BF16)
        dwcr[0] += _dot(dyb, sr16, 0, 0)
        dwci[0] -= _dot(dyb, si16, 0, 0)
        g_re[...] = _dot(dyb, wcr_r[0], 1, 0)
        g_im[...] = -_dot(dyb, wci_r[0], 1, 0)
        lcr, lci = _scan_rows(g_re, g_im, lbr_r[...], -lbi_r[...], lc_re[...], lc_im[...], reverse=True)
        lc_re[...] = lcr
        lc_im[...] = lci
        lam_r = g_re[...]
        lam_i = g_im[...]
        first = lax.broadcasted_iota(jnp.int32, (tt, nq), 0) == 0
        prev_r = jnp.where(first, sbr_r[0], pltpu.roll(s_re[...], 1, 0))
        prev_i = jnp.where(first, sbi_r[0], pltpu.roll(s_im[...], 1, 0))
        dlbr[0] += jnp.sum(lam_r * prev_r + lam_i * prev_i, axis=0, keepdims=True)
        dlbi[0] += jnp.sum(lam_i * prev_r - lam_r * prev_i, axis=0, keepdims=True)
        lr16 = lam_r.astype(BF16)
        li16 = lam_i.astype(BF16)
        du_ref[...] = _dot(lr16, wbr_r[0], 1, 1) + _dot(li16, wbi_r[0], 1, 1) + d_ref[...] * dyf
        dwbr[0] += _dot(ub, lr16, 0, 0)
        dwbi[0] += _dot(ub, li16, 0, 0)
        dd_ref[0] += jnp.sum(dyf * uf, axis=0, keepdims=True)

    last = nch - 1
    wsp = pl.BlockSpec((1, cq, nq), lambda q, i: (q, 0, 0))
    vec = pl.BlockSpec((1, nq), lambda q, i: (0, q))
    act = pl.BlockSpec((tt, cq), lambda q, i: (last - i, q))
    sb = pl.BlockSpec((1, 1, nq), lambda q, i: (last - i, 0, q))
    vec3 = pl.BlockSpec((1, 1, nq), lambda q, i: (q, 0, 0))
    dsp = pl.BlockSpec((1, 1, cq), lambda q, i: (q, 0, 0))
    wsh = jax.ShapeDtypeStruct((nblk, cq, nq), F32)
    v3sh = jax.ShapeDtypeStruct((nblk, 1, nq), F32)
    big = pltpu.VMEM((tt, nq), F32)
    states = pl.BlockSpec((tt, nq), lambda q, i: (last - i, q))
    return _call(
        body, grid=(nblk, nch),
        in_specs=[act, act, states, states, sb, sb, wsp, wsp, wsp, wsp, vec, vec,
                  pl.BlockSpec((1, cq), lambda q, i: (0, q))],
        out_specs=[act, wsp, wsp, wsp, wsp, vec3, vec3, dsp],
        out_shape=[jax.ShapeDtypeStruct(u.shape, F32), wsh, wsh, wsh, wsh, v3sh, v3sh,
                   jax.ShapeDtypeStruct((nblk, 1, cq), F32)],
        scratch_shapes=[big, big, pltpu.VMEM((1, nq), F32), pltpu.VMEM((1, nq), F32)],
        args=(u, dy, st_re, st_im, sb_re, sb_im, wbr, wbi, wcr, wci, lb_re, lb_im, dskip), name=name, comm=comm)


def _conv_taps(cur, prev, w, b):
    ext = jnp.concatenate([prev, cur], axis=0)
    x1 = pltpu.roll(ext, 1, 0)[SUBLANES:, :]
    x2 = pltpu.roll(ext, 2, 0)[SUBLANES:, :]
    return b + x2 * w[0:1, :] + x1 * w[1:2, :] + cur * w[2:3, :], x1, x2


def _conv_fwd(uu, cw, cb, *, name):
    n_rows, f2 = uu.shape
    f = f2 // 2
    tc = _tile(f, 1408, LANES)
    tl = _tile(n_rows, 256)
    nfb = f // tc

    def body(g_ref, u_ref, wg_ref, wu_ref, bg_ref, bu_ref, o_ref, pg, pu):
        @pl.when(pl.program_id(1) == 0)
        def _():
            pg[...] = jnp.zeros_like(pg)
            pu[...] = jnp.zeros_like(pu)
        gcur = g_ref[...]
        ucur = u_ref[...]
        cg, _, _ = _conv_taps(gcur, pg[...], wg_ref[...], bg_ref[...])
        cu, _, _ = _conv_taps(ucur, pu[...], wu_ref[...], bu_ref[...])
        o_ref[...] = (cg * _sigmoid(cg) * cu).astype(o_ref.dtype)
        pg[...] = gcur[tl - SUBLANES:, :]
        pu[...] = ucur[tl - SUBLANES:, :]

    return pl.pallas_call(
        body, grid=(nfb, n_rows // tl),
        in_specs=[pl.BlockSpec((tl, tc), lambda j, i: (i, j)), pl.BlockSpec((tl, tc), lambda j, i: (i, j + nfb)),
                  pl.BlockSpec((CONV_TAPS, tc), lambda j, i: (0, j)),
                  pl.BlockSpec((CONV_TAPS, tc), lambda j, i: (0, j + nfb)),
                  pl.BlockSpec((1, tc), lambda j, i: (0, j)), pl.BlockSpec((1, tc), lambda j, i: (0, j + nfb))],
        out_specs=pl.BlockSpec((tl, tc), lambda j, i: (i, j)),
        out_shape=jax.ShapeDtypeStruct((n_rows, f), BF16),
        scratch_shapes=[pltpu.VMEM((SUBLANES, tc), F32), pltpu.VMEM((SUBLANES, tc), F32)],
        compiler_params=_cp(2), name=name)(uu, uu, cw, cw, cb, cb)


def _conv_bwd(uu, dact, cw, cb, *, name):
    n_rows, f2 = uu.shape
    f = f2 // 2
    tc = _tile(f, 1408, LANES)
    tl = _tile(n_rows, 256)
    nfb = f // tc
    nrb = n_rows // tl
    halo_per_tile = tl // SUBLANES

    def body(g_ref, gh_ref, u_ref, uh_ref, da_ref, wg_ref, wu_ref, bg_ref, bu_ref,
             duu_ref, dw_ref, db_ref, nxt_g, nxt_u):
        i = pl.program_id(1)
        rb = nrb - 1 - i

        @pl.when(i == 0)
        def _():
            for ref in (nxt_g, nxt_u, dw_ref, db_ref):
                ref[...] = jnp.zeros_like(ref)
        has_prev = jnp.where(rb > 0, 1.0, 0.0)
        gcur, ucur = g_ref[...], u_ref[...]
        wg, wu = wg_ref[...], wu_ref[...]
        cg, g1, g2 = _conv_taps(gcur, gh_ref[...] * has_prev, wg, bg_ref[...])
        cu, u1, u2 = _conv_taps(ucur, uh_ref[...] * has_prev, wu, bu_ref[...])
        sg = _sigmoid(cg)
        silu = cg * sg
        da = da_ref[...]

        def transpose_conv(plane, d, cur, x1, x2, w, nxt):
            ext = jnp.concatenate([d, nxt[...]], axis=0)
            d1 = pltpu.roll(ext, tl + SUBLANES - 1, 0)[:tl, :]
            d2 = pltpu.roll(ext, tl + SUBLANES - 2, 0)[:tl, :]
            duu_ref[plane] = (w[2:3, :] * d + w[1:2, :] * d1 + w[0:1, :] * d2).astype(duu_ref.dtype)
            nxt[...] = d[0:SUBLANES, :]
            dw_ref[plane] += jnp.concatenate([jnp.sum(d * x2, axis=0, keepdims=True),
                                              jnp.sum(d * x1, axis=0, keepdims=True),
                                              jnp.sum(d * cur, axis=0, keepdims=True)], axis=0)
            db_ref[plane] += jnp.sum(d, axis=0, keepdims=True)

        transpose_conv(0, da * cu * (sg * (1.0 + cg * (1.0 - sg))), gcur, g1, g2, wg, nxt_g)
        transpose_conv(1, da * silu, ucur, u1, u2, wu, nxt_u)

    def halo(j, i):
        return jnp.maximum((nrb - 1 - i) * halo_per_tile - 1, 0)

    return pl.pallas_call(
        body, grid=(nfb, nrb),
        in_specs=[pl.BlockSpec((tl, tc), lambda j, i: (nrb - 1 - i, j)),
                  pl.BlockSpec((SUBLANES, tc), lambda j, i: (halo(j, i), j)),
                  pl.BlockSpec((tl, tc), lambda j, i: (nrb - 1 - i, j + nfb)),
                  pl.BlockSpec((SUBLANES, tc), lambda j, i: (halo(j, i), j + nfb)),
                  pl.BlockSpec((tl, tc), lambda j, i: (nrb - 1 - i, j)),
                  pl.BlockSpec((CONV_TAPS, tc), lambda j, i: (0, j)),
                  pl.BlockSpec((CONV_TAPS, tc), lambda j, i: (0, j + nfb)),
                  pl.BlockSpec((1, tc), lambda j, i: (0, j)),
                  pl.BlockSpec((1, tc), lambda j, i: (0, j + nfb))],
        out_specs=[pl.BlockSpec((2, tl, tc), lambda j, i: (0, nrb - 1 - i, j)),
                   pl.BlockSpec((2, CONV_TAPS, tc), lambda j, i: (0, 0, j)),
                   pl.BlockSpec((2, 1, tc), lambda j, i: (0, 0, j))],
        out_shape=[jax.ShapeDtypeStruct((2, n_rows, f), BF16), jax.ShapeDtypeStruct((2, CONV_TAPS, f), F32),
                   jax.ShapeDtypeStruct((2, 1, f), F32)],
        scratch_shapes=[pltpu.VMEM((SUBLANES, tc), F32), pltpu.VMEM((SUBLANES, tc), F32)],
        compiler_params=_cp(2), name=name)(uu, uu, uu, uu, dact, cw, cw, cb, cb)


def _log_sigmoid(x):
    t = jnp.exp(-jnp.abs(x))
    log1p_t = jnp.where(t < 1e-3, t * (1.0 - t * (0.5 - t * (1.0 / 3.0))), jnp.log(1.0 + t))
    return jnp.minimum(x, 0.0) - log1p_t


def _dlog_sigmoid(x):
    t = jnp.exp(-jnp.abs(x))
    return jnp.where(x >= 0, t, 1.0) / (1.0 + t)


def _tri_dot(tri, x):
    return jnp.dot(tri, x, precision=lax.Precision.HIGHEST, preferred_element_type=F32)


def _cum_fwd(fl, bf, *, name):
    n_rows, width = fl.shape
    tc = _tile(n_rows, 256)

    def body(fl_ref, bf_ref, o_ref, carry):
        @pl.when(pl.program_id(0) == 0)
        def _():
            carry[...] = jnp.zeros_like(carry)
        x = _log_sigmoid(fl_ref[...] + bf_ref[...])
        r = lax.broadcasted_iota(jnp.int32, (tc, tc), 0)
        c = lax.broadcasted_iota(jnp.int32, (tc, tc), 1)
        y = _tri_dot(jnp.where(r >= c, 1.0, 0.0), x) + carry[...]
        o_ref[...] = y
        carry[...] = y[tc - 1:tc, :]

    return pl.pallas_call(
        body, grid=(n_rows // tc,),
        in_specs=[pl.BlockSpec((tc, width), lambda i: (i, 0)), pl.BlockSpec((1, width), lambda i: (0, 0))],
        out_specs=pl.BlockSpec((tc, width), lambda i: (i, 0)),
        out_shape=jax.ShapeDtypeStruct(fl.shape, F32),
        scratch_shapes=[pltpu.VMEM((1, width), F32)], compiler_params=_cp(1), name=name)(fl, bf)


def _cum_bwd(dcum, fl, bf, *, name):
    n_rows, width = fl.shape
    tc = _tile(n_rows, 256)
    last = n_rows // tc - 1

    def body(dc_ref, fl_ref, bf_ref, dfl_ref, dbf_ref, carry):
        @pl.when(pl.program_id(0) == 0)
        def _():
            carry[...] = jnp.zeros_like(carry)
            dbf_ref[...] = jnp.zeros_like(dbf_ref)
        r = lax.broadcasted_iota(jnp.int32, (tc, tc), 0)
        c = lax.broadcasted_iota(jnp.int32, (tc, tc), 1)
        dls = _tri_dot(jnp.where(r <= c, 1.0, 0.0), dc_ref[...]) + carry[...]
        carry[...] = dls[0:1, :]
        dfl = dls * _dlog_sigmoid(fl_ref[...] + bf_ref[...])
        dfl_ref[...] = dfl.astype(dfl_ref.dtype)
        dbf_ref[...] += jnp.sum(dfl, axis=0, keepdims=True)

    return pl.pallas_call(
        body, grid=(n_rows // tc,),
        in_specs=[pl.BlockSpec((tc, width), lambda i: (last - i, 0)),
                  pl.BlockSpec((tc, width), lambda i: (last - i, 0)),
                  pl.BlockSpec((1, width), lambda i: (0, 0))],
        out_specs=[pl.BlockSpec((tc, width), lambda i: (last - i, 0)), pl.BlockSpec((1, width), lambda i: (0, 0))],
        out_shape=[jax.ShapeDtypeStruct(fl.shape, BF16), jax.ShapeDtypeStruct((1, width), F32)],
        scratch_shapes=[pltpu.VMEM((1, width), F32)], compiler_params=_cp(1), name=name)(dcum, fl, bf)


def _head_masks():
    lane = lax.broadcasted_iota(jnp.int32, (1, LANES), 1)
    return (lane < HEAD_DIM, lane >= HEAD_DIM)


def _flash_fwd(q, kv, cum_c, cum_r, *, tq, name, comm=None):
    n_rows, d = q.shape
    nhp = d // LANES
    tk = tq
    nq = n_rows // tq
    rt = _tile(tk, FLASH_ROW_TILE)
    reps = (1, tq // LANES)

    def body(qi_ref, kj_ref, q_ref, k_ref, v_ref, cq_ref, ck_ref, ot_ref, lse_ref, m0, m1, l0, l1, acc,
             s0, s1, p0, p1, b0, b1):
        i = qi_ref[pl.program_id(1)]
        j = kj_ref[pl.program_id(1)]
        ms, ls = (m0, m1), (l0, l1)
        head_rows = lax.broadcasted_iota(jnp.int32, (LANES, 1), 0) < HEAD_DIM

        @pl.when(j == 0)
        def _():
            for h in range(2):
                ms[h][...] = jnp.full_like(ms[h], -jnp.inf)
                ls[h][...] = jnp.zeros_like(ls[h])
            acc[...] = jnp.zeros_like(acc)

        def block(diagonal):
            qv, kk, vv = q_ref[...], k_ref[...], v_ref[...]
            a = acc[...]
            for h, msk in enumerate(_head_masks()):
                st_sc, pt_sc, bias_sc = ((s0, p0, b0), (s1, p1, b1))[h]
                st_sc[...] = _dot(kk, jnp.where(msk, qv, jnp.zeros_like(qv)), 1, 1)
                bias_sc[...] = jnp.broadcast_to(cq_ref[0, h:h + 1, 0:1] - ck_ref[0, :, h:h + 1], (tk, LANES))
                m_old, l_old = ms[h][...], ls[h][...]
                col_max = jnp.full((SUBLANES, tq), -jnp.inf, F32)
                for r in range(tk // rt):
                    rows = slice(r * rt, (r + 1) * rt)
                    s = st_sc[rows, :] + jnp.tile(bias_sc[rows, :], reps)
                    if diagonal:
                        key = r * rt + lax.broadcasted_iota(jnp.int32, (rt, tq), 0)
                        qry = lax.broadcasted_iota(jnp.int32, (rt, tq), 1)
                        s = jnp.where(key <= qry, s, -jnp.inf)
                    st_sc[rows, :] = s
                    for g in range(rt // SUBLANES):
                        col_max = jnp.maximum(col_max, s[g * SUBLANES:(g + 1) * SUBLANES, :])
                m_new = jnp.maximum(m_old, jnp.max(col_max, axis=0, keepdims=True))
                col_sum = jnp.zeros((SUBLANES, tq), F32)
                for r in range(tk // rt):
                    rows = slice(r * rt, (r + 1) * rt)
                    p = jnp.exp(st_sc[rows, :] - m_new)
                    for g in range(rt // SUBLANES):
                        col_sum = col_sum + p[g * SUBLANES:(g + 1) * SUBLANES, :]
                    pt_sc[rows, :] = p.astype(BF16)
                alpha = jnp.exp(m_old - m_new)
                ms[h][...] = m_new
                ls[h][...] = alpha * l_old + jnp.sum(col_sum, axis=0, keepdims=True)
                pv_t = _dot(jnp.where(msk, vv, jnp.zeros_like(vv)), pt_sc[...], 0, 0)
                a = a * jnp.where(head_rows == (h == 0), alpha, 1.0) + pv_t
            acc[...] = a

        pl.when(j < i)(functools.partial(block, False))
        pl.when(j == i)(functools.partial(block, True))

        @pl.when(j == i)
        def _():
            ot_ref[...] = (acc[...] * jnp.where(head_rows, 1.0 / l0[...], 1.0 / l1[...])).T
            lse_ref[0] = jnp.concatenate([m0[...] + jnp.log(l0[...]), m1[...] + jnp.log(l1[...])], axis=0)

    pairs = [(i, j) for i in range(nq) for j in range(i + 1)]
    qi = jnp.asarray([i for i, _ in pairs], jnp.int32)
    kj = jnp.asarray([j for _, j in pairs], jnp.int32)
    stat = pltpu.VMEM((1, tq), F32)
    return _call(
        body, grid=(nhp, len(pairs)), prefetch=(qi, kj),
        in_specs=[pl.BlockSpec((tq, LANES), lambda hp, t, qi, kj: (qi[t], hp)),
                  pl.BlockSpec((tk, LANES), lambda hp, t, qi, kj: (kj[t], hp)),
                  pl.BlockSpec((tk, LANES), lambda hp, t, qi, kj: (kj[t], nhp + hp)),
                  pl.BlockSpec((1, 2, tq), lambda hp, t, qi, kj: (hp, 0, qi[t])),
                  pl.BlockSpec((1, tk, 2), lambda hp, t, qi, kj: (hp, kj[t], 0))],
        out_specs=[pl.BlockSpec((tq, LANES), lambda hp, t, qi, kj: (qi[t], hp)),
                   pl.BlockSpec((1, 2, tq), lambda hp, t, qi, kj: (hp, 0, qi[t]))],
        scratch_shapes=[stat, stat, stat, stat, pltpu.VMEM((LANES, tq), F32), pltpu.VMEM((tk, tq), F32),
                        pltpu.VMEM((tk, tq), F32), pltpu.VMEM((tk, tq), BF16), pltpu.VMEM((tk, tq), BF16),
                        pltpu.VMEM((tk, LANES), F32), pltpu.VMEM((tk, LANES), F32)],
        out_shape=[jax.ShapeDtypeStruct((n_rows, d), F32), jax.ShapeDtypeStruct((nhp, 2, n_rows), F32)],
        args=(q, kv, kv, cum_r, cum_c), name=name, comm=comm)


def _head_delta(do, o, *, name):
    d = o.shape[1]

    def fn(dd, oo):
        prod = dd.astype(BF16).astype(F32) * oo
        r = lax.broadcasted_iota(jnp.int32, (d, LANES), 0)
        c = lax.broadcasted_iota(jnp.int32, (d, LANES), 1)
        return _tri_dot(prod, jnp.where(jnp.right_shift(r, HEAD_DIM.bit_length() - 1) == c, 1.0, 0.0))

    return _rowwise(fn, [do, o], [], [(LANES, F32)], name=name)[0]


def _flash_bwd(q, kv, lse_r, delta_r, do, cum_c, cum_r, *, tq, name, comm=None):
    n_rows, d = q.shape
    nhp = d // LANES
    tk = tq
    nq = n_rows // tq
    rt = _tile(tk, FLASH_ROW_TILE)
    reps = (1, tq // LANES)

    def body(qi_ref, kj_ref, q_ref, k_ref, v_ref, lse_ref, dl_ref, do_ref, cq_ref, ck_ref,
             dq_ref, dk_ref, dv_ref, dck_ref, dcq_ref, s0, dp0, p0, ds0, b0, b1, ck0, ck1):
        s1, dp1, p1, ds1 = s0, dp0, p0, ds0
        i = qi_ref[pl.program_id(1)]
        j = kj_ref[pl.program_id(1)]

        @pl.when(pl.program_id(1) == 0)
        def _():
            dq_ref[...] = jnp.zeros_like(dq_ref)
            dcq_ref[...] = jnp.zeros_like(dcq_ref)

        @pl.when(i == j)
        def _():
            for ref in (dk_ref, dv_ref, ck0, ck1):
                ref[...] = jnp.zeros_like(ref)

        def block(diagonal):
            qv, kk, vv = q_ref[...], k_ref[...], v_ref[...]
            dob = do_ref[...].astype(BF16)
            dq_acc = jnp.zeros((tq, LANES), F32)
            dk_acc = jnp.zeros((tk, LANES), F32)
            dv_acc = jnp.zeros((tk, LANES), F32)
            query_sums = []
            for h, msk in enumerate(_head_masks()):
                st_sc, dpt_sc, pt_sc, dst_sc, bias_sc, key_part = ((s0, dp0, p0, ds0, b0, ck0),
                                                                  (s1, dp1, p1, ds1, b1, ck1))[h]
                qh = jnp.where(msk, qv, jnp.zeros_like(qv))
                kh = jnp.where(msk, kk, jnp.zeros_like(kk))
                doh = jnp.where(msk, dob, jnp.zeros_like(dob))
                st_sc[...] = _dot(kk, qh, 1, 1)
                dpt_sc[...] = _dot(vv, doh, 1, 1)
                bias_sc[...] = jnp.broadcast_to(cq_ref[0, h:h + 1, 0:1] - ck_ref[0, :, h:h + 1], (tk, LANES))
                lse_row = lse_ref[0, h:h + 1, :]
                delta_row = dl_ref[0, h:h + 1, :]
                col_acc = jnp.zeros((SUBLANES, tq), F32)
                parts = []
                for r in range(tk // rt):
                    rows = slice(r * rt, (r + 1) * rt)
                    s = st_sc[rows, :] + jnp.tile(bias_sc[rows, :], reps)
                    if diagonal:
                        key = r * rt + lax.broadcasted_iota(jnp.int32, (rt, tq), 0)
                        qry = lax.broadcasted_iota(jnp.int32, (rt, tq), 1)
                        s = jnp.where(key <= qry, s, -jnp.inf)
                    p = jnp.exp(s - lse_row)
                    ds = p * (dpt_sc[rows, :] - delta_row)
                    for g in range(rt // SUBLANES):
                        col_acc = col_acc + ds[g * SUBLANES:(g + 1) * SUBLANES, :]
                    part = ds[:, 0:LANES]
                    for g in range(1, tq // LANES):
                        part = part + ds[:, g * LANES:(g + 1) * LANES]
                    parts.append(part)
                    pt_sc[rows, :] = p.astype(BF16)
                    dst_sc[rows, :] = ds.astype(BF16)
                key_part[...] += jnp.concatenate(parts, axis=0)
                query_sums.append(jnp.sum(col_acc, axis=0, keepdims=True))
                dv_acc = dv_acc + _dot(pt_sc[...], doh, 1, 0)
                dsb = dst_sc[...]
                dk_acc = dk_acc + _dot(dsb, qh, 1, 0)
                dq_acc = dq_acc + _dot(dsb, kh, 0, 0)
            off = pl.multiple_of(i * tq, tq)
            dq_ref[pl.ds(off, tq), :] += dq_acc
            dk_ref[...] += dk_acc
            dv_ref[...] += dv_acc
            dcq_ref[0, i] += jnp.concatenate(query_sums, axis=0)

        pl.when(i > j)(functools.partial(block, False))
        pl.when(i == j)(functools.partial(block, True))

        @pl.when(i == nq - 1)
        def _():
            two = lax.broadcasted_iota(jnp.int32, (tk, 2), 1)
            dck_ref[0] = jnp.where(two == 0, -jnp.sum(ck0[...], axis=1, keepdims=True),
                                   -jnp.sum(ck1[...], axis=1, keepdims=True))

    pairs = [(i, j) for j in range(nq) for i in range(j, nq)]
    qi = jnp.asarray([i for i, _ in pairs], jnp.int32)
    kj = jnp.asarray([j for _, j in pairs], jnp.int32)
    score = pltpu.VMEM((tk, tq), F32)
    score16 = pltpu.VMEM((tk, tq), BF16)
    keystat = pltpu.VMEM((tk, LANES), F32)
    return _call(
        body, grid=(nhp, len(pairs)), prefetch=(qi, kj),
        in_specs=[pl.BlockSpec((tq, LANES), lambda hp, t, qi, kj: (qi[t], hp)),
                  pl.BlockSpec((tk, LANES), lambda hp, t, qi, kj: (kj[t], hp)),
                  pl.BlockSpec((tk, LANES), lambda hp, t, qi, kj: (kj[t], nhp + hp)),
                  pl.BlockSpec((1, 2, tq), lambda hp, t, qi, kj: (hp, 0, qi[t])),
                  pl.BlockSpec((1, 2, tq), lambda hp, t, qi, kj: (hp, 0, qi[t])),
                  pl.BlockSpec((tq, LANES), lambda hp, t, qi, kj: (qi[t], hp)),
                  pl.BlockSpec((1, 2, tq), lambda hp, t, qi, kj: (hp, 0, qi[t])),
                  pl.BlockSpec((1, tk, 2), lambda hp, t, qi, kj: (hp, kj[t], 0))],
        out_specs=[pl.BlockSpec((n_rows, LANES), lambda hp, t, qi, kj: (0, hp)),
                   pl.BlockSpec((tk, LANES), lambda hp, t, qi, kj: (kj[t], hp)),
                   pl.BlockSpec((tk, LANES), lambda hp, t, qi, kj: (kj[t], hp)),
                   pl.BlockSpec((1, tk, 2), lambda hp, t, qi, kj: (hp, kj[t], 0)),
                   pl.BlockSpec((1, nq, 2, tq), lambda hp, t, qi, kj: (hp, 0, 0, 0))],
        scratch_shapes=[score, score, score16, score16, keystat, keystat, keystat, keystat],
        out_shape=[jax.ShapeDtypeStruct((n_rows, d), F32), jax.ShapeDtypeStruct((n_rows, d), F32),
                   jax.ShapeDtypeStruct((n_rows, d), F32), jax.ShapeDtypeStruct((nhp, n_rows, 2), F32),
                   jax.ShapeDtypeStruct((nhp, nq, 2, tq), F32)],
        args=(q, kv, kv, lse_r, delta_r, do, cum_r, cum_c), name=name, comm=comm)


def _s5_tables(w, layer):
    g, p = w["lam_re"].shape[1:]
    h = w["ssm_b_re"].shape[3]
    n = g * p
    lr = w["lam_re"][layer].reshape(1, n)
    li = w["lam_im"][layer].reshape(1, n)
    ldt = jnp.broadcast_to(w["log_dt"][layer][:, None], (g, p)).reshape(1, n)
    br = w["ssm_b_re"][layer].transpose(2, 0, 1).reshape(h, n)
    bi = w["ssm_b_im"][layer].transpose(2, 0, 1).reshape(h, n)
    cr = w["ssm_c_re"][layer].transpose(1, 0, 2).reshape(h, n)
    ci = w["ssm_c_im"][layer].transpose(1, 0, 2).reshape(h, n)
    return (lr, li, ldt, br, bi, cr, ci), (g, p, h)


def _local_step(x, tgt, w, net=None, *, attn_tile=1024):
    n_rows, d = x.shape
    n_layers = w["g_mix"].shape[0]
    n_s5 = w["lam_re"].shape[0]
    nh = w["b_f"].shape[0]
    nhp = nh // 2
    assert d == nh * HEAD_DIM
    tq = _tile(n_rows, attn_tile)
    g = {}
    saved = [dict() for _ in range(n_layers)]
    big = {}
    pending = {}

    def wt(name, layer):
        return w[name][layer]

    def carry_gather(group, run):
        if net is None or not net.has_group(group):
            return run(None)
        outs, got = run(net.gather_comm(group))
        net.store_gathered(group, got, w)
        return outs

    def carry_reduce(tag, run):
        keys = list(pending)
        grads = [pending[k][0] for k in keys]
        if net is None or not pending:
            big.update(zip(keys, grads))
            pending.clear()
            return run(None)
        parts = net.reduce_prepare(grads, [pending[k][1] for k in keys], tag)
        outs, landed = run(_chip_exchange_comm(parts))
        big.update(zip(keys, net.reduce_finish(parts, landed, grads, tag)))
        pending.clear()
        return outs

    def by_row_shard(m):
        return m.reshape(N_CHIPS, m.shape[0] // N_CHIPS, m.shape[1])

    def grad_and_copy(x_, dy_, name, *, col_slots=None, scale=None):
        g32, g16 = _mm_tn(x_, dy_, col_slots or 1, scale=scale, wire=True, name=name)
        return (g32, g16) if col_slots else (by_row_shard(g32[0]), by_row_shard(g16[0]))

    h = x
    nxt = _rowwise(lambda a, gg: _rms(a, gg), [x], [_row2(w["g_mix"][0])], [(d, F32)], name="rms_first")[0]
    kvb = fl = cum = cq3 = ck3 = hnkv = None
    bf_pad = jnp.zeros((1, LANES), F32).at[0, :nh].set(w["b_f"])
    for l in range(n_layers):
        sv = saved[l]
        sv["h"] = h
        g_ffn = _row2(w["g_ffn"][l])
        if l < n_s5:
            tabs, (_, p, _) = _s5_tables(w, l)
            prep = _s5_prep(*tabs, p, name=f"s5_prep{l}")
            dskip = w["ssm_d"][l].reshape(1, d)
            y, z, st_re, st_im, sb_re, sb_im = carry_gather(
                f"stage{l}", lambda comm, u=nxt, pr=prep, ds=dskip: _s5_fwd(u, pr, ds, name=f"s5_fwd{l}", comm=comm))
            zz = _mm_cols(z, *wt("w_glu", l), wc=0, name=f"glu_mm{l}")
            h1, hn2 = _rowwise(lambda hh, zq, gg: ((lambda t: (t, _rms(t, gg)))(hh + _glu(zq))),
                               [h, zz], [g_ffn], [(d, F32), (d, BF16)], name=f"mix_out{l}")
            sv.update(u=nxt, prep=prep, tabs=tabs, p=p, dskip=dskip, st_re=st_re, st_im=st_im, sb_re=sb_re,
                      sb_im=sb_im, y=y, z=z, zz=zz)
        else:
            j = l - n_s5
            qs = _mm_cols(nxt, *wt("w_q", j), wc=0, out_dtype=BF16, scale=HEAD_DIM ** -0.5, name=f"q_mm{j}")
            o, lse = carry_gather(
                f"stage{l}", lambda comm, q_=qs: _flash_fwd(q_, kvb, cq3, ck3, tq=tq, name=f"flash_fwd{j}", comm=comm))
            h1, hn2 = _mm_cols(o, *wt("w_o", j), wc=0, name=f"o_mm{j}",
                               epilogue=(lambda aa, hh, gg: ((lambda t: (t, _rms(t, gg)))(hh + aa)), [h], [g_ffn],
                                         [(d, F32), (d, BF16)], ()))
            sv.update(hn=nxt, qs=qs, o=o, lse=lse)
        uu = _mm_cols(hn2, *wt("w_in", l), wc=0, name=f"ffn_in{l}")
        cw, cb = w["conv_w"][l], _row2(w["conv_b"][l])
        act = _conv_fwd(uu, cw, cb, name=f"conv_fwd{l}")
        sv.update(h1=h1, hn2=hn2, uu=uu, act=act, cw=cw, cb=cb)

        def ffn_out(fn, rows, consts, outs, accs=()):
            return _mm_cols(act, *wt("w_out", l), wc=0, name=f"ffn_out{l}", epilogue=(fn, rows, consts, outs, accs))

        if l == n_layers - 1:
            def loss_fn(ff, hh, tt, gg):
                yv, vjp = jax.vjp(_rms, hh + ff, gg)
                err = yv - tt
                part = 0.5 * jnp.sum(jnp.mean(err * err, axis=-1, keepdims=True), axis=0, keepdims=True)
                dh, dg = vjp(err * (1.0 / d))
                return dh, dh, jnp.broadcast_to(part, (1, LANES)), dg
            dcur, dcur16, loss_row, dgf = ffn_out(loss_fn, [h1, tgt], [_row2(w["g_final"])],
                                                  [(d, F32), (d, BF16)], [(1, LANES), (1, d)])
            loss = loss_row[0, 0]
            g["g_final"] = dgf[0]
        elif l + 1 < n_s5:
            h, nxt = ffn_out(lambda ff, hh, gg: ((lambda t: (t, _rms(t, gg)))(hh + ff)), [h1],
                             [_row2(w["g_mix"][l + 1])], [(d, F32), (d, F32)])
        elif l + 1 == n_s5:
            h, nxt, hnkv = ffn_out(lambda ff, hh, g1, g2: ((lambda t: (t, _rms(t, g1), _rms(t, g2)))(hh + ff)), [h1],
                                   [_row2(w["g_mix"][l + 1]), _row2(w["g_kv"])],
                                   [(d, F32), (d, BF16), (d, BF16)])
            kvb = _mm_cols(hnkv, *wt("w_kv", 0), wc=0, out_dtype=BF16, name="kv_mm")
            fl = _mm_cols(hnkv, *wt("w_f", 0), wc=0, name="f_mm")
            cum = _cum_fwd(fl, bf_pad, name="cum_fwd")
            cq3 = cum[:, :nh].reshape(n_rows, nhp, 2).transpose(1, 0, 2)
            ck3 = cum[:, :nh].T.reshape(nhp, 2, n_rows)
        else:
            h, nxt = ffn_out(lambda ff, hh, gg: ((lambda t: (t, _rms(t, gg)))(hh + ff)), [h1],
                             [_row2(w["g_mix"][l + 1])], [(d, F32), (d, BF16)])

    per_layer = {k: [None] * n_layers for k in ("g_mix", "g_ffn", "conv_w", "conv_b")}
    per_s5 = {k: [None] * n_s5 for k in ("lam_re", "lam_im", "log_dt", "ssm_b_re", "ssm_b_im", "ssm_c_re",
                                         "ssm_c_im", "ssm_d")}
    dk_parts, dv_parts, dck_parts = [], [], []
    for l in reversed(range(n_layers)):
        sv = saved[l]
        dact = _mm_cols(dcur16, *wt("w_out", l), wc=1, name=f"ffn_out_dx{l}")
        pending["w_ffn_out", l] = grad_and_copy(sv["act"], dcur16, f"ffn_out_dw{l}")
        duu, dcw, dcb = _conv_bwd(sv["uu"], dact, sv["cw"], sv["cb"], name=f"conv_bwd{l}")
        per_layer["conv_w"][l] = jnp.concatenate([dcw[0], dcw[1]], axis=-1)
        per_layer["conv_b"][l] = jnp.concatenate([dcb[0, 0], dcb[1, 0]])
        node = _node_bwd_fn([1])
        d1, d1_16, dg = _mm_acc(duu, *wt("w_in", l), wc=1, name=f"ffn_in_dx{l}",
                                epilogue=(lambda dhn2, dd, hh, gg: node(dd, hh, dhn2, gg), [dcur, sv["h1"]],
                                          [_row2(w["g_ffn"][l])], [(d, F32), (d, BF16)], [(1, d)]))
        pending["w_ffn_in", l] = grad_and_copy(sv["hn2"], duu, f"ffn_in_dw{l}", col_slots=wt("w_in", l)[0].shape[0])
        per_layer["g_ffn"][l] = dg[0]
        if l < n_s5:
            def glu_bwd(zq, dd):
                _, vjp = jax.vjp(_glu, zq)
                return vjp(dd)[0]
            dzz = _rowwise(glu_bwd, [sv["zz"], d1], [], [(2 * d, BF16)], name=f"glu_bwd{l}")[0]
            def gelu_bwd(dd, yy):
                _, vjp = jax.vjp(_gelu, yy)
                return (vjp(dd)[0],)
            dy = _mm_acc(dzz, *wt("w_glu", l), wc=1, name=f"glu_dx{l}",
                         epilogue=(gelu_bwd, [sv["y"]], [], [(d, F32)], ()))[0]
            pending["w_glu", l] = grad_and_copy(sv["z"], dzz, f"glu_dw{l}", col_slots=wt("w_glu", l)[0].shape[0])
            du, dwbr, dwbi, dwcr, dwci, dlbr, dlbi, dd = carry_reduce(
                f"stage{l}", lambda comm, dy_=dy: _s5_bwd(sv["u"], dy_, sv["st_re"], sv["st_im"], sv["sb_re"],
                                                          sv["sb_im"], sv["prep"], sv["dskip"], name=f"s5_bwd{l}",
                                                          comm=comm))
            dlr, dli, dldt, dbr, dbi, dcr, dci = _s5_prep_bwd(*sv["tabs"], sv["p"], dlbr, dlbi, dwbr, dwbi, dwcr,
                                                              dwci, name=f"s5_prep_bwd{l}")
            gg, p = w["lam_re"].shape[1:]
            hh = w["ssm_b_re"].shape[3]
            per_s5["lam_re"][l] = dlr.reshape(gg, p)
            per_s5["lam_im"][l] = dli.reshape(gg, p)
            per_s5["log_dt"][l] = dldt.reshape(gg, p).sum(axis=1)
            per_s5["ssm_b_re"][l] = dbr.reshape(hh, gg, p).transpose(1, 2, 0)
            per_s5["ssm_b_im"][l] = dbi.reshape(hh, gg, p).transpose(1, 2, 0)
            per_s5["ssm_c_re"][l] = dcr.reshape(hh, gg, p).transpose(1, 0, 2)
            per_s5["ssm_c_im"][l] = dci.reshape(hh, gg, p).transpose(1, 0, 2)
            per_s5["ssm_d"][l] = dd.reshape(d)
            branches = [(w["g_mix"][l], [du])]
        else:
            j = l - n_s5
            do = _mm_cols(d1_16, *wt("w_o", j), wc=1, name=f"o_dx{j}")
            pending["w_o", j] = grad_and_copy(sv["o"], d1_16, f"o_dw{j}")
            delta_r = _head_delta(do, sv["o"], name=f"head_delta{j}")[:, :nh].T.reshape(nhp, 2, n_rows)
            dq, dk, dv, dck, dcq = carry_reduce(
                f"stage{l}", lambda comm, do_=do: _flash_bwd(sv["qs"], kvb, sv["lse"], delta_r, do_, cq3, ck3, tq=tq,
                                                          name=f"flash_bwd{j}", comm=comm))
            dk_parts.append(dk)
            dv_parts.append(dv)
            dck_parts.append(dck.transpose(1, 0, 2).reshape(n_rows, nh)
                             + dcq.transpose(0, 2, 1, 3).reshape(nh, n_rows).T)
            scale = HEAD_DIM ** -0.5
            pending["w_q", j] = grad_and_copy(sv["hn"], dq, f"q_dw{j}", scale=scale)
            branches = []
            if j == 0:
                def kv_sum(*parts):
                    half = len(parts) // 2
                    return jnp.concatenate([sum(parts[:half][1:], parts[0]),
                                            sum(parts[half:][1:], parts[half])], axis=1)
                dkv = _rowwise(kv_sum, dk_parts + dv_parts, [], [(2 * d, BF16)], name="dkv_sum")[0]
                dck_tot = dck_parts[0]
                for extra in dck_parts[1:]:
                    dck_tot = dck_tot + extra
                dcum = jnp.zeros((n_rows, LANES), F32).at[:, :nh].set(dck_tot)
                dfl, dbf = _cum_bwd(dcum, fl, bf_pad, name="cum_bwd")
                g["b_f"] = dbf[0, :nh]
                dhkv_a = _mm_cols(dkv, *wt("w_kv", 0), wc=1, name="kv_dx")
                dhkv_b = _mm_cols(dfl, *wt("w_f", 0), wc=1, name="f_dx")
                d_kvf = jnp.concatenate([_mm_tn(hnkv, dkv, 1, name="kv_dw")[0],
                                         _mm_tn(hnkv, dfl, 1, name="f_dw")[0][:, :nh]], axis=1)
                d_kvf = d_kvf.reshape(d, N_CHIPS, -1).transpose(1, 0, 2)
                pending["w_kvf", 0] = (d_kvf, d_kvf.astype(BF16))
                branches.append((w["g_kv"], [dhkv_a, dhkv_b]))
            node = _node_bwd_fn([1] + [len(dys) for _, dys in branches])
            gains = [_row2(w["g_mix"][l])] + [_row2(gn) for gn, _ in branches]
            res = _mm_cols(dq, *wt("w_q", j), wc=1, scale=scale, name=f"q_dx{j}",
                           epilogue=(lambda dhn, dd, hh, *rest: node(dd, hh, dhn, *rest),
                                     [d1, sv["h"], *[dy for _, dys in branches for dy in dys]], gains,
                                     [(d, F32), (d, BF16)], [(1, d)] * len(gains)))
            dcur, dcur16, dgs = res[0], res[1], [r[0] for r in res[2:]]
        if l < n_s5:
            dcur, dcur16, dgs = _node_bwd(d1, sv["h"], branches, name=f"mix_norm_bwd{l}")
        per_layer["g_mix"][l] = dgs[0]
        if len(dgs) > 1:
            g["g_kv"] = dgs[1]

    if pending:
        grads = [g32 for g32, _ in pending.values()]
        big.update(zip(list(pending), grads if net is None else net.reduce_blocking(grads, "tail")))
    for k, v in (*per_layer.items(), *per_s5.items()):
        g[k] = jnp.stack(v)
    g["big"] = big
    return loss, dcur, g


def _position():
    x, y, c = lax.axis_index("x"), lax.axis_index("y"), lax.axis_index("c")
    chips = [(1 - x, y), (x, 1 - y), (1 - x, 1 - y)]
    return x, y, c, chips


def _all_gather_comm(shards):
    n = len(shards)

    def descriptors(ins, outs, sems):
        send_sems, recv_sems = sems
        x, y, c, chips = _position()
        my_slot = 2 * x + y
        sibling = (x, y, 1 - c)

        def rows(t, half):
            hr = ins[t].shape[0] // 2
            return pl.ds(half * hr, hr)

        def remote(k, t, src, dst, to):
            return pltpu.make_async_remote_copy(src_ref=src, dst_ref=dst, send_sem=send_sems.at[k, t],
                                                recv_sem=recv_sems.at[k, t], device_id=to, device_id_type=MESH)

        own = [remote(6, t, ins[t], outs[t].at[my_slot], sibling) for t in range(n)]
        ici = [remote(j, t, ins[t].at[rows(t, c)], outs[t].at[my_slot, rows(t, c)], (*chip, c))
               for j, chip in enumerate(chips) for t in range(n)]
        slots = [2 * chip[0] + chip[1] for chip in chips]
        fwd = [[remote(3 + j, t, outs[t].at[slots[j], rows(t, c)], outs[t].at[slots[j], rows(t, c)], sibling)
                for t in range(n)] for j in range(len(chips))]
        landed = [[remote(j, t, outs[t].at[slots[j], rows(t, c)], outs[t].at[slots[j], rows(t, c)], (*chips[j], c))
                   for t in range(n)] for j in range(len(chips))]
        from_sibling = [remote(3 + j, t, outs[t].at[slots[j], rows(t, 1 - c)], outs[t].at[slots[j], rows(t, 1 - c)],
                               sibling) for j in range(len(chips)) for t in range(n)]
        return own, ici, fwd, landed, from_sibling

    def start(ins, outs, sems):
        own, ici, _, _, _ = descriptors(ins, outs, sems)
        for cp in own + ici:
            cp.start()

    def finish(ins, outs, sems):
        own, ici, fwd, landed, from_sibling = descriptors(ins, outs, sems)
        for j in range(len(fwd)):
            for cp in landed[j]:
                cp.wait_recv()
            for cp in fwd[j]:
                cp.start()
        for cp in from_sibling + own:
            cp.wait_recv()
        for cp in own + ici + [cp for group in fwd for cp in group]:
            cp.wait_send()

    return _Comm(list(shards), [jax.ShapeDtypeStruct((N_CHIPS,) + a.shape, a.dtype) for a in shards],
                 [pltpu.SemaphoreType.DMA((7, n)), pltpu.SemaphoreType.DMA((7, n))], start, finish)


def _all_gather(shards, *, name):
    return _run_comm(_all_gather_comm(shards), name=name)


def _pair_exchange(grads, *, name):
    n = len(grads)

    def body(*refs):
        ins, outs = refs[:n], refs[n:2 * n]
        send_sems, recv_sems = refs[2 * n:]
        x, y, c, _ = _position()
        copies = [pltpu.make_async_remote_copy(src_ref=ins[t].at[:, 1 - c], dst_ref=outs[t],
                                               send_sem=send_sems.at[t], recv_sem=recv_sems.at[t],
                                               device_id=(x, y, 1 - c), device_id_type=MESH) for t in range(n)]
        for cp in copies:
            cp.start()
        for cp in copies:
            cp.wait()

    return pl.pallas_call(
        body, in_specs=_any_specs(n), out_specs=_any_specs(n),
        out_shape=[jax.ShapeDtypeStruct((a.shape[0],) + a.shape[2:], a.dtype) for a in grads],
        scratch_shapes=[pltpu.SemaphoreType.DMA((n,)), pltpu.SemaphoreType.DMA((n,))], name=name)(*grads)


def _chip_exchange_comm(parts):
    n = len(parts)

    def copies(ins, outs, sems):
        send_sems, recv_sems = sems
        _, _, c, chips = _position()
        return [pltpu.make_async_remote_copy(src_ref=ins[t].at[2 * chip[0] + chip[1]], dst_ref=outs[t].at[j],
                                             send_sem=send_sems.at[j, t], recv_sem=recv_sems.at[j, t],
                                             device_id=(*chip, c), device_id_type=MESH)
                for j, chip in enumerate(chips) for t in range(n)]

    def start(ins, outs, sems):
        for cp in copies(ins, outs, sems):
            cp.start()

    def finish(ins, outs, sems):
        for cp in copies(ins, outs, sems):
            cp.wait()

    return _Comm(list(parts), [jax.ShapeDtypeStruct((N_CHIPS - 1,) + a.shape[1:], a.dtype) for a in parts],
                 [pltpu.SemaphoreType.DMA((N_CHIPS - 1, n)), pltpu.SemaphoreType.DMA((N_CHIPS - 1, n))],
                 start, finish)


def _pair_share(both, *, name):
    n = len(both)

    def body(*refs):
        ins, outs = refs[:n], refs[n:2 * n]
        send_sems, recv_sems = refs[2 * n:]
        x, y, c, _ = _position()
        for t in range(n):
            pltpu.make_async_remote_copy(src_ref=ins[t].at[c], dst_ref=outs[t].at[c], send_sem=send_sems.at[t],
                                         recv_sem=recv_sems.at[t], device_id=(x, y, 1 - c),
                                         device_id_type=MESH).start()
        for t in range(n):
            pltpu.make_async_remote_copy(src_ref=ins[t].at[c], dst_ref=outs[t].at[1 - c], send_sem=send_sems.at[t],
                                         recv_sem=recv_sems.at[t], device_id=(x, y, 1 - c),
                                         device_id_type=MESH).wait()

    return pl.pallas_call(
        body, in_specs=_any_specs(n), out_specs=_any_specs(n),
        out_shape=[jax.ShapeDtypeStruct(a.shape, a.dtype) for a in both],
        input_output_aliases={t: t for t in range(n)},
        scratch_shapes=[pltpu.SemaphoreType.DMA((n,)), pltpu.SemaphoreType.DMA((n,))], name=name)(*both)


def _sum_pair(grad, landed, c, wire_dtype, *, name):
    slots, _, m, n = grad.shape
    tm = _tile(m, 256, 2 * SUBLANES)

    def body(c_ref, g_ref, l_ref, o_ref):
        o_ref[...] = (g_ref[0] + l_ref[...]).astype(wire_dtype)

    return pl.pallas_call(
        body,
        grid_spec=pltpu.PrefetchScalarGridSpec(
            num_scalar_prefetch=1, grid=(slots, m // tm),
            in_specs=[pl.BlockSpec((1, 1, tm, n), lambda s, i, c_ref: (s, c_ref[0], i, 0)),
                      pl.BlockSpec((1, tm, n), lambda s, i, c_ref: (s, i, 0))],
            out_specs=pl.BlockSpec((1, tm, n), lambda s, i, c_ref: (s, i, 0))),
        out_shape=jax.ShapeDtypeStruct((slots, m, n), wire_dtype), compiler_params=_cp(2), name=name)(
            c, grad, landed)


def _sum_chips(part, landed, slot_c, *, name):
    _, m, n = part.shape
    tm = _tile(m, 256, 2 * SUBLANES)

    def body(s_ref, p_ref, l_ref, o_ref):
        acc = p_ref[0].astype(F32)
        for j in range(N_CHIPS - 1):
            acc = acc + l_ref[j].astype(F32)
        o_ref[0] = acc

    return pl.pallas_call(
        body,
        grid_spec=pltpu.PrefetchScalarGridSpec(
            num_scalar_prefetch=1, grid=(m // tm,),
            in_specs=[pl.BlockSpec((1, tm, n), lambda i, s_ref: (s_ref[0], i, 0)),
                      pl.BlockSpec((N_CHIPS - 1, tm, n), lambda i, s_ref: (0, i, 0))],
            out_specs=pl.BlockSpec((1, tm, n), lambda i, s_ref: (s_ref[1], i, 0))),
        out_shape=jax.ShapeDtypeStruct((N_CORES, m, n), F32), compiler_params=_cp(1), name=name)(
            slot_c, part, landed)


def _reduce_prepare(grads, wire_dtypes, tag, copies=None):
    c = lax.axis_index("c").reshape(1).astype(jnp.int32)

    def halves(a):
        lead, last = a.shape[1], a.shape[-1]
        mid = 1
        for s in a.shape[2:-1]:
            mid *= s
        return a.reshape(N_CHIPS, N_CORES, (lead // N_CORES) * mid, last)

    views = [halves(a) for a in grads]
    landed = _pair_exchange(views if copies is None else [halves(a) for a in copies],
                            name=f"rs_pair_exchange_{tag}")
    return [_sum_pair(v, l, c, wire_dtypes[t], name=f"rs_pair_sum_{tag}_{t}")
            for t, (v, l) in enumerate(zip(views, landed))]


def _reduce_finish(parts, landed, grads, tag):
    slot_c = jnp.stack([2 * lax.axis_index("x") + lax.axis_index("y"), lax.axis_index("c")]).astype(jnp.int32)
    both = [_sum_chips(p, l, slot_c, name=f"rs_chip_sum_{tag}_{t}") for t, (p, l) in enumerate(zip(parts, landed))]
    full = _pair_share(both, name=f"rs_pair_share_{tag}")
    return [f.reshape(a.shape[1:]) for f, a in zip(full, grads)]


def _reduce_scatter(grads, wire_dtypes, tag):
    parts = _reduce_prepare(grads, wire_dtypes, tag)
    landed = _run_comm(_chip_exchange_comm(parts), name=f"rs_chip_exchange_{tag}")
    return _reduce_finish(parts, landed, grads, tag)


class _Net:
    def __init__(self, groups, d, nh):
        self.groups, self.d, self.nh = groups, d, nh

    def has_group(self, group):
        return bool(self.groups.get(group))

    def gather_comm(self, group):
        return _all_gather_comm([shard for _, _, shard in self.groups[group]])

    def store_gathered(self, group, got, w):
        d, nh = self.d, self.nh
        for (name, layer, _), full in zip(self.groups[group], got):
            if name == "w_kvf":
                mat = full.transpose(1, 0, 2).reshape(d, -1)
                w["w_kv"][0] = (mat[:, :2 * d][None, None], 0)
                w["w_f"][0] = (jnp.zeros((d, LANES), BF16).at[:, :nh].set(mat[:, 2 * d:])[None, None], 0)
            elif name in ("w_in", "w_glu"):
                w[name][layer] = (full[:, None], 0)
            else:
                w[name][layer] = (full.reshape(1, 1, -1, full.shape[-1]), 0)

    def reduce_prepare(self, grads, copies, tag):
        return _reduce_prepare(grads, [BF16] * len(grads), tag, copies)

    def reduce_finish(self, parts, landed, grads, tag):
        return _reduce_finish(parts, landed, grads, tag)

    def reduce_blocking(self, grads, tag):
        return _reduce_scatter(grads, [BF16] * len(grads), tag)


def _adamw(w, g, m, v, *, name):
    def fn(ww, gg, mm, vv):
        mm = ADAM_B1 * mm + (1.0 - ADAM_B1) * gg
        vv = ADAM_B2 * vv + (1.0 - ADAM_B2) * (gg * gg)
        m_hat = mm / (1.0 - ADAM_B1 ** ADAM_STEP)
        v_hat = vv / (1.0 - ADAM_B2 ** ADAM_STEP)
        delta = -ADAM_LR * (m_hat / (jnp.sqrt(v_hat) + ADAM_EPS) + ADAM_WD * ww)
        return delta, mm, vv

    shape = w.shape
    two_d = [a.reshape(-1, shape[-1]) for a in (w, g, m, v)]
    outs = _rowwise(fn, two_d, [], [(shape[-1], F32)] * 3, name=name)
    return [o.reshape(shape) for o in outs]


def _to_bf16(a, *, name):
    two_d = a.reshape(-1, a.shape[-1])
    return _rowwise(lambda t: t, [two_d], [], [(a.shape[-1], BF16)], name=name)[0].reshape(a.shape)


def _pack(arrays, rows_multiple):
    flat = jnp.concatenate([a.reshape(-1) for a in arrays])
    rows = -(-flat.shape[0] // LANES)
    rows = -(-rows // rows_multiple) * rows_multiple
    return jnp.pad(flat, (0, rows * LANES - flat.shape[0])).reshape(rows, LANES)


def _unpack(packed, like):
    flat = packed.reshape(-1)
    out, pos = [], 0
    for a in like:
        out.append(flat[pos:pos + a.size].reshape(a.shape))
        pos += a.size
    return out


_PARAMS = ("g_mix", "g_ffn", "lam_re", "lam_im", "log_dt", "ssm_b_re", "ssm_b_im", "ssm_c_re", "ssm_c_im", "ssm_d",
           "w_glu", "g_kv", "w_kvf", "b_f", "w_q", "w_o", "w_ffn_in", "ffn_conv_w", "ffn_conv_b", "w_ffn_out",
           "g_final")
_BIG = ("w_glu", "w_kvf", "w_q", "w_o", "w_ffn_in", "w_ffn_out")
_SMALL_SHARDED = ("ssm_d", "ffn_conv_w")


def kernel(x, g_mix, g_ffn, lam_re, lam_im, log_dt, ssm_b_re, ssm_b_im, ssm_c_re, ssm_c_im, ssm_d, w_glu, g_kv, w_kvf, b_f, w_q, w_o, w_ffn_in, ffn_conv_w, ffn_conv_b, w_ffn_out, g_final, loss_target, m_g_mix, m_g_ffn, m_lam_re, m_lam_im, m_log_dt, m_ssm_b_re, m_ssm_b_im, m_ssm_c_re, m_ssm_c_im, m_ssm_d, m_w_glu, m_g_kv, m_w_kvf, m_b_f, m_w_q, m_w_o, m_w_ffn_in, m_ffn_conv_w, m_ffn_conv_b, m_w_ffn_out, m_g_final, v_g_mix, v_g_ffn, v_lam_re, v_lam_im, v_log_dt, v_ssm_b_re, v_ssm_b_im, v_ssm_c_re, v_ssm_c_im, v_ssm_d, v_w_glu, v_g_kv, v_w_kvf, v_b_f, v_w_q, v_w_o, v_w_ffn_in, v_ffn_conv_w, v_ffn_conv_b, v_w_ffn_out, v_g_final):
    p = dict(g_mix=g_mix, g_ffn=g_ffn, lam_re=lam_re, lam_im=lam_im, log_dt=log_dt, ssm_b_re=ssm_b_re,
             ssm_b_im=ssm_b_im, ssm_c_re=ssm_c_re, ssm_c_im=ssm_c_im, ssm_d=ssm_d, w_glu=w_glu, g_kv=g_kv,
             w_kvf=w_kvf, b_f=b_f, w_q=w_q, w_o=w_o, w_ffn_in=w_ffn_in, ffn_conv_w=ffn_conv_w,
             ffn_conv_b=ffn_conv_b, w_ffn_out=w_ffn_out, g_final=g_final)
    mom1 = dict(zip(_PARAMS, (m_g_mix, m_g_ffn, m_lam_re, m_lam_im, m_log_dt, m_ssm_b_re, m_ssm_b_im, m_ssm_c_re,
                              m_ssm_c_im, m_ssm_d, m_w_glu, m_g_kv, m_w_kvf, m_b_f, m_w_q, m_w_o, m_w_ffn_in,
                              m_ffn_conv_w, m_ffn_conv_b, m_w_ffn_out, m_g_final)))
    mom2 = dict(zip(_PARAMS, (v_g_mix, v_g_ffn, v_lam_re, v_lam_im, v_log_dt, v_ssm_b_re, v_ssm_b_im, v_ssm_c_re,
                              v_ssm_c_im, v_ssm_d, v_w_glu, v_g_kv, v_w_kvf, v_b_f, v_w_q, v_w_o, v_w_ffn_in,
                              v_ffn_conv_w, v_ffn_conv_b, v_w_ffn_out, v_g_final)))
    d = x.shape[-1]
    nh = b_f.shape[0]
    slot = 2 * lax.axis_index("x") + lax.axis_index("y")

    wb = {k: _to_bf16(p[k], name=f"to_bf16_{k}") for k in _BIG}
    gd, gcw = _all_gather([ssm_d, ffn_conv_w], name="first_all_gather")
    n_lay, n_s5 = w_ffn_in.shape[0], lam_re.shape[0]
    n_fox = n_lay - n_s5
    groups = {f"stage{l}": [("w_in", l, wb["w_ffn_in"][l]), ("w_out", l, wb["w_ffn_out"][l])] for l in range(n_lay)}
    for l in range(n_s5):
        groups[f"stage{l}"].append(("w_glu", l, wb["w_glu"][l]))
    groups[f"stage{n_s5 - 1}"] += [("w_kvf", 0, wb["w_kvf"])] + [(k, j, wb[k][j]) for k in ("w_q", "w_o")
                                                                for j in range(n_fox)]
    w = dict(p)
    w.update(w_glu=[None] * n_s5, w_in=[None] * n_lay, w_out=[None] * n_lay, w_q=[None] * n_fox,
             w_o=[None] * n_fox, w_kv=[None], w_f=[None],
             conv_w=gcw.transpose(1, 2, 0, 3).reshape(n_lay, CONV_TAPS, -1), conv_b=ffn_conv_b,
             ssm_d=gd.transpose(1, 0, 2).reshape(gd.shape[1], d))

    loss_part, grad_x, g = _local_step(x[0], loss_target[0], w, _Net(groups, d, nh))
    loss = lax.psum(loss_part, ("x", "y", "c"))

    small_names = [k for k in _PARAMS if k not in _BIG]
    small_full = dict(g_mix=g["g_mix"], g_ffn=g["g_ffn"], lam_re=g["lam_re"], lam_im=g["lam_im"], log_dt=g["log_dt"],
                      ssm_b_re=g["ssm_b_re"], ssm_b_im=g["ssm_b_im"], ssm_c_re=g["ssm_c_re"], ssm_c_im=g["ssm_c_im"],
                      ssm_d=g["ssm_d"], g_kv=g["g_kv"], b_f=g["b_f"], ffn_conv_w=g["conv_w"],
                      ffn_conv_b=g["conv_b"], g_final=g["g_final"])
    small_list = [small_full[k] for k in small_names]
    pack = _pack(small_list, N_CHIPS * N_CORES * 2 * SUBLANES)
    pack4 = pack.reshape(N_CHIPS, pack.shape[0] // N_CHIPS, LANES)
    pack_shard = _reduce_scatter([pack4], [F32], "small")[0]
    red_big = {k: g["big"][k, 0] if p[k].ndim == 2 else jnp.stack([g["big"][k, l] for l in range(p[k].shape[0])])
               for k in _BIG}
    pack_all = _all_gather([pack_shard], name="small_grads_all_gather")[0]
    red_small = dict(zip(small_names, _unpack(pack_all, small_list)))
    for k in _SMALL_SHARDED:
        width = p[k].shape[-1]
        red_small[k] = lax.dynamic_slice_in_dim(red_small[k], slot * width, width, axis=red_small[k].ndim - 1)

    grads, deltas, new_m, new_v = {}, {}, {}, {}
    for k in _BIG:
        grads[k] = red_big[k]
        deltas[k], new_m[k], new_v[k] = _adamw(p[k], grads[k], mom1[k], mom2[k], name=f"adamw_{k}")
    packs = [_pack([src[k] for k in small_names], SUBLANES) for src in (p, red_small, mom1, mom2)]
    like = [p[k] for k in small_names]
    outs = [_unpack(o, like) for o in _adamw(*packs, name="adamw_small")]
    for i, k in enumerate(small_names):
        grads[k] = red_small[k]
        deltas[k], new_m[k], new_v[k] = outs[0][i], outs[1][i], outs[2][i]
    return (loss, grad_x[None], *[grads[k] for k in _PARAMS], *[deltas[k] for k in _PARAMS],
            *[new_m[k] for k in _PARAMS], *[new_v[k] for k in _PARAMS])
```

```python
import functools

import jax
import jax.numpy as jnp
from jax import lax
from jax.experimental import pallas as pl
from jax.experimental.pallas import tpu as pltpu

F32 = jnp.float32
BF16 = jnp.bfloat16

RMS_EPS = 1e-6
ADAM_LR = 0.001
ADAM_B1 = 0.9
ADAM_B2 = 0.999
ADAM_EPS = 1e-08
ADAM_WD = 0.01
ADAM_STEP = 10
CONV_TAPS = 3

LANES = 128
SUBLANES = 8
HEAD_DIM = 64
FLASH_ROW_TILE = 32
S5_TIME_CHUNK = 1024
S5_BLOCK_GROUPS = 16
VMEM_LIMIT_BYTES = 48 << 20
MM_BLOCK_BUDGET_BYTES = 30 << 20
N_CHIPS = 4
N_CORES = 2
MESH = pl.DeviceIdType.MESH


def _cp(n_grid):
    return pltpu.CompilerParams(dimension_semantics=("arbitrary",) * n_grid, vmem_limit_bytes=VMEM_LIMIT_BYTES)


def _tile(n, pref, mult=SUBLANES):
    if n <= pref:
        return n
    t = (pref // mult) * mult
    while t >= mult:
        if n % t == 0:
            return t
        t -= mult
    return n


class _Comm:
    def __init__(self, ins, out_shapes, sems, start, finish):
        self.ins, self.out_shapes, self.sems, self.start, self.finish = ins, out_shapes, sems, start, finish


def _any_specs(n):
    return [pl.BlockSpec(memory_space=pl.ANY)] * n


def _run_comm(comm, *, name):
    n_in, n_out = len(comm.ins), len(comm.out_shapes)

    def body(*refs):
        ins, outs, sems = refs[:n_in], refs[n_in:n_in + n_out], refs[n_in + n_out:]
        comm.start(ins, outs, sems)
        comm.finish(ins, outs, sems)

    return pl.pallas_call(body, in_specs=_any_specs(n_in), out_specs=_any_specs(n_out),
                          out_shape=list(comm.out_shapes), scratch_shapes=list(comm.sems), name=name)(*comm.ins)


def _call(body, *, grid, in_specs, out_specs, out_shape, args, name, scratch_shapes=(), prefetch=(), comm=None):
    n_pre, n_in, n_out, n_scr = len(prefetch), len(in_specs), len(out_specs), len(scratch_shapes)
    in_specs, out_specs, out_shape = list(in_specs), list(out_specs), list(out_shape)
    scratch_shapes, args = list(scratch_shapes), list(args)
    kernel_body = body
    if comm is not None:
        n_cin, n_cout = len(comm.ins), len(comm.out_shapes)

        def kernel_body(*refs):
            pos = n_pre + n_in
            c_in = refs[pos:pos + n_cin]
            main_out = refs[pos + n_cin:pos + n_cin + n_out]
            pos += n_cin + n_out
            c_out = refs[pos:pos + n_cout]
            main_scr = refs[pos + n_cout:pos + n_cout + n_scr]
            sems = refs[pos + n_cout + n_scr:]
            ids = [pl.program_id(a) for a in range(len(grid))]
            first = functools.reduce(jnp.logical_and, [i == 0 for i in ids])
            last = functools.reduce(jnp.logical_and, [i == g - 1 for i, g in zip(ids, grid)])
            pl.when(first)(lambda: comm.start(c_in, c_out, sems))
            body(*refs[:n_pre + n_in], *main_out, *main_scr)
            pl.when(last)(lambda: comm.finish(c_in, c_out, sems))

        in_specs += _any_specs(n_cin)
        out_specs += _any_specs(n_cout)
        out_shape += list(comm.out_shapes)
        scratch_shapes += list(comm.sems)
        args += list(comm.ins)
    if prefetch:
        spec = pltpu.PrefetchScalarGridSpec(num_scalar_prefetch=n_pre, grid=grid, in_specs=in_specs,
                                            out_specs=out_specs, scratch_shapes=scratch_shapes)
        res = pl.pallas_call(kernel_body, grid_spec=spec, out_shape=out_shape, compiler_params=_cp(len(grid)),
                             name=name)(*prefetch, *args)
    else:
        res = pl.pallas_call(kernel_body, grid=grid, in_specs=in_specs, out_specs=out_specs, out_shape=out_shape,
                             scratch_shapes=scratch_shapes, compiler_params=_cp(len(grid)), name=name)(*args)
    return (res[:n_out], res[n_out:]) if comm is not None else res


def _row_tile(m, bytes_per_row, fixed_bytes):
    for tm in (1024, 512):
        if m % tm == 0 and 2 * (tm * bytes_per_row + fixed_bytes) <= MM_BLOCK_BUDGET_BYTES:
            return tm
    return _tile(m, 512)


def _dot(a, b, ca, cb):
    return lax.dot_general(a, b, (((ca,), (cb,)), ((), ())), preferred_element_type=F32)


def _epilogue_io(epilogue, m, tm, rows_axis, grid_rank):
    _, rows, consts, outs, accs = epilogue

    def at_rows(width):
        return pl.BlockSpec((tm, width), lambda *g: (g[rows_axis], 0))

    def whole(shape):
        return pl.BlockSpec(shape, lambda *g: (0,) * len(shape))

    in_specs = [at_rows(r.shape[1]) for r in rows] + [whole(c.shape) for c in consts]
    out_specs = [at_rows(wd) for wd, _ in outs] + [whole(s) for s in accs]
    out_shape = ([jax.ShapeDtypeStruct((m, wd), dt) for wd, dt in outs]
                 + [jax.ShapeDtypeStruct(s, F32) for s in accs])
    bytes_per_row = (sum(r.shape[1] * r.dtype.itemsize for r in rows)
                     + sum(wd * jnp.dtype(dt).itemsize for wd, dt in outs))
    return in_specs, out_specs, out_shape, bytes_per_row


def _epilogue_apply(epilogue, block, refs, first_row_tile):
    fn, rows, consts, outs, _ = epilogue
    n_in, n_out = len(rows) + len(consts), len(outs)
    res = fn(block, *[r[...] for r in refs[:n_in]])
    for o, val in zip(refs[n_in:n_in + n_out], res[:n_out]):
        o[...] = val.astype(o.dtype)
    a_refs = refs[n_in + n_out:]
    if a_refs:
        @pl.when(first_row_tile)
        def _():
            for a in a_refs:
                a[...] = jnp.zeros_like(a)
        for a, val in zip(a_refs, res[n_out:]):
            a[...] += val


def _mm_cols(x, w4, layer, *, wc, out_dtype=F32, scale=None, epilogue=None, name):
    m, k = x.shape
    slots, _, k0, k1 = w4.shape
    nb = k1 if wc == 0 else k0
    assert (k0 if wc == 0 else k1) == k
    if epilogue is None:
        tm = _row_tile(m, k * x.dtype.itemsize + nb * jnp.dtype(out_dtype).itemsize, k0 * k1 * w4.dtype.itemsize)
        extra_in, out_specs = [], pl.BlockSpec((tm, nb), lambda s, i: (i, s))
        out_shape = jax.ShapeDtypeStruct((m, slots * nb), out_dtype)
    else:
        assert slots == 1
        bytes_per_row = _epilogue_io(epilogue, m, SUBLANES, 1, 2)[3]
        tm = _row_tile(m, k * x.dtype.itemsize + bytes_per_row, k0 * k1 * w4.dtype.itemsize)
        extra_in, out_specs, out_shape, _ = _epilogue_io(epilogue, m, tm, 1, 2)

    def body(x_ref, w_ref, *refs):
        acc = _dot(x_ref[...].astype(BF16), w_ref[0, 0], 1, wc)
        if scale is not None:
            acc = acc * scale
        if epilogue is None:
            refs[0][...] = acc.astype(out_dtype)
        else:
            _epilogue_apply(epilogue, acc, refs, pl.program_id(1) == 0)

    extra_args = [] if epilogue is None else [*epilogue[1], *epilogue[2]]
    return pl.pallas_call(
        body, grid=(slots, m // tm),
        in_specs=[pl.BlockSpec((tm, k), lambda s, i: (i, 0)),
                  pl.BlockSpec((1, 1, k0, k1), lambda s, i: (s, layer, 0, 0)), *extra_in],
        out_specs=out_specs, out_shape=out_shape,
        compiler_params=_cp(2), name=name)(x, w4, *extra_args)


def _planes(a):
    return a if a.ndim == 3 else a[None]


def _mm_acc(x, w4, layer, *, wc, epilogue=None, name):
    x = _planes(x)
    n_planes, m, width = x.shape
    slots, _, k0, k1 = w4.shape
    kb = k0 if wc == 0 else k1
    nout = k1 if wc == 0 else k0
    assert n_planes * width == slots * kb
    spp = slots // n_planes
    x_spec_w = pl.BlockSpec((1, 1, k0, k1), lambda i, s: (s, layer, 0, 0))
    if epilogue is None:
        tm = _row_tile(m, kb * x.dtype.itemsize + nout * 4, k0 * k1 * w4.dtype.itemsize)

        def body(x_ref, w_ref, o_ref):
            @pl.when(pl.program_id(1) == 0)
            def _():
                o_ref[...] = jnp.zeros_like(o_ref)
            o_ref[...] += _dot(x_ref[0].astype(BF16), w_ref[0, 0], 1, wc)

        return pl.pallas_call(
            body, grid=(m // tm, slots),
            in_specs=[pl.BlockSpec((1, tm, kb), lambda i, s: (s // spp, i, s % spp)), x_spec_w],
            out_specs=pl.BlockSpec((tm, nout), lambda i, s: (i, 0)),
            out_shape=jax.ShapeDtypeStruct((m, nout), F32),
            compiler_params=_cp(2), name=name)(x, w4)

    bytes_per_row = _epilogue_io(epilogue, m, SUBLANES, 0, 2)[3]
    tm = _row_tile(m, kb * x.dtype.itemsize + nout * 2 + bytes_per_row, k0 * k1 * w4.dtype.itemsize)
    extra_in, out_specs, out_shape, _ = _epilogue_io(epilogue, m, tm, 0, 2)

    def body(x_ref, w_ref, *refs):
        acc = refs[-1]

        @pl.when(pl.program_id(1) == 0)
        def _():
            acc[...] = jnp.zeros_like(acc)
        acc[...] += _dot(x_ref[0].astype(BF16), w_ref[0, 0], 1, wc)

        @pl.when(pl.program_id(1) == slots - 1)
        def _():
            _epilogue_apply(epilogue, acc[...], refs[:-1], pl.program_id(0) == 0)

    return pl.pallas_call(
        body, grid=(m // tm, slots),
        in_specs=[pl.BlockSpec((1, tm, kb), lambda i, s: (s // spp, i, s % spp)), x_spec_w, *extra_in],
        out_specs=out_specs, out_shape=out_shape, scratch_shapes=[pltpu.VMEM((tm, nout), F32)],
        compiler_params=_cp(2), name=name)(x, w4, *epilogue[1], *epilogue[2])


def _mm_tn(x, dy, slots, *, scale=None, wire=False, name):
    m, k = x.shape
    dy = _planes(dy)
    n_planes, _, width = dy.shape
    n = n_planes * width // slots
    spp = slots // n_planes
    ta = _tile(k, 512, LANES)
    tm = m
    while tm > 512 and tm % 2 == 0 and (2 * tm * (ta * x.dtype.itemsize + n * dy.dtype.itemsize)
                                         + 2 * ta * n * 4) > MM_BLOCK_BUDGET_BYTES:
        tm //= 2
    n_m = m // tm

    def body(x_ref, dy_ref, o_ref, *wire_ref):
        @pl.when(pl.program_id(2) == 0)
        def _():
            o_ref[...] = jnp.zeros_like(o_ref)
        o_ref[0] += _dot(x_ref[...].astype(BF16), dy_ref[0].astype(BF16), 0, 0)
        if scale is not None or wire:
            @pl.when(pl.program_id(2) == n_m - 1)
            def _():
                if scale is not None:
                    o_ref[...] = o_ref[...] * scale
                if wire:
                    wire_ref[0][...] = o_ref[...].astype(BF16)

    out_spec = pl.BlockSpec((1, ta, n), lambda s, a, i: (s, a, 0))
    return pl.pallas_call(
        body, grid=(slots, k // ta, n_m),
        in_specs=[pl.BlockSpec((tm, ta), lambda s, a, i: (i, a)),
                  pl.BlockSpec((1, tm, n), lambda s, a, i: (s // spp, i, s % spp))],
        out_specs=[out_spec, out_spec] if wire else out_spec,
        out_shape=([jax.ShapeDtypeStruct((slots, k, n), dt) for dt in (F32, BF16)] if wire
                   else jax.ShapeDtypeStruct((slots, k, n), F32)),
        compiler_params=_cp(3), name=name)(x, dy)


def _rowwise(fn, rows, consts, outs, accs=(), *, tl=256, name):
    n_rows = rows[0].shape[0]
    tl = _tile(n_rows, tl)
    n_in = len(rows) + len(consts)
    n_out = len(outs)

    def body(*refs):
        res = fn(*[r[...] for r in refs[:n_in]])
        res = res if isinstance(res, (tuple, list)) else (res,)
        o_refs = refs[n_in:n_in + n_out]
        a_refs = refs[n_in + n_out:]
        for o, val in zip(o_refs, res[:n_out]):
            o[...] = val.astype(o.dtype)
        if a_refs:
            @pl.when(pl.program_id(0) == 0)
            def _():
                for a in a_refs:
                    a[...] = jnp.zeros_like(a)
            for a, val in zip(a_refs, res[n_out:]):
                a[...] += val

    in_specs = ([pl.BlockSpec((tl, r.shape[1]), lambda i: (i, 0)) for r in rows]
                + [pl.BlockSpec(c.shape, lambda i: (0, 0)) for c in consts])
    out_specs = ([pl.BlockSpec((tl, w), lambda i: (i, 0)) for w, _ in outs]
                 + [pl.BlockSpec(s, lambda i: (0, 0)) for s in accs])
    out_shape = ([jax.ShapeDtypeStruct((n_rows, w), dt) for w, dt in outs]
                 + [jax.ShapeDtypeStruct(s, F32) for s in accs])
    return pl.pallas_call(body, grid=(n_rows // tl,), in_specs=in_specs, out_specs=out_specs,
                          out_shape=out_shape, compiler_params=_cp(1), name=name)(*rows, *consts)


def _rms(x, g):
    return x * lax.rsqrt(jnp.mean(x * x, axis=-1, keepdims=True) + RMS_EPS) * g


def _sigmoid(x):
    return 1.0 / (1.0 + jnp.exp(-x))


def _glu(zz):
    d = zz.shape[1] // 2
    return zz[:, :d] * _sigmoid(zz[:, d:])


def _gelu(y):
    return jax.nn.gelu(y)


def _row2(v):
    return v.reshape(1, -1)


def _node_bwd_fn(counts):
    n_dy = sum(counts)

    def fn(d, hh, *rest):
        dys, gs = rest[:n_dy], rest[n_dy:]
        tot, dgs, pos = d, [], 0
        for g, cnt in zip(gs, counts):
            dy = dys[pos].astype(F32)
            for extra in dys[pos + 1:pos + cnt]:
                dy = dy + extra.astype(F32)
            pos += cnt
            _, vjp = jax.vjp(_rms, hh, g)
            dx, dg = vjp(dy)
            tot = tot + dx
            dgs.append(dg)
        return (tot, tot, *dgs)

    return fn


def _node_bwd(d_in, h, branches, *, name):
    width = h.shape[1]
    flat = [dy for _, dys in branches for dy in dys]
    res = _rowwise(_node_bwd_fn([len(dys) for _, dys in branches]), [d_in, h, *flat],
                   [_row2(g) for g, _ in branches], [(width, F32), (width, BF16)], [(1, width)] * len(branches),
                   name=name)
    return res[0], res[1], [r[0] for r in res[2:]]


def _s5_prep_fn(lr, li, ldt, br, bi, cr, ci, *, gq, h, p):
    dt = jnp.exp(ldt)
    mag = jnp.exp(lr * dt)
    lb_re = mag * jnp.cos(li * dt)
    lb_im = mag * jnp.sin(li * dt)
    den = lr * lr + li * li
    nr = lb_re - 1.0
    fr = (nr * lr + lb_im * li) / den
    fi = (lb_im * lr - nr * li) / den
    bb_re = fr * br - fi * bi
    bb_im = fr * bi + fi * br
    shape = (gq * h, gq * p)
    r = lax.broadcasted_iota(jnp.int32, shape, 0)
    c = lax.broadcasted_iota(jnp.int32, shape, 1)
    mask = jnp.where(jnp.right_shift(r, h.bit_length() - 1) == jnp.right_shift(c, p.bit_length() - 1), 1.0, 0.0)

    def expand(t):
        return jnp.concatenate([t] * gq, axis=0) * mask

    return lb_re, lb_im, expand(bb_re), expand(bb_im), expand(cr), expand(ci)


def _s5_prep(lr, li, ldt, br, bi, cr, ci, p, *, name):
    n = lr.shape[1]
    h = br.shape[0]
    gq = S5_BLOCK_GROUPS
    nq, cq = gq * p, gq * h
    nblk = n // nq
    fn = functools.partial(_s5_prep_fn, gq=gq, h=h, p=p)

    def body(lr_r, li_r, ldt_r, br_r, bi_r, cr_r, ci_r, lbr_o, lbi_o, wbr_o, wbi_o, wcr_o, wci_o):
        lb_re, lb_im, wbr, wbi, wcr, wci = fn(lr_r[...], li_r[...], ldt_r[...], br_r[...], bi_r[...],
                                              cr_r[...], ci_r[...])
        lbr_o[...] = lb_re
        lbi_o[...] = lb_im
        wbr_o[0] = wbr.astype(BF16)
        wbi_o[0] = wbi.astype(BF16)
        wcr_o[0] = wcr.astype(BF16)
        wci_o[0] = wci.astype(BF16)

    vec = pl.BlockSpec((1, nq), lambda q: (0, q))
    tab = pl.BlockSpec((h, nq), lambda q: (0, q))
    wsp = pl.BlockSpec((1, cq, nq), lambda q: (q, 0, 0))
    wsh = jax.ShapeDtypeStruct((nblk, cq, nq), BF16)
    vsh = jax.ShapeDtypeStruct((1, n), F32)
    return pl.pallas_call(body, grid=(nblk,), in_specs=[vec, vec, vec, tab, tab, tab, tab],
                          out_specs=[vec, vec, wsp, wsp, wsp, wsp], out_shape=[vsh, vsh, wsh, wsh, wsh, wsh],
                          compiler_params=_cp(1), name=name)(lr, li, ldt, br, bi, cr, ci)


def _s5_prep_bwd(lr, li, ldt, br, bi, cr, ci, p, dlbr, dlbi, dwbr, dwbi, dwcr, dwci, *, name):
    n = lr.shape[1]
    h = br.shape[0]
    gq = S5_BLOCK_GROUPS
    nq, cq = gq * p, gq * h
    nblk = n // nq
    fn = functools.partial(_s5_prep_fn, gq=gq, h=h, p=p)

    def body(lr_r, li_r, ldt_r, br_r, bi_r, cr_r, ci_r, dlbr_r, dlbi_r, dwbr_r, dwbi_r, dwcr_r, dwci_r,
             *outs):
        _, vjp = jax.vjp(fn, lr_r[...], li_r[...], ldt_r[...], br_r[...], bi_r[...], cr_r[...], ci_r[...])
        grads = vjp((dlbr_r[0], dlbi_r[0], dwbr_r[0], dwbi_r[0], dwcr_r[0], dwci_r[0]))
        for o, g in zip(outs, grads):
            o[...] = g

    vec = pl.BlockSpec((1, nq), lambda q: (0, q))
    tab = pl.BlockSpec((h, nq), lambda q: (0, q))
    vec3 = pl.BlockSpec((1, 1, nq), lambda q: (q, 0, 0))
    wsp = pl.BlockSpec((1, cq, nq), lambda q: (q, 0, 0))
    vsh = jax.ShapeDtypeStruct((1, n), F32)
    tsh = jax.ShapeDtypeStruct((h, n), F32)
    return pl.pallas_call(body, grid=(nblk,),
                          in_specs=[vec, vec, vec, tab, tab, tab, tab, vec3, vec3, wsp, wsp, wsp, wsp],
                          out_specs=[vec, vec, vec, tab, tab, tab, tab],
                          out_shape=[vsh, vsh, vsh, tsh, tsh, tsh, tsh],
                          compiler_params=_cp(1), name=name)(lr, li, ldt, br, bi, cr, ci,
                                                             dlbr, dlbi, dwbr, dwbi, dwcr, dwci)


def _scan_rows(s_re, s_im, a_re, a_im, c_re, c_im, *, reverse):
    t_rows, n = s_re.shape
    nb = t_rows // SUBLANES
    row = lax.broadcasted_iota(jnp.int32, (SUBLANES, n), 0)

    def cmul(x, y):
        return x[0] * y[0] - x[1] * y[1], x[0] * y[1] + x[1] * y[0]

    a1 = (jnp.broadcast_to(a_re, (SUBLANES, n)), jnp.broadcast_to(a_im, (SUBLANES, n)))
    a2 = cmul(a1, a1)
    a4 = cmul(a2, a2)
    steps = []
    for dist, (pr, pi) in ((1, a1), (2, a2), (4, a4)):
        keep = (row < SUBLANES - dist) if reverse else (row >= dist)
        steps.append((SUBLANES - dist if reverse else dist, (jnp.where(keep, pr, 0.0), jnp.where(keep, pi, 0.0))))
    pk = (a_re, a_im)
    tab_re = jnp.zeros((SUBLANES, n), F32)
    tab_im = jnp.zeros((SUBLANES, n), F32)
    for i in range(SUBLANES):
        at = (SUBLANES - 1 - i) if reverse else i
        tab_re = jnp.where(row == at, pk[0], tab_re)
        tab_im = jnp.where(row == at, pk[1], tab_im)
        pk = cmul(pk, (a_re, a_im))

    def step(b, carry):
        cr, ci = carry
        blk = (nb - 1 - b) if reverse else b
        off = pl.multiple_of(blk * SUBLANES, SUBLANES)
        x_re = s_re[pl.ds(off, SUBLANES), :]
        x_im = s_im[pl.ds(off, SUBLANES), :]
        for sh, (pr, pi) in steps:
            sh_re = pltpu.roll(x_re, sh, 0)
            sh_im = pltpu.roll(x_im, sh, 0)
            x_re, x_im = x_re + pr * sh_re - pi * sh_im, x_im + pr * sh_im + pi * sh_re
        x_re, x_im = x_re + tab_re * cr - tab_im * ci, x_im + tab_re * ci + tab_im * cr
        s_re[pl.ds(off, SUBLANES), :] = x_re
        s_im[pl.ds(off, SUBLANES), :] = x_im
        edge = 0 if reverse else SUBLANES - 1
        return x_re[edge:edge + 1, :], x_im[edge:edge + 1, :]

    return lax.fori_loop(0, nb, step, (c_re, c_im))


def _s5_fwd(u, prep, dskip, *, name, comm=None):
    lb_re, lb_im, wbr, wbi, wcr, wci = prep
    n_rows, _ = u.shape
    nblk, cq, nq = wbr.shape
    tt = _tile(n_rows, S5_TIME_CHUNK)
    nch = n_rows // tt

    def body(u_ref, wbr_r, wbi_r, wcr_r, wci_r, lbr_r, lbi_r, d_ref, y_ref, z_ref, s_re, s_im, sbr_o, sbi_o,
             c_re, c_im):
        @pl.when(pl.program_id(1) == 0)
        def _():
            c_re[...] = jnp.zeros_like(c_re)
            c_im[...] = jnp.zeros_like(c_im)
        uf = u_ref[...]
        ub = uf.astype(BF16)
        s_re[...] = _dot(ub, wbr_r[0], 1, 0)
        s_im[...] = _dot(ub, wbi_r[0], 1, 0)
        sbr_o[0] = c_re[...]
        sbi_o[0] = c_im[...]
        cr, ci = _scan_rows(s_re, s_im, lbr_r[...], lbi_r[...], c_re[...], c_im[...], reverse=False)
        c_re[...] = cr
        c_im[...] = ci
        y = _dot(s_re[...].astype(BF16), wcr_r[0], 1, 1) - _dot(s_im[...].astype(BF16), wci_r[0], 1, 1)
        y = y + d_ref[...] * uf
        y_ref[...] = y
        z_ref[...] = _gelu(y).astype(BF16)

    wsp = pl.BlockSpec((1, cq, nq), lambda q, i: (q, 0, 0))
    vec = pl.BlockSpec((1, nq), lambda q, i: (0, q))
    act = pl.BlockSpec((tt, cq), lambda q, i: (i, q))
    sb = pl.BlockSpec((1, 1, nq), lambda q, i: (i, 0, q))
    sbsh = jax.ShapeDtypeStruct((nch, 1, nblk * nq), F32)
    states = pl.BlockSpec((tt, nq), lambda q, i: (i, q))
    stsh = jax.ShapeDtypeStruct((n_rows, nblk * nq), F32)
    return _call(
        body, grid=(nblk, nch),
        in_specs=[act, wsp, wsp, wsp, wsp, vec, vec, pl.BlockSpec((1, cq), lambda q, i: (0, q))],
        out_specs=[act, act, states, states, sb, sb],
        out_shape=[jax.ShapeDtypeStruct(u.shape, F32), jax.ShapeDtypeStruct(u.shape, BF16), stsh, stsh, sbsh, sbsh],
        scratch_shapes=[pltpu.VMEM((1, nq), F32), pltpu.VMEM((1, nq), F32)],
        args=(u, wbr, wbi, wcr, wci, lb_re, lb_im, dskip), name=name, comm=comm)


def _s5_bwd(u, dy, st_re, st_im, sb_re, sb_im, prep, dskip, *, name, comm=None):
    lb_re, lb_im, wbr, wbi, wcr, wci = prep
    n_rows, _ = u.shape
    nblk, cq, nq = wbr.shape
    tt = _tile(n_rows, S5_TIME_CHUNK)
    nch = n_rows // tt

    def body(u_ref, dy_ref, s_re, s_im, sbr_r, sbi_r, wbr_r, wbi_r, wcr_r, wci_r, lbr_r, lbi_r, d_ref,
             du_ref, dwbr, dwbi, dwcr, dwci, dlbr, dlbi, dd_ref, g_re, g_im, lc_re, lc_im):
        @pl.when(pl.program_id(1) == 0)
        def _():
            for ref in (lc_re, lc_im, dwbr, dwbi, dwcr, dwci, dlbr, dlbi, dd_ref):
                ref[...] = jnp.zeros_like(ref)
        uf = u_ref[...]
        ub = uf.astype(BF16)
        dyf = dy_ref[...]
        dyb = dyf.astype(BF16)
        sr16 = s_re[...].astype(BF16)
        si16 = s_im[...].astype(BF16)
        dwcr[0] += _dot(dyb, sr16, 0, 0)
        dwci[0] -= _dot(dyb, si16, 0, 0)
        g_re[...] = _dot(dyb, wcr_r[0], 1, 0)
        g_im[...] = -_dot(dyb, wci_r[0], 1, 0)
        lcr, lci = _scan_rows(g_re, g_im, lbr_r[...], -lbi_r[...], lc_re[...], lc_im[...], reverse=True)
        lc_re[...] = lcr
        lc_im[...] = lci
        lam_r = g_re[...]
        lam_i = g_im[...]
        first = lax.broadcasted_iota(jnp.int32, (tt, nq), 0) == 0
        prev_r = jnp.where(first, sbr_r[0], pltpu.roll(s_re[...], 1, 0))
        prev_i = jnp.where(first, sbi_r[0], pltpu.roll(s_im[...], 1, 0))
        dlbr[0] += jnp.sum(lam_r * prev_r + lam_i * prev_i, axis=0, keepdims=True)
        dlbi[0] += jnp.sum(lam_i * prev_r - lam_r * prev_i, axis=0, keepdims=True)
        lr16 = lam_r.astype(BF16)
        li16 = lam_i.astype(BF16)
        du_ref[...] = _dot(lr16, wbr_r[0], 1, 1) + _dot(li16, wbi_r[0], 1, 1) + d_ref[...] * dyf
        dwbr[0] += _dot(ub, lr16, 0, 0)
        dwbi[0] += _dot(ub, li16, 0, 0)
        dd_ref[0] += jnp.sum(dyf * uf, axis=0, keepdims=True)

    last = nch - 1
    wsp = pl.BlockSpec((1, cq, nq), lambda q, i: (q, 0, 0))
    vec = pl.BlockSpec((1, nq), lambda q, i: (0, q))
    act = pl.BlockSpec((tt, cq), lambda q, i: (last - i, q))
    sb = pl.BlockSpec((1, 1, nq), lambda q, i: (last - i, 0, q))
    vec3 = pl.BlockSpec((1, 1, nq), lambda q, i: (q, 0, 0))
    dsp = pl.BlockSpec((1, 1, cq), lambda q, i: (q, 0, 0))
    wsh = jax.ShapeDtypeStruct((nblk, cq, nq), F32)
    v3sh = jax.ShapeDtypeStruct((nblk, 1, nq), F32)
    big = pltpu.VMEM((tt, nq), F32)
    states = pl.BlockSpec((tt, nq), lambda q, i: (last - i, q))
    return _call(
        body, grid=(nblk, nch),
        in_specs=[act, act, states, states, sb, sb, wsp, wsp, wsp, wsp, vec, vec,
                  pl.BlockSpec((1, cq), lambda q, i: (0, q))],
        out_specs=[act, wsp, wsp, wsp, wsp, vec3, vec3, dsp],
        out_shape=[jax.ShapeDtypeStruct(u.shape, F32), wsh, wsh, wsh, wsh, v3sh, v3sh,
                   jax.ShapeDtypeStruct((nblk, 1, cq), F32)],
        scratch_shapes=[big, big, pltpu.VMEM((1, nq), F32), pltpu.VMEM((1, nq), F32)],
        args=(u, dy, st_re, st_im, sb_re, sb_im, wbr, wbi, wcr, wci, lb_re, lb_im, dskip), name=name, comm=comm)


def _conv_taps(cur, prev, w, b):
    ext = jnp.concatenate([prev, cur], axis=0)
    x1 = pltpu.roll(ext, 1, 0)[SUBLANES:, :]
    x2 = pltpu.roll(ext, 2, 0)[SUBLANES:, :]
    return b + x2 * w[0:1, :] + x1 * w[1:2, :] + cur * w[2:3, :], x1, x2


def _conv_fwd(uu, cw, cb, *, name):
    n_rows, f2 = uu.shape
    f = f2 // 2
    tc = _tile(f, 1408, LANES)
    tl = _tile(n_rows, 256)
    nfb = f // tc

    def body(g_ref, u_ref, wg_ref, wu_ref, bg_ref, bu_ref, o_ref, pg, pu):
        @pl.when(pl.program_id(1) == 0)
        def _():
            pg[...] = jnp.zeros_like(pg)
            pu[...] = jnp.zeros_like(pu)
        gcur = g_ref[...]
        ucur = u_ref[...]
        cg, _, _ = _conv_taps(gcur, pg[...], wg_ref[...], bg_ref[...])
        cu, _, _ = _conv_taps(ucur, pu[...], wu_ref[...], bu_ref[...])
        o_ref[...] = (cg * _sigmoid(cg) * cu).astype(o_ref.dtype)
        pg[...] = gcur[tl - SUBLANES:, :]
        pu[...] = ucur[tl - SUBLANES:, :]

    return pl.pallas_call(
        body, grid=(nfb, n_rows // tl),
        in_specs=[pl.BlockSpec((tl, tc), lambda j, i: (i, j)), pl.BlockSpec((tl, tc), lambda j, i: (i, j + nfb)),
                  pl.BlockSpec((CONV_TAPS, tc), lambda j, i: (0, j)),
                  pl.BlockSpec((CONV_TAPS, tc), lambda j, i: (0, j + nfb)),
                  pl.BlockSpec((1, tc), lambda j, i: (0, j)), pl.BlockSpec((1, tc), lambda j, i: (0, j + nfb))],
        out_specs=pl.BlockSpec((tl, tc), lambda j, i: (i, j)),
        out_shape=jax.ShapeDtypeStruct((n_rows, f), BF16),
        scratch_shapes=[pltpu.VMEM((SUBLANES, tc), F32), pltpu.VMEM((SUBLANES, tc), F32)],
        compiler_params=_cp(2), name=name)(uu, uu, cw, cw, cb, cb)


def _conv_bwd(uu, dact, cw, cb, *, name):
    n_rows, f2 = uu.shape
    f = f2 // 2
    tc = _tile(f, 1408, LANES)
    tl = _tile(n_rows, 256)
    nfb = f // tc
    nrb = n_rows // tl
    halo_per_tile = tl // SUBLANES

    def body(g_ref, gh_ref, u_ref, uh_ref, da_ref, wg_ref, wu_ref, bg_ref, bu_ref,
             duu_ref, dw_ref, db_ref, nxt_g, nxt_u):
        i = pl.program_id(1)
        rb = nrb - 1 - i

        @pl.when(i == 0)
        def _():
            for ref in (nxt_g, nxt_u, dw_ref, db_ref):
                ref[...] = jnp.zeros_like(ref)
        has_prev = jnp.where(rb > 0, 1.0, 0.0)
        gcur, ucur = g_ref[...], u_ref[...]
        wg, wu = wg_ref[...], wu_ref[...]
        cg, g1, g2 = _conv_taps(gcur, gh_ref[...] * has_prev, wg, bg_ref[...])
        cu, u1, u2 = _conv_taps(ucur, uh_ref[...] * has_prev, wu, bu_ref[...])
        sg = _sigmoid(cg)
        silu = cg * sg
        da = da_ref[...]

        def transpose_conv(plane, d, cur, x1, x2, w, nxt):
            ext = jnp.concatenate([d, nxt[...]], axis=0)
            d1 = pltpu.roll(ext, tl + SUBLANES - 1, 0)[:tl, :]
            d2 = pltpu.roll(ext, tl + SUBLANES - 2, 0)[:tl, :]
            duu_ref[plane] = (w[2:3, :] * d + w[1:2, :] * d1 + w[0:1, :] * d2).astype(duu_ref.dtype)
            nxt[...] = d[0:SUBLANES, :]
            dw_ref[plane] += jnp.concatenate([jnp.sum(d * x2, axis=0, keepdims=True),
                                              jnp.sum(d * x1, axis=0, keepdims=True),
                                              jnp.sum(d * cur, axis=0, keepdims=True)], axis=0)
            db_ref[plane] += jnp.sum(d, axis=0, keepdims=True)

        transpose_conv(0, da * cu * (sg * (1.0 + cg * (1.0 - sg))), gcur, g1, g2, wg, nxt_g)
        transpose_conv(1, da * silu, ucur, u1, u2, wu, nxt_u)

    def halo(j, i):
        return jnp.maximum((nrb - 1 - i) * halo_per_tile - 1, 0)

    return pl.pallas_call(
        body, grid=(nfb, nrb),
        in_specs=[pl.BlockSpec((tl, tc), lambda j, i: (nrb - 1 - i, j)),
                  pl.BlockSpec((SUBLANES, tc), lambda j, i: (halo(j, i), j)),
                  pl.BlockSpec((tl, tc), lambda j, i: (nrb - 1 - i, j + nfb)),
                  pl.BlockSpec((SUBLANES, tc), lambda j, i: (halo(j, i), j + nfb)),
                  pl.BlockSpec((tl, tc), lambda j, i: (nrb - 1 - i, j)),
                  pl.BlockSpec((CONV_TAPS, tc), lambda j, i: (0, j)),
                  pl.BlockSpec((CONV_TAPS, tc), lambda j, i: (0, j + nfb)),
                  pl.BlockSpec((1, tc), lambda j, i: (0, j)),
                  pl.BlockSpec((1, tc), lambda j, i: (0, j + nfb))],
        out_specs=[pl.BlockSpec((2, tl, tc), lambda j, i: (0, nrb - 1 - i, j)),
                   pl.BlockSpec((2, CONV_TAPS, tc), lambda j, i: (0, 0, j)),
                   pl.BlockSpec((2, 1, tc), lambda j, i: (0, 0, j))],
        out_shape=[jax.ShapeDtypeStruct((2, n_rows, f), BF16), jax.ShapeDtypeStruct((2, CONV_TAPS, f), F32),
                   jax.ShapeDtypeStruct((2, 1, f), F32)],
        scratch_shapes=[pltpu.VMEM((SUBLANES, tc), F32), pltpu.VMEM((SUBLANES, tc), F32)],
        compiler_params=_cp(2), name=name)(uu, uu, uu, uu, dact, cw, cw, cb, cb)


def _log_sigmoid(x):
    t = jnp.exp(-jnp.abs(x))
    log1p_t = jnp.where(t < 1e-3, t * (1.0 - t * (0.5 - t * (1.0 / 3.0))), jnp.log(1.0 + t))
    return jnp.minimum(x, 0.0) - log1p_t


def _dlog_sigmoid(x):
    t = jnp.exp(-jnp.abs(x))
    return jnp.where(x >= 0, t, 1.0) / (1.0 + t)


def _tri_dot(tri, x):
    return jnp.dot(tri, x, precision=lax.Precision.HIGHEST, preferred_element_type=F32)


def _cum_fwd(fl, bf, *, name):
    n_rows, width = fl.shape
    tc = _tile(n_rows, 256)

    def body(fl_ref, bf_ref, o_ref, carry):
        @pl.when(pl.program_id(0) == 0)
        def _():
            carry[...] = jnp.zeros_like(carry)
        x = _log_sigmoid(fl_ref[...] + bf_ref[...])
        r = lax.broadcasted_iota(jnp.int32, (tc, tc), 0)
        c = lax.broadcasted_iota(jnp.int32, (tc, tc), 1)
        y = _tri_dot(jnp.where(r >= c, 1.0, 0.0), x) + carry[...]
        o_ref[...] = y
        carry[...] = y[tc - 1:tc, :]

    return pl.pallas_call(
        body, grid=(n_rows // tc,),
        in_specs=[pl.BlockSpec((tc, width), lambda i: (i, 0)), pl.BlockSpec((1, width), lambda i: (0, 0))],
        out_specs=pl.BlockSpec((tc, width), lambda i: (i, 0)),
        out_shape=jax.ShapeDtypeStruct(fl.shape, F32),
        scratch_shapes=[pltpu.VMEM((1, width), F32)], compiler_params=_cp(1), name=name)(fl, bf)


def _cum_bwd(dcum, fl, bf, *, name):
    n_rows, width = fl.shape
    tc = _tile(n_rows, 256)
    last = n_rows // tc - 1

    def body(dc_ref, fl_ref, bf_ref, dfl_ref, dbf_ref, carry):
        @pl.when(pl.program_id(0) == 0)
        def _():
            carry[...] = jnp.zeros_like(carry)
            dbf_ref[...] = jnp.zeros_like(dbf_ref)
        r = lax.broadcasted_iota(jnp.int32, (tc, tc), 0)
        c = lax.broadcasted_iota(jnp.int32, (tc, tc), 1)
        dls = _tri_dot(jnp.where(r <= c, 1.0, 0.0), dc_ref[...]) + carry[...]
        carry[...] = dls[0:1, :]
        dfl = dls * _dlog_sigmoid(fl_ref[...] + bf_ref[...])
        dfl_ref[...] = dfl.astype(dfl_ref.dtype)
        dbf_ref[...] += jnp.sum(dfl, axis=0, keepdims=True)

    return pl.pallas_call(
        body, grid=(n_rows // tc,),
        in_specs=[pl.BlockSpec((tc, width), lambda i: (last - i, 0)),
                  pl.BlockSpec((tc, width), lambda i: (last - i, 0)),
                  pl.BlockSpec((1, width), lambda i: (0, 0))],
        out_specs=[pl.BlockSpec((tc, width), lambda i: (last - i, 0)), pl.BlockSpec((1, width), lambda i: (0, 0))],
        out_shape=[jax.ShapeDtypeStruct(fl.shape, BF16), jax.ShapeDtypeStruct((1, width), F32)],
        scratch_shapes=[pltpu.VMEM((1, width), F32)], compiler_params=_cp(1), name=name)(dcum, fl, bf)


def _head_masks():
    lane = lax.broadcasted_iota(jnp.int32, (1, LANES), 1)
    return (lane < HEAD_DIM, lane >= HEAD_DIM)


def _flash_fwd(q, kv, cum_c, cum_r, *, tq, name, comm=None):
    n_rows, d = q.shape
    nhp = d // LANES
    tk = tq
    nq = n_rows // tq
    rt = _tile(tk, FLASH_ROW_TILE)
    reps = (1, tq // LANES)

    def body(qi_ref, kj_ref, q_ref, k_ref, v_ref, cq_ref, ck_ref, ot_ref, lse_ref, m0, m1, l0, l1, acc,
             s0, s1, p0, p1, b0, b1):
        i = qi_ref[pl.program_id(1)]
        j = kj_ref[pl.program_id(1)]
        ms, ls = (m0, m1), (l0, l1)
        head_rows = lax.broadcasted_iota(jnp.int32, (LANES, 1), 0) < HEAD_DIM

        @pl.when(j == 0)
        def _():
            for h in range(2):
                ms[h][...] = jnp.full_like(ms[h], -jnp.inf)
                ls[h][...] = jnp.zeros_like(ls[h])
            acc[...] = jnp.zeros_like(acc)

        def block(diagonal):
            qv, kk, vv = q_ref[...], k_ref[...], v_ref[...]
            a = acc[...]
            for h, msk in enumerate(_head_masks()):
                st_sc, pt_sc, bias_sc = ((s0, p0, b0), (s1, p1, b1))[h]
                st_sc[...] = _dot(kk, jnp.where(msk, qv, jnp.zeros_like(qv)), 1, 1)
                bias_sc[...] = jnp.broadcast_to(cq_ref[0, h:h + 1, 0:1] - ck_ref[0, :, h:h + 1], (tk, LANES))
                m_old, l_old = ms[h][...], ls[h][...]
                col_max = jnp.full((SUBLANES, tq), -jnp.inf, F32)
                for r in range(tk // rt):
                    rows = slice(r * rt, (r + 1) * rt)
                    s = st_sc[rows, :] + jnp.tile(bias_sc[rows, :], reps)
                    if diagonal:
                        key = r * rt + lax.broadcasted_iota(jnp.int32, (rt, tq), 0)
                        qry = lax.broadcasted_iota(jnp.int32, (rt, tq), 1)
                        s = jnp.where(key <= qry, s, -jnp.inf)
                    st_sc[rows, :] = s
                    for g in range(rt // SUBLANES):
                        col_max = jnp.maximum(col_max, s[g * SUBLANES:(g + 1) * SUBLANES, :])
                m_new = jnp.maximum(m_old, jnp.max(col_max, axis=0, keepdims=True))
                col_sum = jnp.zeros((SUBLANES, tq), F32)
                for r in range(tk // rt):
                    rows = slice(r * rt, (r + 1) * rt)
                    p = jnp.exp(st_sc[rows, :] - m_new)
                    for g in range(rt // SUBLANES):
                        col_sum = col_sum + p[g * SUBLANES:(g + 1) * SUBLANES, :]
                    pt_sc[rows, :] = p.astype(BF16)
                alpha = jnp.exp(m_old - m_new)
                ms[h][...] = m_new
                ls[h][...] = alpha * l_old + jnp.sum(col_sum, axis=0, keepdims=True)
                pv_t = _dot(jnp.where(msk, vv, jnp.zeros_like(vv)), pt_sc[...], 0, 0)
                a = a * jnp.where(head_rows == (h == 0), alpha, 1.0) + pv_t
            acc[...] = a

        pl.when(j < i)(functools.partial(block, False))
        pl.when(j == i)(functools.partial(block, True))

        @pl.when(j == i)
        def _():
            ot_ref[...] = (acc[...] * jnp.where(head_rows, 1.0 / l0[...], 1.0 / l1[...])).T
            lse_ref[0] = jnp.concatenate([m0[...] + jnp.log(l0[...]), m1[...] + jnp.log(l1[...])], axis=0)

    pairs = [(i, j) for i in range(nq) for j in range(i + 1)]
    qi = jnp.asarray([i for i, _ in pairs], jnp.int32)
    kj = jnp.asarray([j for _, j in pairs], jnp.int32)
    stat = pltpu.VMEM((1, tq), F32)
    return _call(
        body, grid=(nhp, len(pairs)), prefetch=(qi, kj),
        in_specs=[pl.BlockSpec((tq, LANES), lambda hp, t, qi, kj: (qi[t], hp)),
                  pl.BlockSpec((tk, LANES), lambda hp, t, qi, kj: (kj[t], hp)),
                  pl.BlockSpec((tk, LANES), lambda hp, t, qi, kj: (kj[t], nhp + hp)),
                  pl.BlockSpec((1, 2, tq), lambda hp, t, qi, kj: (hp, 0, qi[t])),
                  pl.BlockSpec((1, tk, 2), lambda hp, t, qi, kj: (hp, kj[t], 0))],
        out_specs=[pl.BlockSpec((tq, LANES), lambda hp, t, qi, kj: (qi[t], hp)),
                   pl.BlockSpec((1, 2, tq), lambda hp, t, qi, kj: (hp, 0, qi[t]))],
        scratch_shapes=[stat, stat, stat, stat, pltpu.VMEM((LANES, tq), F32), pltpu.VMEM((tk, tq), F32),
                        pltpu.VMEM((tk, tq), F32), pltpu.VMEM((tk, tq), BF16), pltpu.VMEM((tk, tq), BF16),
                        pltpu.VMEM((tk, LANES), F32), pltpu.VMEM((tk, LANES), F32)],
        out_shape=[jax.ShapeDtypeStruct((n_rows, d), F32), jax.ShapeDtypeStruct((nhp, 2, n_rows), F32)],
        args=(q, kv, kv, cum_r, cum_c), name=name, comm=comm)


def _head_delta(do, o, *, name):
    d = o.shape[1]

    def fn(dd, oo):
        prod = dd.astype(BF16).astype(F32) * oo
        r = lax.broadcasted_iota(jnp.int32, (d, LANES), 0)
        c = lax.broadcasted_iota(jnp.int32, (d, LANES), 1)
        return _tri_dot(prod, jnp.where(jnp.right_shift(r, HEAD_DIM.bit_length() - 1) == c, 1.0, 0.0))

    return _rowwise(fn, [do, o], [], [(LANES, F32)], name=name)[0]


def _flash_bwd(q, kv, lse_r, delta_r, do, cum_c, cum_r, *, tq, name, comm=None):
    n_rows, d = q.shape
    nhp = d // LANES
    tk = tq
    nq = n_rows // tq
    rt = _tile(tk, FLASH_ROW_TILE)
    reps = (1, tq // LANES)

    def body(qi_ref, kj_ref, q_ref, k_ref, v_ref, lse_ref, dl_ref, do_ref, cq_ref, ck_ref,
             dq_ref, dk_ref, dv_ref, dck_ref, dcq_ref, s0, dp0, p0, ds0, b0, b1, ck0, ck1):
        s1, dp1, p1, ds1 = s0, dp0, p0, ds0
        i = qi_ref[pl.program_id(1)]
        j = kj_ref[pl.program_id(1)]

        @pl.when(pl.program_id(1) == 0)
        def _():
            dq_ref[...] = jnp.zeros_like(dq_ref)
            dcq_ref[...] = jnp.zeros_like(dcq_ref)

        @pl.when(i == j)
        def _():
            for ref in (dk_ref, dv_ref, ck0, ck1):
                ref[...] = jnp.zeros_like(ref)

        def block(diagonal):
            qv, kk, vv = q_ref[...], k_ref[...], v_ref[...]
            dob = do_ref[...].astype(BF16)
            dq_acc = jnp.zeros((tq, LANES), F32)
            dk_acc = jnp.zeros((tk, LANES), F32)
            dv_acc = jnp.zeros((tk, LANES), F32)
            query_sums = []
            for h, msk in enumerate(_head_masks()):
                st_sc, dpt_sc, pt_sc, dst_sc, bias_sc, key_part = ((s0, dp0, p0, ds0, b0, ck0),
                                                                  (s1, dp1, p1, ds1, b1, ck1))[h]
                qh = jnp.where(msk, qv, jnp.zeros_like(qv))
                kh = jnp.where(msk, kk, jnp.zeros_like(kk))
                doh = jnp.where(msk, dob, jnp.zeros_like(dob))
                st_sc[...] = _dot(kk, qh, 1, 1)
                dpt_sc[...] = _dot(vv, doh, 1, 1)
                bias_sc[...] = jnp.broadcast_to(cq_ref[0, h:h + 1, 0:1] - ck_ref[0, :, h:h + 1], (tk, LANES))
                lse_row = lse_ref[0, h:h + 1, :]
                delta_row = dl_ref[0, h:h + 1, :]
                col_acc = jnp.zeros((SUBLANES, tq), F32)
                parts = []
                for r in range(tk // rt):
                    rows = slice(r * rt, (r + 1) * rt)
                    s = st_sc[rows, :] + jnp.tile(bias_sc[rows, :], reps)
                    if diagonal:
                        key = r * rt + lax.broadcasted_iota(jnp.int32, (rt, tq), 0)
                        qry = lax.broadcasted_iota(jnp.int32, (rt, tq), 1)
                        s = jnp.where(key <= qry, s, -jnp.inf)
                    p = jnp.exp(s - lse_row)
                    ds = p * (dpt_sc[rows, :] - delta_row)
                    for g in range(rt // SUBLANES):
                        col_acc = col_acc + ds[g * SUBLANES:(g + 1) * SUBLANES, :]
                    part = ds[:, 0:LANES]
                    for g in range(1, tq // LANES):
                        part = part + ds[:, g * LANES:(g + 1) * LANES]
                    parts.append(part)
                    pt_sc[rows, :] = p.astype(BF16)
                    dst_sc[rows, :] = ds.astype(BF16)
                key_part[...] += jnp.concatenate(parts, axis=0)
                query_sums.append(jnp.sum(col_acc, axis=0, keepdims=True))
                dv_acc = dv_acc + _dot(pt_sc[...], doh, 1, 0)
                dsb = dst_sc[...]
                dk_acc = dk_acc + _dot(dsb, qh, 1, 0)
                dq_acc = dq_acc + _dot(dsb, kh, 0, 0)
            off = pl.multiple_of(i * tq, tq)
            dq_ref[pl.ds(off, tq), :] += dq_acc
            dk_ref[...] += dk_acc
            dv_ref[...] += dv_acc
            dcq_ref[0, i] += jnp.concatenate(query_sums, axis=0)

        pl.when(i > j)(functools.partial(block, False))
        pl.when(i == j)(functools.partial(block, True))

        @pl.when(i == nq - 1)
        def _():
            two = lax.broadcasted_iota(jnp.int32, (tk, 2), 1)
            dck_ref[0] = jnp.where(two == 0, -jnp.sum(ck0[...], axis=1, keepdims=True),
                                   -jnp.sum(ck1[...], axis=1, keepdims=True))

    pairs = [(i, j) for j in range(nq) for i in range(j, nq)]
    qi = jnp.asarray([i for i, _ in pairs], jnp.int32)
    kj = jnp.asarray([j for _, j in pairs], jnp.int32)
    score = pltpu.VMEM((tk, tq), F32)
    score16 = pltpu.VMEM((tk, tq), BF16)
    keystat = pltpu.VMEM((tk, LANES), F32)
    return _call(
        body, grid=(nhp, len(pairs)), prefetch=(qi, kj),
        in_specs=[pl.BlockSpec((tq, LANES), lambda hp, t, qi, kj: (qi[t], hp)),
                  pl.BlockSpec((tk, LANES), lambda hp, t, qi, kj: (kj[t], hp)),
                  pl.BlockSpec((tk, LANES), lambda hp, t, qi, kj: (kj[t], nhp + hp)),
                  pl.BlockSpec((1, 2, tq), lambda hp, t, qi, kj: (hp, 0, qi[t])),
                  pl.BlockSpec((1, 2, tq), lambda hp, t, qi, kj: (hp, 0, qi[t])),
                  pl.BlockSpec((tq, LANES), lambda hp, t, qi, kj: (qi[t], hp)),
                  pl.BlockSpec((1, 2, tq), lambda hp, t, qi, kj: (hp, 0, qi[t])),
                  pl.BlockSpec((1, tk, 2), lambda hp, t, qi, kj: (hp, kj[t], 0))],
        out_specs=[pl.BlockSpec((n_rows, LANES), lambda hp, t, qi, kj: (0, hp)),
                   pl.BlockSpec((tk, LANES), lambda hp, t, qi, kj: (kj[t], hp)),
                   pl.BlockSpec((tk, LANES), lambda hp, t, qi, kj: (kj[t], hp)),
                   pl.BlockSpec((1, tk, 2), lambda hp, t, qi, kj: (hp, kj[t], 0)),
                   pl.BlockSpec((1, nq, 2, tq), lambda hp, t, qi, kj: (hp, 0, 0, 0))],
        scratch_shapes=[score, score, score16, score16, keystat, keystat, keystat, keystat],
        out_shape=[jax.ShapeDtypeStruct((n_rows, d), F32), jax.ShapeDtypeStruct((n_rows, d), F32),
                   jax.ShapeDtypeStruct((n_rows, d), F32), jax.ShapeDtypeStruct((nhp, n_rows, 2), F32),
                   jax.ShapeDtypeStruct((nhp, nq, 2, tq), F32)],
        args=(q, kv, kv, lse_r, delta_r, do, cum_r, cum_c), name=name, comm=comm)


def _s5_tables(w, layer):
    g, p = w["lam_re"].shape[1:]
    h = w["ssm_b_re"].shape[3]
    n = g * p
    lr = w["lam_re"][layer].reshape(1, n)
    li = w["lam_im"][layer].reshape(1, n)
    ldt = jnp.broadcast_to(w["log_dt"][layer][:, None], (g, p)).reshape(1, n)
    br = w["ssm_b_re"][layer].transpose(2, 0, 1).reshape(h, n)
    bi = w["ssm_b_im"][layer].transpose(2, 0, 1).reshape(h, n)
    cr = w["ssm_c_re"][layer].transpose(1, 0, 2).reshape(h, n)
    ci = w["ssm_c_im"][layer].transpose(1, 0, 2).reshape(h, n)
    return (lr, li, ldt, br, bi, cr, ci), (g, p, h)


def _local_step(x, tgt, w, net=None, *, attn_tile=1024):
    n_rows, d = x.shape
    n_layers = w["g_mix"].shape[0]
    n_s5 = w["lam_re"].shape[0]
    nh = w["b_f"].shape[0]
    nhp = nh // 2
    assert d == nh * HEAD_DIM
    tq = _tile(n_rows, attn_tile)
    g = {}
    saved = [dict() for _ in range(n_layers)]
    big = {}
    pending = {}

    def wt(name, layer):
        return w[name][layer]

    def carry_gather(group, run):
        if net is None or not net.has_group(group):
            return run(None)
        outs, got = run(net.gather_comm(group))
        net.store_gathered(group, got, w)
        return outs

    def carry_reduce(tag, run):
        keys = list(pending)
        grads = [pending[k][0] for k in keys]
        if net is None or not pending:
            big.update(zip(keys, grads))
            pending.clear()
            return run(None)
        parts = net.reduce_prepare(grads, [pending[k][1] for k in keys], tag)
        outs, landed = run(_chip_exchange_comm(parts))
        big.update(zip(keys, net.reduce_finish(parts, landed, grads, tag)))
        pending.clear()
        return outs

    def by_row_shard(m):
        return m.reshape(N_CHIPS, m.shape[0] // N_CHIPS, m.shape[1])

    def grad_and_copy(x_, dy_, name, *, col_slots=None, scale=None):
        g32, g16 = _mm_tn(x_, dy_, col_slots or 1, scale=scale, wire=True, name=name)
        return (g32, g16) if col_slots else (by_row_shard(g32[0]), by_row_shard(g16[0]))

    h = x
    nxt = _rowwise(lambda a, gg: _rms(a, gg), [x], [_row2(w["g_mix"][0])], [(d, F32)], name="rms_first")[0]
    kvb = fl = cum = cq3 = ck3 = hnkv = None
    bf_pad = jnp.zeros((1, LANES), F32).at[0, :nh].set(w["b_f"])
    for l in range(n_layers):
        sv = saved[l]
        sv["h"] = h
        g_ffn = _row2(w["g_ffn"][l])
        if l < n_s5:
            tabs, (_, p, _) = _s5_tables(w, l)
            prep = _s5_prep(*tabs, p, name=f"s5_prep{l}")
            dskip = w["ssm_d"][l].reshape(1, d)
            y, z, st_re, st_im, sb_re, sb_im = carry_gather(
                f"stage{l}", lambda comm, u=nxt, pr=prep, ds=dskip: _s5_fwd(u, pr, ds, name=f"s5_fwd{l}", comm=comm))
            glu4, glu_l = wt("w_glu", l)
            glu_whole = glu4[:, glu_l].transpose(1, 0, 2).reshape(1, 1, glu4.shape[2], -1)
            zz, h1, hn2 = _mm_cols(z, glu_whole, 0, wc=0, name=f"glu_mm{l}",
                                   epilogue=(lambda zq, hh, gg: ((lambda t: (zq, t, _rms(t, gg)))(hh + _glu(zq))),
                                             [h], [g_ffn], [(2 * d, F32), (d, F32), (d, BF16)], ()))
            sv.update(u=nxt, prep=prep, tabs=tabs, p=p, dskip=dskip, st_re=st_re, st_im=st_im, sb_re=sb_re,
                      sb_im=sb_im, y=y, z=z, zz=zz)
        else:
            j = l - n_s5
            qs = _mm_cols(nxt, *wt("w_q", j), wc=0, out_dtype=BF16, scale=HEAD_DIM ** -0.5, name=f"q_mm{j}")
            o, lse = carry_gather(
                f"stage{l}", lambda comm, q_=qs: _flash_fwd(q_, kvb, cq3, ck3, tq=tq, name=f"flash_fwd{j}", comm=comm))
            h1, hn2 = _mm_cols(o, *wt("w_o", j), wc=0, name=f"o_mm{j}",
                               epilogue=(lambda aa, hh, gg: ((lambda t: (t, _rms(t, gg)))(hh + aa)), [h], [g_ffn],
                                         [(d, F32), (d, BF16)], ()))
            sv.update(hn=nxt, qs=qs, o=o, lse=lse)
        uu = _mm_cols(hn2, *wt("w_in", l), wc=0, name=f"ffn_in{l}")
        cw, cb = w["conv_w"][l], _row2(w["conv_b"][l])
        act = _conv_fwd(uu, cw, cb, name=f"conv_fwd{l}")
        sv.update(h1=h1, hn2=hn2, uu=uu, act=act, cw=cw, cb=cb)

        def ffn_out(fn, rows, consts, outs, accs=()):
            return _mm_cols(act, *wt("w_out", l), wc=0, name=f"ffn_out{l}", epilogue=(fn, rows, consts, outs, accs))

        if l == n_layers - 1:
            def loss_fn(ff, hh, tt, gg):
                yv, vjp = jax.vjp(_rms, hh + ff, gg)
                err = yv - tt
                part = 0.5 * jnp.sum(jnp.mean(err * err, axis=-1, keepdims=True), axis=0, keepdims=True)
                dh, dg = vjp(err * (1.0 / d))
                return dh, dh, jnp.broadcast_to(part, (1, LANES)), dg
            dcur, dcur16, loss_row, dgf = ffn_out(loss_fn, [h1, tgt], [_row2(w["g_final"])],
                                                  [(d, F32), (d, BF16)], [(1, LANES), (1, d)])
            loss = loss_row[0, 0]
            g["g_final"] = dgf[0]
        elif l + 1 < n_s5:
            h, nxt = ffn_out(lambda ff, hh, gg: ((lambda t: (t, _rms(t, gg)))(hh + ff)), [h1],
                             [_row2(w["g_mix"][l + 1])], [(d, F32), (d, F32)])
        elif l + 1 == n_s5:
            h, nxt, hnkv = ffn_out(lambda ff, hh, g1, g2: ((lambda t: (t, _rms(t, g1), _rms(t, g2)))(hh + ff)), [h1],
                                   [_row2(w["g_mix"][l + 1]), _row2(w["g_kv"])],
                                   [(d, F32), (d, BF16), (d, BF16)])
            kvb = _mm_cols(hnkv, *wt("w_kv", 0), wc=0, out_dtype=BF16, name="kv_mm")
            fl = _mm_cols(hnkv, *wt("w_f", 0), wc=0, name="f_mm")
            cum = _cum_fwd(fl, bf_pad, name="cum_fwd")
            cq3 = cum[:, :nh].reshape(n_rows, nhp, 2).transpose(1, 0, 2)
            ck3 = cum[:, :nh].T.reshape(nhp, 2, n_rows)
        else:
            h, nxt = ffn_out(lambda ff, hh, gg: ((lambda t: (t, _rms(t, gg)))(hh + ff)), [h1],
                             [_row2(w["g_mix"][l + 1])], [(d, F32), (d, BF16)])

    per_layer = {k: [None] * n_layers for k in ("g_mix", "g_ffn", "conv_w", "conv_b")}
    per_s5 = {k: [None] * n_s5 for k in ("lam_re", "lam_im", "log_dt", "ssm_b_re", "ssm_b_im", "ssm_c_re",
                                         "ssm_c_im", "ssm_d")}
    dk_parts, dv_parts, dck_parts = [], [], []
    for l in reversed(range(n_layers)):
        sv = saved[l]
        dact = _mm_cols(dcur16, *wt("w_out", l), wc=1, name=f"ffn_out_dx{l}")
        pending["w_ffn_out", l] = grad_and_copy(sv["act"], dcur16, f"ffn_out_dw{l}")
        duu, dcw, dcb = _conv_bwd(sv["uu"], dact, sv["cw"], sv["cb"], name=f"conv_bwd{l}")
        per_layer["conv_w"][l] = jnp.concatenate([dcw[0], dcw[1]], axis=-1)
        per_layer["conv_b"][l] = jnp.concatenate([dcb[0, 0], dcb[1, 0]])
        node = _node_bwd_fn([1])
        d1, d1_16, dg = _mm_acc(duu, *wt("w_in", l), wc=1, name=f"ffn_in_dx{l}",
                                epilogue=(lambda dhn2, dd, hh, gg: node(dd, hh, dhn2, gg), [dcur, sv["h1"]],
                                          [_row2(w["g_ffn"][l])], [(d, F32), (d, BF16)], [(1, d)]))
        pending["w_ffn_in", l] = grad_and_copy(sv["hn2"], duu, f"ffn_in_dw{l}", col_slots=wt("w_in", l)[0].shape[0])
        per_layer["g_ffn"][l] = dg[0]
        if l < n_s5:
            def glu_bwd(zq, dd):
                _, vjp = jax.vjp(_glu, zq)
                return vjp(dd)[0]
            dzz = _rowwise(glu_bwd, [sv["zz"], d1], [], [(2 * d, BF16)], name=f"glu_bwd{l}")[0]
            def gelu_bwd(dd, yy):
                _, vjp = jax.vjp(_gelu, yy)
                return (vjp(dd)[0],)
            dy = _mm_acc(dzz, *wt("w_glu", l), wc=1, name=f"glu_dx{l}",
                         epilogue=(gelu_bwd, [sv["y"]], [], [(d, F32)], ()))[0]
            pending["w_glu", l] = grad_and_copy(sv["z"], dzz, f"glu_dw{l}", col_slots=wt("w_glu", l)[0].shape[0])
            du, dwbr, dwbi, dwcr, dwci, dlbr, dlbi, dd = carry_reduce(
                f"stage{l}", lambda comm, dy_=dy: _s5_bwd(sv["u"], dy_, sv["st_re"], sv["st_im"], sv["sb_re"],
                                                          sv["sb_im"], sv["prep"], sv["dskip"], name=f"s5_bwd{l}",
                                                          comm=comm))
            dlr, dli, dldt, dbr, dbi, dcr, dci = _s5_prep_bwd(*sv["tabs"], sv["p"], dlbr, dlbi, dwbr, dwbi, dwcr,
                                                              dwci, name=f"s5_prep_bwd{l}")
            gg, p = w["lam_re"].shape[1:]
            hh = w["ssm_b_re"].shape[3]
            per_s5["lam_re"][l] = dlr.reshape(gg, p)
            per_s5["lam_im"][l] = dli.reshape(gg, p)
            per_s5["log_dt"][l] = dldt.reshape(gg, p).sum(axis=1)
            per_s5["ssm_b_re"][l] = dbr.reshape(hh, gg, p).transpose(1, 2, 0)
            per_s5["ssm_b_im"][l] = dbi.reshape(hh, gg, p).transpose(1, 2, 0)
            per_s5["ssm_c_re"][l] = dcr.reshape(hh, gg, p).transpose(1, 0, 2)
            per_s5["ssm_c_im"][l] = dci.reshape(hh, gg, p).transpose(1, 0, 2)
            per_s5["ssm_d"][l] = dd.reshape(d)
            branches = [(w["g_mix"][l], [du])]
        else:
            j = l - n_s5
            do = _mm_cols(d1_16, *wt("w_o", j), wc=1, name=f"o_dx{j}")
            pending["w_o", j] = grad_and_copy(sv["o"], d1_16, f"o_dw{j}")
            delta_r = _head_delta(do, sv["o"], name=f"head_delta{j}")[:, :nh].T.reshape(nhp, 2, n_rows)
            dq, dk, dv, dck, dcq = carry_reduce(
                f"stage{l}", lambda comm, do_=do: _flash_bwd(sv["qs"], kvb, sv["lse"], delta_r, do_, cq3, ck3, tq=tq,
                                                          name=f"flash_bwd{j}", comm=comm))
            dk_parts.append(dk)
            dv_parts.append(dv)
            dck_parts.append(dck.transpose(1, 0, 2).reshape(n_rows, nh)
                             + dcq.transpose(0, 2, 1, 3).reshape(nh, n_rows).T)
            scale = HEAD_DIM ** -0.5
            pending["w_q", j] = grad_and_copy(sv["hn"], dq, f"q_dw{j}", scale=scale)
            branches = []
            if j == 0:
                def kv_sum(*parts):
                    half = len(parts) // 2
                    return jnp.concatenate([sum(parts[:half][1:], parts[0]),
                                            sum(parts[half:][1:], parts[half])], axis=1)
                dkv = _rowwise(kv_sum, dk_parts + dv_parts, [], [(2 * d, BF16)], name="dkv_sum")[0]
                dck_tot = dck_parts[0]
                for extra in dck_parts[1:]:
                    dck_tot = dck_tot + extra
                dcum = jnp.zeros((n_rows, LANES), F32).at[:, :nh].set(dck_tot)
                dfl, dbf = _cum_bwd(dcum, fl, bf_pad, name="cum_bwd")
                g["b_f"] = dbf[0, :nh]
                dhkv_a = _mm_cols(dkv, *wt("w_kv", 0), wc=1, name="kv_dx")
                dhkv_b = _mm_cols(dfl, *wt("w_f", 0), wc=1, name="f_dx")
                d_kvf = jnp.concatenate([_mm_tn(hnkv, dkv, 1, name="kv_dw")[0],
                                         _mm_tn(hnkv, dfl, 1, name="f_dw")[0][:, :nh]], axis=1)
                d_kvf = d_kvf.reshape(d, N_CHIPS, -1).transpose(1, 0, 2)
                pending["w_kvf", 0] = (d_kvf, d_kvf.astype(BF16))
                branches.append((w["g_kv"], [dhkv_a, dhkv_b]))
            node = _node_bwd_fn([1] + [len(dys) for _, dys in branches])
            gains = [_row2(w["g_mix"][l])] + [_row2(gn) for gn, _ in branches]
            res = _mm_cols(dq, *wt("w_q", j), wc=1, scale=scale, name=f"q_dx{j}",
                           epilogue=(lambda dhn, dd, hh, *rest: node(dd, hh, dhn, *rest),
                                     [d1, sv["h"], *[dy for _, dys in branches for dy in dys]], gains,
                                     [(d, F32), (d, BF16)], [(1, d)] * len(gains)))
            dcur, dcur16, dgs = res[0], res[1], [r[0] for r in res[2:]]
        if l < n_s5:
            dcur, dcur16, dgs = _node_bwd(d1, sv["h"], branches, name=f"mix_norm_bwd{l}")
        per_layer["g_mix"][l] = dgs[0]
        if len(dgs) > 1:
            g["g_kv"] = dgs[1]

    if pending:
        grads = [g32 for g32, _ in pending.values()]
        big.update(zip(list(pending), grads if net is None else net.reduce_blocking(grads, "tail")))
    for k, v in (*per_layer.items(), *per_s5.items()):
        g[k] = jnp.stack(v)
    g["big"] = big
    return loss, dcur, g


def _position():
    x, y, c = lax.axis_index("x"), lax.axis_index("y"), lax.axis_index("c")
    chips = [(1 - x, y), (x, 1 - y), (1 - x, 1 - y)]
    return x, y, c, chips


def _all_gather_comm(shards):
    n = len(shards)

    def descriptors(ins, outs, sems):
        send_sems, recv_sems = sems
        x, y, c, chips = _position()
        my_slot = 2 * x + y
        sibling = (x, y, 1 - c)

        def rows(t, half):
            hr = ins[t].shape[0] // 2
            return pl.ds(half * hr, hr)

        def remote(k, t, src, dst, to):
            return pltpu.make_async_remote_copy(src_ref=src, dst_ref=dst, send_sem=send_sems.at[k, t],
                                                recv_sem=recv_sems.at[k, t], device_id=to, device_id_type=MESH)

        own = [remote(6, t, ins[t], outs[t].at[my_slot], sibling) for t in range(n)]
        ici = [remote(j, t, ins[t].at[rows(t, c)], outs[t].at[my_slot, rows(t, c)], (*chip, c))
               for j, chip in enumerate(chips) for t in range(n)]
        slots = [2 * chip[0] + chip[1] for chip in chips]
        fwd = [[remote(3 + j, t, outs[t].at[slots[j], rows(t, c)], outs[t].at[slots[j], rows(t, c)], sibling)
                for t in range(n)] for j in range(len(chips))]
        landed = [[remote(j, t, outs[t].at[slots[j], rows(t, c)], outs[t].at[slots[j], rows(t, c)], (*chips[j], c))
                   for t in range(n)] for j in range(len(chips))]
        from_sibling = [remote(3 + j, t, outs[t].at[slots[j], rows(t, 1 - c)], outs[t].at[slots[j], rows(t, 1 - c)],
                               sibling) for j in range(len(chips)) for t in range(n)]
        return own, ici, fwd, landed, from_sibling

    def start(ins, outs, sems):
        own, ici, _, _, _ = descriptors(ins, outs, sems)
        for cp in own + ici:
            cp.start()

    def finish(ins, outs, sems):
        own, ici, fwd, landed, from_sibling = descriptors(ins, outs, sems)
        for j in range(len(fwd)):
            for cp in landed[j]:
                cp.wait_recv()
            for cp in fwd[j]:
                cp.start()
        for cp in from_sibling + own:
            cp.wait_recv()
        for cp in own + ici + [cp for group in fwd for cp in group]:
            cp.wait_send()

    return _Comm(list(shards), [jax.ShapeDtypeStruct((N_CHIPS,) + a.shape, a.dtype) for a in shards],
                 [pltpu.SemaphoreType.DMA((7, n)), pltpu.SemaphoreType.DMA((7, n))], start, finish)


def _all_gather(shards, *, name):
    return _run_comm(_all_gather_comm(shards), name=name)


def _pair_exchange(grads, *, name):
    n = len(grads)

    def body(*refs):
        ins, outs = refs[:n], refs[n:2 * n]
        send_sems, recv_sems = refs[2 * n:]
        x, y, c, _ = _position()
        copies = [pltpu.make_async_remote_copy(src_ref=ins[t].at[:, 1 - c], dst_ref=outs[t],
                                               send_sem=send_sems.at[t], recv_sem=recv_sems.at[t],
                                               device_id=(x, y, 1 - c), device_id_type=MESH) for t in range(n)]
        for cp in copies:
            cp.start()
        for cp in copies:
            cp.wait()

    return pl.pallas_call(
        body, in_specs=_any_specs(n), out_specs=_any_specs(n),
        out_shape=[jax.ShapeDtypeStruct((a.shape[0],) + a.shape[2:], a.dtype) for a in grads],
        scratch_shapes=[pltpu.SemaphoreType.DMA((n,)), pltpu.SemaphoreType.DMA((n,))], name=name)(*grads)


def _chip_exchange_comm(parts):
    n = len(parts)

    def copies(ins, outs, sems):
        send_sems, recv_sems = sems
        _, _, c, chips = _position()
        return [pltpu.make_async_remote_copy(src_ref=ins[t].at[2 * chip[0] + chip[1]], dst_ref=outs[t].at[j],
                                             send_sem=send_sems.at[j, t], recv_sem=recv_sems.at[j, t],
                                             device_id=(*chip, c), device_id_type=MESH)
                for j, chip in enumerate(chips) for t in range(n)]

    def start(ins, outs, sems):
        for cp in copies(ins, outs, sems):
            cp.start()

    def finish(ins, outs, sems):
        for cp in copies(ins, outs, sems):
            cp.wait()

    return _Comm(list(parts), [jax.ShapeDtypeStruct((N_CHIPS - 1,) + a.shape[1:], a.dtype) for a in parts],
                 [pltpu.SemaphoreType.DMA((N_CHIPS - 1, n)), pltpu.SemaphoreType.DMA((N_CHIPS - 1, n))],
                 start, finish)


def _pair_share(both, *, name):
    n = len(both)

    def body(*refs):
        ins, outs = refs[:n], refs[n:2 * n]
        send_sems, recv_sems = refs[2 * n:]
        x, y, c, _ = _position()
        for t in range(n):
            pltpu.make_async_remote_copy(src_ref=ins[t].at[c], dst_ref=outs[t].at[c], send_sem=send_sems.at[t],
                                         recv_sem=recv_sems.at[t], device_id=(x, y, 1 - c),
                                         device_id_type=MESH).start()
        for t in range(n):
            pltpu.make_async_remote_copy(src_ref=ins[t].at[c], dst_ref=outs[t].at[1 - c], send_sem=send_sems.at[t],
                                         recv_sem=recv_sems.at[t], device_id=(x, y, 1 - c),
                                         device_id_type=MESH).wait()

    return pl.pallas_call(
        body, in_specs=_any_specs(n), out_specs=_any_specs(n),
        out_shape=[jax.ShapeDtypeStruct(a.shape, a.dtype) for a in both],
        input_output_aliases={t: t for t in range(n)},
        scratch_shapes=[pltpu.SemaphoreType.DMA((n,)), pltpu.SemaphoreType.DMA((n,))], name=name)(*both)


def _sum_pair(grad, landed, c, wire_dtype, *, name):
    slots, _, m, n = grad.shape
    tm = _tile(m, 256, 2 * SUBLANES)

    def body(c_ref, g_ref, l_ref, o_ref):
        o_ref[...] = (g_ref[0] + l_ref[...]).astype(wire_dtype)

    return pl.pallas_call(
        body,
        grid_spec=pltpu.PrefetchScalarGridSpec(
            num_scalar_prefetch=1, grid=(slots, m // tm),
            in_specs=[pl.BlockSpec((1, 1, tm, n), lambda s, i, c_ref: (s, c_ref[0], i, 0)),
                      pl.BlockSpec((1, tm, n), lambda s, i, c_ref: (s, i, 0))],
            out_specs=pl.BlockSpec((1, tm, n), lambda s, i, c_ref: (s, i, 0))),
        out_shape=jax.ShapeDtypeStruct((slots, m, n), wire_dtype), compiler_params=_cp(2), name=name)(
            c, grad, landed)


def _sum_chips(part, landed, slot_c, *, name):
    _, m, n = part.shape
    tm = _tile(m, 256, 2 * SUBLANES)

    def body(s_ref, p_ref, l_ref, o_ref):
        acc = p_ref[0].astype(F32)
        for j in range(N_CHIPS - 1):
            acc = acc + l_ref[j].astype(F32)
        o_ref[0] = acc

    return pl.pallas_call(
        body,
        grid_spec=pltpu.PrefetchScalarGridSpec(
            num_scalar_prefetch=1, grid=(m // tm,),
            in_specs=[pl.BlockSpec((1, tm, n), lambda i, s_ref: (s_ref[0], i, 0)),
                      pl.BlockSpec((N_CHIPS - 1, tm, n), lambda i, s_ref: (0, i, 0))],
            out_specs=pl.BlockSpec((1, tm, n), lambda i, s_ref: (s_ref[1], i, 0))),
        out_shape=jax.ShapeDtypeStruct((N_CORES, m, n), F32), compiler_params=_cp(1), name=name)(
            slot_c, part, landed)


def _reduce_prepare(grads, wire_dtypes, tag, copies=None):
    c = lax.axis_index("c").reshape(1).astype(jnp.int32)

    def halves(a):
        lead, last = a.shape[1], a.shape[-1]
        mid = 1
        for s in a.shape[2:-1]:
            mid *= s
        return a.reshape(N_CHIPS, N_CORES, (lead // N_CORES) * mid, last)

    views = [halves(a) for a in grads]
    landed = _pair_exchange(views if copies is None else [halves(a) for a in copies],
                            name=f"rs_pair_exchange_{tag}")
    return [_sum_pair(v, l, c, wire_dtypes[t], name=f"rs_pair_sum_{tag}_{t}")
            for t, (v, l) in enumerate(zip(views, landed))]


def _reduce_finish(parts, landed, grads, tag):
    slot_c = jnp.stack([2 * lax.axis_index("x") + lax.axis_index("y"), lax.axis_index("c")]).astype(jnp.int32)
    both = [_sum_chips(p, l, slot_c, name=f"rs_chip_sum_{tag}_{t}") for t, (p, l) in enumerate(zip(parts, landed))]
    full = _pair_share(both, name=f"rs_pair_share_{tag}")
    return [f.reshape(a.shape[1:]) for f, a in zip(full, grads)]


def _reduce_scatter(grads, wire_dtypes, tag):
    parts = _reduce_prepare(grads, wire_dtypes, tag)
    landed = _run_comm(_chip_exchange_comm(parts), name=f"rs_chip_exchange_{tag}")
    return _reduce_finish(parts, landed, grads, tag)


class _Net:
    def __init__(self, groups, d, nh):
        self.groups, self.d, self.nh = groups, d, nh

    def has_group(self, group):
        return bool(self.groups.get(group))

    def gather_comm(self, group):
        return _all_gather_comm([shard for _, _, shard in self.groups[group]])

    def store_gathered(self, group, got, w):
        d, nh = self.d, self.nh
        for (name, layer, _), full in zip(self.groups[group], got):
            if name == "w_kvf":
                mat = full.transpose(1, 0, 2).reshape(d, -1)
                w["w_kv"][0] = (mat[:, :2 * d][None, None], 0)
                w["w_f"][0] = (jnp.zeros((d, LANES), BF16).at[:, :nh].set(mat[:, 2 * d:])[None, None], 0)
            elif name in ("w_in", "w_glu"):
                w[name][layer] = (full[:, None], 0)
            else:
                w[name][layer] = (full.reshape(1, 1, -1, full.shape[-1]), 0)

    def reduce_prepare(self, grads, copies, tag):
        return _reduce_prepare(grads, [BF16] * len(grads), tag, copies)

    def reduce_finish(self, parts, landed, grads, tag):
        return _reduce_finish(parts, landed, grads, tag)

    def reduce_blocking(self, grads, tag):
        return _reduce_scatter(grads, [BF16] * len(grads), tag)


def _adamw(w, g, m, v, *, name):
    def fn(ww, gg, mm, vv):
        mm = ADAM_B1 * mm + (1.0 - ADAM_B1) * gg
        vv = ADAM_B2 * vv + (1.0 - ADAM_B2) * (gg * gg)
        m_hat = mm / (1.0 - ADAM_B1 ** ADAM_STEP)
        v_hat = vv / (1.0 - ADAM_B2 ** ADAM_STEP)
        delta = -ADAM_LR * (m_hat / (jnp.sqrt(v_hat) + ADAM_EPS) + ADAM_WD * ww)
        return delta, mm, vv

    shape = w.shape
    two_d = [a.reshape(-1, shape[-1]) for a in (w, g, m, v)]
    outs = _rowwise(fn, two_d, [], [(shape[-1], F32)] * 3, name=name)
    return [o.reshape(shape) for o in outs]


def _to_bf16(a, *, name):
    two_d = a.reshape(-1, a.shape[-1])
    return _rowwise(lambda t: t, [two_d], [], [(a.shape[-1], BF16)], name=name)[0].reshape(a.shape)


def _pack(arrays, rows_multiple):
    flat = jnp.concatenate([a.reshape(-1) for a in arrays])
    rows = -(-flat.shape[0] // LANES)
    rows = -(-rows // rows_multiple) * rows_multiple
    return jnp.pad(flat, (0, rows * LANES - flat.shape[0])).reshape(rows, LANES)


def _unpack(packed, like):
    flat = packed.reshape(-1)
    out, pos = [], 0
    for a in like:
        out.append(flat[pos:pos + a.size].reshape(a.shape))
        pos += a.size
    return out


_PARAMS = ("g_mix", "g_ffn", "lam_re", "lam_im", "log_dt", "ssm_b_re", "ssm_b_im", "ssm_c_re", "ssm_c_im", "ssm_d",
           "w_glu", "g_kv", "w_kvf", "b_f", "w_q", "w_o", "w_ffn_in", "ffn_conv_w", "ffn_conv_b", "w_ffn_out",
           "g_final")
_BIG = ("w_glu", "w_kvf", "w_q", "w_o", "w_ffn_in", "w_ffn_out")
_SMALL_SHARDED = ("ssm_d", "ffn_conv_w")


def kernel(x, g_mix, g_ffn, lam_re, lam_im, log_dt, ssm_b_re, ssm_b_im, ssm_c_re, ssm_c_im, ssm_d, w_glu, g_kv, w_kvf, b_f, w_q, w_o, w_ffn_in, ffn_conv_w, ffn_conv_b, w_ffn_out, g_final, loss_target, m_g_mix, m_g_ffn, m_lam_re, m_lam_im, m_log_dt, m_ssm_b_re, m_ssm_b_im, m_ssm_c_re, m_ssm_c_im, m_ssm_d, m_w_glu, m_g_kv, m_w_kvf, m_b_f, m_w_q, m_w_o, m_w_ffn_in, m_ffn_conv_w, m_ffn_conv_b, m_w_ffn_out, m_g_final, v_g_mix, v_g_ffn, v_lam_re, v_lam_im, v_log_dt, v_ssm_b_re, v_ssm_b_im, v_ssm_c_re, v_ssm_c_im, v_ssm_d, v_w_glu, v_g_kv, v_w_kvf, v_b_f, v_w_q, v_w_o, v_w_ffn_in, v_ffn_conv_w, v_ffn_conv_b, v_w_ffn_out, v_g_final):
    p = dict(g_mix=g_mix, g_ffn=g_ffn, lam_re=lam_re, lam_im=lam_im, log_dt=log_dt, ssm_b_re=ssm_b_re,
             ssm_b_im=ssm_b_im, ssm_c_re=ssm_c_re, ssm_c_im=ssm_c_im, ssm_d=ssm_d, w_glu=w_glu, g_kv=g_kv,
             w_kvf=w_kvf, b_f=b_f, w_q=w_q, w_o=w_o, w_ffn_in=w_ffn_in, ffn_conv_w=ffn_conv_w,
             ffn_conv_b=ffn_conv_b, w_ffn_out=w_ffn_out, g_final=g_final)
    mom1 = dict(zip(_PARAMS, (m_g_mix, m_g_ffn, m_lam_re, m_lam_im, m_log_dt, m_ssm_b_re, m_ssm_b_im, m_ssm_c_re,
                              m_ssm_c_im, m_ssm_d, m_w_glu, m_g_kv, m_w_kvf, m_b_f, m_w_q, m_w_o, m_w_ffn_in,
                              m_ffn_conv_w, m_ffn_conv_b, m_w_ffn_out, m_g_final)))
    mom2 = dict(zip(_PARAMS, (v_g_mix, v_g_ffn, v_lam_re, v_lam_im, v_log_dt, v_ssm_b_re, v_ssm_b_im, v_ssm_c_re,
                              v_ssm_c_im, v_ssm_d, v_w_glu, v_g_kv, v_w_kvf, v_b_f, v_w_q, v_w_o, v_w_ffn_in,
                              v_ffn_conv_w, v_ffn_conv_b, v_w_ffn_out, v_g_final)))
    d = x.shape[-1]
    nh = b_f.shape[0]
    slot = 2 * lax.axis_index("x") + lax.axis_index("y")

    wb = {k: _to_bf16(p[k], name=f"to_bf16_{k}") for k in _BIG}
    gd, gcw = _all_gather([ssm_d, ffn_conv_w], name="first_all_gather")
    n_lay, n_s5 = w_ffn_in.shape[0], lam_re.shape[0]
    n_fox = n_lay - n_s5
    groups = {f"stage{l}": [("w_in", l, wb["w_ffn_in"][l]), ("w_out", l, wb["w_ffn_out"][l])] for l in range(n_lay)}
    for l in range(n_s5):
        groups[f"stage{l}"].append(("w_glu", l, wb["w_glu"][l]))
    groups[f"stage{n_s5 - 1}"].append(("w_kvf", 0, wb["w_kvf"]))
    for j in range(n_fox):
        groups[f"stage{n_s5 + j - 1}"].append(("w_q", j, wb["w_q"][j]))
        groups[f"stage{n_s5 + j}"].append(("w_o", j, wb["w_o"][j]))
    w = dict(p)
    w.update(w_glu=[None] * n_s5, w_in=[None] * n_lay, w_out=[None] * n_lay, w_q=[None] * n_fox,
             w_o=[None] * n_fox, w_kv=[None], w_f=[None],
             conv_w=gcw.transpose(1, 2, 0, 3).reshape(n_lay, CONV_TAPS, -1), conv_b=ffn_conv_b,
             ssm_d=gd.transpose(1, 0, 2).reshape(gd.shape[1], d))

    loss_part, grad_x, g = _local_step(x[0], loss_target[0], w, _Net(groups, d, nh))
    loss = lax.psum(loss_part, ("x", "y", "c"))

    small_names = [k for k in _PARAMS if k not in _BIG]
    small_full = dict(g_mix=g["g_mix"], g_ffn=g["g_ffn"], lam_re=g["lam_re"], lam_im=g["lam_im"], log_dt=g["log_dt"],
                      ssm_b_re=g["ssm_b_re"], ssm_b_im=g["ssm_b_im"], ssm_c_re=g["ssm_c_re"], ssm_c_im=g["ssm_c_im"],
                      ssm_d=g["ssm_d"], g_kv=g["g_kv"], b_f=g["b_f"], ffn_conv_w=g["conv_w"],
                      ffn_conv_b=g["conv_b"], g_final=g["g_final"])
    small_list = [small_full[k] for k in small_names]
    pack = _pack(small_list, N_CHIPS * N_CORES * 2 * SUBLANES)
    pack4 = pack.reshape(N_CHIPS, pack.shape[0] // N_CHIPS, LANES)
    pack_shard = _reduce_scatter([pack4], [F32], "small")[0]
    red_big = {k: g["big"][k, 0] if p[k].ndim == 2 else jnp.stack([g["big"][k, l] for l in range(p[k].shape[0])])
               for k in _BIG}
    pack_all = _all_gather([pack_shard], name="small_grads_all_gather")[0]
    red_small = dict(zip(small_names, _unpack(pack_all, small_list)))
    for k in _SMALL_SHARDED:
        width = p[k].shape[-1]
        red_small[k] = lax.dynamic_slice_in_dim(red_small[k], slot * width, width, axis=red_small[k].ndim - 1)

    grads, deltas, new_m, new_v = {}, {}, {}, {}
    for k in _BIG:
        grads[k] = red_big[k]
        deltas[k], new_m[k], new_v[k] = _adamw(p[k], grads[k], mom1[k], mom2[k], name=f"adamw_{k}")
    packs = [_pack([src[k] for k in small_names], SUBLANES) for src in (p, red_small, mom1, mom2)]
    like = [p[k] for k in small_names]
    outs = [_unpack(o, like) for o in _adamw(*packs, name="adamw_small")]
    for i, k in enumerate(small_names):
        grads[k] = red_small[k]
        deltas[k], new_m[k], new_v[k] = outs[0][i], outs[1][i], outs[2][i]
    return (loss, grad_x[None], *[grads[k] for k in _PARAMS], *[deltas[k] for k in _PARAMS],
            *[new_m[k] for k in _PARAMS], *[new_v[k] for k in _PARAMS])
```

```python
import functools

import jax
import jax.numpy as jnp
from jax import lax
from jax.experimental import pallas as pl
from jax.experimental.pallas import tpu as pltpu

F32 = jnp.float32
BF16 = jnp.bfloat16

RMS_EPS = 1e-6
ADAM_LR = 0.001
ADAM_B1 = 0.9
ADAM_B2 = 0.999
ADAM_EPS = 1e-08
ADAM_WD = 0.01
ADAM_STEP = 10
CONV_TAPS = 3

LANES = 128
SUBLANES = 8
HEAD_DIM = 64
FLASH_ROW_TILE = 32
S5_TIME_CHUNK = 1024
S5_BLOCK_GROUPS = 16
VMEM_LIMIT_BYTES = 48 << 20
MM_BLOCK_BUDGET_BYTES = 30 << 20
N_CHIPS = 4
N_CORES = 2
MESH = pl.DeviceIdType.MESH


def _cp(n_grid):
    return pltpu.CompilerParams(dimension_semantics=("arbitrary",) * n_grid, vmem_limit_bytes=VMEM_LIMIT_BYTES)


def _tile(n, pref, mult=SUBLANES):
    if n <= pref:
        return n
    t = (pref // mult) * mult
    while t >= mult:
        if n % t == 0:
            return t
        t -= mult
    return n


class _Comm:
    def __init__(self, ins, out_shapes, sems, start, finish):
        self.ins, self.out_shapes, self.sems, self.start, self.finish = ins, out_shapes, sems, start, finish


def _any_specs(n):
    return [pl.BlockSpec(memory_space=pl.ANY)] * n


def _run_comm(comm, *, name):
    n_in, n_out = len(comm.ins), len(comm.out_shapes)

    def body(*refs):
        ins, outs, sems = refs[:n_in], refs[n_in:n_in + n_out], refs[n_in + n_out:]
        comm.start(ins, outs, sems)
        comm.finish(ins, outs, sems)

    return pl.pallas_call(body, in_specs=_any_specs(n_in), out_specs=_any_specs(n_out),
                          out_shape=list(comm.out_shapes), scratch_shapes=list(comm.sems), name=name)(*comm.ins)


def _call(body, *, grid, in_specs, out_specs, out_shape, args, name, scratch_shapes=(), prefetch=(), comm=None):
    n_pre, n_in, n_out, n_scr = len(prefetch), len(in_specs), len(out_specs), len(scratch_shapes)
    in_specs, out_specs, out_shape = list(in_specs), list(out_specs), list(out_shape)
    scratch_shapes, args = list(scratch_shapes), list(args)
    kernel_body = body
    if comm is not None:
        n_cin, n_cout = len(comm.ins), len(comm.out_shapes)

        def kernel_body(*refs):
            pos = n_pre + n_in
            c_in = refs[pos:pos + n_cin]
            main_out = refs[pos + n_cin:pos + n_cin + n_out]
            pos += n_cin + n_out
            c_out = refs[pos:pos + n_cout]
            main_scr = refs[pos + n_cout:pos + n_cout + n_scr]
            sems = refs[pos + n_cout + n_scr:]
            ids = [pl.program_id(a) for a in range(len(grid))]
            first = functools.reduce(jnp.logical_and, [i == 0 for i in ids])
            last = functools.reduce(jnp.logical_and, [i == g - 1 for i, g in zip(ids, grid)])
            pl.when(first)(lambda: comm.start(c_in, c_out, sems))
            body(*refs[:n_pre + n_in], *main_out, *main_scr)
            pl.when(last)(lambda: comm.finish(c_in, c_out, sems))

        in_specs += _any_specs(n_cin)
        out_specs += _any_specs(n_cout)
        out_shape += list(comm.out_shapes)
        scratch_shapes += list(comm.sems)
        args += list(comm.ins)
    if prefetch:
        spec = pltpu.PrefetchScalarGridSpec(num_scalar_prefetch=n_pre, grid=grid, in_specs=in_specs,
                                            out_specs=out_specs, scratch_shapes=scratch_shapes)
        res = pl.pallas_call(kernel_body, grid_spec=spec, out_shape=out_shape, compiler_params=_cp(len(grid)),
                             name=name)(*prefetch, *args)
    else:
        res = pl.pallas_call(kernel_body, grid=grid, in_specs=in_specs, out_specs=out_specs, out_shape=out_shape,
                             scratch_shapes=scratch_shapes, compiler_params=_cp(len(grid)), name=name)(*args)
    return (res[:n_out], res[n_out:]) if comm is not None else res


def _row_tile(m, bytes_per_row, fixed_bytes):
    for tm in (1024, 512):
        if m % tm == 0 and 2 * (tm * bytes_per_row + fixed_bytes) <= MM_BLOCK_BUDGET_BYTES:
            return tm
    return _tile(m, 512)


def _dot(a, b, ca, cb):
    return lax.dot_general(a, b, (((ca,), (cb,)), ((), ())), preferred_element_type=F32)


def _epilogue_io(epilogue, m, tm, rows_axis, grid_rank):
    _, rows, consts, outs, accs = epilogue

    def at_rows(width):
        return pl.BlockSpec((tm, width), lambda *g: (g[rows_axis], 0))

    def whole(shape):
        return pl.BlockSpec(shape, lambda *g: (0,) * len(shape))

    in_specs = [at_rows(r.shape[1]) for r in rows] + [whole(c.shape) for c in consts]
    out_specs = [at_rows(wd) for wd, _ in outs] + [whole(s) for s in accs]
    out_shape = ([jax.ShapeDtypeStruct((m, wd), dt) for wd, dt in outs]
                 + [jax.ShapeDtypeStruct(s, F32) for s in accs])
    bytes_per_row = (sum(r.shape[1] * r.dtype.itemsize for r in rows)
                     + sum(wd * jnp.dtype(dt).itemsize for wd, dt in outs))
    return in_specs, out_specs, out_shape, bytes_per_row


def _epilogue_apply(epilogue, block, refs, first_row_tile):
    fn, rows, consts, outs, _ = epilogue
    n_in, n_out = len(rows) + len(consts), len(outs)
    res = fn(block, *[r[...] for r in refs[:n_in]])
    for o, val in zip(refs[n_in:n_in + n_out], res[:n_out]):
        o[...] = val.astype(o.dtype)
    a_refs = refs[n_in + n_out:]
    if a_refs:
        @pl.when(first_row_tile)
        def _():
            for a in a_refs:
                a[...] = jnp.zeros_like(a)
        for a, val in zip(a_refs, res[n_out:]):
            a[...] += val


def _mm_cols(x, w4, layer, *, wc, out_dtype=F32, scale=None, epilogue=None, name):
    m, k = x.shape
    slots, _, k0, k1 = w4.shape
    nb = k1 if wc == 0 else k0
    assert (k0 if wc == 0 else k1) == k
    if epilogue is None:
        tm = _row_tile(m, k * x.dtype.itemsize + nb * jnp.dtype(out_dtype).itemsize, k0 * k1 * w4.dtype.itemsize)
        extra_in, out_specs = [], pl.BlockSpec((tm, nb), lambda s, i: (i, s))
        out_shape = jax.ShapeDtypeStruct((m, slots * nb), out_dtype)
    else:
        assert slots == 1
        bytes_per_row = _epilogue_io(epilogue, m, SUBLANES, 1, 2)[3]
        tm = _row_tile(m, k * x.dtype.itemsize + bytes_per_row, k0 * k1 * w4.dtype.itemsize)
        extra_in, out_specs, out_shape, _ = _epilogue_io(epilogue, m, tm, 1, 2)

    def body(x_ref, w_ref, *refs):
        acc = _dot(x_ref[...].astype(BF16), w_ref[0, 0], 1, wc)
        if scale is not None:
            acc = acc * scale
        if epilogue is None:
            refs[0][...] = acc.astype(out_dtype)
        else:
            _epilogue_apply(epilogue, acc, refs, pl.program_id(1) == 0)

    extra_args = [] if epilogue is None else [*epilogue[1], *epilogue[2]]
    return pl.pallas_call(
        body, grid=(slots, m // tm),
        in_specs=[pl.BlockSpec((tm, k), lambda s, i: (i, 0)),
                  pl.BlockSpec((1, 1, k0, k1), lambda s, i: (s, layer, 0, 0)), *extra_in],
        out_specs=out_specs, out_shape=out_shape,
        compiler_params=_cp(2), name=name)(x, w4, *extra_args)


def _planes(a):
    return a if a.ndim == 3 else a[None]


def _mm_acc(x, w4, layer, *, wc, epilogue=None, name):
    x = _planes(x)
    n_planes, m, width = x.shape
    slots, _, k0, k1 = w4.shape
    kb = k0 if wc == 0 else k1
    nout = k1 if wc == 0 else k0
    assert n_planes * width == slots * kb
    spp = slots // n_planes
    x_spec_w = pl.BlockSpec((1, 1, k0, k1), lambda i, s: (s, layer, 0, 0))
    if epilogue is None:
        tm = _row_tile(m, kb * x.dtype.itemsize + nout * 4, k0 * k1 * w4.dtype.itemsize)

        def body(x_ref, w_ref, o_ref):
            @pl.when(pl.program_id(1) == 0)
            def _():
                o_ref[...] = jnp.zeros_like(o_ref)
            o_ref[...] += _dot(x_ref[0].astype(BF16), w_ref[0, 0], 1, wc)

        return pl.pallas_call(
            body, grid=(m // tm, slots),
            in_specs=[pl.BlockSpec((1, tm, kb), lambda i, s: (s // spp, i, s % spp)), x_spec_w],
            out_specs=pl.BlockSpec((tm, nout), lambda i, s: (i, 0)),
            out_shape=jax.ShapeDtypeStruct((m, nout), F32),
            compiler_params=_cp(2), name=name)(x, w4)

    bytes_per_row = _epilogue_io(epilogue, m, SUBLANES, 0, 2)[3]
    tm = _row_tile(m, kb * x.dtype.itemsize + nout * 2 + bytes_per_row, k0 * k1 * w4.dtype.itemsize)
    extra_in, out_specs, out_shape, _ = _epilogue_io(epilogue, m, tm, 0, 2)

    def body(x_ref, w_ref, *refs):
        acc = refs[-1]

        @pl.when(pl.program_id(1) == 0)
        def _():
            acc[...] = jnp.zeros_like(acc)
        acc[...] += _dot(x_ref[0].astype(BF16), w_ref[0, 0], 1, wc)

        @pl.when(pl.program_id(1) == slots - 1)
        def _():
            _epilogue_apply(epilogue, acc[...], refs[:-1], pl.program_id(0) == 0)

    return pl.pallas_call(
        body, grid=(m // tm, slots),
        in_specs=[pl.BlockSpec((1, tm, kb), lambda i, s: (s // spp, i, s % spp)), x_spec_w, *extra_in],
        out_specs=out_specs, out_shape=out_shape, scratch_shapes=[pltpu.VMEM((tm, nout), F32)],
        compiler_params=_cp(2), name=name)(x, w4, *epilogue[1], *epilogue[2])


def _mm_tn(x, dy, slots, *, scale=None, wire=False, name):
    m, k = x.shape
    dy = _planes(dy)
    n_planes, _, width = dy.shape
    n = n_planes * width // slots
    spp = slots // n_planes
    ta = _tile(k, 512, LANES)
    tm = m
    while tm > 512 and tm % 2 == 0 and (2 * tm * (ta * x.dtype.itemsize + n * dy.dtype.itemsize)
                                         + 2 * ta * n * 4) > MM_BLOCK_BUDGET_BYTES:
        tm //= 2
    n_m = m // tm

    def body(x_ref, dy_ref, o_ref, *wire_ref):
        @pl.when(pl.program_id(2) == 0)
        def _():
            o_ref[...] = jnp.zeros_like(o_ref)
        o_ref[0] += _dot(x_ref[...].astype(BF16), dy_ref[0].astype(BF16), 0, 0)
        if scale is not None or wire:
            @pl.when(pl.program_id(2) == n_m - 1)
            def _():
                if scale is not None:
                    o_ref[...] = o_ref[...] * scale
                if wire:
                    wire_ref[0][...] = o_ref[...].astype(BF16)

    out_spec = pl.BlockSpec((1, ta, n), lambda s, a, i: (s, a, 0))
    return pl.pallas_call(
        body, grid=(slots, k // ta, n_m),
        in_specs=[pl.BlockSpec((tm, ta), lambda s, a, i: (i, a)),
                  pl.BlockSpec((1, tm, n), lambda s, a, i: (s // spp, i, s % spp))],
        out_specs=[out_spec, out_spec] if wire else out_spec,
        out_shape=([jax.ShapeDtypeStruct((slots, k, n), dt) for dt in (F32, BF16)] if wire
                   else jax.ShapeDtypeStruct((slots, k, n), F32)),
        compiler_params=_cp(3), name=name)(x, dy)


def _rowwise(fn, rows, consts, outs, accs=(), *, tl=256, name):
    n_rows = rows[0].shape[0]
    tl = _tile(n_rows, tl)
    n_in = len(rows) + len(consts)
    n_out = len(outs)

    def body(*refs):
        res = fn(*[r[...] for r in refs[:n_in]])
        res = res if isinstance(res, (tuple, list)) else (res,)
        o_refs = refs[n_in:n_in + n_out]
        a_refs = refs[n_in + n_out:]
        for o, val in zip(o_refs, res[:n_out]):
            o[...] = val.astype(o.dtype)
        if a_refs:
            @pl.when(pl.program_id(0) == 0)
            def _():
                for a in a_refs:
                    a[...] = jnp.zeros_like(a)
            for a, val in zip(a_refs, res[n_out:]):
                a[...] += val

    in_specs = ([pl.BlockSpec((tl, r.shape[1]), lambda i: (i, 0)) for r in rows]
                + [pl.BlockSpec(c.shape, lambda i: (0, 0)) for c in consts])
    out_specs = ([pl.BlockSpec((tl, w), lambda i: (i, 0)) for w, _ in outs]
                 + [pl.BlockSpec(s, lambda i: (0, 0)) for s in accs])
    out_shape = ([jax.ShapeDtypeStruct((n_rows, w), dt) for w, dt in outs]
                 + [jax.ShapeDtypeStruct(s, F32) for s in accs])
    return pl.pallas_call(body, grid=(n_rows // tl,), in_specs=in_specs, out_specs=out_specs,
                          out_shape=out_shape, compiler_params=_cp(1), name=name)(*rows, *consts)


def _rms(x, g):
    return x * lax.rsqrt(jnp.mean(x * x, axis=-1, keepdims=True) + RMS_EPS) * g


def _sigmoid(x):
    return 1.0 / (1.0 + jnp.exp(-x))


def _glu(zz):
    d = zz.shape[1] // 2
    return zz[:, :d] * _sigmoid(zz[:, d:])


def _gelu(y):
    return jax.nn.gelu(y)


def _row2(v):
    return v.reshape(1, -1)


def _node_bwd_fn(counts):
    n_dy = sum(counts)

    def fn(d, hh, *rest):
        dys, gs = rest[:n_dy], rest[n_dy:]
        tot, dgs, pos = d, [], 0
        for g, cnt in zip(gs, counts):
            dy = dys[pos].astype(F32)
            for extra in dys[pos + 1:pos + cnt]:
                dy = dy + extra.astype(F32)
            pos += cnt
            _, vjp = jax.vjp(_rms, hh, g)
            dx, dg = vjp(dy)
            tot = tot + dx
            dgs.append(dg)
        return (tot, tot, *dgs)

    return fn


def _node_bwd(d_in, h, branches, *, name):
    width = h.shape[1]
    flat = [dy for _, dys in branches for dy in dys]
    res = _rowwise(_node_bwd_fn([len(dys) for _, dys in branches]), [d_in, h, *flat],
                   [_row2(g) for g, _ in branches], [(width, F32), (width, BF16)], [(1, width)] * len(branches),
                   name=name)
    return res[0], res[1], [r[0] for r in res[2:]]


def _s5_prep_fn(lr, li, ldt, br, bi, cr, ci, *, gq, h, p):
    dt = jnp.exp(ldt)
    mag = jnp.exp(lr * dt)
    lb_re = mag * jnp.cos(li * dt)
    lb_im = mag * jnp.sin(li * dt)
    den = lr * lr + li * li
    nr = lb_re - 1.0
    fr = (nr * lr + lb_im * li) / den
    fi = (lb_im * lr - nr * li) / den
    bb_re = fr * br - fi * bi
    bb_im = fr * bi + fi * br
    shape = (gq * h, gq * p)
    r = lax.broadcasted_iota(jnp.int32, shape, 0)
    c = lax.broadcasted_iota(jnp.int32, shape, 1)
    mask = jnp.where(jnp.right_shift(r, h.bit_length() - 1) == jnp.right_shift(c, p.bit_length() - 1), 1.0, 0.0)

    def expand(t):
        return jnp.concatenate([t] * gq, axis=0) * mask

    return lb_re, lb_im, expand(bb_re), expand(bb_im), expand(cr), expand(ci)


def _s5_prep(lr, li, ldt, br, bi, cr, ci, p, *, name):
    n = lr.shape[1]
    h = br.shape[0]
    gq = S5_BLOCK_GROUPS
    nq, cq = gq * p, gq * h
    nblk = n // nq
    fn = functools.partial(_s5_prep_fn, gq=gq, h=h, p=p)

    def body(lr_r, li_r, ldt_r, br_r, bi_r, cr_r, ci_r, lbr_o, lbi_o, wbr_o, wbi_o, wcr_o, wci_o):
        lb_re, lb_im, wbr, wbi, wcr, wci = fn(lr_r[...], li_r[...], ldt_r[...], br_r[...], bi_r[...],
                                              cr_r[...], ci_r[...])
        lbr_o[...] = lb_re
        lbi_o[...] = lb_im
        wbr_o[0] = wbr.astype(BF16)
        wbi_o[0] = wbi.astype(BF16)
        wcr_o[0] = wcr.astype(BF16)
        wci_o[0] = wci.astype(BF16)

    vec = pl.BlockSpec((1, nq), lambda q: (0, q))
    tab = pl.BlockSpec((h, nq), lambda q: (0, q))
    wsp = pl.BlockSpec((1, cq, nq), lambda q: (q, 0, 0))
    wsh = jax.ShapeDtypeStruct((nblk, cq, nq), BF16)
    vsh = jax.ShapeDtypeStruct((1, n), F32)
    return pl.pallas_call(body, grid=(nblk,), in_specs=[vec, vec, vec, tab, tab, tab, tab],
                          out_specs=[vec, vec, wsp, wsp, wsp, wsp], out_shape=[vsh, vsh, wsh, wsh, wsh, wsh],
                          compiler_params=_cp(1), name=name)(lr, li, ldt, br, bi, cr, ci)


def _s5_prep_bwd(lr, li, ldt, br, bi, cr, ci, p, dlbr, dlbi, dwbr, dwbi, dwcr, dwci, *, name):
    n = lr.shape[1]
    h = br.shape[0]
    gq = S5_BLOCK_GROUPS
    nq, cq = gq * p, gq * h
    nblk = n // nq
    fn = functools.partial(_s5_prep_fn, gq=gq, h=h, p=p)

    def body(lr_r, li_r, ldt_r, br_r, bi_r, cr_r, ci_r, dlbr_r, dlbi_r, dwbr_r, dwbi_r, dwcr_r, dwci_r,
             *outs):
        _, vjp = jax.vjp(fn, lr_r[...], li_r[...], ldt_r[...], br_r[...], bi_r[...], cr_r[...], ci_r[...])
        grads = vjp((dlbr_r[0], dlbi_r[0], dwbr_r[0], dwbi_r[0], dwcr_r[0], dwci_r[0]))
        for o, g in zip(outs, grads):
            o[...] = g

    vec = pl.BlockSpec((1, nq), lambda q: (0, q))
    tab = pl.BlockSpec((h, nq), lambda q: (0, q))
    vec3 = pl.BlockSpec((1, 1, nq), lambda q: (q, 0, 0))
    wsp = pl.BlockSpec((1, cq, nq), lambda q: (q, 0, 0))
    vsh = jax.ShapeDtypeStruct((1, n), F32)
    tsh = jax.ShapeDtypeStruct((h, n), F32)
    return pl.pallas_call(body, grid=(nblk,),
                          in_specs=[vec, vec, vec, tab, tab, tab, tab, vec3, vec3, wsp, wsp, wsp, wsp],
                          out_specs=[vec, vec, vec, tab, tab, tab, tab],
                          out_shape=[vsh, vsh, vsh, tsh, tsh, tsh, tsh],
                          compiler_params=_cp(1), name=name)(lr, li, ldt, br, bi, cr, ci,
                                                             dlbr, dlbi, dwbr, dwbi, dwcr, dwci)


def _scan_rows(s_re, s_im, a_re, a_im, c_re, c_im, *, reverse):
    t_rows, n = s_re.shape
    nb = t_rows // SUBLANES
    row = lax.broadcasted_iota(jnp.int32, (SUBLANES, n), 0)

    def cmul(x, y):
        return x[0] * y[0] - x[1] * y[1], x[0] * y[1] + x[1] * y[0]

    a1 = (jnp.broadcast_to(a_re, (SUBLANES, n)), jnp.broadcast_to(a_im, (SUBLANES, n)))
    a2 = cmul(a1, a1)
    a4 = cmul(a2, a2)
    steps = []
    for dist, (pr, pi) in ((1, a1), (2, a2), (4, a4)):
        keep = (row < SUBLANES - dist) if reverse else (row >= dist)
        steps.append((SUBLANES - dist if reverse else dist, (jnp.where(keep, pr, 0.0), jnp.where(keep, pi, 0.0))))
    pk = (a_re, a_im)
    tab_re = jnp.zeros((SUBLANES, n), F32)
    tab_im = jnp.zeros((SUBLANES, n), F32)
    for i in range(SUBLANES):
        at = (SUBLANES - 1 - i) if reverse else i
        tab_re = jnp.where(row == at, pk[0], tab_re)
        tab_im = jnp.where(row == at, pk[1], tab_im)
        pk = cmul(pk, (a_re, a_im))

    def step(b, carry):
        cr, ci = carry
        blk = (nb - 1 - b) if reverse else b
        off = pl.multiple_of(blk * SUBLANES, SUBLANES)
        x_re = s_re[pl.ds(off, SUBLANES), :]
        x_im = s_im[pl.ds(off, SUBLANES), :]
        for sh, (pr, pi) in steps:
            sh_re = pltpu.roll(x_re, sh, 0)
            sh_im = pltpu.roll(x_im, sh, 0)
            x_re, x_im = x_re + pr * sh_re - pi * sh_im, x_im + pr * sh_im + pi * sh_re
        x_re, x_im = x_re + tab_re * cr - tab_im * ci, x_im + tab_re * ci + tab_im * cr
        s_re[pl.ds(off, SUBLANES), :] = x_re
        s_im[pl.ds(off, SUBLANES), :] = x_im
        edge = 0 if reverse else SUBLANES - 1
        return x_re[edge:edge + 1, :], x_im[edge:edge + 1, :]

    return lax.fori_loop(0, nb, step, (c_re, c_im))


def _s5_fwd(u, prep, dskip, *, name, comm=None):
    lb_re, lb_im, wbr, wbi, wcr, wci = prep
    n_rows, _ = u.shape
    nblk, cq, nq = wbr.shape
    tt = _tile(n_rows, S5_TIME_CHUNK)
    nch = n_rows // tt

    def body(u_ref, wbr_r, wbi_r, wcr_r, wci_r, lbr_r, lbi_r, d_ref, y_ref, z_ref, s_re, s_im, sbr_o, sbi_o,
             c_re, c_im):
        @pl.when(pl.program_id(1) == 0)
        def _():
            c_re[...] = jnp.zeros_like(c_re)
            c_im[...] = jnp.zeros_like(c_im)
        uf = u_ref[...]
        ub = uf.astype(BF16)
        s_re[...] = _dot(ub, wbr_r[0], 1, 0)
        s_im[...] = _dot(ub, wbi_r[0], 1, 0)
        sbr_o[0] = c_re[...]
        sbi_o[0] = c_im[...]
        cr, ci = _scan_rows(s_re, s_im, lbr_r[...], lbi_r[...], c_re[...], c_im[...], reverse=False)
        c_re[...] = cr
        c_im[...] = ci
        y = _dot(s_re[...].astype(BF16), wcr_r[0], 1, 1) - _dot(s_im[...].astype(BF16), wci_r[0], 1, 1)
        y = y + d_ref[...] * uf
        y_ref[...] = y
        z_ref[...] = _gelu(y).astype(BF16)

    wsp = pl.BlockSpec((1, cq, nq), lambda q, i: (q, 0, 0))
    vec = pl.BlockSpec((1, nq), lambda q, i: (0, q))
    act = pl.BlockSpec((tt, cq), lambda q, i: (i, q))
    sb = pl.BlockSpec((1, 1, nq), lambda q, i: (i, 0, q))
    sbsh = jax.ShapeDtypeStruct((nch, 1, nblk * nq), F32)
    states = pl.BlockSpec((tt, nq), lambda q, i: (i, q))
    stsh = jax.ShapeDtypeStruct((n_rows, nblk * nq), F32)
    return _call(
        body, grid=(nblk, nch),
        in_specs=[act, wsp, wsp, wsp, wsp, vec, vec, pl.BlockSpec((1, cq), lambda q, i: (0, q))],
        out_specs=[act, act, states, states, sb, sb],
        out_shape=[jax.ShapeDtypeStruct(u.shape, F32), jax.ShapeDtypeStruct(u.shape, BF16), stsh, stsh, sbsh, sbsh],
        scratch_shapes=[pltpu.VMEM((1, nq), F32), pltpu.VMEM((1, nq), F32)],
        args=(u, wbr, wbi, wcr, wci, lb_re, lb_im, dskip), name=name, comm=comm)


def _s5_bwd(u, dy, st_re, st_im, sb_re, sb_im, prep, dskip, *, name, comm=None):
    lb_re, lb_im, wbr, wbi, wcr, wci = prep
    n_rows, _ = u.shape
    nblk, cq, nq = wbr.shape
    tt = _tile(n_rows, S5_TIME_CHUNK)
    nch = n_rows // tt

    def body(u_ref, dy_ref, s_re, s_im, sbr_r, sbi_r, wbr_r, wbi_r, wcr_r, wci_r, lbr_r, lbi_r, d_ref,
             du_ref, dwbr, dwbi, dwcr, dwci, dlbr, dlbi, dd_ref, g_re, g_im, lc_re, lc_im):
        @pl.when(pl.program_id(1) == 0)
        def _():
            for ref in (lc_re, lc_im, dwbr, dwbi, dwcr, dwci, dlbr, dlbi, dd_ref):
                ref[...] = jnp.zeros_like(ref)
        uf = u_ref[...]
        ub = uf.astype(BF16)
        dyf = dy_ref[...]
        dyb = dyf.astype(BF16)
        sr16 = s_re[...].astype(BF16)
        si16 = s_im[...].astype(BF16)
        dwcr[0] += _dot(dyb, sr16, 0, 0)
        dwci[0] -= _dot(dyb, si16, 0, 0)
        g_re[...] = _dot(dyb, wcr_r[0], 1, 0)
        g_im[...] = -_dot(dyb, wci_r[0], 1, 0)
        lcr, lci = _scan_rows(g_re, g_im, lbr_r[...], -lbi_r[...], lc_re[...], lc_im[...], reverse=True)
        lc_re[...] = lcr
        lc_im[...] = lci
        lam_r = g_re[...]
        lam_i = g_im[...]
        first = lax.broadcasted_iota(jnp.int32, (tt, nq), 0) == 0
        prev_r = jnp.where(first, sbr_r[0], pltpu.roll(s_re[...], 1, 0))
        prev_i = jnp.where(first, sbi_r[0], pltpu.roll(s_im[...], 1, 0))
        dlbr[0] += jnp.sum(lam_r * prev_r + lam_i * prev_i, axis=0, keepdims=True)
        dlbi[0] += jnp.sum(lam_i * prev_r - lam_r * prev_i, axis=0, keepdims=True)
        lr16 = lam_r.astype(BF16)
        li16 = lam_i.astype(BF16)
        du_ref[...] = _dot(lr16, wbr_r[0], 1, 1) + _dot(li16, wbi_r[0], 1, 1) + d_ref[...] * dyf
        dwbr[0] += _dot(ub, lr16, 0, 0)
        dwbi[0] += _dot(ub, li16, 0, 0)
        dd_ref[0] += jnp.sum(dyf * uf, axis=0, keepdims=True)

    last = nch - 1
    wsp = pl.BlockSpec((1, cq, nq), lambda q, i: (q, 0, 0))
    vec = pl.BlockSpec((1, nq), lambda q, i: (0, q))
    act = pl.BlockSpec((tt, cq), lambda q, i: (last - i, q))
    sb = pl.BlockSpec((1, 1, nq), lambda q, i: (last - i, 0, q))
    vec3 = pl.BlockSpec((1, 1, nq), lambda q, i: (q, 0, 0))
    dsp = pl.BlockSpec((1, 1, cq), lambda q, i: (q, 0, 0))
    wsh = jax.ShapeDtypeStruct((nblk, cq, nq), F32)
    v3sh = jax.ShapeDtypeStruct((nblk, 1, nq), F32)
    big = pltpu.VMEM((tt, nq), F32)
    states = pl.BlockSpec((tt, nq), lambda q, i: (last - i, q))
    return _call(
        body, grid=(nblk, nch),
        in_specs=[act, act, states, states, sb, sb, wsp, wsp, wsp, wsp, vec, vec,
                  pl.BlockSpec((1, cq), lambda q, i: (0, q))],
        out_specs=[act, wsp, wsp, wsp, wsp, vec3, vec3, dsp],
        out_shape=[jax.ShapeDtypeStruct(u.shape, F32), wsh, wsh, wsh, wsh, v3sh, v3sh,
                   jax.ShapeDtypeStruct((nblk, 1, cq), F32)],
        scratch_shapes=[big, big, pltpu.VMEM((1, nq), F32), pltpu.VMEM((1, nq), F32)],
        args=(u, dy, st_re, st_im, sb_re, sb_im, wbr, wbi, wcr, wci, lb_re, lb_im, dskip), name=name, comm=comm)


def _conv_taps(cur, prev, w, b):
    ext = jnp.concatenate([prev, cur], axis=0)
    x1 = pltpu.roll(ext, 1, 0)[SUBLANES:, :]
    x2 = pltpu.roll(ext, 2, 0)[SUBLANES:, :]
    return b + x2 * w[0:1, :] + x1 * w[1:2, :] + cur * w[2:3, :], x1, x2


def _conv_fwd(uu, cw, cb, *, name):
    n_rows, f2 = uu.shape
    f = f2 // 2
    tc = _tile(f, 1408, LANES)
    tl = _tile(n_rows, 256)
    nfb = f // tc

    def body(g_ref, u_ref, wg_ref, wu_ref, bg_ref, bu_ref, o_ref, pg, pu):
        @pl.when(pl.program_id(1) == 0)
        def _():
            pg[...] = jnp.zeros_like(pg)
            pu[...] = jnp.zeros_like(pu)
        gcur = g_ref[...]
        ucur = u_ref[...]
        cg, _, _ = _conv_taps(gcur, pg[...], wg_ref[...], bg_ref[...])
        cu, _, _ = _conv_taps(ucur, pu[...], wu_ref[...], bu_ref[...])
        o_ref[...] = (cg * _sigmoid(cg) * cu).astype(o_ref.dtype)
        pg[...] = gcur[tl - SUBLANES:, :]
        pu[...] = ucur[tl - SUBLANES:, :]

    return pl.pallas_call(
        body, grid=(nfb, n_rows // tl),
        in_specs=[pl.BlockSpec((tl, tc), lambda j, i: (i, j)), pl.BlockSpec((tl, tc), lambda j, i: (i, j + nfb)),
                  pl.BlockSpec((CONV_TAPS, tc), lambda j, i: (0, j)),
                  pl.BlockSpec((CONV_TAPS, tc), lambda j, i: (0, j + nfb)),
                  pl.BlockSpec((1, tc), lambda j, i: (0, j)), pl.BlockSpec((1, tc), lambda j, i: (0, j + nfb))],
        out_specs=pl.BlockSpec((tl, tc), lambda j, i: (i, j)),
        out_shape=jax.ShapeDtypeStruct((n_rows, f), BF16),
        scratch_shapes=[pltpu.VMEM((SUBLANES, tc), F32), pltpu.VMEM((SUBLANES, tc), F32)],
        compiler_params=_cp(2), name=name)(uu, uu, cw, cw, cb, cb)


def _conv_bwd(uu, dact, cw, cb, *, name):
    n_rows, f2 = uu.shape
    f = f2 // 2
    tc = _tile(f, 1408, LANES)
    tl = _tile(n_rows, 256)
    nfb = f // tc
    nrb = n_rows // tl
    halo_per_tile = tl // SUBLANES

    def body(g_ref, gh_ref, u_ref, uh_ref, da_ref, wg_ref, wu_ref, bg_ref, bu_ref,
             duu_ref, dw_ref, db_ref, nxt_g, nxt_u):
        i = pl.program_id(1)
        rb = nrb - 1 - i

        @pl.when(i == 0)
        def _():
            for ref in (nxt_g, nxt_u, dw_ref, db_ref):
                ref[...] = jnp.zeros_like(ref)
        has_prev = jnp.where(rb > 0, 1.0, 0.0)
        gcur, ucur = g_ref[...], u_ref[...]
        wg, wu = wg_ref[...], wu_ref[...]
        cg, g1, g2 = _conv_taps(gcur, gh_ref[...] * has_prev, wg, bg_ref[...])
        cu, u1, u2 = _conv_taps(ucur, uh_ref[...] * has_prev, wu, bu_ref[...])
        sg = _sigmoid(cg)
        silu = cg * sg
        da = da_ref[...]

        def transpose_conv(plane, d, cur, x1, x2, w, nxt):
            ext = jnp.concatenate([d, nxt[...]], axis=0)
            d1 = pltpu.roll(ext, tl + SUBLANES - 1, 0)[:tl, :]
            d2 = pltpu.roll(ext, tl + SUBLANES - 2, 0)[:tl, :]
            duu_ref[plane] = (w[2:3, :] * d + w[1:2, :] * d1 + w[0:1, :] * d2).astype(duu_ref.dtype)
            nxt[...] = d[0:SUBLANES, :]
            dw_ref[plane] += jnp.concatenate([jnp.sum(d * x2, axis=0, keepdims=True),
                                              jnp.sum(d * x1, axis=0, keepdims=True),
                                              jnp.sum(d * cur, axis=0, keepdims=True)], axis=0)
            db_ref[plane] += jnp.sum(d, axis=0, keepdims=True)

        transpose_conv(0, da * cu * (sg * (1.0 + cg * (1.0 - sg))), gcur, g1, g2, wg, nxt_g)
        transpose_conv(1, da * silu, ucur, u1, u2, wu, nxt_u)

    def halo(j, i):
        return jnp.maximum((nrb - 1 - i) * halo_per_tile - 1, 0)

    return pl.pallas_call(
        body, grid=(nfb, nrb),
        in_specs=[pl.BlockSpec((tl, tc), lambda j, i: (nrb - 1 - i, j)),
                  pl.BlockSpec((SUBLANES, tc), lambda j, i: (halo(j, i), j)),
                  pl.BlockSpec((tl, tc), lambda j, i: (nrb - 1 - i, j + nfb)),
                  pl.BlockSpec((SUBLANES, tc), lambda j, i: (halo(j, i), j + nfb)),
                  pl.BlockSpec((tl, tc), lambda j, i: (nrb - 1 - i, j)),
                  pl.BlockSpec((CONV_TAPS, tc), lambda j, i: (0, j)),
                  pl.BlockSpec((CONV_TAPS, tc), lambda j, i: (0, j + nfb)),
                  pl.BlockSpec((1, tc), lambda j, i: (0, j)),
                  pl.BlockSpec((1, tc), lambda j, i: (0, j + nfb))],
        out_specs=[pl.BlockSpec((2, tl, tc), lambda j, i: (0, nrb - 1 - i, j)),
                   pl.BlockSpec((2, CONV_TAPS, tc), lambda j, i: (0, 0, j)),
                   pl.BlockSpec((2, 1, tc), lambda j, i: (0, 0, j))],
        out_shape=[jax.ShapeDtypeStruct((2, n_rows, f), BF16), jax.ShapeDtypeStruct((2, CONV_TAPS, f), F32),
                   jax.ShapeDtypeStruct((2, 1, f), F32)],
        scratch_shapes=[pltpu.VMEM((SUBLANES, tc), F32), pltpu.VMEM((SUBLANES, tc), F32)],
        compiler_params=_cp(2), name=name)(uu, uu, uu, uu, dact, cw, cw, cb, cb)


def _log_sigmoid(x):
    t = jnp.exp(-jnp.abs(x))
    log1p_t = jnp.where(t < 1e-3, t * (1.0 - t * (0.5 - t * (1.0 / 3.0))), jnp.log(1.0 + t))
    return jnp.minimum(x, 0.0) - log1p_t


def _dlog_sigmoid(x):
    t = jnp.exp(-jnp.abs(x))
    return jnp.where(x >= 0, t, 1.0) / (1.0 + t)


def _tri_dot(tri, x):
    return jnp.dot(tri, x, precision=lax.Precision.HIGHEST, preferred_element_type=F32)


def _cum_fwd(fl, bf, *, name):
    n_rows, width = fl.shape
    tc = _tile(n_rows, 256)

    def body(fl_ref, bf_ref, o_ref, carry):
        @pl.when(pl.program_id(0) == 0)
        def _():
            carry[...] = jnp.zeros_like(carry)
        x = _log_sigmoid(fl_ref[...] + bf_ref[...])
        r = lax.broadcasted_iota(jnp.int32, (tc, tc), 0)
        c = lax.broadcasted_iota(jnp.int32, (tc, tc), 1)
        y = _tri_dot(jnp.where(r >= c, 1.0, 0.0), x) + carry[...]
        o_ref[...] = y
        carry[...] = y[tc - 1:tc, :]

    return pl.pallas_call(
        body, grid=(n_rows // tc,),
        in_specs=[pl.BlockSpec((tc, width), lambda i: (i, 0)), pl.BlockSpec((1, width), lambda i: (0, 0))],
        out_specs=pl.BlockSpec((tc, width), lambda i: (i, 0)),
        out_shape=jax.ShapeDtypeStruct(fl.shape, F32),
        scratch_shapes=[pltpu.VMEM((1, width), F32)], compiler_params=_cp(1), name=name)(fl, bf)


def _cum_bwd(dcum, fl, bf, *, name):
    n_rows, width = fl.shape
    tc = _tile(n_rows, 256)
    last = n_rows // tc - 1

    def body(dc_ref, fl_ref, bf_ref, dfl_ref, dbf_ref, carry):
        @pl.when(pl.program_id(0) == 0)
        def _():
            carry[...] = jnp.zeros_like(carry)
            dbf_ref[...] = jnp.zeros_like(dbf_ref)
        r = lax.broadcasted_iota(jnp.int32, (tc, tc), 0)
        c = lax.broadcasted_iota(jnp.int32, (tc, tc), 1)
        dls = _tri_dot(jnp.where(r <= c, 1.0, 0.0), dc_ref[...]) + carry[...]
        carry[...] = dls[0:1, :]
        dfl = dls * _dlog_sigmoid(fl_ref[...] + bf_ref[...])
        dfl_ref[...] = dfl.astype(dfl_ref.dtype)
        dbf_ref[...] += jnp.sum(dfl, axis=0, keepdims=True)

    return pl.pallas_call(
        body, grid=(n_rows // tc,),
        in_specs=[pl.BlockSpec((tc, width), lambda i: (last - i, 0)),
                  pl.BlockSpec((tc, width), lambda i: (last - i, 0)),
                  pl.BlockSpec((1, width), lambda i: (0, 0))],
        out_specs=[pl.BlockSpec((tc, width), lambda i: (last - i, 0)), pl.BlockSpec((1, width), lambda i: (0, 0))],
        out_shape=[jax.ShapeDtypeStruct(fl.shape, BF16), jax.ShapeDtypeStruct((1, width), F32)],
        scratch_shapes=[pltpu.VMEM((1, width), F32)], compiler_params=_cp(1), name=name)(dcum, fl, bf)


def _head_masks():
    lane = lax.broadcasted_iota(jnp.int32, (1, LANES), 1)
    return (lane < HEAD_DIM, lane >= HEAD_DIM)


def _flash_fwd(q, kv, cum_c, cum_r, *, tq, name, comm=None):
    n_rows, d = q.shape
    nhp = d // LANES
    tk = tq
    nq = n_rows // tq
    rt = _tile(tk, FLASH_ROW_TILE)
    reps = (1, tq // LANES)

    def body(qi_ref, kj_ref, q_ref, k_ref, v_ref, cq_ref, ck_ref, ot_ref, lse_ref, m0, m1, l0, l1, acc,
             s0, s1, p0, p1, b0, b1):
        i = qi_ref[pl.program_id(1)]
        j = kj_ref[pl.program_id(1)]
        ms, ls = (m0, m1), (l0, l1)
        head_rows = lax.broadcasted_iota(jnp.int32, (LANES, 1), 0) < HEAD_DIM

        @pl.when(j == 0)
        def _():
            for h in range(2):
                ms[h][...] = jnp.full_like(ms[h], -jnp.inf)
                ls[h][...] = jnp.zeros_like(ls[h])
            acc[...] = jnp.zeros_like(acc)

        def block(diagonal):
            qv, kk, vv = q_ref[...], k_ref[...], v_ref[...]
            a = acc[...]
            for h, msk in enumerate(_head_masks()):
                st_sc, pt_sc, bias_sc = ((s0, p0, b0), (s1, p1, b1))[h]
                st_sc[...] = _dot(kk, jnp.where(msk, qv, jnp.zeros_like(qv)), 1, 1)
                bias_sc[...] = jnp.broadcast_to(cq_ref[0, h:h + 1, 0:1] - ck_ref[0, :, h:h + 1], (tk, LANES))
                m_old, l_old = ms[h][...], ls[h][...]
                col_max = jnp.full((SUBLANES, tq), -jnp.inf, F32)
                for r in range(tk // rt):
                    rows = slice(r * rt, (r + 1) * rt)
                    s = st_sc[rows, :] + jnp.tile(bias_sc[rows, :], reps)
                    if diagonal:
                        key = r * rt + lax.broadcasted_iota(jnp.int32, (rt, tq), 0)
                        qry = lax.broadcasted_iota(jnp.int32, (rt, tq), 1)
                        s = jnp.where(key <= qry, s, -jnp.inf)
                    st_sc[rows, :] = s
                    for g in range(rt // SUBLANES):
                        col_max = jnp.maximum(col_max, s[g * SUBLANES:(g + 1) * SUBLANES, :])
                m_new = jnp.maximum(m_old, jnp.max(col_max, axis=0, keepdims=True))
                col_sum = jnp.zeros((SUBLANES, tq), F32)
                for r in range(tk // rt):
                    rows = slice(r * rt, (r + 1) * rt)
                    p = jnp.exp(st_sc[rows, :] - m_new)
                    for g in range(rt // SUBLANES):
                        col_sum = col_sum + p[g * SUBLANES:(g + 1) * SUBLANES, :]
                    pt_sc[rows, :] = p.astype(BF16)
                alpha = jnp.exp(m_old - m_new)
                ms[h][...] = m_new
                ls[h][...] = alpha * l_old + jnp.sum(col_sum, axis=0, keepdims=True)
                pv_t = _dot(jnp.where(msk, vv, jnp.zeros_like(vv)), pt_sc[...], 0, 0)
                a = a * jnp.where(head_rows == (h == 0), alpha, 1.0) + pv_t
            acc[...] = a

        pl.when(j < i)(functools.partial(block, False))
        pl.when(j == i)(functools.partial(block, True))

        @pl.when(j == i)
        def _():
            ot_ref[...] = (acc[...] * jnp.where(head_rows, 1.0 / l0[...], 1.0 / l1[...])).T
            lse_ref[0] = jnp.concatenate([m0[...] + jnp.log(l0[...]), m1[...] + jnp.log(l1[...])], axis=0)

    pairs = [(i, j) for i in range(nq) for j in range(i + 1)]
    qi = jnp.asarray([i for i, _ in pairs], jnp.int32)
    kj = jnp.asarray([j for _, j in pairs], jnp.int32)
    stat = pltpu.VMEM((1, tq), F32)
    return _call(
        body, grid=(nhp, len(pairs)), prefetch=(qi, kj),
        in_specs=[pl.BlockSpec((tq, LANES), lambda hp, t, qi, kj: (qi[t], hp)),
                  pl.BlockSpec((tk, LANES), lambda hp, t, qi, kj: (kj[t], hp)),
                  pl.BlockSpec((tk, LANES), lambda hp, t, qi, kj: (kj[t], nhp + hp)),
                  pl.BlockSpec((1, 2, tq), lambda hp, t, qi, kj: (hp, 0, qi[t])),
                  pl.BlockSpec((1, tk, 2), lambda hp, t, qi, kj: (hp, kj[t], 0))],
        out_specs=[pl.BlockSpec((tq, LANES), lambda hp, t, qi, kj: (qi[t], hp)),
                   pl.BlockSpec((1, 2, tq), lambda hp, t, qi, kj: (hp, 0, qi[t]))],
        scratch_shapes=[stat, stat, stat, stat, pltpu.VMEM((LANES, tq), F32), pltpu.VMEM((tk, tq), F32),
                        pltpu.VMEM((tk, tq), F32), pltpu.VMEM((tk, tq), BF16), pltpu.VMEM((tk, tq), BF16),
                        pltpu.VMEM((tk, LANES), F32), pltpu.VMEM((tk, LANES), F32)],
        out_shape=[jax.ShapeDtypeStruct((n_rows, d), F32), jax.ShapeDtypeStruct((nhp, 2, n_rows), F32)],
        args=(q, kv, kv, cum_r, cum_c), name=name, comm=comm)


def _head_delta(dd, oo):
    d = oo.shape[1]
    prod = dd.astype(BF16).astype(F32) * oo
    r = lax.broadcasted_iota(jnp.int32, (d, LANES), 0)
    c = lax.broadcasted_iota(jnp.int32, (d, LANES), 1)
    return _tri_dot(prod, jnp.where(jnp.right_shift(r, HEAD_DIM.bit_length() - 1) == c, 1.0, 0.0))


def _flash_bwd(q, kv, lse_r, delta_r, do, cum_c, cum_r, *, tq, name, comm=None):
    n_rows, d = q.shape
    nhp = d // LANES
    tk = tq
    nq = n_rows // tq
    rt = _tile(tk, FLASH_ROW_TILE)
    reps = (1, tq // LANES)

    def body(qi_ref, kj_ref, q_ref, k_ref, v_ref, lse_ref, dl_ref, do_ref, cq_ref, ck_ref,
             dq_ref, dk_ref, dv_ref, dck_ref, dcq_ref, s0, dp0, p0, ds0, b0, b1, ck0, ck1):
        s1, dp1, p1, ds1 = s0, dp0, p0, ds0
        i = qi_ref[pl.program_id(1)]
        j = kj_ref[pl.program_id(1)]

        @pl.when(pl.program_id(1) == 0)
        def _():
            dq_ref[...] = jnp.zeros_like(dq_ref)
            dcq_ref[...] = jnp.zeros_like(dcq_ref)

        @pl.when(i == j)
        def _():
            for ref in (dk_ref, dv_ref, ck0, ck1):
                ref[...] = jnp.zeros_like(ref)

        def block(diagonal):
            qv, kk, vv = q_ref[...], k_ref[...], v_ref[...]
            dob = do_ref[...].astype(BF16)
            dq_acc = jnp.zeros((tq, LANES), F32)
            dk_acc = jnp.zeros((tk, LANES), F32)
            dv_acc = jnp.zeros((tk, LANES), F32)
            query_sums = []
            for h, msk in enumerate(_head_masks()):
                st_sc, dpt_sc, pt_sc, dst_sc, bias_sc, key_part = ((s0, dp0, p0, ds0, b0, ck0),
                                                                  (s1, dp1, p1, ds1, b1, ck1))[h]
                qh = jnp.where(msk, qv, jnp.zeros_like(qv))
                kh = jnp.where(msk, kk, jnp.zeros_like(kk))
                doh = jnp.where(msk, dob, jnp.zeros_like(dob))
                st_sc[...] = _dot(kk, qh, 1, 1)
                dpt_sc[...] = _dot(vv, doh, 1, 1)
                bias_sc[...] = jnp.broadcast_to(cq_ref[0, h:h + 1, 0:1] - ck_ref[0, :, h:h + 1], (tk, LANES))
                lse_row = lse_ref[0, h:h + 1, :]
                delta_row = dl_ref[0, h:h + 1, :]
                col_acc = jnp.zeros((SUBLANES, tq), F32)
                parts = []
                for r in range(tk // rt):
                    rows = slice(r * rt, (r + 1) * rt)
                    s = st_sc[rows, :] + jnp.tile(bias_sc[rows, :], reps)
                    if diagonal:
                        key = r * rt + lax.broadcasted_iota(jnp.int32, (rt, tq), 0)
                        qry = lax.broadcasted_iota(jnp.int32, (rt, tq), 1)
                        s = jnp.where(key <= qry, s, -jnp.inf)
                    p = jnp.exp(s - lse_row)
                    ds = p * (dpt_sc[rows, :] - delta_row)
                    for g in range(rt // SUBLANES):
                        col_acc = col_acc + ds[g * SUBLANES:(g + 1) * SUBLANES, :]
                    part = ds[:, 0:LANES]
                    for g in range(1, tq // LANES):
                        part = part + ds[:, g * LANES:(g + 1) * LANES]
                    parts.append(part)
                    pt_sc[rows, :] = p.astype(BF16)
                    dst_sc[rows, :] = ds.astype(BF16)
                key_part[...] += jnp.concatenate(parts, axis=0)
                query_sums.append(jnp.sum(col_acc, axis=0, keepdims=True))
                dv_acc = dv_acc + _dot(pt_sc[...], doh, 1, 0)
                dsb = dst_sc[...]
                dk_acc = dk_acc + _dot(dsb, qh, 1, 0)
                dq_acc = dq_acc + _dot(dsb, kh, 0, 0)
            off = pl.multiple_of(i * tq, tq)
            dq_ref[pl.ds(off, tq), :] += dq_acc
            dk_ref[...] += dk_acc
            dv_ref[...] += dv_acc
            dcq_ref[0, i] += jnp.concatenate(query_sums, axis=0)

        pl.when(i > j)(functools.partial(block, False))
        pl.when(i == j)(functools.partial(block, True))

        @pl.when(i == nq - 1)
        def _():
            two = lax.broadcasted_iota(jnp.int32, (tk, 2), 1)
            dck_ref[0] = jnp.where(two == 0, -jnp.sum(ck0[...], axis=1, keepdims=True),
                                   -jnp.sum(ck1[...], axis=1, keepdims=True))

    pairs = [(i, j) for j in range(nq) for i in range(j, nq)]
    qi = jnp.asarray([i for i, _ in pairs], jnp.int32)
    kj = jnp.asarray([j for _, j in pairs], jnp.int32)
    score = pltpu.VMEM((tk, tq), F32)
    score16 = pltpu.VMEM((tk, tq), BF16)
    keystat = pltpu.VMEM((tk, LANES), F32)
    return _call(
        body, grid=(nhp, len(pairs)), prefetch=(qi, kj),
        in_specs=[pl.BlockSpec((tq, LANES), lambda hp, t, qi, kj: (qi[t], hp)),
                  pl.BlockSpec((tk, LANES), lambda hp, t, qi, kj: (kj[t], hp)),
                  pl.BlockSpec((tk, LANES), lambda hp, t, qi, kj: (kj[t], nhp + hp)),
                  pl.BlockSpec((1, 2, tq), lambda hp, t, qi, kj: (hp, 0, qi[t])),
                  pl.BlockSpec((1, 2, tq), lambda hp, t, qi, kj: (hp, 0, qi[t])),
                  pl.BlockSpec((tq, LANES), lambda hp, t, qi, kj: (qi[t], hp)),
                  pl.BlockSpec((1, 2, tq), lambda hp, t, qi, kj: (hp, 0, qi[t])),
                  pl.BlockSpec((1, tk, 2), lambda hp, t, qi, kj: (hp, kj[t], 0))],
        out_specs=[pl.BlockSpec((n_rows, LANES), lambda hp, t, qi, kj: (0, hp)),
                   pl.BlockSpec((tk, LANES), lambda hp, t, qi, kj: (kj[t], hp)),
                   pl.BlockSpec((tk, LANES), lambda hp, t, qi, kj: (kj[t], hp)),
                   pl.BlockSpec((1, tk, 2), lambda hp, t, qi, kj: (hp, kj[t], 0)),
                   pl.BlockSpec((1, nq, 2, tq), lambda hp, t, qi, kj: (hp, 0, 0, 0))],
        scratch_shapes=[score, score, score16, score16, keystat, keystat, keystat, keystat],
        out_shape=[jax.ShapeDtypeStruct((n_rows, d), F32), jax.ShapeDtypeStruct((n_rows, d), F32),
                   jax.ShapeDtypeStruct((n_rows, d), F32), jax.ShapeDtypeStruct((nhp, n_rows, 2), F32),
                   jax.ShapeDtypeStruct((nhp, nq, 2, tq), F32)],
        args=(q, kv, kv, lse_r, delta_r, do, cum_r, cum_c), name=name, comm=comm)


def _s5_tables(w, layer):
    g, p = w["lam_re"].shape[1:]
    h = w["ssm_b_re"].shape[3]
    n = g * p
    lr = w["lam_re"][layer].reshape(1, n)
    li = w["lam_im"][layer].reshape(1, n)
    ldt = jnp.broadcast_to(w["log_dt"][layer][:, None], (g, p)).reshape(1, n)
    br = w["ssm_b_re"][layer].transpose(2, 0, 1).reshape(h, n)
    bi = w["ssm_b_im"][layer].transpose(2, 0, 1).reshape(h, n)
    cr = w["ssm_c_re"][layer].transpose(1, 0, 2).reshape(h, n)
    ci = w["ssm_c_im"][layer].transpose(1, 0, 2).reshape(h, n)
    return (lr, li, ldt, br, bi, cr, ci), (g, p, h)


def _local_step(x, tgt, w, net=None, *, attn_tile=1024):
    n_rows, d = x.shape
    n_layers = w["g_mix"].shape[0]
    n_s5 = w["lam_re"].shape[0]
    nh = w["b_f"].shape[0]
    nhp = nh // 2
    assert d == nh * HEAD_DIM
    tq = _tile(n_rows, attn_tile)
    g = {}
    saved = [dict() for _ in range(n_layers)]
    big = {}
    pending = {}

    def wt(name, layer):
        return w[name][layer]

    def carry_gather(group, run):
        if net is None or not net.has_group(group):
            return run(None)
        outs, got = run(net.gather_comm(group))
        net.store_gathered(group, got, w)
        return outs

    def carry_reduce(tag, run):
        keys = list(pending)
        grads = [pending[k][0] for k in keys]
        if net is None or not pending:
            big.update(zip(keys, grads))
            pending.clear()
            return run(None)
        parts = net.reduce_prepare(grads, [pending[k][1] for k in keys], tag)
        outs, landed = run(_chip_exchange_comm(parts))
        big.update(zip(keys, net.reduce_finish(parts, landed, grads, tag)))
        pending.clear()
        return outs

    def by_row_shard(m):
        return m.reshape(N_CHIPS, m.shape[0] // N_CHIPS, m.shape[1])

    def grad_and_copy(x_, dy_, name, *, col_slots=None, scale=None):
        g32, g16 = _mm_tn(x_, dy_, col_slots or 1, scale=scale, wire=True, name=name)
        return (g32, g16) if col_slots else (by_row_shard(g32[0]), by_row_shard(g16[0]))

    h = x
    nxt = _rowwise(lambda a, gg: _rms(a, gg), [x], [_row2(w["g_mix"][0])], [(d, F32)], name="rms_first")[0]
    kvb = fl = cum = cq3 = ck3 = hnkv = None
    bf_pad = jnp.zeros((1, LANES), F32).at[0, :nh].set(w["b_f"])
    for l in range(n_layers):
        sv = saved[l]
        sv["h"] = h
        g_ffn = _row2(w["g_ffn"][l])
        if l < n_s5:
            tabs, (_, p, _) = _s5_tables(w, l)
            prep = _s5_prep(*tabs, p, name=f"s5_prep{l}")
            dskip = w["ssm_d"][l].reshape(1, d)
            y, z, st_re, st_im, sb_re, sb_im = carry_gather(
                f"stage{l}", lambda comm, u=nxt, pr=prep, ds=dskip: _s5_fwd(u, pr, ds, name=f"s5_fwd{l}", comm=comm))
            glu4, glu_l = wt("w_glu", l)
            glu_whole = glu4[:, glu_l].transpose(1, 0, 2).reshape(1, 1, glu4.shape[2], -1)
            zz, h1, hn2 = _mm_cols(z, glu_whole, 0, wc=0, name=f"glu_mm{l}",
                                   epilogue=(lambda zq, hh, gg: ((lambda t: (zq, t, _rms(t, gg)))(hh + _glu(zq))),
                                             [h], [g_ffn], [(2 * d, F32), (d, F32), (d, BF16)], ()))
            sv.update(u=nxt, prep=prep, tabs=tabs, p=p, dskip=dskip, st_re=st_re, st_im=st_im, sb_re=sb_re,
                      sb_im=sb_im, y=y, z=z, zz=zz)
        else:
            j = l - n_s5
            qs = _mm_cols(nxt, *wt("w_q", j), wc=0, out_dtype=BF16, scale=HEAD_DIM ** -0.5, name=f"q_mm{j}")
            o, lse = carry_gather(
                f"stage{l}", lambda comm, q_=qs: _flash_fwd(q_, kvb, cq3, ck3, tq=tq, name=f"flash_fwd{j}", comm=comm))
            h1, hn2 = _mm_cols(o, *wt("w_o", j), wc=0, name=f"o_mm{j}",
                               epilogue=(lambda aa, hh, gg: ((lambda t: (t, _rms(t, gg)))(hh + aa)), [h], [g_ffn],
                                         [(d, F32), (d, BF16)], ()))
            sv.update(hn=nxt, qs=qs, o=o, lse=lse)
        uu = _mm_cols(hn2, *wt("w_in", l), wc=0, name=f"ffn_in{l}")
        cw, cb = w["conv_w"][l], _row2(w["conv_b"][l])
        act = _conv_fwd(uu, cw, cb, name=f"conv_fwd{l}")
        sv.update(h1=h1, hn2=hn2, uu=uu, act=act, cw=cw, cb=cb)

        def ffn_out(fn, rows, consts, outs, accs=()):
            return _mm_cols(act, *wt("w_out", l), wc=0, name=f"ffn_out{l}", epilogue=(fn, rows, consts, outs, accs))

        if l == n_layers - 1:
            def loss_fn(ff, hh, tt, gg):
                yv, vjp = jax.vjp(_rms, hh + ff, gg)
                err = yv - tt
                part = 0.5 * jnp.sum(jnp.mean(err * err, axis=-1, keepdims=True), axis=0, keepdims=True)
                dh, dg = vjp(err * (1.0 / d))
                return dh, dh, jnp.broadcast_to(part, (1, LANES)), dg
            dcur, dcur16, loss_row, dgf = ffn_out(loss_fn, [h1, tgt], [_row2(w["g_final"])],
                                                  [(d, F32), (d, BF16)], [(1, LANES), (1, d)])
            loss = loss_row[0, 0]
            g["g_final"] = dgf[0]
        elif l + 1 < n_s5:
            h, nxt = ffn_out(lambda ff, hh, gg: ((lambda t: (t, _rms(t, gg)))(hh + ff)), [h1],
                             [_row2(w["g_mix"][l + 1])], [(d, F32), (d, F32)])
        elif l + 1 == n_s5:
            h, nxt, hnkv = ffn_out(lambda ff, hh, g1, g2: ((lambda t: (t, _rms(t, g1), _rms(t, g2)))(hh + ff)), [h1],
                                   [_row2(w["g_mix"][l + 1]), _row2(w["g_kv"])],
                                   [(d, F32), (d, BF16), (d, BF16)])
            kvb = _mm_cols(hnkv, *wt("w_kv", 0), wc=0, out_dtype=BF16, name="kv_mm")
            fl = _mm_cols(hnkv, *wt("w_f", 0), wc=0, name="f_mm")
            cum = _cum_fwd(fl, bf_pad, name="cum_fwd")
            cq3 = cum[:, :nh].reshape(n_rows, nhp, 2).transpose(1, 0, 2)
            ck3 = cum[:, :nh].T.reshape(nhp, 2, n_rows)
        else:
            h, nxt = ffn_out(lambda ff, hh, gg: ((lambda t: (t, _rms(t, gg)))(hh + ff)), [h1],
                             [_row2(w["g_mix"][l + 1])], [(d, F32), (d, BF16)])

    per_layer = {k: [None] * n_layers for k in ("g_mix", "g_ffn", "conv_w", "conv_b")}
    per_s5 = {k: [None] * n_s5 for k in ("lam_re", "lam_im", "log_dt", "ssm_b_re", "ssm_b_im", "ssm_c_re",
                                         "ssm_c_im", "ssm_d")}
    dk_parts, dv_parts, dck_parts = [], [], []
    for l in reversed(range(n_layers)):
        sv = saved[l]
        dact = _mm_cols(dcur16, *wt("w_out", l), wc=1, name=f"ffn_out_dx{l}")
        pending["w_ffn_out", l] = grad_and_copy(sv["act"], dcur16, f"ffn_out_dw{l}")
        duu, dcw, dcb = _conv_bwd(sv["uu"], dact, sv["cw"], sv["cb"], name=f"conv_bwd{l}")
        per_layer["conv_w"][l] = jnp.concatenate([dcw[0], dcw[1]], axis=-1)
        per_layer["conv_b"][l] = jnp.concatenate([dcb[0, 0], dcb[1, 0]])
        node = _node_bwd_fn([1])
        d1, d1_16, dg = _mm_acc(duu, *wt("w_in", l), wc=1, name=f"ffn_in_dx{l}",
                                epilogue=(lambda dhn2, dd, hh, gg: node(dd, hh, dhn2, gg), [dcur, sv["h1"]],
                                          [_row2(w["g_ffn"][l])], [(d, F32), (d, BF16)], [(1, d)]))
        pending["w_ffn_in", l] = grad_and_copy(sv["hn2"], duu, f"ffn_in_dw{l}", col_slots=wt("w_in", l)[0].shape[0])
        per_layer["g_ffn"][l] = dg[0]
        if l < n_s5:
            def glu_bwd(zq, dd):
                _, vjp = jax.vjp(_glu, zq)
                return vjp(dd)[0]
            dzz = _rowwise(glu_bwd, [sv["zz"], d1], [], [(2 * d, BF16)], name=f"glu_bwd{l}")[0]
            def gelu_bwd(dd, yy):
                _, vjp = jax.vjp(_gelu, yy)
                return (vjp(dd)[0],)
            dy = _mm_acc(dzz, *wt("w_glu", l), wc=1, name=f"glu_dx{l}",
                         epilogue=(gelu_bwd, [sv["y"]], [], [(d, F32)], ()))[0]
            pending["w_glu", l] = grad_and_copy(sv["z"], dzz, f"glu_dw{l}", col_slots=wt("w_glu", l)[0].shape[0])
            du, dwbr, dwbi, dwcr, dwci, dlbr, dlbi, dd = carry_reduce(
                f"stage{l}", lambda comm, dy_=dy: _s5_bwd(sv["u"], dy_, sv["st_re"], sv["st_im"], sv["sb_re"],
                                                          sv["sb_im"], sv["prep"], sv["dskip"], name=f"s5_bwd{l}",
                                                          comm=comm))
            dlr, dli, dldt, dbr, dbi, dcr, dci = _s5_prep_bwd(*sv["tabs"], sv["p"], dlbr, dlbi, dwbr, dwbi, dwcr,
                                                              dwci, name=f"s5_prep_bwd{l}")
            gg, p = w["lam_re"].shape[1:]
            hh = w["ssm_b_re"].shape[3]
            per_s5["lam_re"][l] = dlr.reshape(gg, p)
            per_s5["lam_im"][l] = dli.reshape(gg, p)
            per_s5["log_dt"][l] = dldt.reshape(gg, p).sum(axis=1)
            per_s5["ssm_b_re"][l] = dbr.reshape(hh, gg, p).transpose(1, 2, 0)
            per_s5["ssm_b_im"][l] = dbi.reshape(hh, gg, p).transpose(1, 2, 0)
            per_s5["ssm_c_re"][l] = dcr.reshape(hh, gg, p).transpose(1, 0, 2)
            per_s5["ssm_c_im"][l] = dci.reshape(hh, gg, p).transpose(1, 0, 2)
            per_s5["ssm_d"][l] = dd.reshape(d)
            branches = [(w["g_mix"][l], [du])]
        else:
            j = l - n_s5
            do, delta = _mm_cols(d1_16, *wt("w_o", j), wc=1, name=f"o_dx{j}",
                                 epilogue=(lambda dd, oo: (dd, _head_delta(dd, oo)), [sv["o"]], [],
                                           [(d, F32), (LANES, F32)], ()))
            pending["w_o", j] = grad_and_copy(sv["o"], d1_16, f"o_dw{j}")
            delta_r = delta[:, :nh].T.reshape(nhp, 2, n_rows)
            dq, dk, dv, dck, dcq = carry_reduce(
                f"stage{l}", lambda comm, do_=do: _flash_bwd(sv["qs"], kvb, sv["lse"], delta_r, do_, cq3, ck3, tq=tq,
                                                          name=f"flash_bwd{j}", comm=comm))
            dk_parts.append(dk)
            dv_parts.append(dv)
            dck_parts.append(dck.transpose(1, 0, 2).reshape(n_rows, nh)
                             + dcq.transpose(0, 2, 1, 3).reshape(nh, n_rows).T)
            scale = HEAD_DIM ** -0.5
            pending["w_q", j] = grad_and_copy(sv["hn"], dq, f"q_dw{j}", scale=scale)
            branches = []
            if j == 0:
                def kv_sum(*parts):
                    half = len(parts) // 2
                    return jnp.concatenate([sum(parts[:half][1:], parts[0]),
                                            sum(parts[half:][1:], parts[half])], axis=1)
                dkv = _rowwise(kv_sum, dk_parts + dv_parts, [], [(2 * d, BF16)], name="dkv_sum")[0]
                dck_tot = dck_parts[0]
                for extra in dck_parts[1:]:
                    dck_tot = dck_tot + extra
                dcum = jnp.zeros((n_rows, LANES), F32).at[:, :nh].set(dck_tot)
                dfl, dbf = _cum_bwd(dcum, fl, bf_pad, name="cum_bwd")
                g["b_f"] = dbf[0, :nh]
                dhkv_a = _mm_cols(dkv, *wt("w_kv", 0), wc=1, name="kv_dx")
                dhkv_b = _mm_cols(dfl, *wt("w_f", 0), wc=1, name="f_dx")
                d_kvf = jnp.concatenate([_mm_tn(hnkv, dkv, 1, name="kv_dw")[0],
                                         _mm_tn(hnkv, dfl, 1, name="f_dw")[0][:, :nh]], axis=1)
                d_kvf = d_kvf.reshape(d, N_CHIPS, -1).transpose(1, 0, 2)
                pending["w_kvf", 0] = (d_kvf, d_kvf.astype(BF16))
                branches.append((w["g_kv"], [dhkv_a, dhkv_b]))
            node = _node_bwd_fn([1] + [len(dys) for _, dys in branches])
            gains = [_row2(w["g_mix"][l])] + [_row2(gn) for gn, _ in branches]
            res = _mm_cols(dq, *wt("w_q", j), wc=1, scale=scale, name=f"q_dx{j}",
                           epilogue=(lambda dhn, dd, hh, *rest: node(dd, hh, dhn, *rest),
                                     [d1, sv["h"], *[dy for _, dys in branches for dy in dys]], gains,
                                     [(d, F32), (d, BF16)], [(1, d)] * len(gains)))
            dcur, dcur16, dgs = res[0], res[1], [r[0] for r in res[2:]]
        if l < n_s5:
            dcur, dcur16, dgs = _node_bwd(d1, sv["h"], branches, name=f"mix_norm_bwd{l}")
        per_layer["g_mix"][l] = dgs[0]
        if len(dgs) > 1:
            g["g_kv"] = dgs[1]

    if pending:
        grads = [g32 for g32, _ in pending.values()]
        big.update(zip(list(pending), grads if net is None else net.reduce_blocking(grads, "tail")))
    for k, v in (*per_layer.items(), *per_s5.items()):
        g[k] = jnp.stack(v)
    g["big"] = big
    return loss, dcur, g


def _position():
    x, y, c = lax.axis_index("x"), lax.axis_index("y"), lax.axis_index("c")
    chips = [(1 - x, y), (x, 1 - y), (1 - x, 1 - y)]
    return x, y, c, chips


def _all_gather_comm(shards):
    n = len(shards)

    def descriptors(ins, outs, sems):
        send_sems, recv_sems = sems
        x, y, c, chips = _position()
        my_slot = 2 * x + y
        sibling = (x, y, 1 - c)

        def rows(t, half):
            hr = ins[t].shape[0] // 2
            return pl.ds(half * hr, hr)

        def remote(k, t, src, dst, to):
            return pltpu.make_async_remote_copy(src_ref=src, dst_ref=dst, send_sem=send_sems.at[k, t],
                                                recv_sem=recv_sems.at[k, t], device_id=to, device_id_type=MESH)

        own = [remote(6, t, ins[t], outs[t].at[my_slot], sibling) for t in range(n)]
        ici = [remote(j, t, ins[t].at[rows(t, c)], outs[t].at[my_slot, rows(t, c)], (*chip, c))
               for j, chip in enumerate(chips) for t in range(n)]
        slots = [2 * chip[0] + chip[1] for chip in chips]
        fwd = [[remote(3 + j, t, outs[t].at[slots[j], rows(t, c)], outs[t].at[slots[j], rows(t, c)], sibling)
                for t in range(n)] for j in range(len(chips))]
        landed = [[remote(j, t, outs[t].at[slots[j], rows(t, c)], outs[t].at[slots[j], rows(t, c)], (*chips[j], c))
                   for t in range(n)] for j in range(len(chips))]
        from_sibling = [remote(3 + j, t, outs[t].at[slots[j], rows(t, 1 - c)], outs[t].at[slots[j], rows(t, 1 - c)],
                               sibling) for j in range(len(chips)) for t in range(n)]
        return own, ici, fwd, landed, from_sibling

    def start(ins, outs, sems):
        own, ici, _, _, _ = descriptors(ins, outs, sems)
        for cp in own + ici:
            cp.start()

    def finish(ins, outs, sems):
        own, ici, fwd, landed, from_sibling = descriptors(ins, outs, sems)
        for j in range(len(fwd)):
            for cp in landed[j]:
                cp.wait_recv()
            for cp in fwd[j]:
                cp.start()
        for cp in from_sibling + own:
            cp.wait_recv()
        for cp in own + ici + [cp for group in fwd for cp in group]:
            cp.wait_send()

    return _Comm(list(shards), [jax.ShapeDtypeStruct((N_CHIPS,) + a.shape, a.dtype) for a in shards],
                 [pltpu.SemaphoreType.DMA((7, n)), pltpu.SemaphoreType.DMA((7, n))], start, finish)


def _all_gather(shards, *, name):
    return _run_comm(_all_gather_comm(shards), name=name)


def _pair_exchange(grads, *, name):
    n = len(grads)

    def body(*refs):
        ins, outs = refs[:n], refs[n:2 * n]
        send_sems, recv_sems = refs[2 * n:]
        x, y, c, _ = _position()
        copies = [pltpu.make_async_remote_copy(src_ref=ins[t].at[:, 1 - c], dst_ref=outs[t],
                                               send_sem=send_sems.at[t], recv_sem=recv_sems.at[t],
                                               device_id=(x, y, 1 - c), device_id_type=MESH) for t in range(n)]
        for cp in copies:
            cp.start()
        for cp in copies:
            cp.wait()

    return pl.pallas_call(
        body, in_specs=_any_specs(n), out_specs=_any_specs(n),
        out_shape=[jax.ShapeDtypeStruct((a.shape[0],) + a.shape[2:], a.dtype) for a in grads],
        scratch_shapes=[pltpu.SemaphoreType.DMA((n,)), pltpu.SemaphoreType.DMA((n,))], name=name)(*grads)


def _chip_exchange_comm(parts):
    n = len(parts)

    def copies(ins, outs, sems):
        send_sems, recv_sems = sems
        _, _, c, chips = _position()
        return [pltpu.make_async_remote_copy(src_ref=ins[t].at[2 * chip[0] + chip[1]], dst_ref=outs[t].at[j],
                                             send_sem=send_sems.at[j, t], recv_sem=recv_sems.at[j, t],
                                             device_id=(*chip, c), device_id_type=MESH)
                for j, chip in enumerate(chips) for t in range(n)]

    def start(ins, outs, sems):
        for cp in copies(ins, outs, sems):
            cp.start()

    def finish(ins, outs, sems):
        for cp in copies(ins, outs, sems):
            cp.wait()

    return _Comm(list(parts), [jax.ShapeDtypeStruct((N_CHIPS - 1,) + a.shape[1:], a.dtype) for a in parts],
                 [pltpu.SemaphoreType.DMA((N_CHIPS - 1, n)), pltpu.SemaphoreType.DMA((N_CHIPS - 1, n))],
                 start, finish)


def _pair_share(both, *, name):
    n = len(both)

    def body(*refs):
        ins, outs = refs[:n], refs[n:2 * n]
        send_sems, recv_sems = refs[2 * n:]
        x, y, c, _ = _position()
        for t in range(n):
            pltpu.make_async_remote_copy(src_ref=ins[t].at[c], dst_ref=outs[t].at[c], send_sem=send_sems.at[t],
                                         recv_sem=recv_sems.at[t], device_id=(x, y, 1 - c),
                                         device_id_type=MESH).start()
        for t in range(n):
            pltpu.make_async_remote_copy(src_ref=ins[t].at[c], dst_ref=outs[t].at[1 - c], send_sem=send_sems.at[t],
                                         recv_sem=recv_sems.at[t], device_id=(x, y, 1 - c),
                                         device_id_type=MESH).wait()

    return pl.pallas_call(
        body, in_specs=_any_specs(n), out_specs=_any_specs(n),
        out_shape=[jax.ShapeDtypeStruct(a.shape, a.dtype) for a in both],
        input_output_aliases={t: t for t in range(n)},
        scratch_shapes=[pltpu.SemaphoreType.DMA((n,)), pltpu.SemaphoreType.DMA((n,))], name=name)(*both)


def _sum_pair(grad, landed, c, wire_dtype, *, name):
    slots, _, m, n = grad.shape
    tm = _tile(m, 256, 2 * SUBLANES)

    def body(c_ref, g_ref, l_ref, o_ref):
        o_ref[...] = (g_ref[0] + l_ref[...]).astype(wire_dtype)

    return pl.pallas_call(
        body,
        grid_spec=pltpu.PrefetchScalarGridSpec(
            num_scalar_prefetch=1, grid=(slots, m // tm),
            in_specs=[pl.BlockSpec((1, 1, tm, n), lambda s, i, c_ref: (s, c_ref[0], i, 0)),
                      pl.BlockSpec((1, tm, n), lambda s, i, c_ref: (s, i, 0))],
            out_specs=pl.BlockSpec((1, tm, n), lambda s, i, c_ref: (s, i, 0))),
        out_shape=jax.ShapeDtypeStruct((slots, m, n), wire_dtype), compiler_params=_cp(2), name=name)(
            c, grad, landed)


def _sum_chips(part, landed, slot_c, *, name):
    _, m, n = part.shape
    tm = _tile(m, 256, 2 * SUBLANES)

    def body(s_ref, p_ref, l_ref, o_ref):
        acc = p_ref[0].astype(F32)
        for j in range(N_CHIPS - 1):
            acc = acc + l_ref[j].astype(F32)
        o_ref[0] = acc

    return pl.pallas_call(
        body,
        grid_spec=pltpu.PrefetchScalarGridSpec(
            num_scalar_prefetch=1, grid=(m // tm,),
            in_specs=[pl.BlockSpec((1, tm, n), lambda i, s_ref: (s_ref[0], i, 0)),
                      pl.BlockSpec((N_CHIPS - 1, tm, n), lambda i, s_ref: (0, i, 0))],
            out_specs=pl.BlockSpec((1, tm, n), lambda i, s_ref: (s_ref[1], i, 0))),
        out_shape=jax.ShapeDtypeStruct((N_CORES, m, n), F32), compiler_params=_cp(1), name=name)(
            slot_c, part, landed)


def _reduce_prepare(grads, wire_dtypes, tag, copies=None):
    c = lax.axis_index("c").reshape(1).astype(jnp.int32)

    def halves(a):
        lead, last = a.shape[1], a.shape[-1]
        mid = 1
        for s in a.shape[2:-1]:
            mid *= s
        return a.reshape(N_CHIPS, N_CORES, (lead // N_CORES) * mid, last)

    views = [halves(a) for a in grads]
    landed = _pair_exchange(views if copies is None else [halves(a) for a in copies],
                            name=f"rs_pair_exchange_{tag}")
    return [_sum_pair(v, l, c, wire_dtypes[t], name=f"rs_pair_sum_{tag}_{t}")
            for t, (v, l) in enumerate(zip(views, landed))]


def _reduce_finish(parts, landed, grads, tag):
    slot_c = jnp.stack([2 * lax.axis_index("x") + lax.axis_index("y"), lax.axis_index("c")]).astype(jnp.int32)
    both = [_sum_chips(p, l, slot_c, name=f"rs_chip_sum_{tag}_{t}") for t, (p, l) in enumerate(zip(parts, landed))]
    full = _pair_share(both, name=f"rs_pair_share_{tag}")
    return [f.reshape(a.shape[1:]) for f, a in zip(full, grads)]


def _reduce_scatter(grads, wire_dtypes, tag):
    parts = _reduce_prepare(grads, wire_dtypes, tag)
    landed = _run_comm(_chip_exchange_comm(parts), name=f"rs_chip_exchange_{tag}")
    return _reduce_finish(parts, landed, grads, tag)


class _Net:
    def __init__(self, groups, d, nh):
        self.groups, self.d, self.nh = groups, d, nh

    def has_group(self, group):
        return bool(self.groups.get(group))

    def gather_comm(self, group):
        return _all_gather_comm([shard for _, _, shard in self.groups[group]])

    def store_gathered(self, group, got, w):
        d, nh = self.d, self.nh
        for (name, layer, _), full in zip(self.groups[group], got):
            if name == "w_kvf":
                mat = full.transpose(1, 0, 2).reshape(d, -1)
                w["w_kv"][0] = (mat[:, :2 * d][None, None], 0)
                w["w_f"][0] = (jnp.zeros((d, LANES), BF16).at[:, :nh].set(mat[:, 2 * d:])[None, None], 0)
            elif name in ("w_in", "w_glu"):
                w[name][layer] = (full[:, None], 0)
            else:
                w[name][layer] = (full.reshape(1, 1, -1, full.shape[-1]), 0)

    def reduce_prepare(self, grads, copies, tag):
        return _reduce_prepare(grads, [BF16] * len(grads), tag, copies)

    def reduce_finish(self, parts, landed, grads, tag):
        return _reduce_finish(parts, landed, grads, tag)

    def reduce_blocking(self, grads, tag):
        return _reduce_scatter(grads, [BF16] * len(grads), tag)


def _adamw(w, g, m, v, *, name):
    def fn(ww, gg, mm, vv):
        mm = ADAM_B1 * mm + (1.0 - ADAM_B1) * gg
        vv = ADAM_B2 * vv + (1.0 - ADAM_B2) * (gg * gg)
        m_hat = mm / (1.0 - ADAM_B1 ** ADAM_STEP)
        v_hat = vv / (1.0 - ADAM_B2 ** ADAM_STEP)
        delta = -ADAM_LR * (m_hat / (jnp.sqrt(v_hat) + ADAM_EPS) + ADAM_WD * ww)
        return delta, mm, vv

    shape = w.shape
    two_d = [a.reshape(-1, shape[-1]) for a in (w, g, m, v)]
    outs = _rowwise(fn, two_d, [], [(shape[-1], F32)] * 3, name=name)
    return [o.reshape(shape) for o in outs]


def _to_bf16(a, *, name):
    two_d = a.reshape(-1, a.shape[-1])
    return _rowwise(lambda t: t, [two_d], [], [(a.shape[-1], BF16)], name=name)[0].reshape(a.shape)


def _pack(arrays, rows_multiple):
    flat = jnp.concatenate([a.reshape(-1) for a in arrays])
    rows = -(-flat.shape[0] // LANES)
    rows = -(-rows // rows_multiple) * rows_multiple
    return jnp.pad(flat, (0, rows * LANES - flat.shape[0])).reshape(rows, LANES)


def _unpack(packed, like):
    flat = packed.reshape(-1)
    out, pos = [], 0
    for a in like:
        out.append(flat[pos:pos + a.size].reshape(a.shape))
        pos += a.size
    return out


_PARAMS = ("g_mix", "g_ffn", "lam_re", "lam_im", "log_dt", "ssm_b_re", "ssm_b_im", "ssm_c_re", "ssm_c_im", "ssm_d",
           "w_glu", "g_kv", "w_kvf", "b_f", "w_q", "w_o", "w_ffn_in", "ffn_conv_w", "ffn_conv_b", "w_ffn_out",
           "g_final")
_BIG = ("w_glu", "w_kvf", "w_q", "w_o", "w_ffn_in", "w_ffn_out")
_SMALL_SHARDED = ("ssm_d", "ffn_conv_w")


def kernel(x, g_mix, g_ffn, lam_re, lam_im, log_dt, ssm_b_re, ssm_b_im, ssm_c_re, ssm_c_im, ssm_d, w_glu, g_kv, w_kvf, b_f, w_q, w_o, w_ffn_in, ffn_conv_w, ffn_conv_b, w_ffn_out, g_final, loss_target, m_g_mix, m_g_ffn, m_lam_re, m_lam_im, m_log_dt, m_ssm_b_re, m_ssm_b_im, m_ssm_c_re, m_ssm_c_im, m_ssm_d, m_w_glu, m_g_kv, m_w_kvf, m_b_f, m_w_q, m_w_o, m_w_ffn_in, m_ffn_conv_w, m_ffn_conv_b, m_w_ffn_out, m_g_final, v_g_mix, v_g_ffn, v_lam_re, v_lam_im, v_log_dt, v_ssm_b_re, v_ssm_b_im, v_ssm_c_re, v_ssm_c_im, v_ssm_d, v_w_glu, v_g_kv, v_w_kvf, v_b_f, v_w_q, v_w_o, v_w_ffn_in, v_ffn_conv_w, v_ffn_conv_b, v_w_ffn_out, v_g_final):
    p = dict(g_mix=g_mix, g_ffn=g_ffn, lam_re=lam_re, lam_im=lam_im, log_dt=log_dt, ssm_b_re=ssm_b_re,
             ssm_b_im=ssm_b_im, ssm_c_re=ssm_c_re, ssm_c_im=ssm_c_im, ssm_d=ssm_d, w_glu=w_glu, g_kv=g_kv,
             w_kvf=w_kvf, b_f=b_f, w_q=w_q, w_o=w_o, w_ffn_in=w_ffn_in, ffn_conv_w=ffn_conv_w,
             ffn_conv_b=ffn_conv_b, w_ffn_out=w_ffn_out, g_final=g_final)
    mom1 = dict(zip(_PARAMS, (m_g_mix, m_g_ffn, m_lam_re, m_lam_im, m_log_dt, m_ssm_b_re, m_ssm_b_im, m_ssm_c_re,
                              m_ssm_c_im, m_ssm_d, m_w_glu, m_g_kv, m_w_kvf, m_b_f, m_w_q, m_w_o, m_w_ffn_in,
                              m_ffn_conv_w, m_ffn_conv_b, m_w_ffn_out, m_g_final)))
    mom2 = dict(zip(_PARAMS, (v_g_mix, v_g_ffn, v_lam_re, v_lam_im, v_log_dt, v_ssm_b_re, v_ssm_b_im, v_ssm_c_re,
                              v_ssm_c_im, v_ssm_d, v_w_glu, v_g_kv, v_w_kvf, v_b_f, v_w_q, v_w_o, v_w_ffn_in,
                              v_ffn_conv_w, v_ffn_conv_b, v_w_ffn_out, v_g_final)))
    d = x.shape[-1]
    nh = b_f.shape[0]
    slot = 2 * lax.axis_index("x") + lax.axis_index("y")

    wb = {k: _to_bf16(p[k], name=f"to_bf16_{k}") for k in _BIG}
    gd, gcw = _all_gather([ssm_d, ffn_conv_w], name="first_all_gather")
    n_lay, n_s5 = w_ffn_in.shape[0], lam_re.shape[0]
    n_fox = n_lay - n_s5
    groups = {f"stage{l}": [("w_in", l, wb["w_ffn_in"][l]), ("w_out", l, wb["w_ffn_out"][l])] for l in range(n_lay)}
    for l in range(n_s5):
        groups[f"stage{l}"].append(("w_glu", l, wb["w_glu"][l]))
    groups[f"stage{n_s5 - 1}"].append(("w_kvf", 0, wb["w_kvf"]))
    for j in range(n_fox):
        groups[f"stage{n_s5 + j - 1}"].append(("w_q", j, wb["w_q"][j]))
        groups[f"stage{n_s5 + j}"].append(("w_o", j, wb["w_o"][j]))
    w = dict(p)
    w.update(w_glu=[None] * n_s5, w_in=[None] * n_lay, w_out=[None] * n_lay, w_q=[None] * n_fox,
             w_o=[None] * n_fox, w_kv=[None], w_f=[None],
             conv_w=gcw.transpose(1, 2, 0, 3).reshape(n_lay, CONV_TAPS, -1), conv_b=ffn_conv_b,
             ssm_d=gd.transpose(1, 0, 2).reshape(gd.shape[1], d))

    loss_part, grad_x, g = _local_step(x[0], loss_target[0], w, _Net(groups, d, nh))
    loss = lax.psum(loss_part, ("x", "y", "c"))

    small_names = [k for k in _PARAMS if k not in _BIG]
    small_full = dict(g_mix=g["g_mix"], g_ffn=g["g_ffn"], lam_re=g["lam_re"], lam_im=g["lam_im"], log_dt=g["log_dt"],
                      ssm_b_re=g["ssm_b_re"], ssm_b_im=g["ssm_b_im"], ssm_c_re=g["ssm_c_re"], ssm_c_im=g["ssm_c_im"],
                      ssm_d=g["ssm_d"], g_kv=g["g_kv"], b_f=g["b_f"], ffn_conv_w=g["conv_w"],
                      ffn_conv_b=g["conv_b"], g_final=g["g_final"])
    small_list = [small_full[k] for k in small_names]
    pack = _pack(small_list, N_CHIPS * N_CORES * 2 * SUBLANES)
    pack4 = pack.reshape(N_CHIPS, pack.shape[0] // N_CHIPS, LANES)
    pack_shard = _reduce_scatter([pack4], [F32], "small")[0]
    red_big = {k: g["big"][k, 0] if p[k].ndim == 2 else jnp.stack([g["big"][k, l] for l in range(p[k].shape[0])])
               for k in _BIG}
    pack_all = _all_gather([pack_shard], name="small_grads_all_gather")[0]
    red_small = dict(zip(small_names, _unpack(pack_all, small_list)))
    for k in _SMALL_SHARDED:
        width = p[k].shape[-1]
        red_small[k] = lax.dynamic_slice_in_dim(red_small[k], slot * width, width, axis=red_small[k].ndim - 1)

    grads, deltas, new_m, new_v = {}, {}, {}, {}
    for k in _BIG:
        grads[k] = red_big[k]
        deltas[k], new_m[k], new_v[k] = _adamw(p[k], grads[k], mom1[k], mom2[k], name=f"adamw_{k}")
    packs = [_pack([src[k] for k in small_names], SUBLANES) for src in (p, red_small, mom1, mom2)]
    like = [p[k] for k in small_names]
    outs = [_unpack(o, like) for o in _adamw(*packs, name="adamw_small")]
    for i, k in enumerate(small_names):
        grads[k] = red_small[k]
        deltas[k], new_m[k], new_v[k] = outs[0][i], outs[1][i], outs[2][i]
    return (loss, grad_x[None], *[grads[k] for k in _PARAMS], *[deltas[k] for k in _PARAMS],
            *[new_m[k] for k in _PARAMS], *[new_v[k] for k in _PARAMS])
```

```python
import functools

import jax
import jax.numpy as jnp
from jax import lax
from jax.experimental import pallas as pl
from jax.experimental.pallas import tpu as pltpu

F32 = jnp.float32
BF16 = jnp.bfloat16

RMS_EPS = 1e-6
ADAM_LR = 0.001
ADAM_B1 = 0.9
ADAM_B2 = 0.999
ADAM_EPS = 1e-08
ADAM_WD = 0.01
ADAM_STEP = 10
CONV_TAPS = 3

LANES = 128
SUBLANES = 8
HEAD_DIM = 64
FLASH_ROW_TILE = 32
S5_TIME_CHUNK = 1024
S5_BLOCK_GROUPS = 16
VMEM_LIMIT_BYTES = 48 << 20
MM_BLOCK_BUDGET_BYTES = 30 << 20
ROWWISE_BLOCK_BUDGET_BYTES = 20 << 20
N_CHIPS = 4
N_CORES = 2
MESH = pl.DeviceIdType.MESH


def _cp(n_grid):
    return pltpu.CompilerParams(dimension_semantics=("arbitrary",) * n_grid, vmem_limit_bytes=VMEM_LIMIT_BYTES)


def _tile(n, pref, mult=SUBLANES):
    if n <= pref:
        return n
    t = (pref // mult) * mult
    while t >= mult:
        if n % t == 0:
            return t
        t -= mult
    return n


class _Comm:
    def __init__(self, ins, out_shapes, sems, start, finish):
        self.ins, self.out_shapes, self.sems, self.start, self.finish = ins, out_shapes, sems, start, finish


def _any_specs(n):
    return [pl.BlockSpec(memory_space=pl.ANY)] * n


def _run_comm(comm, *, name):
    n_in, n_out = len(comm.ins), len(comm.out_shapes)

    def body(*refs):
        ins, outs, sems = refs[:n_in], refs[n_in:n_in + n_out], refs[n_in + n_out:]
        comm.start(ins, outs, sems)
        comm.finish(ins, outs, sems)

    return pl.pallas_call(body, in_specs=_any_specs(n_in), out_specs=_any_specs(n_out),
                          out_shape=list(comm.out_shapes), scratch_shapes=list(comm.sems), name=name)(*comm.ins)


def _call(body, *, grid, in_specs, out_specs, out_shape, args, name, scratch_shapes=(), prefetch=(), comm=None):
    n_pre, n_in, n_out, n_scr = len(prefetch), len(in_specs), len(out_specs), len(scratch_shapes)
    in_specs, out_specs, out_shape = list(in_specs), list(out_specs), list(out_shape)
    scratch_shapes, args = list(scratch_shapes), list(args)
    kernel_body = body
    if comm is not None:
        n_cin, n_cout = len(comm.ins), len(comm.out_shapes)

        def kernel_body(*refs):
            pos = n_pre + n_in
            c_in = refs[pos:pos + n_cin]
            main_out = refs[pos + n_cin:pos + n_cin + n_out]
            pos += n_cin + n_out
            c_out = refs[pos:pos + n_cout]
            main_scr = refs[pos + n_cout:pos + n_cout + n_scr]
            sems = refs[pos + n_cout + n_scr:]
            ids = [pl.program_id(a) for a in range(len(grid))]
            first = functools.reduce(jnp.logical_and, [i == 0 for i in ids])
            last = functools.reduce(jnp.logical_and, [i == g - 1 for i, g in zip(ids, grid)])
            pl.when(first)(lambda: comm.start(c_in, c_out, sems))
            body(*refs[:n_pre + n_in], *main_out, *main_scr)
            pl.when(last)(lambda: comm.finish(c_in, c_out, sems))

        in_specs += _any_specs(n_cin)
        out_specs += _any_specs(n_cout)
        out_shape += list(comm.out_shapes)
        scratch_shapes += list(comm.sems)
        args += list(comm.ins)
    if prefetch:
        spec = pltpu.PrefetchScalarGridSpec(num_scalar_prefetch=n_pre, grid=grid, in_specs=in_specs,
                                            out_specs=out_specs, scratch_shapes=scratch_shapes)
        res = pl.pallas_call(kernel_body, grid_spec=spec, out_shape=out_shape, compiler_params=_cp(len(grid)),
                             name=name)(*prefetch, *args)
    else:
        res = pl.pallas_call(kernel_body, grid=grid, in_specs=in_specs, out_specs=out_specs, out_shape=out_shape,
                             scratch_shapes=scratch_shapes, compiler_params=_cp(len(grid)), name=name)(*args)
    return (res[:n_out], res[n_out:]) if comm is not None else res


def _row_tile(m, bytes_per_row, fixed_bytes):
    for tm in (1024, 512):
        if m % tm == 0 and 2 * (tm * bytes_per_row + fixed_bytes) <= MM_BLOCK_BUDGET_BYTES:
            return tm
    return _tile(m, 512)


def _dot(a, b, ca, cb):
    return lax.dot_general(a, b, (((ca,), (cb,)), ((), ())), preferred_element_type=F32)


def _epilogue_io(epilogue, m, tm, rows_axis, grid_rank):
    _, rows, consts, outs, accs = epilogue

    def at_rows(width):
        return pl.BlockSpec((tm, width), lambda *g: (g[rows_axis], 0))

    def whole(shape):
        return pl.BlockSpec(shape, lambda *g: (0,) * len(shape))

    in_specs = [at_rows(r.shape[1]) for r in rows] + [whole(c.shape) for c in consts]
    out_specs = [at_rows(wd) for wd, _ in outs] + [whole(s) for s in accs]
    out_shape = ([jax.ShapeDtypeStruct((m, wd), dt) for wd, dt in outs]
                 + [jax.ShapeDtypeStruct(s, F32) for s in accs])
    bytes_per_row = (sum(r.shape[1] * r.dtype.itemsize for r in rows)
                     + sum(wd * jnp.dtype(dt).itemsize for wd, dt in outs))
    return in_specs, out_specs, out_shape, bytes_per_row


def _epilogue_apply(epilogue, block, refs, first_row_tile):
    fn, rows, consts, outs, _ = epilogue
    n_in, n_out = len(rows) + len(consts), len(outs)
    res = fn(block, *[r[...] for r in refs[:n_in]])
    for o, val in zip(refs[n_in:n_in + n_out], res[:n_out]):
        o[...] = val.astype(o.dtype)
    a_refs = refs[n_in + n_out:]
    if a_refs:
        @pl.when(first_row_tile)
        def _():
            for a in a_refs:
                a[...] = jnp.zeros_like(a)
        for a, val in zip(a_refs, res[n_out:]):
            a[...] += val


def _mm_cols(x, w4, layer, *, wc, out_dtype=F32, scale=None, epilogue=None, name):
    m, k = x.shape
    slots, _, k0, k1 = w4.shape
    nb = k1 if wc == 0 else k0
    assert (k0 if wc == 0 else k1) == k
    if epilogue is None:
        tm = _row_tile(m, k * x.dtype.itemsize + nb * jnp.dtype(out_dtype).itemsize, k0 * k1 * w4.dtype.itemsize)
        extra_in, out_specs = [], pl.BlockSpec((tm, nb), lambda s, i: (i, s))
        out_shape = jax.ShapeDtypeStruct((m, slots * nb), out_dtype)
    else:
        assert slots == 1
        bytes_per_row = _epilogue_io(epilogue, m, SUBLANES, 1, 2)[3]
        tm = _row_tile(m, k * x.dtype.itemsize + bytes_per_row, k0 * k1 * w4.dtype.itemsize)
        extra_in, out_specs, out_shape, _ = _epilogue_io(epilogue, m, tm, 1, 2)

    def body(x_ref, w_ref, *refs):
        acc = _dot(x_ref[...].astype(BF16), w_ref[0, 0], 1, wc)
        if scale is not None:
            acc = acc * scale
        if epilogue is None:
            refs[0][...] = acc.astype(out_dtype)
        else:
            _epilogue_apply(epilogue, acc, refs, pl.program_id(1) == 0)

    extra_args = [] if epilogue is None else [*epilogue[1], *epilogue[2]]
    return pl.pallas_call(
        body, grid=(slots, m // tm),
        in_specs=[pl.BlockSpec((tm, k), lambda s, i: (i, 0)),
                  pl.BlockSpec((1, 1, k0, k1), lambda s, i: (s, layer, 0, 0)), *extra_in],
        out_specs=out_specs, out_shape=out_shape,
        compiler_params=_cp(2), name=name)(x, w4, *extra_args)


def _planes(a):
    return a if a.ndim == 3 else a[None]


def _mm_acc(x, w4, layer, *, wc, epilogue=None, name):
    x = _planes(x)
    n_planes, m, width = x.shape
    slots, _, k0, k1 = w4.shape
    kb = k0 if wc == 0 else k1
    nout = k1 if wc == 0 else k0
    assert n_planes * width == slots * kb
    spp = slots // n_planes
    x_spec_w = pl.BlockSpec((1, 1, k0, k1), lambda i, s: (s, layer, 0, 0))
    if epilogue is None:
        tm = _row_tile(m, kb * x.dtype.itemsize + nout * 4, k0 * k1 * w4.dtype.itemsize)

        def body(x_ref, w_ref, o_ref):
            @pl.when(pl.program_id(1) == 0)
            def _():
                o_ref[...] = jnp.zeros_like(o_ref)
            o_ref[...] += _dot(x_ref[0].astype(BF16), w_ref[0, 0], 1, wc)

        return pl.pallas_call(
            body, grid=(m // tm, slots),
            in_specs=[pl.BlockSpec((1, tm, kb), lambda i, s: (s // spp, i, s % spp)), x_spec_w],
            out_specs=pl.BlockSpec((tm, nout), lambda i, s: (i, 0)),
            out_shape=jax.ShapeDtypeStruct((m, nout), F32),
            compiler_params=_cp(2), name=name)(x, w4)

    bytes_per_row = _epilogue_io(epilogue, m, SUBLANES, 0, 2)[3]
    tm = _row_tile(m, kb * x.dtype.itemsize + nout * 2 + bytes_per_row, k0 * k1 * w4.dtype.itemsize)
    extra_in, out_specs, out_shape, _ = _epilogue_io(epilogue, m, tm, 0, 2)

    def body(x_ref, w_ref, *refs):
        acc = refs[-1]

        @pl.when(pl.program_id(1) == 0)
        def _():
            acc[...] = jnp.zeros_like(acc)
        acc[...] += _dot(x_ref[0].astype(BF16), w_ref[0, 0], 1, wc)

        @pl.when(pl.program_id(1) == slots - 1)
        def _():
            _epilogue_apply(epilogue, acc[...], refs[:-1], pl.program_id(0) == 0)

    return pl.pallas_call(
        body, grid=(m // tm, slots),
        in_specs=[pl.BlockSpec((1, tm, kb), lambda i, s: (s // spp, i, s % spp)), x_spec_w, *extra_in],
        out_specs=out_specs, out_shape=out_shape, scratch_shapes=[pltpu.VMEM((tm, nout), F32)],
        compiler_params=_cp(2), name=name)(x, w4, *epilogue[1], *epilogue[2])


def _mm_tn(x, dy, slots, *, scale=None, wire=False, name):
    m, k = x.shape
    dy = _planes(dy)
    n_planes, _, width = dy.shape
    n = n_planes * width // slots
    spp = slots // n_planes
    ta = _tile(k, 512, LANES)
    tm = m
    while tm > 512 and tm % 2 == 0 and (2 * tm * (ta * x.dtype.itemsize + n * dy.dtype.itemsize)
                                         + 2 * ta * n * 4) > MM_BLOCK_BUDGET_BYTES:
        tm //= 2
    n_m = m // tm

    def body(x_ref, dy_ref, o_ref, *wire_ref):
        @pl.when(pl.program_id(2) == 0)
        def _():
            o_ref[...] = jnp.zeros_like(o_ref)
        o_ref[0] += _dot(x_ref[...].astype(BF16), dy_ref[0].astype(BF16), 0, 0)
        if scale is not None or wire:
            @pl.when(pl.program_id(2) == n_m - 1)
            def _():
                if scale is not None:
                    o_ref[...] = o_ref[...] * scale
                if wire:
                    wire_ref[0][...] = o_ref[...].astype(BF16)

    out_spec = pl.BlockSpec((1, ta, n), lambda s, a, i: (s, a, 0))
    return pl.pallas_call(
        body, grid=(slots, k // ta, n_m),
        in_specs=[pl.BlockSpec((tm, ta), lambda s, a, i: (i, a)),
                  pl.BlockSpec((1, tm, n), lambda s, a, i: (s // spp, i, s % spp))],
        out_specs=[out_spec, out_spec] if wire else out_spec,
        out_shape=([jax.ShapeDtypeStruct((slots, k, n), dt) for dt in (F32, BF16)] if wire
                   else jax.ShapeDtypeStruct((slots, k, n), F32)),
        compiler_params=_cp(3), name=name)(x, dy)


def _rowwise(fn, rows, consts, outs, accs=(), *, name):
    n_rows = rows[0].shape[0]
    bytes_per_row = (sum(r.shape[1] * r.dtype.itemsize for r in rows)
                     + sum(wd * jnp.dtype(dt).itemsize for wd, dt in outs))
    tl = _tile(n_rows, 512 if 2 * 512 * bytes_per_row <= ROWWISE_BLOCK_BUDGET_BYTES else 256)
    n_in = len(rows) + len(consts)
    n_out = len(outs)

    def body(*refs):
        res = fn(*[r[...] for r in refs[:n_in]])
        res = res if isinstance(res, (tuple, list)) else (res,)
        o_refs = refs[n_in:n_in + n_out]
        a_refs = refs[n_in + n_out:]
        for o, val in zip(o_refs, res[:n_out]):
            o[...] = val.astype(o.dtype)
        if a_refs:
            @pl.when(pl.program_id(0) == 0)
            def _():
                for a in a_refs:
                    a[...] = jnp.zeros_like(a)
            for a, val in zip(a_refs, res[n_out:]):
                a[...] += val

    in_specs = ([pl.BlockSpec((tl, r.shape[1]), lambda i: (i, 0)) for r in rows]
                + [pl.BlockSpec(c.shape, lambda i: (0, 0)) for c in consts])
    out_specs = ([pl.BlockSpec((tl, w), lambda i: (i, 0)) for w, _ in outs]
                 + [pl.BlockSpec(s, lambda i: (0, 0)) for s in accs])
    out_shape = ([jax.ShapeDtypeStruct((n_rows, w), dt) for w, dt in outs]
                 + [jax.ShapeDtypeStruct(s, F32) for s in accs])
    return pl.pallas_call(body, grid=(n_rows // tl,), in_specs=in_specs, out_specs=out_specs,
                          out_shape=out_shape, compiler_params=_cp(1), name=name)(*rows, *consts)


def _rms(x, g):
    return x * lax.rsqrt(jnp.mean(x * x, axis=-1, keepdims=True) + RMS_EPS) * g


def _sigmoid(x):
    return 1.0 / (1.0 + jnp.exp(-x))


def _glu(zz):
    d = zz.shape[1] // 2
    return zz[:, :d] * _sigmoid(zz[:, d:])


def _gelu(y):
    return jax.nn.gelu(y)


def _row2(v):
    return v.reshape(1, -1)


def _node_bwd_fn(counts):
    n_dy = sum(counts)

    def fn(d, hh, *rest):
        dys, gs = rest[:n_dy], rest[n_dy:]
        tot, dgs, pos = d, [], 0
        for g, cnt in zip(gs, counts):
            dy = dys[pos].astype(F32)
            for extra in dys[pos + 1:pos + cnt]:
                dy = dy + extra.astype(F32)
            pos += cnt
            _, vjp = jax.vjp(_rms, hh, g)
            dx, dg = vjp(dy)
            tot = tot + dx
            dgs.append(dg)
        return (tot, tot, *dgs)

    return fn


def _node_bwd(d_in, h, branches, *, name):
    width = h.shape[1]
    flat = [dy for _, dys in branches for dy in dys]
    res = _rowwise(_node_bwd_fn([len(dys) for _, dys in branches]), [d_in, h, *flat],
                   [_row2(g) for g, _ in branches], [(width, F32), (width, BF16)], [(1, width)] * len(branches),
                   name=name)
    return res[0], res[1], [r[0] for r in res[2:]]


def _s5_prep_fn(lr, li, ldt, br, bi, cr, ci, *, gq, h, p):
    dt = jnp.exp(ldt)
    mag = jnp.exp(lr * dt)
    lb_re = mag * jnp.cos(li * dt)
    lb_im = mag * jnp.sin(li * dt)
    den = lr * lr + li * li
    nr = lb_re - 1.0
    fr = (nr * lr + lb_im * li) / den
    fi = (lb_im * lr - nr * li) / den
    bb_re = fr * br - fi * bi
    bb_im = fr * bi + fi * br
    shape = (gq * h, gq * p)
    r = lax.broadcasted_iota(jnp.int32, shape, 0)
    c = lax.broadcasted_iota(jnp.int32, shape, 1)
    mask = jnp.where(jnp.right_shift(r, h.bit_length() - 1) == jnp.right_shift(c, p.bit_length() - 1), 1.0, 0.0)

    def expand(t):
        return jnp.concatenate([t] * gq, axis=0) * mask

    return lb_re, lb_im, expand(bb_re), expand(bb_im), expand(cr), expand(ci)


def _s5_prep(lr, li, ldt, br, bi, cr, ci, p, *, name):
    n = lr.shape[1]
    h = br.shape[0]
    gq = S5_BLOCK_GROUPS
    nq, cq = gq * p, gq * h
    nblk = n // nq
    fn = functools.partial(_s5_prep_fn, gq=gq, h=h, p=p)

    def body(lr_r, li_r, ldt_r, br_r, bi_r, cr_r, ci_r, lbr_o, lbi_o, wbr_o, wbi_o, wcr_o, wci_o):
        lb_re, lb_im, wbr, wbi, wcr, wci = fn(lr_r[...], li_r[...], ldt_r[...], br_r[...], bi_r[...],
                                              cr_r[...], ci_r[...])
        lbr_o[...] = lb_re
        lbi_o[...] = lb_im
        wbr_o[0] = wbr.astype(BF16)
        wbi_o[0] = wbi.astype(BF16)
        wcr_o[0] = wcr.astype(BF16)
        wci_o[0] = wci.astype(BF16)

    vec = pl.BlockSpec((1, nq), lambda q: (0, q))
    tab = pl.BlockSpec((h, nq), lambda q: (0, q))
    wsp = pl.BlockSpec((1, cq, nq), lambda q: (q, 0, 0))
    wsh = jax.ShapeDtypeStruct((nblk, cq, nq), BF16)
    vsh = jax.ShapeDtypeStruct((1, n), F32)
    return pl.pallas_call(body, grid=(nblk,), in_specs=[vec, vec, vec, tab, tab, tab, tab],
                          out_specs=[vec, vec, wsp, wsp, wsp, wsp], out_shape=[vsh, vsh, wsh, wsh, wsh, wsh],
                          compiler_params=_cp(1), name=name)(lr, li, ldt, br, bi, cr, ci)


def _s5_prep_bwd(lr, li, ldt, br, bi, cr, ci, p, dlbr, dlbi, dwbr, dwbi, dwcr, dwci, *, name):
    n = lr.shape[1]
    h = br.shape[0]
    gq = S5_BLOCK_GROUPS
    nq, cq = gq * p, gq * h
    nblk = n // nq
    fn = functools.partial(_s5_prep_fn, gq=gq, h=h, p=p)

    def body(lr_r, li_r, ldt_r, br_r, bi_r, cr_r, ci_r, dlbr_r, dlbi_r, dwbr_r, dwbi_r, dwcr_r, dwci_r,
             *outs):
        _, vjp = jax.vjp(fn, lr_r[...], li_r[...], ldt_r[...], br_r[...], bi_r[...], cr_r[...], ci_r[...])
        grads = vjp((dlbr_r[0], dlbi_r[0], dwbr_r[0], dwbi_r[0], dwcr_r[0], dwci_r[0]))
        for o, g in zip(outs, grads):
            o[...] = g

    vec = pl.BlockSpec((1, nq), lambda q: (0, q))
    tab = pl.BlockSpec((h, nq), lambda q: (0, q))
    vec3 = pl.BlockSpec((1, 1, nq), lambda q: (q, 0, 0))
    wsp = pl.BlockSpec((1, cq, nq), lambda q: (q, 0, 0))
    vsh = jax.ShapeDtypeStruct((1, n), F32)
    tsh = jax.ShapeDtypeStruct((h, n), F32)
    return pl.pallas_call(body, grid=(nblk,),
                          in_specs=[vec, vec, vec, tab, tab, tab, tab, vec3, vec3, wsp, wsp, wsp, wsp],
                          out_specs=[vec, vec, vec, tab, tab, tab, tab],
                          out_shape=[vsh, vsh, vsh, tsh, tsh, tsh, tsh],
                          compiler_params=_cp(1), name=name)(lr, li, ldt, br, bi, cr, ci,
                                                             dlbr, dlbi, dwbr, dwbi, dwcr, dwci)


def _scan_rows(s_re, s_im, a_re, a_im, c_re, c_im, *, reverse):
    t_rows, n = s_re.shape
    nb = t_rows // SUBLANES
    row = lax.broadcasted_iota(jnp.int32, (SUBLANES, n), 0)

    def cmul(x, y):
        return x[0] * y[0] - x[1] * y[1], x[0] * y[1] + x[1] * y[0]

    a1 = (jnp.broadcast_to(a_re, (SUBLANES, n)), jnp.broadcast_to(a_im, (SUBLANES, n)))
    a2 = cmul(a1, a1)
    a4 = cmul(a2, a2)
    steps = []
    for dist, (pr, pi) in ((1, a1), (2, a2), (4, a4)):
        keep = (row < SUBLANES - dist) if reverse else (row >= dist)
        steps.append((SUBLANES - dist if reverse else dist, (jnp.where(keep, pr, 0.0), jnp.where(keep, pi, 0.0))))
    pk = (a_re, a_im)
    tab_re = jnp.zeros((SUBLANES, n), F32)
    tab_im = jnp.zeros((SUBLANES, n), F32)
    for i in range(SUBLANES):
        at = (SUBLANES - 1 - i) if reverse else i
        tab_re = jnp.where(row == at, pk[0], tab_re)
        tab_im = jnp.where(row == at, pk[1], tab_im)
        pk = cmul(pk, (a_re, a_im))

    def step(b, carry):
        cr, ci = carry
        blk = (nb - 1 - b) if reverse else b
        off = pl.multiple_of(blk * SUBLANES, SUBLANES)
        x_re = s_re[pl.ds(off, SUBLANES), :]
        x_im = s_im[pl.ds(off, SUBLANES), :]
        for sh, (pr, pi) in steps:
            sh_re = pltpu.roll(x_re, sh, 0)
            sh_im = pltpu.roll(x_im, sh, 0)
            x_re, x_im = x_re + pr * sh_re - pi * sh_im, x_im + pr * sh_im + pi * sh_re
        x_re, x_im = x_re + tab_re * cr - tab_im * ci, x_im + tab_re * ci + tab_im * cr
        s_re[pl.ds(off, SUBLANES), :] = x_re
        s_im[pl.ds(off, SUBLANES), :] = x_im
        edge = 0 if reverse else SUBLANES - 1
        return x_re[edge:edge + 1, :], x_im[edge:edge + 1, :]

    return lax.fori_loop(0, nb, step, (c_re, c_im))


def _s5_fwd(u, prep, dskip, *, name, comm=None):
    lb_re, lb_im, wbr, wbi, wcr, wci = prep
    n_rows, _ = u.shape
    nblk, cq, nq = wbr.shape
    tt = _tile(n_rows, S5_TIME_CHUNK)
    nch = n_rows // tt

    def body(u_ref, wbr_r, wbi_r, wcr_r, wci_r, lbr_r, lbi_r, d_ref, y_ref, z_ref, s_re, s_im, sbr_o, sbi_o,
             c_re, c_im):
        @pl.when(pl.program_id(1) == 0)
        def _():
            c_re[...] = jnp.zeros_like(c_re)
            c_im[...] = jnp.zeros_like(c_im)
        uf = u_ref[...]
        ub = uf.astype(BF16)
        s_re[...] = _dot(ub, wbr_r[0], 1, 0)
        s_im[...] = _dot(ub, wbi_r[0], 1, 0)
        sbr_o[0] = c_re[...]
        sbi_o[0] = c_im[...]
        cr, ci = _scan_rows(s_re, s_im, lbr_r[...], lbi_r[...], c_re[...], c_im[...], reverse=False)
        c_re[...] = cr
        c_im[...] = ci
        y = _dot(s_re[...].astype(BF16), wcr_r[0], 1, 1) - _dot(s_im[...].astype(BF16), wci_r[0], 1, 1)
        y = y + d_ref[...] * uf
        y_ref[...] = y
        z_ref[...] = _gelu(y).astype(BF16)

    wsp = pl.BlockSpec((1, cq, nq), lambda q, i: (q, 0, 0))
    vec = pl.BlockSpec((1, nq), lambda q, i: (0, q))
    act = pl.BlockSpec((tt, cq), lambda q, i: (i, q))
    sb = pl.BlockSpec((1, 1, nq), lambda q, i: (i, 0, q))
    sbsh = jax.ShapeDtypeStruct((nch, 1, nblk * nq), F32)
    states = pl.BlockSpec((tt, nq), lambda q, i: (i, q))
    stsh = jax.ShapeDtypeStruct((n_rows, nblk * nq), F32)
    return _call(
        body, grid=(nblk, nch),
        in_specs=[act, wsp, wsp, wsp, wsp, vec, vec, pl.BlockSpec((1, cq), lambda q, i: (0, q))],
        out_specs=[act, act, states, states, sb, sb],
        out_shape=[jax.ShapeDtypeStruct(u.shape, F32), jax.ShapeDtypeStruct(u.shape, BF16), stsh, stsh, sbsh, sbsh],
        scratch_shapes=[pltpu.VMEM((1, nq), F32), pltpu.VMEM((1, nq), F32)],
        args=(u, wbr, wbi, wcr, wci, lb_re, lb_im, dskip), name=name, comm=comm)


def _s5_bwd(u, dy, st_re, st_im, sb_re, sb_im, prep, dskip, *, name, comm=None):
    lb_re, lb_im, wbr, wbi, wcr, wci = prep
    n_rows, _ = u.shape
    nblk, cq, nq = wbr.shape
    tt = _tile(n_rows, S5_TIME_CHUNK)
    nch = n_rows // tt

    def body(u_ref, dy_ref, s_re, s_im, sbr_r, sbi_r, wbr_r, wbi_r, wcr_r, wci_r, lbr_r, lbi_r, d_ref,
             du_ref, dwbr, dwbi, dwcr, dwci, dlbr, dlbi, dd_ref, g_re, g_im, lc_re, lc_im):
        @pl.when(pl.program_id(1) == 0)
        def _():
            for ref in (lc_re, lc_im, dwbr, dwbi, dwcr, dwci, dlbr, dlbi, dd_ref):
                ref[...] = jnp.zeros_like(ref)
        uf = u_ref[...]
        ub = uf.astype(BF16)
        dyf = dy_ref[...]
        dyb = dyf.astype(BF16)
        sr16 = s_re[...].astype(BF16)
        si16 = s_im[...].astype(BF16)
        dwcr[0] += _dot(dyb, sr16, 0, 0)
        dwci[0] -= _dot(dyb, si16, 0, 0)
        g_re[...] = _dot(dyb, wcr_r[0], 1, 0)
        g_im[...] = -_dot(dyb, wci_r[0], 1, 0)
        lcr, lci = _scan_rows(g_re, g_im, lbr_r[...], -lbi_r[...], lc_re[...], lc_im[...], reverse=True)
        lc_re[...] = lcr
        lc_im[...] = lci
        lam_r = g_re[...]
        lam_i = g_im[...]
        first = lax.broadcasted_iota(jnp.int32, (tt, nq), 0) == 0
        prev_r = jnp.where(first, sbr_r[0], pltpu.roll(s_re[...], 1, 0))
        prev_i = jnp.where(first, sbi_r[0], pltpu.roll(s_im[...], 1, 0))
        dlbr[0] += jnp.sum(lam_r * prev_r + lam_i * prev_i, axis=0, keepdims=True)
        dlbi[0] += jnp.sum(lam_i * prev_r - lam_r * prev_i, axis=0, keepdims=True)
        lr16 = lam_r.astype(BF16)
        li16 = lam_i.astype(BF16)
        du_ref[...] = _dot(lr16, wbr_r[0], 1, 1) + _dot(li16, wbi_r[0], 1, 1) + d_ref[...] * dyf
        dwbr[0] += _dot(ub, lr16, 0, 0)
        dwbi[0] += _dot(ub, li16, 0, 0)
        dd_ref[0] += jnp.sum(dyf * uf, axis=0, keepdims=True)

    last = nch - 1
    wsp = pl.BlockSpec((1, cq, nq), lambda q, i: (q, 0, 0))
    vec = pl.BlockSpec((1, nq), lambda q, i: (0, q))
    act = pl.BlockSpec((tt, cq), lambda q, i: (last - i, q))
    sb = pl.BlockSpec((1, 1, nq), lambda q, i: (last - i, 0, q))
    vec3 = pl.BlockSpec((1, 1, nq), lambda q, i: (q, 0, 0))
    dsp = pl.BlockSpec((1, 1, cq), lambda q, i: (q, 0, 0))
    wsh = jax.ShapeDtypeStruct((nblk, cq, nq), F32)
    v3sh = jax.ShapeDtypeStruct((nblk, 1, nq), F32)
    big = pltpu.VMEM((tt, nq), F32)
    states = pl.BlockSpec((tt, nq), lambda q, i: (last - i, q))
    return _call(
        body, grid=(nblk, nch),
        in_specs=[act, act, states, states, sb, sb, wsp, wsp, wsp, wsp, vec, vec,
                  pl.BlockSpec((1, cq), lambda q, i: (0, q))],
        out_specs=[act, wsp, wsp, wsp, wsp, vec3, vec3, dsp],
        out_shape=[jax.ShapeDtypeStruct(u.shape, F32), wsh, wsh, wsh, wsh, v3sh, v3sh,
                   jax.ShapeDtypeStruct((nblk, 1, cq), F32)],
        scratch_shapes=[big, big, pltpu.VMEM((1, nq), F32), pltpu.VMEM((1, nq), F32)],
        args=(u, dy, st_re, st_im, sb_re, sb_im, wbr, wbi, wcr, wci, lb_re, lb_im, dskip), name=name, comm=comm)


def _conv_taps(cur, prev, w, b):
    ext = jnp.concatenate([prev, cur], axis=0)
    x1 = pltpu.roll(ext, 1, 0)[SUBLANES:, :]
    x2 = pltpu.roll(ext, 2, 0)[SUBLANES:, :]
    return b + x2 * w[0:1, :] + x1 * w[1:2, :] + cur * w[2:3, :], x1, x2


def _conv_fwd(uu, cw, cb, *, name):
    n_rows, f2 = uu.shape
    f = f2 // 2
    tc = _tile(f, 1408, LANES)
    tl = _tile(n_rows, 256)
    nfb = f // tc

    def body(g_ref, u_ref, wg_ref, wu_ref, bg_ref, bu_ref, o_ref, pg, pu):
        @pl.when(pl.program_id(1) == 0)
        def _():
            pg[...] = jnp.zeros_like(pg)
            pu[...] = jnp.zeros_like(pu)
        gcur = g_ref[...]
        ucur = u_ref[...]
        cg, _, _ = _conv_taps(gcur, pg[...], wg_ref[...], bg_ref[...])
        cu, _, _ = _conv_taps(ucur, pu[...], wu_ref[...], bu_ref[...])
        o_ref[...] = (cg * _sigmoid(cg) * cu).astype(o_ref.dtype)
        pg[...] = gcur[tl - SUBLANES:, :]
        pu[...] = ucur[tl - SUBLANES:, :]

    return pl.pallas_call(
        body, grid=(nfb, n_rows // tl),
        in_specs=[pl.BlockSpec((tl, tc), lambda j, i: (i, j)), pl.BlockSpec((tl, tc), lambda j, i: (i, j + nfb)),
                  pl.BlockSpec((CONV_TAPS, tc), lambda j, i: (0, j)),
                  pl.BlockSpec((CONV_TAPS, tc), lambda j, i: (0, j + nfb)),
                  pl.BlockSpec((1, tc), lambda j, i: (0, j)), pl.BlockSpec((1, tc), lambda j, i: (0, j + nfb))],
        out_specs=pl.BlockSpec((tl, tc), lambda j, i: (i, j)),
        out_shape=jax.ShapeDtypeStruct((n_rows, f), BF16),
        scratch_shapes=[pltpu.VMEM((SUBLANES, tc), F32), pltpu.VMEM((SUBLANES, tc), F32)],
        compiler_params=_cp(2), name=name)(uu, uu, cw, cw, cb, cb)


def _conv_bwd(uu, dact, cw, cb, *, name):
    n_rows, f2 = uu.shape
    f = f2 // 2
    tc = _tile(f, 1408, LANES)
    tl = _tile(n_rows, 256)
    nfb = f // tc
    nrb = n_rows // tl
    halo_per_tile = tl // SUBLANES

    def body(g_ref, gh_ref, u_ref, uh_ref, da_ref, wg_ref, wu_ref, bg_ref, bu_ref,
             duu_ref, dw_ref, db_ref, nxt_g, nxt_u):
        i = pl.program_id(1)
        rb = nrb - 1 - i

        @pl.when(i == 0)
        def _():
            for ref in (nxt_g, nxt_u, dw_ref, db_ref):
                ref[...] = jnp.zeros_like(ref)
        has_prev = jnp.where(rb > 0, 1.0, 0.0)
        gcur, ucur = g_ref[...], u_ref[...]
        wg, wu = wg_ref[...], wu_ref[...]
        cg, g1, g2 = _conv_taps(gcur, gh_ref[...] * has_prev, wg, bg_ref[...])
        cu, u1, u2 = _conv_taps(ucur, uh_ref[...] * has_prev, wu, bu_ref[...])
        sg = _sigmoid(cg)
        silu = cg * sg
        da = da_ref[...]

        def transpose_conv(plane, d, cur, x1, x2, w, nxt):
            ext = jnp.concatenate([d, nxt[...]], axis=0)
            d1 = pltpu.roll(ext, tl + SUBLANES - 1, 0)[:tl, :]
            d2 = pltpu.roll(ext, tl + SUBLANES - 2, 0)[:tl, :]
            duu_ref[plane] = (w[2:3, :] * d + w[1:2, :] * d1 + w[0:1, :] * d2).astype(duu_ref.dtype)
            nxt[...] = d[0:SUBLANES, :]
            dw_ref[plane] += jnp.concatenate([jnp.sum(d * x2, axis=0, keepdims=True),
                                              jnp.sum(d * x1, axis=0, keepdims=True),
                                              jnp.sum(d * cur, axis=0, keepdims=True)], axis=0)
            db_ref[plane] += jnp.sum(d, axis=0, keepdims=True)

        transpose_conv(0, da * cu * (sg * (1.0 + cg * (1.0 - sg))), gcur, g1, g2, wg, nxt_g)
        transpose_conv(1, da * silu, ucur, u1, u2, wu, nxt_u)

    def halo(j, i):
        return jnp.maximum((nrb - 1 - i) * halo_per_tile - 1, 0)

    return pl.pallas_call(
        body, grid=(nfb, nrb),
        in_specs=[pl.BlockSpec((tl, tc), lambda j, i: (nrb - 1 - i, j)),
                  pl.BlockSpec((SUBLANES, tc), lambda j, i: (halo(j, i), j)),
                  pl.BlockSpec((tl, tc), lambda j, i: (nrb - 1 - i, j + nfb)),
                  pl.BlockSpec((SUBLANES, tc), lambda j, i: (halo(j, i), j + nfb)),
                  pl.BlockSpec((tl, tc), lambda j, i: (nrb - 1 - i, j)),
                  pl.BlockSpec((CONV_TAPS, tc), lambda j, i: (0, j)),
                  pl.BlockSpec((CONV_TAPS, tc), lambda j, i: (0, j + nfb)),
                  pl.BlockSpec((1, tc), lambda j, i: (0, j)),
                  pl.BlockSpec((1, tc), lambda j, i: (0, j + nfb))],
        out_specs=[pl.BlockSpec((2, tl, tc), lambda j, i: (0, nrb - 1 - i, j)),
                   pl.BlockSpec((2, CONV_TAPS, tc), lambda j, i: (0, 0, j)),
                   pl.BlockSpec((2, 1, tc), lambda j, i: (0, 0, j))],
        out_shape=[jax.ShapeDtypeStruct((2, n_rows, f), BF16), jax.ShapeDtypeStruct((2, CONV_TAPS, f), F32),
                   jax.ShapeDtypeStruct((2, 1, f), F32)],
        scratch_shapes=[pltpu.VMEM((SUBLANES, tc), F32), pltpu.VMEM((SUBLANES, tc), F32)],
        compiler_params=_cp(2), name=name)(uu, uu, uu, uu, dact, cw, cw, cb, cb)


def _log_sigmoid(x):
    t = jnp.exp(-jnp.abs(x))
    log1p_t = jnp.where(t < 1e-3, t * (1.0 - t * (0.5 - t * (1.0 / 3.0))), jnp.log(1.0 + t))
    return jnp.minimum(x, 0.0) - log1p_t


def _dlog_sigmoid(x):
    t = jnp.exp(-jnp.abs(x))
    return jnp.where(x >= 0, t, 1.0) / (1.0 + t)


def _tri_dot(tri, x):
    return jnp.dot(tri, x, precision=lax.Precision.HIGHEST, preferred_element_type=F32)


def _cum_fwd(fl, bf, *, name):
    n_rows, width = fl.shape
    tc = _tile(n_rows, 256)

    def body(fl_ref, bf_ref, o_ref, carry):
        @pl.when(pl.program_id(0) == 0)
        def _():
            carry[...] = jnp.zeros_like(carry)
        x = _log_sigmoid(fl_ref[...] + bf_ref[...])
        r = lax.broadcasted_iota(jnp.int32, (tc, tc), 0)
        c = lax.broadcasted_iota(jnp.int32, (tc, tc), 1)
        y = _tri_dot(jnp.where(r >= c, 1.0, 0.0), x) + carry[...]
        o_ref[...] = y
        carry[...] = y[tc - 1:tc, :]

    return pl.pallas_call(
        body, grid=(n_rows // tc,),
        in_specs=[pl.BlockSpec((tc, width), lambda i: (i, 0)), pl.BlockSpec((1, width), lambda i: (0, 0))],
        out_specs=pl.BlockSpec((tc, width), lambda i: (i, 0)),
        out_shape=jax.ShapeDtypeStruct(fl.shape, F32),
        scratch_shapes=[pltpu.VMEM((1, width), F32)], compiler_params=_cp(1), name=name)(fl, bf)


def _cum_bwd(dcum, fl, bf, *, name):
    n_rows, width = fl.shape
    tc = _tile(n_rows, 256)
    last = n_rows // tc - 1

    def body(dc_ref, fl_ref, bf_ref, dfl_ref, dbf_ref, carry):
        @pl.when(pl.program_id(0) == 0)
        def _():
            carry[...] = jnp.zeros_like(carry)
            dbf_ref[...] = jnp.zeros_like(dbf_ref)
        r = lax.broadcasted_iota(jnp.int32, (tc, tc), 0)
        c = lax.broadcasted_iota(jnp.int32, (tc, tc), 1)
        dls = _tri_dot(jnp.where(r <= c, 1.0, 0.0), dc_ref[...]) + carry[...]
        carry[...] = dls[0:1, :]
        dfl = dls * _dlog_sigmoid(fl_ref[...] + bf_ref[...])
        dfl_ref[...] = dfl.astype(dfl_ref.dtype)
        dbf_ref[...] += jnp.sum(dfl, axis=0, keepdims=True)

    return pl.pallas_call(
        body, grid=(n_rows // tc,),
        in_specs=[pl.BlockSpec((tc, width), lambda i: (last - i, 0)),
                  pl.BlockSpec((tc, width), lambda i: (last - i, 0)),
                  pl.BlockSpec((1, width), lambda i: (0, 0))],
        out_specs=[pl.BlockSpec((tc, width), lambda i: (last - i, 0)), pl.BlockSpec((1, width), lambda i: (0, 0))],
        out_shape=[jax.ShapeDtypeStruct(fl.shape, BF16), jax.ShapeDtypeStruct((1, width), F32)],
        scratch_shapes=[pltpu.VMEM((1, width), F32)], compiler_params=_cp(1), name=name)(dcum, fl, bf)


def _head_masks():
    lane = lax.broadcasted_iota(jnp.int32, (1, LANES), 1)
    return (lane < HEAD_DIM, lane >= HEAD_DIM)


def _flash_fwd(q, kv, cum_c, cum_r, *, tq, name, comm=None):
    n_rows, d = q.shape
    nhp = d // LANES
    tk = tq
    nq = n_rows // tq
    rt = _tile(tk, FLASH_ROW_TILE)
    reps = (1, tq // LANES)

    def body(qi_ref, kj_ref, q_ref, k_ref, v_ref, cq_ref, ck_ref, ot_ref, lse_ref, m0, m1, l0, l1, acc,
             s0, s1, p0, p1, b0, b1):
        i = qi_ref[pl.program_id(1)]
        j = kj_ref[pl.program_id(1)]
        ms, ls = (m0, m1), (l0, l1)
        head_rows = lax.broadcasted_iota(jnp.int32, (LANES, 1), 0) < HEAD_DIM

        @pl.when(j == 0)
        def _():
            for h in range(2):
                ms[h][...] = jnp.full_like(ms[h], -jnp.inf)
                ls[h][...] = jnp.zeros_like(ls[h])
            acc[...] = jnp.zeros_like(acc)

        def block(diagonal):
            qv, kk, vv = q_ref[...], k_ref[...], v_ref[...]
            a = acc[...]
            for h, msk in enumerate(_head_masks()):
                st_sc, pt_sc, bias_sc = ((s0, p0, b0), (s1, p1, b1))[h]
                st_sc[...] = _dot(kk, jnp.where(msk, qv, jnp.zeros_like(qv)), 1, 1)
                bias_sc[...] = jnp.broadcast_to(cq_ref[0, h:h + 1, 0:1] - ck_ref[0, :, h:h + 1], (tk, LANES))
                m_old, l_old = ms[h][...], ls[h][...]
                col_max = jnp.full((SUBLANES, tq), -jnp.inf, F32)
                for r in range(tk // rt):
                    rows = slice(r * rt, (r + 1) * rt)
                    s = st_sc[rows, :] + jnp.tile(bias_sc[rows, :], reps)
                    if diagonal:
                        key = r * rt + lax.broadcasted_iota(jnp.int32, (rt, tq), 0)
                        qry = lax.broadcasted_iota(jnp.int32, (rt, tq), 1)
                        s = jnp.where(key <= qry, s, -jnp.inf)
                    st_sc[rows, :] = s
                    for g in range(rt // SUBLANES):
                        col_max = jnp.maximum(col_max, s[g * SUBLANES:(g + 1) * SUBLANES, :])
                m_new = jnp.maximum(m_old, jnp.max(col_max, axis=0, keepdims=True))
                col_sum = jnp.zeros((SUBLANES, tq), F32)
                for r in range(tk // rt):
                    rows = slice(r * rt, (r + 1) * rt)
                    p = jnp.exp(st_sc[rows, :] - m_new)
                    for g in range(rt // SUBLANES):
                        col_sum = col_sum + p[g * SUBLANES:(g + 1) * SUBLANES, :]
                    pt_sc[rows, :] = p.astype(BF16)
                alpha = jnp.exp(m_old - m_new)
                ms[h][...] = m_new
                ls[h][...] = alpha * l_old + jnp.sum(col_sum, axis=0, keepdims=True)
                pv_t = _dot(jnp.where(msk, vv, jnp.zeros_like(vv)), pt_sc[...], 0, 0)
                a = a * jnp.where(head_rows == (h == 0), alpha, 1.0) + pv_t
            acc[...] = a

        pl.when(j < i)(functools.partial(block, False))
        pl.when(j == i)(functools.partial(block, True))

        @pl.when(j == i)
        def _():
            ot_ref[...] = (acc[...] * jnp.where(head_rows, 1.0 / l0[...], 1.0 / l1[...])).T
            lse_ref[0] = jnp.concatenate([m0[...] + jnp.log(l0[...]), m1[...] + jnp.log(l1[...])], axis=0)

    pairs = [(i, j) for i in range(nq) for j in range(i + 1)]
    qi = jnp.asarray([i for i, _ in pairs], jnp.int32)
    kj = jnp.asarray([j for _, j in pairs], jnp.int32)
    stat = pltpu.VMEM((1, tq), F32)
    return _call(
        body, grid=(nhp, len(pairs)), prefetch=(qi, kj),
        in_specs=[pl.BlockSpec((tq, LANES), lambda hp, t, qi, kj: (qi[t], hp)),
                  pl.BlockSpec((tk, LANES), lambda hp, t, qi, kj: (kj[t], hp)),
                  pl.BlockSpec((tk, LANES), lambda hp, t, qi, kj: (kj[t], nhp + hp)),
                  pl.BlockSpec((1, 2, tq), lambda hp, t, qi, kj: (hp, 0, qi[t])),
                  pl.BlockSpec((1, tk, 2), lambda hp, t, qi, kj: (hp, kj[t], 0))],
        out_specs=[pl.BlockSpec((tq, LANES), lambda hp, t, qi, kj: (qi[t], hp)),
                   pl.BlockSpec((1, 2, tq), lambda hp, t, qi, kj: (hp, 0, qi[t]))],
        scratch_shapes=[stat, stat, stat, stat, pltpu.VMEM((LANES, tq), F32), pltpu.VMEM((tk, tq), F32),
                        pltpu.VMEM((tk, tq), F32), pltpu.VMEM((tk, tq), BF16), pltpu.VMEM((tk, tq), BF16),
                        pltpu.VMEM((tk, LANES), F32), pltpu.VMEM((tk, LANES), F32)],
        out_shape=[jax.ShapeDtypeStruct((n_rows, d), F32), jax.ShapeDtypeStruct((nhp, 2, n_rows), F32)],
        args=(q, kv, kv, cum_r, cum_c), name=name, comm=comm)


def _head_delta(dd, oo):
    d = oo.shape[1]
    prod = dd.astype(BF16).astype(F32) * oo
    r = lax.broadcasted_iota(jnp.int32, (d, LANES), 0)
    c = lax.broadcasted_iota(jnp.int32, (d, LANES), 1)
    return _tri_dot(prod, jnp.where(jnp.right_shift(r, HEAD_DIM.bit_length() - 1) == c, 1.0, 0.0))


def _flash_bwd(q, kv, lse_r, delta_r, do, cum_c, cum_r, *, tq, name, comm=None):
    n_rows, d = q.shape
    nhp = d // LANES
    tk = tq
    nq = n_rows // tq
    rt = _tile(tk, FLASH_ROW_TILE)
    reps = (1, tq // LANES)

    def body(qi_ref, kj_ref, q_ref, k_ref, v_ref, lse_ref, dl_ref, do_ref, cq_ref, ck_ref,
             dq_ref, dk_ref, dv_ref, dck_ref, dcq_ref, s0, dp0, p0, ds0, b0, b1, ck0, ck1):
        s1, dp1, p1, ds1 = s0, dp0, p0, ds0
        i = qi_ref[pl.program_id(1)]
        j = kj_ref[pl.program_id(1)]

        @pl.when(pl.program_id(1) == 0)
        def _():
            dq_ref[...] = jnp.zeros_like(dq_ref)
            dcq_ref[...] = jnp.zeros_like(dcq_ref)

        @pl.when(i == j)
        def _():
            for ref in (dk_ref, dv_ref, ck0, ck1):
                ref[...] = jnp.zeros_like(ref)

        def block(diagonal):
            qv, kk, vv = q_ref[...], k_ref[...], v_ref[...]
            dob = do_ref[...].astype(BF16)
            dq_acc = jnp.zeros((tq, LANES), F32)
            dk_acc = jnp.zeros((tk, LANES), F32)
            dv_acc = jnp.zeros((tk, LANES), F32)
            query_sums = []
            for h, msk in enumerate(_head_masks()):
                st_sc, dpt_sc, pt_sc, dst_sc, bias_sc, key_part = ((s0, dp0, p0, ds0, b0, ck0),
                                                                  (s1, dp1, p1, ds1, b1, ck1))[h]
                qh = jnp.where(msk, qv, jnp.zeros_like(qv))
                kh = jnp.where(msk, kk, jnp.zeros_like(kk))
                doh = jnp.where(msk, dob, jnp.zeros_like(dob))
                st_sc[...] = _dot(kk, qh, 1, 1)
                dpt_sc[...] = _dot(vv, doh, 1, 1)
                bias_sc[...] = jnp.broadcast_to(cq_ref[0, h:h + 1, 0:1] - ck_ref[0, :, h:h + 1], (tk, LANES))
                lse_row = lse_ref[0, h:h + 1, :]
                delta_row = dl_ref[0, h:h + 1, :]
                col_acc = jnp.zeros((SUBLANES, tq), F32)
                parts = []
                for r in range(tk // rt):
                    rows = slice(r * rt, (r + 1) * rt)
                    s = st_sc[rows, :] + jnp.tile(bias_sc[rows, :], reps)
                    if diagonal:
                        key = r * rt + lax.broadcasted_iota(jnp.int32, (rt, tq), 0)
                        qry = lax.broadcasted_iota(jnp.int32, (rt, tq), 1)
                        s = jnp.where(key <= qry, s, -jnp.inf)
                    p = jnp.exp(s - lse_row)
                    ds = p * (dpt_sc[rows, :] - delta_row)
                    for g in range(rt // SUBLANES):
                        col_acc = col_acc + ds[g * SUBLANES:(g + 1) * SUBLANES, :]
                    part = ds[:, 0:LANES]
                    for g in range(1, tq // LANES):
                        part = part + ds[:, g * LANES:(g + 1) * LANES]
                    parts.append(part)
                    pt_sc[rows, :] = p.astype(BF16)
                    dst_sc[rows, :] = ds.astype(BF16)
                key_part[...] += jnp.concatenate(parts, axis=0)
                query_sums.append(jnp.sum(col_acc, axis=0, keepdims=True))
                dv_acc = dv_acc + _dot(pt_sc[...], doh, 1, 0)
                dsb = dst_sc[...]
                dk_acc = dk_acc + _dot(dsb, qh, 1, 0)
                dq_acc = dq_acc + _dot(dsb, kh, 0, 0)
            off = pl.multiple_of(i * tq, tq)
            dq_ref[pl.ds(off, tq), :] += dq_acc
            dk_ref[...] += dk_acc
            dv_ref[...] += dv_acc
            dcq_ref[0, i] += jnp.concatenate(query_sums, axis=0)

        pl.when(i > j)(functools.partial(block, False))
        pl.when(i == j)(functools.partial(block, True))

        @pl.when(i == nq - 1)
        def _():
            two = lax.broadcasted_iota(jnp.int32, (tk, 2), 1)
            dck_ref[0] = jnp.where(two == 0, -jnp.sum(ck0[...], axis=1, keepdims=True),
                                   -jnp.sum(ck1[...], axis=1, keepdims=True))

    pairs = [(i, j) for j in range(nq) for i in range(j, nq)]
    qi = jnp.asarray([i for i, _ in pairs], jnp.int32)
    kj = jnp.asarray([j for _, j in pairs], jnp.int32)
    score = pltpu.VMEM((tk, tq), F32)
    score16 = pltpu.VMEM((tk, tq), BF16)
    keystat = pltpu.VMEM((tk, LANES), F32)
    return _call(
        body, grid=(nhp, len(pairs)), prefetch=(qi, kj),
        in_specs=[pl.BlockSpec((tq, LANES), lambda hp, t, qi, kj: (qi[t], hp)),
                  pl.BlockSpec((tk, LANES), lambda hp, t, qi, kj: (kj[t], hp)),
                  pl.BlockSpec((tk, LANES), lambda hp, t, qi, kj: (kj[t], nhp + hp)),
                  pl.BlockSpec((1, 2, tq), lambda hp, t, qi, kj: (hp, 0, qi[t])),
                  pl.BlockSpec((1, 2, tq), lambda hp, t, qi, kj: (hp, 0, qi[t])),
                  pl.BlockSpec((tq, LANES), lambda hp, t, qi, kj: (qi[t], hp)),
                  pl.BlockSpec((1, 2, tq), lambda hp, t, qi, kj: (hp, 0, qi[t])),
                  pl.BlockSpec((1, tk, 2), lambda hp, t, qi, kj: (hp, kj[t], 0))],
        out_specs=[pl.BlockSpec((n_rows, LANES), lambda hp, t, qi, kj: (0, hp)),
                   pl.BlockSpec((tk, LANES), lambda hp, t, qi, kj: (kj[t], hp)),
                   pl.BlockSpec((tk, LANES), lambda hp, t, qi, kj: (kj[t], hp)),
                   pl.BlockSpec((1, tk, 2), lambda hp, t, qi, kj: (hp, kj[t], 0)),
                   pl.BlockSpec((1, nq, 2, tq), lambda hp, t, qi, kj: (hp, 0, 0, 0))],
        scratch_shapes=[score, score, score16, score16, keystat, keystat, keystat, keystat],
        out_shape=[jax.ShapeDtypeStruct((n_rows, d), F32), jax.ShapeDtypeStruct((n_rows, d), F32),
                   jax.ShapeDtypeStruct((n_rows, d), F32), jax.ShapeDtypeStruct((nhp, n_rows, 2), F32),
                   jax.ShapeDtypeStruct((nhp, nq, 2, tq), F32)],
        args=(q, kv, kv, lse_r, delta_r, do, cum_r, cum_c), name=name, comm=comm)


def _s5_tables(w, layer):
    g, p = w["lam_re"].shape[1:]
    h = w["ssm_b_re"].shape[3]
    n = g * p
    lr = w["lam_re"][layer].reshape(1, n)
    li = w["lam_im"][layer].reshape(1, n)
    ldt = jnp.broadcast_to(w["log_dt"][layer][:, None], (g, p)).reshape(1, n)
    br = w["ssm_b_re"][layer].transpose(2, 0, 1).reshape(h, n)
    bi = w["ssm_b_im"][layer].transpose(2, 0, 1).reshape(h, n)
    cr = w["ssm_c_re"][layer].transpose(1, 0, 2).reshape(h, n)
    ci = w["ssm_c_im"][layer].transpose(1, 0, 2).reshape(h, n)
    return (lr, li, ldt, br, bi, cr, ci), (g, p, h)


def _local_step(x, tgt, w, net=None, *, attn_tile=1024):
    n_rows, d = x.shape
    n_layers = w["g_mix"].shape[0]
    n_s5 = w["lam_re"].shape[0]
    nh = w["b_f"].shape[0]
    nhp = nh // 2
    assert d == nh * HEAD_DIM
    tq = _tile(n_rows, attn_tile)
    g = {}
    saved = [dict() for _ in range(n_layers)]
    big = {}
    pending = {}

    def wt(name, layer):
        return w[name][layer]

    def carry_gather(group, run):
        if net is None or not net.has_group(group):
            return run(None)
        outs, got = run(net.gather_comm(group))
        net.store_gathered(group, got, w)
        return outs

    def carry_reduce(tag, run):
        keys = list(pending)
        grads = [pending[k][0] for k in keys]
        if net is None or not pending:
            big.update(zip(keys, grads))
            pending.clear()
            return run(None)
        parts = net.reduce_prepare(grads, [pending[k][1] for k in keys], tag)
        outs, landed = run(_chip_exchange_comm(parts))
        big.update(zip(keys, net.reduce_finish(parts, landed, grads, tag)))
        pending.clear()
        return outs

    def by_row_shard(m):
        return m.reshape(N_CHIPS, m.shape[0] // N_CHIPS, m.shape[1])

    def grad_and_copy(x_, dy_, name, *, col_slots=None, scale=None):
        g32, g16 = _mm_tn(x_, dy_, col_slots or 1, scale=scale, wire=True, name=name)
        return (g32, g16) if col_slots else (by_row_shard(g32[0]), by_row_shard(g16[0]))

    h = x
    nxt = _rowwise(lambda a, gg: _rms(a, gg), [x], [_row2(w["g_mix"][0])], [(d, F32)], name="rms_first")[0]
    kvb = fl = cum = cq3 = ck3 = hnkv = None
    bf_pad = jnp.zeros((1, LANES), F32).at[0, :nh].set(w["b_f"])
    for l in range(n_layers):
        sv = saved[l]
        sv["h"] = h
        g_ffn = _row2(w["g_ffn"][l])
        if l < n_s5:
            tabs, (_, p, _) = _s5_tables(w, l)
            prep = _s5_prep(*tabs, p, name=f"s5_prep{l}")
            dskip = w["ssm_d"][l].reshape(1, d)
            y, z, st_re, st_im, sb_re, sb_im = carry_gather(
                f"stage{l}", lambda comm, u=nxt, pr=prep, ds=dskip: _s5_fwd(u, pr, ds, name=f"s5_fwd{l}", comm=comm))
            glu4, glu_l = wt("w_glu", l)
            glu_whole = glu4[:, glu_l].transpose(1, 0, 2).reshape(1, 1, glu4.shape[2], -1)
            zz, h1, hn2 = _mm_cols(z, glu_whole, 0, wc=0, name=f"glu_mm{l}",
                                   epilogue=(lambda zq, hh, gg: ((lambda t: (zq, t, _rms(t, gg)))(hh + _glu(zq))),
                                             [h], [g_ffn], [(2 * d, F32), (d, F32), (d, BF16)], ()))
            sv.update(u=nxt, prep=prep, tabs=tabs, p=p, dskip=dskip, st_re=st_re, st_im=st_im, sb_re=sb_re,
                      sb_im=sb_im, y=y, z=z, zz=zz)
        else:
            j = l - n_s5
            qs = _mm_cols(nxt, *wt("w_q", j), wc=0, out_dtype=BF16, scale=HEAD_DIM ** -0.5, name=f"q_mm{j}")
            o, lse = carry_gather(
                f"stage{l}", lambda comm, q_=qs: _flash_fwd(q_, kvb, cq3, ck3, tq=tq, name=f"flash_fwd{j}", comm=comm))
            h1, hn2 = _mm_cols(o, *wt("w_o", j), wc=0, name=f"o_mm{j}",
                               epilogue=(lambda aa, hh, gg: ((lambda t: (t, _rms(t, gg)))(hh + aa)), [h], [g_ffn],
                                         [(d, F32), (d, BF16)], ()))
            sv.update(hn=nxt, qs=qs, o=o, lse=lse)
        uu = _mm_cols(hn2, *wt("w_in", l), wc=0, name=f"ffn_in{l}")
        cw, cb = w["conv_w"][l], _row2(w["conv_b"][l])
        act = _conv_fwd(uu, cw, cb, name=f"conv_fwd{l}")
        sv.update(h1=h1, hn2=hn2, uu=uu, act=act, cw=cw, cb=cb)

        def ffn_out(fn, rows, consts, outs, accs=()):
            return _mm_cols(act, *wt("w_out", l), wc=0, name=f"ffn_out{l}", epilogue=(fn, rows, consts, outs, accs))

        if l == n_layers - 1:
            def loss_fn(ff, hh, tt, gg):
                yv, vjp = jax.vjp(_rms, hh + ff, gg)
                err = yv - tt
                part = 0.5 * jnp.sum(jnp.mean(err * err, axis=-1, keepdims=True), axis=0, keepdims=True)
                dh, dg = vjp(err * (1.0 / d))
                return dh, dh, jnp.broadcast_to(part, (1, LANES)), dg
            dcur, dcur16, loss_row, dgf = ffn_out(loss_fn, [h1, tgt], [_row2(w["g_final"])],
                                                  [(d, F32), (d, BF16)], [(1, LANES), (1, d)])
            loss = loss_row[0, 0]
            g["g_final"] = dgf[0]
        elif l + 1 < n_s5:
            h, nxt = ffn_out(lambda ff, hh, gg: ((lambda t: (t, _rms(t, gg)))(hh + ff)), [h1],
                             [_row2(w["g_mix"][l + 1])], [(d, F32), (d, F32)])
        elif l + 1 == n_s5:
            h, nxt, hnkv = ffn_out(lambda ff, hh, g1, g2: ((lambda t: (t, _rms(t, g1), _rms(t, g2)))(hh + ff)), [h1],
                                   [_row2(w["g_mix"][l + 1]), _row2(w["g_kv"])],
                                   [(d, F32), (d, BF16), (d, BF16)])
            kvb = _mm_cols(hnkv, *wt("w_kv", 0), wc=0, out_dtype=BF16, name="kv_mm")
            fl = _mm_cols(hnkv, *wt("w_f", 0), wc=0, name="f_mm")
            cum = _cum_fwd(fl, bf_pad, name="cum_fwd")
            cq3 = cum[:, :nh].reshape(n_rows, nhp, 2).transpose(1, 0, 2)
            ck3 = cum[:, :nh].T.reshape(nhp, 2, n_rows)
        else:
            h, nxt = ffn_out(lambda ff, hh, gg: ((lambda t: (t, _rms(t, gg)))(hh + ff)), [h1],
                             [_row2(w["g_mix"][l + 1])], [(d, F32), (d, BF16)])

    per_layer = {k: [None] * n_layers for k in ("g_mix", "g_ffn", "conv_w", "conv_b")}
    per_s5 = {k: [None] * n_s5 for k in ("lam_re", "lam_im", "log_dt", "ssm_b_re", "ssm_b_im", "ssm_c_re",
                                         "ssm_c_im", "ssm_d")}
    dk_parts, dv_parts, dck_parts = [], [], []
    for l in reversed(range(n_layers)):
        sv = saved[l]
        dact = _mm_cols(dcur16, *wt("w_out", l), wc=1, name=f"ffn_out_dx{l}")
        pending["w_ffn_out", l] = grad_and_copy(sv["act"], dcur16, f"ffn_out_dw{l}")
        duu, dcw, dcb = _conv_bwd(sv["uu"], dact, sv["cw"], sv["cb"], name=f"conv_bwd{l}")
        per_layer["conv_w"][l] = jnp.concatenate([dcw[0], dcw[1]], axis=-1)
        per_layer["conv_b"][l] = jnp.concatenate([dcb[0, 0], dcb[1, 0]])
        node = _node_bwd_fn([1])
        d1, d1_16, dg = _mm_acc(duu, *wt("w_in", l), wc=1, name=f"ffn_in_dx{l}",
                                epilogue=(lambda dhn2, dd, hh, gg: node(dd, hh, dhn2, gg), [dcur, sv["h1"]],
                                          [_row2(w["g_ffn"][l])], [(d, F32), (d, BF16)], [(1, d)]))
        pending["w_ffn_in", l] = grad_and_copy(sv["hn2"], duu, f"ffn_in_dw{l}", col_slots=wt("w_in", l)[0].shape[0])
        per_layer["g_ffn"][l] = dg[0]
        if l < n_s5:
            def glu_bwd(zq, dd):
                _, vjp = jax.vjp(_glu, zq)
                return vjp(dd)[0]
            dzz = _rowwise(glu_bwd, [sv["zz"], d1], [], [(2 * d, BF16)], name=f"glu_bwd{l}")[0]
            def gelu_bwd(dd, yy):
                _, vjp = jax.vjp(_gelu, yy)
                return (vjp(dd)[0],)
            dy = _mm_acc(dzz, *wt("w_glu", l), wc=1, name=f"glu_dx{l}",
                         epilogue=(gelu_bwd, [sv["y"]], [], [(d, F32)], ()))[0]
            pending["w_glu", l] = grad_and_copy(sv["z"], dzz, f"glu_dw{l}", col_slots=wt("w_glu", l)[0].shape[0])
            du, dwbr, dwbi, dwcr, dwci, dlbr, dlbi, dd = carry_reduce(
                f"stage{l}", lambda comm, dy_=dy: _s5_bwd(sv["u"], dy_, sv["st_re"], sv["st_im"], sv["sb_re"],
                                                          sv["sb_im"], sv["prep"], sv["dskip"], name=f"s5_bwd{l}",
                                                          comm=comm))
            dlr, dli, dldt, dbr, dbi, dcr, dci = _s5_prep_bwd(*sv["tabs"], sv["p"], dlbr, dlbi, dwbr, dwbi, dwcr,
                                                              dwci, name=f"s5_prep_bwd{l}")
            gg, p = w["lam_re"].shape[1:]
            hh = w["ssm_b_re"].shape[3]
            per_s5["lam_re"][l] = dlr.reshape(gg, p)
            per_s5["lam_im"][l] = dli.reshape(gg, p)
            per_s5["log_dt"][l] = dldt.reshape(gg, p).sum(axis=1)
            per_s5["ssm_b_re"][l] = dbr.reshape(hh, gg, p).transpose(1, 2, 0)
            per_s5["ssm_b_im"][l] = dbi.reshape(hh, gg, p).transpose(1, 2, 0)
            per_s5["ssm_c_re"][l] = dcr.reshape(hh, gg, p).transpose(1, 0, 2)
            per_s5["ssm_c_im"][l] = dci.reshape(hh, gg, p).transpose(1, 0, 2)
            per_s5["ssm_d"][l] = dd.reshape(d)
            branches = [(w["g_mix"][l], [du])]
        else:
            j = l - n_s5
            do, delta = _mm_cols(d1_16, *wt("w_o", j), wc=1, name=f"o_dx{j}",
                                 epilogue=(lambda dd, oo: (dd, _head_delta(dd, oo)), [sv["o"]], [],
                                           [(d, F32), (LANES, F32)], ()))
            pending["w_o", j] = grad_and_copy(sv["o"], d1_16, f"o_dw{j}")
            delta_r = delta[:, :nh].T.reshape(nhp, 2, n_rows)
            dq, dk, dv, dck, dcq = carry_reduce(
                f"stage{l}", lambda comm, do_=do: _flash_bwd(sv["qs"], kvb, sv["lse"], delta_r, do_, cq3, ck3, tq=tq,
                                                          name=f"flash_bwd{j}", comm=comm))
            dk_parts.append(dk)
            dv_parts.append(dv)
            dck_parts.append(dck.transpose(1, 0, 2).reshape(n_rows, nh)
                             + dcq.transpose(0, 2, 1, 3).reshape(nh, n_rows).T)
            scale = HEAD_DIM ** -0.5
            pending["w_q", j] = grad_and_copy(sv["hn"], dq, f"q_dw{j}", scale=scale)
            branches = []
            if j == 0:
                def kv_sum(*parts):
                    half = len(parts) // 2
                    return jnp.concatenate([sum(parts[:half][1:], parts[0]),
                                            sum(parts[half:][1:], parts[half])], axis=1)
                dkv = _rowwise(kv_sum, dk_parts + dv_parts, [], [(2 * d, BF16)], name="dkv_sum")[0]
                dck_tot = dck_parts[0]
                for extra in dck_parts[1:]:
                    dck_tot = dck_tot + extra
                dcum = jnp.zeros((n_rows, LANES), F32).at[:, :nh].set(dck_tot)
                dfl, dbf = _cum_bwd(dcum, fl, bf_pad, name="cum_bwd")
                g["b_f"] = dbf[0, :nh]
                dhkv_a = _mm_cols(dkv, *wt("w_kv", 0), wc=1, name="kv_dx")
                dhkv_b = _mm_cols(dfl, *wt("w_f", 0), wc=1, name="f_dx")
                d_kvf = jnp.concatenate([_mm_tn(hnkv, dkv, 1, name="kv_dw")[0],
                                         _mm_tn(hnkv, dfl, 1, name="f_dw")[0][:, :nh]], axis=1)
                d_kvf = d_kvf.reshape(d, N_CHIPS, -1).transpose(1, 0, 2)
                pending["w_kvf", 0] = (d_kvf, d_kvf.astype(BF16))
                branches.append((w["g_kv"], [dhkv_a, dhkv_b]))
            node = _node_bwd_fn([1] + [len(dys) for _, dys in branches])
            gains = [_row2(w["g_mix"][l])] + [_row2(gn) for gn, _ in branches]
            res = _mm_cols(dq, *wt("w_q", j), wc=1, scale=scale, name=f"q_dx{j}",
                           epilogue=(lambda dhn, dd, hh, *rest: node(dd, hh, dhn, *rest),
                                     [d1, sv["h"], *[dy for _, dys in branches for dy in dys]], gains,
                                     [(d, F32), (d, BF16)], [(1, d)] * len(gains)))
            dcur, dcur16, dgs = res[0], res[1], [r[0] for r in res[2:]]
        if l < n_s5:
            dcur, dcur16, dgs = _node_bwd(d1, sv["h"], branches, name=f"mix_norm_bwd{l}")
        per_layer["g_mix"][l] = dgs[0]
        if len(dgs) > 1:
            g["g_kv"] = dgs[1]

    if pending:
        grads = [g32 for g32, _ in pending.values()]
        big.update(zip(list(pending), grads if net is None else net.reduce_blocking(grads, "tail")))
    for k, v in (*per_layer.items(), *per_s5.items()):
        g[k] = jnp.stack(v)
    g["big"] = big
    return loss, dcur, g


def _position():
    x, y, c = lax.axis_index("x"), lax.axis_index("y"), lax.axis_index("c")
    chips = [(1 - x, y), (x, 1 - y), (1 - x, 1 - y)]
    return x, y, c, chips


def _all_gather_comm(shards):
    n = len(shards)

    def descriptors(ins, outs, sems):
        send_sems, recv_sems = sems
        x, y, c, chips = _position()
        my_slot = 2 * x + y
        sibling = (x, y, 1 - c)

        def rows(t, half):
            hr = ins[t].shape[0] // 2
            return pl.ds(half * hr, hr)

        def remote(k, t, src, dst, to):
            return pltpu.make_async_remote_copy(src_ref=src, dst_ref=dst, send_sem=send_sems.at[k, t],
                                                recv_sem=recv_sems.at[k, t], device_id=to, device_id_type=MESH)

        own = [remote(6, t, ins[t], outs[t].at[my_slot], sibling) for t in range(n)]
        ici = [remote(j, t, ins[t].at[rows(t, c)], outs[t].at[my_slot, rows(t, c)], (*chip, c))
               for j, chip in enumerate(chips) for t in range(n)]
        slots = [2 * chip[0] + chip[1] for chip in chips]
        fwd = [[remote(3 + j, t, outs[t].at[slots[j], rows(t, c)], outs[t].at[slots[j], rows(t, c)], sibling)
                for t in range(n)] for j in range(len(chips))]
        landed = [[remote(j, t, outs[t].at[slots[j], rows(t, c)], outs[t].at[slots[j], rows(t, c)], (*chips[j], c))
                   for t in range(n)] for j in range(len(chips))]
        from_sibling = [remote(3 + j, t, outs[t].at[slots[j], rows(t, 1 - c)], outs[t].at[slots[j], rows(t, 1 - c)],
                               sibling) for j in range(len(chips)) for t in range(n)]
        return own, ici, fwd, landed, from_sibling

    def start(ins, outs, sems):
        own, ici, _, _, _ = descriptors(ins, outs, sems)
        for cp in own + ici:
            cp.start()

    def finish(ins, outs, sems):
        own, ici, fwd, landed, from_sibling = descriptors(ins, outs, sems)
        for j in range(len(fwd)):
            for cp in landed[j]:
                cp.wait_recv()
            for cp in fwd[j]:
                cp.start()
        for cp in from_sibling + own:
            cp.wait_recv()
        for cp in own + ici + [cp for group in fwd for cp in group]:
            cp.wait_send()

    return _Comm(list(shards), [jax.ShapeDtypeStruct((N_CHIPS,) + a.shape, a.dtype) for a in shards],
                 [pltpu.SemaphoreType.DMA((7, n)), pltpu.SemaphoreType.DMA((7, n))], start, finish)


def _all_gather(shards, *, name):
    return _run_comm(_all_gather_comm(shards), name=name)


def _pair_exchange(grads, *, name):
    n = len(grads)

    def body(*refs):
        ins, outs = refs[:n], refs[n:2 * n]
        send_sems, recv_sems = refs[2 * n:]
        x, y, c, _ = _position()
        copies = [pltpu.make_async_remote_copy(src_ref=ins[t].at[:, 1 - c], dst_ref=outs[t],
                                               send_sem=send_sems.at[t], recv_sem=recv_sems.at[t],
                                               device_id=(x, y, 1 - c), device_id_type=MESH) for t in range(n)]
        for cp in copies:
            cp.start()
        for cp in copies:
            cp.wait()

    return pl.pallas_call(
        body, in_specs=_any_specs(n), out_specs=_any_specs(n),
        out_shape=[jax.ShapeDtypeStruct((a.shape[0],) + a.shape[2:], a.dtype) for a in grads],
        scratch_shapes=[pltpu.SemaphoreType.DMA((n,)), pltpu.SemaphoreType.DMA((n,))], name=name)(*grads)


def _chip_exchange_comm(parts):
    n = len(parts)

    def copies(ins, outs, sems):
        send_sems, recv_sems = sems
        _, _, c, chips = _position()
        return [pltpu.make_async_remote_copy(src_ref=ins[t].at[2 * chip[0] + chip[1]], dst_ref=outs[t].at[j],
                                             send_sem=send_sems.at[j, t], recv_sem=recv_sems.at[j, t],
                                             device_id=(*chip, c), device_id_type=MESH)
                for j, chip in enumerate(chips) for t in range(n)]

    def start(ins, outs, sems):
        for cp in copies(ins, outs, sems):
            cp.start()

    def finish(ins, outs, sems):
        for cp in copies(ins, outs, sems):
            cp.wait()

    return _Comm(list(parts), [jax.ShapeDtypeStruct((N_CHIPS - 1,) + a.shape[1:], a.dtype) for a in parts],
                 [pltpu.SemaphoreType.DMA((N_CHIPS - 1, n)), pltpu.SemaphoreType.DMA((N_CHIPS - 1, n))],
                 start, finish)


def _pair_share(both, *, name):
    n = len(both)

    def body(*refs):
        ins, outs = refs[:n], refs[n:2 * n]
        send_sems, recv_sems = refs[2 * n:]
        x, y, c, _ = _position()
        for t in range(n):
            pltpu.make_async_remote_copy(src_ref=ins[t].at[c], dst_ref=outs[t].at[c], send_sem=send_sems.at[t],
                                         recv_sem=recv_sems.at[t], device_id=(x, y, 1 - c),
                                         device_id_type=MESH).start()
        for t in range(n):
            pltpu.make_async_remote_copy(src_ref=ins[t].at[c], dst_ref=outs[t].at[1 - c], send_sem=send_sems.at[t],
                                         recv_sem=recv_sems.at[t], device_id=(x, y, 1 - c),
                                         device_id_type=MESH).wait()

    return pl.pallas_call(
        body, in_specs=_any_specs(n), out_specs=_any_specs(n),
        out_shape=[jax.ShapeDtypeStruct(a.shape, a.dtype) for a in both],
        input_output_aliases={t: t for t in range(n)},
        scratch_shapes=[pltpu.SemaphoreType.DMA((n,)), pltpu.SemaphoreType.DMA((n,))], name=name)(*both)


def _sum_pair(grad, landed, c, wire_dtype, *, name):
    slots, _, m, n = grad.shape
    tm = _tile(m, 256, 2 * SUBLANES)

    def body(c_ref, g_ref, l_ref, o_ref):
        o_ref[...] = (g_ref[0] + l_ref[...]).astype(wire_dtype)

    return pl.pallas_call(
        body,
        grid_spec=pltpu.PrefetchScalarGridSpec(
            num_scalar_prefetch=1, grid=(slots, m // tm),
            in_specs=[pl.BlockSpec((1, 1, tm, n), lambda s, i, c_ref: (s, c_ref[0], i, 0)),
                      pl.BlockSpec((1, tm, n), lambda s, i, c_ref: (s, i, 0))],
            out_specs=pl.BlockSpec((1, tm, n), lambda s, i, c_ref: (s, i, 0))),
        out_shape=jax.ShapeDtypeStruct((slots, m, n), wire_dtype), compiler_params=_cp(2), name=name)(
            c, grad, landed)


def _sum_chips(part, landed, slot_c, *, name):
    _, m, n = part.shape
    tm = _tile(m, 256, 2 * SUBLANES)

    def body(s_ref, p_ref, l_ref, o_ref):
        acc = p_ref[0].astype(F32)
        for j in range(N_CHIPS - 1):
            acc = acc + l_ref[j].astype(F32)
        o_ref[0] = acc

    return pl.pallas_call(
        body,
        grid_spec=pltpu.PrefetchScalarGridSpec(
            num_scalar_prefetch=1, grid=(m // tm,),
            in_specs=[pl.BlockSpec((1, tm, n), lambda i, s_ref: (s_ref[0], i, 0)),
                      pl.BlockSpec((N_CHIPS - 1, tm, n), lambda i, s_ref: (0, i, 0))],
            out_specs=pl.BlockSpec((1, tm, n), lambda i, s_ref: (s_ref[1], i, 0))),
        out_shape=jax.ShapeDtypeStruct((N_CORES, m, n), F32), compiler_params=_cp(1), name=name)(
            slot_c, part, landed)


def _reduce_prepare(grads, wire_dtypes, tag, copies=None):
    c = lax.axis_index("c").reshape(1).astype(jnp.int32)

    def halves(a):
        lead, last = a.shape[1], a.shape[-1]
        mid = 1
        for s in a.shape[2:-1]:
            mid *= s
        return a.reshape(N_CHIPS, N_CORES, (lead // N_CORES) * mid, last)

    views = [halves(a) for a in grads]
    landed = _pair_exchange(views if copies is None else [halves(a) for a in copies],
                            name=f"rs_pair_exchange_{tag}")
    return [_sum_pair(v, l, c, wire_dtypes[t], name=f"rs_pair_sum_{tag}_{t}")
            for t, (v, l) in enumerate(zip(views, landed))]


def _reduce_finish(parts, landed, grads, tag):
    slot_c = jnp.stack([2 * lax.axis_index("x") + lax.axis_index("y"), lax.axis_index("c")]).astype(jnp.int32)
    both = [_sum_chips(p, l, slot_c, name=f"rs_chip_sum_{tag}_{t}") for t, (p, l) in enumerate(zip(parts, landed))]
    full = _pair_share(both, name=f"rs_pair_share_{tag}")
    return [f.reshape(a.shape[1:]) for f, a in zip(full, grads)]


def _reduce_scatter(grads, wire_dtypes, tag):
    parts = _reduce_prepare(grads, wire_dtypes, tag)
    landed = _run_comm(_chip_exchange_comm(parts), name=f"rs_chip_exchange_{tag}")
    return _reduce_finish(parts, landed, grads, tag)


class _Net:
    def __init__(self, groups, d, nh):
        self.groups, self.d, self.nh = groups, d, nh

    def has_group(self, group):
        return bool(self.groups.get(group))

    def gather_comm(self, group):
        return _all_gather_comm([shard for _, _, shard in self.groups[group]])

    def store_gathered(self, group, got, w):
        d, nh = self.d, self.nh
        for (name, layer, _), full in zip(self.groups[group], got):
            if name == "w_kvf":
                mat = full.transpose(1, 0, 2).reshape(d, -1)
                w["w_kv"][0] = (mat[:, :2 * d][None, None], 0)
                w["w_f"][0] = (jnp.zeros((d, LANES), BF16).at[:, :nh].set(mat[:, 2 * d:])[None, None], 0)
            elif name in ("w_in", "w_glu"):
                w[name][layer] = (full[:, None], 0)
            else:
                w[name][layer] = (full.reshape(1, 1, -1, full.shape[-1]), 0)

    def reduce_prepare(self, grads, copies, tag):
        return _reduce_prepare(grads, [BF16] * len(grads), tag, copies)

    def reduce_finish(self, parts, landed, grads, tag):
        return _reduce_finish(parts, landed, grads, tag)

    def reduce_blocking(self, grads, tag):
        return _reduce_scatter(grads, [BF16] * len(grads), tag)


def _adamw(w, g, m, v, *, name):
    def fn(ww, gg, mm, vv):
        mm = ADAM_B1 * mm + (1.0 - ADAM_B1) * gg
        vv = ADAM_B2 * vv + (1.0 - ADAM_B2) * (gg * gg)
        m_hat = mm / (1.0 - ADAM_B1 ** ADAM_STEP)
        v_hat = vv / (1.0 - ADAM_B2 ** ADAM_STEP)
        delta = -ADAM_LR * (m_hat / (jnp.sqrt(v_hat) + ADAM_EPS) + ADAM_WD * ww)
        return delta, mm, vv

    shape = w.shape
    two_d = [a.reshape(-1, shape[-1]) for a in (w, g, m, v)]
    outs = _rowwise(fn, two_d, [], [(shape[-1], F32)] * 3, name=name)
    return [o.reshape(shape) for o in outs]


def _to_bf16(a, *, name):
    two_d = a.reshape(-1, a.shape[-1])
    return _rowwise(lambda t: t, [two_d], [], [(a.shape[-1], BF16)], name=name)[0].reshape(a.shape)


def _pack(arrays, rows_multiple):
    flat = jnp.concatenate([a.reshape(-1) for a in arrays])
    rows = -(-flat.shape[0] // LANES)
    rows = -(-rows // rows_multiple) * rows_multiple
    return jnp.pad(flat, (0, rows * LANES - flat.shape[0])).reshape(rows, LANES)


def _unpack(packed, like):
    flat = packed.reshape(-1)
    out, pos = [], 0
    for a in like:
        out.append(flat[pos:pos + a.size].reshape(a.shape))
        pos += a.size
    return out


_PARAMS = ("g_mix", "g_ffn", "lam_re", "lam_im", "log_dt", "ssm_b_re", "ssm_b_im", "ssm_c_re", "ssm_c_im", "ssm_d",
           "w_glu", "g_kv", "w_kvf", "b_f", "w_q", "w_o", "w_ffn_in", "ffn_conv_w", "ffn_conv_b", "w_ffn_out",
           "g_final")
_BIG = ("w_glu", "w_kvf", "w_q", "w_o", "w_ffn_in", "w_ffn_out")
_SMALL_SHARDED = ("ssm_d", "ffn_conv_w")


def kernel(x, g_mix, g_ffn, lam_re, lam_im, log_dt, ssm_b_re, ssm_b_im, ssm_c_re, ssm_c_im, ssm_d, w_glu, g_kv, w_kvf, b_f, w_q, w_o, w_ffn_in, ffn_conv_w, ffn_conv_b, w_ffn_out, g_final, loss_target, m_g_mix, m_g_ffn, m_lam_re, m_lam_im, m_log_dt, m_ssm_b_re, m_ssm_b_im, m_ssm_c_re, m_ssm_c_im, m_ssm_d, m_w_glu, m_g_kv, m_w_kvf, m_b_f, m_w_q, m_w_o, m_w_ffn_in, m_ffn_conv_w, m_ffn_conv_b, m_w_ffn_out, m_g_final, v_g_mix, v_g_ffn, v_lam_re, v_lam_im, v_log_dt, v_ssm_b_re, v_ssm_b_im, v_ssm_c_re, v_ssm_c_im, v_ssm_d, v_w_glu, v_g_kv, v_w_kvf, v_b_f, v_w_q, v_w_o, v_w_ffn_in, v_ffn_conv_w, v_ffn_conv_b, v_w_ffn_out, v_g_final):
    p = dict(g_mix=g_mix, g_ffn=g_ffn, lam_re=lam_re, lam_im=lam_im, log_dt=log_dt, ssm_b_re=ssm_b_re,
             ssm_b_im=ssm_b_im, ssm_c_re=ssm_c_re, ssm_c_im=ssm_c_im, ssm_d=ssm_d, w_glu=w_glu, g_kv=g_kv,
             w_kvf=w_kvf, b_f=b_f, w_q=w_q, w_o=w_o, w_ffn_in=w_ffn_in, ffn_conv_w=ffn_conv_w,
             ffn_conv_b=ffn_conv_b, w_ffn_out=w_ffn_out, g_final=g_final)
    mom1 = dict(zip(_PARAMS, (m_g_mix, m_g_ffn, m_lam_re, m_lam_im, m_log_dt, m_ssm_b_re, m_ssm_b_im, m_ssm_c_re,
                              m_ssm_c_im, m_ssm_d, m_w_glu, m_g_kv, m_w_kvf, m_b_f, m_w_q, m_w_o, m_w_ffn_in,
                              m_ffn_conv_w, m_ffn_conv_b, m_w_ffn_out, m_g_final)))
    mom2 = dict(zip(_PARAMS, (v_g_mix, v_g_ffn, v_lam_re, v_lam_im, v_log_dt, v_ssm_b_re, v_ssm_b_im, v_ssm_c_re,
                              v_ssm_c_im, v_ssm_d, v_w_glu, v_g_kv, v_w_kvf, v_b_f, v_w_q, v_w_o, v_w_ffn_in,
                              v_ffn_conv_w, v_ffn_conv_b, v_w_ffn_out, v_g_final)))
    d = x.shape[-1]
    nh = b_f.shape[0]
    slot = 2 * lax.axis_index("x") + lax.axis_index("y")

    wb = {k: _to_bf16(p[k], name=f"to_bf16_{k}") for k in _BIG}
    gd, gcw = _all_gather([ssm_d, ffn_conv_w], name="first_all_gather")
    n_lay, n_s5 = w_ffn_in.shape[0], lam_re.shape[0]
    n_fox = n_lay - n_s5
    groups = {f"stage{l}": [("w_in", l, wb["w_ffn_in"][l]), ("w_out", l, wb["w_ffn_out"][l])] for l in range(n_lay)}
    for l in range(n_s5):
        groups[f"stage{l}"].append(("w_glu", l, wb["w_glu"][l]))
    groups[f"stage{n_s5 - 1}"].append(("w_kvf", 0, wb["w_kvf"]))
    for j in range(n_fox):
        groups[f"stage{n_s5 + j - 1}"].append(("w_q", j, wb["w_q"][j]))
        groups[f"stage{n_s5 + j}"].append(("w_o", j, wb["w_o"][j]))
    w = dict(p)
    w.update(w_glu=[None] * n_s5, w_in=[None] * n_lay, w_out=[None] * n_lay, w_q=[None] * n_fox,
             w_o=[None] * n_fox, w_kv=[None], w_f=[None],
             conv_w=gcw.transpose(1, 2, 0, 3).reshape(n_lay, CONV_TAPS, -1), conv_b=ffn_conv_b,
             ssm_d=gd.transpose(1, 0, 2).reshape(gd.shape[1], d))

    loss_part, grad_x, g = _local_step(x[0], loss_target[0], w, _Net(groups, d, nh))
    loss = lax.psum(loss_part, ("x", "y", "c"))

    small_names = [k for k in _PARAMS if k not in _BIG]
    small_full = dict(g_mix=g["g_mix"], g_ffn=g["g_ffn"], lam_re=g["lam_re"], lam_im=g["lam_im"], log_dt=g["log_dt"],
                      ssm_b_re=g["ssm_b_re"], ssm_b_im=g["ssm_b_im"], ssm_c_re=g["ssm_c_re"], ssm_c_im=g["ssm_c_im"],
                      ssm_d=g["ssm_d"], g_kv=g["g_kv"], b_f=g["b_f"], ffn_conv_w=g["conv_w"],
                      ffn_conv_b=g["conv_b"], g_final=g["g_final"])
    small_list = [small_full[k] for k in small_names]
    pack = _pack(small_list, N_CHIPS * N_CORES * 2 * SUBLANES)
    pack4 = pack.reshape(N_CHIPS, pack.shape[0] // N_CHIPS, LANES)
    pack_shard = _reduce_scatter([pack4], [F32], "small")[0]
    red_big = {k: g["big"][k, 0] if p[k].ndim == 2 else jnp.stack([g["big"][k, l] for l in range(p[k].shape[0])])
               for k in _BIG}
    pack_all = _all_gather([pack_shard], name="small_grads_all_gather")[0]
    red_small = dict(zip(small_names, _unpack(pack_all, small_list)))
    for k in _SMALL_SHARDED:
        width = p[k].shape[-1]
        red_small[k] = lax.dynamic_slice_in_dim(red_small[k], slot * width, width, axis=red_small[k].ndim - 1)

    grads, deltas, new_m, new_v = {}, {}, {}, {}
    for k in _BIG:
        grads[k] = red_big[k]
        deltas[k], new_m[k], new_v[k] = _adamw(p[k], grads[k], mom1[k], mom2[k], name=f"adamw_{k}")
    packs = [_pack([src[k] for k in small_names], SUBLANES) for src in (p, red_small, mom1, mom2)]
    like = [p[k] for k in small_names]
    outs = [_unpack(o, like) for o in _adamw(*packs, name="adamw_small")]
    for i, k in enumerate(small_names):
        grads[k] = red_small[k]
        deltas[k], new_m[k], new_v[k] = outs[0][i], outs[1][i], outs[2][i]
    return (loss, grad_x[None], *[grads[k] for k in _PARAMS], *[deltas[k] for k in _PARAMS],
            *[new_m[k] for k in _PARAMS], *[new_v[k] for k in _PARAMS])
```
